```python
import jax, jax.numpy as jnp
from jax import lax
import numpy as np

D_MODEL = 1024
BATCH = 16
SEQ = 2048
DEPTH = 2

CHUNK = 64
RET_HEADS = 4
RET_QK_DIM = 128
RET_V_DIM = 256
RET_QK = RET_HEADS * RET_QK_DIM
RET_V = RET_HEADS * RET_V_DIM
SC_WIDTH = D_MODEL
SC_KERNEL = 3
CF_WIDTH = D_MODEL
CF_KERNEL = 31
N_BRANCH = 3
D_FF = 4 * D_MODEL
ROPE_BASE = 10000.0
NORM_EPS = 1e-6
LN_EPS = 1e-5
N_NORMS = 6
IN_SPLITS = (RET_QK, RET_QK, RET_V, RET_V, SC_WIDTH, SC_WIDTH, SC_WIDTH, 2 * CF_WIDTH, N_BRANCH * D_MODEL)
D_IN = 2 * RET_QK + 2 * RET_V + 3 * SC_WIDTH + 2 * CF_WIDTH + N_BRANCH * D_MODEL

kernel_name = "hybrid_retention_conv_macaron_encoder"


def rms_norm(x, g):
    xf = x.astype(jnp.float32)
    y = xf * lax.rsqrt(jnp.mean(xf * xf, axis=-1, keepdims=True) + NORM_EPS)
    return (y * g.astype(jnp.float32)).astype(x.dtype)


def layer_norm(x, g, b):
    xf = x.astype(jnp.float32)
    mu = jnp.mean(xf, axis=-1, keepdims=True)
    var = jnp.mean(jnp.square(xf - mu), axis=-1, keepdims=True)
    y = (xf - mu) * lax.rsqrt(var + LN_EPS)
    return (y * g.astype(jnp.float32) + b.astype(jnp.float32)).astype(x.dtype)


def swiglu_ffn(h, w_gu, w_down):
    gate, up = jnp.split(h @ w_gu, 2, axis=-1)
    return (jax.nn.silu(gate) * up) @ w_down


def causal_depthwise_conv(x, w):
    k = w.shape[0]
    return lax.conv_general_dilated(
        x, w[:, None, :].astype(x.dtype), window_strides=(1,), padding=[(k - 1, 0)],
        dimension_numbers=("NWC", "WIO", "NWC"), feature_group_count=x.shape[-1])


def rotary(x, positions):
    half = x.shape[-1] // 2
    inv_freq = ROPE_BASE ** (-jnp.arange(half, dtype=jnp.float32) / half)
    ang = positions.astype(jnp.float32)[..., None] * inv_freq
    cos = jnp.cos(ang)[:, :, None, :]
    sin = jnp.sin(ang)[:, :, None, :]
    x1, x2 = x[..., :half], x[..., half:]
    return jnp.concatenate([x1 * cos - x2 * sin, x2 * cos + x1 * sin], axis=-1)


def chunkwise_retention(q, k, v, positions):
    b, s, h, dk = q.shape
    dv = v.shape[-1]
    n = s // CHUNK
    q = rotary(q, positions)
    k = rotary(k, positions) * (dk ** -0.5)
    log_g = jnp.log(1.0 - 2.0 ** (-5.0 - jnp.arange(h, dtype=jnp.float32)))
    idx = jnp.arange(CHUNK, dtype=jnp.float32)
    decay_intra = jnp.exp(log_g[:, None, None] * jnp.abs(idx[:, None] - idx[None, :]))
    xi = jnp.exp(log_g[None, :] * (idx[:, None] + 1.0))
    zeta = jnp.exp(log_g[None, :] * (CHUNK - 1.0 - idx[:, None]))
    g_chunk = jnp.exp(log_g * CHUNK)

    qc = q.reshape(b, n, CHUNK, h, dk)
    kc = k.reshape(b, n, CHUNK, h, dk)
    vc = v.reshape(b, n, CHUNK, h, dv)
    scores = jnp.einsum("bnihd,bnjhd->bnhij", qc, kc) * decay_intra
    intra = jnp.einsum("bnhij,bnjhe->bnihe", scores, vc)

    kz = kc * zeta[None, None, :, :, None]

    def step(state, inp):
        q_n, k_n, v_n = inp
        inter = jnp.einsum("bihd,bhde->bihe", q_n, state) * xi[None, :, :, None]
        state = state * g_chunk[None, :, None, None] + jnp.einsum("bjhd,bjhe->bhde", k_n, v_n)
        return state, inter

    state0 = jnp.zeros((b, h, dk, dv), jnp.float32)
    _, inter = lax.scan(step, state0, (qc.swapaxes(0, 1), kz.swapaxes(0, 1), vc.swapaxes(0, 1)))
    out = intra + inter.swapaxes(0, 1)
    return out.reshape(b, s, h, dv)


def head_norm(o):
    mu = jnp.mean(o, axis=-1, keepdims=True)
    var = jnp.mean(jnp.square(o - mu), axis=-1, keepdims=True)
    return (o - mu) * lax.rsqrt(var + LN_EPS)


def hybrid_mixer(h, positions, w_in, w_ret_o, sc_conv_w, w_sc_o,
                 cf_dw_w, cf_dw_b, cf_ln_g, cf_ln_b, w_cf_o, w_o):
    b, s, _ = h.shape
    split_at = [int(v) for v in np.cumsum(IN_SPLITS)[:-1]]
    q, k, v, g_ret, sc_b, sc_c, sc_x, cf_in, gate_logits = jnp.split(h @ w_in, split_at, axis=-1)

    o = chunkwise_retention(
        q.reshape(b, s, RET_HEADS, RET_QK_DIM).astype(jnp.float32),
        k.reshape(b, s, RET_HEADS, RET_QK_DIM).astype(jnp.float32),
        v.reshape(b, s, RET_HEADS, RET_V_DIM).astype(jnp.float32),
        positions)
    o = head_norm(o).reshape(b, s, RET_V).astype(h.dtype)
    y_a = (jax.nn.silu(g_ret) * o) @ w_ret_o

    y_b = (sc_b * causal_depthwise_conv(sc_c * sc_x, sc_conv_w)) @ w_sc_o

    glu_a, glu_b = jnp.split(cf_in, 2, axis=-1)
    u = glu_a * jax.nn.sigmoid(glu_b)
    u = causal_depthwise_conv(u, cf_dw_w) + cf_dw_b
    u = jax.nn.silu(layer_norm(u, cf_ln_g, cf_ln_b))
    y_c = u @ w_cf_o

    gates = jax.nn.sigmoid(gate_logits).reshape(b, s, N_BRANCH, D_MODEL)
    merged = gates[:, :, 0] * y_a + gates[:, :, 1] * y_b + gates[:, :, 2] * y_c
    return merged @ w_o


def _fwd_setup_inputs(seed: int = 0) -> dict:
    key = jax.random.key(seed)
    ks = jax.random.split(key, 20)
    f32 = jnp.float32

    def dense(k, shape, fan_in):
        return jax.random.normal(k, shape, f32) * (fan_in ** -0.5)

    x = jax.random.normal(ks[0], (BATCH, SEQ, D_MODEL), f32)
    offsets = jax.random.randint(ks[1], (BATCH, 1), 0, 64, dtype=jnp.int32) * CHUNK
    positions = offsets + jnp.arange(SEQ, dtype=jnp.int32)[None, :]
    return {
        "x": x,
        "positions": positions,
        "norm_g": 1.0 + 0.02 * jax.random.normal(ks[2], (DEPTH, N_NORMS, D_MODEL), f32),
        "ffn1_w_gu": dense(ks[3], (DEPTH, D_MODEL, 2 * D_FF), D_MODEL),
        "ffn1_w_down": dense(ks[4], (DEPTH, D_FF, D_MODEL), D_FF),
        "w_in": dense(ks[5], (DEPTH, D_MODEL, D_IN), D_MODEL),
        "w_ret_o": dense(ks[6], (DEPTH, RET_V, D_MODEL), RET_V),
        "sc_conv_w": dense(ks[7], (DEPTH, SC_KERNEL, SC_WIDTH), SC_KERNEL),
        "w_sc_o": dense(ks[8], (DEPTH, SC_WIDTH, D_MODEL), SC_WIDTH),
        "cf_dw_w": dense(ks[9], (DEPTH, CF_KERNEL, CF_WIDTH), CF_KERNEL),
        "cf_dw_b": 0.02 * jax.random.normal(ks[10], (DEPTH, CF_WIDTH), f32),
        "cf_ln_g": 1.0 + 0.02 * jax.random.normal(ks[11], (DEPTH, CF_WIDTH), f32),
        "cf_ln_b": 0.02 * jax.random.normal(ks[12], (DEPTH, CF_WIDTH), f32),
        "w_cf_o": dense(ks[13], (DEPTH, CF_WIDTH, D_MODEL), CF_WIDTH),
        "w_o": dense(ks[14], (DEPTH, D_MODEL, D_MODEL), D_MODEL),
        "ffn2_w_gu": dense(ks[15], (DEPTH, D_MODEL, 2 * D_FF), D_MODEL),
        "ffn2_w_down": dense(ks[16], (DEPTH, D_FF, D_MODEL), D_FF),
    }


def _fwd_reference(x, positions, norm_g, ffn1_w_gu, ffn1_w_down, w_in, w_ret_o, sc_conv_w, w_sc_o,
              cf_dw_w, cf_dw_b, cf_ln_g, cf_ln_b, w_cf_o, w_o, ffn2_w_gu, ffn2_w_down):
    for l in range(DEPTH):
        g = norm_g[l]
        x = x + 0.5 * rms_norm(swiglu_ffn(rms_norm(x, g[0]), ffn1_w_gu[l], ffn1_w_down[l]), g[1])
        m = hybrid_mixer(rms_norm(x, g[2]), positions, w_in[l], w_ret_o[l], sc_conv_w[l], w_sc_o[l],
                         cf_dw_w[l], cf_dw_b[l], cf_ln_g[l], cf_ln_b[l], w_cf_o[l], w_o[l])
        x = x + rms_norm(m, g[3])
        x = x + 0.5 * rms_norm(swiglu_ffn(rms_norm(x, g[4]), ffn2_w_gu[l], ffn2_w_down[l]), g[5])
    return x


import jax as _jax
import jax.numpy as _jnp

TWIN_FORMAT = 'train_step'
FWD_PARAMS = ['x', 'positions', 'norm_g', 'ffn1_w_gu', 'ffn1_w_down', 'w_in', 'w_ret_o', 'sc_conv_w', 'w_sc_o', 'cf_dw_w', 'cf_dw_b', 'cf_ln_g', 'cf_ln_b', 'w_cf_o', 'w_o', 'ffn2_w_gu', 'ffn2_w_down']
TWIN_WEIGHTS = ['norm_g', 'ffn1_w_gu', 'ffn1_w_down', 'w_in', 'w_ret_o', 'sc_conv_w', 'w_sc_o', 'cf_dw_w', 'cf_dw_b', 'cf_ln_g', 'cf_ln_b', 'w_cf_o', 'w_o', 'ffn2_w_gu', 'ffn2_w_down']
TWIN_DIFF_INPUT = 'x'
TWIN_INPUTS = ['x', 'positions', 'norm_g', 'ffn1_w_gu', 'ffn1_w_down', 'w_in', 'w_ret_o', 'sc_conv_w', 'w_sc_o', 'cf_dw_w', 'cf_dw_b', 'cf_ln_g', 'cf_ln_b', 'w_cf_o', 'w_o', 'ffn2_w_gu', 'ffn2_w_down', 'loss_target', 'm_norm_g', 'm_ffn1_w_gu', 'm_ffn1_w_down', 'm_w_in', 'm_w_ret_o', 'm_sc_conv_w', 'm_w_sc_o', 'm_cf_dw_w', 'm_cf_dw_b', 'm_cf_ln_g', 'm_cf_ln_b', 'm_w_cf_o', 'm_w_o', 'm_ffn2_w_gu', 'm_ffn2_w_down', 'v_norm_g', 'v_ffn1_w_gu', 'v_ffn1_w_down', 'v_w_in', 'v_w_ret_o', 'v_sc_conv_w', 'v_w_sc_o', 'v_cf_dw_w', 'v_cf_dw_b', 'v_cf_ln_g', 'v_cf_ln_b', 'v_w_cf_o', 'v_w_o', 'v_ffn2_w_gu', 'v_ffn2_w_down']
TWIN_OUTPUTS = ['loss', 'grad_x', 'grad_norm_g', 'grad_ffn1_w_gu', 'grad_ffn1_w_down', 'grad_w_in', 'grad_w_ret_o', 'grad_sc_conv_w', 'grad_w_sc_o', 'grad_cf_dw_w', 'grad_cf_dw_b', 'grad_cf_ln_g', 'grad_cf_ln_b', 'grad_w_cf_o', 'grad_w_o', 'grad_ffn2_w_gu', 'grad_ffn2_w_down', 'delta_norm_g', 'delta_ffn1_w_gu', 'delta_ffn1_w_down', 'delta_w_in', 'delta_w_ret_o', 'delta_sc_conv_w', 'delta_w_sc_o', 'delta_cf_dw_w', 'delta_cf_dw_b', 'delta_cf_ln_g', 'delta_cf_ln_b', 'delta_w_cf_o', 'delta_w_o', 'delta_ffn2_w_gu', 'delta_ffn2_w_down', 'new_m_norm_g', 'new_m_ffn1_w_gu', 'new_m_ffn1_w_down', 'new_m_w_in', 'new_m_w_ret_o', 'new_m_sc_conv_w', 'new_m_w_sc_o', 'new_m_cf_dw_w', 'new_m_cf_dw_b', 'new_m_cf_ln_g', 'new_m_cf_ln_b', 'new_m_w_cf_o', 'new_m_w_o', 'new_m_ffn2_w_gu', 'new_m_ffn2_w_down', 'new_v_norm_g', 'new_v_ffn1_w_gu', 'new_v_ffn1_w_down', 'new_v_w_in', 'new_v_w_ret_o', 'new_v_sc_conv_w', 'new_v_w_sc_o', 'new_v_cf_dw_w', 'new_v_cf_dw_b', 'new_v_cf_ln_g', 'new_v_cf_ln_b', 'new_v_w_cf_o', 'new_v_w_o', 'new_v_ffn2_w_gu', 'new_v_ffn2_w_down']
TWIN_LEAF_KINDS = {'loss': 'loss', 'grad_x': 'grad_x', 'grad_norm_g': 'grad_w', 'grad_ffn1_w_gu': 'grad_w', 'grad_ffn1_w_down': 'grad_w', 'grad_w_in': 'grad_w', 'grad_w_ret_o': 'grad_w', 'grad_sc_conv_w': 'grad_w', 'grad_w_sc_o': 'grad_w', 'grad_cf_dw_w': 'grad_w', 'grad_cf_dw_b': 'grad_w', 'grad_cf_ln_g': 'grad_w', 'grad_cf_ln_b': 'grad_w', 'grad_w_cf_o': 'grad_w', 'grad_w_o': 'grad_w', 'grad_ffn2_w_gu': 'grad_w', 'grad_ffn2_w_down': 'grad_w', 'delta_norm_g': 'delta_w', 'delta_ffn1_w_gu': 'delta_w', 'delta_ffn1_w_down': 'delta_w', 'delta_w_in': 'delta_w', 'delta_w_ret_o': 'delta_w', 'delta_sc_conv_w': 'delta_w', 'delta_w_sc_o': 'delta_w', 'delta_cf_dw_w': 'delta_w', 'delta_cf_dw_b': 'delta_w', 'delta_cf_ln_g': 'delta_w', 'delta_cf_ln_b': 'delta_w', 'delta_w_cf_o': 'delta_w', 'delta_w_o': 'delta_w', 'delta_ffn2_w_gu': 'delta_w', 'delta_ffn2_w_down': 'delta_w', 'new_m_norm_g': 'new_m', 'new_m_ffn1_w_gu': 'new_m', 'new_m_ffn1_w_down': 'new_m', 'new_m_w_in': 'new_m', 'new_m_w_ret_o': 'new_m', 'new_m_sc_conv_w': 'new_m', 'new_m_w_sc_o': 'new_m', 'new_m_cf_dw_w': 'new_m', 'new_m_cf_dw_b': 'new_m', 'new_m_cf_ln_g': 'new_m', 'new_m_cf_ln_b': 'new_m', 'new_m_w_cf_o': 'new_m', 'new_m_w_o': 'new_m', 'new_m_ffn2_w_gu': 'new_m', 'new_m_ffn2_w_down': 'new_m', 'new_v_norm_g': 'new_v', 'new_v_ffn1_w_gu': 'new_v', 'new_v_ffn1_w_down': 'new_v', 'new_v_w_in': 'new_v', 'new_v_w_ret_o': 'new_v', 'new_v_sc_conv_w': 'new_v', 'new_v_w_sc_o': 'new_v', 'new_v_cf_dw_w': 'new_v', 'new_v_cf_dw_b': 'new_v', 'new_v_cf_ln_g': 'new_v', 'new_v_cf_ln_b': 'new_v', 'new_v_w_cf_o': 'new_v', 'new_v_w_o': 'new_v', 'new_v_ffn2_w_gu': 'new_v', 'new_v_ffn2_w_down': 'new_v'}


def _forward(args):
    return _fwd_reference(*[args[k] for k in FWD_PARAMS])


def _output_shape():
    out = _jax.eval_shape(lambda: _forward(_fwd_setup_inputs(0)))
    return out.shape, out.dtype

N_MICROBATCH = 1
ADAM_LR = 0.001
ADAM_B1 = 0.9
ADAM_B2 = 0.999
ADAM_EPS = 1e-08
ADAM_WD = 0.01
ADAM_STEP = 10
PER_EXAMPLE_BATCH_AXIS = {'x': 0, 'positions': 0, 'loss_target': 0}
SHARED_INPUTS = []
_WEIGHT_DTYPES = {'norm_g': _jnp.float32, 'ffn1_w_gu': _jnp.float32, 'ffn1_w_down': _jnp.float32, 'w_in': _jnp.float32, 'w_ret_o': _jnp.float32, 'sc_conv_w': _jnp.float32, 'w_sc_o': _jnp.float32, 'cf_dw_w': _jnp.float32, 'cf_dw_b': _jnp.float32, 'cf_ln_g': _jnp.float32, 'cf_ln_b': _jnp.float32, 'w_cf_o': _jnp.float32, 'w_o': _jnp.float32, 'ffn2_w_gu': _jnp.float32, 'ffn2_w_down': _jnp.float32}
MOMENT_SCALE = {'norm_g': 1.355213e+01, 'ffn1_w_gu': 2.552203e-01, 'ffn1_w_down': 5.137364e-01, 'w_in': 2.922494e-01, 'w_ret_o': 2.569565e-01, 'sc_conv_w': 4.196234e-01, 'w_sc_o': 4.292070e-01, 'cf_dw_w': 3.073552e-01, 'cf_dw_b': 2.546179e+00, 'cf_ln_g': 9.708988e-01, 'cf_ln_b': 1.556523e+00, 'w_cf_o': 5.726544e-01, 'w_o': 7.265504e-01, 'ffn2_w_gu': 1.563148e-01, 'ffn2_w_down': 3.424621e-01}


def _to_microbatches(a, axis):
    t = _jnp.moveaxis(a, axis, 0)
    t = t.reshape((N_MICROBATCH, t.shape[0] // N_MICROBATCH) + t.shape[1:])
    return _jnp.moveaxis(t, 1, axis + 1)


def setup_inputs(seed: int = 0) -> dict:
    inp = _fwd_setup_inputs(seed)
    key = _jax.random.fold_in(_jax.random.key(seed), 7919)
    shape, _ = _output_shape()
    out = dict(inp)
    out["loss_target"] = _jax.random.normal(_jax.random.fold_in(key, 0), shape, _jnp.float32)
    for i, name in enumerate(TWIN_WEIGHTS):
        w = inp[name].astype(_jnp.float32)
        if MOMENT_SCALE is None:
            s = _jnp.sqrt(_jnp.mean(_jnp.square(w)) + 1e-30)
        else:
            s = MOMENT_SCALE[name]
        km, kv = _jax.random.split(_jax.random.fold_in(key, i + 1))
        out[name] = w
        out["m_" + name] = s * _jax.random.normal(km, w.shape, _jnp.float32)
        out["v_" + name] = (s * s) * _jax.random.uniform(kv, w.shape, _jnp.float32, 0.5, 1.5)
    if N_MICROBATCH > 1:
        for name, axis in PER_EXAMPLE_BATCH_AXIS.items():
            out[name] = _to_microbatches(out[name], axis)
    return {'x': out['x'], 'positions': out['positions'], 'norm_g': out['norm_g'], 'ffn1_w_gu': out['ffn1_w_gu'], 'ffn1_w_down': out['ffn1_w_down'], 'w_in': out['w_in'], 'w_ret_o': out['w_ret_o'], 'sc_conv_w': out['sc_conv_w'], 'w_sc_o': out['w_sc_o'], 'cf_dw_w': out['cf_dw_w'], 'cf_dw_b': out['cf_dw_b'], 'cf_ln_g': out['cf_ln_g'], 'cf_ln_b': out['cf_ln_b'], 'w_cf_o': out['w_cf_o'], 'w_o': out['w_o'], 'ffn2_w_gu': out['ffn2_w_gu'], 'ffn2_w_down': out['ffn2_w_down'], 'loss_target': out['loss_target'], 'm_norm_g': out['m_norm_g'], 'm_ffn1_w_gu': out['m_ffn1_w_gu'], 'm_ffn1_w_down': out['m_ffn1_w_down'], 'm_w_in': out['m_w_in'], 'm_w_ret_o': out['m_w_ret_o'], 'm_sc_conv_w': out['m_sc_conv_w'], 'm_w_sc_o': out['m_w_sc_o'], 'm_cf_dw_w': out['m_cf_dw_w'], 'm_cf_dw_b': out['m_cf_dw_b'], 'm_cf_ln_g': out['m_cf_ln_g'], 'm_cf_ln_b': out['m_cf_ln_b'], 'm_w_cf_o': out['m_w_cf_o'], 'm_w_o': out['m_w_o'], 'm_ffn2_w_gu': out['m_ffn2_w_gu'], 'm_ffn2_w_down': out['m_ffn2_w_down'], 'v_norm_g': out['v_norm_g'], 'v_ffn1_w_gu': out['v_ffn1_w_gu'], 'v_ffn1_w_down': out['v_ffn1_w_down'], 'v_w_in': out['v_w_in'], 'v_w_ret_o': out['v_w_ret_o'], 'v_sc_conv_w': out['v_sc_conv_w'], 'v_w_sc_o': out['v_w_sc_o'], 'v_cf_dw_w': out['v_cf_dw_w'], 'v_cf_dw_b': out['v_cf_dw_b'], 'v_cf_ln_g': out['v_cf_ln_g'], 'v_cf_ln_b': out['v_cf_ln_b'], 'v_w_cf_o': out['v_w_cf_o'], 'v_w_o': out['v_w_o'], 'v_ffn2_w_gu': out['v_ffn2_w_gu'], 'v_ffn2_w_down': out['v_ffn2_w_down']}


def _loss(weights, diff, rest, loss_target):
    with _jax.named_scope("forward"):
        args = {**rest, TWIN_DIFF_INPUT: diff, **{k: w.astype(_WEIGHT_DTYPES[k]) for k, w in weights.items()}}
        y = _forward(args)
    with _jax.named_scope("loss_head"):
        err = _jnp.square(y.astype(_jnp.float32) - loss_target)
        return 0.5 * _jnp.sum(_jnp.mean(err, axis=-1)) if err.ndim else 0.5 * err


def _adamw(w, g, m, v):
    m = ADAM_B1 * m + (1.0 - ADAM_B1) * g
    v = ADAM_B2 * v + (1.0 - ADAM_B2) * _jnp.square(g)
    m_hat = m / (1.0 - ADAM_B1 ** ADAM_STEP)
    v_hat = v / (1.0 - ADAM_B2 ** ADAM_STEP)
    delta = -ADAM_LR * (m_hat / (_jnp.sqrt(v_hat) + ADAM_EPS) + ADAM_WD * w)
    return delta, m, v


def reference(x, positions, norm_g, ffn1_w_gu, ffn1_w_down, w_in, w_ret_o, sc_conv_w, w_sc_o, cf_dw_w, cf_dw_b, cf_ln_g, cf_ln_b, w_cf_o, w_o, ffn2_w_gu, ffn2_w_down, loss_target, m_norm_g, m_ffn1_w_gu, m_ffn1_w_down, m_w_in, m_w_ret_o, m_sc_conv_w, m_w_sc_o, m_cf_dw_w, m_cf_dw_b, m_cf_ln_g, m_cf_ln_b, m_w_cf_o, m_w_o, m_ffn2_w_gu, m_ffn2_w_down, v_norm_g, v_ffn1_w_gu, v_ffn1_w_down, v_w_in, v_w_ret_o, v_sc_conv_w, v_w_sc_o, v_cf_dw_w, v_cf_dw_b, v_cf_ln_g, v_cf_ln_b, v_w_cf_o, v_w_o, v_ffn2_w_gu, v_ffn2_w_down):
    given = dict(x=x, positions=positions, norm_g=norm_g, ffn1_w_gu=ffn1_w_gu, ffn1_w_down=ffn1_w_down, w_in=w_in, w_ret_o=w_ret_o, sc_conv_w=sc_conv_w, w_sc_o=w_sc_o, cf_dw_w=cf_dw_w, cf_dw_b=cf_dw_b, cf_ln_g=cf_ln_g, cf_ln_b=cf_ln_b, w_cf_o=w_cf_o, w_o=w_o, ffn2_w_gu=ffn2_w_gu, ffn2_w_down=ffn2_w_down, loss_target=loss_target, m_norm_g=m_norm_g, m_ffn1_w_gu=m_ffn1_w_gu, m_ffn1_w_down=m_ffn1_w_down, m_w_in=m_w_in, m_w_ret_o=m_w_ret_o, m_sc_conv_w=m_sc_conv_w, m_w_sc_o=m_w_sc_o, m_cf_dw_w=m_cf_dw_w, m_cf_dw_b=m_cf_dw_b, m_cf_ln_g=m_cf_ln_g, m_cf_ln_b=m_cf_ln_b, m_w_cf_o=m_w_cf_o, m_w_o=m_w_o, m_ffn2_w_gu=m_ffn2_w_gu, m_ffn2_w_down=m_ffn2_w_down, v_norm_g=v_norm_g, v_ffn1_w_gu=v_ffn1_w_gu, v_ffn1_w_down=v_ffn1_w_down, v_w_in=v_w_in, v_w_ret_o=v_w_ret_o, v_sc_conv_w=v_sc_conv_w, v_w_sc_o=v_w_sc_o, v_cf_dw_w=v_cf_dw_w, v_cf_dw_b=v_cf_dw_b, v_cf_ln_g=v_cf_ln_g, v_cf_ln_b=v_cf_ln_b, v_w_cf_o=v_w_cf_o, v_w_o=v_w_o, v_ffn2_w_gu=v_ffn2_w_gu, v_ffn2_w_down=v_ffn2_w_down)
    weights = {n: given[n] for n in TWIN_WEIGHTS}
    shared = {n: given[n] for n in SHARED_INPUTS}
    per_example = {n: given[n] for n in ['x', 'positions']}
    grad_fn = _jax.value_and_grad(_loss, argnums=(0, 1))

    def one_microbatch(ex, loss_target):
        ex = dict(ex)
        diff = ex.pop(TWIN_DIFF_INPUT)
        return grad_fn(weights, diff, {**shared, **ex}, loss_target)

    if N_MICROBATCH == 1:
        loss, (grad_w, grad_x) = one_microbatch(per_example, given["loss_target"])
    else:
        def body(carry, xs):
            loss_sum, grad_sum = carry
            l_k, (gw_k, gx_k) = one_microbatch(xs[0], xs[1])
            with _jax.named_scope("update"):
                return (loss_sum + l_k, _jax.tree.map(_jnp.add, grad_sum, gw_k)), gx_k

        init = (_jnp.zeros((), _jnp.float32), _jax.tree.map(_jnp.zeros_like, weights))
        (loss, grad_w), grad_x = _jax.lax.scan(body, init, (per_example, given["loss_target"]))
    with _jax.named_scope("update"):
        delta_w, new_m, new_v = {}, {}, {}
        for n in TWIN_WEIGHTS:
            delta_w[n], new_m[n], new_v[n] = _adamw(weights[n], grad_w[n], given["m_" + n], given["v_" + n])
    return (loss, grad_x, *[grad_w[n] for n in TWIN_WEIGHTS], *[delta_w[n] for n in TWIN_WEIGHTS],
            *[new_m[n] for n in TWIN_WEIGHTS], *[new_v[n] for n in TWIN_WEIGHTS])
```

```python
import functools

import jax
import jax.numpy as jnp
import numpy as np
from jax import lax
from jax.experimental import pallas as pl
from jax.experimental.pallas import tpu as pltpu

F32 = jnp.float32
BF16 = jnp.bfloat16
MXU_DTYPE = BF16
VMEM_LIMIT_BYTES = 56 * 1024 * 1024
MESH = pl.DeviceIdType.MESH

N_CHIP = 4
DEPTH = 2
CHUNK = 64
RET_HEADS = 4
RET_QK_DIM = 128
RET_V_DIM = 256
SC_KERNEL = 3
CF_KERNEL = 31
ROPE_BASE = 10000.0
NORM_EPS = 1e-6
LN_EPS = 1e-5
ADAM_LR = 0.001
ADAM_B1 = 0.9
ADAM_B2 = 0.999
ADAM_EPS = 1e-08
ADAM_WD = 0.01
ADAM_STEP = 10

CONV_PAD = 32
CONV_TS = 128
CONV_TC = 512
RET_TQ = 512


def _params(sem):
    return pltpu.CompilerParams(dimension_semantics=sem, vmem_limit_bytes=VMEM_LIMIT_BYTES)


def _axes():
    return lax.axis_index("x"), lax.axis_index("y"), lax.axis_index("c")


NN = (((1,), (0,)), ((), ()))
NT = (((1,), (1,)), ((), ()))
TN = (((0,), (0,)), ((), ()))


def _mm(name, a, b, out_shape, out_dtype, grid, a_spec, b_spec, o_spec, dims, acc_shape, into=None):
    nk = grid[2]

    def body(*refs):
        a_ref, b_ref = refs[0], refs[1]
        o_ref = refs[3] if into is not None else refs[2]
        part = lax.dot_general(a_ref[...], b_ref[...], dims, preferred_element_type=F32)
        if nk == 1:
            o_ref[...] = part.astype(o_ref.dtype)
        else:
            acc = refs[-1]
            k = pl.program_id(2)

            @pl.when(k == 0)
            def _():
                acc[...] = part

            @pl.when(k > 0)
            def _():
                acc[...] += part

            @pl.when(k == nk - 1)
            def _():
                o_ref[...] = acc[...].astype(o_ref.dtype)

    in_specs = [a_spec, b_spec]
    args = [a, b]
    aliases = {}
    if into is not None:
        in_specs.append(pl.BlockSpec(memory_space=pl.ANY))
        args.append(into)
        aliases = {2: 0}
    scratch = [pltpu.VMEM(acc_shape, F32)] if nk > 1 else []
    return pl.pallas_call(
        body, name=name, grid=grid, in_specs=in_specs, out_specs=o_spec,
        out_shape=jax.ShapeDtypeStruct(out_shape, out_dtype), scratch_shapes=scratch,
        input_output_aliases=aliases,
        compiler_params=_params(("parallel", "parallel", "arbitrary")),
    )(*args)


def _tile(n, target):
    best = None
    for t in range(128, min(n, target) + 1, 128):
        if n % t == 0:
            best = t
    assert best is not None, (n, target)
    return best


def mm_fwd(name, a, w4, layer, mode, out_dtype):
    t = a.shape[0]
    _, _, r, c = w4.shape
    tm = min(t, 1024)
    if mode == "col":
        tn = _tile(c, 1536)
        npj = c // tn
        grid = (t // tm, N_CHIP * npj, 1)
        a_spec = pl.BlockSpec((tm, r), lambda i, j, k: (i, 0))
        b_spec = pl.BlockSpec((None, None, r, tn), lambda i, j, k: (j // npj, layer, 0, j % npj))
        o_spec = pl.BlockSpec((tm, tn), lambda i, j, k: (i, j))
        return _mm(name, a, w4, (t, N_CHIP * c), out_dtype, grid, a_spec, b_spec, o_spec, NN, (tm, tn))
    tk = min(r, 1024)
    npk = r // tk
    grid = (t // tm, 1, N_CHIP * npk)
    a_spec = pl.BlockSpec((tm, tk), lambda i, j, k: (i, k))
    b_spec = pl.BlockSpec((None, None, tk, c), lambda i, j, k: (k // npk, layer, k % npk, 0))
    o_spec = pl.BlockSpec((tm, c), lambda i, j, k: (i, 0))
    return _mm(name, a, w4, (t, c), out_dtype, grid, a_spec, b_spec, o_spec, NN, (tm, c))


def mm_dx(name, dy, w4, layer, mode, out_dtype):
    t = dy.shape[0]
    _, _, r, c = w4.shape
    tm = min(t, 1024)
    if mode == "col":
        tn = _tile(c, 1536)
        npj = c // tn
        grid = (t // tm, 1, N_CHIP * npj)
        a_spec = pl.BlockSpec((tm, tn), lambda i, j, k: (i, k))
        b_spec = pl.BlockSpec((None, None, r, tn), lambda i, j, k: (k // npj, layer, 0, k % npj))
        o_spec = pl.BlockSpec((tm, r), lambda i, j, k: (i, 0))
        return _mm(name, dy, w4, (t, r), out_dtype, grid, a_spec, b_spec, o_spec, NT, (tm, r))
    grid = (t // tm, N_CHIP, 1)
    a_spec = pl.BlockSpec((tm, c), lambda i, j, k: (i, 0))
    b_spec = pl.BlockSpec((None, None, r, c), lambda i, j, k: (j, layer, 0, 0))
    o_spec = pl.BlockSpec((tm, r), lambda i, j, k: (i, j))
    return _mm(name, dy, w4, (t, N_CHIP * r), out_dtype, grid, a_spec, b_spec, o_spec, NT, (tm, r))


def mm_dw(name, a, dy, layer, mode, shape4, into):
    t = a.shape[0]
    _, _, r, c = shape4
    tt = min(t, 1024)
    nk = t // tt
    if mode == "col":
        tn = _tile(c, 1536)
        npj = c // tn
        grid = (1, N_CHIP * npj, nk)
        a_spec = pl.BlockSpec((tt, r), lambda i, j, k: (k, 0))
        b_spec = pl.BlockSpec((tt, tn), lambda i, j, k: (k, j))
        o_spec = pl.BlockSpec((None, None, r, tn), lambda i, j, k: (j // npj, layer, 0, j % npj))
        return _mm(name, a, dy, shape4, MXU_DTYPE, grid, a_spec, b_spec, o_spec, TN, (r, tn), into=into)
    grid = (N_CHIP, 1, nk)
    a_spec = pl.BlockSpec((tt, r), lambda i, j, k: (k, i))
    b_spec = pl.BlockSpec((tt, c), lambda i, j, k: (k, 0))
    o_spec = pl.BlockSpec((None, None, r, c), lambda i, j, k: (i, layer, 0, 0))
    return _mm(name, a, dy, shape4, MXU_DTYPE, grid, a_spec, b_spec, o_spec, TN, (r, c), into=into)


def _rowwise(name, fn, rows, pars, outs, accs=(), tm=256, ncol=1):
    t = rows[0][0].shape[0]
    nrow, npar, nout = len(rows), len(pars), len(outs)

    def body(*refs):
        vals = [r[...] for r in refs[:nrow + npar]]
        res = fn(*vals)
        out_refs = refs[nrow + npar:nrow + npar + nout]
        acc_refs = refs[nrow + npar + nout:]
        for o, v in zip(out_refs, res[:nout]):
            o[...] = v.astype(o.dtype)
        i = pl.program_id(1)
        for a, v in zip(acc_refs, res[nout:]):
            @pl.when(i == 0)
            def _(a=a, v=v):
                a[...] = v.astype(F32)

            @pl.when(i > 0)
            def _(a=a, v=v):
                a[...] += v.astype(F32)

    in_specs = [pl.BlockSpec((tm, w), functools.partial(lambda j, i, b: (i, b + j), b=b)) for _, w, b in rows]
    for arr, w in pars:
        if w is None:
            in_specs.append(pl.BlockSpec(arr.shape, lambda j, i: (0, 0)))
        else:
            in_specs.append(pl.BlockSpec((1, w), lambda j, i: (0, j)))
    out_specs = [pl.BlockSpec((tm, w), lambda j, i: (i, j)) for _, w, _ in outs]
    out_specs += [pl.BlockSpec((1, w), lambda j, i: (0, j)) for _, w in accs]
    out_shape = [jax.ShapeDtypeStruct((t, tw), dt) for tw, _, dt in outs]
    out_shape += [jax.ShapeDtypeStruct((1, tw), F32) for tw, _ in accs]
    res = pl.pallas_call(
        body, name=name, grid=(ncol, t // tm), in_specs=in_specs, out_specs=out_specs, out_shape=out_shape,
        compiler_params=_params(("parallel", "arbitrary" if accs else "parallel")),
    )(*[r[0] for r in rows], *[p[0] for p in pars])
    return res


def _rms(x, g):
    xf = x.astype(F32)
    return xf * lax.rsqrt(jnp.mean(xf * xf, axis=-1, keepdims=True) + NORM_EPS) * g


def _silu(x):
    return x * jax.nn.sigmoid(x)


def rms_fwd(name, x, g):
    d = x.shape[1]
    return _rowwise(name, lambda x, g: (_rms(x, g),), [(x, d, 0)], [(g, None)], [(d, d, MXU_DTYPE)], tm=512)[0]


def rms_bwd(name, x, g, dh, dres):
    d = x.shape[1]

    def fn(x, dh, dres, g):
        _, vjp = jax.vjp(_rms, x, g)
        dx, dg = vjp(dh.astype(F32))
        return dres + dx, dg

    return _rowwise(name, fn, [(x, d, 0), (dh, d, 0), (dres, d, 0)], [(g, None)], [(d, d, F32)], [(d, d)], tm=256)


def post_fwd(name, x, y, g, scale):
    d = x.shape[1]
    return _rowwise(name, lambda x, y, g: (x + scale * _rms(y, g),), [(x, d, 0), (y, d, 0)], [(g, None)],
                    [(d, d, F32)], tm=512)[0]


def post_bwd(name, y, g, dx, scale):
    d = y.shape[1]

    def fn(y, dx, g):
        _, vjp = jax.vjp(lambda y, g: scale * _rms(y, g), y, g)
        return vjp(dx)

    return _rowwise(name, fn, [(y, d, 0), (dx, d, 0)], [(g, None)], [(d, d, MXU_DTYPE)], [(d, d)], tm=256)


def _swiglu(gu):
    f = gu.shape[1] // 2
    return _silu(gu[:, :f].astype(F32)) * gu[:, f:].astype(F32)


def swiglu_fwd(name, gu):
    w = gu.shape[1]
    return _rowwise(name, lambda gu: (_swiglu(gu),), [(gu, w, 0)], [], [(w // 2, w // 2, MXU_DTYPE)], tm=256)[0]


def swiglu_bwd(name, gu, da):
    w = gu.shape[1]
    f = w // 2

    def fn(gu, da):
        gate, up = gu[:, :f].astype(F32), gu[:, f:].astype(F32)
        _, vjp = jax.vjp(lambda a, b: _silu(a) * b, gate, up)
        dgate, dup = vjp(da.astype(F32))
        return (jnp.concatenate([dgate, dup], axis=-1),)

    return _rowwise(name, fn, [(gu, w, 0), (da, f, 0)], [], [(w, w, MXU_DTYPE)], tm=128)[0]


def _head_gate(o, g):
    mu = jnp.mean(o, axis=-1, keepdims=True)
    var = jnp.mean(jnp.square(o - mu), axis=-1, keepdims=True)
    return _silu(g.astype(F32)) * ((o - mu) * lax.rsqrt(var + LN_EPS))


def head_gate_fwd(name, o, p, gate_blk):
    dv = RET_V_DIM
    return _rowwise(name, lambda o, g: (_head_gate(o, g),), [(o, dv, 0), (p, dv, gate_blk)], [],
                    [(RET_HEADS * dv, dv, MXU_DTYPE)], tm=512, ncol=RET_HEADS)[0]


def head_gate_bwd(name, o, p, gate_blk, da):
    dv = RET_V_DIM

    def fn(o, g, da):
        _, vjp = jax.vjp(_head_gate, o, g.astype(F32))
        return vjp(da.astype(F32))

    w = RET_HEADS * dv
    return _rowwise(name, fn, [(o, dv, 0), (p, dv, gate_blk), (da, dv, 0)], [],
                    [(w, dv, MXU_DTYPE), (w, dv, MXU_DTYPE)], tm=512, ncol=RET_HEADS)


def _ln_silu(u, g, b):
    mu = jnp.mean(u, axis=-1, keepdims=True)
    var = jnp.mean(jnp.square(u - mu), axis=-1, keepdims=True)
    return _silu((u - mu) * lax.rsqrt(var + LN_EPS) * g + b)


def ln_silu_fwd(name, u, g, b):
    d = u.shape[1]
    return _rowwise(name, lambda u, g, b: (_ln_silu(u, g, b),), [(u, d, 0)], [(g, None), (b, None)],
                    [(d, d, MXU_DTYPE)], tm=512)[0]


def ln_silu_bwd(name, u, g, b, dc):
    d = u.shape[1]

    def fn(u, dc, g, b):
        _, vjp = jax.vjp(_ln_silu, u, g, b)
        return vjp(dc.astype(F32))

    return _rowwise(name, fn, [(u, d, 0), (dc, d, 0)], [(g, None), (b, None)], [(d, d, F32)], [(d, d), (d, d)],
                    tm=256)


def _merge(g0, g1, g2, ya, yb, yc):
    s = jax.nn.sigmoid
    return s(g0.astype(F32)) * ya + s(g1.astype(F32)) * yb + s(g2.astype(F32)) * yc


def merge_fwd(name, p, blk, ya, yb, yc):
    d = ya.shape[1]
    rows = [(p, d, blk), (p, d, blk + 1), (p, d, blk + 2), (ya, d, 0), (yb, d, 0), (yc, d, 0)]
    return _rowwise(name, lambda *v: (_merge(*v),), rows, [], [(d, d, MXU_DTYPE)], tm=256)[0]


def merge_bwd(name, p, blk, ya, yb, yc, dmg):
    d = ya.shape[1]

    def fn(g0, g1, g2, ya, yb, yc, dmg):
        _, vjp = jax.vjp(_merge, g0.astype(F32), g1.astype(F32), g2.astype(F32), ya, yb, yc)
        return vjp(dmg.astype(F32))

    rows = [(p, d, blk), (p, d, blk + 1), (p, d, blk + 2), (ya, d, 0), (yb, d, 0), (yc, d, 0), (dmg, d, 0)]
    return _rowwise(name, fn, rows, [], [(d, d, MXU_DTYPE)] * 6, tm=256)


def loss_head(name, y, target):
    t, d = y.shape
    tm = 512

    def body(y_ref, t_ref, dy_ref, loss_ref):
        err = y_ref[...] - t_ref[...]
        dy_ref[...] = err * (1.0 / d)
        part = jnp.sum(jnp.sum(err * err, axis=1, keepdims=True), axis=0, keepdims=True) * (0.5 / d)

        @pl.when(pl.program_id(0) == 0)
        def _():
            loss_ref[...] = part

        @pl.when(pl.program_id(0) > 0)
        def _():
            loss_ref[...] += part

    return pl.pallas_call(
        body, name=name, grid=(t // tm,),
        in_specs=[pl.BlockSpec((tm, d), lambda i: (i, 0))] * 2,
        out_specs=[pl.BlockSpec((tm, d), lambda i: (i, 0)), pl.BlockSpec((1, 1), lambda i: (0, 0))],
        out_shape=[jax.ShapeDtypeStruct((t, d), F32), jax.ShapeDtypeStruct((1, 1), F32)],
        compiler_params=_params(("arbitrary",)),
    )(y, target)


def _rot(x, cos2, sin2):
    return x * cos2 + pltpu.roll(x, RET_QK_DIM // 2, 1) * sin2


def _decay_mask(lg, n0, rows, cols):
    n = n0 + lax.broadcasted_iota(jnp.int32, (rows, cols), 0)
    m = lax.broadcasted_iota(jnp.int32, (rows, cols), 1)
    shift = CHUNK.bit_length() - 1
    dist = jnp.abs(n - m).astype(F32)
    return jnp.where((m >> shift) <= (n >> shift), jnp.exp(lg * dist), 0.0)


def _ret_specs(s):
    dk, dv, h = RET_QK_DIM, RET_V_DIM, RET_HEADS
    return [
        pl.BlockSpec((s, dk), lambda b, hh: (b, hh)),
        pl.BlockSpec((s, dk), lambda b, hh: (b, h + hh)),
        pl.BlockSpec((s, dv), lambda b, hh: (b, (2 * h * dk) // dv + hh)),
        pl.BlockSpec((s, dk), lambda b, hh: (b, 0)),
        pl.BlockSpec((s, dk), lambda b, hh: (b, 0)),
        pl.BlockSpec((None, 1, dk), lambda b, hh: (hh, 0, 0)),
    ]


def retention_fwd(name, p, cos2, sin2, log_g, nb, s):
    dk, dv, h = RET_QK_DIM, RET_V_DIM, RET_HEADS

    def body(q_ref, k_ref, v_ref, cos_ref, sin_ref, lg_ref, o_ref, kr_ref):
        lg = lg_ref[0:1, 0:1]
        kr = _rot(k_ref[...].astype(F32), cos_ref[...], sin_ref[...]) * (dk ** -0.5)
        kr_ref[...] = kr.astype(kr_ref.dtype)
        for qi in range(s // RET_TQ):
            n0, kmax = qi * RET_TQ, (qi + 1) * RET_TQ
            rows = pl.ds(n0, RET_TQ)
            qr = _rot(q_ref[rows, :].astype(F32), cos_ref[rows, :], sin_ref[rows, :]).astype(MXU_DTYPE)
            sc = lax.dot_general(qr, kr_ref[0:kmax, :], NT, preferred_element_type=F32)
            pm = (sc * _decay_mask(lg, n0, RET_TQ, kmax)).astype(MXU_DTYPE)
            o_ref[rows, :] = lax.dot_general(pm, v_ref[0:kmax, :], NN, preferred_element_type=F32)

    return pl.pallas_call(
        body, name=name, grid=(nb, h), in_specs=_ret_specs(s),
        out_specs=pl.BlockSpec((s, dv), lambda b, hh: (b, hh)),
        out_shape=jax.ShapeDtypeStruct((nb * s, h * dv), F32),
        scratch_shapes=[pltpu.VMEM((s, dk), MXU_DTYPE)],
        compiler_params=_params(("parallel", "parallel")),
    )(p, p, p, cos2, sin2, log_g)


def retention_bwd(name, p, cos2, sin2, log_g, do, nb, s):
    dk, dv, h = RET_QK_DIM, RET_V_DIM, RET_HEADS

    def body(q_ref, k_ref, v_ref, cos_ref, sin_ref, lg_ref, do_ref, dq_ref, dk_ref, dv_ref, kr_ref, dk_acc, dv_acc):
        lg = lg_ref[0:1, 0:1]
        kr = _rot(k_ref[...].astype(F32), cos_ref[...], sin_ref[...]) * (dk ** -0.5)
        kr_ref[...] = kr.astype(kr_ref.dtype)
        dk_acc[...] = jnp.zeros_like(dk_acc)
        dv_acc[...] = jnp.zeros_like(dv_acc)
        for qi in range(s // RET_TQ):
            n0, kmax = qi * RET_TQ, (qi + 1) * RET_TQ
            rows = pl.ds(n0, RET_TQ)
            cq, sq = cos_ref[rows, :], sin_ref[rows, :]
            qr = _rot(q_ref[rows, :].astype(F32), cq, sq).astype(MXU_DTYPE)
            dob = do_ref[rows, :]
            mask = _decay_mask(lg, n0, RET_TQ, kmax)
            sc = lax.dot_general(qr, kr_ref[0:kmax, :], NT, preferred_element_type=F32)
            pm = (sc * mask).astype(MXU_DTYPE)
            dv_acc[0:kmax, :] += lax.dot_general(pm, dob, TN, preferred_element_type=F32)
            dp = lax.dot_general(dob, v_ref[0:kmax, :], NT, preferred_element_type=F32)
            ds = (dp * mask).astype(MXU_DTYPE)
            dqr = lax.dot_general(ds, kr_ref[0:kmax, :], NN, preferred_element_type=F32)
            dq_ref[rows, :] = _rot(dqr, cq, -sq).astype(dq_ref.dtype)
            dk_acc[0:kmax, :] += lax.dot_general(ds, qr, TN, preferred_element_type=F32)
        dkr = dk_acc[...] * (dk ** -0.5)
        dk_ref[...] = _rot(dkr, cos_ref[...], -sin_ref[...]).astype(dk_ref.dtype)
        dv_ref[...] = dv_acc[...].astype(dv_ref.dtype)

    t = nb * s
    return pl.pallas_call(
        body, name=name, grid=(nb, h),
        in_specs=_ret_specs(s) + [pl.BlockSpec((s, dv), lambda b, hh: (b, hh))],
        out_specs=[pl.BlockSpec((s, dk), lambda b, hh: (b, hh)), pl.BlockSpec((s, dk), lambda b, hh: (b, hh)),
                   pl.BlockSpec((s, dv), lambda b, hh: (b, hh))],
        out_shape=[jax.ShapeDtypeStruct((t, h * dk), MXU_DTYPE), jax.ShapeDtypeStruct((t, h * dk), MXU_DTYPE),
                   jax.ShapeDtypeStruct((t, h * dv), MXU_DTYPE)],
        scratch_shapes=[pltpu.VMEM((s, dk), MXU_DTYPE), pltpu.VMEM((s, dk), F32), pltpu.VMEM((s, dv), F32)],
        compiler_params=_params(("parallel", "parallel")),
    )(p, p, p, cos2, sin2, log_g, do)


def _conv_grid(t, d, nb):
    s = t // nb
    ns, nc = s // CONV_TS, d // CONV_TC
    return s, ns, nc


def _causal_taps(pad_ref, w_ref, k):
    acc = None
    for j in range(k):
        term = w_ref[j:j + 1, :] * pad_ref[pl.ds(CONV_PAD - (k - 1) + j, CONV_TS), :]
        acc = term if acc is None else acc + term
    return acc


def _carry_past(pad_ref, s_idx):
    @pl.when(s_idx == 0)
    def _():
        pad_ref[0:CONV_PAD, :] = jnp.zeros((CONV_PAD, pad_ref.shape[1]), F32)

    @pl.when(s_idx > 0)
    def _():
        pad_ref[0:CONV_PAD, :] = pad_ref[CONV_TS:CONV_TS + CONV_PAD, :]


def _carry_future(pad_ref, s_idx):
    @pl.when(s_idx == 0)
    def _():
        pad_ref[CONV_TS:CONV_TS + CONV_PAD, :] = jnp.zeros((CONV_PAD, pad_ref.shape[1]), F32)

    @pl.when(s_idx > 0)
    def _():
        pad_ref[CONV_TS:CONV_TS + CONV_PAD, :] = pad_ref[0:CONV_PAD, :]


def _conv_bwd_taps(pad_ref, w_ref, x, dw_ref, k):
    acc = None
    for j in range(k):
        sh = pad_ref[pl.ds(k - 1 - j, CONV_TS), :]
        term = w_ref[j:j + 1, :] * sh
        acc = term if acc is None else acc + term
        dw_ref[j:j + 1, :] += jnp.sum(x * sh, axis=0, keepdims=True)
    return acc


def short_conv_fwd(name, p, blk_b, w, nb):
    t = p.shape[0]
    d = w.shape[1]
    s, ns, nc = _conv_grid(t, d, nb)
    cb = d // CONV_TC

    def body(b_ref, c_ref, x_ref, w_ref, y_ref, cz_ref, pad_ref):
        _carry_past(pad_ref, pl.program_id(2))
        pad_ref[CONV_PAD:CONV_PAD + CONV_TS, :] = c_ref[...].astype(F32) * x_ref[...].astype(F32)
        cz = _causal_taps(pad_ref, w_ref, SC_KERNEL)
        cz_ref[...] = cz
        y_ref[...] = (b_ref[...].astype(F32) * cz).astype(y_ref.dtype)

    def pspec(off):
        return pl.BlockSpec((CONV_TS, CONV_TC), lambda c, b, si: (b * ns + si, (blk_b + off) * cb + c))

    ospec = pl.BlockSpec((CONV_TS, CONV_TC), lambda c, b, si: (b * ns + si, c))
    return pl.pallas_call(
        body, name=name, grid=(nc, nb, ns),
        in_specs=[pspec(0), pspec(1), pspec(2), pl.BlockSpec((SC_KERNEL, CONV_TC), lambda c, b, si: (0, c))],
        out_specs=[ospec, ospec],
        out_shape=[jax.ShapeDtypeStruct((t, d), MXU_DTYPE), jax.ShapeDtypeStruct((t, d), F32)],
        scratch_shapes=[pltpu.VMEM((CONV_PAD + CONV_TS, CONV_TC), F32)],
        compiler_params=_params(("parallel", "arbitrary", "arbitrary")),
    )(p, p, p, w)


def short_conv_bwd(name, p, blk_b, w, cz, dy, nb):
    t = p.shape[0]
    d = w.shape[1]
    s, ns, nc = _conv_grid(t, d, nb)
    cb = d // CONV_TC

    def body(b_ref, c_ref, x_ref, w_ref, cz_ref, dy_ref, db_ref, dc_ref, dx_ref, dw_ref, pad_ref):
        si = pl.program_id(2)
        _carry_future(pad_ref, si)
        dyv = dy_ref[...].astype(F32)
        cv, xv = c_ref[...].astype(F32), x_ref[...].astype(F32)
        db_ref[...] = (dyv * cz_ref[...]).astype(db_ref.dtype)
        pad_ref[0:CONV_TS, :] = dyv * b_ref[...].astype(F32)

        @pl.when(jnp.logical_and(pl.program_id(1) == 0, si == 0))
        def _():
            dw_ref[...] = jnp.zeros_like(dw_ref)

        dz = _conv_bwd_taps(pad_ref, w_ref, cv * xv, dw_ref, SC_KERNEL)
        dc_ref[...] = (dz * xv).astype(dc_ref.dtype)
        dx_ref[...] = (dz * cv).astype(dx_ref.dtype)

    def row(b, si):
        return b * ns + (ns - 1 - si)

    def pspec(off):
        return pl.BlockSpec((CONV_TS, CONV_TC), lambda c, b, si: (row(b, si), (blk_b + off) * cb + c))

    ospec = pl.BlockSpec((CONV_TS, CONV_TC), lambda c, b, si: (row(b, si), c))
    wspec = pl.BlockSpec((SC_KERNEL, CONV_TC), lambda c, b, si: (0, c))
    return pl.pallas_call(
        body, name=name, grid=(nc, nb, ns),
        in_specs=[pspec(0), pspec(1), pspec(2), wspec, ospec, ospec],
        out_specs=[ospec, ospec, ospec, wspec],
        out_shape=[jax.ShapeDtypeStruct((t, d), MXU_DTYPE)] * 3 + [jax.ShapeDtypeStruct((SC_KERNEL, d), F32)],
        scratch_shapes=[pltpu.VMEM((CONV_TS + CONV_PAD, CONV_TC), F32)],
        compiler_params=_params(("parallel", "arbitrary", "arbitrary")),
    )(p, p, p, w, cz, dy)


def conformer_conv_fwd(name, p, blk_a, w, bias, nb):
    t = p.shape[0]
    d = w.shape[1]
    s, ns, nc = _conv_grid(t, d, nb)
    cb = d // CONV_TC

    def body(a_ref, b_ref, w_ref, bias_ref, u_ref, pad_ref):
        _carry_past(pad_ref, pl.program_id(2))
        pad_ref[CONV_PAD:CONV_PAD + CONV_TS, :] = a_ref[...].astype(F32) * jax.nn.sigmoid(b_ref[...].astype(F32))
        u_ref[...] = _causal_taps(pad_ref, w_ref, CF_KERNEL) + bias_ref[...]

    def pspec(off):
        return pl.BlockSpec((CONV_TS, CONV_TC), lambda c, b, si: (b * ns + si, (blk_a + off) * cb + c))

    return pl.pallas_call(
        body, name=name, grid=(nc, nb, ns),
        in_specs=[pspec(0), pspec(1), pl.BlockSpec((CF_KERNEL, CONV_TC), lambda c, b, si: (0, c)),
                  pl.BlockSpec((1, CONV_TC), lambda c, b, si: (0, c))],
        out_specs=pl.BlockSpec((CONV_TS, CONV_TC), lambda c, b, si: (b * ns + si, c)),
        out_shape=jax.ShapeDtypeStruct((t, d), F32),
        scratch_shapes=[pltpu.VMEM((CONV_PAD + CONV_TS, CONV_TC), F32)],
        compiler_params=_params(("parallel", "arbitrary", "arbitrary")),
    )(p, p, w, bias)


def conformer_conv_bwd(name, p, blk_a, w, du, nb):
    t = p.shape[0]
    d = w.shape[1]
    s, ns, nc = _conv_grid(t, d, nb)
    cb = d // CONV_TC

    def body(a_ref, b_ref, w_ref, du_ref, da_ref, db_ref, dw_ref, dbias_ref, pad_ref):
        si = pl.program_id(2)
        _carry_future(pad_ref, si)
        duv = du_ref[...]
        av = a_ref[...].astype(F32)
        sg = jax.nn.sigmoid(b_ref[...].astype(F32))
        pad_ref[0:CONV_TS, :] = duv

        @pl.when(jnp.logical_and(pl.program_id(1) == 0, si == 0))
        def _():
            dw_ref[...] = jnp.zeros_like(dw_ref)
            dbias_ref[...] = jnp.zeros_like(dbias_ref)

        du0 = _conv_bwd_taps(pad_ref, w_ref, av * sg, dw_ref, CF_KERNEL)
        da_ref[...] = (du0 * sg).astype(da_ref.dtype)
        db_ref[...] = (du0 * av * sg * (1.0 - sg)).astype(db_ref.dtype)
        dbias_ref[...] += jnp.sum(duv, axis=0, keepdims=True)

    def row(b, si):
        return b * ns + (ns - 1 - si)

    def pspec(off):
        return pl.BlockSpec((CONV_TS, CONV_TC), lambda c, b, si: (row(b, si), (blk_a + off) * cb + c))

    ospec = pl.BlockSpec((CONV_TS, CONV_TC), lambda c, b, si: (row(b, si), c))
    wspec = pl.BlockSpec((CF_KERNEL, CONV_TC), lambda c, b, si: (0, c))
    bspec = pl.BlockSpec((1, CONV_TC), lambda c, b, si: (0, c))
    return pl.pallas_call(
        body, name=name, grid=(nc, nb, ns),
        in_specs=[pspec(0), pspec(1), wspec, ospec],
        out_specs=[ospec, ospec, wspec, bspec],
        out_shape=[jax.ShapeDtypeStruct((t, d), MXU_DTYPE)] * 2
        + [jax.ShapeDtypeStruct((CF_KERNEL, d), F32), jax.ShapeDtypeStruct((1, d), F32)],
        scratch_shapes=[pltpu.VMEM((CONV_TS + CONV_PAD, CONV_TC), F32)],
        compiler_params=_params(("parallel", "arbitrary", "arbitrary")),
    )(p, p, w, du)


BIG = ("ffn1_w_gu", "ffn1_w_down", "w_in", "w_ret_o", "w_sc_o", "w_cf_o", "w_o", "ffn2_w_gu", "ffn2_w_down")
MODE = {"ffn1_w_gu": "col", "ffn1_w_down": "row", "w_in": "col", "w_ret_o": "row", "w_sc_o": "row",
        "w_cf_o": "row", "w_o": "row", "ffn2_w_gu": "col", "ffn2_w_down": "row"}


def _rope_tables(positions):
    half = RET_QK_DIM // 2
    inv_freq = ROPE_BASE ** (-jnp.arange(half, dtype=F32) / half)
    ang = positions.astype(F32)[..., None] * inv_freq
    cos, sin = jnp.cos(ang), jnp.sin(ang)
    nb, s = positions.shape
    cos2 = jnp.concatenate([cos, cos], axis=-1).reshape(nb * s, RET_QK_DIM)
    sin2 = jnp.concatenate([-sin, sin], axis=-1).reshape(nb * s, RET_QK_DIM)
    return cos2, sin2


def _log_gamma():
    lg = jnp.log(1.0 - 2.0 ** (-5.0 - jnp.arange(RET_HEADS, dtype=F32)))
    return jnp.broadcast_to(lg[:, None, None], (RET_HEADS, 1, RET_QK_DIM))


def local_step(x, positions, target, w, small):
    nb, s, d = x.shape
    t = nb * s
    cos2, sin2 = _rope_tables(positions)
    log_g = _log_gamma()
    xs = x.reshape(t, d)
    blk_gate, blk_scb, blk_cfa, blk_merge = 2, 3, 6, 8
    gate_blk_dv = (blk_gate * d) // RET_V_DIM

    def g_row(l, i):
        return small["norm_g"][l, i][None, :]

    saved = []
    for l in range(DEPTH):
        sv = {}
        for tag, i0 in (("ffn1", 0), ("ffn2", 4)):
            if tag == "ffn2":
                sv["mx_x"] = xs
                h = rms_fwd("mx_rms", xs, g_row(l, 2))
                p = mm_fwd("mx_in", h, w["w_in"], l, "col", MXU_DTYPE)
                o = retention_fwd("ret_fwd", p, cos2, sin2, log_g, nb, s)
                ya_in = head_gate_fwd("ret_gate", o, p, gate_blk_dv)
                yb_in, cz = short_conv_fwd("sc_fwd", p, blk_scb, small["sc_conv_w"][l], nb)
                u1 = conformer_conv_fwd("cf_fwd", p, blk_cfa, small["cf_dw_w"][l], small["cf_dw_b"][l][None, :], nb)
                yc_in = ln_silu_fwd("cf_ln", u1, small["cf_ln_g"][l][None, :], small["cf_ln_b"][l][None, :])
                ya = mm_fwd("mx_ya", ya_in, w["w_ret_o"], l, "row", F32)
                yb = mm_fwd("mx_yb", yb_in, w["w_sc_o"], l, "row", F32)
                yc = mm_fwd("mx_yc", yc_in, w["w_cf_o"], l, "row", F32)
                mg = merge_fwd("mx_merge", p, blk_merge, ya, yb, yc)
                m = mm_fwd("mx_o", mg, w["w_o"], l, "row", F32)
                xs = post_fwd("mx_post", xs, m, g_row(l, 3), 1.0)
                sv.update(mx_h=h, p=p, o=o, ya_in=ya_in, yb_in=yb_in, cz=cz, u1=u1, yc_in=yc_in, ya=ya, yb=yb, yc=yc,
                          mg=mg, m=m)
            sv[tag + "_x"] = xs
            h = rms_fwd("ffn_rms", xs, g_row(l, i0))
            gu = mm_fwd("ffn_gu", h, w[tag + "_w_gu"], l, "col", MXU_DTYPE)
            a = swiglu_fwd("ffn_act", gu)
            y = mm_fwd("ffn_down", a, w[tag + "_w_down"], l, "row", F32)
            xs = post_fwd("ffn_post", xs, y, g_row(l, i0 + 1), 0.5)
            sv.update({tag + "_h": h, tag + "_gu": gu, tag + "_a": a, tag + "_y": y})
        saved.append(sv)

    dxs, loss = loss_head("loss", xs, target.reshape(t, d))

    gbig = {n: None for n in BIG}
    dnorm = [[None] * 6 for _ in range(DEPTH)]
    gsmall = {n: [None] * DEPTH for n in ("sc_conv_w", "cf_dw_w", "cf_dw_b", "cf_ln_g", "cf_ln_b")}

    def dw(name, a, dy, l, wname):
        gbig[wname] = mm_dw(name, a, dy, l, MODE[wname], w[wname].shape, gbig[wname])

    for l in reversed(range(DEPTH)):
        sv = saved[l]
        for tag, i0 in (("ffn2", 4), ("ffn1", 0)):
            dy, dnorm[l][i0 + 1] = post_bwd("ffn_post_bwd", sv[tag + "_y"], g_row(l, i0 + 1), dxs, 0.5)
            da = mm_dx("ffn_down_dx", dy, w[tag + "_w_down"], l, "row", MXU_DTYPE)
            dw("ffn_down_dw", sv[tag + "_a"], dy, l, tag + "_w_down")
            dgu = swiglu_bwd("ffn_act_bwd", sv[tag + "_gu"], da)
            dh = mm_dx("ffn_gu_dx", dgu, w[tag + "_w_gu"], l, "col", F32)
            dw("ffn_gu_dw", sv[tag + "_h"], dgu, l, tag + "_w_gu")
            dxs, dnorm[l][i0] = rms_bwd("ffn_rms_bwd", sv[tag + "_x"], g_row(l, i0), dh, dxs)
            if tag == "ffn2":
                p = sv["p"]
                dm, dnorm[l][3] = post_bwd("mx_post_bwd", sv["m"], g_row(l, 3), dxs, 1.0)
                dmg = mm_dx("mx_o_dx", dm, w["w_o"], l, "row", MXU_DTYPE)
                dw("mx_o_dw", sv["mg"], dm, l, "w_o")
                dg0, dg1, dg2, dya, dyb, dyc = merge_bwd("mx_merge_bwd", p, blk_merge, sv["ya"], sv["yb"], sv["yc"], dmg)
                dya_in = mm_dx("mx_ya_dx", dya, w["w_ret_o"], l, "row", MXU_DTYPE)
                dw("mx_ya_dw", sv["ya_in"], dya, l, "w_ret_o")
                dyb_in = mm_dx("mx_yb_dx", dyb, w["w_sc_o"], l, "row", MXU_DTYPE)
                dw("mx_yb_dw", sv["yb_in"], dyb, l, "w_sc_o")
                dyc_in = mm_dx("mx_yc_dx", dyc, w["w_cf_o"], l, "row", F32)
                dw("mx_yc_dw", sv["yc_in"], dyc, l, "w_cf_o")
                do, dgret = head_gate_bwd("ret_gate_bwd", sv["o"], p, gate_blk_dv, dya_in)
                dq, dk, dv = retention_bwd("ret_bwd", p, cos2, sin2, log_g, do, nb, s)
                dscb, dscc, dscx, gsmall["sc_conv_w"][l] = short_conv_bwd(
                    "sc_bwd", p, blk_scb, small["sc_conv_w"][l], sv["cz"], dyb_in, nb)
                du1, dlg, dlb = ln_silu_bwd("cf_ln_bwd", sv["u1"], small["cf_ln_g"][l][None, :],
                                            small["cf_ln_b"][l][None, :], dyc_in)
                gsmall["cf_ln_g"][l], gsmall["cf_ln_b"][l] = dlg[0], dlb[0]
                dcfa, dcfb, gsmall["cf_dw_w"][l], dbias = conformer_conv_bwd(
                    "cf_bwd", p, blk_cfa, small["cf_dw_w"][l], du1, nb)
                gsmall["cf_dw_b"][l] = dbias[0]
                dp = jnp.concatenate([dq, dk, dv, dgret, dscb, dscc, dscx, dcfa, dcfb, dg0, dg1, dg2], axis=1)
                dh = mm_dx("mx_in_dx", dp, w["w_in"], l, "col", F32)
                dw("mx_in_dw", sv["mx_h"], dp, l, "w_in")
                dxs, dnorm[l][2] = rms_bwd("mx_rms_bwd", sv["mx_x"], g_row(l, 2), dh, dxs)

    gs = {n: jnp.stack(v) for n, v in gsmall.items()}
    gs["norm_g"] = jnp.stack([jnp.concatenate(r, axis=0) for r in dnorm])
    return loss, dxs.reshape(nb, s, d), gbig, gs


ANY = pl.BlockSpec(memory_space=pl.ANY)
VMEM_WHOLE = pl.BlockSpec(memory_space=pltpu.VMEM)


def _other_chips(x, y):
    return [(1 - x, y), (x, 1 - y), (1 - x, 1 - y)]


def _remote(src, dst, send_sem, recv_sem, to):
    return pltpu.make_async_remote_copy(src_ref=src, dst_ref=dst, send_sem=send_sem, recv_sem=recv_sem,
                                        device_id=to, device_id_type=MESH)


def allgather_big(shards):
    n = len(shards)

    def body(*refs):
        ins, outs = refs[:n], refs[n:2 * n]
        ici_send, ici_recv, d2d_send, d2d_recv, loc_sem = refs[2 * n:]
        x, y, c = _axes()
        me = 2 * x + y
        chips = _other_chips(x, y)
        sib = (x, y, 1 - c)

        def block(w, chip, cc):
            rh = ins[w].shape[1] // 2
            return outs[w].at[chip, :, pl.ds(cc * rh, rh), :]

        local = [pltpu.make_async_copy(ins[w], outs[w].at[me], loc_sem.at[w]) for w in range(n)]
        for cp in local:
            cp.start()
        started = []
        for w in range(n):
            rh = ins[w].shape[1] // 2
            for k, (px, py) in enumerate(chips):
                cp = _remote(ins[w].at[:, pl.ds(c * rh, rh), :], block(w, me, c), ici_send.at[3 * w + k],
                             ici_recv.at[3 * w + k], (px, py, c))
                cp.start()
                started.append(cp)
        for w in range(n):
            for k, (px, py) in enumerate(chips):
                got = block(w, 2 * px + py, c)
                _remote(got, got, ici_send.at[3 * w + k], ici_recv.at[3 * w + k], (px, py, c)).wait_recv()
                cp = _remote(got, got, d2d_send.at[3 * w + k], d2d_recv.at[3 * w + k], sib)
                cp.start()
                started.append(cp)
        for w in range(n):
            for k, (px, py) in enumerate(chips):
                blk = block(w, 2 * px + py, 1 - c)
                _remote(blk, blk, d2d_send.at[3 * w + k], d2d_recv.at[3 * w + k], sib).wait_recv()
        for cp in started:
            cp.wait_send()
        for cp in local:
            cp.wait()

    sems = [pltpu.SemaphoreType.DMA((3 * n,))] * 4 + [pltpu.SemaphoreType.DMA((n,))]
    return pl.pallas_call(
        body, name="allgather_big", in_specs=[ANY] * n, out_specs=[ANY] * n,
        out_shape=[jax.ShapeDtypeStruct((N_CHIP,) + s.shape, s.dtype) for s in shards], scratch_shapes=sems,
    )(*shards)


def exchange_sibling_halves(grads):
    n = len(grads)

    def body(*refs):
        ins, outs = refs[:n], refs[n:2 * n]
        send, recv = refs[2 * n:]
        x, y, c = _axes()
        cps = []
        for w in range(n):
            rh = ins[w].shape[2] // 2
            cp = _remote(ins[w].at[:, :, pl.ds((1 - c) * rh, rh), :], outs[w], send.at[w], recv.at[w], (x, y, 1 - c))
            cp.start()
            cps.append(cp)
        for cp in cps:
            cp.wait()

    return pl.pallas_call(
        body, name="exchange_sibling_halves", in_specs=[ANY] * n, out_specs=[ANY] * n,
        out_shape=[jax.ShapeDtypeStruct(g.shape[:2] + (g.shape[2] // 2, g.shape[3]), g.dtype) for g in grads],
        scratch_shapes=[pltpu.SemaphoreType.DMA((n,))] * 2,
    )(*grads)


def add_halves(g4, land, core):
    nq, nl, r, c = g4.shape
    rh = r // 2

    def body(core_ref, a_ref, b_ref, o_ref):
        o_ref[...] = (a_ref[...].astype(F32) + b_ref[...].astype(F32)).astype(o_ref.dtype)

    return pl.pallas_call(
        body, name="add_halves",
        grid_spec=pltpu.PrefetchScalarGridSpec(
            num_scalar_prefetch=1, grid=(nq * nl,),
            in_specs=[pl.BlockSpec((None, rh, c), lambda i, core_ref: (i, core_ref[0], 0)),
                      pl.BlockSpec((None, rh, c), lambda i, core_ref: (i, 0, 0))],
            out_specs=pl.BlockSpec((None, rh, c), lambda i, core_ref: (i, 0, 0))),
        out_shape=jax.ShapeDtypeStruct((nq * nl, rh, c), g4.dtype),
        compiler_params=_params(("parallel",)),
    )(core, g4.reshape(nq * nl, r, c), land.reshape(nq * nl, rh, c)).reshape(nq, nl, rh, c)


def exchange_chip_partials(parts):
    n = len(parts)

    def body(*refs):
        ins, outs = refs[:n], refs[n:2 * n]
        send, recv, loc_sem = refs[2 * n:]
        x, y, c = _axes()
        me = 2 * x + y
        chips = _other_chips(x, y)
        local = [pltpu.make_async_copy(ins[w].at[me], outs[w].at[me], loc_sem.at[w]) for w in range(n)]
        for cp in local:
            cp.start()
        cps = []
        for w in range(n):
            for k, (px, py) in enumerate(chips):
                cp = _remote(ins[w].at[2 * px + py], outs[w].at[me], send.at[3 * w + k], recv.at[3 * w + k], (px, py, c))
                cp.start()
                cps.append(cp)
        for w in range(n):
            for k, (px, py) in enumerate(chips):
                got = outs[w].at[2 * px + py]
                _remote(got, got, send.at[3 * w + k], recv.at[3 * w + k], (px, py, c)).wait_recv()
        for cp in cps:
            cp.wait_send()
        for cp in local:
            cp.wait()

    return pl.pallas_call(
        body, name="exchange_chip_partials", in_specs=[ANY] * n, out_specs=[ANY] * n,
        out_shape=[jax.ShapeDtypeStruct(p.shape, p.dtype) for p in parts],
        scratch_shapes=[pltpu.SemaphoreType.DMA((3 * n,))] * 2 + [pltpu.SemaphoreType.DMA((n,))],
    )(*parts)


def sum_chip_partials(land, core):
    nq, nl, rh, c = land.shape
    tr = rh // 2

    def body(core_ref, a_ref, o_ref):
        acc = a_ref[0].astype(F32)
        for j in range(1, nq):
            acc = acc + a_ref[j].astype(F32)
        o_ref[...] = acc

    return pl.pallas_call(
        body, name="sum_chip_partials",
        grid_spec=pltpu.PrefetchScalarGridSpec(
            num_scalar_prefetch=1, grid=(nl, rh // tr),
            in_specs=[pl.BlockSpec((nq, None, tr, c), lambda l, i, core_ref: (0, l, i, 0))],
            out_specs=pl.BlockSpec((None, tr, c), lambda l, i, core_ref: (l, core_ref[0] * (rh // tr) + i, 0))),
        out_shape=jax.ShapeDtypeStruct((nl, 2 * rh, c), F32),
        compiler_params=_params(("parallel", "parallel")),
    )(core, land)


def exchange_final_halves(gsum):
    n = len(gsum)

    def body(*refs):
        outs = refs[n:2 * n]
        send, recv = refs[2 * n:]
        x, y, c = _axes()
        cps = []
        for w in range(n):
            rh = outs[w].shape[1] // 2
            mine = outs[w].at[:, pl.ds(c * rh, rh), :]
            cp = _remote(mine, mine, send.at[w], recv.at[w], (x, y, 1 - c))
            cp.start()
            cps.append(cp)
        for w in range(n):
            rh = outs[w].shape[1] // 2
            theirs = outs[w].at[:, pl.ds((1 - c) * rh, rh), :]
            _remote(theirs, theirs, send.at[w], recv.at[w], (x, y, 1 - c)).wait_recv()
        for cp in cps:
            cp.wait_send()

    return pl.pallas_call(
        body, name="exchange_final_halves", in_specs=[ANY] * n, out_specs=[ANY] * n,
        out_shape=[jax.ShapeDtypeStruct(g.shape, g.dtype) for g in gsum],
        input_output_aliases={i: i for i in range(n)},
        scratch_shapes=[pltpu.SemaphoreType.DMA((n,))] * 2,
    )(*gsum)


def allgather_small(pk):
    def body(in_ref, out_ref, send, recv):
        x, y, c = _axes()
        me = 2 * x + y
        chips = _other_chips(x, y)
        out_ref[pl.ds(me, 1)] = in_ref[...][None]
        cps = []
        for k, (px, py) in enumerate(chips):
            cp = _remote(in_ref, out_ref.at[me], send.at[k], recv.at[k], (px, py, c))
            cp.start()
            cps.append(cp)
        for k, (px, py) in enumerate(chips):
            got = out_ref.at[2 * px + py]
            _remote(got, got, send.at[k], recv.at[k], (px, py, c)).wait_recv()
        for cp in cps:
            cp.wait_send()

    return pl.pallas_call(
        body, name="allgather_small", in_specs=[VMEM_WHOLE], out_specs=VMEM_WHOLE,
        out_shape=jax.ShapeDtypeStruct((N_CHIP,) + pk.shape, pk.dtype),
        scratch_shapes=[pltpu.SemaphoreType.DMA((3,))] * 2,
    )(pk)


def allreduce_small(g):
    ndev = 8

    def body(in_ref, out_ref, slots, send, recv):
        x, y, c = _axes()
        me = 4 * x + 2 * y + c
        slots[pl.ds(me, 1)] = in_ref[...][None]
        peers = []
        for mask in range(1, ndev):
            px = 1 - x if mask & 4 else x
            py = 1 - y if mask & 2 else y
            pc = 1 - c if mask & 1 else c
            peers.append((px, py, pc))
        cps = []
        for k, peer in enumerate(peers):
            cp = _remote(in_ref, slots.at[me], send.at[k], recv.at[k], peer)
            cp.start()
            cps.append(cp)
        for k, (px, py, pc) in enumerate(peers):
            got = slots.at[4 * px + 2 * py + pc]
            _remote(got, got, send.at[k], recv.at[k], (px, py, pc)).wait_recv()
        for cp in cps:
            cp.wait_send()
        acc = slots[0]
        for d in range(1, ndev):
            acc = acc + slots[d]
        out_ref[...] = acc

    return pl.pallas_call(
        body, name="allreduce_small", in_specs=[VMEM_WHOLE], out_specs=VMEM_WHOLE,
        out_shape=jax.ShapeDtypeStruct(g.shape, g.dtype),
        scratch_shapes=[pltpu.VMEM((ndev,) + g.shape, g.dtype), pltpu.SemaphoreType.DMA((ndev - 1,)),
                        pltpu.SemaphoreType.DMA((ndev - 1,))],
    )(g)


def adamw(w, g, m, v):
    shape = w.shape
    cols = shape[-1]
    rows = int(np.prod(shape[:-1]))
    tr = rows
    for cand in (256, 128):
        if rows % cand == 0 and cand * cols * 4 <= 2 * 1024 * 1024:
            tr = cand
            break
    c1 = 1.0 - ADAM_B1 ** ADAM_STEP
    c2 = 1.0 - ADAM_B2 ** ADAM_STEP

    def body(w_ref, g_ref, m_ref, v_ref, d_ref, nm_ref, nv_ref):
        gv = g_ref[...]
        nm = ADAM_B1 * m_ref[...] + (1.0 - ADAM_B1) * gv
        nv = ADAM_B2 * v_ref[...] + (1.0 - ADAM_B2) * jnp.square(gv)
        d_ref[...] = -ADAM_LR * ((nm / c1) / (jnp.sqrt(nv / c2) + ADAM_EPS) + ADAM_WD * w_ref[...])
        nm_ref[...] = nm
        nv_ref[...] = nv

    spec = pl.BlockSpec((tr, cols), lambda i: (i, 0))
    res = pl.pallas_call(
        body, name="adamw", grid=(rows // tr,), in_specs=[spec] * 4, out_specs=[spec] * 3,
        out_shape=[jax.ShapeDtypeStruct((rows, cols), F32)] * 3, compiler_params=_params(("parallel",)),
    )(*[a.reshape(rows, cols) for a in (w, g, m, v)])
    return [r.reshape(shape) for r in res]


WEIGHTS = ("norm_g", "ffn1_w_gu", "ffn1_w_down", "w_in", "w_ret_o", "sc_conv_w", "w_sc_o", "cf_dw_w", "cf_dw_b",
           "cf_ln_g", "cf_ln_b", "w_cf_o", "w_o", "ffn2_w_gu", "ffn2_w_down")
SHARDED_SMALL = ("norm_g", "sc_conv_w", "cf_dw_w")
REPLICATED_SMALL = ("cf_dw_b", "cf_ln_g", "cf_ln_b")
SUBLANES = 8


def _pack_rows(parts):
    padded, offs, at = [], [], 0
    for p in parts:
        r = -(-p.shape[0] // SUBLANES) * SUBLANES
        padded.append(jnp.pad(p, ((0, r - p.shape[0]), (0, 0))))
        offs.append(at)
        at += r
    return jnp.concatenate(padded, axis=0), offs


def kernel(x, positions, norm_g, ffn1_w_gu, ffn1_w_down, w_in, w_ret_o, sc_conv_w, w_sc_o, cf_dw_w, cf_dw_b, cf_ln_g, cf_ln_b, w_cf_o, w_o, ffn2_w_gu, ffn2_w_down, loss_target, m_norm_g, m_ffn1_w_gu, m_ffn1_w_down, m_w_in, m_w_ret_o, m_sc_conv_w, m_w_sc_o, m_cf_dw_w, m_cf_dw_b, m_cf_ln_g, m_cf_ln_b, m_w_cf_o, m_w_o, m_ffn2_w_gu, m_ffn2_w_down, v_norm_g, v_ffn1_w_gu, v_ffn1_w_down, v_w_in, v_w_ret_o, v_sc_conv_w, v_w_sc_o, v_cf_dw_w, v_cf_dw_b, v_cf_ln_g, v_cf_ln_b, v_w_cf_o, v_w_o, v_ffn2_w_gu, v_ffn2_w_down):
    wts = dict(zip(WEIGHTS, (norm_g, ffn1_w_gu, ffn1_w_down, w_in, w_ret_o, sc_conv_w, w_sc_o, cf_dw_w, cf_dw_b,
                             cf_ln_g, cf_ln_b, w_cf_o, w_o, ffn2_w_gu, ffn2_w_down)))
    mom = dict(zip(WEIGHTS, (m_norm_g, m_ffn1_w_gu, m_ffn1_w_down, m_w_in, m_w_ret_o, m_sc_conv_w, m_w_sc_o,
                             m_cf_dw_w, m_cf_dw_b, m_cf_ln_g, m_cf_ln_b, m_w_cf_o, m_w_o, m_ffn2_w_gu, m_ffn2_w_down)))
    var = dict(zip(WEIGHTS, (v_norm_g, v_ffn1_w_gu, v_ffn1_w_down, v_w_in, v_w_ret_o, v_sc_conv_w, v_w_sc_o,
                             v_cf_dw_w, v_cf_dw_b, v_cf_ln_g, v_cf_ln_b, v_w_cf_o, v_w_o, v_ffn2_w_gu, v_ffn2_w_down)))
    nl = norm_g.shape[0]
    dq = norm_g.shape[-1]
    d = N_CHIP * dq
    chip = 2 * lax.axis_index("x") + lax.axis_index("y")
    core = lax.axis_index("c").astype(jnp.int32).reshape(1)

    w4 = dict(zip(BIG, allgather_big([wts[n].astype(MXU_DTYPE) for n in BIG])))
    pk, offs = _pack_rows([wts[n].reshape(-1, dq) for n in SHARDED_SMALL])
    gk = allgather_small(pk)
    gk = gk.transpose(1, 0, 2).reshape(pk.shape[0], d)
    small = {n: wts[n] for n in REPLICATED_SMALL}
    for n, o in zip(SHARDED_SMALL, offs):
        rows = wts[n].shape[0] * wts[n].shape[1]
        small[n] = gk[o:o + rows].reshape(wts[n].shape[:2] + (d,))

    loss, grad_x, gbig, gs = local_step(x, positions, loss_target, w4, small)

    glist = [gbig[n] for n in BIG]
    land = exchange_sibling_halves(glist)
    parts = [add_halves(g, ld, core) for g, ld in zip(glist, land)]
    land = exchange_chip_partials(parts)
    gsum = exchange_final_halves([sum_chip_partials(ld, core) for ld in land])
    grads = dict(zip(BIG, gsum))

    names = SHARDED_SMALL + REPLICATED_SMALL
    pg, offs = _pack_rows([gs[n].reshape(-1, d) for n in names])
    tot = allreduce_small(pg)
    for n, o in zip(names, offs):
        rows = int(np.prod(gs[n].shape[:-1]))
        full = tot[o:o + rows]
        if n in SHARDED_SMALL:
            full = lax.dynamic_slice_in_dim(full, chip * dq, dq, axis=1)
        grads[n] = full.reshape(wts[n].shape)

    delta, new_m, new_v = {}, {}, {}
    for n in WEIGHTS:
        delta[n], new_m[n], new_v[n] = adamw(wts[n], grads[n], mom[n], var[n])
    loss_all = lax.psum(loss[0, 0], ("x", "y", "c"))
    return (loss_all, grad_x, *[grads[n] for n in WEIGHTS], *[delta[n] for n in WEIGHTS],
            *[new_m[n] for n in WEIGHTS], *[new_v[n] for n in WEIGHTS])
```

```python
import functools

import jax
import jax.numpy as jnp
import numpy as np
from jax import lax
from jax.experimental import pallas as pl
from jax.experimental.pallas import tpu as pltpu

F32 = jnp.float32
BF16 = jnp.bfloat16
MXU_DTYPE = BF16
VMEM_LIMIT_BYTES = 56 * 1024 * 1024
MESH = pl.DeviceIdType.MESH

N_CHIP = 4
CHUNK = 64
RET_HEADS = 4
RET_QK_DIM = 128
RET_V_DIM = 256
SC_KERNEL = 3
CF_KERNEL = 31
ROPE_BASE = 10000.0
NORM_EPS = 1e-6
LN_EPS = 1e-5
ADAM_LR = 0.001
ADAM_B1 = 0.9
ADAM_B2 = 0.999
ADAM_EPS = 1e-08
ADAM_WD = 0.01
ADAM_STEP = 10

CONV_PAD = 32
CONV_TS = 128
CONV_TC = 512
RET_TQ = 512
MM_TM = 1024
MM_TN = 1536
MM_K1 = 1024


def _params(sem):
    return pltpu.CompilerParams(dimension_semantics=sem, vmem_limit_bytes=VMEM_LIMIT_BYTES)


def _axes():
    return lax.axis_index("x"), lax.axis_index("y"), lax.axis_index("c")


NN = (((1,), (0,)), ((), ()))
NT = (((1,), (1,)), ((), ()))
TN = (((0,), (0,)), ((), ()))


def _mm(name, a, b, out_shape, out_dtype, grid, a_spec, b_spec, o_spec, dims, acc_shape):
    nk = grid[2]

    def body(a_ref, b_ref, o_ref, *scratch):
        bv = b_ref[...]
        if bv.ndim == 3:
            bv = bv.reshape(-1, bv.shape[-1])
        part = lax.dot_general(a_ref[...], bv, dims, preferred_element_type=F32)

        def put(v):
            o_ref[...] = v.reshape(o_ref.shape).astype(o_ref.dtype)

        if nk == 1:
            put(part)
        else:
            acc = scratch[0]
            k = pl.program_id(2)

            @pl.when(k == 0)
            def _():
                acc[...] = part

            @pl.when(k > 0)
            def _():
                acc[...] += part

            @pl.when(k == nk - 1)
            def _():
                put(acc[...])

    scratch = [pltpu.VMEM(acc_shape, F32)] if nk > 1 else []
    return pl.pallas_call(
        body, name=name, grid=grid, in_specs=[a_spec, b_spec], out_specs=o_spec,
        out_shape=jax.ShapeDtypeStruct(out_shape, out_dtype), scratch_shapes=scratch,
        compiler_params=_params(("parallel", "parallel", "arbitrary")),
    )(a, b)


def _tile(n, target):
    best = None
    for t in range(128, min(n, target) + 1, 128):
        if n % t == 0:
            best = t
    assert best is not None, (n, target)
    return best


def mm_fwd(name, a, w4, mode, out_dtype):
    t = a.shape[0]
    _, r, c = w4.shape
    tm = min(t, MM_TM)
    if mode == "col":
        tn = _tile(c, MM_TN)
        npj = c // tn
        grid = (t // tm, N_CHIP * npj, 1)
        a_spec = pl.BlockSpec((tm, r), lambda i, j, k: (i, 0))
        b_spec = pl.BlockSpec((None, r, tn), lambda i, j, k: (j // npj, 0, j % npj))
        o_spec = pl.BlockSpec((tm, tn), lambda i, j, k: (i, j))
        return _mm(name, a, w4, (t, N_CHIP * c), out_dtype, grid, a_spec, b_spec, o_spec, NN, (tm, tn))
    if N_CHIP * r <= MM_K1:
        grid = (t // tm, 1, 1)
        a_spec = pl.BlockSpec((tm, N_CHIP * r), lambda i, j, k: (i, 0))
        b_spec = pl.BlockSpec((N_CHIP, r, c), lambda i, j, k: (0, 0, 0))
        o_spec = pl.BlockSpec((tm, c), lambda i, j, k: (i, 0))
        return _mm(name, a, w4, (t, c), out_dtype, grid, a_spec, b_spec, o_spec, NN, (tm, c))
    grid = (t // tm, 1, N_CHIP)
    a_spec = pl.BlockSpec((tm, r), lambda i, j, k: (i, k))
    b_spec = pl.BlockSpec((None, r, c), lambda i, j, k: (k, 0, 0))
    o_spec = pl.BlockSpec((tm, c), lambda i, j, k: (i, 0))
    return _mm(name, a, w4, (t, c), out_dtype, grid, a_spec, b_spec, o_spec, NN, (tm, c))


def mm_dx(name, dy, w4, mode, out_dtype):
    t = dy.shape[0]
    _, r, c = w4.shape
    tm = min(t, MM_TM)
    if mode == "col":
        tn = _tile(c, MM_TN)
        npj = c // tn
        grid = (t // tm, 1, N_CHIP * npj)
        a_spec = pl.BlockSpec((tm, tn), lambda i, j, k: (i, k))
        b_spec = pl.BlockSpec((None, r, tn), lambda i, j, k: (k // npj, 0, k % npj))
        o_spec = pl.BlockSpec((tm, r), lambda i, j, k: (i, 0))
        return _mm(name, dy, w4, (t, r), out_dtype, grid, a_spec, b_spec, o_spec, NT, (tm, r))
    if N_CHIP * r <= MM_K1:
        grid = (t // tm, 1, 1)
        a_spec = pl.BlockSpec((tm, c), lambda i, j, k: (i, 0))
        b_spec = pl.BlockSpec((N_CHIP, r, c), lambda i, j, k: (0, 0, 0))
        o_spec = pl.BlockSpec((tm, N_CHIP * r), lambda i, j, k: (i, 0))
        return _mm(name, dy, w4, (t, N_CHIP * r), out_dtype, grid, a_spec, b_spec, o_spec, NT, (tm, N_CHIP * r))
    grid = (t // tm, N_CHIP, 1)
    a_spec = pl.BlockSpec((tm, c), lambda i, j, k: (i, 0))
    b_spec = pl.BlockSpec((None, r, c), lambda i, j, k: (j, 0, 0))
    o_spec = pl.BlockSpec((tm, r), lambda i, j, k: (i, j))
    return _mm(name, dy, w4, (t, N_CHIP * r), out_dtype, grid, a_spec, b_spec, o_spec, NT, (tm, r))


def mm_dw(name, a, dy, mode, shape3):
    t = a.shape[0]
    _, r, c = shape3
    if mode == "col":
        tt = min(t, MM_TM)
        tn = _tile(c, MM_TN)
        npj = c // tn
        grid = (1, N_CHIP * npj, t // tt)
        a_spec = pl.BlockSpec((tt, r), lambda i, j, k: (k, 0))
        b_spec = pl.BlockSpec((tt, tn), lambda i, j, k: (k, j))
        o_spec = pl.BlockSpec((None, r, tn), lambda i, j, k: (j // npj, 0, j % npj))
        return _mm(name, a, dy, shape3, MXU_DTYPE, grid, a_spec, b_spec, o_spec, TN, (r, tn))
    if N_CHIP * r <= MM_K1:
        tt = min(t, 2 * MM_TM)
        grid = (1, 1, t // tt)
        a_spec = pl.BlockSpec((tt, N_CHIP * r), lambda i, j, k: (k, 0))
        b_spec = pl.BlockSpec((tt, c), lambda i, j, k: (k, 0))
        o_spec = pl.BlockSpec((N_CHIP, r, c), lambda i, j, k: (0, 0, 0))
        return _mm(name, a, dy, shape3, MXU_DTYPE, grid, a_spec, b_spec, o_spec, TN, (N_CHIP * r, c))
    tt = min(t, MM_TM)
    grid = (N_CHIP, 1, t // tt)
    a_spec = pl.BlockSpec((tt, r), lambda i, j, k: (k, i))
    b_spec = pl.BlockSpec((tt, c), lambda i, j, k: (k, 0))
    o_spec = pl.BlockSpec((None, r, c), lambda i, j, k: (i, 0, 0))
    return _mm(name, a, dy, shape3, MXU_DTYPE, grid, a_spec, b_spec, o_spec, TN, (r, c))


def _rowwise(name, fn, rows, pars, outs, accs=(), tm=256, ncol=1):
    t = rows[0][0].shape[0]
    nrow, npar, nout = len(rows), len(pars), len(outs)

    def body(*refs):
        vals = [r[...] for r in refs[:nrow + npar]]
        res = fn(*vals)
        out_refs = refs[nrow + npar:nrow + npar + nout]
        acc_refs = refs[nrow + npar + nout:]
        for o, v in zip(out_refs, res[:nout]):
            o[...] = v.astype(o.dtype)
        i = pl.program_id(1)
        for a, v in zip(acc_refs, res[nout:]):
            @pl.when(i == 0)
            def _(a=a, v=v):
                a[...] = v.astype(F32)

            @pl.when(i > 0)
            def _(a=a, v=v):
                a[...] += v.astype(F32)

    in_specs = [pl.BlockSpec((tm, w), functools.partial(lambda j, i, b: (i, b + j), b=b)) for _, w, b in rows]
    for arr, w in pars:
        if w is None:
            in_specs.append(pl.BlockSpec(arr.shape, lambda j, i: (0, 0)))
        else:
            in_specs.append(pl.BlockSpec((1, w), lambda j, i: (0, j)))
    out_specs = [pl.BlockSpec((tm, w), lambda j, i: (i, j)) for _, w, _ in outs]
    out_specs += [pl.BlockSpec((1, w), lambda j, i: (0, j)) for _, w in accs]
    out_shape = [jax.ShapeDtypeStruct((t, tw), dt) for tw, _, dt in outs]
    out_shape += [jax.ShapeDtypeStruct((1, tw), F32) for tw, _ in accs]
    res = pl.pallas_call(
        body, name=name, grid=(ncol, t // tm), in_specs=in_specs, out_specs=out_specs, out_shape=out_shape,
        compiler_params=_params(("parallel", "arbitrary" if accs else "parallel")),
    )(*[r[0] for r in rows], *[p[0] for p in pars])
    return res


def _rms(x, g):
    xf = x.astype(F32)
    return xf * lax.rsqrt(jnp.mean(xf * xf, axis=-1, keepdims=True) + NORM_EPS) * g


def _silu(x):
    return x * jax.nn.sigmoid(x)


def rms_fwd(name, x, g):
    d = x.shape[1]
    return _rowwise(name, lambda x, g: (_rms(x, g),), [(x, d, 0)], [(g, None)], [(d, d, MXU_DTYPE)], tm=512)[0]


def rms_bwd(name, x, g, dh, dres):
    d = x.shape[1]

    def fn(x, dh, dres, g):
        _, vjp = jax.vjp(_rms, x, g)
        dx, dg = vjp(dh.astype(F32))
        return dres + dx, dg

    return _rowwise(name, fn, [(x, d, 0), (dh, d, 0), (dres, d, 0)], [(g, None)], [(d, d, F32)], [(d, d)], tm=256)


def post_fwd(name, x, y, g, scale):
    d = x.shape[1]
    return _rowwise(name, lambda x, y, g: (x + scale * _rms(y, g),), [(x, d, 0), (y, d, 0)], [(g, None)],
                    [(d, d, F32)], tm=512)[0]


def post_bwd(name, y, g, dx, scale):
    d = y.shape[1]

    def fn(y, dx, g):
        _, vjp = jax.vjp(lambda y, g: scale * _rms(y, g), y, g)
        return vjp(dx)

    return _rowwise(name, fn, [(y, d, 0), (dx, d, 0)], [(g, None)], [(d, d, MXU_DTYPE)], [(d, d)], tm=256)


def _swiglu(gu):
    f = gu.shape[1] // 2
    return _silu(gu[:, :f].astype(F32)) * gu[:, f:].astype(F32)


def swiglu_fwd(name, gu):
    w = gu.shape[1]
    return _rowwise(name, lambda gu: (_swiglu(gu),), [(gu, w, 0)], [], [(w // 2, w // 2, MXU_DTYPE)], tm=256)[0]


def swiglu_bwd(name, gu, da):
    w = gu.shape[1]
    f = w // 2

    def fn(gu, da):
        gate, up = gu[:, :f].astype(F32), gu[:, f:].astype(F32)
        _, vjp = jax.vjp(lambda a, b: _silu(a) * b, gate, up)
        dgate, dup = vjp(da.astype(F32))
        return (jnp.concatenate([dgate, dup], axis=-1),)

    return _rowwise(name, fn, [(gu, w, 0), (da, f, 0)], [], [(w, w, MXU_DTYPE)], tm=128)[0]


def _head_gate(o, g):
    mu = jnp.mean(o, axis=-1, keepdims=True)
    var = jnp.mean(jnp.square(o - mu), axis=-1, keepdims=True)
    return _silu(g.astype(F32)) * ((o - mu) * lax.rsqrt(var + LN_EPS))


def head_gate_fwd(name, o, p, gate_blk):
    dv = RET_V_DIM
    return _rowwise(name, lambda o, g: (_head_gate(o, g),), [(o, dv, 0), (p, dv, gate_blk)], [],
                    [(RET_HEADS * dv, dv, MXU_DTYPE)], tm=512, ncol=RET_HEADS)[0]


def head_gate_bwd(name, o, p, gate_blk, da):
    dv = RET_V_DIM

    def fn(o, g, da):
        _, vjp = jax.vjp(_head_gate, o, g.astype(F32))
        return vjp(da.astype(F32))

    w = RET_HEADS * dv
    return _rowwise(name, fn, [(o, dv, 0), (p, dv, gate_blk), (da, dv, 0)], [],
                    [(w, dv, MXU_DTYPE), (w, dv, MXU_DTYPE)], tm=512, ncol=RET_HEADS)


def _ln_silu(u, g, b):
    mu = jnp.mean(u, axis=-1, keepdims=True)
    var = jnp.mean(jnp.square(u - mu), axis=-1, keepdims=True)
    return _silu((u - mu) * lax.rsqrt(var + LN_EPS) * g + b)


def ln_silu_fwd(name, u, g, b):
    d = u.shape[1]
    return _rowwise(name, lambda u, g, b: (_ln_silu(u, g, b),), [(u, d, 0)], [(g, None), (b, None)],
                    [(d, d, MXU_DTYPE)], tm=512)[0]


def ln_silu_bwd(name, u, g, b, dc):
    d = u.shape[1]

    def fn(u, dc, g, b):
        _, vjp = jax.vjp(_ln_silu, u, g, b)
        return vjp(dc.astype(F32))

    return _rowwise(name, fn, [(u, d, 0), (dc, d, 0)], [(g, None), (b, None)], [(d, d, F32)], [(d, d), (d, d)],
                    tm=256)


def _merge(g0, g1, g2, ya, yb, yc):
    s = jax.nn.sigmoid
    return s(g0.astype(F32)) * ya + s(g1.astype(F32)) * yb + s(g2.astype(F32)) * yc


def merge_fwd(name, p, blk, ya, yb, yc):
    d = ya.shape[1]
    rows = [(p, d, blk), (p, d, blk + 1), (p, d, blk + 2), (ya, d, 0), (yb, d, 0), (yc, d, 0)]
    return _rowwise(name, lambda *v: (_merge(*v),), rows, [], [(d, d, MXU_DTYPE)], tm=256)[0]


def merge_bwd(name, p, blk, ya, yb, yc, dmg):
    d = ya.shape[1]

    def fn(g0, g1, g2, ya, yb, yc, dmg):
        _, vjp = jax.vjp(_merge, g0.astype(F32), g1.astype(F32), g2.astype(F32), ya, yb, yc)
        return vjp(dmg.astype(F32))

    rows = [(p, d, blk), (p, d, blk + 1), (p, d, blk + 2), (ya, d, 0), (yb, d, 0), (yc, d, 0), (dmg, d, 0)]
    return _rowwise(name, fn, rows, [], [(d, d, MXU_DTYPE)] * 6, tm=256)


def loss_head(name, y, target):
    t, d = y.shape
    tm = 512

    def body(y_ref, t_ref, dy_ref, loss_ref):
        err = y_ref[...] - t_ref[...]
        dy_ref[...] = err * (1.0 / d)
        part = jnp.sum(jnp.sum(err * err, axis=1, keepdims=True), axis=0, keepdims=True) * (0.5 / d)

        @pl.when(pl.program_id(0) == 0)
        def _():
            loss_ref[...] = part

        @pl.when(pl.program_id(0) > 0)
        def _():
            loss_ref[...] += part

    return pl.pallas_call(
        body, name=name, grid=(t // tm,),
        in_specs=[pl.BlockSpec((tm, d), lambda i: (i, 0))] * 2,
        out_specs=[pl.BlockSpec((tm, d), lambda i: (i, 0)), pl.BlockSpec((1, 1), lambda i: (0, 0))],
        out_shape=[jax.ShapeDtypeStruct((t, d), F32), jax.ShapeDtypeStruct((1, 1), F32)],
        compiler_params=_params(("arbitrary",)),
    )(y, target)


def _rot(x, cos2, sin2):
    return x * cos2 + pltpu.roll(x, RET_QK_DIM // 2, 1) * sin2


def _decay_mask(lg, n0, rows, cols):
    n = n0 + lax.broadcasted_iota(jnp.int32, (rows, cols), 0)
    m = lax.broadcasted_iota(jnp.int32, (rows, cols), 1)
    shift = CHUNK.bit_length() - 1
    dist = jnp.abs(n - m).astype(F32)
    return jnp.where((m >> shift) <= (n >> shift), jnp.exp(lg * dist), 0.0)


def _ret_specs(s):
    dk, dv, h = RET_QK_DIM, RET_V_DIM, RET_HEADS
    return [
        pl.BlockSpec((s, dk), lambda b, hh: (b, hh)),
        pl.BlockSpec((s, dk), lambda b, hh: (b, h + hh)),
        pl.BlockSpec((s, dv), lambda b, hh: (b, (2 * h * dk) // dv + hh)),
        pl.BlockSpec((s, dk), lambda b, hh: (b, 0)),
        pl.BlockSpec((s, dk), lambda b, hh: (b, 0)),
        pl.BlockSpec((None, 1, dk), lambda b, hh: (hh, 0, 0)),
    ]


def retention_fwd(name, p, cos2, sin2, log_g, nb, s):
    dk, dv, h = RET_QK_DIM, RET_V_DIM, RET_HEADS

    def body(q_ref, k_ref, v_ref, cos_ref, sin_ref, lg_ref, o_ref, kr_ref):
        lg = lg_ref[0:1, 0:1]
        kr = _rot(k_ref[...].astype(F32), cos_ref[...], sin_ref[...]) * (dk ** -0.5)
        kr_ref[...] = kr.astype(kr_ref.dtype)
        for qi in range(s // RET_TQ):
            n0, kmax = qi * RET_TQ, (qi + 1) * RET_TQ
            rows = pl.ds(n0, RET_TQ)
            qr = _rot(q_ref[rows, :].astype(F32), cos_ref[rows, :], sin_ref[rows, :]).astype(MXU_DTYPE)
            sc = lax.dot_general(qr, kr_ref[0:kmax, :], NT, preferred_element_type=F32)
            pm = (sc * _decay_mask(lg, n0, RET_TQ, kmax)).astype(MXU_DTYPE)
            o_ref[rows, :] = lax.dot_general(pm, v_ref[0:kmax, :], NN, preferred_element_type=F32)

    return pl.pallas_call(
        body, name=name, grid=(nb, h), in_specs=_ret_specs(s),
        out_specs=pl.BlockSpec((s, dv), lambda b, hh: (b, hh)),
        out_shape=jax.ShapeDtypeStruct((nb * s, h * dv), F32),
        scratch_shapes=[pltpu.VMEM((s, dk), MXU_DTYPE)],
        compiler_params=_params(("parallel", "parallel")),
    )(p, p, p, cos2, sin2, log_g)


def retention_bwd(name, p, cos2, sin2, log_g, do, nb, s):
    dk, dv, h = RET_QK_DIM, RET_V_DIM, RET_HEADS

    def body(q_ref, k_ref, v_ref, cos_ref, sin_ref, lg_ref, do_ref, dq_ref, dk_ref, dv_ref, kr_ref, dk_acc, dv_acc):
        lg = lg_ref[0:1, 0:1]
        kr = _rot(k_ref[...].astype(F32), cos_ref[...], sin_ref[...]) * (dk ** -0.5)
        kr_ref[...] = kr.astype(kr_ref.dtype)
        dk_acc[...] = jnp.zeros_like(dk_acc)
        dv_acc[...] = jnp.zeros_like(dv_acc)
        for qi in range(s // RET_TQ):
            n0, kmax = qi * RET_TQ, (qi + 1) * RET_TQ
            rows = pl.ds(n0, RET_TQ)
            cq, sq = cos_ref[rows, :], sin_ref[rows, :]
            qr = _rot(q_ref[rows, :].astype(F32), cq, sq).astype(MXU_DTYPE)
            dob = do_ref[rows, :]
            mask = _decay_mask(lg, n0, RET_TQ, kmax)
            sc = lax.dot_general(qr, kr_ref[0:kmax, :], NT, preferred_element_type=F32)
            pm = (sc * mask).astype(MXU_DTYPE)
            dv_acc[0:kmax, :] += lax.dot_general(pm, dob, TN, preferred_element_type=F32)
            dp = lax.dot_general(dob, v_ref[0:kmax, :], NT, preferred_element_type=F32)
            ds = (dp * mask).astype(MXU_DTYPE)
            dqr = lax.dot_general(ds, kr_ref[0:kmax, :], NN, preferred_element_type=F32)
            dq_ref[rows, :] = _rot(dqr, cq, -sq).astype(dq_ref.dtype)
            dk_acc[0:kmax, :] += lax.dot_general(ds, qr, TN, preferred_element_type=F32)
        dkr = dk_acc[...] * (dk ** -0.5)
        dk_ref[...] = _rot(dkr, cos_ref[...], -sin_ref[...]).astype(dk_ref.dtype)
        dv_ref[...] = dv_acc[...].astype(dv_ref.dtype)

    t = nb * s
    return pl.pallas_call(
        body, name=name, grid=(nb, h),
        in_specs=_ret_specs(s) + [pl.BlockSpec((s, dv), lambda b, hh: (b, hh))],
        out_specs=[pl.BlockSpec((s, dk), lambda b, hh: (b, hh)), pl.BlockSpec((s, dk), lambda b, hh: (b, hh)),
                   pl.BlockSpec((s, dv), lambda b, hh: (b, hh))],
        out_shape=[jax.ShapeDtypeStruct((t, h * dk), MXU_DTYPE), jax.ShapeDtypeStruct((t, h * dk), MXU_DTYPE),
                   jax.ShapeDtypeStruct((t, h * dv), MXU_DTYPE)],
        scratch_shapes=[pltpu.VMEM((s, dk), MXU_DTYPE), pltpu.VMEM((s, dk), F32), pltpu.VMEM((s, dv), F32)],
        compiler_params=_params(("parallel", "parallel")),
    )(p, p, p, cos2, sin2, log_g, do)


def _conv_grid(t, d, nb):
    s = t // nb
    ns, nc = s // CONV_TS, d // CONV_TC
    return s, ns, nc


def _causal_taps(pad_ref, w_ref, k):
    acc = None
    for j in range(k):
        term = w_ref[j:j + 1, :] * pad_ref[pl.ds(CONV_PAD - (k - 1) + j, CONV_TS), :]
        acc = term if acc is None else acc + term
    return acc


def _carry_past(pad_ref, s_idx):
    @pl.when(s_idx == 0)
    def _():
        pad_ref[0:CONV_PAD, :] = jnp.zeros((CONV_PAD, pad_ref.shape[1]), F32)

    @pl.when(s_idx > 0)
    def _():
        pad_ref[0:CONV_PAD, :] = pad_ref[CONV_TS:CONV_TS + CONV_PAD, :]


def _carry_future(pad_ref, s_idx):
    @pl.when(s_idx == 0)
    def _():
        pad_ref[CONV_TS:CONV_TS + CONV_PAD, :] = jnp.zeros((CONV_PAD, pad_ref.shape[1]), F32)

    @pl.when(s_idx > 0)
    def _():
        pad_ref[CONV_TS:CONV_TS + CONV_PAD, :] = pad_ref[0:CONV_PAD, :]


def _conv_bwd_taps(pad_ref, w_ref, x, dw_ref, k):
    acc = None
    for j in range(k):
        sh = pad_ref[pl.ds(k - 1 - j, CONV_TS), :]
        term = w_ref[j:j + 1, :] * sh
        acc = term if acc is None else acc + term
        dw_ref[j:j + 1, :] += jnp.sum(x * sh, axis=0, keepdims=True)
    return acc


def short_conv_fwd(name, p, blk_b, w, nb):
    t = p.shape[0]
    d = w.shape[1]
    s, ns, nc = _conv_grid(t, d, nb)
    cb = d // CONV_TC

    def body(b_ref, c_ref, x_ref, w_ref, y_ref, cz_ref, pad_ref):
        _carry_past(pad_ref, pl.program_id(2))
        pad_ref[CONV_PAD:CONV_PAD + CONV_TS, :] = c_ref[...].astype(F32) * x_ref[...].astype(F32)
        cz = _causal_taps(pad_ref, w_ref, SC_KERNEL)
        cz_ref[...] = cz
        y_ref[...] = (b_ref[...].astype(F32) * cz).astype(y_ref.dtype)

    def pspec(off):
        return pl.BlockSpec((CONV_TS, CONV_TC), lambda c, b, si: (b * ns + si, (blk_b + off) * cb + c))

    ospec = pl.BlockSpec((CONV_TS, CONV_TC), lambda c, b, si: (b * ns + si, c))
    return pl.pallas_call(
        body, name=name, grid=(nc, nb, ns),
        in_specs=[pspec(0), pspec(1), pspec(2), pl.BlockSpec((SC_KERNEL, CONV_TC), lambda c, b, si: (0, c))],
        out_specs=[ospec, ospec],
        out_shape=[jax.ShapeDtypeStruct((t, d), MXU_DTYPE), jax.ShapeDtypeStruct((t, d), F32)],
        scratch_shapes=[pltpu.VMEM((CONV_PAD + CONV_TS, CONV_TC), F32)],
        compiler_params=_params(("parallel", "arbitrary", "arbitrary")),
    )(p, p, p, w)


def short_conv_bwd(name, p, blk_b, w, cz, dy, nb):
    t = p.shape[0]
    d = w.shape[1]
    s, ns, nc = _conv_grid(t, d, nb)
    cb = d // CONV_TC

    def body(b_ref, c_ref, x_ref, w_ref, cz_ref, dy_ref, db_ref, dc_ref, dx_ref, dw_ref, pad_ref):
        si = pl.program_id(2)
        _carry_future(pad_ref, si)
        dyv = dy_ref[...].astype(F32)
        cv, xv = c_ref[...].astype(F32), x_ref[...].astype(F32)
        db_ref[...] = (dyv * cz_ref[...]).astype(db_ref.dtype)
        pad_ref[0:CONV_TS, :] = dyv * b_ref[...].astype(F32)

        @pl.when(jnp.logical_and(pl.program_id(1) == 0, si == 0))
        def _():
            dw_ref[...] = jnp.zeros_like(dw_ref)

        dz = _conv_bwd_taps(pad_ref, w_ref, cv * xv, dw_ref, SC_KERNEL)
        dc_ref[...] = (dz * xv).astype(dc_ref.dtype)
        dx_ref[...] = (dz * cv).astype(dx_ref.dtype)

    def row(b, si):
        return b * ns + (ns - 1 - si)

    def pspec(off):
        return pl.BlockSpec((CONV_TS, CONV_TC), lambda c, b, si: (row(b, si), (blk_b + off) * cb + c))

    ospec = pl.BlockSpec((CONV_TS, CONV_TC), lambda c, b, si: (row(b, si), c))
    wspec = pl.BlockSpec((SC_KERNEL, CONV_TC), lambda c, b, si: (0, c))
    return pl.pallas_call(
        body, name=name, grid=(nc, nb, ns),
        in_specs=[pspec(0), pspec(1), pspec(2), wspec, ospec, ospec],
        out_specs=[ospec, ospec, ospec, wspec],
        out_shape=[jax.ShapeDtypeStruct((t, d), MXU_DTYPE)] * 3 + [jax.ShapeDtypeStruct((SC_KERNEL, d), F32)],
        scratch_shapes=[pltpu.VMEM((CONV_TS + CONV_PAD, CONV_TC), F32)],
        compiler_params=_params(("parallel", "arbitrary", "arbitrary")),
    )(p, p, p, w, cz, dy)


def conformer_conv_fwd(name, p, blk_a, w, bias, nb):
    t = p.shape[0]
    d = w.shape[1]
    s, ns, nc = _conv_grid(t, d, nb)
    cb = d // CONV_TC

    def body(a_ref, b_ref, w_ref, bias_ref, u_ref, pad_ref):
        _carry_past(pad_ref, pl.program_id(2))
        pad_ref[CONV_PAD:CONV_PAD + CONV_TS, :] = a_ref[...].astype(F32) * jax.nn.sigmoid(b_ref[...].astype(F32))
        u_ref[...] = _causal_taps(pad_ref, w_ref, CF_KERNEL) + bias_ref[...]

    def pspec(off):
        return pl.BlockSpec((CONV_TS, CONV_TC), lambda c, b, si: (b * ns + si, (blk_a + off) * cb + c))

    return pl.pallas_call(
        body, name=name, grid=(nc, nb, ns),
        in_specs=[pspec(0), pspec(1), pl.BlockSpec((CF_KERNEL, CONV_TC), lambda c, b, si: (0, c)),
                  pl.BlockSpec((1, CONV_TC), lambda c, b, si: (0, c))],
        out_specs=pl.BlockSpec((CONV_TS, CONV_TC), lambda c, b, si: (b * ns + si, c)),
        out_shape=jax.ShapeDtypeStruct((t, d), F32),
        scratch_shapes=[pltpu.VMEM((CONV_PAD + CONV_TS, CONV_TC), F32)],
        compiler_params=_params(("parallel", "arbitrary", "arbitrary")),
    )(p, p, w, bias)


def conformer_conv_bwd(name, p, blk_a, w, du, nb):
    t = p.shape[0]
    d = w.shape[1]
    s, ns, nc = _conv_grid(t, d, nb)
    cb = d // CONV_TC

    def body(a_ref, b_ref, w_ref, du_ref, da_ref, db_ref, dw_ref, dbias_ref, pad_ref):
        si = pl.program_id(2)
        _carry_future(pad_ref, si)
        duv = du_ref[...]
        av = a_ref[...].astype(F32)
        sg = jax.nn.sigmoid(b_ref[...].astype(F32))
        pad_ref[0:CONV_TS, :] = duv

        @pl.when(jnp.logical_and(pl.program_id(1) == 0, si == 0))
        def _():
            dw_ref[...] = jnp.zeros_like(dw_ref)
            dbias_ref[...] = jnp.zeros_like(dbias_ref)

        du0 = _conv_bwd_taps(pad_ref, w_ref, av * sg, dw_ref, CF_KERNEL)
        da_ref[...] = (du0 * sg).astype(da_ref.dtype)
        db_ref[...] = (du0 * av * sg * (1.0 - sg)).astype(db_ref.dtype)
        dbias_ref[...] += jnp.sum(duv, axis=0, keepdims=True)

    def row(b, si):
        return b * ns + (ns - 1 - si)

    def pspec(off):
        return pl.BlockSpec((CONV_TS, CONV_TC), lambda c, b, si: (row(b, si), (blk_a + off) * cb + c))

    ospec = pl.BlockSpec((CONV_TS, CONV_TC), lambda c, b, si: (row(b, si), c))
    wspec = pl.BlockSpec((CF_KERNEL, CONV_TC), lambda c, b, si: (0, c))
    bspec = pl.BlockSpec((1, CONV_TC), lambda c, b, si: (0, c))
    return pl.pallas_call(
        body, name=name, grid=(nc, nb, ns),
        in_specs=[pspec(0), pspec(1), wspec, ospec],
        out_specs=[ospec, ospec, wspec, bspec],
        out_shape=[jax.ShapeDtypeStruct((t, d), MXU_DTYPE)] * 2
        + [jax.ShapeDtypeStruct((CF_KERNEL, d), F32), jax.ShapeDtypeStruct((1, d), F32)],
        scratch_shapes=[pltpu.VMEM((CONV_TS + CONV_PAD, CONV_TC), F32)],
        compiler_params=_params(("parallel", "arbitrary", "arbitrary")),
    )(p, p, w, du)


BLOCKS = ("ffn1", "mixer", "ffn2")
BLOCK_WEIGHTS = {"ffn1": ("ffn1_w_gu", "ffn1_w_down"), "mixer": ("w_in", "w_ret_o", "w_sc_o", "w_cf_o", "w_o"),
                 "ffn2": ("ffn2_w_gu", "ffn2_w_down")}
BIG = BLOCK_WEIGHTS["ffn1"] + BLOCK_WEIGHTS["mixer"] + BLOCK_WEIGHTS["ffn2"]
MODE = {"ffn1_w_gu": "col", "ffn1_w_down": "row", "w_in": "col", "w_ret_o": "row", "w_sc_o": "row",
        "w_cf_o": "row", "w_o": "row", "ffn2_w_gu": "col", "ffn2_w_down": "row"}
NORM_OF = {"ffn1": 0, "mixer": 2, "ffn2": 4}
BLK_GATE, BLK_SCB, BLK_CFA, BLK_MERGE = 2, 3, 6, 8


def _rope_tables(positions):
    half = RET_QK_DIM // 2
    inv_freq = ROPE_BASE ** (-jnp.arange(half, dtype=F32) / half)
    ang = positions.astype(F32)[..., None] * inv_freq
    cos, sin = jnp.cos(ang), jnp.sin(ang)
    nb, s = positions.shape
    cos2 = jnp.concatenate([cos, cos], axis=-1).reshape(nb * s, RET_QK_DIM)
    sin2 = jnp.concatenate([-sin, sin], axis=-1).reshape(nb * s, RET_QK_DIM)
    return cos2, sin2


def _log_gamma():
    lg = jnp.log(1.0 - 2.0 ** (-5.0 - jnp.arange(RET_HEADS, dtype=F32)))
    return jnp.broadcast_to(lg[:, None, None], (RET_HEADS, 1, RET_QK_DIM))


def _ffn_fwd(xs, w, tag, g_pre, g_post):
    h = rms_fwd("ffn_rms", xs, g_pre)
    gu = mm_fwd("ffn_gu", h, w[tag + "_w_gu"], "col", MXU_DTYPE)
    a = swiglu_fwd("ffn_act", gu)
    y = mm_fwd("ffn_down", a, w[tag + "_w_down"], "row", F32)
    out = post_fwd("ffn_post", xs, y, g_post, 0.5)
    return out, dict(x=xs, h=h, gu=gu, a=a, y=y, w=w)


def _ffn_bwd(dxs, sv, tag, g_pre, g_post):
    w = sv["w"]
    gu_w, down_w = w[tag + "_w_gu"], w[tag + "_w_down"]
    dy, dg_post = post_bwd("ffn_post_bwd", sv["y"], g_post, dxs, 0.5)
    da = mm_dx("ffn_down_dx", dy, down_w, "row", MXU_DTYPE)
    grads = {tag + "_w_down": mm_dw("ffn_down_dw", sv["a"], dy, "row", down_w.shape)}
    dgu = swiglu_bwd("ffn_act_bwd", sv["gu"], da)
    dh = mm_dx("ffn_gu_dx", dgu, gu_w, "col", F32)
    grads[tag + "_w_gu"] = mm_dw("ffn_gu_dw", sv["h"], dgu, "col", gu_w.shape)
    dxs, dg_pre = rms_bwd("ffn_rms_bwd", sv["x"], g_pre, dh, dxs)
    return dxs, grads, dg_pre, dg_post


def _mixer_fwd(xs, w, sm, g_pre, g_post, rope, nb, s):
    cos2, sin2, log_g = rope
    d = xs.shape[1]
    gate_blk = (BLK_GATE * d) // RET_V_DIM
    h = rms_fwd("mx_rms", xs, g_pre)
    p = mm_fwd("mx_in", h, w["w_in"], "col", MXU_DTYPE)
    o = retention_fwd("ret_fwd", p, cos2, sin2, log_g, nb, s)
    ya_in = head_gate_fwd("ret_gate", o, p, gate_blk)
    yb_in, cz = short_conv_fwd("sc_fwd", p, BLK_SCB, sm["sc_conv_w"], nb)
    u1 = conformer_conv_fwd("cf_fwd", p, BLK_CFA, sm["cf_dw_w"], sm["cf_dw_b"], nb)
    yc_in = ln_silu_fwd("cf_ln", u1, sm["cf_ln_g"], sm["cf_ln_b"])
    ya = mm_fwd("mx_proj", ya_in, w["w_ret_o"], "row", F32)
    yb = mm_fwd("mx_proj", yb_in, w["w_sc_o"], "row", F32)
    yc = mm_fwd("mx_proj", yc_in, w["w_cf_o"], "row", F32)
    mg = merge_fwd("mx_merge", p, BLK_MERGE, ya, yb, yc)
    m = mm_fwd("mx_proj", mg, w["w_o"], "row", F32)
    out = post_fwd("mx_post", xs, m, g_post, 1.0)
    return out, dict(x=xs, h=h, p=p, o=o, ya_in=ya_in, yb_in=yb_in, cz=cz, u1=u1, yc_in=yc_in, ya=ya, yb=yb, yc=yc,
                     mg=mg, m=m, w=w)


def _mixer_bwd(dxs, sv, sm, g_pre, g_post, rope, nb, s):
    cos2, sin2, log_g = rope
    w, p = sv["w"], sv["p"]
    d = dxs.shape[1]
    gate_blk = (BLK_GATE * d) // RET_V_DIM
    grads, gsm = {}, {}

    def proj_bwd(wname, a_in, dy, out_dtype):
        grads[wname] = mm_dw("mx_proj_dw", a_in, dy, "row", w[wname].shape)
        return mm_dx("mx_proj_dx", dy, w[wname], "row", out_dtype)

    dm, dg_post = post_bwd("mx_post_bwd", sv["m"], g_post, dxs, 1.0)
    dmg = proj_bwd("w_o", sv["mg"], dm, MXU_DTYPE)
    dg0, dg1, dg2, dya, dyb, dyc = merge_bwd("mx_merge_bwd", p, BLK_MERGE, sv["ya"], sv["yb"], sv["yc"], dmg)
    dya_in = proj_bwd("w_ret_o", sv["ya_in"], dya, MXU_DTYPE)
    dyb_in = proj_bwd("w_sc_o", sv["yb_in"], dyb, MXU_DTYPE)
    dyc_in = proj_bwd("w_cf_o", sv["yc_in"], dyc, MXU_DTYPE)
    do, dgret = head_gate_bwd("ret_gate_bwd", sv["o"], p, gate_blk, dya_in)
    dq, dk, dv = retention_bwd("ret_bwd", p, cos2, sin2, log_g, do, nb, s)
    dscb, dscc, dscx, gsm["sc_conv_w"] = short_conv_bwd("sc_bwd", p, BLK_SCB, sm["sc_conv_w"], sv["cz"], dyb_in, nb)
    du1, dlg, dlb = ln_silu_bwd("cf_ln_bwd", sv["u1"], sm["cf_ln_g"], sm["cf_ln_b"], dyc_in)
    dcfa, dcfb, gsm["cf_dw_w"], dbias = conformer_conv_bwd("cf_bwd", p, BLK_CFA, sm["cf_dw_w"], du1, nb)
    gsm.update(cf_ln_g=dlg[0], cf_ln_b=dlb[0], cf_dw_b=dbias[0])
    dp = jnp.concatenate([dq, dk, dv, dgret, dscb, dscc, dscx, dcfa, dcfb, dg0, dg1, dg2], axis=1)
    dh = mm_dx("mx_in_dx", dp, w["w_in"], "col", F32)
    grads["w_in"] = mm_dw("mx_in_dw", sv["h"], dp, "col", w["w_in"].shape)
    dxs, dg_pre = rms_bwd("mx_rms_bwd", sv["x"], g_pre, dh, dxs)
    return dxs, grads, gsm, dg_pre, dg_post


def local_step(x, positions, target, small, fetch, push):
    nb, s, d = x.shape
    t = nb * s
    depth = small["norm_g"].shape[0]
    rope = _rope_tables(positions) + (_log_gamma(),)
    xs = x.reshape(t, d)
    token = [None]

    def gain(l, i):
        g = small["norm_g"][l, i][None, :]
        if token[0] is not None:
            g, token[0] = g + token[0], None
        return g

    def mixer_small(l):
        return dict(sc_conv_w=small["sc_conv_w"][l], cf_dw_w=small["cf_dw_w"][l], cf_dw_b=small["cf_dw_b"][l][None, :],
                    cf_ln_g=small["cf_ln_g"][l][None, :], cf_ln_b=small["cf_ln_b"][l][None, :])

    saved = {}
    for l in range(depth):
        for blk in BLOCKS:
            w = fetch(l, blk, xs)
            i0 = NORM_OF[blk]
            if blk == "mixer":
                xs, saved[l, blk] = _mixer_fwd(xs, w, mixer_small(l), gain(l, i0), gain(l, i0 + 1), rope, nb, s)
            else:
                xs, saved[l, blk] = _ffn_fwd(xs, w, blk, gain(l, i0), gain(l, i0 + 1))

    dxs, loss = loss_head("loss", xs, target.reshape(t, d))

    dnorm = [[None] * 6 for _ in range(depth)]
    gsmall = {n: [None] * depth for n in ("sc_conv_w", "cf_dw_w", "cf_dw_b", "cf_ln_g", "cf_ln_b")}
    for l in reversed(range(depth)):
        for blk in reversed(BLOCKS):
            i0 = NORM_OF[blk]
            g_post, g_pre = gain(l, i0 + 1), gain(l, i0)
            if blk == "mixer":
                dxs, grads, gsm, dnorm[l][i0], dnorm[l][i0 + 1] = _mixer_bwd(
                    dxs, saved[l, blk], mixer_small(l), g_pre, g_post, rope, nb, s)
                for n, v in gsm.items():
                    gsmall[n][l] = v
            else:
                dxs, grads, dnorm[l][i0], dnorm[l][i0 + 1] = _ffn_bwd(dxs, saved[l, blk], blk, g_pre, g_post)
            token[0] = push(l, blk, grads)

    gs = {n: jnp.stack(v) for n, v in gsmall.items()}
    gs["norm_g"] = jnp.stack([jnp.concatenate(r, axis=0) for r in dnorm])
    return loss, dxs.reshape(nb, s, d), gs


ANY = pl.BlockSpec(memory_space=pl.ANY)
HBM = pl.BlockSpec(memory_space=pltpu.HBM)
SEM = pl.BlockSpec(memory_space=pltpu.SEMAPHORE)
VMEM_WHOLE = pl.BlockSpec(memory_space=pltpu.VMEM)
EFFECT = pltpu.SideEffectType.DATAFLOW_SIDE_EFFECTING
TOKEN = jax.ShapeDtypeStruct((8, 128), F32)


def _other_chips(x, y):
    return [(1 - x, y), (x, 1 - y), (1 - x, 1 - y)]


def _remote(src, dst, send_sem, recv_sem, to):
    return pltpu.make_async_remote_copy(src_ref=src, dst_ref=dst, send_sem=send_sem, recv_sem=recv_sem,
                                        device_id=to, device_id_type=MESH)


def _in_hbm(v):
    return pltpu.with_memory_space_constraint(v, pltpu.HBM)


def _gather_copies(shards, lands, send, recv):
    x, y, c = _axes()
    me = 2 * x + y
    mine, theirs = [], []
    for a, (sh, ld) in enumerate(zip(shards, lands)):
        rh = sh.shape[0] // 2
        rows = pl.ds(c * rh, rh)
        for k, (px, py) in enumerate(_other_chips(x, y)):
            to = (px, py, c)
            mine.append(_remote(sh.at[rows, :], ld.at[me, rows, :], send.at[3 * a + k], recv.at[3 * a + k], to))
            theirs.append(_remote(sh.at[rows, :], ld.at[2 * px + py, rows, :], send.at[3 * a + k],
                                  recv.at[3 * a + k], to))
    return mine, theirs


def gather_start(groups):
    flat = [s for g in groups for s in g]
    n, ng = len(flat), len(groups)
    sizes = [len(g) for g in groups]

    def body(*refs):
        shards, lands = refs[:n], refs[n:2 * n]
        sems = refs[2 * n:2 * n + 2 * ng]
        token = refs[-1]
        at = 0
        for g, m in enumerate(sizes):
            mine, _ = _gather_copies(shards[at:at + m], lands[at:at + m], sems[2 * g], sems[2 * g + 1])
            for cp in mine:
                cp.start()
            at += m
        token[...] = jnp.zeros_like(token)

    sem_shapes = []
    for m in sizes:
        sem_shapes += [pltpu.SemaphoreType.DMA((3 * m,))] * 2
    hbm_shapes = [pltpu.HBM(s.shape, s.dtype) for s in flat]
    hbm_shapes += [pltpu.HBM((N_CHIP,) + s.shape, s.dtype) for s in flat]
    res = pl.pallas_call(
        body, name="gather_start", in_specs=[HBM] * (2 * n),
        out_specs=[SEM] * (2 * ng) + [HBM] * (2 * n) + [VMEM_WHOLE],
        out_shape=sem_shapes + hbm_shapes + [TOKEN],
        input_output_aliases={i: 2 * ng + i for i in range(2 * n)},
        compiler_params=pltpu.CompilerParams(has_side_effects=EFFECT),
    )(*[_in_hbm(s) for s in flat], *[_in_hbm(lax.empty((N_CHIP,) + s.shape, s.dtype)) for s in flat])
    sems, thru, token = res[:2 * ng], res[2 * ng:2 * ng + 2 * n], res[-1]
    out, at = [], 0
    for g, m in enumerate(sizes):
        out.append((sems[2 * g], sems[2 * g + 1], thru[at:at + m], thru[n + at:n + at + m]))
        at += m
    return out, token


def gather_wait(shards, lands, send, recv, after):
    m = len(shards)

    def body(*refs):
        mine, theirs = _gather_copies(refs[:m], refs[m:2 * m], refs[2 * m], refs[2 * m + 1])
        for cp in mine:
            cp.wait_send()
        for cp in theirs:
            cp.wait_recv()

    res = pl.pallas_call(
        body, name="gather_wait", in_specs=[HBM] * (2 * m) + [SEM, SEM, ANY], out_specs=[HBM] * (2 * m),
        out_shape=[pltpu.HBM(s.shape, s.dtype) for s in shards] + [pltpu.HBM(l.shape, l.dtype) for l in lands],
        input_output_aliases={i: i for i in range(2 * m)},
        compiler_params=pltpu.CompilerParams(has_side_effects=EFFECT),
    )(*shards, *lands, send, recv, after)
    return res[:m], res[m:]


def sibling_fill(shards, lands):
    m = len(shards)

    def body(*refs):
        shs, lds = refs[:m], refs[m:2 * m]
        send, recv, loc_sem = refs[3 * m:]
        x, y, c = _axes()
        me = 2 * x + y
        sib = (x, y, 1 - c)
        local = [pltpu.make_async_copy(shs[a], lds[a].at[me], loc_sem.at[a]) for a in range(m)]
        for cp in local:
            cp.start()
        cps = []
        for a in range(m):
            rh = shs[a].shape[0] // 2
            for k, (px, py) in enumerate(_other_chips(x, y)):
                got = lds[a].at[2 * px + py, pl.ds(c * rh, rh), :]
                cp = _remote(got, got, send.at[3 * a + k], recv.at[3 * a + k], sib)
                cp.start()
                cps.append(cp)
        for a in range(m):
            rh = shs[a].shape[0] // 2
            for k, (px, py) in enumerate(_other_chips(x, y)):
                blk = lds[a].at[2 * px + py, pl.ds((1 - c) * rh, rh), :]
                _remote(blk, blk, send.at[3 * a + k], recv.at[3 * a + k], sib).wait_recv()
        for cp in cps:
            cp.wait_send()
        for cp in local:
            cp.wait()

    return pl.pallas_call(
        body, name="sibling_fill", in_specs=[ANY] * (2 * m), out_specs=[ANY] * m,
        out_shape=[jax.ShapeDtypeStruct(l.shape, l.dtype) for l in lands],
        input_output_aliases={m + i: i for i in range(m)},
        scratch_shapes=[pltpu.SemaphoreType.DMA((3 * m,))] * 2 + [pltpu.SemaphoreType.DMA((m,))],
    )(*shards, *lands)


def exchange_sibling_halves(grads):
    n = len(grads)

    def body(*refs):
        ins, outs = refs[:n], refs[n:2 * n]
        send, recv = refs[2 * n:]
        x, y, c = _axes()
        cps = []
        for a in range(n):
            rh = ins[a].shape[1] // 2
            cp = _remote(ins[a].at[:, pl.ds((1 - c) * rh, rh), :], outs[a], send.at[a], recv.at[a], (x, y, 1 - c))
            cp.start()
            cps.append(cp)
        for cp in cps:
            cp.wait()

    return pl.pallas_call(
        body, name="exchange_sibling_halves", in_specs=[ANY] * n, out_specs=[ANY] * n,
        out_shape=[jax.ShapeDtypeStruct((g.shape[0], g.shape[1] // 2, g.shape[2]), g.dtype) for g in grads],
        scratch_shapes=[pltpu.SemaphoreType.DMA((n,))] * 2,
    )(*grads)


def add_halves(g4, land, ids):
    nq, r, c = g4.shape
    rh = r // 2

    def body(ids_ref, a_ref, b_ref, o_ref):
        o_ref[...] = (a_ref[...].astype(F32) + b_ref[...].astype(F32)).astype(o_ref.dtype)

    return pl.pallas_call(
        body, name="add_halves",
        grid_spec=pltpu.PrefetchScalarGridSpec(
            num_scalar_prefetch=1, grid=(nq,),
            in_specs=[pl.BlockSpec((None, rh, c), lambda i, ids_ref: (i, ids_ref[1], 0)),
                      pl.BlockSpec((None, rh, c), lambda i, ids_ref: (i, 0, 0))],
            out_specs=pl.BlockSpec((None, rh, c), lambda i, ids_ref: (i, 0, 0))),
        out_shape=jax.ShapeDtypeStruct((nq, rh, c), g4.dtype),
        compiler_params=_params(("parallel",)),
    )(ids, g4, land)


def _scatter_copies(parts, lands, send, recv):
    x, y, c = _axes()
    cps = []
    for a, (pt, ld) in enumerate(zip(parts, lands)):
        for k, (px, py) in enumerate(_other_chips(x, y)):
            cps.append(_remote(pt.at[2 * px + py], ld.at[k], send.at[3 * a + k], recv.at[3 * a + k], (px, py, c)))
    return cps


def scatter_start(parts):
    m = len(parts)

    def body(*refs):
        for cp in _scatter_copies(refs[:m], refs[m:2 * m], refs[2 * m], refs[2 * m + 1]):
            cp.start()
        refs[-1][...] = jnp.zeros_like(refs[-1])

    lands = [lax.empty((N_CHIP - 1,) + p.shape[1:], p.dtype) for p in parts]
    res = pl.pallas_call(
        body, name="scatter_start", in_specs=[HBM] * (2 * m), out_specs=[SEM, SEM] + [HBM] * (2 * m) + [VMEM_WHOLE],
        out_shape=[pltpu.SemaphoreType.DMA((3 * m,))] * 2 + [pltpu.HBM(p.shape, p.dtype) for p in parts]
        + [pltpu.HBM(l.shape, l.dtype) for l in lands] + [TOKEN],
        input_output_aliases={i: 2 + i for i in range(2 * m)},
        compiler_params=pltpu.CompilerParams(has_side_effects=EFFECT),
    )(*[_in_hbm(p) for p in parts], *[_in_hbm(l) for l in lands])
    return res[0], res[1], res[2:2 + m], res[2 + m:2 + 2 * m], res[-1]


def scatter_wait(parts, lands, send, recv, after):
    m = len(parts)

    def body(*refs):
        for cp in _scatter_copies(refs[:m], refs[m:2 * m], refs[2 * m], refs[2 * m + 1]):
            cp.wait_send()
            cp.wait_recv()

    res = pl.pallas_call(
        body, name="scatter_wait", in_specs=[HBM] * (2 * m) + [SEM, SEM, ANY], out_specs=[HBM] * (2 * m),
        out_shape=[pltpu.HBM(p.shape, p.dtype) for p in parts] + [pltpu.HBM(l.shape, l.dtype) for l in lands],
        input_output_aliases={i: i for i in range(2 * m)},
        compiler_params=pltpu.CompilerParams(has_side_effects=EFFECT),
    )(*parts, *lands, send, recv, after)
    return res[:m], res[m:]


def sum_partials(part, land, ids, layer, depth, into):
    _, rh, c = part.shape
    tr = min(rh, 256)
    nt = rh // tr

    def body(ids_ref, p_ref, l_ref, *rest):
        o_ref = rest[-1]
        acc = p_ref[...].astype(F32)
        for k in range(N_CHIP - 1):
            acc = acc + l_ref[k].astype(F32)
        o_ref[...] = acc

    in_specs = [pl.BlockSpec((None, tr, c), lambda i, ids_ref: (ids_ref[0], i, 0)),
                pl.BlockSpec((N_CHIP - 1, tr, c), lambda i, ids_ref: (0, i, 0))]
    args = [ids, part, land]
    aliases = {}
    if into is not None:
        in_specs.append(ANY)
        args.append(into)
        aliases = {3: 0}
    return pl.pallas_call(
        body, name="sum_partials",
        grid_spec=pltpu.PrefetchScalarGridSpec(
            num_scalar_prefetch=1, grid=(nt,), in_specs=in_specs,
            out_specs=pl.BlockSpec((None, tr, c), lambda i, ids_ref: (layer, ids_ref[1] * nt + i, 0))),
        out_shape=jax.ShapeDtypeStruct((depth, 2 * rh, c), F32), input_output_aliases=aliases,
        compiler_params=_params(("parallel",)),
    )(*args)


def exchange_final_halves(bufs, layers):
    n = len(bufs)

    def body(*refs):
        outs = refs[n:2 * n]
        send, recv = refs[2 * n:]
        x, y, c = _axes()
        sib = (x, y, 1 - c)
        cps, at = [], 0
        for a in range(n):
            rh = outs[a].shape[1] // 2
            for l in layers[a]:
                mine = outs[a].at[l, pl.ds(c * rh, rh), :]
                cp = _remote(mine, mine, send.at[at], recv.at[at], sib)
                cp.start()
                cps.append(cp)
                at += 1
        at = 0
        for a in range(n):
            rh = outs[a].shape[1] // 2
            for l in layers[a]:
                theirs = outs[a].at[l, pl.ds((1 - c) * rh, rh), :]
                _remote(theirs, theirs, send.at[at], recv.at[at], sib).wait_recv()
                at += 1
        for cp in cps:
            cp.wait_send()

    ncp = sum(len(ls) for ls in layers)
    return pl.pallas_call(
        body, name="exchange_final_halves", in_specs=[ANY] * n, out_specs=[ANY] * n,
        out_shape=[jax.ShapeDtypeStruct(g.shape, g.dtype) for g in bufs],
        input_output_aliases={i: i for i in range(n)},
        scratch_shapes=[pltpu.SemaphoreType.DMA((ncp,))] * 2,
    )(*bufs)


def allgather_small(pk):
    def body(in_ref, out_ref, send, recv):
        x, y, c = _axes()
        me = 2 * x + y
        chips = _other_chips(x, y)
        out_ref[pl.ds(me, 1)] = in_ref[...][None]
        cps = []
        for k, (px, py) in enumerate(chips):
            cp = _remote(in_ref, out_ref.at[me], send.at[k], recv.at[k], (px, py, c))
            cp.start()
            cps.append(cp)
        for k, (px, py) in enumerate(chips):
            got = out_ref.at[2 * px + py]
            _remote(got, got, send.at[k], recv.at[k], (px, py, c)).wait_recv()
        for cp in cps:
            cp.wait_send()

    return pl.pallas_call(
        body, name="allgather_small", in_specs=[VMEM_WHOLE], out_specs=VMEM_WHOLE,
        out_shape=jax.ShapeDtypeStruct((N_CHIP,) + pk.shape, pk.dtype),
        scratch_shapes=[pltpu.SemaphoreType.DMA((3,))] * 2,
    )(pk)


def allreduce_small(g):
    ndev = 8

    def body(in_ref, out_ref, slots, send, recv):
        x, y, c = _axes()
        me = 4 * x + 2 * y + c
        slots[pl.ds(me, 1)] = in_ref[...][None]
        peers = []
        for mask in range(1, ndev):
            px = 1 - x if mask & 4 else x
            py = 1 - y if mask & 2 else y
            pc = 1 - c if mask & 1 else c
            peers.append((px, py, pc))
        cps = []
        for k, peer in enumerate(peers):
            cp = _remote(in_ref, slots.at[me], send.at[k], recv.at[k], peer)
            cp.start()
            cps.append(cp)
        for k, (px, py, pc) in enumerate(peers):
            got = slots.at[4 * px + 2 * py + pc]
            _remote(got, got, send.at[k], recv.at[k], (px, py, pc)).wait_recv()
        for cp in cps:
            cp.wait_send()
        acc = slots[0]
        for d in range(1, ndev):
            acc = acc + slots[d]
        out_ref[...] = acc

    return pl.pallas_call(
        body, name="allreduce_small", in_specs=[VMEM_WHOLE], out_specs=VMEM_WHOLE,
        out_shape=jax.ShapeDtypeStruct(g.shape, g.dtype),
        scratch_shapes=[pltpu.VMEM((ndev,) + g.shape, g.dtype), pltpu.SemaphoreType.DMA((ndev - 1,)),
                        pltpu.SemaphoreType.DMA((ndev - 1,))],
    )(g)


def adamw(w, g, m, v):
    shape = w.shape
    cols = shape[-1]
    rows = int(np.prod(shape[:-1]))
    tr = rows
    for cand in (256, 128):
        if rows % cand == 0 and cand * cols * 4 <= 2 * 1024 * 1024:
            tr = cand
            break
    c1 = 1.0 - ADAM_B1 ** ADAM_STEP
    c2 = 1.0 - ADAM_B2 ** ADAM_STEP

    def body(w_ref, g_ref, m_ref, v_ref, d_ref, nm_ref, nv_ref):
        gv = g_ref[...]
        nm = ADAM_B1 * m_ref[...] + (1.0 - ADAM_B1) * gv
        nv = ADAM_B2 * v_ref[...] + (1.0 - ADAM_B2) * jnp.square(gv)
        d_ref[...] = -ADAM_LR * ((nm / c1) / (jnp.sqrt(nv / c2) + ADAM_EPS) + ADAM_WD * w_ref[...])
        nm_ref[...] = nm
        nv_ref[...] = nv

    spec = pl.BlockSpec((tr, cols), lambda i: (i, 0))
    res = pl.pallas_call(
        body, name="adamw", grid=(rows // tr,), in_specs=[spec] * 4, out_specs=[spec] * 3,
        out_shape=[jax.ShapeDtypeStruct((rows, cols), F32)] * 3, compiler_params=_params(("parallel",)),
    )(*[a.reshape(rows, cols) for a in (w, g, m, v)])
    return [r.reshape(shape) for r in res]


WEIGHTS = ("norm_g", "ffn1_w_gu", "ffn1_w_down", "w_in", "w_ret_o", "sc_conv_w", "w_sc_o", "cf_dw_w", "cf_dw_b",
           "cf_ln_g", "cf_ln_b", "w_cf_o", "w_o", "ffn2_w_gu", "ffn2_w_down")
SHARDED_SMALL = ("norm_g", "sc_conv_w", "cf_dw_w")
REPLICATED_SMALL = ("cf_dw_b", "cf_ln_g", "cf_ln_b")
SUBLANES = 8


def _pack_rows(parts):
    padded, offs, at = [], [], 0
    for p in parts:
        r = -(-p.shape[0] // SUBLANES) * SUBLANES
        padded.append(jnp.pad(p, ((0, r - p.shape[0]), (0, 0))))
        offs.append(at)
        at += r
    return jnp.concatenate(padded, axis=0), offs


def kernel(x, positions, norm_g, ffn1_w_gu, ffn1_w_down, w_in, w_ret_o, sc_conv_w, w_sc_o, cf_dw_w, cf_dw_b, cf_ln_g, cf_ln_b, w_cf_o, w_o, ffn2_w_gu, ffn2_w_down, loss_target, m_norm_g, m_ffn1_w_gu, m_ffn1_w_down, m_w_in, m_w_ret_o, m_sc_conv_w, m_w_sc_o, m_cf_dw_w, m_cf_dw_b, m_cf_ln_g, m_cf_ln_b, m_w_cf_o, m_w_o, m_ffn2_w_gu, m_ffn2_w_down, v_norm_g, v_ffn1_w_gu, v_ffn1_w_down, v_w_in, v_w_ret_o, v_sc_conv_w, v_w_sc_o, v_cf_dw_w, v_cf_dw_b, v_cf_ln_g, v_cf_ln_b, v_w_cf_o, v_w_o, v_ffn2_w_gu, v_ffn2_w_down):
    wts = dict(zip(WEIGHTS, (norm_g, ffn1_w_gu, ffn1_w_down, w_in, w_ret_o, sc_conv_w, w_sc_o, cf_dw_w, cf_dw_b,
                             cf_ln_g, cf_ln_b, w_cf_o, w_o, ffn2_w_gu, ffn2_w_down)))
    mom = dict(zip(WEIGHTS, (m_norm_g, m_ffn1_w_gu, m_ffn1_w_down, m_w_in, m_w_ret_o, m_sc_conv_w, m_w_sc_o,
                             m_cf_dw_w, m_cf_dw_b, m_cf_ln_g, m_cf_ln_b, m_w_cf_o, m_w_o, m_ffn2_w_gu, m_ffn2_w_down)))
    var = dict(zip(WEIGHTS, (v_norm_g, v_ffn1_w_gu, v_ffn1_w_down, v_w_in, v_w_ret_o, v_sc_conv_w, v_w_sc_o,
                             v_cf_dw_w, v_cf_dw_b, v_cf_ln_g, v_cf_ln_b, v_w_cf_o, v_w_o, v_ffn2_w_gu, v_ffn2_w_down)))
    depth = norm_g.shape[0]
    dq = norm_g.shape[-1]
    d = N_CHIP * dq
    chip = 2 * lax.axis_index("x") + lax.axis_index("y")
    ids = jnp.stack([chip, lax.axis_index("c")]).astype(jnp.int32)

    order = [(l, blk) for l in range(depth) for blk in BLOCKS]
    cast = {n: wts[n].astype(MXU_DTYPE) for n in BIG}
    started, token = gather_start([[cast[n][l] for n in BLOCK_WEIGHTS[blk]] for l, blk in order])
    started = dict(zip(order, started))

    def fetch(l, blk, after):
        send, recv, shards, lands = started[l, blk]
        shards, lands = gather_wait(shards, lands, send, recv, after)
        return dict(zip(BLOCK_WEIGHTS[blk], sibling_fill(shards, lands)))

    pk, offs = _pack_rows([wts[n].reshape(-1, dq) for n in SHARDED_SMALL])
    gk = allgather_small(pk + token[0:1, 0:1])
    gk = gk.transpose(1, 0, 2).reshape(pk.shape[0], d)
    small = {n: wts[n] for n in REPLICATED_SMALL}
    for n, o in zip(SHARDED_SMALL, offs):
        rows = wts[n].shape[0] * wts[n].shape[1]
        small[n] = gk[o:o + rows].reshape(wts[n].shape[:2] + (d,))

    gsum = {n: None for n in BIG}
    inflight = []

    def land_sums(after):
        (l, blk), parts, lands, send, recv = inflight.pop()
        parts, lands = scatter_wait(parts, lands, send, recv, after)
        for n, pt, ld in zip(BLOCK_WEIGHTS[blk], parts, lands):
            gsum[n] = sum_partials(pt, ld, ids, l, depth, gsum[n])

    def push(l, blk, grads):
        gl = [grads[n] for n in BLOCK_WEIGHTS[blk]]
        parts = [add_halves(g, ld, ids) for g, ld in zip(gl, exchange_sibling_halves(gl))]
        send, recv, thru, lands, tok = scatter_start(parts)
        if inflight:
            land_sums(parts[0])
        inflight.append(((l, blk), thru, lands, send, recv))
        return tok[0:1, 0:1]

    loss, grad_x, gs = local_step(x, positions, loss_target, small, fetch, push)

    names = SHARDED_SMALL + REPLICATED_SMALL
    pg, offs = _pack_rows([gs[n].reshape(-1, d) for n in names])
    tot = allreduce_small(pg)
    grads = {}
    for n, o in zip(names, offs):
        rows = int(np.prod(gs[n].shape[:-1]))
        full = tot[o:o + rows]
        if n in SHARDED_SMALL:
            full = lax.dynamic_slice_in_dim(full, chip * dq, dq, axis=1)
        grads[n] = full.reshape(wts[n].shape)

    last = BLOCK_WEIGHTS[order[0][1]]
    early = [n for n in BIG if n not in last]
    every = tuple(range(depth))
    done = exchange_final_halves([gsum[n] for n in early + list(last)],
                                 [every] * len(early) + [every[1:]] * len(last))
    for n, g in zip(early + list(last), done):
        gsum[n] = g
    delta, new_m, new_v = {}, {}, {}
    for n in WEIGHTS:
        if n not in last:
            grads[n] = gsum[n] if n in BIG else grads[n]
            delta[n], new_m[n], new_v[n] = adamw(wts[n], grads[n], mom[n], var[n])
    land_sums(delta[early[-1]])
    done = exchange_final_halves([gsum[n] for n in last], [every[:1]] * len(last))
    for n, g in zip(last, done):
        grads[n] = g
        delta[n], new_m[n], new_v[n] = adamw(wts[n], g, mom[n], var[n])

    loss_all = lax.psum(loss[0, 0], ("x", "y", "c"))
    return (loss_all, grad_x, *[grads[n] for n in WEIGHTS], *[delta[n] for n in WEIGHTS],
            *[new_m[n] for n in WEIGHTS], *[new_v[n] for n in WEIGHTS])
```

```python
import functools

import jax
import jax.numpy as jnp
import numpy as np
from jax import lax
from jax.experimental import pallas as pl
from jax.experimental.pallas import tpu as pltpu

F32 = jnp.float32
BF16 = jnp.bfloat16
MXU_DTYPE = BF16
VMEM_LIMIT_BYTES = 56 * 1024 * 1024
MESH = pl.DeviceIdType.MESH

N_CHIP = 4
CHUNK = 64
RET_HEADS = 4
RET_QK_DIM = 128
RET_V_DIM = 256
SC_KERNEL = 3
CF_KERNEL = 31
ROPE_BASE = 10000.0
NORM_EPS = 1e-6
LN_EPS = 1e-5
ADAM_LR = 0.001
ADAM_B1 = 0.9
ADAM_B2 = 0.999
ADAM_EPS = 1e-08
ADAM_WD = 0.01
ADAM_STEP = 10

CONV_PAD = 32
CONV_TS = 128
CONV_TC = 512
RET_TQ = 512
MM_TM = 1024
MM_TN = 1536
MM_K1 = 1024


def _params(sem):
    return pltpu.CompilerParams(dimension_semantics=sem, vmem_limit_bytes=VMEM_LIMIT_BYTES)


def _axes():
    return lax.axis_index("x"), lax.axis_index("y"), lax.axis_index("c")


NN = (((1,), (0,)), ((), ()))
NT = (((1,), (1,)), ((), ()))
TN = (((0,), (0,)), ((), ()))


def _mm(name, a, b, out_shape, out_dtype, grid, a_spec, b_spec, o_spec, dims, acc_shape):
    nk = grid[2]

    def body(a_ref, b_ref, o_ref, *scratch):
        bv = b_ref[...]
        if bv.ndim == 3:
            bv = bv.reshape(-1, bv.shape[-1])
        part = lax.dot_general(a_ref[...], bv, dims, preferred_element_type=F32)

        def put(v):
            o_ref[...] = v.reshape(o_ref.shape).astype(o_ref.dtype)

        if nk == 1:
            put(part)
        else:
            acc = scratch[0]
            k = pl.program_id(2)

            @pl.when(k == 0)
            def _():
                acc[...] = part

            @pl.when(k > 0)
            def _():
                acc[...] += part

            @pl.when(k == nk - 1)
            def _():
                put(acc[...])

    scratch = [pltpu.VMEM(acc_shape, F32)] if nk > 1 else []
    return pl.pallas_call(
        body, name=name, grid=grid, in_specs=[a_spec, b_spec], out_specs=o_spec,
        out_shape=jax.ShapeDtypeStruct(out_shape, out_dtype), scratch_shapes=scratch,
        compiler_params=_params(("parallel", "parallel", "arbitrary")),
    )(a, b)


def _tile(n, target):
    best = None
    for t in range(128, min(n, target) + 1, 128):
        if n % t == 0:
            best = t
    assert best is not None, (n, target)
    return best


def mm_fwd(name, a, w4, mode, out_dtype):
    t = a.shape[0]
    _, r, c = w4.shape
    tm = min(t, MM_TM)
    if mode == "col":
        tn = _tile(c, MM_TN)
        npj = c // tn
        grid = (t // tm, N_CHIP * npj, 1)
        a_spec = pl.BlockSpec((tm, r), lambda i, j, k: (i, 0))
        b_spec = pl.BlockSpec((None, r, tn), lambda i, j, k: (j // npj, 0, j % npj))
        o_spec = pl.BlockSpec((tm, tn), lambda i, j, k: (i, j))
        return _mm(name, a, w4, (t, N_CHIP * c), out_dtype, grid, a_spec, b_spec, o_spec, NN, (tm, tn))
    if N_CHIP * r <= MM_K1:
        grid = (t // tm, 1, 1)
        a_spec = pl.BlockSpec((tm, N_CHIP * r), lambda i, j, k: (i, 0))
        b_spec = pl.BlockSpec((N_CHIP, r, c), lambda i, j, k: (0, 0, 0))
        o_spec = pl.BlockSpec((tm, c), lambda i, j, k: (i, 0))
        return _mm(name, a, w4, (t, c), out_dtype, grid, a_spec, b_spec, o_spec, NN, (tm, c))
    grid = (t // tm, 1, N_CHIP)
    a_spec = pl.BlockSpec((tm, r), lambda i, j, k: (i, k))
    b_spec = pl.BlockSpec((None, r, c), lambda i, j, k: (k, 0, 0))
    o_spec = pl.BlockSpec((tm, c), lambda i, j, k: (i, 0))
    return _mm(name, a, w4, (t, c), out_dtype, grid, a_spec, b_spec, o_spec, NN, (tm, c))


def mm_dx(name, dy, w4, mode, out_dtype):
    t = dy.shape[0]
    _, r, c = w4.shape
    tm = min(t, MM_TM)
    if mode == "col":
        tn = _tile(c, MM_TN)
        npj = c // tn
        grid = (t // tm, 1, N_CHIP * npj)
        a_spec = pl.BlockSpec((tm, tn), lambda i, j, k: (i, k))
        b_spec = pl.BlockSpec((None, r, tn), lambda i, j, k: (k // npj, 0, k % npj))
        o_spec = pl.BlockSpec((tm, r), lambda i, j, k: (i, 0))
        return _mm(name, dy, w4, (t, r), out_dtype, grid, a_spec, b_spec, o_spec, NT, (tm, r))
    if N_CHIP * r <= MM_K1:
        grid = (t // tm, 1, 1)
        a_spec = pl.BlockSpec((tm, c), lambda i, j, k: (i, 0))
        b_spec = pl.BlockSpec((N_CHIP, r, c), lambda i, j, k: (0, 0, 0))
        o_spec = pl.BlockSpec((tm, N_CHIP * r), lambda i, j, k: (i, 0))
        return _mm(name, dy, w4, (t, N_CHIP * r), out_dtype, grid, a_spec, b_spec, o_spec, NT, (tm, N_CHIP * r))
    grid = (t // tm, N_CHIP, 1)
    a_spec = pl.BlockSpec((tm, c), lambda i, j, k: (i, 0))
    b_spec = pl.BlockSpec((None, r, c), lambda i, j, k: (j, 0, 0))
    o_spec = pl.BlockSpec((tm, r), lambda i, j, k: (i, j))
    return _mm(name, dy, w4, (t, N_CHIP * r), out_dtype, grid, a_spec, b_spec, o_spec, NT, (tm, r))


def mm_dw(name, a, dy, mode, shape3):
    t = a.shape[0]
    _, r, c = shape3
    if mode == "col":
        tt = min(t, MM_TM)
        tn = _tile(c, MM_TN)
        npj = c // tn
        grid = (1, N_CHIP * npj, t // tt)
        a_spec = pl.BlockSpec((tt, r), lambda i, j, k: (k, 0))
        b_spec = pl.BlockSpec((tt, tn), lambda i, j, k: (k, j))
        o_spec = pl.BlockSpec((None, r, tn), lambda i, j, k: (j // npj, 0, j % npj))
        return _mm(name, a, dy, shape3, MXU_DTYPE, grid, a_spec, b_spec, o_spec, TN, (r, tn))
    if N_CHIP * r <= MM_K1:
        tt = min(t, 2 * MM_TM)
        grid = (1, 1, t // tt)
        a_spec = pl.BlockSpec((tt, N_CHIP * r), lambda i, j, k: (k, 0))
        b_spec = pl.BlockSpec((tt, c), lambda i, j, k: (k, 0))
        o_spec = pl.BlockSpec((N_CHIP, r, c), lambda i, j, k: (0, 0, 0))
        return _mm(name, a, dy, shape3, MXU_DTYPE, grid, a_spec, b_spec, o_spec, TN, (N_CHIP * r, c))
    tt = min(t, MM_TM)
    grid = (N_CHIP, 1, t // tt)
    a_spec = pl.BlockSpec((tt, r), lambda i, j, k: (k, i))
    b_spec = pl.BlockSpec((tt, c), lambda i, j, k: (k, 0))
    o_spec = pl.BlockSpec((None, r, c), lambda i, j, k: (i, 0, 0))
    return _mm(name, a, dy, shape3, MXU_DTYPE, grid, a_spec, b_spec, o_spec, TN, (r, c))


def _rowwise(name, fn, rows, pars, outs, accs=(), tm=256, ncol=1):
    t = rows[0][0].shape[0]
    nrow, npar, nout = len(rows), len(pars), len(outs)

    def body(*refs):
        vals = [r[...] for r in refs[:nrow + npar]]
        res = fn(*vals)
        out_refs = refs[nrow + npar:nrow + npar + nout]
        acc_refs = refs[nrow + npar + nout:]
        for o, v in zip(out_refs, res[:nout]):
            o[...] = v.astype(o.dtype)
        i = pl.program_id(1)
        for a, v in zip(acc_refs, res[nout:]):
            @pl.when(i == 0)
            def _(a=a, v=v):
                a[...] = v.astype(F32)

            @pl.when(i > 0)
            def _(a=a, v=v):
                a[...] += v.astype(F32)

    in_specs = [pl.BlockSpec((tm, w), functools.partial(lambda j, i, b: (i, b + j), b=b)) for _, w, b in rows]
    for arr, w in pars:
        if w is None:
            in_specs.append(pl.BlockSpec(arr.shape, lambda j, i: (0, 0)))
        else:
            in_specs.append(pl.BlockSpec((1, w), lambda j, i: (0, j)))
    out_specs = [pl.BlockSpec((tm, w), lambda j, i: (i, j)) for _, w, _ in outs]
    out_specs += [pl.BlockSpec((1, w), lambda j, i: (0, j)) for _, w in accs]
    out_shape = [jax.ShapeDtypeStruct((t, tw), dt) for tw, _, dt in outs]
    out_shape += [jax.ShapeDtypeStruct((1, tw), F32) for tw, _ in accs]
    res = pl.pallas_call(
        body, name=name, grid=(ncol, t // tm), in_specs=in_specs, out_specs=out_specs, out_shape=out_shape,
        compiler_params=_params(("parallel", "arbitrary" if accs else "parallel")),
    )(*[r[0] for r in rows], *[p[0] for p in pars])
    return res


def _rms(x, g):
    xf = x.astype(F32)
    return xf * lax.rsqrt(jnp.mean(xf * xf, axis=-1, keepdims=True) + NORM_EPS) * g


def _silu(x):
    return x * jax.nn.sigmoid(x)


def rms_fwd(name, x, g):
    d = x.shape[1]
    return _rowwise(name, lambda x, g: (_rms(x, g),), [(x, d, 0)], [(g, None)], [(d, d, MXU_DTYPE)], tm=512)[0]


def rms_bwd(name, x, g, dh, dres):
    d = x.shape[1]

    def fn(x, dh, dres, g):
        _, vjp = jax.vjp(_rms, x, g)
        dx, dg = vjp(dh.astype(F32))
        return dres + dx, dg

    return _rowwise(name, fn, [(x, d, 0), (dh, d, 0), (dres, d, 0)], [(g, None)], [(d, d, F32)], [(d, d)], tm=256)


def post_fwd(name, x, y, g, scale):
    d = x.shape[1]
    return _rowwise(name, lambda x, y, g: (x + scale * _rms(y, g),), [(x, d, 0), (y, d, 0)], [(g, None)],
                    [(d, d, F32)], tm=512)[0]


def post_bwd(name, y, g, dx, scale):
    d = y.shape[1]

    def fn(y, dx, g):
        _, vjp = jax.vjp(lambda y, g: scale * _rms(y, g), y, g)
        return vjp(dx)

    return _rowwise(name, fn, [(y, d, 0), (dx, d, 0)], [(g, None)], [(d, d, MXU_DTYPE)], [(d, d)], tm=256)


def _swiglu(gu):
    f = gu.shape[1] // 2
    return _silu(gu[:, :f].astype(F32)) * gu[:, f:].astype(F32)


def swiglu_fwd(name, gu):
    w = gu.shape[1]
    return _rowwise(name, lambda gu: (_swiglu(gu),), [(gu, w, 0)], [], [(w // 2, w // 2, MXU_DTYPE)], tm=256)[0]


def swiglu_bwd(name, gu, da):
    w = gu.shape[1]
    f = w // 2

    def fn(gu, da):
        gate, up = gu[:, :f].astype(F32), gu[:, f:].astype(F32)
        _, vjp = jax.vjp(lambda a, b: _silu(a) * b, gate, up)
        dgate, dup = vjp(da.astype(F32))
        return (jnp.concatenate([dgate, dup], axis=-1),)

    return _rowwise(name, fn, [(gu, w, 0), (da, f, 0)], [], [(w, w, MXU_DTYPE)], tm=128)[0]


def _head_gate(o, g):
    mu = jnp.mean(o, axis=-1, keepdims=True)
    var = jnp.mean(jnp.square(o - mu), axis=-1, keepdims=True)
    return _silu(g.astype(F32)) * ((o - mu) * lax.rsqrt(var + LN_EPS))


def head_gate_fwd(name, o, p, gate_blk):
    dv = RET_V_DIM
    return _rowwise(name, lambda o, g: (_head_gate(o, g),), [(o, dv, 0), (p, dv, gate_blk)], [],
                    [(RET_HEADS * dv, dv, MXU_DTYPE)], tm=512, ncol=RET_HEADS)[0]


def head_gate_bwd(name, o, p, gate_blk, da):
    dv = RET_V_DIM

    def fn(o, g, da):
        _, vjp = jax.vjp(_head_gate, o, g.astype(F32))
        return vjp(da.astype(F32))

    w = RET_HEADS * dv
    return _rowwise(name, fn, [(o, dv, 0), (p, dv, gate_blk), (da, dv, 0)], [],
                    [(w, dv, MXU_DTYPE), (w, dv, MXU_DTYPE)], tm=512, ncol=RET_HEADS)


def _ln_silu(u, g, b):
    mu = jnp.mean(u, axis=-1, keepdims=True)
    var = jnp.mean(jnp.square(u - mu), axis=-1, keepdims=True)
    return _silu((u - mu) * lax.rsqrt(var + LN_EPS) * g + b)


def ln_silu_fwd(name, u, g, b):
    d = u.shape[1]
    return _rowwise(name, lambda u, g, b: (_ln_silu(u, g, b),), [(u, d, 0)], [(g, None), (b, None)],
                    [(d, d, MXU_DTYPE)], tm=512)[0]


def ln_silu_bwd(name, u, g, b, dc):
    d = u.shape[1]

    def fn(u, dc, g, b):
        _, vjp = jax.vjp(_ln_silu, u, g, b)
        return vjp(dc.astype(F32))

    return _rowwise(name, fn, [(u, d, 0), (dc, d, 0)], [(g, None), (b, None)], [(d, d, F32)], [(d, d), (d, d)],
                    tm=256)


def _merge(g0, g1, g2, ya, yb, yc):
    s = jax.nn.sigmoid
    return s(g0.astype(F32)) * ya + s(g1.astype(F32)) * yb + s(g2.astype(F32)) * yc


def merge_fwd(name, p, blk, ya, yb, yc):
    d = ya.shape[1]
    rows = [(p, d, blk), (p, d, blk + 1), (p, d, blk + 2), (ya, d, 0), (yb, d, 0), (yc, d, 0)]
    return _rowwise(name, lambda *v: (_merge(*v),), rows, [], [(d, d, MXU_DTYPE)], tm=256)[0]


def merge_bwd(name, p, blk, ya, yb, yc, dmg):
    d = ya.shape[1]

    def fn(g0, g1, g2, ya, yb, yc, dmg):
        _, vjp = jax.vjp(_merge, g0.astype(F32), g1.astype(F32), g2.astype(F32), ya, yb, yc)
        return vjp(dmg.astype(F32))

    rows = [(p, d, blk), (p, d, blk + 1), (p, d, blk + 2), (ya, d, 0), (yb, d, 0), (yc, d, 0), (dmg, d, 0)]
    return _rowwise(name, fn, rows, [], [(d, d, MXU_DTYPE)] * 6, tm=256)


def loss_head(name, y, target):
    t, d = y.shape
    tm = 512

    def body(y_ref, t_ref, dy_ref, loss_ref):
        err = y_ref[...] - t_ref[...]
        dy_ref[...] = err * (1.0 / d)
        part = jnp.sum(jnp.sum(err * err, axis=1, keepdims=True), axis=0, keepdims=True) * (0.5 / d)

        @pl.when(pl.program_id(0) == 0)
        def _():
            loss_ref[...] = part

        @pl.when(pl.program_id(0) > 0)
        def _():
            loss_ref[...] += part

    return pl.pallas_call(
        body, name=name, grid=(t // tm,),
        in_specs=[pl.BlockSpec((tm, d), lambda i: (i, 0))] * 2,
        out_specs=[pl.BlockSpec((tm, d), lambda i: (i, 0)), pl.BlockSpec((1, 1), lambda i: (0, 0))],
        out_shape=[jax.ShapeDtypeStruct((t, d), F32), jax.ShapeDtypeStruct((1, 1), F32)],
        compiler_params=_params(("arbitrary",)),
    )(y, target)


def _rot(x, cos2, sin2):
    return x * cos2 + pltpu.roll(x, RET_QK_DIM // 2, 1) * sin2


def _decay_mask(lg, n0, rows, cols):
    n = n0 + lax.broadcasted_iota(jnp.int32, (rows, cols), 0)
    m = lax.broadcasted_iota(jnp.int32, (rows, cols), 1)
    shift = CHUNK.bit_length() - 1
    dist = jnp.abs(n - m).astype(F32)
    return jnp.where((m >> shift) <= (n >> shift), jnp.exp(lg * dist), 0.0)


def _ret_specs(s):
    dk, dv, h = RET_QK_DIM, RET_V_DIM, RET_HEADS
    return [
        pl.BlockSpec((s, dk), lambda b, hh: (b, hh)),
        pl.BlockSpec((s, dk), lambda b, hh: (b, h + hh)),
        pl.BlockSpec((s, dv), lambda b, hh: (b, (2 * h * dk) // dv + hh)),
        pl.BlockSpec((s, dk), lambda b, hh: (b, 0)),
        pl.BlockSpec((s, dk), lambda b, hh: (b, 0)),
        pl.BlockSpec((None, 1, dk), lambda b, hh: (hh, 0, 0)),
    ]


def retention_fwd(name, p, cos2, sin2, log_g, nb, s):
    dk, dv, h = RET_QK_DIM, RET_V_DIM, RET_HEADS

    def body(q_ref, k_ref, v_ref, cos_ref, sin_ref, lg_ref, o_ref, kr_ref):
        lg = lg_ref[0:1, 0:1]
        kr = _rot(k_ref[...].astype(F32), cos_ref[...], sin_ref[...]) * (dk ** -0.5)
        kr_ref[...] = kr.astype(kr_ref.dtype)
        for qi in range(s // RET_TQ):
            n0, kmax = qi * RET_TQ, (qi + 1) * RET_TQ
            rows = pl.ds(n0, RET_TQ)
            qr = _rot(q_ref[rows, :].astype(F32), cos_ref[rows, :], sin_ref[rows, :]).astype(MXU_DTYPE)
            sc = lax.dot_general(qr, kr_ref[0:kmax, :], NT, preferred_element_type=F32)
            pm = (sc * _decay_mask(lg, n0, RET_TQ, kmax)).astype(MXU_DTYPE)
            o_ref[rows, :] = lax.dot_general(pm, v_ref[0:kmax, :], NN, preferred_element_type=F32)

    return pl.pallas_call(
        body, name=name, grid=(nb, h), in_specs=_ret_specs(s),
        out_specs=pl.BlockSpec((s, dv), lambda b, hh: (b, hh)),
        out_shape=jax.ShapeDtypeStruct((nb * s, h * dv), F32),
        scratch_shapes=[pltpu.VMEM((s, dk), MXU_DTYPE)],
        compiler_params=_params(("parallel", "parallel")),
    )(p, p, p, cos2, sin2, log_g)


def retention_bwd(name, p, cos2, sin2, log_g, do, nb, s):
    dk, dv, h = RET_QK_DIM, RET_V_DIM, RET_HEADS

    def body(q_ref, k_ref, v_ref, cos_ref, sin_ref, lg_ref, do_ref, dq_ref, dk_ref, dv_ref, kr_ref, dk_acc, dv_acc):
        lg = lg_ref[0:1, 0:1]
        kr = _rot(k_ref[...].astype(F32), cos_ref[...], sin_ref[...]) * (dk ** -0.5)
        kr_ref[...] = kr.astype(kr_ref.dtype)
        dk_acc[...] = jnp.zeros_like(dk_acc)
        dv_acc[...] = jnp.zeros_like(dv_acc)
        for qi in range(s // RET_TQ):
            n0, kmax = qi * RET_TQ, (qi + 1) * RET_TQ
            rows = pl.ds(n0, RET_TQ)
            cq, sq = cos_ref[rows, :], sin_ref[rows, :]
            qr = _rot(q_ref[rows, :].astype(F32), cq, sq).astype(MXU_DTYPE)
            dob = do_ref[rows, :]
            mask = _decay_mask(lg, n0, RET_TQ, kmax)
            sc = lax.dot_general(qr, kr_ref[0:kmax, :], NT, preferred_element_type=F32)
            pm = (sc * mask).astype(MXU_DTYPE)
            dv_acc[0:kmax, :] += lax.dot_general(pm, dob, TN, preferred_element_type=F32)
            dp = lax.dot_general(dob, v_ref[0:kmax, :], NT, preferred_element_type=F32)
            ds = (dp * mask).astype(MXU_DTYPE)
            dqr = lax.dot_general(ds, kr_ref[0:kmax, :], NN, preferred_element_type=F32)
            dq_ref[rows, :] = _rot(dqr, cq, -sq).astype(dq_ref.dtype)
            dk_acc[0:kmax, :] += lax.dot_general(ds, qr, TN, preferred_element_type=F32)
        dkr = dk_acc[...] * (dk ** -0.5)
        dk_ref[...] = _rot(dkr, cos_ref[...], -sin_ref[...]).astype(dk_ref.dtype)
        dv_ref[...] = dv_acc[...].astype(dv_ref.dtype)

    t = nb * s
    return pl.pallas_call(
        body, name=name, grid=(nb, h),
        in_specs=_ret_specs(s) + [pl.BlockSpec((s, dv), lambda b, hh: (b, hh))],
        out_specs=[pl.BlockSpec((s, dk), lambda b, hh: (b, hh)), pl.BlockSpec((s, dk), lambda b, hh: (b, hh)),
                   pl.BlockSpec((s, dv), lambda b, hh: (b, hh))],
        out_shape=[jax.ShapeDtypeStruct((t, h * dk), MXU_DTYPE), jax.ShapeDtypeStruct((t, h * dk), MXU_DTYPE),
                   jax.ShapeDtypeStruct((t, h * dv), MXU_DTYPE)],
        scratch_shapes=[pltpu.VMEM((s, dk), MXU_DTYPE), pltpu.VMEM((s, dk), F32), pltpu.VMEM((s, dv), F32)],
        compiler_params=_params(("parallel", "parallel")),
    )(p, p, p, cos2, sin2, log_g, do)


def _conv_grid(t, d, nb):
    s = t // nb
    ns, nc = s // CONV_TS, d // CONV_TC
    return s, ns, nc


def _causal_taps(pad_ref, w_ref, k):
    acc = None
    for j in range(k):
        term = w_ref[j:j + 1, :] * pad_ref[pl.ds(CONV_PAD - (k - 1) + j, CONV_TS), :]
        acc = term if acc is None else acc + term
    return acc


def _carry_past(pad_ref, s_idx):
    @pl.when(s_idx == 0)
    def _():
        pad_ref[0:CONV_PAD, :] = jnp.zeros((CONV_PAD, pad_ref.shape[1]), F32)

    @pl.when(s_idx > 0)
    def _():
        pad_ref[0:CONV_PAD, :] = pad_ref[CONV_TS:CONV_TS + CONV_PAD, :]


def _carry_future(pad_ref, s_idx):
    @pl.when(s_idx == 0)
    def _():
        pad_ref[CONV_TS:CONV_TS + CONV_PAD, :] = jnp.zeros((CONV_PAD, pad_ref.shape[1]), F32)

    @pl.when(s_idx > 0)
    def _():
        pad_ref[CONV_TS:CONV_TS + CONV_PAD, :] = pad_ref[0:CONV_PAD, :]


def _conv_bwd_taps(pad_ref, w_ref, x, dw_ref, k):
    acc = None
    for j in range(k):
        sh = pad_ref[pl.ds(k - 1 - j, CONV_TS), :]
        term = w_ref[j:j + 1, :] * sh
        acc = term if acc is None else acc + term
        dw_ref[j:j + 1, :] += jnp.sum(x * sh, axis=0, keepdims=True)
    return acc


def short_conv_fwd(name, p, blk_b, w, nb):
    t = p.shape[0]
    d = w.shape[1]
    s, ns, nc = _conv_grid(t, d, nb)
    cb = d // CONV_TC

    def body(b_ref, c_ref, x_ref, w_ref, y_ref, cz_ref, pad_ref):
        _carry_past(pad_ref, pl.program_id(2))
        pad_ref[CONV_PAD:CONV_PAD + CONV_TS, :] = c_ref[...].astype(F32) * x_ref[...].astype(F32)
        cz = _causal_taps(pad_ref, w_ref, SC_KERNEL)
        cz_ref[...] = cz
        y_ref[...] = (b_ref[...].astype(F32) * cz).astype(y_ref.dtype)

    def pspec(off):
        return pl.BlockSpec((CONV_TS, CONV_TC), lambda c, b, si: (b * ns + si, (blk_b + off) * cb + c))

    ospec = pl.BlockSpec((CONV_TS, CONV_TC), lambda c, b, si: (b * ns + si, c))
    return pl.pallas_call(
        body, name=name, grid=(nc, nb, ns),
        in_specs=[pspec(0), pspec(1), pspec(2), pl.BlockSpec((SC_KERNEL, CONV_TC), lambda c, b, si: (0, c))],
        out_specs=[ospec, ospec],
        out_shape=[jax.ShapeDtypeStruct((t, d), MXU_DTYPE), jax.ShapeDtypeStruct((t, d), F32)],
        scratch_shapes=[pltpu.VMEM((CONV_PAD + CONV_TS, CONV_TC), F32)],
        compiler_params=_params(("parallel", "arbitrary", "arbitrary")),
    )(p, p, p, w)


def short_conv_bwd(name, p, blk_b, w, cz, dy, nb):
    t = p.shape[0]
    d = w.shape[1]
    s, ns, nc = _conv_grid(t, d, nb)
    cb = d // CONV_TC

    def body(b_ref, c_ref, x_ref, w_ref, cz_ref, dy_ref, db_ref, dc_ref, dx_ref, dw_ref, pad_ref):
        si = pl.program_id(2)
        _carry_future(pad_ref, si)
        dyv = dy_ref[...].astype(F32)
        cv, xv = c_ref[...].astype(F32), x_ref[...].astype(F32)
        db_ref[...] = (dyv * cz_ref[...]).astype(db_ref.dtype)
        pad_ref[0:CONV_TS, :] = dyv * b_ref[...].astype(F32)

        @pl.when(jnp.logical_and(pl.program_id(1) == 0, si == 0))
        def _():
            dw_ref[...] = jnp.zeros_like(dw_ref)

        dz = _conv_bwd_taps(pad_ref, w_ref, cv * xv, dw_ref, SC_KERNEL)
        dc_ref[...] = (dz * xv).astype(dc_ref.dtype)
        dx_ref[...] = (dz * cv).astype(dx_ref.dtype)

    def row(b, si):
        return b * ns + (ns - 1 - si)

    def pspec(off):
        return pl.BlockSpec((CONV_TS, CONV_TC), lambda c, b, si: (row(b, si), (blk_b + off) * cb + c))

    ospec = pl.BlockSpec((CONV_TS, CONV_TC), lambda c, b, si: (row(b, si), c))
    wspec = pl.BlockSpec((SC_KERNEL, CONV_TC), lambda c, b, si: (0, c))
    return pl.pallas_call(
        body, name=name, grid=(nc, nb, ns),
        in_specs=[pspec(0), pspec(1), pspec(2), wspec, ospec, ospec],
        out_specs=[ospec, ospec, ospec, wspec],
        out_shape=[jax.ShapeDtypeStruct((t, d), MXU_DTYPE)] * 3 + [jax.ShapeDtypeStruct((SC_KERNEL, d), F32)],
        scratch_shapes=[pltpu.VMEM((CONV_TS + CONV_PAD, CONV_TC), F32)],
        compiler_params=_params(("parallel", "arbitrary", "arbitrary")),
    )(p, p, p, w, cz, dy)


def conformer_conv_fwd(name, p, blk_a, w, bias, nb):
    t = p.shape[0]
    d = w.shape[1]
    s, ns, nc = _conv_grid(t, d, nb)
    cb = d // CONV_TC

    def body(a_ref, b_ref, w_ref, bias_ref, u_ref, pad_ref):
        _carry_past(pad_ref, pl.program_id(2))
        pad_ref[CONV_PAD:CONV_PAD + CONV_TS, :] = a_ref[...].astype(F32) * jax.nn.sigmoid(b_ref[...].astype(F32))
        u_ref[...] = _causal_taps(pad_ref, w_ref, CF_KERNEL) + bias_ref[...]

    def pspec(off):
        return pl.BlockSpec((CONV_TS, CONV_TC), lambda c, b, si: (b * ns + si, (blk_a + off) * cb + c))

    return pl.pallas_call(
        body, name=name, grid=(nc, nb, ns),
        in_specs=[pspec(0), pspec(1), pl.BlockSpec((CF_KERNEL, CONV_TC), lambda c, b, si: (0, c)),
                  pl.BlockSpec((1, CONV_TC), lambda c, b, si: (0, c))],
        out_specs=pl.BlockSpec((CONV_TS, CONV_TC), lambda c, b, si: (b * ns + si, c)),
        out_shape=jax.ShapeDtypeStruct((t, d), F32),
        scratch_shapes=[pltpu.VMEM((CONV_PAD + CONV_TS, CONV_TC), F32)],
        compiler_params=_params(("parallel", "arbitrary", "arbitrary")),
    )(p, p, w, bias)


def conformer_conv_bwd(name, p, blk_a, w, du, nb):
    t = p.shape[0]
    d = w.shape[1]
    s, ns, nc = _conv_grid(t, d, nb)
    cb = d // CONV_TC

    def body(a_ref, b_ref, w_ref, du_ref, da_ref, db_ref, dw_ref, dbias_ref, pad_ref):
        si = pl.program_id(2)
        _carry_future(pad_ref, si)
        duv = du_ref[...]
        av = a_ref[...].astype(F32)
        sg = jax.nn.sigmoid(b_ref[...].astype(F32))
        pad_ref[0:CONV_TS, :] = duv

        @pl.when(jnp.logical_and(pl.program_id(1) == 0, si == 0))
        def _():
            dw_ref[...] = jnp.zeros_like(dw_ref)
            dbias_ref[...] = jnp.zeros_like(dbias_ref)

        du0 = _conv_bwd_taps(pad_ref, w_ref, av * sg, dw_ref, CF_KERNEL)
        da_ref[...] = (du0 * sg).astype(da_ref.dtype)
        db_ref[...] = (du0 * av * sg * (1.0 - sg)).astype(db_ref.dtype)
        dbias_ref[...] += jnp.sum(duv, axis=0, keepdims=True)

    def row(b, si):
        return b * ns + (ns - 1 - si)

    def pspec(off):
        return pl.BlockSpec((CONV_TS, CONV_TC), lambda c, b, si: (row(b, si), (blk_a + off) * cb + c))

    ospec = pl.BlockSpec((CONV_TS, CONV_TC), lambda c, b, si: (row(b, si), c))
    wspec = pl.BlockSpec((CF_KERNEL, CONV_TC), lambda c, b, si: (0, c))
    bspec = pl.BlockSpec((1, CONV_TC), lambda c, b, si: (0, c))
    return pl.pallas_call(
        body, name=name, grid=(nc, nb, ns),
        in_specs=[pspec(0), pspec(1), wspec, ospec],
        out_specs=[ospec, ospec, wspec, bspec],
        out_shape=[jax.ShapeDtypeStruct((t, d), MXU_DTYPE)] * 2
        + [jax.ShapeDtypeStruct((CF_KERNEL, d), F32), jax.ShapeDtypeStruct((1, d), F32)],
        scratch_shapes=[pltpu.VMEM((CONV_TS + CONV_PAD, CONV_TC), F32)],
        compiler_params=_params(("parallel", "arbitrary", "arbitrary")),
    )(p, p, w, du)


BLOCKS = ("ffn1", "mixer", "ffn2")
BLOCK_WEIGHTS = {"ffn1": ("ffn1_w_gu", "ffn1_w_down"), "mixer": ("w_in", "w_ret_o", "w_sc_o", "w_cf_o", "w_o"),
                 "ffn2": ("ffn2_w_gu", "ffn2_w_down")}
BIG = BLOCK_WEIGHTS["ffn1"] + BLOCK_WEIGHTS["mixer"] + BLOCK_WEIGHTS["ffn2"]
MODE = {"ffn1_w_gu": "col", "ffn1_w_down": "row", "w_in": "col", "w_ret_o": "row", "w_sc_o": "row",
        "w_cf_o": "row", "w_o": "row", "ffn2_w_gu": "col", "ffn2_w_down": "row"}
NORM_OF = {"ffn1": 0, "mixer": 2, "ffn2": 4}
BLK_GATE, BLK_SCB, BLK_CFA, BLK_MERGE = 2, 3, 6, 8


def _rope_tables(positions):
    half = RET_QK_DIM // 2
    inv_freq = ROPE_BASE ** (-jnp.arange(half, dtype=F32) / half)
    ang = positions.astype(F32)[..., None] * inv_freq
    cos, sin = jnp.cos(ang), jnp.sin(ang)
    nb, s = positions.shape
    cos2 = jnp.concatenate([cos, cos], axis=-1).reshape(nb * s, RET_QK_DIM)
    sin2 = jnp.concatenate([-sin, sin], axis=-1).reshape(nb * s, RET_QK_DIM)
    return cos2, sin2


def _log_gamma():
    lg = jnp.log(1.0 - 2.0 ** (-5.0 - jnp.arange(RET_HEADS, dtype=F32)))
    return jnp.broadcast_to(lg[:, None, None], (RET_HEADS, 1, RET_QK_DIM))


def _ffn_fwd(xs, w, tag, g_pre, g_post):
    h = rms_fwd("ffn_rms", xs, g_pre)
    gu = mm_fwd("ffn_gu", h, w[tag + "_w_gu"], "col", MXU_DTYPE)
    a = swiglu_fwd("ffn_act", gu)
    y = mm_fwd("ffn_down", a, w[tag + "_w_down"], "row", F32)
    out = post_fwd("ffn_post", xs, y, g_post, 0.5)
    return out, dict(x=xs, h=h, gu=gu, a=a, y=y, w=w)


def _ffn_bwd(dxs, sv, tag, g_pre, g_post):
    w = sv["w"]
    gu_w, down_w = w[tag + "_w_gu"], w[tag + "_w_down"]
    dy, dg_post = post_bwd("ffn_post_bwd", sv["y"], g_post, dxs, 0.5)
    da = mm_dx("ffn_down_dx", dy, down_w, "row", MXU_DTYPE)
    grads = {tag + "_w_down": mm_dw("ffn_down_dw", sv["a"], dy, "row", down_w.shape)}
    dgu = swiglu_bwd("ffn_act_bwd", sv["gu"], da)
    dh = mm_dx("ffn_gu_dx", dgu, gu_w, "col", F32)
    grads[tag + "_w_gu"] = mm_dw("ffn_gu_dw", sv["h"], dgu, "col", gu_w.shape)
    dxs, dg_pre = rms_bwd("ffn_rms_bwd", sv["x"], g_pre, dh, dxs)
    return dxs, grads, dg_pre, dg_post


def _mixer_fwd(xs, w, sm, g_pre, g_post, rope, nb, s):
    cos2, sin2, log_g = rope
    d = xs.shape[1]
    gate_blk = (BLK_GATE * d) // RET_V_DIM
    h = rms_fwd("mx_rms", xs, g_pre)
    p = mm_fwd("mx_in", h, w["w_in"], "col", MXU_DTYPE)
    o = retention_fwd("ret_fwd", p, cos2, sin2, log_g, nb, s)
    ya_in = head_gate_fwd("ret_gate", o, p, gate_blk)
    yb_in, cz = short_conv_fwd("sc_fwd", p, BLK_SCB, sm["sc_conv_w"], nb)
    u1 = conformer_conv_fwd("cf_fwd", p, BLK_CFA, sm["cf_dw_w"], sm["cf_dw_b"], nb)
    yc_in = ln_silu_fwd("cf_ln", u1, sm["cf_ln_g"], sm["cf_ln_b"])
    ya = mm_fwd("mx_proj", ya_in, w["w_ret_o"], "row", F32)
    yb = mm_fwd("mx_proj", yb_in, w["w_sc_o"], "row", F32)
    yc = mm_fwd("mx_proj", yc_in, w["w_cf_o"], "row", F32)
    mg = merge_fwd("mx_merge", p, BLK_MERGE, ya, yb, yc)
    m = mm_fwd("mx_proj", mg, w["w_o"], "row", F32)
    out = post_fwd("mx_post", xs, m, g_post, 1.0)
    return out, dict(x=xs, h=h, p=p, o=o, ya_in=ya_in, yb_in=yb_in, cz=cz, u1=u1, yc_in=yc_in, ya=ya, yb=yb, yc=yc,
                     mg=mg, m=m, w=w)


def _mixer_bwd(dxs, sv, sm, g_pre, g_post, rope, nb, s):
    cos2, sin2, log_g = rope
    w, p = sv["w"], sv["p"]
    d = dxs.shape[1]
    gate_blk = (BLK_GATE * d) // RET_V_DIM
    grads, gsm = {}, {}

    def proj_bwd(wname, a_in, dy, out_dtype):
        grads[wname] = mm_dw("mx_proj_dw", a_in, dy, "row", w[wname].shape)
        return mm_dx("mx_proj_dx", dy, w[wname], "row", out_dtype)

    dm, dg_post = post_bwd("mx_post_bwd", sv["m"], g_post, dxs, 1.0)
    dmg = proj_bwd("w_o", sv["mg"], dm, MXU_DTYPE)
    dg0, dg1, dg2, dya, dyb, dyc = merge_bwd("mx_merge_bwd", p, BLK_MERGE, sv["ya"], sv["yb"], sv["yc"], dmg)
    dya_in = proj_bwd("w_ret_o", sv["ya_in"], dya, MXU_DTYPE)
    dyb_in = proj_bwd("w_sc_o", sv["yb_in"], dyb, MXU_DTYPE)
    dyc_in = proj_bwd("w_cf_o", sv["yc_in"], dyc, MXU_DTYPE)
    do, dgret = head_gate_bwd("ret_gate_bwd", sv["o"], p, gate_blk, dya_in)
    dq, dk, dv = retention_bwd("ret_bwd", p, cos2, sin2, log_g, do, nb, s)
    dscb, dscc, dscx, gsm["sc_conv_w"] = short_conv_bwd("sc_bwd", p, BLK_SCB, sm["sc_conv_w"], sv["cz"], dyb_in, nb)
    du1, dlg, dlb = ln_silu_bwd("cf_ln_bwd", sv["u1"], sm["cf_ln_g"], sm["cf_ln_b"], dyc_in)
    dcfa, dcfb, gsm["cf_dw_w"], dbias = conformer_conv_bwd("cf_bwd", p, BLK_CFA, sm["cf_dw_w"], du1, nb)
    gsm.update(cf_ln_g=dlg[0], cf_ln_b=dlb[0], cf_dw_b=dbias[0])
    dp = jnp.concatenate([dq, dk, dv, dgret, dscb, dscc, dscx, dcfa, dcfb, dg0, dg1, dg2], axis=1)
    dh = mm_dx("mx_in_dx", dp, w["w_in"], "col", F32)
    grads["w_in"] = mm_dw("mx_in_dw", sv["h"], dp, "col", w["w_in"].shape)
    dxs, dg_pre = rms_bwd("mx_rms_bwd", sv["x"], g_pre, dh, dxs)
    return dxs, grads, gsm, dg_pre, dg_post


def local_step(x, positions, target, small, fetch, push):
    nb, s, d = x.shape
    t = nb * s
    depth = small["norm_g"].shape[0]
    rope = _rope_tables(positions) + (_log_gamma(),)
    xs = x.reshape(t, d)
    token = [None]

    def gain(l, i):
        g = small["norm_g"][l, i][None, :]
        if token[0] is not None:
            g, token[0] = g + token[0], None
        return g

    def mixer_small(l):
        return dict(sc_conv_w=small["sc_conv_w"][l], cf_dw_w=small["cf_dw_w"][l], cf_dw_b=small["cf_dw_b"][l][None, :],
                    cf_ln_g=small["cf_ln_g"][l][None, :], cf_ln_b=small["cf_ln_b"][l][None, :])

    saved = {}
    for l in range(depth):
        for blk in BLOCKS:
            w = fetch(l, blk, xs)
            i0 = NORM_OF[blk]
            if blk == "mixer":
                xs, saved[l, blk] = _mixer_fwd(xs, w, mixer_small(l), gain(l, i0), gain(l, i0 + 1), rope, nb, s)
            else:
                xs, saved[l, blk] = _ffn_fwd(xs, w, blk, gain(l, i0), gain(l, i0 + 1))

    dxs, loss = loss_head("loss", xs, target.reshape(t, d))

    dnorm = [[None] * 6 for _ in range(depth)]
    gsmall = {n: [None] * depth for n in ("sc_conv_w", "cf_dw_w", "cf_dw_b", "cf_ln_g", "cf_ln_b")}
    for l in reversed(range(depth)):
        for blk in reversed(BLOCKS):
            i0 = NORM_OF[blk]
            g_post, g_pre = gain(l, i0 + 1), gain(l, i0)
            if blk == "mixer":
                dxs, grads, gsm, dnorm[l][i0], dnorm[l][i0 + 1] = _mixer_bwd(
                    dxs, saved[l, blk], mixer_small(l), g_pre, g_post, rope, nb, s)
                for n, v in gsm.items():
                    gsmall[n][l] = v
            else:
                dxs, grads, dnorm[l][i0], dnorm[l][i0 + 1] = _ffn_bwd(dxs, saved[l, blk], blk, g_pre, g_post)
            token[0] = push(l, blk, grads)

    gs = {n: jnp.stack(v) for n, v in gsmall.items()}
    gs["norm_g"] = jnp.stack([jnp.concatenate(r, axis=0) for r in dnorm])
    return loss, dxs.reshape(nb, s, d), gs


ANY = pl.BlockSpec(memory_space=pl.ANY)
HBM = pl.BlockSpec(memory_space=pltpu.HBM)
SEM = pl.BlockSpec(memory_space=pltpu.SEMAPHORE)
VMEM_WHOLE = pl.BlockSpec(memory_space=pltpu.VMEM)
EFFECT = pltpu.SideEffectType.DATAFLOW_SIDE_EFFECTING
TOKEN = jax.ShapeDtypeStruct((8, 128), F32)


def _other_chips(x, y):
    return [(1 - x, y), (x, 1 - y), (1 - x, 1 - y)]


def _remote(src, dst, send_sem, recv_sem, to):
    return pltpu.make_async_remote_copy(src_ref=src, dst_ref=dst, send_sem=send_sem, recv_sem=recv_sem,
                                        device_id=to, device_id_type=MESH)


def _in_hbm(v):
    return pltpu.with_memory_space_constraint(v, pltpu.HBM)


def place_quarter(w, layer, ids):
    _, r, c = w.shape
    tr = min(r, 256)

    def body(ids_ref, w_ref, o_ref):
        o_ref[...] = w_ref[...].astype(o_ref.dtype)

    return pl.pallas_call(
        body, name="place_quarter",
        grid_spec=pltpu.PrefetchScalarGridSpec(
            num_scalar_prefetch=1, grid=(r // tr,),
            in_specs=[pl.BlockSpec((None, tr, c), lambda i, ids_ref: (layer, i, 0))],
            out_specs=pl.BlockSpec((None, tr, c), lambda i, ids_ref: (ids_ref[0], i, 0))),
        out_shape=jax.ShapeDtypeStruct((N_CHIP, r, c), MXU_DTYPE),
        compiler_params=_params(("parallel",)),
    )(ids, w)


def _gather_copies(lands, send, recv):
    x, y, c = _axes()
    me = 2 * x + y
    mine, theirs = [], []
    for a, ld in enumerate(lands):
        rh = ld.shape[1] // 2
        rows = pl.ds(c * rh, rh)
        for k, (px, py) in enumerate(_other_chips(x, y)):
            to = (px, py, c)
            mine.append(_remote(ld.at[me, rows, :], ld.at[me, rows, :], send.at[3 * a + k], recv.at[3 * a + k], to))
            got = ld.at[2 * px + py, rows, :]
            theirs.append(_remote(got, got, send.at[3 * a + k], recv.at[3 * a + k], to))
    return mine, theirs


def gather_start(groups, after):
    flat = [s for g in groups for s in g]
    n, ng = len(flat), len(groups)
    sizes = [len(g) for g in groups]

    def body(*refs):
        lands = refs[:n]
        sems = refs[n + 1:n + 1 + 2 * ng]
        token = refs[-1]
        at = 0
        for g, m in enumerate(sizes):
            mine, _ = _gather_copies(lands[at:at + m], sems[2 * g], sems[2 * g + 1])
            for cp in mine:
                cp.start()
            at += m
        token[...] = jnp.zeros_like(token)

    sem_shapes = []
    for m in sizes:
        sem_shapes += [pltpu.SemaphoreType.DMA((3 * m,))] * 2
    res = pl.pallas_call(
        body, name="gather_start", in_specs=[HBM] * n + [ANY],
        out_specs=[SEM] * (2 * ng) + [HBM] * n + [VMEM_WHOLE],
        out_shape=sem_shapes + [pltpu.HBM(s.shape, s.dtype) for s in flat] + [TOKEN],
        input_output_aliases={i: 2 * ng + i for i in range(n)},
        compiler_params=pltpu.CompilerParams(has_side_effects=EFFECT),
    )(*[_in_hbm(s) for s in flat], after)
    sems, thru, token = res[:2 * ng], res[2 * ng:2 * ng + n], res[-1]
    out, at = [], 0
    for g, m in enumerate(sizes):
        out.append((sems[2 * g], sems[2 * g + 1], thru[at:at + m]))
        at += m
    return out, token


def gather_wait(lands, send, recv, after):
    m = len(lands)

    def body(*refs):
        mine, theirs = _gather_copies(refs[:m], refs[m], refs[m + 1])
        for cp in mine:
            cp.wait_send()
        for cp in theirs:
            cp.wait_recv()

    return pl.pallas_call(
        body, name="gather_wait", in_specs=[HBM] * m + [SEM, SEM, ANY], out_specs=[HBM] * m,
        out_shape=[pltpu.HBM(l.shape, l.dtype) for l in lands],
        input_output_aliases={i: i for i in range(m)},
        compiler_params=pltpu.CompilerParams(has_side_effects=EFFECT),
    )(*lands, send, recv, after)


def sibling_fill(lands):
    m = len(lands)

    def body(*refs):
        lds = refs[:m]
        send, recv = refs[2 * m:]
        x, y, c = _axes()
        sib = (x, y, 1 - c)
        cps = []
        for a in range(m):
            rh = lds[a].shape[1] // 2
            for k, (px, py) in enumerate(_other_chips(x, y)):
                got = lds[a].at[2 * px + py, pl.ds(c * rh, rh), :]
                cp = _remote(got, got, send.at[3 * a + k], recv.at[3 * a + k], sib)
                cp.start()
                cps.append(cp)
        for a in range(m):
            rh = lds[a].shape[1] // 2
            for k, (px, py) in enumerate(_other_chips(x, y)):
                blk = lds[a].at[2 * px + py, pl.ds((1 - c) * rh, rh), :]
                _remote(blk, blk, send.at[3 * a + k], recv.at[3 * a + k], sib).wait_recv()
        for cp in cps:
            cp.wait_send()

    return pl.pallas_call(
        body, name="sibling_fill", in_specs=[ANY] * m, out_specs=[ANY] * m,
        out_shape=[jax.ShapeDtypeStruct(l.shape, l.dtype) for l in lands],
        input_output_aliases={i: i for i in range(m)},
        scratch_shapes=[pltpu.SemaphoreType.DMA((3 * m,))] * 2,
    )(*lands)


def exchange_sibling_halves(grads):
    n = len(grads)

    def body(*refs):
        ins, outs = refs[:n], refs[n:2 * n]
        send, recv = refs[2 * n:]
        x, y, c = _axes()
        cps = []
        for a in range(n):
            rh = ins[a].shape[1] // 2
            cp = _remote(ins[a].at[:, pl.ds((1 - c) * rh, rh), :], outs[a], send.at[a], recv.at[a], (x, y, 1 - c))
            cp.start()
            cps.append(cp)
        for cp in cps:
            cp.wait()

    return pl.pallas_call(
        body, name="exchange_sibling_halves", in_specs=[ANY] * n, out_specs=[ANY] * n,
        out_shape=[jax.ShapeDtypeStruct((g.shape[0], g.shape[1] // 2, g.shape[2]), g.dtype) for g in grads],
        scratch_shapes=[pltpu.SemaphoreType.DMA((n,))] * 2,
    )(*grads)


def add_halves(g4, land, ids):
    nq, r, c = g4.shape
    rh = r // 2

    def body(ids_ref, a_ref, b_ref, o_ref):
        o_ref[...] = (a_ref[...].astype(F32) + b_ref[...].astype(F32)).astype(o_ref.dtype)

    return pl.pallas_call(
        body, name="add_halves",
        grid_spec=pltpu.PrefetchScalarGridSpec(
            num_scalar_prefetch=1, grid=(nq,),
            in_specs=[pl.BlockSpec((None, rh, c), lambda i, ids_ref: (i, ids_ref[1], 0)),
                      pl.BlockSpec((None, rh, c), lambda i, ids_ref: (i, 0, 0))],
            out_specs=pl.BlockSpec((None, rh, c), lambda i, ids_ref: (i, 0, 0))),
        out_shape=jax.ShapeDtypeStruct((nq, rh, c), g4.dtype),
        compiler_params=_params(("parallel",)),
    )(ids, g4, land)


def _scatter_copies(parts, lands, send, recv):
    x, y, c = _axes()
    cps = []
    for a, (pt, ld) in enumerate(zip(parts, lands)):
        for k, (px, py) in enumerate(_other_chips(x, y)):
            cps.append(_remote(pt.at[2 * px + py], ld.at[k], send.at[3 * a + k], recv.at[3 * a + k], (px, py, c)))
    return cps


def scatter_start(parts):
    m = len(parts)

    def body(*refs):
        for cp in _scatter_copies(refs[:m], refs[m:2 * m], refs[2 * m], refs[2 * m + 1]):
            cp.start()
        refs[-1][...] = jnp.zeros_like(refs[-1])

    lands = [lax.empty((N_CHIP - 1,) + p.shape[1:], p.dtype) for p in parts]
    res = pl.pallas_call(
        body, name="scatter_start", in_specs=[HBM] * (2 * m), out_specs=[SEM, SEM] + [HBM] * (2 * m) + [VMEM_WHOLE],
        out_shape=[pltpu.SemaphoreType.DMA((3 * m,))] * 2 + [pltpu.HBM(p.shape, p.dtype) for p in parts]
        + [pltpu.HBM(l.shape, l.dtype) for l in lands] + [TOKEN],
        input_output_aliases={i: 2 + i for i in range(2 * m)},
        compiler_params=pltpu.CompilerParams(has_side_effects=EFFECT),
    )(*[_in_hbm(p) for p in parts], *[_in_hbm(l) for l in lands])
    return res[0], res[1], res[2:2 + m], res[2 + m:2 + 2 * m], res[-1]


def scatter_wait(parts, lands, send, recv, after):
    m = len(parts)

    def body(*refs):
        for cp in _scatter_copies(refs[:m], refs[m:2 * m], refs[2 * m], refs[2 * m + 1]):
            cp.wait_send()
            cp.wait_recv()

    res = pl.pallas_call(
        body, name="scatter_wait", in_specs=[HBM] * (2 * m) + [SEM, SEM, ANY], out_specs=[HBM] * (2 * m),
        out_shape=[pltpu.HBM(p.shape, p.dtype) for p in parts] + [pltpu.HBM(l.shape, l.dtype) for l in lands],
        input_output_aliases={i: i for i in range(2 * m)},
        compiler_params=pltpu.CompilerParams(has_side_effects=EFFECT),
    )(*parts, *lands, send, recv, after)
    return res[:m], res[m:]


def sum_partials(part, land, ids, layer, depth, into):
    _, rh, c = part.shape
    tr = min(rh, 256)
    nt = rh // tr

    def body(ids_ref, p_ref, l_ref, *rest):
        o_ref = rest[-1]
        acc = p_ref[...].astype(F32)
        for k in range(N_CHIP - 1):
            acc = acc + l_ref[k].astype(F32)
        o_ref[...] = acc

    in_specs = [pl.BlockSpec((None, tr, c), lambda i, ids_ref: (ids_ref[0], i, 0)),
                pl.BlockSpec((N_CHIP - 1, tr, c), lambda i, ids_ref: (0, i, 0))]
    args = [ids, part, land]
    aliases = {}
    if into is not None:
        in_specs.append(ANY)
        args.append(into)
        aliases = {3: 0}
    return pl.pallas_call(
        body, name="sum_partials",
        grid_spec=pltpu.PrefetchScalarGridSpec(
            num_scalar_prefetch=1, grid=(nt,), in_specs=in_specs,
            out_specs=pl.BlockSpec((None, tr, c), lambda i, ids_ref: (layer, ids_ref[1] * nt + i, 0))),
        out_shape=jax.ShapeDtypeStruct((depth, 2 * rh, c), F32), input_output_aliases=aliases,
        compiler_params=_params(("parallel",)),
    )(*args)


def exchange_final_halves(bufs, layers):
    n = len(bufs)

    def body(*refs):
        outs = refs[n:2 * n]
        send, recv = refs[2 * n:]
        x, y, c = _axes()
        sib = (x, y, 1 - c)
        cps, at = [], 0
        for a in range(n):
            rh = outs[a].shape[1] // 2
            for l in layers[a]:
                mine = outs[a].at[l, pl.ds(c * rh, rh), :]
                cp = _remote(mine, mine, send.at[at], recv.at[at], sib)
                cp.start()
                cps.append(cp)
                at += 1
        at = 0
        for a in range(n):
            rh = outs[a].shape[1] // 2
            for l in layers[a]:
                theirs = outs[a].at[l, pl.ds((1 - c) * rh, rh), :]
                _remote(theirs, theirs, send.at[at], recv.at[at], sib).wait_recv()
                at += 1
        for cp in cps:
            cp.wait_send()

    ncp = sum(len(ls) for ls in layers)
    return pl.pallas_call(
        body, name="exchange_final_halves", in_specs=[ANY] * n, out_specs=[ANY] * n,
        out_shape=[jax.ShapeDtypeStruct(g.shape, g.dtype) for g in bufs],
        input_output_aliases={i: i for i in range(n)},
        scratch_shapes=[pltpu.SemaphoreType.DMA((ncp,))] * 2,
    )(*bufs)


def allgather_small(pk):
    def body(in_ref, out_ref, send, recv):
        x, y, c = _axes()
        me = 2 * x + y
        chips = _other_chips(x, y)
        out_ref[pl.ds(me, 1)] = in_ref[...][None]
        cps = []
        for k, (px, py) in enumerate(chips):
            cp = _remote(in_ref, out_ref.at[me], send.at[k], recv.at[k], (px, py, c))
            cp.start()
            cps.append(cp)
        for k, (px, py) in enumerate(chips):
            got = out_ref.at[2 * px + py]
            _remote(got, got, send.at[k], recv.at[k], (px, py, c)).wait_recv()
        for cp in cps:
            cp.wait_send()

    return pl.pallas_call(
        body, name="allgather_small", in_specs=[VMEM_WHOLE], out_specs=VMEM_WHOLE,
        out_shape=jax.ShapeDtypeStruct((N_CHIP,) + pk.shape, pk.dtype),
        scratch_shapes=[pltpu.SemaphoreType.DMA((3,))] * 2,
    )(pk)


def allreduce_small(g):
    ndev = 8

    def body(in_ref, out_ref, slots, send, recv):
        x, y, c = _axes()
        me = 4 * x + 2 * y + c
        slots[pl.ds(me, 1)] = in_ref[...][None]
        peers = []
        for mask in range(1, ndev):
            px = 1 - x if mask & 4 else x
            py = 1 - y if mask & 2 else y
            pc = 1 - c if mask & 1 else c
            peers.append((px, py, pc))
        cps = []
        for k, peer in enumerate(peers):
            cp = _remote(in_ref, slots.at[me], send.at[k], recv.at[k], peer)
            cp.start()
            cps.append(cp)
        for k, (px, py, pc) in enumerate(peers):
            got = slots.at[4 * px + 2 * py + pc]
            _remote(got, got, send.at[k], recv.at[k], (px, py, pc)).wait_recv()
        for cp in cps:
            cp.wait_send()
        acc = slots[0]
        for d in range(1, ndev):
            acc = acc + slots[d]
        out_ref[...] = acc

    return pl.pallas_call(
        body, name="allreduce_small", in_specs=[VMEM_WHOLE], out_specs=VMEM_WHOLE,
        out_shape=jax.ShapeDtypeStruct(g.shape, g.dtype),
        scratch_shapes=[pltpu.VMEM((ndev,) + g.shape, g.dtype), pltpu.SemaphoreType.DMA((ndev - 1,)),
                        pltpu.SemaphoreType.DMA((ndev - 1,))],
    )(g)


def adamw(w, g, m, v):
    shape = w.shape
    cols = shape[-1]
    rows = int(np.prod(shape[:-1]))
    tr = rows
    for cand in (256, 128):
        if rows % cand == 0 and cand * cols * 4 <= 2 * 1024 * 1024:
            tr = cand
            break
    c1 = 1.0 - ADAM_B1 ** ADAM_STEP
    c2 = 1.0 - ADAM_B2 ** ADAM_STEP

    def body(w_ref, g_ref, m_ref, v_ref, d_ref, nm_ref, nv_ref):
        gv = g_ref[...]
        nm = ADAM_B1 * m_ref[...] + (1.0 - ADAM_B1) * gv
        nv = ADAM_B2 * v_ref[...] + (1.0 - ADAM_B2) * jnp.square(gv)
        d_ref[...] = -ADAM_LR * ((nm / c1) / (jnp.sqrt(nv / c2) + ADAM_EPS) + ADAM_WD * w_ref[...])
        nm_ref[...] = nm
        nv_ref[...] = nv

    spec = pl.BlockSpec((tr, cols), lambda i: (i, 0))
    res = pl.pallas_call(
        body, name="adamw", grid=(rows // tr,), in_specs=[spec] * 4, out_specs=[spec] * 3,
        out_shape=[jax.ShapeDtypeStruct((rows, cols), F32)] * 3, compiler_params=_params(("parallel",)),
    )(*[a.reshape(rows, cols) for a in (w, g, m, v)])
    return [r.reshape(shape) for r in res]


WEIGHTS = ("norm_g", "ffn1_w_gu", "ffn1_w_down", "w_in", "w_ret_o", "sc_conv_w", "w_sc_o", "cf_dw_w", "cf_dw_b",
           "cf_ln_g", "cf_ln_b", "w_cf_o", "w_o", "ffn2_w_gu", "ffn2_w_down")
SHARDED_SMALL = ("norm_g", "sc_conv_w", "cf_dw_w")
REPLICATED_SMALL = ("cf_dw_b", "cf_ln_g", "cf_ln_b")
SUBLANES = 8


def _pack_rows(parts):
    padded, offs, at = [], [], 0
    for p in parts:
        r = -(-p.shape[0] // SUBLANES) * SUBLANES
        padded.append(jnp.pad(p, ((0, r - p.shape[0]), (0, 0))))
        offs.append(at)
        at += r
    return jnp.concatenate(padded, axis=0), offs


def kernel(x, positions, norm_g, ffn1_w_gu, ffn1_w_down, w_in, w_ret_o, sc_conv_w, w_sc_o, cf_dw_w, cf_dw_b, cf_ln_g, cf_ln_b, w_cf_o, w_o, ffn2_w_gu, ffn2_w_down, loss_target, m_norm_g, m_ffn1_w_gu, m_ffn1_w_down, m_w_in, m_w_ret_o, m_sc_conv_w, m_w_sc_o, m_cf_dw_w, m_cf_dw_b, m_cf_ln_g, m_cf_ln_b, m_w_cf_o, m_w_o, m_ffn2_w_gu, m_ffn2_w_down, v_norm_g, v_ffn1_w_gu, v_ffn1_w_down, v_w_in, v_w_ret_o, v_sc_conv_w, v_w_sc_o, v_cf_dw_w, v_cf_dw_b, v_cf_ln_g, v_cf_ln_b, v_w_cf_o, v_w_o, v_ffn2_w_gu, v_ffn2_w_down):
    wts = dict(zip(WEIGHTS, (norm_g, ffn1_w_gu, ffn1_w_down, w_in, w_ret_o, sc_conv_w, w_sc_o, cf_dw_w, cf_dw_b,
                             cf_ln_g, cf_ln_b, w_cf_o, w_o, ffn2_w_gu, ffn2_w_down)))
    mom = dict(zip(WEIGHTS, (m_norm_g, m_ffn1_w_gu, m_ffn1_w_down, m_w_in, m_w_ret_o, m_sc_conv_w, m_w_sc_o,
                             m_cf_dw_w, m_cf_dw_b, m_cf_ln_g, m_cf_ln_b, m_w_cf_o, m_w_o, m_ffn2_w_gu, m_ffn2_w_down)))
    var = dict(zip(WEIGHTS, (v_norm_g, v_ffn1_w_gu, v_ffn1_w_down, v_w_in, v_w_ret_o, v_sc_conv_w, v_w_sc_o,
                             v_cf_dw_w, v_cf_dw_b, v_cf_ln_g, v_cf_ln_b, v_w_cf_o, v_w_o, v_ffn2_w_gu, v_ffn2_w_down)))
    depth = norm_g.shape[0]
    dq = norm_g.shape[-1]
    d = N_CHIP * dq
    chip = 2 * lax.axis_index("x") + lax.axis_index("y")
    ids = jnp.stack([chip, lax.axis_index("c")]).astype(jnp.int32)

    pk, offs = _pack_rows([wts[n].reshape(-1, dq) for n in SHARDED_SMALL])
    gk4 = allgather_small(pk)
    gk = gk4.transpose(1, 0, 2).reshape(pk.shape[0], d)
    small = {n: wts[n] for n in REPLICATED_SMALL}
    for n, o in zip(SHARDED_SMALL, offs):
        rows = wts[n].shape[0] * wts[n].shape[1]
        small[n] = gk[o:o + rows].reshape(wts[n].shape[:2] + (d,))

    order = [(l, blk) for l in range(depth) for blk in BLOCKS]
    started, token = gather_start([[place_quarter(wts[n], l, ids) for n in BLOCK_WEIGHTS[blk]] for l, blk in order], gk4)
    started = dict(zip(order, started))
    small["norm_g"] = small["norm_g"] + token[0:1, 0:1]

    def fetch(l, blk, after):
        send, recv, lands = started[l, blk]
        lands = gather_wait(lands, send, recv, after)
        return dict(zip(BLOCK_WEIGHTS[blk], sibling_fill(lands)))

    gsum = {n: None for n in BIG}
    inflight = []

    def land_sums(after):
        (l, blk), parts, lands, send, recv = inflight.pop()
        parts, lands = scatter_wait(parts, lands, send, recv, after)
        for n, pt, ld in zip(BLOCK_WEIGHTS[blk], parts, lands):
            gsum[n] = sum_partials(pt, ld, ids, l, depth, gsum[n])

    def push(l, blk, grads):
        gl = [grads[n] for n in BLOCK_WEIGHTS[blk]]
        parts = [add_halves(g, ld, ids) for g, ld in zip(gl, exchange_sibling_halves(gl))]
        send, recv, thru, lands, tok = scatter_start(parts)
        if inflight:
            land_sums(parts[0])
        inflight.append(((l, blk), thru, lands, send, recv))
        return tok[0:1, 0:1]

    loss, grad_x, gs = local_step(x, positions, loss_target, small, fetch, push)

    names = SHARDED_SMALL + REPLICATED_SMALL
    pg, offs = _pack_rows([gs[n].reshape(-1, d) for n in names])
    tot = allreduce_small(pg)
    grads = {}
    for n, o in zip(names, offs):
        rows = int(np.prod(gs[n].shape[:-1]))
        full = tot[o:o + rows]
        if n in SHARDED_SMALL:
            full = lax.dynamic_slice_in_dim(full, chip * dq, dq, axis=1)
        grads[n] = full.reshape(wts[n].shape)

    last = BLOCK_WEIGHTS[order[0][1]]
    early = [n for n in BIG if n not in last]
    every = tuple(range(depth))
    done = exchange_final_halves([gsum[n] for n in early + list(last)],
                                 [every] * len(early) + [every[1:]] * len(last))
    for n, g in zip(early + list(last), done):
        gsum[n] = g
    delta, new_m, new_v = {}, {}, {}
    for n in WEIGHTS:
        if n not in last:
            grads[n] = gsum[n] if n in BIG else grads[n]
            delta[n], new_m[n], new_v[n] = adamw(wts[n], grads[n], mom[n], var[n])
    land_sums(delta[early[-1]])
    done = exchange_final_halves([gsum[n] for n in last], [every[:1]] * len(last))
    for n, g in zip(last, done):
        grads[n] = g
        delta[n], new_m[n], new_v[n] = adamw(wts[n], g, mom[n], var[n])

    loss_all = lax.psum(loss[0, 0], ("x", "y", "c"))
    return (loss_all, grad_x, *[grads[n] for n in WEIGHTS], *[delta[n] for n in WEIGHTS],
            *[new_m[n] for n in WEIGHTS], *[new_v[n] for n in WEIGHTS])
```

```python
import functools

import jax
import jax.numpy as jnp
import numpy as np
from jax import lax
from jax.experimental import pallas as pl
from jax.experimental.pallas import tpu as pltpu

F32 = jnp.float32
BF16 = jnp.bfloat16
MXU_DTYPE = BF16
VMEM_LIMIT_BYTES = 56 * 1024 * 1024
MESH = pl.DeviceIdType.MESH

N_CHIP = 4
CHUNK = 64
RET_HEADS = 4
RET_QK_DIM = 128
RET_V_DIM = 256
SC_KERNEL = 3
CF_KERNEL = 31
ROPE_BASE = 10000.0
NORM_EPS = 1e-6
LN_EPS = 1e-5
ADAM_LR = 0.001
ADAM_B1 = 0.9
ADAM_B2 = 0.999
ADAM_EPS = 1e-08
ADAM_WD = 0.01
ADAM_STEP = 10

SUBLANES = 8
CONV_PAD = 32
CONV_TS = 128
CONV_TC = 512
CONV_SCRATCH = [pltpu.VMEM((CONV_TS + CONV_PAD, CONV_TC), F32),
                pltpu.VMEM((SUBLANES - 1, CONV_TS + CONV_PAD - SUBLANES, CONV_TC), F32)]
RET_TQ = 512
MM_TM = 1024
MM_TN = 1536
MM_K1 = 1024


def _params(sem):
    return pltpu.CompilerParams(dimension_semantics=sem, vmem_limit_bytes=VMEM_LIMIT_BYTES)


def _axes():
    return lax.axis_index("x"), lax.axis_index("y"), lax.axis_index("c")


NN = (((1,), (0,)), ((), ()))
NT = (((1,), (1,)), ((), ()))
TN = (((0,), (0,)), ((), ()))


def _mm(name, a, b, out_shape, out_dtype, grid, a_spec, b_spec, o_spec, dims, acc_shape):
    nk = grid[2]

    def body(a_ref, b_ref, o_ref, *scratch):
        bv = b_ref[...]
        if bv.ndim == 3:
            bv = bv.reshape(-1, bv.shape[-1])
        part = lax.dot_general(a_ref[...], bv, dims, preferred_element_type=F32)

        def put(v):
            o_ref[...] = v.reshape(o_ref.shape).astype(o_ref.dtype)

        if nk == 1:
            put(part)
        else:
            acc = scratch[0]
            k = pl.program_id(2)

            @pl.when(k == 0)
            def _():
                acc[...] = part

            @pl.when(k > 0)
            def _():
                acc[...] += part

            @pl.when(k == nk - 1)
            def _():
                put(acc[...])

    scratch = [pltpu.VMEM(acc_shape, F32)] if nk > 1 else []
    return pl.pallas_call(
        body, name=name, grid=grid, in_specs=[a_spec, b_spec], out_specs=o_spec,
        out_shape=jax.ShapeDtypeStruct(out_shape, out_dtype), scratch_shapes=scratch,
        compiler_params=_params(("parallel", "parallel", "arbitrary")),
    )(a, b)


def _tile(n, target):
    best = None
    for t in range(128, min(n, target) + 1, 128):
        if n % t == 0:
            best = t
    assert best is not None, (n, target)
    return best


def mm_fwd(name, a, w4, mode, out_dtype):
    t = a.shape[0]
    _, r, c = w4.shape
    tm = min(t, MM_TM)
    if mode == "col":
        tn = _tile(c, MM_TN)
        npj = c // tn
        grid = (t // tm, N_CHIP * npj, 1)
        a_spec = pl.BlockSpec((tm, r), lambda i, j, k: (i, 0))
        b_spec = pl.BlockSpec((None, r, tn), lambda i, j, k: (j // npj, 0, j % npj))
        o_spec = pl.BlockSpec((tm, tn), lambda i, j, k: (i, j))
        return _mm(name, a, w4, (t, N_CHIP * c), out_dtype, grid, a_spec, b_spec, o_spec, NN, (tm, tn))
    if N_CHIP * r <= MM_K1:
        grid = (t // tm, 1, 1)
        a_spec = pl.BlockSpec((tm, N_CHIP * r), lambda i, j, k: (i, 0))
        b_spec = pl.BlockSpec((N_CHIP, r, c), lambda i, j, k: (0, 0, 0))
        o_spec = pl.BlockSpec((tm, c), lambda i, j, k: (i, 0))
        return _mm(name, a, w4, (t, c), out_dtype, grid, a_spec, b_spec, o_spec, NN, (tm, c))
    grid = (t // tm, 1, N_CHIP)
    a_spec = pl.BlockSpec((tm, r), lambda i, j, k: (i, k))
    b_spec = pl.BlockSpec((None, r, c), lambda i, j, k: (k, 0, 0))
    o_spec = pl.BlockSpec((tm, c), lambda i, j, k: (i, 0))
    return _mm(name, a, w4, (t, c), out_dtype, grid, a_spec, b_spec, o_spec, NN, (tm, c))


def mm_dx(name, dy, w4, mode, out_dtype):
    t = dy.shape[-2]
    _, r, c = w4.shape
    tm = min(t, MM_TM)
    if mode == "col":
        tn = _tile(c, MM_TN)
        npj = c // tn
        hb = N_CHIP // 2 * npj
        grid = (t // tm, 1, N_CHIP * npj)
        if dy.ndim == 3:
            a_spec = pl.BlockSpec((None, tm, tn), lambda i, j, k: (k // hb, i, k % hb))
        else:
            a_spec = pl.BlockSpec((tm, tn), lambda i, j, k: (i, k))
        b_spec = pl.BlockSpec((None, r, tn), lambda i, j, k: (k // npj, 0, k % npj))
        o_spec = pl.BlockSpec((tm, r), lambda i, j, k: (i, 0))
        return _mm(name, dy, w4, (t, r), out_dtype, grid, a_spec, b_spec, o_spec, NT, (tm, r))
    if N_CHIP * r <= MM_K1:
        grid = (t // tm, 1, 1)
        a_spec = pl.BlockSpec((tm, c), lambda i, j, k: (i, 0))
        b_spec = pl.BlockSpec((N_CHIP, r, c), lambda i, j, k: (0, 0, 0))
        o_spec = pl.BlockSpec((tm, N_CHIP * r), lambda i, j, k: (i, 0))
        return _mm(name, dy, w4, (t, N_CHIP * r), out_dtype, grid, a_spec, b_spec, o_spec, NT, (tm, N_CHIP * r))
    grid = (t // tm, N_CHIP, 1)
    a_spec = pl.BlockSpec((tm, c), lambda i, j, k: (i, 0))
    b_spec = pl.BlockSpec((None, r, c), lambda i, j, k: (j, 0, 0))
    o_spec = pl.BlockSpec((tm, r), lambda i, j, k: (i, j))
    return _mm(name, dy, w4, (t, N_CHIP * r), out_dtype, grid, a_spec, b_spec, o_spec, NT, (tm, r))


def mm_dw(name, a, dy, mode, shape3):
    t = a.shape[0]
    _, r, c = shape3
    if mode == "col":
        tt = min(t, MM_TM)
        tn = _tile(c, MM_TN)
        npj = c // tn
        grid = (1, N_CHIP * npj, t // tt)
        a_spec = pl.BlockSpec((tt, r), lambda i, j, k: (k, 0))
        hb = N_CHIP // 2 * npj
        if dy.ndim == 3:
            b_spec = pl.BlockSpec((None, tt, tn), lambda i, j, k: (j // hb, k, j % hb))
        else:
            b_spec = pl.BlockSpec((tt, tn), lambda i, j, k: (k, j))
        o_spec = pl.BlockSpec((None, r, tn), lambda i, j, k: (j // npj, 0, j % npj))
        return _mm(name, a, dy, shape3, MXU_DTYPE, grid, a_spec, b_spec, o_spec, TN, (r, tn))
    if N_CHIP * r <= MM_K1:
        tt = min(t, 2 * MM_TM)
        grid = (1, 1, t // tt)
        a_spec = pl.BlockSpec((tt, N_CHIP * r), lambda i, j, k: (k, 0))
        b_spec = pl.BlockSpec((tt, c), lambda i, j, k: (k, 0))
        o_spec = pl.BlockSpec((N_CHIP, r, c), lambda i, j, k: (0, 0, 0))
        return _mm(name, a, dy, shape3, MXU_DTYPE, grid, a_spec, b_spec, o_spec, TN, (N_CHIP * r, c))
    tt = min(t, MM_TM)
    grid = (N_CHIP, 1, t // tt)
    a_spec = pl.BlockSpec((tt, r), lambda i, j, k: (k, i))
    b_spec = pl.BlockSpec((tt, c), lambda i, j, k: (k, 0))
    o_spec = pl.BlockSpec((None, r, c), lambda i, j, k: (i, 0, 0))
    return _mm(name, a, dy, shape3, MXU_DTYPE, grid, a_spec, b_spec, o_spec, TN, (r, c))


def _rowwise(name, fn, rows, pars, outs, accs=(), tm=256, ncol=1):
    t = rows[0][0].shape[0]
    nrow, npar, nout = len(rows), len(pars), len(outs)

    def body(*refs):
        vals = [r[...] for r in refs[:nrow + npar]]
        res = fn(*vals)
        out_refs = refs[nrow + npar:nrow + npar + nout]
        acc_refs = refs[nrow + npar + nout:]
        for o, v in zip(out_refs, res[:nout]):
            o[...] = v.astype(o.dtype)
        i = pl.program_id(1)
        for a, v in zip(acc_refs, res[nout:]):
            @pl.when(i == 0)
            def _(a=a, v=v):
                a[...] = v.astype(F32)

            @pl.when(i > 0)
            def _(a=a, v=v):
                a[...] += v.astype(F32)

    in_specs = [pl.BlockSpec((tm, w), functools.partial(lambda j, i, b: (i, b + j), b=b)) for _, w, b in rows]
    for arr, w in pars:
        if w is None:
            in_specs.append(pl.BlockSpec(arr.shape, lambda j, i: (0, 0)))
        else:
            in_specs.append(pl.BlockSpec((1, w), lambda j, i: (0, j)))
    out_specs = [pl.BlockSpec((tm, w), lambda j, i: (i, j)) for _, w, _ in outs]
    out_specs += [pl.BlockSpec((1, w), lambda j, i: (0, j)) for _, w in accs]
    out_shape = [jax.ShapeDtypeStruct((t, tw), dt) for tw, _, dt in outs]
    out_shape += [jax.ShapeDtypeStruct((1, tw), F32) for tw, _ in accs]
    res = pl.pallas_call(
        body, name=name, grid=(ncol, t // tm), in_specs=in_specs, out_specs=out_specs, out_shape=out_shape,
        compiler_params=_params(("parallel", "arbitrary" if accs else "parallel")),
    )(*[r[0] for r in rows], *[p[0] for p in pars])
    return res


def _rms(x, g):
    xf = x.astype(F32)
    return xf * lax.rsqrt(jnp.mean(xf * xf, axis=-1, keepdims=True) + NORM_EPS) * g


def _silu(x):
    return x * jax.nn.sigmoid(x)


def rms_fwd(name, x, g):
    d = x.shape[1]
    return _rowwise(name, lambda x, g: (_rms(x, g),), [(x, d, 0)], [(g, None)], [(d, d, MXU_DTYPE)], tm=512)[0]


def rms_bwd(name, x, g, dh, dres):
    d = x.shape[1]

    def fn(x, dh, dres, g):
        _, vjp = jax.vjp(_rms, x, g)
        dx, dg = vjp(dh.astype(F32))
        return dres + dx, dg

    return _rowwise(name, fn, [(x, d, 0), (dh, d, 0), (dres, d, 0)], [(g, None)], [(d, d, F32)], [(d, d)], tm=256)


def post_fwd(name, x, y, g, scale):
    d = x.shape[1]
    return _rowwise(name, lambda x, y, g: (x + scale * _rms(y, g),), [(x, d, 0), (y, d, 0)], [(g, None)],
                    [(d, d, F32)], tm=512)[0]


def post_bwd(name, y, g, dx, scale):
    d = y.shape[1]

    def fn(y, dx, g):
        _, vjp = jax.vjp(lambda y, g: scale * _rms(y, g), y, g)
        return vjp(dx)

    return _rowwise(name, fn, [(y, d, 0), (dx, d, 0)], [(g, None)], [(d, d, MXU_DTYPE)], [(d, d)], tm=256)


def ffn_up(name, h, w4):
    t = h.shape[0]
    _, r, c = w4.shape
    tm = min(t, MM_TM)
    tn = _tile(c, MM_TM)
    npj = c // tn
    half = N_CHIP // 2

    def body(h_ref, wg_ref, wu_ref, gu_ref, a_ref):
        hv = h_ref[...]
        g = lax.dot_general(hv, wg_ref[...], NN, preferred_element_type=F32)
        u = lax.dot_general(hv, wu_ref[...], NN, preferred_element_type=F32)
        gu_ref[0] = g.astype(gu_ref.dtype)
        gu_ref[1] = u.astype(gu_ref.dtype)
        a_ref[...] = (_silu(g) * u).astype(a_ref.dtype)

    f = half * c
    return pl.pallas_call(
        body, name=name, grid=(t // tm, half * npj),
        in_specs=[pl.BlockSpec((tm, r), lambda i, j: (i, 0)),
                  pl.BlockSpec((None, r, tn), lambda i, j: (j // npj, 0, j % npj)),
                  pl.BlockSpec((None, r, tn), lambda i, j: (half + j // npj, 0, j % npj))],
        out_specs=[pl.BlockSpec((2, tm, tn), lambda i, j: (0, i, j)), pl.BlockSpec((tm, tn), lambda i, j: (i, j))],
        out_shape=[jax.ShapeDtypeStruct((2, t, f), MXU_DTYPE), jax.ShapeDtypeStruct((t, f), MXU_DTYPE)],
        compiler_params=_params(("parallel", "parallel")),
    )(h, w4, w4)


def ffn_down_dx(name, dy, w4, gu):
    t = dy.shape[0]
    _, r, c = w4.shape
    tm = min(t, MM_TM)

    def body(dy_ref, w_ref, gu_ref, o_ref):
        da = lax.dot_general(dy_ref[...], w_ref[...], NT, preferred_element_type=F32)
        gate, up = gu_ref[0].astype(F32), gu_ref[1].astype(F32)
        sg = jax.nn.sigmoid(gate)
        o_ref[0] = (da * up * (sg * (1.0 + gate * (1.0 - sg)))).astype(o_ref.dtype)
        o_ref[1] = (da * (gate * sg)).astype(o_ref.dtype)

    return pl.pallas_call(
        body, name=name, grid=(t // tm, N_CHIP),
        in_specs=[pl.BlockSpec((tm, c), lambda i, j: (i, 0)), pl.BlockSpec((None, r, c), lambda i, j: (j, 0, 0)),
                  pl.BlockSpec((2, tm, r), lambda i, j: (0, i, j))],
        out_specs=pl.BlockSpec((2, tm, r), lambda i, j: (0, i, j)),
        out_shape=jax.ShapeDtypeStruct((2, t, N_CHIP * r), MXU_DTYPE),
        compiler_params=_params(("parallel", "parallel")),
    )(dy, w4, gu)


def _head_gate(o, g):
    mu = jnp.mean(o, axis=-1, keepdims=True)
    var = jnp.mean(jnp.square(o - mu), axis=-1, keepdims=True)
    return _silu(g.astype(F32)) * ((o - mu) * lax.rsqrt(var + LN_EPS))


def head_gate_fwd(name, o, p, gate_blk):
    dv = RET_V_DIM
    return _rowwise(name, lambda o, g: (_head_gate(o, g),), [(o, dv, 0), (p, dv, gate_blk)], [],
                    [(RET_HEADS * dv, dv, MXU_DTYPE)], tm=512, ncol=RET_HEADS)[0]


def head_gate_bwd(name, o, p, gate_blk, da):
    dv = RET_V_DIM

    def fn(o, g, da):
        _, vjp = jax.vjp(_head_gate, o, g.astype(F32))
        return vjp(da.astype(F32))

    w = RET_HEADS * dv
    return _rowwise(name, fn, [(o, dv, 0), (p, dv, gate_blk), (da, dv, 0)], [],
                    [(w, dv, MXU_DTYPE), (w, dv, MXU_DTYPE)], tm=512, ncol=RET_HEADS)


def _ln_silu(u, g, b):
    mu = jnp.mean(u, axis=-1, keepdims=True)
    var = jnp.mean(jnp.square(u - mu), axis=-1, keepdims=True)
    return _silu((u - mu) * lax.rsqrt(var + LN_EPS) * g + b)


def ln_silu_fwd(name, u, g, b):
    d = u.shape[1]
    return _rowwise(name, lambda u, g, b: (_ln_silu(u, g, b),), [(u, d, 0)], [(g, None), (b, None)],
                    [(d, d, MXU_DTYPE)], tm=512)[0]


def ln_silu_bwd(name, u, g, b, dc):
    d = u.shape[1]

    def fn(u, dc, g, b):
        _, vjp = jax.vjp(_ln_silu, u, g, b)
        return vjp(dc.astype(F32))

    return _rowwise(name, fn, [(u, d, 0), (dc, d, 0)], [(g, None), (b, None)], [(d, d, F32)], [(d, d), (d, d)],
                    tm=256)


def _merge(g0, g1, g2, ya, yb, yc):
    s = jax.nn.sigmoid
    return s(g0.astype(F32)) * ya + s(g1.astype(F32)) * yb + s(g2.astype(F32)) * yc


def merge_fwd(name, p, blk, ya, yb, yc):
    d = ya.shape[1]
    rows = [(p, d, blk), (p, d, blk + 1), (p, d, blk + 2), (ya, d, 0), (yb, d, 0), (yc, d, 0)]
    return _rowwise(name, lambda *v: (_merge(*v),), rows, [], [(d, d, MXU_DTYPE)], tm=256)[0]


def merge_bwd(name, p, blk, ya, yb, yc, dmg):
    d = ya.shape[1]

    def fn(g0, g1, g2, ya, yb, yc, dmg):
        _, vjp = jax.vjp(_merge, g0.astype(F32), g1.astype(F32), g2.astype(F32), ya, yb, yc)
        return vjp(dmg.astype(F32))

    rows = [(p, d, blk), (p, d, blk + 1), (p, d, blk + 2), (ya, d, 0), (yb, d, 0), (yc, d, 0), (dmg, d, 0)]
    return _rowwise(name, fn, rows, [], [(d, d, MXU_DTYPE)] * 6, tm=256)


def loss_head(name, y, target):
    t, d = y.shape
    tm = 512

    def body(y_ref, t_ref, dy_ref, loss_ref):
        err = y_ref[...] - t_ref[...]
        dy_ref[...] = err * (1.0 / d)
        part = jnp.sum(jnp.sum(err * err, axis=1, keepdims=True), axis=0, keepdims=True) * (0.5 / d)

        @pl.when(pl.program_id(0) == 0)
        def _():
            loss_ref[...] = part

        @pl.when(pl.program_id(0) > 0)
        def _():
            loss_ref[...] += part

    return pl.pallas_call(
        body, name=name, grid=(t // tm,),
        in_specs=[pl.BlockSpec((tm, d), lambda i: (i, 0))] * 2,
        out_specs=[pl.BlockSpec((tm, d), lambda i: (i, 0)), pl.BlockSpec((1, 1), lambda i: (0, 0))],
        out_shape=[jax.ShapeDtypeStruct((t, d), F32), jax.ShapeDtypeStruct((1, 1), F32)],
        compiler_params=_params(("arbitrary",)),
    )(y, target)


def _rot(x, cos2, sin2):
    return x * cos2 + pltpu.roll(x, RET_QK_DIM // 2, 1) * sin2


def _decay_mask(lg, n0, rows, cols):
    n = n0 + lax.broadcasted_iota(jnp.int32, (rows, cols), 0)
    m = lax.broadcasted_iota(jnp.int32, (rows, cols), 1)
    shift = CHUNK.bit_length() - 1
    dist = jnp.abs(n - m).astype(F32)
    return jnp.where((m >> shift) <= (n >> shift), jnp.exp(lg * dist), 0.0)


def _ret_specs(s):
    dk, dv, h = RET_QK_DIM, RET_V_DIM, RET_HEADS
    return [
        pl.BlockSpec((s, dk), lambda b, hh: (b, hh)),
        pl.BlockSpec((s, dk), lambda b, hh: (b, h + hh)),
        pl.BlockSpec((s, dv), lambda b, hh: (b, (2 * h * dk) // dv + hh)),
        pl.BlockSpec((s, dk), lambda b, hh: (b, 0)),
        pl.BlockSpec((s, dk), lambda b, hh: (b, 0)),
        pl.BlockSpec((None, 1, dk), lambda b, hh: (hh, 0, 0)),
    ]


def retention_fwd(name, p, cos2, sin2, log_g, nb, s):
    dk, dv, h = RET_QK_DIM, RET_V_DIM, RET_HEADS

    def body(q_ref, k_ref, v_ref, cos_ref, sin_ref, lg_ref, o_ref, kr_ref):
        lg = lg_ref[0:1, 0:1]
        kr = _rot(k_ref[...].astype(F32), cos_ref[...], sin_ref[...]) * (dk ** -0.5)
        kr_ref[...] = kr.astype(kr_ref.dtype)
        for qi in range(s // RET_TQ):
            n0, kmax = qi * RET_TQ, (qi + 1) * RET_TQ
            rows = pl.ds(n0, RET_TQ)
            qr = _rot(q_ref[rows, :].astype(F32), cos_ref[rows, :], sin_ref[rows, :]).astype(MXU_DTYPE)
            sc = lax.dot_general(qr, kr_ref[0:kmax, :], NT, preferred_element_type=F32)
            pm = (sc * _decay_mask(lg, n0, RET_TQ, kmax)).astype(MXU_DTYPE)
            o_ref[rows, :] = lax.dot_general(pm, v_ref[0:kmax, :], NN, preferred_element_type=F32)

    return pl.pallas_call(
        body, name=name, grid=(nb, h), in_specs=_ret_specs(s),
        out_specs=pl.BlockSpec((s, dv), lambda b, hh: (b, hh)),
        out_shape=jax.ShapeDtypeStruct((nb * s, h * dv), F32),
        scratch_shapes=[pltpu.VMEM((s, dk), MXU_DTYPE)],
        compiler_params=_params(("parallel", "parallel")),
    )(p, p, p, cos2, sin2, log_g)


def retention_bwd(name, p, cos2, sin2, log_g, do, nb, s):
    dk, dv, h = RET_QK_DIM, RET_V_DIM, RET_HEADS

    def body(q_ref, k_ref, v_ref, cos_ref, sin_ref, lg_ref, do_ref, dq_ref, dk_ref, dv_ref, kr_ref, dk_acc, dv_acc):
        lg = lg_ref[0:1, 0:1]
        kr = _rot(k_ref[...].astype(F32), cos_ref[...], sin_ref[...]) * (dk ** -0.5)
        kr_ref[...] = kr.astype(kr_ref.dtype)
        dk_acc[...] = jnp.zeros_like(dk_acc)
        dv_acc[...] = jnp.zeros_like(dv_acc)
        for qi in range(s // RET_TQ):
            n0, kmax = qi * RET_TQ, (qi + 1) * RET_TQ
            rows = pl.ds(n0, RET_TQ)
            cq, sq = cos_ref[rows, :], sin_ref[rows, :]
            qr = _rot(q_ref[rows, :].astype(F32), cq, sq).astype(MXU_DTYPE)
            dob = do_ref[rows, :]
            mask = _decay_mask(lg, n0, RET_TQ, kmax)
            sc = lax.dot_general(qr, kr_ref[0:kmax, :], NT, preferred_element_type=F32)
            pm = (sc * mask).astype(MXU_DTYPE)
            dv_acc[0:kmax, :] += lax.dot_general(pm, dob, TN, preferred_element_type=F32)
            dp = lax.dot_general(dob, v_ref[0:kmax, :], NT, preferred_element_type=F32)
            ds = (dp * mask).astype(MXU_DTYPE)
            dqr = lax.dot_general(ds, kr_ref[0:kmax, :], NN, preferred_element_type=F32)
            dq_ref[rows, :] = _rot(dqr, cq, -sq).astype(dq_ref.dtype)
            dk_acc[0:kmax, :] += lax.dot_general(ds, qr, TN, preferred_element_type=F32)
        dkr = dk_acc[...] * (dk ** -0.5)
        dk_ref[...] = _rot(dkr, cos_ref[...], -sin_ref[...]).astype(dk_ref.dtype)
        dv_ref[...] = dv_acc[...].astype(dv_ref.dtype)

    t = nb * s
    return pl.pallas_call(
        body, name=name, grid=(nb, h),
        in_specs=_ret_specs(s) + [pl.BlockSpec((s, dv), lambda b, hh: (b, hh))],
        out_specs=[pl.BlockSpec((s, dk), lambda b, hh: (b, hh)), pl.BlockSpec((s, dk), lambda b, hh: (b, hh)),
                   pl.BlockSpec((s, dv), lambda b, hh: (b, hh))],
        out_shape=[jax.ShapeDtypeStruct((t, h * dk), MXU_DTYPE), jax.ShapeDtypeStruct((t, h * dk), MXU_DTYPE),
                   jax.ShapeDtypeStruct((t, h * dv), MXU_DTYPE)],
        scratch_shapes=[pltpu.VMEM((s, dk), MXU_DTYPE), pltpu.VMEM((s, dk), F32), pltpu.VMEM((s, dv), F32)],
        compiler_params=_params(("parallel", "parallel")),
    )(p, p, p, cos2, sin2, log_g, do)


def _conv_grid(t, d, nb):
    s = t // nb
    ns, nc = s // CONV_TS, d // CONV_TC
    return s, ns, nc


def _shifted(pad_ref, sh_ref, offsets):
    n = sh_ref.shape[1]
    for b in sorted({off % SUBLANES for off in offsets} - {0}):
        sh_ref[b - 1] = pad_ref[pl.ds(b, n), :]

    def read(off):
        a, b = off - off % SUBLANES, off % SUBLANES
        return pad_ref[pl.ds(a, CONV_TS), :] if b == 0 else sh_ref[b - 1, pl.ds(a, CONV_TS), :]

    return read


def _causal_taps(pad_ref, sh_ref, w_ref, k):
    offs = [CONV_PAD - (k - 1) + j for j in range(k)]
    read = _shifted(pad_ref, sh_ref, offs)
    acc = None
    for j in range(k):
        term = w_ref[j:j + 1, :] * read(offs[j])
        acc = term if acc is None else acc + term
    return acc


def _carry_past(pad_ref, s_idx):
    @pl.when(s_idx == 0)
    def _():
        pad_ref[0:CONV_PAD, :] = jnp.zeros((CONV_PAD, pad_ref.shape[1]), F32)

    @pl.when(s_idx > 0)
    def _():
        pad_ref[0:CONV_PAD, :] = pad_ref[CONV_TS:CONV_TS + CONV_PAD, :]


def _carry_future(pad_ref, s_idx):
    @pl.when(s_idx == 0)
    def _():
        pad_ref[CONV_TS:CONV_TS + CONV_PAD, :] = jnp.zeros((CONV_PAD, pad_ref.shape[1]), F32)

    @pl.when(s_idx > 0)
    def _():
        pad_ref[CONV_TS:CONV_TS + CONV_PAD, :] = pad_ref[0:CONV_PAD, :]


def _conv_bwd_taps(pad_ref, sh_ref, w_ref, x, dw_ref, k):
    read = _shifted(pad_ref, sh_ref, range(k))
    acc = None
    for j in range(k):
        sh = read(k - 1 - j)
        term = w_ref[j:j + 1, :] * sh
        acc = term if acc is None else acc + term
        dw_ref[j:j + 1, :] += jnp.sum(x * sh, axis=0, keepdims=True)
    return acc


def short_conv_fwd(name, p, blk_b, w, nb):
    t = p.shape[0]
    d = w.shape[1]
    s, ns, nc = _conv_grid(t, d, nb)
    cb = d // CONV_TC

    def body(b_ref, c_ref, x_ref, w_ref, y_ref, cz_ref, pad_ref, sh_ref):
        _carry_past(pad_ref, pl.program_id(2))
        pad_ref[CONV_PAD:CONV_PAD + CONV_TS, :] = c_ref[...].astype(F32) * x_ref[...].astype(F32)
        cz = _causal_taps(pad_ref, sh_ref, w_ref, SC_KERNEL)
        cz_ref[...] = cz
        y_ref[...] = (b_ref[...].astype(F32) * cz).astype(y_ref.dtype)

    def pspec(off):
        return pl.BlockSpec((CONV_TS, CONV_TC), lambda c, b, si: (b * ns + si, (blk_b + off) * cb + c))

    ospec = pl.BlockSpec((CONV_TS, CONV_TC), lambda c, b, si: (b * ns + si, c))
    return pl.pallas_call(
        body, name=name, grid=(nc, nb, ns),
        in_specs=[pspec(0), pspec(1), pspec(2), pl.BlockSpec((SC_KERNEL, CONV_TC), lambda c, b, si: (0, c))],
        out_specs=[ospec, ospec],
        out_shape=[jax.ShapeDtypeStruct((t, d), MXU_DTYPE), jax.ShapeDtypeStruct((t, d), F32)],
        scratch_shapes=CONV_SCRATCH,
        compiler_params=_params(("parallel", "arbitrary", "arbitrary")),
    )(p, p, p, w)


def short_conv_bwd(name, p, blk_b, w, cz, dy, nb):
    t = p.shape[0]
    d = w.shape[1]
    s, ns, nc = _conv_grid(t, d, nb)
    cb = d // CONV_TC

    def body(b_ref, c_ref, x_ref, w_ref, cz_ref, dy_ref, db_ref, dc_ref, dx_ref, dw_ref, pad_ref, sh_ref):
        si = pl.program_id(2)
        _carry_future(pad_ref, si)
        dyv = dy_ref[...].astype(F32)
        cv, xv = c_ref[...].astype(F32), x_ref[...].astype(F32)
        db_ref[...] = (dyv * cz_ref[...]).astype(db_ref.dtype)
        pad_ref[0:CONV_TS, :] = dyv * b_ref[...].astype(F32)

        @pl.when(jnp.logical_and(pl.program_id(1) == 0, si == 0))
        def _():
            dw_ref[...] = jnp.zeros_like(dw_ref)

        dz = _conv_bwd_taps(pad_ref, sh_ref, w_ref, cv * xv, dw_ref, SC_KERNEL)
        dc_ref[...] = (dz * xv).astype(dc_ref.dtype)
        dx_ref[...] = (dz * cv).astype(dx_ref.dtype)

    def row(b, si):
        return b * ns + (ns - 1 - si)

    def pspec(off):
        return pl.BlockSpec((CONV_TS, CONV_TC), lambda c, b, si: (row(b, si), (blk_b + off) * cb + c))

    ospec = pl.BlockSpec((CONV_TS, CONV_TC), lambda c, b, si: (row(b, si), c))
    wspec = pl.BlockSpec((SC_KERNEL, CONV_TC), lambda c, b, si: (0, c))
    return pl.pallas_call(
        body, name=name, grid=(nc, nb, ns),
        in_specs=[pspec(0), pspec(1), pspec(2), wspec, ospec, ospec],
        out_specs=[ospec, ospec, ospec, wspec],
        out_shape=[jax.ShapeDtypeStruct((t, d), MXU_DTYPE)] * 3 + [jax.ShapeDtypeStruct((SC_KERNEL, d), F32)],
        scratch_shapes=CONV_SCRATCH,
        compiler_params=_params(("parallel", "arbitrary", "arbitrary")),
    )(p, p, p, w, cz, dy)


def conformer_conv_fwd(name, p, blk_a, w, bias, nb):
    t = p.shape[0]
    d = w.shape[1]
    s, ns, nc = _conv_grid(t, d, nb)
    cb = d // CONV_TC

    def body(a_ref, b_ref, w_ref, bias_ref, u_ref, pad_ref, sh_ref):
        _carry_past(pad_ref, pl.program_id(2))
        pad_ref[CONV_PAD:CONV_PAD + CONV_TS, :] = a_ref[...].astype(F32) * jax.nn.sigmoid(b_ref[...].astype(F32))
        u_ref[...] = _causal_taps(pad_ref, sh_ref, w_ref, CF_KERNEL) + bias_ref[...]

    def pspec(off):
        return pl.BlockSpec((CONV_TS, CONV_TC), lambda c, b, si: (b * ns + si, (blk_a + off) * cb + c))

    return pl.pallas_call(
        body, name=name, grid=(nc, nb, ns),
        in_specs=[pspec(0), pspec(1), pl.BlockSpec((CF_KERNEL, CONV_TC), lambda c, b, si: (0, c)),
                  pl.BlockSpec((1, CONV_TC), lambda c, b, si: (0, c))],
        out_specs=pl.BlockSpec((CONV_TS, CONV_TC), lambda c, b, si: (b * ns + si, c)),
        out_shape=jax.ShapeDtypeStruct((t, d), F32),
        scratch_shapes=CONV_SCRATCH,
        compiler_params=_params(("parallel", "arbitrary", "arbitrary")),
    )(p, p, w, bias)


def conformer_conv_bwd(name, p, blk_a, w, du, nb):
    t = p.shape[0]
    d = w.shape[1]
    s, ns, nc = _conv_grid(t, d, nb)
    cb = d // CONV_TC

    def body(a_ref, b_ref, w_ref, du_ref, da_ref, db_ref, dw_ref, dbias_ref, pad_ref, sh_ref):
        si = pl.program_id(2)
        _carry_future(pad_ref, si)
        duv = du_ref[...]
        av = a_ref[...].astype(F32)
        sg = jax.nn.sigmoid(b_ref[...].astype(F32))
        pad_ref[0:CONV_TS, :] = duv

        @pl.when(jnp.logical_and(pl.program_id(1) == 0, si == 0))
        def _():
            dw_ref[...] = jnp.zeros_like(dw_ref)
            dbias_ref[...] = jnp.zeros_like(dbias_ref)

        du0 = _conv_bwd_taps(pad_ref, sh_ref, w_ref, av * sg, dw_ref, CF_KERNEL)
        da_ref[...] = (du0 * sg).astype(da_ref.dtype)
        db_ref[...] = (du0 * av * sg * (1.0 - sg)).astype(db_ref.dtype)
        dbias_ref[...] += jnp.sum(duv, axis=0, keepdims=True)

    def row(b, si):
        return b * ns + (ns - 1 - si)

    def pspec(off):
        return pl.BlockSpec((CONV_TS, CONV_TC), lambda c, b, si: (row(b, si), (blk_a + off) * cb + c))

    ospec = pl.BlockSpec((CONV_TS, CONV_TC), lambda c, b, si: (row(b, si), c))
    wspec = pl.BlockSpec((CF_KERNEL, CONV_TC), lambda c, b, si: (0, c))
    bspec = pl.BlockSpec((1, CONV_TC), lambda c, b, si: (0, c))
    return pl.pallas_call(
        body, name=name, grid=(nc, nb, ns),
        in_specs=[pspec(0), pspec(1), wspec, ospec],
        out_specs=[ospec, ospec, wspec, bspec],
        out_shape=[jax.ShapeDtypeStruct((t, d), MXU_DTYPE)] * 2
        + [jax.ShapeDtypeStruct((CF_KERNEL, d), F32), jax.ShapeDtypeStruct((1, d), F32)],
        scratch_shapes=CONV_SCRATCH,
        compiler_params=_params(("parallel", "arbitrary", "arbitrary")),
    )(p, p, w, du)


BLOCKS = ("ffn1", "mixer", "ffn2")
BLOCK_WEIGHTS = {"ffn1": ("ffn1_w_gu", "ffn1_w_down"), "mixer": ("w_in", "w_ret_o", "w_sc_o", "w_cf_o", "w_o"),
                 "ffn2": ("ffn2_w_gu", "ffn2_w_down")}
BIG = BLOCK_WEIGHTS["ffn1"] + BLOCK_WEIGHTS["mixer"] + BLOCK_WEIGHTS["ffn2"]
MODE = {"ffn1_w_gu": "col", "ffn1_w_down": "row", "w_in": "col", "w_ret_o": "row", "w_sc_o": "row",
        "w_cf_o": "row", "w_o": "row", "ffn2_w_gu": "col", "ffn2_w_down": "row"}
NORM_OF = {"ffn1": 0, "mixer": 2, "ffn2": 4}
BLK_GATE, BLK_SCB, BLK_CFA, BLK_MERGE = 2, 3, 6, 8


def _rope_tables(positions):
    half = RET_QK_DIM // 2
    inv_freq = ROPE_BASE ** (-jnp.arange(half, dtype=F32) / half)
    ang = positions.astype(F32)[..., None] * inv_freq
    cos, sin = jnp.cos(ang), jnp.sin(ang)
    nb, s = positions.shape
    cos2 = jnp.concatenate([cos, cos], axis=-1).reshape(nb * s, RET_QK_DIM)
    sin2 = jnp.concatenate([-sin, sin], axis=-1).reshape(nb * s, RET_QK_DIM)
    return cos2, sin2


def _log_gamma():
    lg = jnp.log(1.0 - 2.0 ** (-5.0 - jnp.arange(RET_HEADS, dtype=F32)))
    return jnp.broadcast_to(lg[:, None, None], (RET_HEADS, 1, RET_QK_DIM))


def _ffn_fwd(xs, w, tag, g_pre, g_post):
    h = rms_fwd("ffn_rms", xs, g_pre)
    gu, a = ffn_up("ffn_up", h, w[tag + "_w_gu"])
    y = mm_fwd("ffn_down", a, w[tag + "_w_down"], "row", F32)
    out = post_fwd("ffn_post", xs, y, g_post, 0.5)
    return out, dict(x=xs, h=h, gu=gu, a=a, y=y, w=w)


def _ffn_bwd(dxs, sv, tag, g_pre, g_post):
    w = sv["w"]
    gu_w, down_w = w[tag + "_w_gu"], w[tag + "_w_down"]
    dy, dg_post = post_bwd("ffn_post_bwd", sv["y"], g_post, dxs, 0.5)
    dgu = ffn_down_dx("ffn_down_dx", dy, down_w, sv["gu"])
    grads = {tag + "_w_down": mm_dw("ffn_down_dw", sv["a"], dy, "row", down_w.shape)}
    dh = mm_dx("ffn_gu_dx", dgu, gu_w, "col", F32)
    grads[tag + "_w_gu"] = mm_dw("ffn_gu_dw", sv["h"], dgu, "col", gu_w.shape)
    dxs, dg_pre = rms_bwd("ffn_rms_bwd", sv["x"], g_pre, dh, dxs)
    return dxs, grads, dg_pre, dg_post


def _mixer_fwd(xs, w, sm, g_pre, g_post, rope, nb, s):
    cos2, sin2, log_g = rope
    d = xs.shape[1]
    gate_blk = (BLK_GATE * d) // RET_V_DIM
    h = rms_fwd("mx_rms", xs, g_pre)
    p = mm_fwd("mx_in", h, w["w_in"], "col", MXU_DTYPE)
    o = retention_fwd("ret_fwd", p, cos2, sin2, log_g, nb, s)
    ya_in = head_gate_fwd("ret_gate", o, p, gate_blk)
    yb_in, cz = short_conv_fwd("sc_fwd", p, BLK_SCB, sm["sc_conv_w"], nb)
    u1 = conformer_conv_fwd("cf_fwd", p, BLK_CFA, sm["cf_dw_w"], sm["cf_dw_b"], nb)
    yc_in = ln_silu_fwd("cf_ln", u1, sm["cf_ln_g"], sm["cf_ln_b"])
    ya = mm_fwd("mx_proj", ya_in, w["w_ret_o"], "row", F32)
    yb = mm_fwd("mx_proj", yb_in, w["w_sc_o"], "row", F32)
    yc = mm_fwd("mx_proj", yc_in, w["w_cf_o"], "row", F32)
    mg = merge_fwd("mx_merge", p, BLK_MERGE, ya, yb, yc)
    m = mm_fwd("mx_proj", mg, w["w_o"], "row", F32)
    out = post_fwd("mx_post", xs, m, g_post, 1.0)
    return out, dict(x=xs, h=h, p=p, o=o, ya_in=ya_in, yb_in=yb_in, cz=cz, u1=u1, yc_in=yc_in, ya=ya, yb=yb, yc=yc,
                     mg=mg, m=m, w=w)


def _mixer_bwd(dxs, sv, sm, g_pre, g_post, rope, nb, s):
    cos2, sin2, log_g = rope
    w, p = sv["w"], sv["p"]
    d = dxs.shape[1]
    gate_blk = (BLK_GATE * d) // RET_V_DIM
    grads, gsm = {}, {}

    def proj_bwd(wname, a_in, dy, out_dtype):
        grads[wname] = mm_dw("mx_proj_dw", a_in, dy, "row", w[wname].shape)
        return mm_dx("mx_proj_dx", dy, w[wname], "row", out_dtype)

    dm, dg_post = post_bwd("mx_post_bwd", sv["m"], g_post, dxs, 1.0)
    dmg = proj_bwd("w_o", sv["mg"], dm, MXU_DTYPE)
    dg0, dg1, dg2, dya, dyb, dyc = merge_bwd("mx_merge_bwd", p, BLK_MERGE, sv["ya"], sv["yb"], sv["yc"], dmg)
    dya_in = proj_bwd("w_ret_o", sv["ya_in"], dya, MXU_DTYPE)
    dyb_in = proj_bwd("w_sc_o", sv["yb_in"], dyb, MXU_DTYPE)
    dyc_in = proj_bwd("w_cf_o", sv["yc_in"], dyc, MXU_DTYPE)
    do, dgret = head_gate_bwd("ret_gate_bwd", sv["o"], p, gate_blk, dya_in)
    dq, dk, dv = retention_bwd("ret_bwd", p, cos2, sin2, log_g, do, nb, s)
    dscb, dscc, dscx, gsm["sc_conv_w"] = short_conv_bwd("sc_bwd", p, BLK_SCB, sm["sc_conv_w"], sv["cz"], dyb_in, nb)
    du1, dlg, dlb = ln_silu_bwd("cf_ln_bwd", sv["u1"], sm["cf_ln_g"], sm["cf_ln_b"], dyc_in)
    dcfa, dcfb, gsm["cf_dw_w"], dbias = conformer_conv_bwd("cf_bwd", p, BLK_CFA, sm["cf_dw_w"], du1, nb)
    gsm.update(cf_ln_g=dlg[0], cf_ln_b=dlb[0], cf_dw_b=dbias[0])
    dp = jnp.concatenate([dq, dk, dv, dgret, dscb, dscc, dscx, dcfa, dcfb, dg0, dg1, dg2], axis=1)
    dh = mm_dx("mx_in_dx", dp, w["w_in"], "col", F32)
    grads["w_in"] = mm_dw("mx_in_dw", sv["h"], dp, "col", w["w_in"].shape)
    dxs, dg_pre = rms_bwd("mx_rms_bwd", sv["x"], g_pre, dh, dxs)
    return dxs, grads, gsm, dg_pre, dg_post


def local_step(x, positions, target, small, fetch, push):
    nb, s, d = x.shape
    t = nb * s
    depth = small["norm_g"].shape[0]
    rope = _rope_tables(positions) + (_log_gamma(),)
    xs = x.reshape(t, d)
    token = [None]

    def gain(l, i):
        g = small["norm_g"][l, i][None, :]
        if token[0] is not None:
            g, token[0] = g + token[0], None
        return g

    def mixer_small(l):
        return dict(sc_conv_w=small["sc_conv_w"][l], cf_dw_w=small["cf_dw_w"][l], cf_dw_b=small["cf_dw_b"][l][None, :],
                    cf_ln_g=small["cf_ln_g"][l][None, :], cf_ln_b=small["cf_ln_b"][l][None, :])

    saved = {}
    for l in range(depth):
        for blk in BLOCKS:
            w = fetch(l, blk, xs)
            i0 = NORM_OF[blk]
            if blk == "mixer":
                xs, saved[l, blk] = _mixer_fwd(xs, w, mixer_small(l), gain(l, i0), gain(l, i0 + 1), rope, nb, s)
            else:
                xs, saved[l, blk] = _ffn_fwd(xs, w, blk, gain(l, i0), gain(l, i0 + 1))

    dxs, loss = loss_head("loss", xs, target.reshape(t, d))

    dnorm = [[None] * 6 for _ in range(depth)]
    gsmall = {n: [None] * depth for n in ("sc_conv_w", "cf_dw_w", "cf_dw_b", "cf_ln_g", "cf_ln_b")}
    for l in reversed(range(depth)):
        for blk in reversed(BLOCKS):
            i0 = NORM_OF[blk]
            g_post, g_pre = gain(l, i0 + 1), gain(l, i0)
            if blk == "mixer":
                dxs, grads, gsm, dnorm[l][i0], dnorm[l][i0 + 1] = _mixer_bwd(
                    dxs, saved[l, blk], mixer_small(l), g_pre, g_post, rope, nb, s)
                for n, v in gsm.items():
                    gsmall[n][l] = v
            else:
                dxs, grads, dnorm[l][i0], dnorm[l][i0 + 1] = _ffn_bwd(dxs, saved[l, blk], blk, g_pre, g_post)
            token[0] = push(l, blk, grads)

    gs = {n: jnp.stack(v) for n, v in gsmall.items()}
    gs["norm_g"] = jnp.stack([jnp.concatenate(r, axis=0) for r in dnorm])
    return loss, dxs.reshape(nb, s, d), gs


ANY = pl.BlockSpec(memory_space=pl.ANY)
HBM = pl.BlockSpec(memory_space=pltpu.HBM)
SEM = pl.BlockSpec(memory_space=pltpu.SEMAPHORE)
VMEM_WHOLE = pl.BlockSpec(memory_space=pltpu.VMEM)
EFFECT = pltpu.SideEffectType.DATAFLOW_SIDE_EFFECTING
TOKEN = jax.ShapeDtypeStruct((8, 128), F32)


def _other_chips(x, y):
    return [(1 - x, y), (x, 1 - y), (1 - x, 1 - y)]


def _remote(src, dst, send_sem, recv_sem, to):
    return pltpu.make_async_remote_copy(src_ref=src, dst_ref=dst, send_sem=send_sem, recv_sem=recv_sem,
                                        device_id=to, device_id_type=MESH)


def _in_hbm(v):
    return pltpu.with_memory_space_constraint(v, pltpu.HBM)


def place_quarter(w, layer, ids, after):
    _, r, c = w.shape
    tr = min(r, 256)

    def body(ids_ref, w_ref, after_ref, o_ref):
        o_ref[...] = w_ref[...].astype(o_ref.dtype)

    return pl.pallas_call(
        body, name="place_quarter",
        grid_spec=pltpu.PrefetchScalarGridSpec(
            num_scalar_prefetch=1, grid=(r // tr,),
            in_specs=[pl.BlockSpec((None, tr, c), lambda i, ids_ref: (layer, i, 0)), ANY],
            out_specs=pl.BlockSpec((None, tr, c), lambda i, ids_ref: (ids_ref[0], i, 0))),
        out_shape=jax.ShapeDtypeStruct((N_CHIP, r, c), MXU_DTYPE),
        compiler_params=_params(("parallel",)),
    )(ids, w, after)


def _gather_copies(lands, send, recv):
    x, y, c = _axes()
    me = 2 * x + y
    mine, theirs = [], []
    for a, ld in enumerate(lands):
        rh = ld.shape[1] // 2
        rows = pl.ds(c * rh, rh)
        for k, (px, py) in enumerate(_other_chips(x, y)):
            to = (px, py, c)
            mine.append(_remote(ld.at[me, rows, :], ld.at[me, rows, :], send.at[3 * a + k], recv.at[3 * a + k], to))
            got = ld.at[2 * px + py, rows, :]
            theirs.append(_remote(got, got, send.at[3 * a + k], recv.at[3 * a + k], to))
    return mine, theirs


def gather_start(name, groups, after):
    flat = [s for g in groups for s in g]
    n, ng = len(flat), len(groups)
    sizes = [len(g) for g in groups]

    def body(*refs):
        lands = refs[:n]
        sems = refs[n + 1:n + 1 + 2 * ng]
        token = refs[-1]
        at = 0
        for g, m in enumerate(sizes):
            mine, _ = _gather_copies(lands[at:at + m], sems[2 * g], sems[2 * g + 1])
            for cp in mine:
                cp.start()
            at += m
        token[...] = jnp.zeros_like(token)

    sem_shapes = []
    for m in sizes:
        sem_shapes += [pltpu.SemaphoreType.DMA((3 * m,))] * 2
    res = pl.pallas_call(
        body, name=name, in_specs=[HBM] * n + [ANY],
        out_specs=[SEM] * (2 * ng) + [HBM] * n + [VMEM_WHOLE],
        out_shape=sem_shapes + [pltpu.HBM(s.shape, s.dtype) for s in flat] + [TOKEN],
        input_output_aliases={i: 2 * ng + i for i in range(n)},
        compiler_params=pltpu.CompilerParams(has_side_effects=EFFECT),
    )(*[_in_hbm(s) for s in flat], after)
    sems, thru, token = res[:2 * ng], res[2 * ng:2 * ng + n], res[-1]
    out, at = [], 0
    for g, m in enumerate(sizes):
        out.append((sems[2 * g], sems[2 * g + 1], thru[at:at + m]))
        at += m
    return out, token


def gather_wait(lands, send, recv, after):
    m = len(lands)

    def body(*refs):
        mine, theirs = _gather_copies(refs[:m], refs[m], refs[m + 1])
        for cp in mine:
            cp.wait_send()
        for cp in theirs:
            cp.wait_recv()

    return pl.pallas_call(
        body, name="gather_wait", in_specs=[HBM] * m + [SEM, SEM, ANY], out_specs=[HBM] * m,
        out_shape=[pltpu.HBM(l.shape, l.dtype) for l in lands],
        input_output_aliases={i: i for i in range(m)},
        compiler_params=pltpu.CompilerParams(has_side_effects=EFFECT),
    )(*lands, send, recv, after)


def sibling_fill(lands):
    m = len(lands)

    def body(*refs):
        lds = refs[:m]
        send, recv = refs[2 * m:]
        x, y, c = _axes()
        sib = (x, y, 1 - c)
        cps = []
        for a in range(m):
            rh = lds[a].shape[1] // 2
            for k, (px, py) in enumerate(_other_chips(x, y)):
                got = lds[a].at[2 * px + py, pl.ds(c * rh, rh), :]
                cp = _remote(got, got, send.at[3 * a + k], recv.at[3 * a + k], sib)
                cp.start()
                cps.append(cp)
        for a in range(m):
            rh = lds[a].shape[1] // 2
            for k, (px, py) in enumerate(_other_chips(x, y)):
                blk = lds[a].at[2 * px + py, pl.ds((1 - c) * rh, rh), :]
                _remote(blk, blk, send.at[3 * a + k], recv.at[3 * a + k], sib).wait_recv()
        for cp in cps:
            cp.wait_send()

    return pl.pallas_call(
        body, name="sibling_fill", in_specs=[ANY] * m, out_specs=[ANY] * m,
        out_shape=[jax.ShapeDtypeStruct(l.shape, l.dtype) for l in lands],
        input_output_aliases={i: i for i in range(m)},
        scratch_shapes=[pltpu.SemaphoreType.DMA((3 * m,))] * 2,
    )(*lands)


def exchange_sibling_halves(grads):
    n = len(grads)

    def body(*refs):
        ins, outs = refs[:n], refs[n:2 * n]
        send, recv = refs[2 * n:]
        x, y, c = _axes()
        cps = []
        for a in range(n):
            rh = ins[a].shape[1] // 2
            cp = _remote(ins[a].at[:, pl.ds((1 - c) * rh, rh), :], outs[a], send.at[a], recv.at[a], (x, y, 1 - c))
            cp.start()
            cps.append(cp)
        for cp in cps:
            cp.wait()

    return pl.pallas_call(
        body, name="exchange_sibling_halves", in_specs=[ANY] * n, out_specs=[ANY] * n,
        out_shape=[jax.ShapeDtypeStruct((g.shape[0], g.shape[1] // 2, g.shape[2]), g.dtype) for g in grads],
        scratch_shapes=[pltpu.SemaphoreType.DMA((n,))] * 2,
    )(*grads)


def add_halves(g4, land, ids):
    nq, r, c = g4.shape
    rh = r // 2

    def body(ids_ref, a_ref, b_ref, o_ref):
        o_ref[...] = (a_ref[...].astype(F32) + b_ref[...].astype(F32)).astype(o_ref.dtype)

    return pl.pallas_call(
        body, name="add_halves",
        grid_spec=pltpu.PrefetchScalarGridSpec(
            num_scalar_prefetch=1, grid=(nq,),
            in_specs=[pl.BlockSpec((None, rh, c), lambda i, ids_ref: (i, ids_ref[1], 0)),
                      pl.BlockSpec((None, rh, c), lambda i, ids_ref: (i, 0, 0))],
            out_specs=pl.BlockSpec((None, rh, c), lambda i, ids_ref: (i, 0, 0))),
        out_shape=jax.ShapeDtypeStruct((nq, rh, c), g4.dtype),
        compiler_params=_params(("parallel",)),
    )(ids, g4, land)


def _scatter_copies(parts, lands, send, recv):
    x, y, c = _axes()
    cps = []
    for a, (pt, ld) in enumerate(zip(parts, lands)):
        for k, (px, py) in enumerate(_other_chips(x, y)):
            cps.append(_remote(pt.at[2 * px + py], ld.at[k], send.at[3 * a + k], recv.at[3 * a + k], (px, py, c)))
    return cps


def scatter_start(parts):
    m = len(parts)

    def body(*refs):
        for cp in _scatter_copies(refs[:m], refs[m:2 * m], refs[2 * m], refs[2 * m + 1]):
            cp.start()
        refs[-1][...] = jnp.zeros_like(refs[-1])

    lands = [lax.empty((N_CHIP - 1,) + p.shape[1:], p.dtype) for p in parts]
    res = pl.pallas_call(
        body, name="scatter_start", in_specs=[HBM] * (2 * m), out_specs=[SEM, SEM] + [HBM] * (2 * m) + [VMEM_WHOLE],
        out_shape=[pltpu.SemaphoreType.DMA((3 * m,))] * 2 + [pltpu.HBM(p.shape, p.dtype) for p in parts]
        + [pltpu.HBM(l.shape, l.dtype) for l in lands] + [TOKEN],
        input_output_aliases={i: 2 + i for i in range(2 * m)},
        compiler_params=pltpu.CompilerParams(has_side_effects=EFFECT),
    )(*[_in_hbm(p) for p in parts], *[_in_hbm(l) for l in lands])
    return res[0], res[1], res[2:2 + m], res[2 + m:2 + 2 * m], res[-1]


def scatter_wait(parts, lands, send, recv, after):
    m = len(parts)

    def body(*refs):
        for cp in _scatter_copies(refs[:m], refs[m:2 * m], refs[2 * m], refs[2 * m + 1]):
            cp.wait_send()
            cp.wait_recv()

    res = pl.pallas_call(
        body, name="scatter_wait", in_specs=[HBM] * (2 * m) + [SEM, SEM] + [ANY] * len(after),
        out_specs=[HBM] * (2 * m),
        out_shape=[pltpu.HBM(p.shape, p.dtype) for p in parts] + [pltpu.HBM(l.shape, l.dtype) for l in lands],
        input_output_aliases={i: i for i in range(2 * m)},
        compiler_params=pltpu.CompilerParams(has_side_effects=EFFECT),
    )(*parts, *lands, send, recv, *after)
    return res[:m], res[m:]


def sum_partials(part, land, ids, layer, depth, into):
    _, rh, c = part.shape
    tr = min(rh, 256)
    nt = rh // tr

    def body(ids_ref, p_ref, l_ref, *rest):
        o_ref = rest[-1]
        acc = p_ref[...].astype(F32)
        for k in range(N_CHIP - 1):
            acc = acc + l_ref[k].astype(F32)
        o_ref[...] = acc

    in_specs = [pl.BlockSpec((None, tr, c), lambda i, ids_ref: (ids_ref[0], i, 0)),
                pl.BlockSpec((N_CHIP - 1, tr, c), lambda i, ids_ref: (0, i, 0))]
    args = [ids, part, land]
    aliases = {}
    if into is not None:
        in_specs.append(ANY)
        args.append(into)
        aliases = {3: 0}
    return pl.pallas_call(
        body, name="sum_partials",
        grid_spec=pltpu.PrefetchScalarGridSpec(
            num_scalar_prefetch=1, grid=(nt,), in_specs=in_specs,
            out_specs=pl.BlockSpec((None, tr, c), lambda i, ids_ref: (layer, ids_ref[1] * nt + i, 0))),
        out_shape=jax.ShapeDtypeStruct((depth, 2 * rh, c), F32), input_output_aliases=aliases,
        compiler_params=_params(("parallel",)),
    )(*args)


def exchange_final_halves(bufs, layers):
    n = len(bufs)

    def body(*refs):
        outs = refs[n:2 * n]
        send, recv = refs[2 * n:]
        x, y, c = _axes()
        sib = (x, y, 1 - c)
        cps, at = [], 0
        for a in range(n):
            rh = outs[a].shape[1] // 2
            for l in layers[a]:
                mine = outs[a].at[l, pl.ds(c * rh, rh), :]
                cp = _remote(mine, mine, send.at[at], recv.at[at], sib)
                cp.start()
                cps.append(cp)
                at += 1
        at = 0
        for a in range(n):
            rh = outs[a].shape[1] // 2
            for l in layers[a]:
                theirs = outs[a].at[l, pl.ds((1 - c) * rh, rh), :]
                _remote(theirs, theirs, send.at[at], recv.at[at], sib).wait_recv()
                at += 1
        for cp in cps:
            cp.wait_send()

    ncp = sum(len(ls) for ls in layers)
    return pl.pallas_call(
        body, name="exchange_final_halves", in_specs=[ANY] * n, out_specs=[ANY] * n,
        out_shape=[jax.ShapeDtypeStruct(g.shape, g.dtype) for g in bufs],
        input_output_aliases={i: i for i in range(n)},
        scratch_shapes=[pltpu.SemaphoreType.DMA((ncp,))] * 2,
    )(*bufs)


def allgather_small(pk):
    def body(in_ref, out_ref, send, recv):
        x, y, c = _axes()
        me = 2 * x + y
        chips = _other_chips(x, y)
        out_ref[pl.ds(me, 1)] = in_ref[...][None]
        cps = []
        for k, (px, py) in enumerate(chips):
            cp = _remote(in_ref, out_ref.at[me], send.at[k], recv.at[k], (px, py, c))
            cp.start()
            cps.append(cp)
        for k, (px, py) in enumerate(chips):
            got = out_ref.at[2 * px + py]
            _remote(got, got, send.at[k], recv.at[k], (px, py, c)).wait_recv()
        for cp in cps:
            cp.wait_send()

    return pl.pallas_call(
        body, name="allgather_small", in_specs=[VMEM_WHOLE], out_specs=VMEM_WHOLE,
        out_shape=jax.ShapeDtypeStruct((N_CHIP,) + pk.shape, pk.dtype),
        scratch_shapes=[pltpu.SemaphoreType.DMA((3,))] * 2,
    )(pk)


def allreduce_small(g):
    ndev = 8

    def body(in_ref, out_ref, slots, send, recv):
        x, y, c = _axes()
        me = 4 * x + 2 * y + c
        slots[pl.ds(me, 1)] = in_ref[...][None]
        peers = []
        for mask in range(1, ndev):
            px = 1 - x if mask & 4 else x
            py = 1 - y if mask & 2 else y
            pc = 1 - c if mask & 1 else c
            peers.append((px, py, pc))
        cps = []
        for k, peer in enumerate(peers):
            cp = _remote(in_ref, slots.at[me], send.at[k], recv.at[k], peer)
            cp.start()
            cps.append(cp)
        for k, (px, py, pc) in enumerate(peers):
            got = slots.at[4 * px + 2 * py + pc]
            _remote(got, got, send.at[k], recv.at[k], (px, py, pc)).wait_recv()
        for cp in cps:
            cp.wait_send()
        acc = slots[0]
        for d in range(1, ndev):
            acc = acc + slots[d]
        out_ref[...] = acc

    return pl.pallas_call(
        body, name="allreduce_small", in_specs=[VMEM_WHOLE], out_specs=VMEM_WHOLE,
        out_shape=jax.ShapeDtypeStruct(g.shape, g.dtype),
        scratch_shapes=[pltpu.VMEM((ndev,) + g.shape, g.dtype), pltpu.SemaphoreType.DMA((ndev - 1,)),
                        pltpu.SemaphoreType.DMA((ndev - 1,))],
    )(g)


def adamw(w, g, m, v):
    shape = w.shape
    cols = shape[-1]
    rows = int(np.prod(shape[:-1]))
    tr = rows
    for cand in (256, 128):
        if rows % cand == 0 and cand * cols * 4 <= 2 * 1024 * 1024:
            tr = cand
            break
    c1 = 1.0 - ADAM_B1 ** ADAM_STEP
    c2 = 1.0 - ADAM_B2 ** ADAM_STEP

    def body(w_ref, g_ref, m_ref, v_ref, d_ref, nm_ref, nv_ref, g_out):
        gv = g_ref[...]
        g_out[...] = gv
        nm = ADAM_B1 * m_ref[...] + (1.0 - ADAM_B1) * gv
        nv = ADAM_B2 * v_ref[...] + (1.0 - ADAM_B2) * jnp.square(gv)
        d_ref[...] = -ADAM_LR * ((nm / c1) / (jnp.sqrt(nv / c2) + ADAM_EPS) + ADAM_WD * w_ref[...])
        nm_ref[...] = nm
        nv_ref[...] = nv

    spec = pl.BlockSpec((tr, cols), lambda i: (i, 0))
    res = pl.pallas_call(
        body, name="adamw", grid=(rows // tr,), in_specs=[spec] * 4, out_specs=[spec] * 4,
        out_shape=[jax.ShapeDtypeStruct((rows, cols), F32)] * 4, compiler_params=_params(("parallel",)),
    )(*[a.reshape(rows, cols) for a in (w, g, m, v)])
    return [r.reshape(shape) for r in res]


WEIGHTS = ("norm_g", "ffn1_w_gu", "ffn1_w_down", "w_in", "w_ret_o", "sc_conv_w", "w_sc_o", "cf_dw_w", "cf_dw_b",
           "cf_ln_g", "cf_ln_b", "w_cf_o", "w_o", "ffn2_w_gu", "ffn2_w_down")
SHARDED_SMALL = ("norm_g", "sc_conv_w", "cf_dw_w")
REPLICATED_SMALL = ("cf_dw_b", "cf_ln_g", "cf_ln_b")

def _pack_rows(parts):
    padded, offs, at = [], [], 0
    for p in parts:
        r = -(-p.shape[0] // SUBLANES) * SUBLANES
        padded.append(jnp.pad(p, ((0, r - p.shape[0]), (0, 0))))
        offs.append(at)
        at += r
    return jnp.concatenate(padded, axis=0), offs


def kernel(x, positions, norm_g, ffn1_w_gu, ffn1_w_down, w_in, w_ret_o, sc_conv_w, w_sc_o, cf_dw_w, cf_dw_b, cf_ln_g, cf_ln_b, w_cf_o, w_o, ffn2_w_gu, ffn2_w_down, loss_target, m_norm_g, m_ffn1_w_gu, m_ffn1_w_down, m_w_in, m_w_ret_o, m_sc_conv_w, m_w_sc_o, m_cf_dw_w, m_cf_dw_b, m_cf_ln_g, m_cf_ln_b, m_w_cf_o, m_w_o, m_ffn2_w_gu, m_ffn2_w_down, v_norm_g, v_ffn1_w_gu, v_ffn1_w_down, v_w_in, v_w_ret_o, v_sc_conv_w, v_w_sc_o, v_cf_dw_w, v_cf_dw_b, v_cf_ln_g, v_cf_ln_b, v_w_cf_o, v_w_o, v_ffn2_w_gu, v_ffn2_w_down):
    wts = dict(zip(WEIGHTS, (norm_g, ffn1_w_gu, ffn1_w_down, w_in, w_ret_o, sc_conv_w, w_sc_o, cf_dw_w, cf_dw_b,
                             cf_ln_g, cf_ln_b, w_cf_o, w_o, ffn2_w_gu, ffn2_w_down)))
    mom = dict(zip(WEIGHTS, (m_norm_g, m_ffn1_w_gu, m_ffn1_w_down, m_w_in, m_w_ret_o, m_sc_conv_w, m_w_sc_o,
                             m_cf_dw_w, m_cf_dw_b, m_cf_ln_g, m_cf_ln_b, m_w_cf_o, m_w_o, m_ffn2_w_gu, m_ffn2_w_down)))
    var = dict(zip(WEIGHTS, (v_norm_g, v_ffn1_w_gu, v_ffn1_w_down, v_w_in, v_w_ret_o, v_sc_conv_w, v_w_sc_o,
                             v_cf_dw_w, v_cf_dw_b, v_cf_ln_g, v_cf_ln_b, v_w_cf_o, v_w_o, v_ffn2_w_gu, v_ffn2_w_down)))
    depth = norm_g.shape[0]
    dq = norm_g.shape[-1]
    d = N_CHIP * dq
    chip = 2 * lax.axis_index("x") + lax.axis_index("y")
    ids = jnp.stack([chip, lax.axis_index("c")]).astype(jnp.int32)

    pk, offs = _pack_rows([wts[n].reshape(-1, dq) for n in SHARDED_SMALL])
    gk4 = allgather_small(pk)
    gk = gk4.transpose(1, 0, 2).reshape(pk.shape[0], d)
    small = {n: wts[n] for n in REPLICATED_SMALL}
    for n, o in zip(SHARDED_SMALL, offs):
        rows = wts[n].shape[0] * wts[n].shape[1]
        small[n] = gk[o:o + rows].reshape(wts[n].shape[:2] + (d,))

    order = [(l, blk) for l in range(depth) for blk in BLOCKS]
    def placed(groups, after):
        return [[place_quarter(wts[n], l, ids, after) for n in BLOCK_WEIGHTS[blk]] for l, blk in groups]

    first, token = gather_start("gather_start_first", placed(order[:1], gk4), gk4)
    rest, token = gather_start("gather_start_rest", placed(order[1:], token), token)
    started = dict(zip(order, first + rest))
    small["norm_g"] = small["norm_g"] + token[0:1, 0:1]

    def fetch(l, blk, after):
        send, recv, lands = started[l, blk]
        lands = gather_wait(lands, send, recv, after)
        return dict(zip(BLOCK_WEIGHTS[blk], sibling_fill(lands)))

    gsum = {n: None for n in BIG}
    inflight = []

    def land_sums(after):
        (l, blk), parts, lands, send, recv = inflight.pop()
        parts, lands = scatter_wait(parts, lands, send, recv, after)
        for n, pt, ld in zip(BLOCK_WEIGHTS[blk], parts, lands):
            gsum[n] = sum_partials(pt, ld, ids, l, depth, gsum[n])

    def push(l, blk, grads):
        gl = [grads[n] for n in BLOCK_WEIGHTS[blk]]
        parts = [add_halves(g, ld, ids) for g, ld in zip(gl, exchange_sibling_halves(gl))]
        send, recv, thru, lands, tok = scatter_start(parts)
        if inflight:
            land_sums((parts[0],))
        inflight.append(((l, blk), thru, lands, send, recv))
        return tok[0:1, 0:1]

    loss, grad_x, gs = local_step(x, positions, loss_target, small, fetch, push)

    names = SHARDED_SMALL + REPLICATED_SMALL
    pg, offs = _pack_rows([gs[n].reshape(-1, d) for n in names])
    tot = allreduce_small(pg)
    grads = {}
    for n, o in zip(names, offs):
        rows = int(np.prod(gs[n].shape[:-1]))
        full = tot[o:o + rows]
        if n in SHARDED_SMALL:
            full = lax.dynamic_slice_in_dim(full, chip * dq, dq, axis=1)
        grads[n] = full.reshape(wts[n].shape)

    last = BLOCK_WEIGHTS[order[0][1]]
    early = [n for n in BIG if n not in last]
    every = tuple(range(depth))
    done = exchange_final_halves([gsum[n] for n in early + list(last)],
                                 [every] * len(early) + [every[1:]] * len(last))
    for n, g in zip(early + list(last), done):
        gsum[n] = g
    delta, new_m, new_v = {}, {}, {}
    for n in WEIGHTS:
        if n not in last:
            g = gsum[n] if n in BIG else grads[n]
            delta[n], new_m[n], new_v[n], grads[n] = adamw(wts[n], g, mom[n], var[n])
    land_sums(tuple(delta[n] for n in WEIGHTS if n not in last) + (grad_x,))
    done = exchange_final_halves([gsum[n] for n in last], [every[:1]] * len(last))
    for n, g in zip(last, done):
        delta[n], new_m[n], new_v[n], grads[n] = adamw(wts[n], g, mom[n], var[n])

    loss_all = lax.psum(loss[0, 0], ("x", "y", "c"))
    return (loss_all, grad_x, *[grads[n] for n in WEIGHTS], *[delta[n] for n in WEIGHTS],
            *[new_m[n] for n in WEIGHTS], *[new_v[n] for n in WEIGHTS])
```

```python
import functools

import jax
import jax.numpy as jnp
import numpy as np
from jax import lax
from jax.experimental import pallas as pl
from jax.experimental.pallas import tpu as pltpu

F32 = jnp.float32
BF16 = jnp.bfloat16
MXU_DTYPE = BF16
VMEM_LIMIT_BYTES = 56 * 1024 * 1024
MESH = pl.DeviceIdType.MESH

N_CHIP = 4
CHUNK = 64
RET_HEADS = 4
RET_QK_DIM = 128
RET_V_DIM = 256
SC_KERNEL = 3
CF_KERNEL = 31
ROPE_BASE = 10000.0
NORM_EPS = 1e-6
LN_EPS = 1e-5
ADAM_LR = 0.001
ADAM_B1 = 0.9
ADAM_B2 = 0.999
ADAM_EPS = 1e-08
ADAM_WD = 0.01
ADAM_STEP = 10

SUBLANES = 8
CONV_PAD = 32
CONV_TS = 128
CONV_TC = 512
CONV_ROWS = 16
CONV_SCRATCH = [pltpu.VMEM((CONV_TS + CONV_PAD, CONV_TC), F32),
                pltpu.VMEM((SUBLANES - 1, CONV_TS + CONV_PAD - SUBLANES, CONV_TC), F32)]
RET_TQ = 512
MM_TM = 1024
MM_TN = 1536
MM_K1 = 1024
STREAM_ROWS = 128


def _params(sem):
    return pltpu.CompilerParams(dimension_semantics=sem, vmem_limit_bytes=VMEM_LIMIT_BYTES)


def _axes():
    return lax.axis_index("x"), lax.axis_index("y"), lax.axis_index("c")


NN = (((1,), (0,)), ((), ()))
NT = (((1,), (1,)), ((), ()))
TN = (((0,), (0,)), ((), ()))


def _mm(name, a, b, out_shape, out_dtype, grid, a_spec, b_spec, o_spec, dims, acc_shape):
    nk = grid[2]

    def body(a_ref, b_ref, o_ref, *scratch):
        bv = b_ref[...]
        if bv.ndim == 3:
            bv = bv.reshape(-1, bv.shape[-1])
        part = lax.dot_general(a_ref[...], bv, dims, preferred_element_type=F32)

        def put(v):
            o_ref[...] = v.reshape(o_ref.shape).astype(o_ref.dtype)

        if nk == 1:
            put(part)
        else:
            acc = scratch[0]
            k = pl.program_id(2)

            @pl.when(k == 0)
            def _():
                acc[...] = part

            @pl.when(k > 0)
            def _():
                acc[...] += part

            @pl.when(k == nk - 1)
            def _():
                put(acc[...])

    scratch = [pltpu.VMEM(acc_shape, F32)] if nk > 1 else []
    return pl.pallas_call(
        body, name=name, grid=grid, in_specs=[a_spec, b_spec], out_specs=o_spec,
        out_shape=jax.ShapeDtypeStruct(out_shape, out_dtype), scratch_shapes=scratch,
        compiler_params=_params(("parallel", "parallel", "arbitrary")),
    )(a, b)


def _tile(n, target):
    best = None
    for t in range(128, min(n, target) + 1, 128):
        if n % t == 0:
            best = t
    assert best is not None, (n, target)
    return best


def mm_fwd(name, a, w4, mode, out_dtype):
    t = a.shape[0]
    _, r, c = w4.shape
    tm = min(t, MM_TM)
    if mode == "col":
        tn = _tile(c, MM_TN)
        npj = c // tn
        grid = (t // tm, N_CHIP * npj, 1)
        a_spec = pl.BlockSpec((tm, r), lambda i, j, k: (i, 0))
        b_spec = pl.BlockSpec((None, r, tn), lambda i, j, k: (j // npj, 0, j % npj))
        o_spec = pl.BlockSpec((tm, tn), lambda i, j, k: (i, j))
        return _mm(name, a, w4, (t, N_CHIP * c), out_dtype, grid, a_spec, b_spec, o_spec, NN, (tm, tn))
    if N_CHIP * r <= MM_K1:
        grid = (t // tm, 1, 1)
        a_spec = pl.BlockSpec((tm, N_CHIP * r), lambda i, j, k: (i, 0))
        b_spec = pl.BlockSpec((N_CHIP, r, c), lambda i, j, k: (0, 0, 0))
        o_spec = pl.BlockSpec((tm, c), lambda i, j, k: (i, 0))
        return _mm(name, a, w4, (t, c), out_dtype, grid, a_spec, b_spec, o_spec, NN, (tm, c))
    grid = (t // tm, 1, N_CHIP)
    a_spec = pl.BlockSpec((tm, r), lambda i, j, k: (i, k))
    b_spec = pl.BlockSpec((None, r, c), lambda i, j, k: (k, 0, 0))
    o_spec = pl.BlockSpec((tm, c), lambda i, j, k: (i, 0))
    return _mm(name, a, w4, (t, c), out_dtype, grid, a_spec, b_spec, o_spec, NN, (tm, c))


def mm_dx(name, dy, w4, mode, out_dtype):
    t = dy.shape[-2]
    _, r, c = w4.shape
    tm = min(t, MM_TM)
    if mode == "col":
        tn = _tile(c, MM_TN)
        npj = c // tn
        hb = N_CHIP // 2 * npj
        grid = (t // tm, 1, N_CHIP * npj)
        if dy.ndim == 3:
            a_spec = pl.BlockSpec((None, tm, tn), lambda i, j, k: (k // hb, i, k % hb))
        else:
            a_spec = pl.BlockSpec((tm, tn), lambda i, j, k: (i, k))
        b_spec = pl.BlockSpec((None, r, tn), lambda i, j, k: (k // npj, 0, k % npj))
        o_spec = pl.BlockSpec((tm, r), lambda i, j, k: (i, 0))
        return _mm(name, dy, w4, (t, r), out_dtype, grid, a_spec, b_spec, o_spec, NT, (tm, r))
    if N_CHIP * r <= MM_K1:
        grid = (t // tm, 1, 1)
        a_spec = pl.BlockSpec((tm, c), lambda i, j, k: (i, 0))
        b_spec = pl.BlockSpec((N_CHIP, r, c), lambda i, j, k: (0, 0, 0))
        o_spec = pl.BlockSpec((tm, N_CHIP * r), lambda i, j, k: (i, 0))
        return _mm(name, dy, w4, (t, N_CHIP * r), out_dtype, grid, a_spec, b_spec, o_spec, NT, (tm, N_CHIP * r))
    grid = (t // tm, N_CHIP, 1)
    a_spec = pl.BlockSpec((tm, c), lambda i, j, k: (i, 0))
    b_spec = pl.BlockSpec((None, r, c), lambda i, j, k: (j, 0, 0))
    o_spec = pl.BlockSpec((tm, r), lambda i, j, k: (i, j))
    return _mm(name, dy, w4, (t, N_CHIP * r), out_dtype, grid, a_spec, b_spec, o_spec, NT, (tm, r))


def mm_dw(name, a, dy, mode, shape3):
    t = a.shape[0]
    _, r, c = shape3
    if mode == "col":
        tt = min(t, MM_TM)
        tn = _tile(c, MM_TN)
        npj = c // tn
        grid = (1, N_CHIP * npj, t // tt)
        a_spec = pl.BlockSpec((tt, r), lambda i, j, k: (k, 0))
        hb = N_CHIP // 2 * npj
        if dy.ndim == 3:
            b_spec = pl.BlockSpec((None, tt, tn), lambda i, j, k: (j // hb, k, j % hb))
        else:
            b_spec = pl.BlockSpec((tt, tn), lambda i, j, k: (k, j))
        o_spec = pl.BlockSpec((None, r, tn), lambda i, j, k: (j // npj, 0, j % npj))
        return _mm(name, a, dy, shape3, MXU_DTYPE, grid, a_spec, b_spec, o_spec, TN, (r, tn))
    if N_CHIP * r <= MM_K1:
        tt = min(t, 2 * MM_TM)
        grid = (1, 1, t // tt)
        a_spec = pl.BlockSpec((tt, N_CHIP * r), lambda i, j, k: (k, 0))
        b_spec = pl.BlockSpec((tt, c), lambda i, j, k: (k, 0))
        o_spec = pl.BlockSpec((N_CHIP, r, c), lambda i, j, k: (0, 0, 0))
        return _mm(name, a, dy, shape3, MXU_DTYPE, grid, a_spec, b_spec, o_spec, TN, (N_CHIP * r, c))
    tt = min(t, MM_TM)
    grid = (N_CHIP, 1, t // tt)
    a_spec = pl.BlockSpec((tt, r), lambda i, j, k: (k, i))
    b_spec = pl.BlockSpec((tt, c), lambda i, j, k: (k, 0))
    o_spec = pl.BlockSpec((None, r, c), lambda i, j, k: (i, 0, 0))
    return _mm(name, a, dy, shape3, MXU_DTYPE, grid, a_spec, b_spec, o_spec, TN, (r, c))


def _rowwise(name, fn, rows, pars, outs, accs=(), tm=256, ncol=1):
    t = rows[0][0].shape[0]
    nrow, npar, nout = len(rows), len(pars), len(outs)

    def body(*refs):
        vals = [r[...] for r in refs[:nrow + npar]]
        res = fn(*vals)
        out_refs = refs[nrow + npar:nrow + npar + nout]
        acc_refs = refs[nrow + npar + nout:]
        for o, v in zip(out_refs, res[:nout]):
            o[...] = v.astype(o.dtype)
        i = pl.program_id(1)
        for a, v in zip(acc_refs, res[nout:]):
            @pl.when(i == 0)
            def _(a=a, v=v):
                a[...] = v.astype(F32)

            @pl.when(i > 0)
            def _(a=a, v=v):
                a[...] += v.astype(F32)

    in_specs = [pl.BlockSpec((tm, w), functools.partial(lambda j, i, b: (i, b + j), b=b)) for _, w, b in rows]
    for arr, w in pars:
        if w is None:
            in_specs.append(pl.BlockSpec(arr.shape, lambda j, i: (0, 0)))
        else:
            in_specs.append(pl.BlockSpec((1, w), lambda j, i: (0, j)))
    out_specs = [pl.BlockSpec((tm, w), lambda j, i: (i, j)) for _, w, _ in outs]
    out_specs += [pl.BlockSpec((1, w), lambda j, i: (0, j)) for _, w in accs]
    out_shape = [jax.ShapeDtypeStruct((t, tw), dt) for tw, _, dt in outs]
    out_shape += [jax.ShapeDtypeStruct((1, tw), F32) for tw, _ in accs]
    res = pl.pallas_call(
        body, name=name, grid=(ncol, t // tm), in_specs=in_specs, out_specs=out_specs, out_shape=out_shape,
        compiler_params=_params(("parallel", "arbitrary" if accs else "parallel")),
    )(*[r[0] for r in rows], *[p[0] for p in pars])
    return res


def _rms(x, g):
    xf = x.astype(F32)
    return xf * lax.rsqrt(jnp.mean(xf * xf, axis=-1, keepdims=True) + NORM_EPS) * g


def _silu(x):
    return x * jax.nn.sigmoid(x)


def rms_fwd(name, x, g):
    d = x.shape[1]
    return _rowwise(name, lambda x, g: (_rms(x, g),), [(x, d, 0)], [(g, None)], [(d, d, MXU_DTYPE)], tm=512)[0]


def rms_bwd(name, x, g, dh, dres):
    d = x.shape[1]

    def fn(x, dh, dres, g):
        _, vjp = jax.vjp(_rms, x, g)
        dx, dg = vjp(dh.astype(F32))
        return dres + dx, dg

    return _rowwise(name, fn, [(x, d, 0), (dh, d, 0), (dres, d, 0)], [(g, None)], [(d, d, F32)], [(d, d)], tm=256)


def post_fwd(name, x, y, g, scale):
    d = x.shape[1]
    return _rowwise(name, lambda x, y, g: (x + scale * _rms(y, g),), [(x, d, 0), (y, d, 0)], [(g, None)],
                    [(d, d, F32)], tm=512)[0]


def post_bwd(name, y, g, dx, scale):
    d = y.shape[1]

    def fn(y, dx, g):
        _, vjp = jax.vjp(lambda y, g: scale * _rms(y, g), y, g)
        return vjp(dx)

    return _rowwise(name, fn, [(y, d, 0), (dx, d, 0)], [(g, None)], [(d, d, MXU_DTYPE)], [(d, d)], tm=256)


def ffn_up(name, h, w4):
    t = h.shape[0]
    _, r, c = w4.shape
    tm = min(t, MM_TM)
    tn = _tile(c, MM_TM)
    npj = c // tn
    half = N_CHIP // 2

    def body(h_ref, wg_ref, wu_ref, gu_ref, a_ref):
        hv = h_ref[...]
        g = lax.dot_general(hv, wg_ref[...], NN, preferred_element_type=F32)
        u = lax.dot_general(hv, wu_ref[...], NN, preferred_element_type=F32)
        gu_ref[0] = g.astype(gu_ref.dtype)
        gu_ref[1] = u.astype(gu_ref.dtype)
        a_ref[...] = (_silu(g) * u).astype(a_ref.dtype)

    f = half * c
    return pl.pallas_call(
        body, name=name, grid=(t // tm, half * npj),
        in_specs=[pl.BlockSpec((tm, r), lambda i, j: (i, 0)),
                  pl.BlockSpec((None, r, tn), lambda i, j: (j // npj, 0, j % npj)),
                  pl.BlockSpec((None, r, tn), lambda i, j: (half + j // npj, 0, j % npj))],
        out_specs=[pl.BlockSpec((2, tm, tn), lambda i, j: (0, i, j)), pl.BlockSpec((tm, tn), lambda i, j: (i, j))],
        out_shape=[jax.ShapeDtypeStruct((2, t, f), MXU_DTYPE), jax.ShapeDtypeStruct((t, f), MXU_DTYPE)],
        compiler_params=_params(("parallel", "parallel")),
    )(h, w4, w4)


def ffn_down_dx(name, dy, w4, gu):
    t = dy.shape[0]
    _, r, c = w4.shape
    tm = min(t, MM_TM)

    def body(dy_ref, w_ref, gu_ref, o_ref):
        da = lax.dot_general(dy_ref[...], w_ref[...], NT, preferred_element_type=F32)
        gate, up = gu_ref[0].astype(F32), gu_ref[1].astype(F32)
        sg = jax.nn.sigmoid(gate)
        o_ref[0] = (da * up * (sg * (1.0 + gate * (1.0 - sg)))).astype(o_ref.dtype)
        o_ref[1] = (da * (gate * sg)).astype(o_ref.dtype)

    return pl.pallas_call(
        body, name=name, grid=(t // tm, N_CHIP),
        in_specs=[pl.BlockSpec((tm, c), lambda i, j: (i, 0)), pl.BlockSpec((None, r, c), lambda i, j: (j, 0, 0)),
                  pl.BlockSpec((2, tm, r), lambda i, j: (0, i, j))],
        out_specs=pl.BlockSpec((2, tm, r), lambda i, j: (0, i, j)),
        out_shape=jax.ShapeDtypeStruct((2, t, N_CHIP * r), MXU_DTYPE),
        compiler_params=_params(("parallel", "parallel")),
    )(dy, w4, gu)


def _head_gate(o, g):
    mu = jnp.mean(o, axis=-1, keepdims=True)
    var = jnp.mean(jnp.square(o - mu), axis=-1, keepdims=True)
    return _silu(g.astype(F32)) * ((o - mu) * lax.rsqrt(var + LN_EPS))


def head_gate_fwd(name, o, p, gate_blk):
    dv = RET_V_DIM
    return _rowwise(name, lambda o, g: (_head_gate(o, g),), [(o, dv, 0), (p, dv, gate_blk)], [],
                    [(RET_HEADS * dv, dv, MXU_DTYPE)], tm=512, ncol=RET_HEADS)[0]


def head_gate_bwd(name, o, p, gate_blk, da):
    dv = RET_V_DIM

    def fn(o, g, da):
        _, vjp = jax.vjp(_head_gate, o, g.astype(F32))
        return vjp(da.astype(F32))

    w = RET_HEADS * dv
    return _rowwise(name, fn, [(o, dv, 0), (p, dv, gate_blk), (da, dv, 0)], [],
                    [(w, dv, MXU_DTYPE), (w, dv, MXU_DTYPE)], tm=512, ncol=RET_HEADS)


def _ln_silu(u, g, b):
    mu = jnp.mean(u, axis=-1, keepdims=True)
    var = jnp.mean(jnp.square(u - mu), axis=-1, keepdims=True)
    return _silu((u - mu) * lax.rsqrt(var + LN_EPS) * g + b)


def ln_silu_fwd(name, u, g, b):
    d = u.shape[1]
    return _rowwise(name, lambda u, g, b: (_ln_silu(u, g, b),), [(u, d, 0)], [(g, None), (b, None)],
                    [(d, d, MXU_DTYPE)], tm=512)[0]


def ln_silu_bwd(name, u, g, b, dc):
    d = u.shape[1]

    def fn(u, dc, g, b):
        _, vjp = jax.vjp(_ln_silu, u, g, b)
        return vjp(dc.astype(F32))

    return _rowwise(name, fn, [(u, d, 0), (dc, d, 0)], [(g, None), (b, None)], [(d, d, F32)], [(d, d), (d, d)],
                    tm=256)


def _merge(g0, g1, g2, ya, yb, yc):
    s = jax.nn.sigmoid
    return s(g0.astype(F32)) * ya + s(g1.astype(F32)) * yb + s(g2.astype(F32)) * yc


def merge_fwd(name, p, blk, ya, yb, yc):
    d = ya.shape[1]
    rows = [(p, d, blk), (p, d, blk + 1), (p, d, blk + 2), (ya, d, 0), (yb, d, 0), (yc, d, 0)]
    return _rowwise(name, lambda *v: (_merge(*v),), rows, [], [(d, d, MXU_DTYPE)], tm=256)[0]


def merge_bwd(name, p, blk, ya, yb, yc, dmg):
    d = ya.shape[1]

    def fn(g0, g1, g2, ya, yb, yc, dmg):
        _, vjp = jax.vjp(_merge, g0.astype(F32), g1.astype(F32), g2.astype(F32), ya, yb, yc)
        return vjp(dmg.astype(F32))

    rows = [(p, d, blk), (p, d, blk + 1), (p, d, blk + 2), (ya, d, 0), (yb, d, 0), (yc, d, 0), (dmg, d, 0)]
    return _rowwise(name, fn, rows, [], [(d, d, MXU_DTYPE)] * 6, tm=256)


def concat_cols(name, pieces):
    t = pieces[0].shape[0]
    widths = [p.shape[1] for p in pieces]
    tm = 256

    def body(*refs):
        o_ref, at = refs[-1], 0
        for r, w in zip(refs[:-1], widths):
            o_ref[:, at:at + w] = r[...]
            at += w

    return pl.pallas_call(
        body, name=name, grid=(t // tm,),
        in_specs=[pl.BlockSpec((tm, w), lambda i: (i, 0)) for w in widths],
        out_specs=pl.BlockSpec((tm, sum(widths)), lambda i: (i, 0)),
        out_shape=jax.ShapeDtypeStruct((t, sum(widths)), pieces[0].dtype),
        compiler_params=_params(("parallel",)),
    )(*pieces)


def loss_head(name, y, target):
    t, d = y.shape
    tm = 512

    def body(y_ref, t_ref, dy_ref, loss_ref):
        err = y_ref[...] - t_ref[...]
        dy_ref[...] = err * (1.0 / d)
        part = jnp.sum(jnp.sum(err * err, axis=1, keepdims=True), axis=0, keepdims=True) * (0.5 / d)

        @pl.when(pl.program_id(0) == 0)
        def _():
            loss_ref[...] = part

        @pl.when(pl.program_id(0) > 0)
        def _():
            loss_ref[...] += part

    return pl.pallas_call(
        body, name=name, grid=(t // tm,),
        in_specs=[pl.BlockSpec((tm, d), lambda i: (i, 0))] * 2,
        out_specs=[pl.BlockSpec((tm, d), lambda i: (i, 0)), pl.BlockSpec((1, 1), lambda i: (0, 0))],
        out_shape=[jax.ShapeDtypeStruct((t, d), F32), jax.ShapeDtypeStruct((1, 1), F32)],
        compiler_params=_params(("arbitrary",)),
    )(y, target)


def _rot(x, cos2, sin2):
    return x * cos2 + pltpu.roll(x, RET_QK_DIM // 2, 1) * sin2


def _decay_mask(lg, n0, rows, cols):
    n = n0 + lax.broadcasted_iota(jnp.int32, (rows, cols), 0)
    m = lax.broadcasted_iota(jnp.int32, (rows, cols), 1)
    shift = CHUNK.bit_length() - 1
    dist = jnp.abs(n - m).astype(F32)
    return jnp.where((m >> shift) <= (n >> shift), jnp.exp(lg * dist), 0.0)


def _ret_specs(s):
    dk, dv, h = RET_QK_DIM, RET_V_DIM, RET_HEADS
    return [
        pl.BlockSpec((s, dk), lambda b, hh: (b, hh)),
        pl.BlockSpec((s, dk), lambda b, hh: (b, h + hh)),
        pl.BlockSpec((s, dv), lambda b, hh: (b, (2 * h * dk) // dv + hh)),
        pl.BlockSpec((s, dk), lambda b, hh: (b, 0)),
        pl.BlockSpec((s, dk), lambda b, hh: (b, 0)),
        pl.BlockSpec((None, 1, dk), lambda b, hh: (hh, 0, 0)),
    ]


def retention_fwd(name, p, cos2, sin2, log_g, nb, s):
    dk, dv, h = RET_QK_DIM, RET_V_DIM, RET_HEADS

    def body(q_ref, k_ref, v_ref, cos_ref, sin_ref, lg_ref, o_ref, kr_ref):
        lg = lg_ref[0:1, 0:1]
        kr = _rot(k_ref[...].astype(F32), cos_ref[...], sin_ref[...]) * (dk ** -0.5)
        kr_ref[...] = kr.astype(kr_ref.dtype)
        for qi in range(s // RET_TQ):
            n0, kmax = qi * RET_TQ, (qi + 1) * RET_TQ
            rows = pl.ds(n0, RET_TQ)
            qr = _rot(q_ref[rows, :].astype(F32), cos_ref[rows, :], sin_ref[rows, :]).astype(MXU_DTYPE)
            sc = lax.dot_general(qr, kr_ref[0:kmax, :], NT, preferred_element_type=F32)
            pm = (sc * _decay_mask(lg, n0, RET_TQ, kmax)).astype(MXU_DTYPE)
            o_ref[rows, :] = lax.dot_general(pm, v_ref[0:kmax, :], NN, preferred_element_type=F32)

    return pl.pallas_call(
        body, name=name, grid=(nb, h), in_specs=_ret_specs(s),
        out_specs=pl.BlockSpec((s, dv), lambda b, hh: (b, hh)),
        out_shape=jax.ShapeDtypeStruct((nb * s, h * dv), F32),
        scratch_shapes=[pltpu.VMEM((s, dk), MXU_DTYPE)],
        compiler_params=_params(("parallel", "parallel")),
    )(p, p, p, cos2, sin2, log_g)


def retention_bwd(name, p, cos2, sin2, log_g, do, nb, s):
    dk, dv, h = RET_QK_DIM, RET_V_DIM, RET_HEADS

    def body(q_ref, k_ref, v_ref, cos_ref, sin_ref, lg_ref, do_ref, dq_ref, dk_ref, dv_ref, kr_ref, dk_acc, dv_acc):
        lg = lg_ref[0:1, 0:1]
        kr = _rot(k_ref[...].astype(F32), cos_ref[...], sin_ref[...]) * (dk ** -0.5)
        kr_ref[...] = kr.astype(kr_ref.dtype)
        dk_acc[...] = jnp.zeros_like(dk_acc)
        dv_acc[...] = jnp.zeros_like(dv_acc)
        for qi in range(s // RET_TQ):
            n0, kmax = qi * RET_TQ, (qi + 1) * RET_TQ
            rows = pl.ds(n0, RET_TQ)
            cq, sq = cos_ref[rows, :], sin_ref[rows, :]
            qr = _rot(q_ref[rows, :].astype(F32), cq, sq).astype(MXU_DTYPE)
            dob = do_ref[rows, :]
            mask = _decay_mask(lg, n0, RET_TQ, kmax)
            sc = lax.dot_general(qr, kr_ref[0:kmax, :], NT, preferred_element_type=F32)
            pm = (sc * mask).astype(MXU_DTYPE)
            dv_acc[0:kmax, :] += lax.dot_general(pm, dob, TN, preferred_element_type=F32)
            dp = lax.dot_general(dob, v_ref[0:kmax, :], NT, preferred_element_type=F32)
            ds = (dp * mask).astype(MXU_DTYPE)
            dqr = lax.dot_general(ds, kr_ref[0:kmax, :], NN, preferred_element_type=F32)
            dq_ref[rows, :] = _rot(dqr, cq, -sq).astype(dq_ref.dtype)
            dk_acc[0:kmax, :] += lax.dot_general(ds, qr, TN, preferred_element_type=F32)
        dkr = dk_acc[...] * (dk ** -0.5)
        dk_ref[...] = _rot(dkr, cos_ref[...], -sin_ref[...]).astype(dk_ref.dtype)
        dv_ref[...] = dv_acc[...].astype(dv_ref.dtype)

    t = nb * s
    return pl.pallas_call(
        body, name=name, grid=(nb, h),
        in_specs=_ret_specs(s) + [pl.BlockSpec((s, dv), lambda b, hh: (b, hh))],
        out_specs=[pl.BlockSpec((s, dk), lambda b, hh: (b, hh)), pl.BlockSpec((s, dk), lambda b, hh: (b, hh)),
                   pl.BlockSpec((s, dv), lambda b, hh: (b, hh))],
        out_shape=[jax.ShapeDtypeStruct((t, h * dk), MXU_DTYPE), jax.ShapeDtypeStruct((t, h * dk), MXU_DTYPE),
                   jax.ShapeDtypeStruct((t, h * dv), MXU_DTYPE)],
        scratch_shapes=[pltpu.VMEM((s, dk), MXU_DTYPE), pltpu.VMEM((s, dk), F32), pltpu.VMEM((s, dv), F32)],
        compiler_params=_params(("parallel", "parallel")),
    )(p, p, p, cos2, sin2, log_g, do)


def _conv_grid(t, d, nb):
    s = t // nb
    ns, nc = s // CONV_TS, d // CONV_TC
    return s, ns, nc


def _shifted(pad_ref, sh_ref, offsets):
    n = sh_ref.shape[1]
    for b in sorted({off % SUBLANES for off in offsets} - {0}):
        sh_ref[b - 1] = pad_ref[pl.ds(b, n), :]

    def read(off, r0):
        a, b = off - off % SUBLANES + r0, off % SUBLANES
        return pad_ref[pl.ds(a, CONV_ROWS), :] if b == 0 else sh_ref[b - 1, pl.ds(a, CONV_ROWS), :]

    return read


def _causal_taps(pad_ref, sh_ref, w_ref, k, emit):
    offs = [CONV_PAD - (k - 1) + j for j in range(k)]
    read = _shifted(pad_ref, sh_ref, offs)
    for r0 in range(0, CONV_TS, CONV_ROWS):
        acc = None
        for j in range(k):
            term = w_ref[j:j + 1, :] * read(offs[j], r0)
            acc = term if acc is None else acc + term
        emit(r0, acc)


def _carry_past(pad_ref, s_idx):
    @pl.when(s_idx == 0)
    def _():
        pad_ref[0:CONV_PAD, :] = jnp.zeros((CONV_PAD, pad_ref.shape[1]), F32)

    @pl.when(s_idx > 0)
    def _():
        pad_ref[0:CONV_PAD, :] = pad_ref[CONV_TS:CONV_TS + CONV_PAD, :]


def _carry_future(pad_ref, s_idx):
    @pl.when(s_idx == 0)
    def _():
        pad_ref[CONV_TS:CONV_TS + CONV_PAD, :] = jnp.zeros((CONV_PAD, pad_ref.shape[1]), F32)

    @pl.when(s_idx > 0)
    def _():
        pad_ref[CONV_TS:CONV_TS + CONV_PAD, :] = pad_ref[0:CONV_PAD, :]


def _conv_bwd_taps(pad_ref, sh_ref, w_ref, dw_acc, k, x_rows, emit, mix):
    read = _shifted(pad_ref, sh_ref, range(k))
    for r0 in range(0, CONV_TS, CONV_ROWS):
        ops = x_rows(r0)
        x = mix(ops)
        acc = None
        for j in range(k):
            sh = read(k - 1 - j, r0)
            term = w_ref[j:j + 1, :] * sh
            acc = term if acc is None else acc + term
            prod = x * sh
            part = prod[0:SUBLANES]
            for q in range(SUBLANES, CONV_ROWS, SUBLANES):
                part = part + prod[q:q + SUBLANES]
            dw_acc[j] += part
        emit(r0, ops, acc)


def _conv_bwd_edges(dw_acc, dw_ref, nb, ns, extra=()):
    first = jnp.logical_and(pl.program_id(1) == 0, pl.program_id(2) == 0)
    last = jnp.logical_and(pl.program_id(1) == nb - 1, pl.program_id(2) == ns - 1)

    @pl.when(first)
    def _():
        dw_acc[...] = jnp.zeros_like(dw_acc)
        for r in extra:
            r[...] = jnp.zeros_like(r)

    def finish():
        @pl.when(last)
        def _():
            dw_ref[...] = jnp.sum(dw_acc[...], axis=1)

    return finish


def short_conv_fwd(name, p, blk_b, w, nb):
    t = p.shape[0]
    d = w.shape[1]
    s, ns, nc = _conv_grid(t, d, nb)
    cb = d // CONV_TC

    def body(b_ref, c_ref, x_ref, w_ref, y_ref, cz_ref, pad_ref, sh_ref):
        _carry_past(pad_ref, pl.program_id(2))
        pad_ref[CONV_PAD:CONV_PAD + CONV_TS, :] = c_ref[...].astype(F32) * x_ref[...].astype(F32)

        def emit(r0, cz):
            rows = pl.ds(r0, CONV_ROWS)
            cz_ref[rows, :] = cz
            y_ref[rows, :] = (b_ref[rows, :].astype(F32) * cz).astype(y_ref.dtype)

        _causal_taps(pad_ref, sh_ref, w_ref, SC_KERNEL, emit)

    def pspec(off):
        return pl.BlockSpec((CONV_TS, CONV_TC), lambda c, b, si: (b * ns + si, (blk_b + off) * cb + c))

    ospec = pl.BlockSpec((CONV_TS, CONV_TC), lambda c, b, si: (b * ns + si, c))
    return pl.pallas_call(
        body, name=name, grid=(nc, nb, ns),
        in_specs=[pspec(0), pspec(1), pspec(2), pl.BlockSpec((SC_KERNEL, CONV_TC), lambda c, b, si: (0, c))],
        out_specs=[ospec, ospec],
        out_shape=[jax.ShapeDtypeStruct((t, d), MXU_DTYPE), jax.ShapeDtypeStruct((t, d), F32)],
        scratch_shapes=CONV_SCRATCH,
        compiler_params=_params(("parallel", "arbitrary", "arbitrary")),
    )(p, p, p, w)


def short_conv_bwd(name, p, blk_b, w, cz, dy, nb):
    t = p.shape[0]
    d = w.shape[1]
    s, ns, nc = _conv_grid(t, d, nb)
    cb = d // CONV_TC

    def body(b_ref, c_ref, x_ref, w_ref, cz_ref, dy_ref, db_ref, dc_ref, dx_ref, dw_ref, pad_ref, sh_ref, dw_acc):
        _carry_future(pad_ref, pl.program_id(2))
        dyv = dy_ref[...].astype(F32)
        db_ref[...] = (dyv * cz_ref[...]).astype(db_ref.dtype)
        pad_ref[0:CONV_TS, :] = dyv * b_ref[...].astype(F32)
        finish = _conv_bwd_edges(dw_acc, dw_ref, nb, ns)

        def x_rows(r0):
            rows = pl.ds(r0, CONV_ROWS)
            return c_ref[rows, :].astype(F32), x_ref[rows, :].astype(F32)

        def emit(r0, cx, dz):
            rows = pl.ds(r0, CONV_ROWS)
            dc_ref[rows, :] = (dz * cx[1]).astype(dc_ref.dtype)
            dx_ref[rows, :] = (dz * cx[0]).astype(dx_ref.dtype)

        _conv_bwd_taps(pad_ref, sh_ref, w_ref, dw_acc, SC_KERNEL, x_rows, emit, lambda cx: cx[0] * cx[1])
        finish()

    def row(b, si):
        return b * ns + (ns - 1 - si)

    def pspec(off):
        return pl.BlockSpec((CONV_TS, CONV_TC), lambda c, b, si: (row(b, si), (blk_b + off) * cb + c))

    ospec = pl.BlockSpec((CONV_TS, CONV_TC), lambda c, b, si: (row(b, si), c))
    wspec = pl.BlockSpec((SC_KERNEL, CONV_TC), lambda c, b, si: (0, c))
    return pl.pallas_call(
        body, name=name, grid=(nc, nb, ns),
        in_specs=[pspec(0), pspec(1), pspec(2), wspec, ospec, ospec],
        out_specs=[ospec, ospec, ospec, wspec],
        out_shape=[jax.ShapeDtypeStruct((t, d), MXU_DTYPE)] * 3 + [jax.ShapeDtypeStruct((SC_KERNEL, d), F32)],
        scratch_shapes=CONV_SCRATCH + [pltpu.VMEM((SC_KERNEL, SUBLANES, CONV_TC), F32)],
        compiler_params=_params(("parallel", "arbitrary", "arbitrary")),
    )(p, p, p, w, cz, dy)


def conformer_conv_fwd(name, p, blk_a, w, bias, nb):
    t = p.shape[0]
    d = w.shape[1]
    s, ns, nc = _conv_grid(t, d, nb)
    cb = d // CONV_TC

    def body(a_ref, b_ref, w_ref, bias_ref, u_ref, pad_ref, sh_ref):
        _carry_past(pad_ref, pl.program_id(2))
        pad_ref[CONV_PAD:CONV_PAD + CONV_TS, :] = a_ref[...].astype(F32) * jax.nn.sigmoid(b_ref[...].astype(F32))

        def emit(r0, u):
            u_ref[pl.ds(r0, CONV_ROWS), :] = u + bias_ref[...]

        _causal_taps(pad_ref, sh_ref, w_ref, CF_KERNEL, emit)

    def pspec(off):
        return pl.BlockSpec((CONV_TS, CONV_TC), lambda c, b, si: (b * ns + si, (blk_a + off) * cb + c))

    return pl.pallas_call(
        body, name=name, grid=(nc, nb, ns),
        in_specs=[pspec(0), pspec(1), pl.BlockSpec((CF_KERNEL, CONV_TC), lambda c, b, si: (0, c)),
                  pl.BlockSpec((1, CONV_TC), lambda c, b, si: (0, c))],
        out_specs=pl.BlockSpec((CONV_TS, CONV_TC), lambda c, b, si: (b * ns + si, c)),
        out_shape=jax.ShapeDtypeStruct((t, d), F32),
        scratch_shapes=CONV_SCRATCH,
        compiler_params=_params(("parallel", "arbitrary", "arbitrary")),
    )(p, p, w, bias)


def conformer_conv_bwd(name, p, blk_a, w, du, nb):
    t = p.shape[0]
    d = w.shape[1]
    s, ns, nc = _conv_grid(t, d, nb)
    cb = d // CONV_TC

    def body(a_ref, b_ref, w_ref, du_ref, da_ref, db_ref, dw_ref, dbias_ref, pad_ref, sh_ref, dw_acc):
        _carry_future(pad_ref, pl.program_id(2))
        duv = du_ref[...]
        pad_ref[0:CONV_TS, :] = duv
        finish = _conv_bwd_edges(dw_acc, dw_ref, nb, ns, extra=(dbias_ref,))
        dbias_ref[...] += jnp.sum(duv, axis=0, keepdims=True)

        def x_rows(r0):
            rows = pl.ds(r0, CONV_ROWS)
            return a_ref[rows, :].astype(F32), jax.nn.sigmoid(b_ref[rows, :].astype(F32))

        def emit(r0, asg, du0):
            rows = pl.ds(r0, CONV_ROWS)
            av, sg = asg
            da_ref[rows, :] = (du0 * sg).astype(da_ref.dtype)
            db_ref[rows, :] = (du0 * av * sg * (1.0 - sg)).astype(db_ref.dtype)

        _conv_bwd_taps(pad_ref, sh_ref, w_ref, dw_acc, CF_KERNEL, x_rows, emit, lambda asg: asg[0] * asg[1])
        finish()

    def row(b, si):
        return b * ns + (ns - 1 - si)

    def pspec(off):
        return pl.BlockSpec((CONV_TS, CONV_TC), lambda c, b, si: (row(b, si), (blk_a + off) * cb + c))

    ospec = pl.BlockSpec((CONV_TS, CONV_TC), lambda c, b, si: (row(b, si), c))
    wspec = pl.BlockSpec((CF_KERNEL, CONV_TC), lambda c, b, si: (0, c))
    bspec = pl.BlockSpec((1, CONV_TC), lambda c, b, si: (0, c))
    return pl.pallas_call(
        body, name=name, grid=(nc, nb, ns),
        in_specs=[pspec(0), pspec(1), wspec, ospec],
        out_specs=[ospec, ospec, wspec, bspec],
        out_shape=[jax.ShapeDtypeStruct((t, d), MXU_DTYPE)] * 2
        + [jax.ShapeDtypeStruct((CF_KERNEL, d), F32), jax.ShapeDtypeStruct((1, d), F32)],
        scratch_shapes=CONV_SCRATCH + [pltpu.VMEM((CF_KERNEL, SUBLANES, CONV_TC), F32)],
        compiler_params=_params(("parallel", "arbitrary", "arbitrary")),
    )(p, p, w, du)


BLOCKS = ("ffn1", "mixer", "ffn2")
BLOCK_WEIGHTS = {"ffn1": ("ffn1_w_gu", "ffn1_w_down"), "mixer": ("w_in", "w_ret_o", "w_sc_o", "w_cf_o", "w_o"),
                 "ffn2": ("ffn2_w_gu", "ffn2_w_down")}
BIG = BLOCK_WEIGHTS["ffn1"] + BLOCK_WEIGHTS["mixer"] + BLOCK_WEIGHTS["ffn2"]
MODE = {"ffn1_w_gu": "col", "ffn1_w_down": "row", "w_in": "col", "w_ret_o": "row", "w_sc_o": "row",
        "w_cf_o": "row", "w_o": "row", "ffn2_w_gu": "col", "ffn2_w_down": "row"}
NORM_OF = {"ffn1": 0, "mixer": 2, "ffn2": 4}
BLK_GATE, BLK_SCB, BLK_CFA, BLK_MERGE = 2, 3, 6, 8


def _rope_tables(positions):
    half = RET_QK_DIM // 2
    inv_freq = ROPE_BASE ** (-jnp.arange(half, dtype=F32) / half)
    ang = positions.astype(F32)[..., None] * inv_freq
    cos, sin = jnp.cos(ang), jnp.sin(ang)
    nb, s = positions.shape
    cos2 = jnp.concatenate([cos, cos], axis=-1).reshape(nb * s, RET_QK_DIM)
    sin2 = jnp.concatenate([-sin, sin], axis=-1).reshape(nb * s, RET_QK_DIM)
    return cos2, sin2


def _log_gamma():
    lg = jnp.log(1.0 - 2.0 ** (-5.0 - jnp.arange(RET_HEADS, dtype=F32)))
    return jnp.broadcast_to(lg[:, None, None], (RET_HEADS, 1, RET_QK_DIM))


def _ffn_fwd(xs, w, tag, g_pre, g_post):
    h = rms_fwd("ffn_rms", xs, g_pre)
    gu, a = ffn_up("ffn_up", h, w[tag + "_w_gu"])
    y = mm_fwd("ffn_down", a, w[tag + "_w_down"], "row", F32)
    out = post_fwd("ffn_post", xs, y, g_post, 0.5)
    return out, dict(x=xs, h=h, gu=gu, a=a, y=y, w=w)


def _ffn_bwd(dxs, sv, tag, g_pre, g_post):
    w = sv["w"]
    gu_w, down_w = w[tag + "_w_gu"], w[tag + "_w_down"]
    dy, dg_post = post_bwd("ffn_post_bwd", sv["y"], g_post, dxs, 0.5)
    dgu = ffn_down_dx("ffn_down_dx", dy, down_w, sv["gu"])
    grads = {tag + "_w_down": mm_dw("ffn_down_dw", sv["a"], dy, "row", down_w.shape)}
    dh = mm_dx("ffn_gu_dx", dgu, gu_w, "col", F32)
    grads[tag + "_w_gu"] = mm_dw("ffn_gu_dw", sv["h"], dgu, "col", gu_w.shape)
    dxs, dg_pre = rms_bwd("ffn_rms_bwd", sv["x"], g_pre, dh, dxs)
    return dxs, grads, dg_pre, dg_post


def _mixer_fwd(xs, w, sm, g_pre, g_post, rope, nb, s):
    cos2, sin2, log_g = rope
    d = xs.shape[1]
    gate_blk = (BLK_GATE * d) // RET_V_DIM
    h = rms_fwd("mx_rms", xs, g_pre)
    p = mm_fwd("mx_in", h, w["w_in"], "col", MXU_DTYPE)
    o = retention_fwd("ret_fwd", p, cos2, sin2, log_g, nb, s)
    ya_in = head_gate_fwd("ret_gate", o, p, gate_blk)
    yb_in, cz = short_conv_fwd("sc_fwd", p, BLK_SCB, sm["sc_conv_w"], nb)
    u1 = conformer_conv_fwd("cf_fwd", p, BLK_CFA, sm["cf_dw_w"], sm["cf_dw_b"], nb)
    yc_in = ln_silu_fwd("cf_ln", u1, sm["cf_ln_g"], sm["cf_ln_b"])
    ya = mm_fwd("mx_proj", ya_in, w["w_ret_o"], "row", F32)
    yb = mm_fwd("mx_proj", yb_in, w["w_sc_o"], "row", F32)
    yc = mm_fwd("mx_proj", yc_in, w["w_cf_o"], "row", F32)
    mg = merge_fwd("mx_merge", p, BLK_MERGE, ya, yb, yc)
    m = mm_fwd("mx_proj", mg, w["w_o"], "row", F32)
    out = post_fwd("mx_post", xs, m, g_post, 1.0)
    return out, dict(x=xs, h=h, p=p, o=o, ya_in=ya_in, yb_in=yb_in, cz=cz, u1=u1, yc_in=yc_in, ya=ya, yb=yb, yc=yc,
                     mg=mg, m=m, w=w)


def _mixer_bwd(dxs, sv, sm, g_pre, g_post, rope, nb, s):
    cos2, sin2, log_g = rope
    w, p = sv["w"], sv["p"]
    d = dxs.shape[1]
    gate_blk = (BLK_GATE * d) // RET_V_DIM
    grads, gsm = {}, {}

    def proj_bwd(wname, a_in, dy, out_dtype):
        grads[wname] = mm_dw("mx_proj_dw", a_in, dy, "row", w[wname].shape)
        return mm_dx("mx_proj_dx", dy, w[wname], "row", out_dtype)

    dm, dg_post = post_bwd("mx_post_bwd", sv["m"], g_post, dxs, 1.0)
    dmg = proj_bwd("w_o", sv["mg"], dm, MXU_DTYPE)
    dg0, dg1, dg2, dya, dyb, dyc = merge_bwd("mx_merge_bwd", p, BLK_MERGE, sv["ya"], sv["yb"], sv["yc"], dmg)
    dya_in = proj_bwd("w_ret_o", sv["ya_in"], dya, MXU_DTYPE)
    dyb_in = proj_bwd("w_sc_o", sv["yb_in"], dyb, MXU_DTYPE)
    dyc_in = proj_bwd("w_cf_o", sv["yc_in"], dyc, MXU_DTYPE)
    do, dgret = head_gate_bwd("ret_gate_bwd", sv["o"], p, gate_blk, dya_in)
    dq, dk, dv = retention_bwd("ret_bwd", p, cos2, sin2, log_g, do, nb, s)
    dscb, dscc, dscx, gsm["sc_conv_w"] = short_conv_bwd("sc_bwd", p, BLK_SCB, sm["sc_conv_w"], sv["cz"], dyb_in, nb)
    du1, dlg, dlb = ln_silu_bwd("cf_ln_bwd", sv["u1"], sm["cf_ln_g"], sm["cf_ln_b"], dyc_in)
    dcfa, dcfb, gsm["cf_dw_w"], dbias = conformer_conv_bwd("cf_bwd", p, BLK_CFA, sm["cf_dw_w"], du1, nb)
    gsm.update(cf_ln_g=dlg[0], cf_ln_b=dlb[0], cf_dw_b=dbias[0])
    dp = concat_cols("mx_dp", [dq, dk, dv, dgret, dscb, dscc, dscx, dcfa, dcfb, dg0, dg1, dg2])
    dh = mm_dx("mx_in_dx", dp, w["w_in"], "col", F32)
    grads["w_in"] = mm_dw("mx_in_dw", sv["h"], dp, "col", w["w_in"].shape)
    dxs, dg_pre = rms_bwd("mx_rms_bwd", sv["x"], g_pre, dh, dxs)
    return dxs, grads, gsm, dg_pre, dg_post


def local_step(x, positions, target, small, fetch, push):
    nb, s, d = x.shape
    t = nb * s
    depth = small["norm_g"].shape[0]
    rope = _rope_tables(positions) + (_log_gamma(),)
    xs = x.reshape(t, d)
    token = [None]

    def gain(l, i):
        g = small["norm_g"][l, i][None, :]
        if token[0] is not None:
            g, token[0] = g + token[0], None
        return g

    def mixer_small(l):
        return dict(sc_conv_w=small["sc_conv_w"][l], cf_dw_w=small["cf_dw_w"][l], cf_dw_b=small["cf_dw_b"][l][None, :],
                    cf_ln_g=small["cf_ln_g"][l][None, :], cf_ln_b=small["cf_ln_b"][l][None, :])

    saved = {}
    for l in range(depth):
        for blk in BLOCKS:
            w = fetch(l, blk, xs)
            i0 = NORM_OF[blk]
            if blk == "mixer":
                xs, saved[l, blk] = _mixer_fwd(xs, w, mixer_small(l), gain(l, i0), gain(l, i0 + 1), rope, nb, s)
            else:
                xs, saved[l, blk] = _ffn_fwd(xs, w, blk, gain(l, i0), gain(l, i0 + 1))

    dxs, loss = loss_head("loss", xs, target.reshape(t, d))

    dnorm = [[None] * 6 for _ in range(depth)]
    gsmall = {n: [None] * depth for n in ("sc_conv_w", "cf_dw_w", "cf_dw_b", "cf_ln_g", "cf_ln_b")}
    for l in reversed(range(depth)):
        for blk in reversed(BLOCKS):
            i0 = NORM_OF[blk]
            g_post, g_pre = gain(l, i0 + 1), gain(l, i0)
            if blk == "mixer":
                dxs, grads, gsm, dnorm[l][i0], dnorm[l][i0 + 1] = _mixer_bwd(
                    dxs, saved[l, blk], mixer_small(l), g_pre, g_post, rope, nb, s)
                for n, v in gsm.items():
                    gsmall[n][l] = v
            else:
                dxs, grads, dnorm[l][i0], dnorm[l][i0 + 1] = _ffn_bwd(dxs, saved[l, blk], blk, g_pre, g_post)
            token[0] = push(l, blk, grads)

    gs = {n: jnp.stack(v) for n, v in gsmall.items()}
    gs["norm_g"] = jnp.stack([jnp.concatenate(r, axis=0) for r in dnorm])
    return loss, dxs.reshape(nb, s, d), gs


ANY = pl.BlockSpec(memory_space=pl.ANY)
HBM = pl.BlockSpec(memory_space=pltpu.HBM)
SEM = pl.BlockSpec(memory_space=pltpu.SEMAPHORE)
VMEM_WHOLE = pl.BlockSpec(memory_space=pltpu.VMEM)
EFFECT = pltpu.SideEffectType.DATAFLOW_SIDE_EFFECTING
TOKEN = jax.ShapeDtypeStruct((8, 128), F32)


def _other_chips(x, y):
    return [(1 - x, y), (x, 1 - y), (1 - x, 1 - y)]


def _remote(src, dst, send_sem, recv_sem, to):
    return pltpu.make_async_remote_copy(src_ref=src, dst_ref=dst, send_sem=send_sem, recv_sem=recv_sem,
                                        device_id=to, device_id_type=MESH)


def _in_hbm(v):
    return pltpu.with_memory_space_constraint(v, pltpu.HBM)


def place_quarter(w, layer, ids, after):
    _, r, c = w.shape
    tr = min(r, STREAM_ROWS)

    def body(ids_ref, w_ref, after_ref, o_ref):
        o_ref[...] = w_ref[...].astype(o_ref.dtype)

    return pl.pallas_call(
        body, name="place_quarter",
        grid_spec=pltpu.PrefetchScalarGridSpec(
            num_scalar_prefetch=1, grid=(r // tr,),
            in_specs=[pl.BlockSpec((None, tr, c), lambda i, ids_ref: (layer, i, 0)), ANY],
            out_specs=pl.BlockSpec((None, tr, c), lambda i, ids_ref: (ids_ref[0], i, 0))),
        out_shape=jax.ShapeDtypeStruct((N_CHIP, r, c), MXU_DTYPE),
        compiler_params=_params(("parallel",)),
    )(ids, w, after)


def _gather_copies(lands, send, recv):
    x, y, c = _axes()
    me = 2 * x + y
    mine, theirs = [], []
    for a, ld in enumerate(lands):
        rh = ld.shape[1] // 2
        rows = pl.ds(c * rh, rh)
        for k, (px, py) in enumerate(_other_chips(x, y)):
            to = (px, py, c)
            mine.append(_remote(ld.at[me, rows, :], ld.at[me, rows, :], send.at[3 * a + k], recv.at[3 * a + k], to))
            got = ld.at[2 * px + py, rows, :]
            theirs.append(_remote(got, got, send.at[3 * a + k], recv.at[3 * a + k], to))
    return mine, theirs


def gather_start(name, groups, after):
    flat = [s for g in groups for s in g]
    n, ng = len(flat), len(groups)
    sizes = [len(g) for g in groups]

    def body(*refs):
        lands = refs[:n]
        sems = refs[n + 1:n + 1 + 2 * ng]
        token = refs[-1]
        at = 0
        for g, m in enumerate(sizes):
            mine, _ = _gather_copies(lands[at:at + m], sems[2 * g], sems[2 * g + 1])
            for cp in mine:
                cp.start()
            at += m
        token[...] = jnp.zeros_like(token)

    sem_shapes = []
    for m in sizes:
        sem_shapes += [pltpu.SemaphoreType.DMA((3 * m,))] * 2
    res = pl.pallas_call(
        body, name=name, in_specs=[HBM] * n + [ANY],
        out_specs=[SEM] * (2 * ng) + [HBM] * n + [VMEM_WHOLE],
        out_shape=sem_shapes + [pltpu.HBM(s.shape, s.dtype) for s in flat] + [TOKEN],
        input_output_aliases={i: 2 * ng + i for i in range(n)},
        compiler_params=pltpu.CompilerParams(has_side_effects=EFFECT),
    )(*[_in_hbm(s) for s in flat], after)
    sems, thru, token = res[:2 * ng], res[2 * ng:2 * ng + n], res[-1]
    out, at = [], 0
    for g, m in enumerate(sizes):
        out.append((sems[2 * g], sems[2 * g + 1], thru[at:at + m]))
        at += m
    return out, token


def gather_wait(lands, send, recv, after):
    m = len(lands)

    def body(*refs):
        mine, theirs = _gather_copies(refs[:m], refs[m], refs[m + 1])
        for cp in mine:
            cp.wait_send()
        for cp in theirs:
            cp.wait_recv()

    return pl.pallas_call(
        body, name="gather_wait", in_specs=[HBM] * m + [SEM, SEM, ANY], out_specs=[HBM] * m,
        out_shape=[pltpu.HBM(l.shape, l.dtype) for l in lands],
        input_output_aliases={i: i for i in range(m)},
        compiler_params=pltpu.CompilerParams(has_side_effects=EFFECT),
    )(*lands, send, recv, after)


def sibling_fill(lands):
    m = len(lands)

    def body(*refs):
        lds = refs[:m]
        send, recv = refs[2 * m:]
        x, y, c = _axes()
        sib = (x, y, 1 - c)
        cps = []
        for a in range(m):
            rh = lds[a].shape[1] // 2
            for k, (px, py) in enumerate(_other_chips(x, y)):
                got = lds[a].at[2 * px + py, pl.ds(c * rh, rh), :]
                cp = _remote(got, got, send.at[3 * a + k], recv.at[3 * a + k], sib)
                cp.start()
                cps.append(cp)
        for a in range(m):
            rh = lds[a].shape[1] // 2
            for k, (px, py) in enumerate(_other_chips(x, y)):
                blk = lds[a].at[2 * px + py, pl.ds((1 - c) * rh, rh), :]
                _remote(blk, blk, send.at[3 * a + k], recv.at[3 * a + k], sib).wait_recv()
        for cp in cps:
            cp.wait_send()

    return pl.pallas_call(
        body, name="sibling_fill", in_specs=[ANY] * m, out_specs=[ANY] * m,
        out_shape=[jax.ShapeDtypeStruct(l.shape, l.dtype) for l in lands],
        input_output_aliases={i: i for i in range(m)},
        scratch_shapes=[pltpu.SemaphoreType.DMA((3 * m,))] * 2,
    )(*lands)


def _presum_copies(grads, lands, send, recv):
    x, y, c = _axes()
    cps = []
    for a, (g, ld) in enumerate(zip(grads, lands)):
        rh = g.shape[1] // 2
        cps.append(_remote(g.at[:, pl.ds((1 - c) * rh, rh), :], ld, send.at[a], recv.at[a], (x, y, 1 - c)))
    return cps


def presum_start(grads):
    m = len(grads)

    def body(*refs):
        for cp in _presum_copies(refs[:m], refs[m:2 * m], refs[2 * m], refs[2 * m + 1]):
            cp.start()
        refs[-1][...] = jnp.zeros_like(refs[-1])

    lands = [lax.empty((g.shape[0], g.shape[1] // 2, g.shape[2]), g.dtype) for g in grads]
    res = pl.pallas_call(
        body, name="presum_start", in_specs=[HBM] * (2 * m), out_specs=[SEM, SEM] + [HBM] * (2 * m) + [VMEM_WHOLE],
        out_shape=[pltpu.SemaphoreType.DMA((m,))] * 2 + [pltpu.HBM(g.shape, g.dtype) for g in grads]
        + [pltpu.HBM(l.shape, l.dtype) for l in lands] + [TOKEN],
        input_output_aliases={i: 2 + i for i in range(2 * m)},
        compiler_params=pltpu.CompilerParams(has_side_effects=EFFECT),
    )(*[_in_hbm(g) for g in grads], *[_in_hbm(l) for l in lands])
    return res[0], res[1], res[2:2 + m], res[2 + m:2 + 2 * m], res[-1]


def presum_wait(grads, lands, send, recv, after):
    m = len(grads)

    def body(*refs):
        for cp in _presum_copies(refs[:m], refs[m:2 * m], refs[2 * m], refs[2 * m + 1]):
            cp.wait_send()
            cp.wait_recv()

    res = pl.pallas_call(
        body, name="presum_wait", in_specs=[HBM] * (2 * m) + [SEM, SEM] + [ANY] * len(after),
        out_specs=[HBM] * (2 * m),
        out_shape=[pltpu.HBM(g.shape, g.dtype) for g in grads] + [pltpu.HBM(l.shape, l.dtype) for l in lands],
        input_output_aliases={i: i for i in range(2 * m)},
        compiler_params=pltpu.CompilerParams(has_side_effects=EFFECT),
    )(*grads, *lands, send, recv, *after)
    return res[:m], res[m:]


def add_halves(g4, land, ids):
    nq, r, c = g4.shape
    rh = r // 2
    tr = min(rh, STREAM_ROWS)
    nt = rh // tr

    def body(ids_ref, a_ref, b_ref, o_ref):
        o_ref[...] = (a_ref[...].astype(F32) + b_ref[...].astype(F32)).astype(o_ref.dtype)

    return pl.pallas_call(
        body, name="add_halves",
        grid_spec=pltpu.PrefetchScalarGridSpec(
            num_scalar_prefetch=1, grid=(nq, nt),
            in_specs=[pl.BlockSpec((None, tr, c), lambda i, j, ids_ref: (i, ids_ref[1] * nt + j, 0)),
                      pl.BlockSpec((None, tr, c), lambda i, j, ids_ref: (i, j, 0))],
            out_specs=pl.BlockSpec((None, tr, c), lambda i, j, ids_ref: (i, j, 0))),
        out_shape=jax.ShapeDtypeStruct((nq, rh, c), g4.dtype),
        compiler_params=_params(("parallel", "parallel")),
    )(ids, g4, land)


def _scatter_copies(parts, lands, send, recv):
    x, y, c = _axes()
    cps = []
    for a, (pt, ld) in enumerate(zip(parts, lands)):
        for k, (px, py) in enumerate(_other_chips(x, y)):
            cps.append(_remote(pt.at[2 * px + py], ld.at[k], send.at[3 * a + k], recv.at[3 * a + k], (px, py, c)))
    return cps


def scatter_start(parts):
    m = len(parts)

    def body(*refs):
        for cp in _scatter_copies(refs[:m], refs[m:2 * m], refs[2 * m], refs[2 * m + 1]):
            cp.start()
        refs[-1][...] = jnp.zeros_like(refs[-1])

    lands = [lax.empty((N_CHIP - 1,) + p.shape[1:], p.dtype) for p in parts]
    res = pl.pallas_call(
        body, name="scatter_start", in_specs=[HBM] * (2 * m), out_specs=[SEM, SEM] + [HBM] * (2 * m) + [VMEM_WHOLE],
        out_shape=[pltpu.SemaphoreType.DMA((3 * m,))] * 2 + [pltpu.HBM(p.shape, p.dtype) for p in parts]
        + [pltpu.HBM(l.shape, l.dtype) for l in lands] + [TOKEN],
        input_output_aliases={i: 2 + i for i in range(2 * m)},
        compiler_params=pltpu.CompilerParams(has_side_effects=EFFECT),
    )(*[_in_hbm(p) for p in parts], *[_in_hbm(l) for l in lands])
    return res[0], res[1], res[2:2 + m], res[2 + m:2 + 2 * m], res[-1]


def scatter_wait(parts, lands, send, recv, after):
    m = len(parts)

    def body(*refs):
        for cp in _scatter_copies(refs[:m], refs[m:2 * m], refs[2 * m], refs[2 * m + 1]):
            cp.wait_send()
            cp.wait_recv()

    res = pl.pallas_call(
        body, name="scatter_wait", in_specs=[HBM] * (2 * m) + [SEM, SEM] + [ANY] * len(after),
        out_specs=[HBM] * (2 * m),
        out_shape=[pltpu.HBM(p.shape, p.dtype) for p in parts] + [pltpu.HBM(l.shape, l.dtype) for l in lands],
        input_output_aliases={i: i for i in range(2 * m)},
        compiler_params=pltpu.CompilerParams(has_side_effects=EFFECT),
    )(*parts, *lands, send, recv, *after)
    return res[:m], res[m:]


def sum_partials(part, land, ids, layer, depth, into):
    _, rh, c = part.shape
    tr = min(rh, STREAM_ROWS)
    nt = rh // tr

    def body(ids_ref, p_ref, l_ref, *rest):
        o_ref = rest[-1]
        acc = p_ref[...].astype(F32)
        for k in range(N_CHIP - 1):
            acc = acc + l_ref[k].astype(F32)
        o_ref[...] = acc

    in_specs = [pl.BlockSpec((None, tr, c), lambda i, ids_ref: (ids_ref[0], i, 0)),
                pl.BlockSpec((N_CHIP - 1, tr, c), lambda i, ids_ref: (0, i, 0))]
    args = [ids, part, land]
    aliases = {}
    if into is not None:
        in_specs.append(ANY)
        args.append(into)
        aliases = {3: 0}
    return pl.pallas_call(
        body, name="sum_partials",
        grid_spec=pltpu.PrefetchScalarGridSpec(
            num_scalar_prefetch=1, grid=(nt,), in_specs=in_specs,
            out_specs=pl.BlockSpec((None, tr, c), lambda i, ids_ref: (layer, ids_ref[1] * nt + i, 0))),
        out_shape=jax.ShapeDtypeStruct((depth, 2 * rh, c), F32), input_output_aliases=aliases,
        compiler_params=_params(("parallel",)),
    )(*args)


def exchange_final_halves(bufs, layers):
    n = len(bufs)

    def body(*refs):
        outs = refs[n:2 * n]
        send, recv = refs[2 * n:]
        x, y, c = _axes()
        sib = (x, y, 1 - c)
        cps, at = [], 0
        for a in range(n):
            rh = outs[a].shape[1] // 2
            for l in layers[a]:
                mine = outs[a].at[l, pl.ds(c * rh, rh), :]
                cp = _remote(mine, mine, send.at[at], recv.at[at], sib)
                cp.start()
                cps.append(cp)
                at += 1
        at = 0
        for a in range(n):
            rh = outs[a].shape[1] // 2
            for l in layers[a]:
                theirs = outs[a].at[l, pl.ds((1 - c) * rh, rh), :]
                _remote(theirs, theirs, send.at[at], recv.at[at], sib).wait_recv()
                at += 1
        for cp in cps:
            cp.wait_send()

    ncp = sum(len(ls) for ls in layers)
    return pl.pallas_call(
        body, name="exchange_final_halves", in_specs=[ANY] * n, out_specs=[ANY] * n,
        out_shape=[jax.ShapeDtypeStruct(g.shape, g.dtype) for g in bufs],
        input_output_aliases={i: i for i in range(n)},
        scratch_shapes=[pltpu.SemaphoreType.DMA((ncp,))] * 2,
    )(*bufs)


def allgather_small(pk):
    def body(in_ref, out_ref, send, recv):
        x, y, c = _axes()
        me = 2 * x + y
        chips = _other_chips(x, y)
        out_ref[pl.ds(me, 1)] = in_ref[...][None]
        cps = []
        for k, (px, py) in enumerate(chips):
            cp = _remote(in_ref, out_ref.at[me], send.at[k], recv.at[k], (px, py, c))
            cp.start()
            cps.append(cp)
        for k, (px, py) in enumerate(chips):
            got = out_ref.at[2 * px + py]
            _remote(got, got, send.at[k], recv.at[k], (px, py, c)).wait_recv()
        for cp in cps:
            cp.wait_send()

    return pl.pallas_call(
        body, name="allgather_small", in_specs=[VMEM_WHOLE], out_specs=VMEM_WHOLE,
        out_shape=jax.ShapeDtypeStruct((N_CHIP,) + pk.shape, pk.dtype),
        scratch_shapes=[pltpu.SemaphoreType.DMA((3,))] * 2,
    )(pk)


def allreduce_small(g):
    ndev = 8

    def body(in_ref, out_ref, slots, send, recv):
        x, y, c = _axes()
        me = 4 * x + 2 * y + c
        slots[pl.ds(me, 1)] = in_ref[...][None]
        peers = []
        for mask in range(1, ndev):
            px = 1 - x if mask & 4 else x
            py = 1 - y if mask & 2 else y
            pc = 1 - c if mask & 1 else c
            peers.append((px, py, pc))
        cps = []
        for k, peer in enumerate(peers):
            cp = _remote(in_ref, slots.at[me], send.at[k], recv.at[k], peer)
            cp.start()
            cps.append(cp)
        for k, (px, py, pc) in enumerate(peers):
            got = slots.at[4 * px + 2 * py + pc]
            _remote(got, got, send.at[k], recv.at[k], (px, py, pc)).wait_recv()
        for cp in cps:
            cp.wait_send()
        acc = slots[0]
        for d in range(1, ndev):
            acc = acc + slots[d]
        out_ref[...] = acc

    return pl.pallas_call(
        body, name="allreduce_small", in_specs=[VMEM_WHOLE], out_specs=VMEM_WHOLE,
        out_shape=jax.ShapeDtypeStruct(g.shape, g.dtype),
        scratch_shapes=[pltpu.VMEM((ndev,) + g.shape, g.dtype), pltpu.SemaphoreType.DMA((ndev - 1,)),
                        pltpu.SemaphoreType.DMA((ndev - 1,))],
    )(g)


def adamw(w, g, m, v):
    shape = w.shape
    cols = shape[-1]
    rows = int(np.prod(shape[:-1]))
    tr = rows
    for cand in (256, 128):
        if rows % cand == 0 and cand * cols * 4 <= 2 * 1024 * 1024:
            tr = cand
            break
    c1 = 1.0 - ADAM_B1 ** ADAM_STEP
    c2 = 1.0 - ADAM_B2 ** ADAM_STEP

    def body(w_ref, g_ref, m_ref, v_ref, d_ref, nm_ref, nv_ref, g_out):
        gv = g_ref[...]
        g_out[...] = gv
        nm = ADAM_B1 * m_ref[...] + (1.0 - ADAM_B1) * gv
        nv = ADAM_B2 * v_ref[...] + (1.0 - ADAM_B2) * jnp.square(gv)
        d_ref[...] = -ADAM_LR * ((nm / c1) / (jnp.sqrt(nv / c2) + ADAM_EPS) + ADAM_WD * w_ref[...])
        nm_ref[...] = nm
        nv_ref[...] = nv

    spec = pl.BlockSpec((tr, cols), lambda i: (i, 0))
    res = pl.pallas_call(
        body, name="adamw", grid=(rows // tr,), in_specs=[spec] * 4, out_specs=[spec] * 4,
        out_shape=[jax.ShapeDtypeStruct((rows, cols), F32)] * 4, compiler_params=_params(("parallel",)),
    )(*[a.reshape(rows, cols) for a in (w, g, m, v)])
    return [r.reshape(shape) for r in res]


WEIGHTS = ("norm_g", "ffn1_w_gu", "ffn1_w_down", "w_in", "w_ret_o", "sc_conv_w", "w_sc_o", "cf_dw_w", "cf_dw_b",
           "cf_ln_g", "cf_ln_b", "w_cf_o", "w_o", "ffn2_w_gu", "ffn2_w_down")
SHARDED_SMALL = ("norm_g", "sc_conv_w", "cf_dw_w")
REPLICATED_SMALL = ("cf_dw_b", "cf_ln_g", "cf_ln_b")

def _pack_rows(parts):
    padded, offs, at = [], [], 0
    for p in parts:
        r = -(-p.shape[0] // SUBLANES) * SUBLANES
        padded.append(jnp.pad(p, ((0, r - p.shape[0]), (0, 0))))
        offs.append(at)
        at += r
    return jnp.concatenate(padded, axis=0), offs


def kernel(x, positions, norm_g, ffn1_w_gu, ffn1_w_down, w_in, w_ret_o, sc_conv_w, w_sc_o, cf_dw_w, cf_dw_b, cf_ln_g, cf_ln_b, w_cf_o, w_o, ffn2_w_gu, ffn2_w_down, loss_target, m_norm_g, m_ffn1_w_gu, m_ffn1_w_down, m_w_in, m_w_ret_o, m_sc_conv_w, m_w_sc_o, m_cf_dw_w, m_cf_dw_b, m_cf_ln_g, m_cf_ln_b, m_w_cf_o, m_w_o, m_ffn2_w_gu, m_ffn2_w_down, v_norm_g, v_ffn1_w_gu, v_ffn1_w_down, v_w_in, v_w_ret_o, v_sc_conv_w, v_w_sc_o, v_cf_dw_w, v_cf_dw_b, v_cf_ln_g, v_cf_ln_b, v_w_cf_o, v_w_o, v_ffn2_w_gu, v_ffn2_w_down):
    wts = dict(zip(WEIGHTS, (norm_g, ffn1_w_gu, ffn1_w_down, w_in, w_ret_o, sc_conv_w, w_sc_o, cf_dw_w, cf_dw_b,
                             cf_ln_g, cf_ln_b, w_cf_o, w_o, ffn2_w_gu, ffn2_w_down)))
    mom = dict(zip(WEIGHTS, (m_norm_g, m_ffn1_w_gu, m_ffn1_w_down, m_w_in, m_w_ret_o, m_sc_conv_w, m_w_sc_o,
                             m_cf_dw_w, m_cf_dw_b, m_cf_ln_g, m_cf_ln_b, m_w_cf_o, m_w_o, m_ffn2_w_gu, m_ffn2_w_down)))
    var = dict(zip(WEIGHTS, (v_norm_g, v_ffn1_w_gu, v_ffn1_w_down, v_w_in, v_w_ret_o, v_sc_conv_w, v_w_sc_o,
                             v_cf_dw_w, v_cf_dw_b, v_cf_ln_g, v_cf_ln_b, v_w_cf_o, v_w_o, v_ffn2_w_gu, v_ffn2_w_down)))
    depth = norm_g.shape[0]
    dq = norm_g.shape[-1]
    d = N_CHIP * dq
    chip = 2 * lax.axis_index("x") + lax.axis_index("y")
    ids = jnp.stack([chip, lax.axis_index("c")]).astype(jnp.int32)

    pk, offs = _pack_rows([wts[n].reshape(-1, dq) for n in SHARDED_SMALL])
    gk4 = allgather_small(pk)
    gk = gk4.transpose(1, 0, 2).reshape(pk.shape[0], d)
    small = {n: wts[n] for n in REPLICATED_SMALL}
    for n, o in zip(SHARDED_SMALL, offs):
        rows = wts[n].shape[0] * wts[n].shape[1]
        small[n] = gk[o:o + rows].reshape(wts[n].shape[:2] + (d,))

    order = [(l, blk) for l in range(depth) for blk in BLOCKS]
    def placed(groups, after):
        return [[place_quarter(wts[n], l, ids, after) for n in BLOCK_WEIGHTS[blk]] for l, blk in groups]

    first, token = gather_start("gather_start_first", placed(order[:1], gk4), gk4)
    rest, token = gather_start("gather_start_rest", placed(order[1:], token), token)
    started = dict(zip(order, first + rest))
    small["norm_g"] = small["norm_g"] + token[0:1, 0:1]

    def fetch(l, blk, after):
        send, recv, lands = started[l, blk]
        lands = gather_wait(lands, send, recv, token if (l, blk) == order[0] else after)
        return dict(zip(BLOCK_WEIGHTS[blk], sibling_fill(lands)))

    gsum = {n: None for n in BIG}
    presums, scatters = [], []

    def scatter_next(after):
        group, gl, lands, send, recv = presums.pop(0)
        gl, lands = presum_wait(gl, lands, send, recv, after)
        send, recv, parts, lands, tok = scatter_start([add_halves(g, ld, ids) for g, ld in zip(gl, lands)])
        scatters.append((group, parts, lands, send, recv))
        return tok

    def sum_next(after):
        (l, blk), parts, lands, send, recv = scatters.pop(0)
        parts, lands = scatter_wait(parts, lands, send, recv, after)
        for n, pt, ld in zip(BLOCK_WEIGHTS[blk], parts, lands):
            gsum[n] = sum_partials(pt, ld, ids, l, depth, gsum[n])

    def push(l, blk, grads):
        send, recv, gl, lands, tok = presum_start([grads[n] for n in BLOCK_WEIGHTS[blk]])
        if scatters:
            sum_next((gl[0],))
        if presums:
            tok = tok + scatter_next((gl[0],))
        presums.append(((l, blk), gl, lands, send, recv))
        return tok[0:1, 0:1]

    loss, grad_x, gs = local_step(x, positions, loss_target, small, fetch, push)

    names = SHARDED_SMALL + REPLICATED_SMALL
    pg, offs = _pack_rows([gs[n].reshape(-1, d) for n in names])
    tot = allreduce_small(pg)
    sum_next((scatter_next((grad_x, tot)),))
    grads = {}
    for n, o in zip(names, offs):
        rows = int(np.prod(gs[n].shape[:-1]))
        full = tot[o:o + rows]
        if n in SHARDED_SMALL:
            full = lax.dynamic_slice_in_dim(full, chip * dq, dq, axis=1)
        grads[n] = full.reshape(wts[n].shape)

    last = BLOCK_WEIGHTS[order[0][1]]
    early = [n for n in BIG if n not in last]
    every = tuple(range(depth))
    done = exchange_final_halves([gsum[n] for n in early + list(last)],
                                 [every] * len(early) + [every[1:]] * len(last))
    for n, g in zip(early + list(last), done):
        gsum[n] = g
    delta, new_m, new_v = {}, {}, {}
    for n in WEIGHTS:
        if n not in last:
            g = gsum[n] if n in BIG else grads[n]
            delta[n], new_m[n], new_v[n], grads[n] = adamw(wts[n], g, mom[n], var[n])
    sum_next(tuple(delta[n] for n in WEIGHTS if n not in last))
    done = exchange_final_halves([gsum[n] for n in last], [every[:1]] * len(last))
    for n, g in zip(last, done):
        delta[n], new_m[n], new_v[n], grads[n] = adamw(wts[n], g, mom[n], var[n])

    loss_all = lax.psum(loss[0, 0], ("x", "y", "c"))
    return (loss_all, grad_x, *[grads[n] for n in WEIGHTS], *[delta[n] for n in WEIGHTS],
            *[new_m[n] for n in WEIGHTS], *[new_v[n] for n in WEIGHTS])
```

```python
import functools

import jax
import jax.numpy as jnp
import numpy as np
from jax import lax
from jax.experimental import pallas as pl
from jax.experimental.pallas import tpu as pltpu

F32 = jnp.float32
BF16 = jnp.bfloat16
MXU_DTYPE = BF16
VMEM_LIMIT_BYTES = 56 * 1024 * 1024
MESH = pl.DeviceIdType.MESH

N_CHIP = 4
CHUNK = 64
RET_HEADS = 4
RET_QK_DIM = 128
RET_V_DIM = 256
SC_KERNEL = 3
CF_KERNEL = 31
ROPE_BASE = 10000.0
NORM_EPS = 1e-6
LN_EPS = 1e-5
ADAM_LR = 0.001
ADAM_B1 = 0.9
ADAM_B2 = 0.999
ADAM_EPS = 1e-08
ADAM_WD = 0.01
ADAM_STEP = 10

SUBLANES = 8
CONV_PAD = 32
CONV_TS = 128
CONV_TC = 512
CONV_ROWS = 16
CONV_SCRATCH = [pltpu.VMEM((CONV_TS + CONV_PAD, CONV_TC), F32),
                pltpu.VMEM((SUBLANES - 1, CONV_TS + CONV_PAD - SUBLANES, CONV_TC), F32)]
RET_TQ = 512
MM_TM = 1024
MM_TN = 1536
MM_K1 = 1024
MM_W1 = 8 << 20
MM_SLICE = 256
MM_IN_BYTES = 36 << 20
STREAM_STEPS = 2


def _params(sem):
    return pltpu.CompilerParams(dimension_semantics=sem, vmem_limit_bytes=VMEM_LIMIT_BYTES)


def _axes():
    return lax.axis_index("x"), lax.axis_index("y"), lax.axis_index("c")


NN = (((1,), (0,)), ((), ()))
NT = (((1,), (1,)), ((), ()))
TN = (((0,), (0,)), ((), ()))


def _mm(name, a, b, out_shape, out_dtype, grid, a_spec, b_spec, o_spec, dims, acc_shape):
    nk = grid[2]

    def body(a_ref, b_ref, o_ref, *scratch):
        bv = b_ref[...]
        if bv.ndim == 3:
            bv = bv.reshape(-1, bv.shape[-1])
        part = lax.dot_general(a_ref[...], bv, dims, preferred_element_type=F32)

        def put(v):
            o_ref[...] = v.reshape(o_ref.shape).astype(o_ref.dtype)

        if nk == 1:
            put(part)
        else:
            acc = scratch[0]
            k = pl.program_id(2)

            @pl.when(k == 0)
            def _():
                acc[...] = part

            @pl.when(k > 0)
            def _():
                acc[...] += part

            @pl.when(k == nk - 1)
            def _():
                put(acc[...])

    scratch = [pltpu.VMEM(acc_shape, F32)] if nk > 1 else []
    return pl.pallas_call(
        body, name=name, grid=grid, in_specs=[a_spec, b_spec], out_specs=o_spec,
        out_shape=jax.ShapeDtypeStruct(out_shape, out_dtype), scratch_shapes=scratch,
        compiler_params=_params(("parallel", "parallel", "arbitrary")),
    )(a, b)


def _tile(n, target):
    best = None
    for t in range(128, min(n, target) + 1, 128):
        if n % t == 0:
            best = t
    assert best is not None, (n, target)
    return best


def _token_rows(t, width):
    tt = t
    while tt > MM_TM and tt * width * jnp.dtype(MXU_DTYPE).itemsize * 2 > MM_IN_BYTES:
        tt //= 2
    return tt


def mm_fwd(name, a, w4, mode, out_dtype):
    t = a.shape[0]
    _, r, c = w4.shape
    tm = min(t, MM_TM)
    if mode == "col":
        tn = _tile(c, MM_TN)
        npj = c // tn
        grid = (t // tm, N_CHIP * npj, 1)
        a_spec = pl.BlockSpec((tm, r), lambda i, j, k: (i, 0))
        b_spec = pl.BlockSpec((None, r, tn), lambda i, j, k: (j // npj, 0, j % npj))
        o_spec = pl.BlockSpec((tm, tn), lambda i, j, k: (i, j))
        return _mm(name, a, w4, (t, N_CHIP * c), out_dtype, grid, a_spec, b_spec, o_spec, NN, (tm, tn))
    if w4.size * w4.dtype.itemsize <= MM_W1:
        grid = (t // tm, 1, 1)
        a_spec = pl.BlockSpec((tm, N_CHIP * r), lambda i, j, k: (i, 0))
        b_spec = pl.BlockSpec((N_CHIP, r, c), lambda i, j, k: (0, 0, 0))
        o_spec = pl.BlockSpec((tm, c), lambda i, j, k: (i, 0))
        return _mm(name, a, w4, (t, c), out_dtype, grid, a_spec, b_spec, o_spec, NN, (tm, c))
    grid = (t // tm, 1, N_CHIP)
    a_spec = pl.BlockSpec((tm, r), lambda i, j, k: (i, k))
    b_spec = pl.BlockSpec((None, r, c), lambda i, j, k: (k, 0, 0))
    o_spec = pl.BlockSpec((tm, c), lambda i, j, k: (i, 0))
    return _mm(name, a, w4, (t, c), out_dtype, grid, a_spec, b_spec, o_spec, NN, (tm, c))


def mm_dx(name, dy, w4, mode, out_dtype):
    t = dy.shape[-2]
    _, r, c = w4.shape
    tm = min(t, MM_TM)
    if mode == "col":
        tn, npj = c, 1
        hb = N_CHIP // 2 * npj
        grid = (t // tm, 1, N_CHIP * npj)
        if dy.ndim == 3:
            a_spec = pl.BlockSpec((None, tm, tn), lambda i, j, k: (k // hb, i, k % hb))
        else:
            a_spec = pl.BlockSpec((tm, tn), lambda i, j, k: (i, k))
        b_spec = pl.BlockSpec((None, r, tn), lambda i, j, k: (k // npj, 0, k % npj))
        o_spec = pl.BlockSpec((tm, r), lambda i, j, k: (i, 0))
        return _mm(name, dy, w4, (t, r), out_dtype, grid, a_spec, b_spec, o_spec, NT, (tm, r))
    if N_CHIP * r <= MM_K1:
        grid = (t // tm, 1, 1)
        a_spec = pl.BlockSpec((tm, c), lambda i, j, k: (i, 0))
        b_spec = pl.BlockSpec((N_CHIP, r, c), lambda i, j, k: (0, 0, 0))
        o_spec = pl.BlockSpec((tm, N_CHIP * r), lambda i, j, k: (i, 0))
        return _mm(name, dy, w4, (t, N_CHIP * r), out_dtype, grid, a_spec, b_spec, o_spec, NT, (tm, N_CHIP * r))
    grid = (t // tm, N_CHIP, 1)
    a_spec = pl.BlockSpec((tm, c), lambda i, j, k: (i, 0))
    b_spec = pl.BlockSpec((None, r, c), lambda i, j, k: (j, 0, 0))
    o_spec = pl.BlockSpec((tm, r), lambda i, j, k: (i, j))
    return _mm(name, dy, w4, (t, N_CHIP * r), out_dtype, grid, a_spec, b_spec, o_spec, NT, (tm, r))


def mm_dw(name, a, dy, mode, shape3):
    t = a.shape[0]
    _, r, c = shape3
    if mode == "col":
        tn = _tile(c, MM_TN)
        npj = c // tn
        tt = _token_rows(t, r + tn)
        grid = (1, N_CHIP * npj, t // tt)
        a_spec = pl.BlockSpec((tt, r), lambda i, j, k: (k, 0))
        hb = N_CHIP // 2 * npj
        if dy.ndim == 3:
            b_spec = pl.BlockSpec((None, tt, tn), lambda i, j, k: (j // hb, k, j % hb))
        else:
            b_spec = pl.BlockSpec((tt, tn), lambda i, j, k: (k, j))
        o_spec = pl.BlockSpec((None, r, tn), lambda i, j, k: (j // npj, 0, j % npj))
        return _mm(name, a, dy, shape3, MXU_DTYPE, grid, a_spec, b_spec, o_spec, TN, (r, tn))
    if N_CHIP * r <= MM_K1:
        tt = _token_rows(t, N_CHIP * r + c)
        grid = (1, 1, t // tt)
        a_spec = pl.BlockSpec((tt, N_CHIP * r), lambda i, j, k: (k, 0))
        b_spec = pl.BlockSpec((tt, c), lambda i, j, k: (k, 0))
        o_spec = pl.BlockSpec((N_CHIP, r, c), lambda i, j, k: (0, 0, 0))
        return _mm(name, a, dy, shape3, MXU_DTYPE, grid, a_spec, b_spec, o_spec, TN, (N_CHIP * r, c))
    tt = _token_rows(t, r + c)
    grid = (N_CHIP, 1, t // tt)
    a_spec = pl.BlockSpec((tt, r), lambda i, j, k: (k, i))
    b_spec = pl.BlockSpec((tt, c), lambda i, j, k: (k, 0))
    o_spec = pl.BlockSpec((None, r, c), lambda i, j, k: (i, 0, 0))
    return _mm(name, a, dy, shape3, MXU_DTYPE, grid, a_spec, b_spec, o_spec, TN, (r, c))


def _rowwise(name, fn, rows, pars, outs, accs=(), tm=256, ncol=1):
    t = rows[0][0].shape[0]
    nrow, npar, nout = len(rows), len(pars), len(outs)

    def body(*refs):
        vals = [r[...] for r in refs[:nrow + npar]]
        res = fn(*vals)
        out_refs = refs[nrow + npar:nrow + npar + nout]
        acc_refs = refs[nrow + npar + nout:]
        for o, v in zip(out_refs, res[:nout]):
            o[...] = v.astype(o.dtype)
        i = pl.program_id(1)
        for a, v in zip(acc_refs, res[nout:]):
            @pl.when(i == 0)
            def _(a=a, v=v):
                a[...] = v.astype(F32)

            @pl.when(i > 0)
            def _(a=a, v=v):
                a[...] += v.astype(F32)

    in_specs = [pl.BlockSpec((tm, w), functools.partial(lambda j, i, b: (i, b + j), b=b)) for _, w, b in rows]
    for arr, w in pars:
        if w is None:
            in_specs.append(pl.BlockSpec(arr.shape, lambda j, i: (0, 0)))
        else:
            in_specs.append(pl.BlockSpec((1, w), lambda j, i: (0, j)))
    out_specs = [pl.BlockSpec((tm, w), lambda j, i: (i, j)) for _, w, _ in outs]
    out_specs += [pl.BlockSpec((1, w), lambda j, i: (0, j)) for _, w in accs]
    out_shape = [jax.ShapeDtypeStruct((t, tw), dt) for tw, _, dt in outs]
    out_shape += [jax.ShapeDtypeStruct((1, tw), F32) for tw, _ in accs]
    res = pl.pallas_call(
        body, name=name, grid=(ncol, t // tm), in_specs=in_specs, out_specs=out_specs, out_shape=out_shape,
        compiler_params=_params(("parallel", "arbitrary" if accs else "parallel")),
    )(*[r[0] for r in rows], *[p[0] for p in pars])
    return res


def _rms(x, g):
    xf = x.astype(F32)
    return xf * lax.rsqrt(jnp.mean(xf * xf, axis=-1, keepdims=True) + NORM_EPS) * g


def _silu(x):
    return x * jax.nn.sigmoid(x)


def rms_fwd(name, x, g):
    d = x.shape[1]
    return _rowwise(name, lambda x, g: (_rms(x, g),), [(x, d, 0)], [(g, None)], [(d, d, MXU_DTYPE)], tm=512)[0]


def rms_bwd(name, x, g, dh, dres):
    d = x.shape[1]

    def fn(x, dh, dres, g):
        _, vjp = jax.vjp(_rms, x, g)
        dx, dg = vjp(dh.astype(F32))
        return dres + dx, dg

    return _rowwise(name, fn, [(x, d, 0), (dh, d, 0), (dres, d, 0)], [(g, None)], [(d, d, F32)], [(d, d)], tm=256)


def post_fwd(name, x, y, g, scale):
    d = x.shape[1]
    return _rowwise(name, lambda x, y, g: (x + scale * _rms(y, g),), [(x, d, 0), (y, d, 0)], [(g, None)],
                    [(d, d, F32)], tm=512)[0]


def post_bwd(name, y, g, dx, scale):
    d = y.shape[1]

    def fn(y, dx, g):
        _, vjp = jax.vjp(lambda y, g: scale * _rms(y, g), y, g)
        return vjp(dx)

    return _rowwise(name, fn, [(y, d, 0), (dx, d, 0)], [(g, None)], [(d, d, MXU_DTYPE)], [(d, d)], tm=256)


def ffn_up(name, h, w4):
    t = h.shape[0]
    _, r, c = w4.shape
    tm = min(t, MM_TM)
    tn = _tile(c, MM_TM)
    npj = c // tn
    half = N_CHIP // 2

    def body(h_ref, wg_ref, wu_ref, gu_ref, a_ref):
        hv = h_ref[...]
        g = lax.dot_general(hv, wg_ref[...], NN, preferred_element_type=F32)
        u = lax.dot_general(hv, wu_ref[...], NN, preferred_element_type=F32)
        gu_ref[0] = g.astype(gu_ref.dtype)
        gu_ref[1] = u.astype(gu_ref.dtype)
        a_ref[...] = (_silu(g) * u).astype(a_ref.dtype)

    f = half * c
    return pl.pallas_call(
        body, name=name, grid=(t // tm, half * npj),
        in_specs=[pl.BlockSpec((tm, r), lambda i, j: (i, 0)),
                  pl.BlockSpec((None, r, tn), lambda i, j: (j // npj, 0, j % npj)),
                  pl.BlockSpec((None, r, tn), lambda i, j: (half + j // npj, 0, j % npj))],
        out_specs=[pl.BlockSpec((2, tm, tn), lambda i, j: (0, i, j)), pl.BlockSpec((tm, tn), lambda i, j: (i, j))],
        out_shape=[jax.ShapeDtypeStruct((2, t, f), MXU_DTYPE), jax.ShapeDtypeStruct((t, f), MXU_DTYPE)],
        compiler_params=_params(("parallel", "parallel")),
    )(h, w4, w4)


def ffn_down_dx(name, dy, w4, gu):
    t = dy.shape[0]
    _, r, c = w4.shape
    tm = min(t, MM_TM)

    def body(dy_ref, w_ref, gu_ref, o_ref):
        dyv = dy_ref[...]
        for n0 in range(0, r, MM_SLICE):
            cols = pl.ds(n0, MM_SLICE)
            da = lax.dot_general(dyv, w_ref[cols, :], NT, preferred_element_type=F32)
            gate, up = gu_ref[0, :, cols].astype(F32), gu_ref[1, :, cols].astype(F32)
            sg = jax.nn.sigmoid(gate)
            o_ref[0, :, cols] = (da * up * (sg * (1.0 + gate * (1.0 - sg)))).astype(o_ref.dtype)
            o_ref[1, :, cols] = (da * (gate * sg)).astype(o_ref.dtype)

    return pl.pallas_call(
        body, name=name, grid=(t // tm, N_CHIP),
        in_specs=[pl.BlockSpec((tm, c), lambda i, j: (i, 0)), pl.BlockSpec((None, r, c), lambda i, j: (j, 0, 0)),
                  pl.BlockSpec((2, tm, r), lambda i, j: (0, i, j))],
        out_specs=pl.BlockSpec((2, tm, r), lambda i, j: (0, i, j)),
        out_shape=jax.ShapeDtypeStruct((2, t, N_CHIP * r), MXU_DTYPE),
        compiler_params=_params(("parallel", "parallel")),
    )(dy, w4, gu)


def _head_gate(o, g):
    mu = jnp.mean(o, axis=-1, keepdims=True)
    var = jnp.mean(jnp.square(o - mu), axis=-1, keepdims=True)
    return _silu(g.astype(F32)) * ((o - mu) * lax.rsqrt(var + LN_EPS))


def head_gate_fwd(name, o, p, gate_blk):
    dv = RET_V_DIM
    return _rowwise(name, lambda o, g: (_head_gate(o, g),), [(o, dv, 0), (p, dv, gate_blk)], [],
                    [(RET_HEADS * dv, dv, MXU_DTYPE)], tm=512, ncol=RET_HEADS)[0]


def head_gate_bwd(name, o, p, gate_blk, da):
    dv = RET_V_DIM

    def fn(o, g, da):
        _, vjp = jax.vjp(_head_gate, o, g.astype(F32))
        return vjp(da.astype(F32))

    w = RET_HEADS * dv
    return _rowwise(name, fn, [(o, dv, 0), (p, dv, gate_blk), (da, dv, 0)], [],
                    [(w, dv, MXU_DTYPE), (w, dv, MXU_DTYPE)], tm=512, ncol=RET_HEADS)


def _ln_silu(u, g, b):
    mu = jnp.mean(u, axis=-1, keepdims=True)
    var = jnp.mean(jnp.square(u - mu), axis=-1, keepdims=True)
    return _silu((u - mu) * lax.rsqrt(var + LN_EPS) * g + b)


def ln_silu_fwd(name, u, g, b):
    d = u.shape[1]
    return _rowwise(name, lambda u, g, b: (_ln_silu(u, g, b),), [(u, d, 0)], [(g, None), (b, None)],
                    [(d, d, MXU_DTYPE)], tm=512)[0]


def ln_silu_bwd(name, u, g, b, dc):
    d = u.shape[1]

    def fn(u, dc, g, b):
        _, vjp = jax.vjp(_ln_silu, u, g, b)
        return vjp(dc.astype(F32))

    return _rowwise(name, fn, [(u, d, 0), (dc, d, 0)], [(g, None), (b, None)], [(d, d, F32)], [(d, d), (d, d)],
                    tm=256)


def _merge(g0, g1, g2, ya, yb, yc):
    s = jax.nn.sigmoid
    return s(g0.astype(F32)) * ya + s(g1.astype(F32)) * yb + s(g2.astype(F32)) * yc


def merge_fwd(name, p, blk, ya, yb, yc):
    d = ya.shape[1]
    rows = [(p, d, blk), (p, d, blk + 1), (p, d, blk + 2), (ya, d, 0), (yb, d, 0), (yc, d, 0)]
    return _rowwise(name, lambda *v: (_merge(*v),), rows, [], [(d, d, MXU_DTYPE)], tm=256)[0]


def merge_bwd(name, p, blk, ya, yb, yc, dmg):
    d = ya.shape[1]

    def fn(g0, g1, g2, ya, yb, yc, dmg):
        _, vjp = jax.vjp(_merge, g0.astype(F32), g1.astype(F32), g2.astype(F32), ya, yb, yc)
        return vjp(dmg.astype(F32))

    rows = [(p, d, blk), (p, d, blk + 1), (p, d, blk + 2), (ya, d, 0), (yb, d, 0), (yc, d, 0), (dmg, d, 0)]
    return _rowwise(name, fn, rows, [], [(d, d, MXU_DTYPE)] * 6, tm=256)


def concat_cols(name, pieces):
    t = pieces[0].shape[0]
    widths = [p.shape[1] for p in pieces]
    tm = 256

    def body(*refs):
        o_ref, at = refs[-1], 0
        for r, w in zip(refs[:-1], widths):
            o_ref[:, at:at + w] = r[...]
            at += w

    return pl.pallas_call(
        body, name=name, grid=(t // tm,),
        in_specs=[pl.BlockSpec((tm, w), lambda i: (i, 0)) for w in widths],
        out_specs=pl.BlockSpec((tm, sum(widths)), lambda i: (i, 0)),
        out_shape=jax.ShapeDtypeStruct((t, sum(widths)), pieces[0].dtype),
        compiler_params=_params(("parallel",)),
    )(*pieces)


def loss_head(name, y, target):
    t, d = y.shape
    tm = 512

    def body(y_ref, t_ref, dy_ref, loss_ref):
        err = y_ref[...] - t_ref[...]
        dy_ref[...] = err * (1.0 / d)
        part = jnp.sum(jnp.sum(err * err, axis=1, keepdims=True), axis=0, keepdims=True) * (0.5 / d)

        @pl.when(pl.program_id(0) == 0)
        def _():
            loss_ref[...] = part

        @pl.when(pl.program_id(0) > 0)
        def _():
            loss_ref[...] += part

    return pl.pallas_call(
        body, name=name, grid=(t // tm,),
        in_specs=[pl.BlockSpec((tm, d), lambda i: (i, 0))] * 2,
        out_specs=[pl.BlockSpec((tm, d), lambda i: (i, 0)), pl.BlockSpec((1, 1), lambda i: (0, 0))],
        out_shape=[jax.ShapeDtypeStruct((t, d), F32), jax.ShapeDtypeStruct((1, 1), F32)],
        compiler_params=_params(("arbitrary",)),
    )(y, target)


def _rot(x, cos2, sin2):
    return x * cos2 + pltpu.roll(x, RET_QK_DIM // 2, 1) * sin2


def _decay_mask(lg, n0, rows, cols):
    n = n0 + lax.broadcasted_iota(jnp.int32, (rows, cols), 0)
    m = lax.broadcasted_iota(jnp.int32, (rows, cols), 1)
    shift = CHUNK.bit_length() - 1
    dist = jnp.abs(n - m).astype(F32)
    return jnp.where((m >> shift) <= (n >> shift), jnp.exp(lg * dist), 0.0)


def _ret_specs(s):
    dk, dv, h = RET_QK_DIM, RET_V_DIM, RET_HEADS
    return [
        pl.BlockSpec((s, dk), lambda b, hh: (b, hh)),
        pl.BlockSpec((s, dk), lambda b, hh: (b, h + hh)),
        pl.BlockSpec((s, dv), lambda b, hh: (b, (2 * h * dk) // dv + hh)),
        pl.BlockSpec((s, dk), lambda b, hh: (b, 0)),
        pl.BlockSpec((s, dk), lambda b, hh: (b, 0)),
        pl.BlockSpec((None, 1, dk), lambda b, hh: (hh, 0, 0)),
    ]


def retention_fwd(name, p, cos2, sin2, log_g, nb, s):
    dk, dv, h = RET_QK_DIM, RET_V_DIM, RET_HEADS

    def body(q_ref, k_ref, v_ref, cos_ref, sin_ref, lg_ref, o_ref, kr_ref):
        lg = lg_ref[0:1, 0:1]
        kr = _rot(k_ref[...].astype(F32), cos_ref[...], sin_ref[...]) * (dk ** -0.5)
        kr_ref[...] = kr.astype(kr_ref.dtype)
        for qi in range(s // RET_TQ):
            n0, kmax = qi * RET_TQ, (qi + 1) * RET_TQ
            rows = pl.ds(n0, RET_TQ)
            qr = _rot(q_ref[rows, :].astype(F32), cos_ref[rows, :], sin_ref[rows, :]).astype(MXU_DTYPE)
            sc = lax.dot_general(qr, kr_ref[0:kmax, :], NT, preferred_element_type=F32)
            pm = (sc * _decay_mask(lg, n0, RET_TQ, kmax)).astype(MXU_DTYPE)
            o_ref[rows, :] = lax.dot_general(pm, v_ref[0:kmax, :], NN, preferred_element_type=F32)

    return pl.pallas_call(
        body, name=name, grid=(nb, h), in_specs=_ret_specs(s),
        out_specs=pl.BlockSpec((s, dv), lambda b, hh: (b, hh)),
        out_shape=jax.ShapeDtypeStruct((nb * s, h * dv), F32),
        scratch_shapes=[pltpu.VMEM((s, dk), MXU_DTYPE)],
        compiler_params=_params(("parallel", "parallel")),
    )(p, p, p, cos2, sin2, log_g)


def retention_bwd(name, p, cos2, sin2, log_g, do, nb, s):
    dk, dv, h = RET_QK_DIM, RET_V_DIM, RET_HEADS

    def body(q_ref, k_ref, v_ref, cos_ref, sin_ref, lg_ref, do_ref, dq_ref, dk_ref, dv_ref, kr_ref, dk_acc, dv_acc):
        lg = lg_ref[0:1, 0:1]
        kr = _rot(k_ref[...].astype(F32), cos_ref[...], sin_ref[...]) * (dk ** -0.5)
        kr_ref[...] = kr.astype(kr_ref.dtype)
        dk_acc[...] = jnp.zeros_like(dk_acc)
        dv_acc[...] = jnp.zeros_like(dv_acc)
        for qi in range(s // RET_TQ):
            n0, kmax = qi * RET_TQ, (qi + 1) * RET_TQ
            rows = pl.ds(n0, RET_TQ)
            cq, sq = cos_ref[rows, :], sin_ref[rows, :]
            qr = _rot(q_ref[rows, :].astype(F32), cq, sq).astype(MXU_DTYPE)
            dob = do_ref[rows, :]
            mask = _decay_mask(lg, n0, RET_TQ, kmax)
            sc = lax.dot_general(qr, kr_ref[0:kmax, :], NT, preferred_element_type=F32)
            pm = (sc * mask).astype(MXU_DTYPE)
            dv_acc[0:kmax, :] += lax.dot_general(pm, dob, TN, preferred_element_type=F32)
            dp = lax.dot_general(dob, v_ref[0:kmax, :], NT, preferred_element_type=F32)
            ds = (dp * mask).astype(MXU_DTYPE)
            dqr = lax.dot_general(ds, kr_ref[0:kmax, :], NN, preferred_element_type=F32)
            dq_ref[rows, :] = _rot(dqr, cq, -sq).astype(dq_ref.dtype)
            dk_acc[0:kmax, :] += lax.dot_general(ds, qr, TN, preferred_element_type=F32)
        dkr = dk_acc[...] * (dk ** -0.5)
        dk_ref[...] = _rot(dkr, cos_ref[...], -sin_ref[...]).astype(dk_ref.dtype)
        dv_ref[...] = dv_acc[...].astype(dv_ref.dtype)

    t = nb * s
    return pl.pallas_call(
        body, name=name, grid=(nb, h),
        in_specs=_ret_specs(s) + [pl.BlockSpec((s, dv), lambda b, hh: (b, hh))],
        out_specs=[pl.BlockSpec((s, dk), lambda b, hh: (b, hh)), pl.BlockSpec((s, dk), lambda b, hh: (b, hh)),
                   pl.BlockSpec((s, dv), lambda b, hh: (b, hh))],
        out_shape=[jax.ShapeDtypeStruct((t, h * dk), MXU_DTYPE), jax.ShapeDtypeStruct((t, h * dk), MXU_DTYPE),
                   jax.ShapeDtypeStruct((t, h * dv), MXU_DTYPE)],
        scratch_shapes=[pltpu.VMEM((s, dk), MXU_DTYPE), pltpu.VMEM((s, dk), F32), pltpu.VMEM((s, dv), F32)],
        compiler_params=_params(("parallel", "parallel")),
    )(p, p, p, cos2, sin2, log_g, do)


def _conv_grid(t, d, nb):
    s = t // nb
    ns, nc = s // CONV_TS, d // CONV_TC
    return s, ns, nc


def _shifted(pad_ref, sh_ref, offsets):
    n = sh_ref.shape[1]
    for b in sorted({off % SUBLANES for off in offsets} - {0}):
        sh_ref[b - 1] = pad_ref[pl.ds(b, n), :]

    def read(off, r0):
        a, b = off - off % SUBLANES + r0, off % SUBLANES
        return pad_ref[pl.ds(a, CONV_ROWS), :] if b == 0 else sh_ref[b - 1, pl.ds(a, CONV_ROWS), :]

    return read


def _causal_taps(pad_ref, sh_ref, w_ref, k, emit):
    offs = [CONV_PAD - (k - 1) + j for j in range(k)]
    read = _shifted(pad_ref, sh_ref, offs)
    for r0 in range(0, CONV_TS, CONV_ROWS):
        acc = None
        for j in range(k):
            term = w_ref[j:j + 1, :] * read(offs[j], r0)
            acc = term if acc is None else acc + term
        emit(r0, acc)


def _carry_past(pad_ref, s_idx):
    @pl.when(s_idx == 0)
    def _():
        pad_ref[0:CONV_PAD, :] = jnp.zeros((CONV_PAD, pad_ref.shape[1]), F32)

    @pl.when(s_idx > 0)
    def _():
        pad_ref[0:CONV_PAD, :] = pad_ref[CONV_TS:CONV_TS + CONV_PAD, :]


def _carry_future(pad_ref, s_idx):
    @pl.when(s_idx == 0)
    def _():
        pad_ref[CONV_TS:CONV_TS + CONV_PAD, :] = jnp.zeros((CONV_PAD, pad_ref.shape[1]), F32)

    @pl.when(s_idx > 0)
    def _():
        pad_ref[CONV_TS:CONV_TS + CONV_PAD, :] = pad_ref[0:CONV_PAD, :]


def _conv_bwd_taps(pad_ref, sh_ref, w_ref, dw_acc, k, x_rows, emit, mix):
    read = _shifted(pad_ref, sh_ref, range(k))
    for r0 in range(0, CONV_TS, CONV_ROWS):
        ops = x_rows(r0)
        x = mix(ops)
        acc = None
        for j in range(k):
            sh = read(k - 1 - j, r0)
            term = w_ref[j:j + 1, :] * sh
            acc = term if acc is None else acc + term
            prod = x * sh
            part = prod[0:SUBLANES]
            for q in range(SUBLANES, CONV_ROWS, SUBLANES):
                part = part + prod[q:q + SUBLANES]
            dw_acc[j] += part
        emit(r0, ops, acc)


def _conv_bwd_edges(dw_acc, dw_ref, nb, ns, extra=()):
    first = jnp.logical_and(pl.program_id(1) == 0, pl.program_id(2) == 0)
    last = jnp.logical_and(pl.program_id(1) == nb - 1, pl.program_id(2) == ns - 1)

    @pl.when(first)
    def _():
        dw_acc[...] = jnp.zeros_like(dw_acc)
        for r in extra:
            r[...] = jnp.zeros_like(r)

    def finish():
        @pl.when(last)
        def _():
            dw_ref[...] = jnp.sum(dw_acc[...], axis=1)

    return finish


def short_conv_fwd(name, p, blk_b, w, nb):
    t = p.shape[0]
    d = w.shape[1]
    s, ns, nc = _conv_grid(t, d, nb)
    cb = d // CONV_TC

    def body(b_ref, c_ref, x_ref, w_ref, y_ref, cz_ref, pad_ref, sh_ref):
        _carry_past(pad_ref, pl.program_id(2))
        pad_ref[CONV_PAD:CONV_PAD + CONV_TS, :] = c_ref[...].astype(F32) * x_ref[...].astype(F32)

        def emit(r0, cz):
            rows = pl.ds(r0, CONV_ROWS)
            cz_ref[rows, :] = cz
            y_ref[rows, :] = (b_ref[rows, :].astype(F32) * cz).astype(y_ref.dtype)

        _causal_taps(pad_ref, sh_ref, w_ref, SC_KERNEL, emit)

    def pspec(off):
        return pl.BlockSpec((CONV_TS, CONV_TC), lambda c, b, si: (b * ns + si, (blk_b + off) * cb + c))

    ospec = pl.BlockSpec((CONV_TS, CONV_TC), lambda c, b, si: (b * ns + si, c))
    return pl.pallas_call(
        body, name=name, grid=(nc, nb, ns),
        in_specs=[pspec(0), pspec(1), pspec(2), pl.BlockSpec((SC_KERNEL, CONV_TC), lambda c, b, si: (0, c))],
        out_specs=[ospec, ospec],
        out_shape=[jax.ShapeDtypeStruct((t, d), MXU_DTYPE), jax.ShapeDtypeStruct((t, d), F32)],
        scratch_shapes=CONV_SCRATCH,
        compiler_params=_params(("parallel", "arbitrary", "arbitrary")),
    )(p, p, p, w)


def short_conv_bwd(name, p, blk_b, w, cz, dy, nb):
    t = p.shape[0]
    d = w.shape[1]
    s, ns, nc = _conv_grid(t, d, nb)
    cb = d // CONV_TC

    def body(b_ref, c_ref, x_ref, w_ref, cz_ref, dy_ref, db_ref, dc_ref, dx_ref, dw_ref, pad_ref, sh_ref, dw_acc):
        _carry_future(pad_ref, pl.program_id(2))
        dyv = dy_ref[...].astype(F32)
        db_ref[...] = (dyv * cz_ref[...]).astype(db_ref.dtype)
        pad_ref[0:CONV_TS, :] = dyv * b_ref[...].astype(F32)
        finish = _conv_bwd_edges(dw_acc, dw_ref, nb, ns)

        def x_rows(r0):
            rows = pl.ds(r0, CONV_ROWS)
            return c_ref[rows, :].astype(F32), x_ref[rows, :].astype(F32)

        def emit(r0, cx, dz):
            rows = pl.ds(r0, CONV_ROWS)
            dc_ref[rows, :] = (dz * cx[1]).astype(dc_ref.dtype)
            dx_ref[rows, :] = (dz * cx[0]).astype(dx_ref.dtype)

        _conv_bwd_taps(pad_ref, sh_ref, w_ref, dw_acc, SC_KERNEL, x_rows, emit, lambda cx: cx[0] * cx[1])
        finish()

    def row(b, si):
        return b * ns + (ns - 1 - si)

    def pspec(off):
        return pl.BlockSpec((CONV_TS, CONV_TC), lambda c, b, si: (row(b, si), (blk_b + off) * cb + c))

    ospec = pl.BlockSpec((CONV_TS, CONV_TC), lambda c, b, si: (row(b, si), c))
    wspec = pl.BlockSpec((SC_KERNEL, CONV_TC), lambda c, b, si: (0, c))
    return pl.pallas_call(
        body, name=name, grid=(nc, nb, ns),
        in_specs=[pspec(0), pspec(1), pspec(2), wspec, ospec, ospec],
        out_specs=[ospec, ospec, ospec, wspec],
        out_shape=[jax.ShapeDtypeStruct((t, d), MXU_DTYPE)] * 3 + [jax.ShapeDtypeStruct((SC_KERNEL, d), F32)],
        scratch_shapes=CONV_SCRATCH + [pltpu.VMEM((SC_KERNEL, SUBLANES, CONV_TC), F32)],
        compiler_params=_params(("parallel", "arbitrary", "arbitrary")),
    )(p, p, p, w, cz, dy)


def conformer_conv_fwd(name, p, blk_a, w, bias, nb):
    t = p.shape[0]
    d = w.shape[1]
    s, ns, nc = _conv_grid(t, d, nb)
    cb = d // CONV_TC

    def body(a_ref, b_ref, w_ref, bias_ref, u_ref, pad_ref, sh_ref):
        _carry_past(pad_ref, pl.program_id(2))
        pad_ref[CONV_PAD:CONV_PAD + CONV_TS, :] = a_ref[...].astype(F32) * jax.nn.sigmoid(b_ref[...].astype(F32))

        def emit(r0, u):
            u_ref[pl.ds(r0, CONV_ROWS), :] = u + bias_ref[...]

        _causal_taps(pad_ref, sh_ref, w_ref, CF_KERNEL, emit)

    def pspec(off):
        return pl.BlockSpec((CONV_TS, CONV_TC), lambda c, b, si: (b * ns + si, (blk_a + off) * cb + c))

    return pl.pallas_call(
        body, name=name, grid=(nc, nb, ns),
        in_specs=[pspec(0), pspec(1), pl.BlockSpec((CF_KERNEL, CONV_TC), lambda c, b, si: (0, c)),
                  pl.BlockSpec((1, CONV_TC), lambda c, b, si: (0, c))],
        out_specs=pl.BlockSpec((CONV_TS, CONV_TC), lambda c, b, si: (b * ns + si, c)),
        out_shape=jax.ShapeDtypeStruct((t, d), F32),
        scratch_shapes=CONV_SCRATCH,
        compiler_params=_params(("parallel", "arbitrary", "arbitrary")),
    )(p, p, w, bias)


def conformer_conv_bwd(name, p, blk_a, w, du, nb):
    t = p.shape[0]
    d = w.shape[1]
    s, ns, nc = _conv_grid(t, d, nb)
    cb = d // CONV_TC

    def body(a_ref, b_ref, w_ref, du_ref, da_ref, db_ref, dw_ref, dbias_ref, pad_ref, sh_ref, dw_acc):
        _carry_future(pad_ref, pl.program_id(2))
        duv = du_ref[...]
        pad_ref[0:CONV_TS, :] = duv
        finish = _conv_bwd_edges(dw_acc, dw_ref, nb, ns, extra=(dbias_ref,))
        dbias_ref[...] += jnp.sum(duv, axis=0, keepdims=True)

        def x_rows(r0):
            rows = pl.ds(r0, CONV_ROWS)
            return a_ref[rows, :].astype(F32), jax.nn.sigmoid(b_ref[rows, :].astype(F32))

        def emit(r0, asg, du0):
            rows = pl.ds(r0, CONV_ROWS)
            av, sg = asg
            da_ref[rows, :] = (du0 * sg).astype(da_ref.dtype)
            db_ref[rows, :] = (du0 * av * sg * (1.0 - sg)).astype(db_ref.dtype)

        _conv_bwd_taps(pad_ref, sh_ref, w_ref, dw_acc, CF_KERNEL, x_rows, emit, lambda asg: asg[0] * asg[1])
        finish()

    def row(b, si):
        return b * ns + (ns - 1 - si)

    def pspec(off):
        return pl.BlockSpec((CONV_TS, CONV_TC), lambda c, b, si: (row(b, si), (blk_a + off) * cb + c))

    ospec = pl.BlockSpec((CONV_TS, CONV_TC), lambda c, b, si: (row(b, si), c))
    wspec = pl.BlockSpec((CF_KERNEL, CONV_TC), lambda c, b, si: (0, c))
    bspec = pl.BlockSpec((1, CONV_TC), lambda c, b, si: (0, c))
    return pl.pallas_call(
        body, name=name, grid=(nc, nb, ns),
        in_specs=[pspec(0), pspec(1), wspec, ospec],
        out_specs=[ospec, ospec, wspec, bspec],
        out_shape=[jax.ShapeDtypeStruct((t, d), MXU_DTYPE)] * 2
        + [jax.ShapeDtypeStruct((CF_KERNEL, d), F32), jax.ShapeDtypeStruct((1, d), F32)],
        scratch_shapes=CONV_SCRATCH + [pltpu.VMEM((CF_KERNEL, SUBLANES, CONV_TC), F32)],
        compiler_params=_params(("parallel", "arbitrary", "arbitrary")),
    )(p, p, w, du)


BLOCKS = ("ffn1", "mixer", "ffn2")
BLOCK_WEIGHTS = {"ffn1": ("ffn1_w_gu", "ffn1_w_down"), "mixer": ("w_in", "w_ret_o", "w_sc_o", "w_cf_o", "w_o"),
                 "ffn2": ("ffn2_w_gu", "ffn2_w_down")}
BIG = BLOCK_WEIGHTS["ffn1"] + BLOCK_WEIGHTS["mixer"] + BLOCK_WEIGHTS["ffn2"]
MODE = {"ffn1_w_gu": "col", "ffn1_w_down": "row", "w_in": "col", "w_ret_o": "row", "w_sc_o": "row",
        "w_cf_o": "row", "w_o": "row", "ffn2_w_gu": "col", "ffn2_w_down": "row"}
NORM_OF = {"ffn1": 0, "mixer": 2, "ffn2": 4}
BLK_GATE, BLK_SCB, BLK_CFA, BLK_MERGE = 2, 3, 6, 8


def _rope_tables(positions):
    half = RET_QK_DIM // 2
    inv_freq = ROPE_BASE ** (-jnp.arange(half, dtype=F32) / half)
    ang = positions.astype(F32)[..., None] * inv_freq
    cos, sin = jnp.cos(ang), jnp.sin(ang)
    nb, s = positions.shape
    cos2 = jnp.concatenate([cos, cos], axis=-1).reshape(nb * s, RET_QK_DIM)
    sin2 = jnp.concatenate([-sin, sin], axis=-1).reshape(nb * s, RET_QK_DIM)
    return cos2, sin2


def _log_gamma():
    lg = jnp.log(1.0 - 2.0 ** (-5.0 - jnp.arange(RET_HEADS, dtype=F32)))
    return jnp.broadcast_to(lg[:, None, None], (RET_HEADS, 1, RET_QK_DIM))


def _ffn_fwd(xs, w, tag, g_pre, g_post):
    h = rms_fwd("ffn_rms", xs, g_pre)
    gu, a = ffn_up("ffn_up", h, w[tag + "_w_gu"])
    y = mm_fwd("ffn_down", a, w[tag + "_w_down"], "row", F32)
    out = post_fwd("ffn_post", xs, y, g_post, 0.5)
    return out, dict(x=xs, h=h, gu=gu, a=a, y=y, w=w)


def _ffn_bwd(dxs, sv, tag, g_pre, g_post):
    w = sv["w"]
    gu_w, down_w = w[tag + "_w_gu"], w[tag + "_w_down"]
    dy, dg_post = post_bwd("ffn_post_bwd", sv["y"], g_post, dxs, 0.5)
    dgu = ffn_down_dx("ffn_down_dx", dy, down_w, sv["gu"])
    grads = {tag + "_w_down": mm_dw("ffn_down_dw", sv["a"], dy, "row", down_w.shape)}
    dh = mm_dx("ffn_gu_dx", dgu, gu_w, "col", F32)
    grads[tag + "_w_gu"] = mm_dw("ffn_gu_dw", sv["h"], dgu, "col", gu_w.shape)
    dxs, dg_pre = rms_bwd("ffn_rms_bwd", sv["x"], g_pre, dh, dxs)
    return dxs, grads, dg_pre, dg_post


def _mixer_fwd(xs, w, sm, g_pre, g_post, rope, nb, s):
    cos2, sin2, log_g = rope
    d = xs.shape[1]
    gate_blk = (BLK_GATE * d) // RET_V_DIM
    h = rms_fwd("mx_rms", xs, g_pre)
    p = mm_fwd("mx_in", h, w["w_in"], "col", MXU_DTYPE)
    o = retention_fwd("ret_fwd", p, cos2, sin2, log_g, nb, s)
    ya_in = head_gate_fwd("ret_gate", o, p, gate_blk)
    yb_in, cz = short_conv_fwd("sc_fwd", p, BLK_SCB, sm["sc_conv_w"], nb)
    u1 = conformer_conv_fwd("cf_fwd", p, BLK_CFA, sm["cf_dw_w"], sm["cf_dw_b"], nb)
    yc_in = ln_silu_fwd("cf_ln", u1, sm["cf_ln_g"], sm["cf_ln_b"])
    ya = mm_fwd("mx_proj", ya_in, w["w_ret_o"], "row", F32)
    yb = mm_fwd("mx_proj", yb_in, w["w_sc_o"], "row", F32)
    yc = mm_fwd("mx_proj", yc_in, w["w_cf_o"], "row", F32)
    mg = merge_fwd("mx_merge", p, BLK_MERGE, ya, yb, yc)
    m = mm_fwd("mx_proj", mg, w["w_o"], "row", F32)
    out = post_fwd("mx_post", xs, m, g_post, 1.0)
    return out, dict(x=xs, h=h, p=p, o=o, ya_in=ya_in, yb_in=yb_in, cz=cz, u1=u1, yc_in=yc_in, ya=ya, yb=yb, yc=yc,
                     mg=mg, m=m, w=w)


def _mixer_bwd(dxs, sv, sm, g_pre, g_post, rope, nb, s):
    cos2, sin2, log_g = rope
    w, p = sv["w"], sv["p"]
    d = dxs.shape[1]
    gate_blk = (BLK_GATE * d) // RET_V_DIM
    grads, gsm = {}, {}

    def proj_bwd(wname, a_in, dy, out_dtype):
        grads[wname] = mm_dw("mx_proj_dw", a_in, dy, "row", w[wname].shape)
        return mm_dx("mx_proj_dx", dy, w[wname], "row", out_dtype)

    dm, dg_post = post_bwd("mx_post_bwd", sv["m"], g_post, dxs, 1.0)
    dmg = proj_bwd("w_o", sv["mg"], dm, MXU_DTYPE)
    dg0, dg1, dg2, dya, dyb, dyc = merge_bwd("mx_merge_bwd", p, BLK_MERGE, sv["ya"], sv["yb"], sv["yc"], dmg)
    dya_in = proj_bwd("w_ret_o", sv["ya_in"], dya, MXU_DTYPE)
    dyb_in = proj_bwd("w_sc_o", sv["yb_in"], dyb, MXU_DTYPE)
    dyc_in = proj_bwd("w_cf_o", sv["yc_in"], dyc, MXU_DTYPE)
    do, dgret = head_gate_bwd("ret_gate_bwd", sv["o"], p, gate_blk, dya_in)
    dq, dk, dv = retention_bwd("ret_bwd", p, cos2, sin2, log_g, do, nb, s)
    dscb, dscc, dscx, gsm["sc_conv_w"] = short_conv_bwd("sc_bwd", p, BLK_SCB, sm["sc_conv_w"], sv["cz"], dyb_in, nb)
    du1, dlg, dlb = ln_silu_bwd("cf_ln_bwd", sv["u1"], sm["cf_ln_g"], sm["cf_ln_b"], dyc_in)
    dcfa, dcfb, gsm["cf_dw_w"], dbias = conformer_conv_bwd("cf_bwd", p, BLK_CFA, sm["cf_dw_w"], du1, nb)
    gsm.update(cf_ln_g=dlg[0], cf_ln_b=dlb[0], cf_dw_b=dbias[0])
    dp = concat_cols("mx_dp", [dq, dk, dv, dgret, dscb, dscc, dscx, dcfa, dcfb, dg0, dg1, dg2])
    dh = mm_dx("mx_in_dx", dp, w["w_in"], "col", F32)
    grads["w_in"] = mm_dw("mx_in_dw", sv["h"], dp, "col", w["w_in"].shape)
    dxs, dg_pre = rms_bwd("mx_rms_bwd", sv["x"], g_pre, dh, dxs)
    return dxs, grads, gsm, dg_pre, dg_post


def local_step(x, positions, target, small, fetch, push):
    nb, s, d = x.shape
    t = nb * s
    depth = small["norm_g"].shape[0]
    rope = _rope_tables(positions) + (_log_gamma(),)
    xs = x.reshape(t, d)
    token = [None]

    def gain(l, i):
        g = small["norm_g"][l, i][None, :]
        if token[0] is not None:
            g, token[0] = g + token[0], None
        return g

    def mixer_small(l):
        return dict(sc_conv_w=small["sc_conv_w"][l], cf_dw_w=small["cf_dw_w"][l], cf_dw_b=small["cf_dw_b"][l][None, :],
                    cf_ln_g=small["cf_ln_g"][l][None, :], cf_ln_b=small["cf_ln_b"][l][None, :])

    saved = {}
    for l in range(depth):
        for blk in BLOCKS:
            w = fetch(l, blk, xs)
            i0 = NORM_OF[blk]
            if blk == "mixer":
                xs, saved[l, blk] = _mixer_fwd(xs, w, mixer_small(l), gain(l, i0), gain(l, i0 + 1), rope, nb, s)
            else:
                xs, saved[l, blk] = _ffn_fwd(xs, w, blk, gain(l, i0), gain(l, i0 + 1))

    dxs, loss = loss_head("loss", xs, target.reshape(t, d))

    dnorm = [[None] * 6 for _ in range(depth)]
    gsmall = {n: [None] * depth for n in ("sc_conv_w", "cf_dw_w", "cf_dw_b", "cf_ln_g", "cf_ln_b")}
    for l in reversed(range(depth)):
        for blk in reversed(BLOCKS):
            i0 = NORM_OF[blk]
            g_post, g_pre = gain(l, i0 + 1), gain(l, i0)
            if blk == "mixer":
                dxs, grads, gsm, dnorm[l][i0], dnorm[l][i0 + 1] = _mixer_bwd(
                    dxs, saved[l, blk], mixer_small(l), g_pre, g_post, rope, nb, s)
                for n, v in gsm.items():
                    gsmall[n][l] = v
            else:
                dxs, grads, dnorm[l][i0], dnorm[l][i0 + 1] = _ffn_bwd(dxs, saved[l, blk], blk, g_pre, g_post)
            token[0] = push(l, blk, grads)

    gs = {n: jnp.stack(v) for n, v in gsmall.items()}
    gs["norm_g"] = jnp.stack([jnp.concatenate(r, axis=0) for r in dnorm])
    return loss, dxs.reshape(nb, s, d), gs


ANY = pl.BlockSpec(memory_space=pl.ANY)
HBM = pl.BlockSpec(memory_space=pltpu.HBM)
SEM = pl.BlockSpec(memory_space=pltpu.SEMAPHORE)
VMEM_WHOLE = pl.BlockSpec(memory_space=pltpu.VMEM)
EFFECT = pltpu.SideEffectType.DATAFLOW_SIDE_EFFECTING
TOKEN = jax.ShapeDtypeStruct((8, 128), F32)


def _other_chips(x, y):
    return [(1 - x, y), (x, 1 - y), (1 - x, 1 - y)]


def _remote(src, dst, send_sem, recv_sem, to):
    return pltpu.make_async_remote_copy(src_ref=src, dst_ref=dst, send_sem=send_sem, recv_sem=recv_sem,
                                        device_id=to, device_id_type=MESH)


def _in_hbm(v):
    return pltpu.with_memory_space_constraint(v, pltpu.HBM)


def place_quarters(ws, layer, ids, after):
    m = len(ws)

    def body(ids_ref, *refs):
        for w_ref, o_ref in zip(refs[:m], refs[m + 1:]):
            o_ref[...] = w_ref[...].astype(o_ref.dtype)

    def spec(w, where):
        return pl.BlockSpec((None, w.shape[1] // STREAM_STEPS, w.shape[2]), where)

    return pl.pallas_call(
        body, name="place_quarters",
        grid_spec=pltpu.PrefetchScalarGridSpec(
            num_scalar_prefetch=1, grid=(STREAM_STEPS,),
            in_specs=[spec(w, lambda i, ids_ref: (layer, i, 0)) for w in ws] + [ANY],
            out_specs=[spec(w, lambda i, ids_ref: (ids_ref[0], i, 0)) for w in ws]),
        out_shape=[jax.ShapeDtypeStruct((N_CHIP,) + w.shape[1:], MXU_DTYPE) for w in ws],
        compiler_params=_params(("parallel",)),
    )(ids, *ws, after)


def _gather_copies(lands, send, recv):
    x, y, c = _axes()
    me = 2 * x + y
    mine, theirs = [], []
    for a, ld in enumerate(lands):
        rh = ld.shape[1] // 2
        rows = pl.ds(c * rh, rh)
        for k, (px, py) in enumerate(_other_chips(x, y)):
            to = (px, py, c)
            mine.append(_remote(ld.at[me, rows, :], ld.at[me, rows, :], send.at[3 * a + k], recv.at[3 * a + k], to))
            got = ld.at[2 * px + py, rows, :]
            theirs.append(_remote(got, got, send.at[3 * a + k], recv.at[3 * a + k], to))
    return mine, theirs


def gather_start(name, groups, after):
    flat = [s for g in groups for s in g]
    n, ng = len(flat), len(groups)
    sizes = [len(g) for g in groups]

    def body(*refs):
        lands = refs[:n]
        sems = refs[n + 1:n + 1 + 2 * ng]
        token = refs[-1]
        at = 0
        for g, m in enumerate(sizes):
            mine, _ = _gather_copies(lands[at:at + m], sems[2 * g], sems[2 * g + 1])
            for cp in mine:
                cp.start()
            at += m
        token[...] = jnp.zeros_like(token)

    sem_shapes = []
    for m in sizes:
        sem_shapes += [pltpu.SemaphoreType.DMA((3 * m,))] * 2
    res = pl.pallas_call(
        body, name=name, in_specs=[HBM] * n + [ANY],
        out_specs=[SEM] * (2 * ng) + [HBM] * n + [VMEM_WHOLE],
        out_shape=sem_shapes + [pltpu.HBM(s.shape, s.dtype) for s in flat] + [TOKEN],
        input_output_aliases={i: 2 * ng + i for i in range(n)},
        compiler_params=pltpu.CompilerParams(has_side_effects=EFFECT),
    )(*[_in_hbm(s) for s in flat], after)
    sems, thru, token = res[:2 * ng], res[2 * ng:2 * ng + n], res[-1]
    out, at = [], 0
    for g, m in enumerate(sizes):
        out.append((sems[2 * g], sems[2 * g + 1], thru[at:at + m]))
        at += m
    return out, token


def gather_wait(lands, send, recv, after):
    m = len(lands)

    def body(*refs):
        mine, theirs = _gather_copies(refs[:m], refs[m], refs[m + 1])
        for cp in mine:
            cp.wait_send()
        for cp in theirs:
            cp.wait_recv()

    return pl.pallas_call(
        body, name="gather_wait", in_specs=[HBM] * m + [SEM, SEM, ANY], out_specs=[HBM] * m,
        out_shape=[pltpu.HBM(l.shape, l.dtype) for l in lands],
        input_output_aliases={i: i for i in range(m)},
        compiler_params=pltpu.CompilerParams(has_side_effects=EFFECT),
    )(*lands, send, recv, after)


def sibling_fill(lands):
    m = len(lands)

    def body(*refs):
        lds = refs[:m]
        send, recv = refs[2 * m:]
        x, y, c = _axes()
        sib = (x, y, 1 - c)
        cps = []
        for a in range(m):
            rh = lds[a].shape[1] // 2
            for k, (px, py) in enumerate(_other_chips(x, y)):
                got = lds[a].at[2 * px + py, pl.ds(c * rh, rh), :]
                cp = _remote(got, got, send.at[3 * a + k], recv.at[3 * a + k], sib)
                cp.start()
                cps.append(cp)
        for a in range(m):
            rh = lds[a].shape[1] // 2
            for k, (px, py) in enumerate(_other_chips(x, y)):
                blk = lds[a].at[2 * px + py, pl.ds((1 - c) * rh, rh), :]
                _remote(blk, blk, send.at[3 * a + k], recv.at[3 * a + k], sib).wait_recv()
        for cp in cps:
            cp.wait_send()

    return pl.pallas_call(
        body, name="sibling_fill", in_specs=[ANY] * m, out_specs=[ANY] * m,
        out_shape=[jax.ShapeDtypeStruct(l.shape, l.dtype) for l in lands],
        input_output_aliases={i: i for i in range(m)},
        scratch_shapes=[pltpu.SemaphoreType.DMA((3 * m,))] * 2,
    )(*lands)


def _presum_copies(grads, lands, send, recv):
    x, y, c = _axes()
    cps = []
    for a, (g, ld) in enumerate(zip(grads, lands)):
        rh = g.shape[1] // 2
        cps.append(_remote(g.at[:, pl.ds((1 - c) * rh, rh), :], ld, send.at[a], recv.at[a], (x, y, 1 - c)))
    return cps


def presum_start(grads):
    m = len(grads)

    def body(*refs):
        for cp in _presum_copies(refs[:m], refs[m:2 * m], refs[2 * m], refs[2 * m + 1]):
            cp.start()
        refs[-1][...] = jnp.zeros_like(refs[-1])

    lands = [lax.empty((g.shape[0], g.shape[1] // 2, g.shape[2]), g.dtype) for g in grads]
    res = pl.pallas_call(
        body, name="presum_start", in_specs=[HBM] * (2 * m), out_specs=[SEM, SEM] + [HBM] * (2 * m) + [VMEM_WHOLE],
        out_shape=[pltpu.SemaphoreType.DMA((m,))] * 2 + [pltpu.HBM(g.shape, g.dtype) for g in grads]
        + [pltpu.HBM(l.shape, l.dtype) for l in lands] + [TOKEN],
        input_output_aliases={i: 2 + i for i in range(2 * m)},
        compiler_params=pltpu.CompilerParams(has_side_effects=EFFECT),
    )(*[_in_hbm(g) for g in grads], *[_in_hbm(l) for l in lands])
    return res[0], res[1], res[2:2 + m], res[2 + m:2 + 2 * m], res[-1]


def presum_wait(grads, lands, send, recv, after):
    m = len(grads)

    def body(*refs):
        for cp in _presum_copies(refs[:m], refs[m:2 * m], refs[2 * m], refs[2 * m + 1]):
            cp.wait_send()
            cp.wait_recv()

    res = pl.pallas_call(
        body, name="presum_wait", in_specs=[HBM] * (2 * m) + [SEM, SEM] + [ANY] * len(after),
        out_specs=[HBM] * (2 * m),
        out_shape=[pltpu.HBM(g.shape, g.dtype) for g in grads] + [pltpu.HBM(l.shape, l.dtype) for l in lands],
        input_output_aliases={i: i for i in range(2 * m)},
        compiler_params=pltpu.CompilerParams(has_side_effects=EFFECT),
    )(*grads, *lands, send, recv, *after)
    return res[:m], res[m:]


def add_halves(gs, lands, ids):
    m = len(gs)

    def body(ids_ref, *refs):
        for a_ref, b_ref, o_ref in zip(refs[:m], refs[m:2 * m], refs[2 * m:]):
            o_ref[...] = (a_ref[...].astype(F32) + b_ref[...].astype(F32)).astype(o_ref.dtype)

    def spec(ld, where):
        return pl.BlockSpec((None,) + ld.shape[1:], where)

    return pl.pallas_call(
        body, name="add_halves",
        grid_spec=pltpu.PrefetchScalarGridSpec(
            num_scalar_prefetch=1, grid=(N_CHIP,),
            in_specs=[spec(ld, lambda i, ids_ref: (i, ids_ref[1], 0)) for ld in lands]
            + [spec(ld, lambda i, ids_ref: (i, 0, 0)) for ld in lands],
            out_specs=[spec(ld, lambda i, ids_ref: (i, 0, 0)) for ld in lands]),
        out_shape=[jax.ShapeDtypeStruct(ld.shape, ld.dtype) for ld in lands],
        compiler_params=_params(("parallel",)),
    )(ids, *gs, *lands)


def _scatter_copies(parts, lands, send, recv):
    x, y, c = _axes()
    cps = []
    for a, (pt, ld) in enumerate(zip(parts, lands)):
        for k, (px, py) in enumerate(_other_chips(x, y)):
            cps.append(_remote(pt.at[2 * px + py], ld.at[k], send.at[3 * a + k], recv.at[3 * a + k], (px, py, c)))
    return cps


def scatter_start(parts):
    m = len(parts)

    def body(*refs):
        for cp in _scatter_copies(refs[:m], refs[m:2 * m], refs[2 * m], refs[2 * m + 1]):
            cp.start()
        refs[-1][...] = jnp.zeros_like(refs[-1])

    lands = [lax.empty((N_CHIP - 1,) + p.shape[1:], p.dtype) for p in parts]
    res = pl.pallas_call(
        body, name="scatter_start", in_specs=[HBM] * (2 * m), out_specs=[SEM, SEM] + [HBM] * (2 * m) + [VMEM_WHOLE],
        out_shape=[pltpu.SemaphoreType.DMA((3 * m,))] * 2 + [pltpu.HBM(p.shape, p.dtype) for p in parts]
        + [pltpu.HBM(l.shape, l.dtype) for l in lands] + [TOKEN],
        input_output_aliases={i: 2 + i for i in range(2 * m)},
        compiler_params=pltpu.CompilerParams(has_side_effects=EFFECT),
    )(*[_in_hbm(p) for p in parts], *[_in_hbm(l) for l in lands])
    return res[0], res[1], res[2:2 + m], res[2 + m:2 + 2 * m], res[-1]


def scatter_wait(parts, lands, send, recv, after):
    m = len(parts)

    def body(*refs):
        for cp in _scatter_copies(refs[:m], refs[m:2 * m], refs[2 * m], refs[2 * m + 1]):
            cp.wait_send()
            cp.wait_recv()

    res = pl.pallas_call(
        body, name="scatter_wait", in_specs=[HBM] * (2 * m) + [SEM, SEM] + [ANY] * len(after),
        out_specs=[HBM] * (2 * m),
        out_shape=[pltpu.HBM(p.shape, p.dtype) for p in parts] + [pltpu.HBM(l.shape, l.dtype) for l in lands],
        input_output_aliases={i: i for i in range(2 * m)},
        compiler_params=pltpu.CompilerParams(has_side_effects=EFFECT),
    )(*parts, *lands, send, recv, *after)
    return res[:m], res[m:]


def sum_partials(parts, lands, ids, layer, depth, intos):
    m = len(parts)
    nt = STREAM_STEPS

    def body(ids_ref, *refs):
        for p_ref, l_ref, o_ref in zip(refs[:m], refs[m:2 * m], refs[-m:]):
            acc = p_ref[...].astype(F32)
            for k in range(N_CHIP - 1):
                acc = acc + l_ref[k].astype(F32)
            o_ref[...] = acc

    def rows(p):
        return p.shape[1] // nt

    in_specs = [pl.BlockSpec((None, rows(p), p.shape[2]), lambda i, ids_ref: (ids_ref[0], i, 0)) for p in parts]
    in_specs += [pl.BlockSpec((N_CHIP - 1, rows(p), p.shape[2]), lambda i, ids_ref: (0, i, 0)) for p in parts]
    args = [ids, *parts, *lands]
    aliases = {}
    if intos is not None:
        in_specs += [ANY] * m
        args += list(intos)
        aliases = {1 + 2 * m + a: a for a in range(m)}
    return pl.pallas_call(
        body, name="sum_partials",
        grid_spec=pltpu.PrefetchScalarGridSpec(
            num_scalar_prefetch=1, grid=(nt,), in_specs=in_specs,
            out_specs=[pl.BlockSpec((None, rows(p), p.shape[2]), lambda i, ids_ref: (layer, ids_ref[1] * nt + i, 0))
                       for p in parts]),
        out_shape=[jax.ShapeDtypeStruct((depth, 2 * p.shape[1], p.shape[2]), F32) for p in parts],
        input_output_aliases=aliases, compiler_params=_params(("parallel",)),
    )(*args)


def exchange_final_halves(bufs, layers):
    n = len(bufs)

    def body(*refs):
        outs = refs[n:2 * n]
        send, recv = refs[2 * n:]
        x, y, c = _axes()
        sib = (x, y, 1 - c)
        cps, at = [], 0
        for a in range(n):
            rh = outs[a].shape[1] // 2
            for l in layers[a]:
                mine = outs[a].at[l, pl.ds(c * rh, rh), :]
                cp = _remote(mine, mine, send.at[at], recv.at[at], sib)
                cp.start()
                cps.append(cp)
                at += 1
        at = 0
        for a in range(n):
            rh = outs[a].shape[1] // 2
            for l in layers[a]:
                theirs = outs[a].at[l, pl.ds((1 - c) * rh, rh), :]
                _remote(theirs, theirs, send.at[at], recv.at[at], sib).wait_recv()
                at += 1
        for cp in cps:
            cp.wait_send()

    ncp = sum(len(ls) for ls in layers)
    return pl.pallas_call(
        body, name="exchange_final_halves", in_specs=[ANY] * n, out_specs=[ANY] * n,
        out_shape=[jax.ShapeDtypeStruct(g.shape, g.dtype) for g in bufs],
        input_output_aliases={i: i for i in range(n)},
        scratch_shapes=[pltpu.SemaphoreType.DMA((ncp,))] * 2,
    )(*bufs)


def allgather_small(pk):
    def body(in_ref, out_ref, send, recv):
        x, y, c = _axes()
        me = 2 * x + y
        chips = _other_chips(x, y)
        out_ref[pl.ds(me, 1)] = in_ref[...][None]
        cps = []
        for k, (px, py) in enumerate(chips):
            cp = _remote(in_ref, out_ref.at[me], send.at[k], recv.at[k], (px, py, c))
            cp.start()
            cps.append(cp)
        for k, (px, py) in enumerate(chips):
            got = out_ref.at[2 * px + py]
            _remote(got, got, send.at[k], recv.at[k], (px, py, c)).wait_recv()
        for cp in cps:
            cp.wait_send()

    return pl.pallas_call(
        body, name="allgather_small", in_specs=[VMEM_WHOLE], out_specs=VMEM_WHOLE,
        out_shape=jax.ShapeDtypeStruct((N_CHIP,) + pk.shape, pk.dtype),
        scratch_shapes=[pltpu.SemaphoreType.DMA((3,))] * 2,
    )(pk)


def allreduce_small(g):
    ndev = 8

    def body(in_ref, out_ref, slots, send, recv):
        x, y, c = _axes()
        me = 4 * x + 2 * y + c
        slots[pl.ds(me, 1)] = in_ref[...][None]
        peers = []
        for mask in range(1, ndev):
            px = 1 - x if mask & 4 else x
            py = 1 - y if mask & 2 else y
            pc = 1 - c if mask & 1 else c
            peers.append((px, py, pc))
        cps = []
        for k, peer in enumerate(peers):
            cp = _remote(in_ref, slots.at[me], send.at[k], recv.at[k], peer)
            cp.start()
            cps.append(cp)
        for k, (px, py, pc) in enumerate(peers):
            got = slots.at[4 * px + 2 * py + pc]
            _remote(got, got, send.at[k], recv.at[k], (px, py, pc)).wait_recv()
        for cp in cps:
            cp.wait_send()
        acc = slots[0]
        for d in range(1, ndev):
            acc = acc + slots[d]
        out_ref[...] = acc

    return pl.pallas_call(
        body, name="allreduce_small", in_specs=[VMEM_WHOLE], out_specs=VMEM_WHOLE,
        out_shape=jax.ShapeDtypeStruct(g.shape, g.dtype),
        scratch_shapes=[pltpu.VMEM((ndev,) + g.shape, g.dtype), pltpu.SemaphoreType.DMA((ndev - 1,)),
                        pltpu.SemaphoreType.DMA((ndev - 1,))],
    )(g)


def adamw(w, g, m, v):
    shape = w.shape
    cols = shape[-1]
    rows = int(np.prod(shape[:-1]))
    tr = rows
    for cand in (256, 128):
        if rows % cand == 0 and cand * cols * 4 <= 2 * 1024 * 1024:
            tr = cand
            break
    c1 = 1.0 - ADAM_B1 ** ADAM_STEP
    c2 = 1.0 - ADAM_B2 ** ADAM_STEP

    def body(w_ref, g_ref, m_ref, v_ref, d_ref, nm_ref, nv_ref, g_out):
        gv = g_ref[...]
        g_out[...] = gv
        nm = ADAM_B1 * m_ref[...] + (1.0 - ADAM_B1) * gv
        nv = ADAM_B2 * v_ref[...] + (1.0 - ADAM_B2) * jnp.square(gv)
        d_ref[...] = -ADAM_LR * ((nm / c1) / (jnp.sqrt(nv / c2) + ADAM_EPS) + ADAM_WD * w_ref[...])
        nm_ref[...] = nm
        nv_ref[...] = nv

    spec = pl.BlockSpec((tr, cols), lambda i: (i, 0))
    res = pl.pallas_call(
        body, name="adamw", grid=(rows // tr,), in_specs=[spec] * 4, out_specs=[spec] * 4,
        out_shape=[jax.ShapeDtypeStruct((rows, cols), F32)] * 4, compiler_params=_params(("parallel",)),
    )(*[a.reshape(rows, cols) for a in (w, g, m, v)])
    return [r.reshape(shape) for r in res]


WEIGHTS = ("norm_g", "ffn1_w_gu", "ffn1_w_down", "w_in", "w_ret_o", "sc_conv_w", "w_sc_o", "cf_dw_w", "cf_dw_b",
           "cf_ln_g", "cf_ln_b", "w_cf_o", "w_o", "ffn2_w_gu", "ffn2_w_down")
SHARDED_SMALL = ("norm_g", "sc_conv_w", "cf_dw_w")
REPLICATED_SMALL = ("cf_dw_b", "cf_ln_g", "cf_ln_b")

def _pack_rows(parts):
    padded, offs, at = [], [], 0
    for p in parts:
        r = -(-p.shape[0] // SUBLANES) * SUBLANES
        padded.append(jnp.pad(p, ((0, r - p.shape[0]), (0, 0))))
        offs.append(at)
        at += r
    return jnp.concatenate(padded, axis=0), offs


def kernel(x, positions, norm_g, ffn1_w_gu, ffn1_w_down, w_in, w_ret_o, sc_conv_w, w_sc_o, cf_dw_w, cf_dw_b, cf_ln_g, cf_ln_b, w_cf_o, w_o, ffn2_w_gu, ffn2_w_down, loss_target, m_norm_g, m_ffn1_w_gu, m_ffn1_w_down, m_w_in, m_w_ret_o, m_sc_conv_w, m_w_sc_o, m_cf_dw_w, m_cf_dw_b, m_cf_ln_g, m_cf_ln_b, m_w_cf_o, m_w_o, m_ffn2_w_gu, m_ffn2_w_down, v_norm_g, v_ffn1_w_gu, v_ffn1_w_down, v_w_in, v_w_ret_o, v_sc_conv_w, v_w_sc_o, v_cf_dw_w, v_cf_dw_b, v_cf_ln_g, v_cf_ln_b, v_w_cf_o, v_w_o, v_ffn2_w_gu, v_ffn2_w_down):
    wts = dict(zip(WEIGHTS, (norm_g, ffn1_w_gu, ffn1_w_down, w_in, w_ret_o, sc_conv_w, w_sc_o, cf_dw_w, cf_dw_b,
                             cf_ln_g, cf_ln_b, w_cf_o, w_o, ffn2_w_gu, ffn2_w_down)))
    mom = dict(zip(WEIGHTS, (m_norm_g, m_ffn1_w_gu, m_ffn1_w_down, m_w_in, m_w_ret_o, m_sc_conv_w, m_w_sc_o,
                             m_cf_dw_w, m_cf_dw_b, m_cf_ln_g, m_cf_ln_b, m_w_cf_o, m_w_o, m_ffn2_w_gu, m_ffn2_w_down)))
    var = dict(zip(WEIGHTS, (v_norm_g, v_ffn1_w_gu, v_ffn1_w_down, v_w_in, v_w_ret_o, v_sc_conv_w, v_w_sc_o,
                             v_cf_dw_w, v_cf_dw_b, v_cf_ln_g, v_cf_ln_b, v_w_cf_o, v_w_o, v_ffn2_w_gu, v_ffn2_w_down)))
    depth = norm_g.shape[0]
    dq = norm_g.shape[-1]
    d = N_CHIP * dq
    chip = 2 * lax.axis_index("x") + lax.axis_index("y")
    ids = jnp.stack([chip, lax.axis_index("c")]).astype(jnp.int32)

    pk, offs = _pack_rows([wts[n].reshape(-1, dq) for n in SHARDED_SMALL])
    gk4 = allgather_small(pk)
    gk = gk4.transpose(1, 0, 2).reshape(pk.shape[0], d)
    small = {n: wts[n] for n in REPLICATED_SMALL}
    for n, o in zip(SHARDED_SMALL, offs):
        rows = wts[n].shape[0] * wts[n].shape[1]
        small[n] = gk[o:o + rows].reshape(wts[n].shape[:2] + (d,))

    order = [(l, blk) for l in range(depth) for blk in BLOCKS]
    def placed(groups, after):
        return [place_quarters([wts[n] for n in BLOCK_WEIGHTS[blk]], l, ids, after) for l, blk in groups]

    first, token = gather_start("gather_start_first", placed(order[:1], gk4), gk4)
    rest, token = gather_start("gather_start_rest", placed(order[1:], token), token)
    started = dict(zip(order, first + rest))
    small["norm_g"] = small["norm_g"] + token[0:1, 0:1]

    def fetch(l, blk, after):
        send, recv, lands = started[l, blk]
        lands = gather_wait(lands, send, recv, token if (l, blk) == order[0] else after)
        return dict(zip(BLOCK_WEIGHTS[blk], sibling_fill(lands)))

    gsum = {n: None for n in BIG}
    presums, scatters = [], []

    def scatter_next(after):
        group, gl, lands, send, recv = presums.pop(0)
        gl, lands = presum_wait(gl, lands, send, recv, after)
        send, recv, parts, lands, tok = scatter_start(add_halves(gl, lands, ids))
        scatters.append((group, parts, lands, send, recv))
        return tok

    def sum_next(after):
        (l, blk), parts, lands, send, recv = scatters.pop(0)
        parts, lands = scatter_wait(parts, lands, send, recv, after)
        names = BLOCK_WEIGHTS[blk]
        intos = None if gsum[names[0]] is None else [gsum[n] for n in names]
        gsum.update(zip(names, sum_partials(parts, lands, ids, l, depth, intos)))

    def push(l, blk, grads):
        send, recv, gl, lands, tok = presum_start([grads[n] for n in BLOCK_WEIGHTS[blk]])
        if scatters:
            sum_next((gl[0],))
        if presums:
            tok = tok + scatter_next((gl[0],))
        presums.append(((l, blk), gl, lands, send, recv))
        return tok[0:1, 0:1]

    loss, grad_x, gs = local_step(x, positions, loss_target, small, fetch, push)

    names = SHARDED_SMALL + REPLICATED_SMALL
    pg, offs = _pack_rows([gs[n].reshape(-1, d) for n in names])
    tot = allreduce_small(pg)
    sum_next((scatter_next((grad_x, tot)),))
    grads = {}
    for n, o in zip(names, offs):
        rows = int(np.prod(gs[n].shape[:-1]))
        full = tot[o:o + rows]
        if n in SHARDED_SMALL:
            full = lax.dynamic_slice_in_dim(full, chip * dq, dq, axis=1)
        grads[n] = full.reshape(wts[n].shape)

    last = BLOCK_WEIGHTS[order[0][1]]
    early = [n for n in BIG if n not in last]
    every = tuple(range(depth))
    done = exchange_final_halves([gsum[n] for n in early + list(last)],
                                 [every] * len(early) + [every[1:]] * len(last))
    for n, g in zip(early + list(last), done):
        gsum[n] = g
    delta, new_m, new_v = {}, {}, {}
    for n in WEIGHTS:
        if n not in last:
            g = gsum[n] if n in BIG else grads[n]
            delta[n], new_m[n], new_v[n], grads[n] = adamw(wts[n], g, mom[n], var[n])
    sum_next(tuple(delta[n] for n in WEIGHTS if n not in last))
    done = exchange_final_halves([gsum[n] for n in last], [every[:1]] * len(last))
    for n, g in zip(last, done):
        delta[n], new_m[n], new_v[n], grads[n] = adamw(wts[n], g, mom[n], var[n])

    loss_all = lax.psum(loss[0, 0], ("x", "y", "c"))
    return (loss_all, grad_x, *[grads[n] for n in WEIGHTS], *[delta[n] for n in WEIGHTS],
            *[new_m[n] for n in WEIGHTS], *[new_v[n] for n in WEIGHTS])
```

```python
import functools

import jax
import jax.numpy as jnp
import numpy as np
from jax import lax
from jax.experimental import pallas as pl
from jax.experimental.pallas import tpu as pltpu

F32 = jnp.float32
BF16 = jnp.bfloat16
MXU_DTYPE = BF16
VMEM_LIMIT_BYTES = 56 * 1024 * 1024
MESH = pl.DeviceIdType.MESH

N_CHIP = 4
CHUNK = 64
RET_HEADS = 4
RET_QK_DIM = 128
RET_V_DIM = 256
SC_KERNEL = 3
CF_KERNEL = 31
ROPE_BASE = 10000.0
NORM_EPS = 1e-6
LN_EPS = 1e-5
ADAM_LR = 0.001
ADAM_B1 = 0.9
ADAM_B2 = 0.999
ADAM_EPS = 1e-08
ADAM_WD = 0.01
ADAM_STEP = 10

SUBLANES = 8
CONV_PAD = 32
CONV_TS = 128
CONV_TC = 512
CONV_ROWS = 16
CONV_SCRATCH = [pltpu.VMEM((CONV_TS + CONV_PAD, CONV_TC), F32),
                pltpu.VMEM((SUBLANES - 1, CONV_TS + CONV_PAD - SUBLANES, CONV_TC), F32)]
RET_TQ = 512
MM_TM = 1024
MM_TN = 1536
MM_K1 = 1024
MM_W1 = 8 << 20
MM_SLICE = 256
MM_IN_BYTES = 36 << 20
STREAM_STEPS = 2


def _params(sem):
    return pltpu.CompilerParams(dimension_semantics=sem, vmem_limit_bytes=VMEM_LIMIT_BYTES)


def _axes():
    return lax.axis_index("x"), lax.axis_index("y"), lax.axis_index("c")


NN = (((1,), (0,)), ((), ()))
NT = (((1,), (1,)), ((), ()))
TN = (((0,), (0,)), ((), ()))


def _mm(name, a, b, out_shape, out_dtype, grid, a_spec, b_spec, o_spec, dims, acc_shape):
    nk = grid[2]

    def body(a_ref, b_ref, o_ref, *scratch):
        bv = b_ref[...]
        if bv.ndim == 3:
            bv = bv.reshape(-1, bv.shape[-1])
        part = lax.dot_general(a_ref[...], bv, dims, preferred_element_type=F32)

        def put(v):
            o_ref[...] = v.reshape(o_ref.shape).astype(o_ref.dtype)

        if nk == 1:
            put(part)
        else:
            acc = scratch[0]
            k = pl.program_id(2)

            @pl.when(k == 0)
            def _():
                acc[...] = part

            @pl.when(k > 0)
            def _():
                acc[...] += part

            @pl.when(k == nk - 1)
            def _():
                put(acc[...])

    scratch = [pltpu.VMEM(acc_shape, F32)] if nk > 1 else []
    return pl.pallas_call(
        body, name=name, grid=grid, in_specs=[a_spec, b_spec], out_specs=o_spec,
        out_shape=jax.ShapeDtypeStruct(out_shape, out_dtype), scratch_shapes=scratch,
        compiler_params=_params(("parallel", "parallel", "arbitrary")),
    )(a, b)


def _tile(n, target):
    best = None
    for t in range(128, min(n, target) + 1, 128):
        if n % t == 0:
            best = t
    assert best is not None, (n, target)
    return best


def _token_rows(t, width):
    tt = t
    while tt > MM_TM and tt * width * jnp.dtype(MXU_DTYPE).itemsize * 2 > MM_IN_BYTES:
        tt //= 2
    return tt


def mm_fwd(name, a, w4, mode, out_dtype):
    t = a.shape[0]
    _, r, c = w4.shape
    tm = min(t, MM_TM)
    if mode == "col":
        tn = _tile(c, MM_TN)
        npj = c // tn
        grid = (t // tm, N_CHIP * npj, 1)
        a_spec = pl.BlockSpec((tm, r), lambda i, j, k: (i, 0))
        b_spec = pl.BlockSpec((None, r, tn), lambda i, j, k: (j // npj, 0, j % npj))
        o_spec = pl.BlockSpec((tm, tn), lambda i, j, k: (i, j))
        return _mm(name, a, w4, (t, N_CHIP * c), out_dtype, grid, a_spec, b_spec, o_spec, NN, (tm, tn))
    if w4.size * w4.dtype.itemsize <= MM_W1:
        grid = (t // tm, 1, 1)
        a_spec = pl.BlockSpec((tm, N_CHIP * r), lambda i, j, k: (i, 0))
        b_spec = pl.BlockSpec((N_CHIP, r, c), lambda i, j, k: (0, 0, 0))
        o_spec = pl.BlockSpec((tm, c), lambda i, j, k: (i, 0))
        return _mm(name, a, w4, (t, c), out_dtype, grid, a_spec, b_spec, o_spec, NN, (tm, c))
    grid = (t // tm, 1, N_CHIP)
    a_spec = pl.BlockSpec((tm, r), lambda i, j, k: (i, k))
    b_spec = pl.BlockSpec((None, r, c), lambda i, j, k: (k, 0, 0))
    o_spec = pl.BlockSpec((tm, c), lambda i, j, k: (i, 0))
    return _mm(name, a, w4, (t, c), out_dtype, grid, a_spec, b_spec, o_spec, NN, (tm, c))


def mm_dx(name, dy, w4, mode, out_dtype):
    t = dy.shape[-2]
    _, r, c = w4.shape
    tm = min(t, MM_TM)
    if mode == "col":
        tn, npj = c, 1
        hb = N_CHIP // 2 * npj
        grid = (t // tm, 1, N_CHIP * npj)
        if dy.ndim == 3:
            a_spec = pl.BlockSpec((None, tm, tn), lambda i, j, k: (k // hb, i, k % hb))
        else:
            a_spec = pl.BlockSpec((tm, tn), lambda i, j, k: (i, k))
        b_spec = pl.BlockSpec((None, r, tn), lambda i, j, k: (k // npj, 0, k % npj))
        o_spec = pl.BlockSpec((tm, r), lambda i, j, k: (i, 0))
        return _mm(name, dy, w4, (t, r), out_dtype, grid, a_spec, b_spec, o_spec, NT, (tm, r))
    if N_CHIP * r <= MM_K1:
        grid = (t // tm, 1, 1)
        a_spec = pl.BlockSpec((tm, c), lambda i, j, k: (i, 0))
        b_spec = pl.BlockSpec((N_CHIP, r, c), lambda i, j, k: (0, 0, 0))
        o_spec = pl.BlockSpec((tm, N_CHIP * r), lambda i, j, k: (i, 0))
        return _mm(name, dy, w4, (t, N_CHIP * r), out_dtype, grid, a_spec, b_spec, o_spec, NT, (tm, N_CHIP * r))
    grid = (t // tm, N_CHIP, 1)
    a_spec = pl.BlockSpec((tm, c), lambda i, j, k: (i, 0))
    b_spec = pl.BlockSpec((None, r, c), lambda i, j, k: (j, 0, 0))
    o_spec = pl.BlockSpec((tm, r), lambda i, j, k: (i, j))
    return _mm(name, dy, w4, (t, N_CHIP * r), out_dtype, grid, a_spec, b_spec, o_spec, NT, (tm, r))


def mm_dw(name, a, dy, mode, shape3):
    t = a.shape[0]
    _, r, c = shape3
    if mode == "col":
        tn = _tile(c, MM_TN)
        npj = c // tn
        tt = _token_rows(t, r + tn)
        grid = (1, N_CHIP * npj, t // tt)
        a_spec = pl.BlockSpec((tt, r), lambda i, j, k: (k, 0))
        hb = N_CHIP // 2 * npj
        if dy.ndim == 3:
            b_spec = pl.BlockSpec((None, tt, tn), lambda i, j, k: (j // hb, k, j % hb))
        else:
            b_spec = pl.BlockSpec((tt, tn), lambda i, j, k: (k, j))
        o_spec = pl.BlockSpec((None, r, tn), lambda i, j, k: (j // npj, 0, j % npj))
        return _mm(name, a, dy, shape3, MXU_DTYPE, grid, a_spec, b_spec, o_spec, TN, (r, tn))
    if N_CHIP * r <= MM_K1:
        tt = _token_rows(t, N_CHIP * r + c)
        grid = (1, 1, t // tt)
        a_spec = pl.BlockSpec((tt, N_CHIP * r), lambda i, j, k: (k, 0))
        b_spec = pl.BlockSpec((tt, c), lambda i, j, k: (k, 0))
        o_spec = pl.BlockSpec((N_CHIP, r, c), lambda i, j, k: (0, 0, 0))
        return _mm(name, a, dy, shape3, MXU_DTYPE, grid, a_spec, b_spec, o_spec, TN, (N_CHIP * r, c))
    tt = _token_rows(t, r + c)
    grid = (N_CHIP, 1, t // tt)
    a_spec = pl.BlockSpec((tt, r), lambda i, j, k: (k, i))
    b_spec = pl.BlockSpec((tt, c), lambda i, j, k: (k, 0))
    o_spec = pl.BlockSpec((None, r, c), lambda i, j, k: (i, 0, 0))
    return _mm(name, a, dy, shape3, MXU_DTYPE, grid, a_spec, b_spec, o_spec, TN, (r, c))


def _rowwise(name, fn, rows, pars, outs, accs=(), tm=256, ncol=1):
    t = rows[0][0].shape[0]
    nrow, npar, nout = len(rows), len(pars), len(outs)

    def body(*refs):
        vals = [r[...] for r in refs[:nrow + npar]]
        res = fn(*vals)
        out_refs = refs[nrow + npar:nrow + npar + nout]
        acc_refs = refs[nrow + npar + nout:]
        for o, v in zip(out_refs, res[:nout]):
            o[...] = v.astype(o.dtype)
        i = pl.program_id(1)
        for a, v in zip(acc_refs, res[nout:]):
            @pl.when(i == 0)
            def _(a=a, v=v):
                a[...] = v.astype(F32)

            @pl.when(i > 0)
            def _(a=a, v=v):
                a[...] += v.astype(F32)

    in_specs = [pl.BlockSpec((tm, w), functools.partial(lambda j, i, b: (i, b + j), b=b)) for _, w, b in rows]
    for arr, w in pars:
        if w is None:
            in_specs.append(pl.BlockSpec(arr.shape, lambda j, i: (0, 0)))
        else:
            in_specs.append(pl.BlockSpec((1, w), lambda j, i: (0, j)))
    out_specs = [pl.BlockSpec((tm, w), lambda j, i: (i, j)) for _, w, _ in outs]
    out_specs += [pl.BlockSpec((1, w), lambda j, i: (0, j)) for _, w in accs]
    out_shape = [jax.ShapeDtypeStruct((t, tw), dt) for tw, _, dt in outs]
    out_shape += [jax.ShapeDtypeStruct((1, tw), F32) for tw, _ in accs]
    res = pl.pallas_call(
        body, name=name, grid=(ncol, t // tm), in_specs=in_specs, out_specs=out_specs, out_shape=out_shape,
        compiler_params=_params(("parallel", "arbitrary" if accs else "parallel")),
    )(*[r[0] for r in rows], *[p[0] for p in pars])
    return res


def _rms(x, g):
    xf = x.astype(F32)
    return xf * lax.rsqrt(jnp.mean(xf * xf, axis=-1, keepdims=True) + NORM_EPS) * g


def _silu(x):
    return x * jax.nn.sigmoid(x)


def rms_fwd(name, x, g):
    d = x.shape[1]
    return _rowwise(name, lambda x, g: (_rms(x, g),), [(x, d, 0)], [(g, None)], [(d, d, MXU_DTYPE)], tm=512)[0]


def rms_bwd(name, x, g, dh, dres):
    d = x.shape[1]

    def fn(x, dh, dres, g):
        _, vjp = jax.vjp(_rms, x, g)
        dx, dg = vjp(dh.astype(F32))
        return dres + dx, dg

    return _rowwise(name, fn, [(x, d, 0), (dh, d, 0), (dres, d, 0)], [(g, None)], [(d, d, F32)], [(d, d)], tm=256)


def post_fwd(name, x, y, g, scale):
    d = x.shape[1]
    return _rowwise(name, lambda x, y, g: (x + scale * _rms(y, g),), [(x, d, 0), (y, d, 0)], [(g, None)],
                    [(d, d, F32)], tm=512)[0]


def post_bwd(name, y, g, dx, scale):
    d = y.shape[1]

    def fn(y, dx, g):
        _, vjp = jax.vjp(lambda y, g: scale * _rms(y, g), y, g)
        return vjp(dx)

    return _rowwise(name, fn, [(y, d, 0), (dx, d, 0)], [(g, None)], [(d, d, MXU_DTYPE)], [(d, d)], tm=256)


def ffn_up(name, h, w4):
    t = h.shape[0]
    _, r, c = w4.shape
    tm = min(t, MM_TM)
    tn = _tile(c, MM_TM)
    npj = c // tn
    half = N_CHIP // 2

    def body(h_ref, wg_ref, wu_ref, gu_ref, a_ref):
        hv = h_ref[...]
        g = lax.dot_general(hv, wg_ref[...], NN, preferred_element_type=F32)
        u = lax.dot_general(hv, wu_ref[...], NN, preferred_element_type=F32)
        gu_ref[0] = g.astype(gu_ref.dtype)
        gu_ref[1] = u.astype(gu_ref.dtype)
        a_ref[...] = (_silu(g) * u).astype(a_ref.dtype)

    f = half * c
    return pl.pallas_call(
        body, name=name, grid=(t // tm, half * npj),
        in_specs=[pl.BlockSpec((tm, r), lambda i, j: (i, 0)),
                  pl.BlockSpec((None, r, tn), lambda i, j: (j // npj, 0, j % npj)),
                  pl.BlockSpec((None, r, tn), lambda i, j: (half + j // npj, 0, j % npj))],
        out_specs=[pl.BlockSpec((2, tm, tn), lambda i, j: (0, i, j)), pl.BlockSpec((tm, tn), lambda i, j: (i, j))],
        out_shape=[jax.ShapeDtypeStruct((2, t, f), MXU_DTYPE), jax.ShapeDtypeStruct((t, f), MXU_DTYPE)],
        compiler_params=_params(("parallel", "parallel")),
    )(h, w4, w4)


def ffn_down_dx(name, dy, w4, gu):
    t = dy.shape[0]
    _, r, c = w4.shape
    tm = min(t, MM_TM)

    def body(dy_ref, w_ref, gu_ref, o_ref):
        dyv = dy_ref[...]
        for n0 in range(0, r, MM_SLICE):
            cols = pl.ds(n0, MM_SLICE)
            da = lax.dot_general(dyv, w_ref[cols, :], NT, preferred_element_type=F32)
            gate, up = gu_ref[0, :, cols].astype(F32), gu_ref[1, :, cols].astype(F32)
            sg = jax.nn.sigmoid(gate)
            silu = gate * sg
            o_ref[0, :, cols] = (da * up * (sg + silu * (1.0 - sg))).astype(o_ref.dtype)
            o_ref[1, :, cols] = (da * silu).astype(o_ref.dtype)

    return pl.pallas_call(
        body, name=name, grid=(t // tm, N_CHIP),
        in_specs=[pl.BlockSpec((tm, c), lambda i, j: (i, 0)), pl.BlockSpec((None, r, c), lambda i, j: (j, 0, 0)),
                  pl.BlockSpec((2, tm, r), lambda i, j: (0, i, j))],
        out_specs=pl.BlockSpec((2, tm, r), lambda i, j: (0, i, j)),
        out_shape=jax.ShapeDtypeStruct((2, t, N_CHIP * r), MXU_DTYPE),
        compiler_params=_params(("parallel", "parallel")),
    )(dy, w4, gu)


def _head_gate(o, g):
    mu = jnp.mean(o, axis=-1, keepdims=True)
    var = jnp.mean(jnp.square(o - mu), axis=-1, keepdims=True)
    return _silu(g.astype(F32)) * ((o - mu) * lax.rsqrt(var + LN_EPS))


def head_gate_fwd(name, o, p, gate_blk):
    dv = RET_V_DIM
    return _rowwise(name, lambda o, g: (_head_gate(o, g),), [(o, dv, 0), (p, dv, gate_blk)], [],
                    [(RET_HEADS * dv, dv, MXU_DTYPE)], tm=512, ncol=RET_HEADS)[0]


def head_gate_bwd(name, o, p, gate_blk, da):
    dv = RET_V_DIM

    def fn(o, g, da):
        _, vjp = jax.vjp(_head_gate, o, g.astype(F32))
        return vjp(da.astype(F32))

    w = RET_HEADS * dv
    return _rowwise(name, fn, [(o, dv, 0), (p, dv, gate_blk), (da, dv, 0)], [],
                    [(w, dv, MXU_DTYPE), (w, dv, MXU_DTYPE)], tm=512, ncol=RET_HEADS)


def _ln_silu(u, g, b):
    mu = jnp.mean(u, axis=-1, keepdims=True)
    var = jnp.mean(jnp.square(u - mu), axis=-1, keepdims=True)
    return _silu((u - mu) * lax.rsqrt(var + LN_EPS) * g + b)


def ln_silu_fwd(name, u, g, b):
    d = u.shape[1]
    return _rowwise(name, lambda u, g, b: (_ln_silu(u, g, b),), [(u, d, 0)], [(g, None), (b, None)],
                    [(d, d, MXU_DTYPE)], tm=512)[0]


def ln_silu_bwd(name, u, g, b, dc):
    d = u.shape[1]

    def fn(u, dc, g, b):
        _, vjp = jax.vjp(_ln_silu, u, g, b)
        return vjp(dc.astype(F32))

    return _rowwise(name, fn, [(u, d, 0), (dc, d, 0)], [(g, None), (b, None)], [(d, d, F32)], [(d, d), (d, d)],
                    tm=256)


def _merge(g0, g1, g2, ya, yb, yc):
    s = jax.nn.sigmoid
    return s(g0.astype(F32)) * ya + s(g1.astype(F32)) * yb + s(g2.astype(F32)) * yc


def merge_fwd(name, p, blk, ya, yb, yc):
    d = ya.shape[1]
    rows = [(p, d, blk), (p, d, blk + 1), (p, d, blk + 2), (ya, d, 0), (yb, d, 0), (yc, d, 0)]
    return _rowwise(name, lambda *v: (_merge(*v),), rows, [], [(d, d, MXU_DTYPE)], tm=256)[0]


def merge_bwd(name, p, blk, ya, yb, yc, dmg):
    d = ya.shape[1]

    def fn(g0, g1, g2, ya, yb, yc, dmg):
        _, vjp = jax.vjp(_merge, g0.astype(F32), g1.astype(F32), g2.astype(F32), ya, yb, yc)
        return vjp(dmg.astype(F32))

    rows = [(p, d, blk), (p, d, blk + 1), (p, d, blk + 2), (ya, d, 0), (yb, d, 0), (yc, d, 0), (dmg, d, 0)]
    return _rowwise(name, fn, rows, [], [(d, d, MXU_DTYPE)] * 6, tm=256)


def concat_cols(name, pieces):
    t = pieces[0].shape[0]
    widths = [p.shape[1] for p in pieces]
    tm = 256

    def body(*refs):
        o_ref, at = refs[-1], 0
        for r, w in zip(refs[:-1], widths):
            o_ref[:, at:at + w] = r[...]
            at += w

    return pl.pallas_call(
        body, name=name, grid=(t // tm,),
        in_specs=[pl.BlockSpec((tm, w), lambda i: (i, 0)) for w in widths],
        out_specs=pl.BlockSpec((tm, sum(widths)), lambda i: (i, 0)),
        out_shape=jax.ShapeDtypeStruct((t, sum(widths)), pieces[0].dtype),
        compiler_params=_params(("parallel",)),
    )(*pieces)


def loss_head(name, y, target):
    t, d = y.shape
    tm = 512

    def body(y_ref, t_ref, dy_ref, loss_ref):
        err = y_ref[...] - t_ref[...]
        dy_ref[...] = err * (1.0 / d)
        part = jnp.sum(jnp.sum(err * err, axis=1, keepdims=True), axis=0, keepdims=True) * (0.5 / d)

        @pl.when(pl.program_id(0) == 0)
        def _():
            loss_ref[...] = part

        @pl.when(pl.program_id(0) > 0)
        def _():
            loss_ref[...] += part

    return pl.pallas_call(
        body, name=name, grid=(t // tm,),
        in_specs=[pl.BlockSpec((tm, d), lambda i: (i, 0))] * 2,
        out_specs=[pl.BlockSpec((tm, d), lambda i: (i, 0)), pl.BlockSpec((1, 1), lambda i: (0, 0))],
        out_shape=[jax.ShapeDtypeStruct((t, d), F32), jax.ShapeDtypeStruct((1, 1), F32)],
        compiler_params=_params(("arbitrary",)),
    )(y, target)


def _rot(x, cos2, sin2):
    return x * cos2 + pltpu.roll(x, RET_QK_DIM // 2, 1) * sin2


def _decay_mask(lg, n0, rows, cols):
    n = n0 + lax.broadcasted_iota(jnp.int32, (rows, cols), 0)
    m = lax.broadcasted_iota(jnp.int32, (rows, cols), 1)
    shift = CHUNK.bit_length() - 1
    dist = jnp.abs(n - m).astype(F32)
    return jnp.where((m >> shift) <= (n >> shift), jnp.exp(lg * dist), 0.0)


def _ret_specs(s):
    dk, dv, h = RET_QK_DIM, RET_V_DIM, RET_HEADS
    return [
        pl.BlockSpec((s, dk), lambda b, hh: (b, hh)),
        pl.BlockSpec((s, dk), lambda b, hh: (b, h + hh)),
        pl.BlockSpec((s, dv), lambda b, hh: (b, (2 * h * dk) // dv + hh)),
        pl.BlockSpec((s, dk), lambda b, hh: (b, 0)),
        pl.BlockSpec((s, dk), lambda b, hh: (b, 0)),
        pl.BlockSpec((None, 1, dk), lambda b, hh: (hh, 0, 0)),
    ]


def retention_fwd(name, p, cos2, sin2, log_g, nb, s):
    dk, dv, h = RET_QK_DIM, RET_V_DIM, RET_HEADS

    def body(q_ref, k_ref, v_ref, cos_ref, sin_ref, lg_ref, o_ref, kr_ref):
        lg = lg_ref[0:1, 0:1]
        kr = _rot(k_ref[...].astype(F32), cos_ref[...], sin_ref[...]) * (dk ** -0.5)
        kr_ref[...] = kr.astype(kr_ref.dtype)
        for qi in range(s // RET_TQ):
            n0, kmax = qi * RET_TQ, (qi + 1) * RET_TQ
            rows = pl.ds(n0, RET_TQ)
            qr = _rot(q_ref[rows, :].astype(F32), cos_ref[rows, :], sin_ref[rows, :]).astype(MXU_DTYPE)
            sc = lax.dot_general(qr, kr_ref[0:kmax, :], NT, preferred_element_type=F32)
            pm = (sc * _decay_mask(lg, n0, RET_TQ, kmax)).astype(MXU_DTYPE)
            o_ref[rows, :] = lax.dot_general(pm, v_ref[0:kmax, :], NN, preferred_element_type=F32)

    return pl.pallas_call(
        body, name=name, grid=(nb, h), in_specs=_ret_specs(s),
        out_specs=pl.BlockSpec((s, dv), lambda b, hh: (b, hh)),
        out_shape=jax.ShapeDtypeStruct((nb * s, h * dv), F32),
        scratch_shapes=[pltpu.VMEM((s, dk), MXU_DTYPE)],
        compiler_params=_params(("parallel", "parallel")),
    )(p, p, p, cos2, sin2, log_g)


def retention_bwd(name, p, cos2, sin2, log_g, do, nb, s):
    dk, dv, h = RET_QK_DIM, RET_V_DIM, RET_HEADS

    def body(q_ref, k_ref, v_ref, cos_ref, sin_ref, lg_ref, do_ref, dq_ref, dk_ref, dv_ref, kr_ref, dk_acc, dv_acc):
        lg = lg_ref[0:1, 0:1]
        kr = _rot(k_ref[...].astype(F32), cos_ref[...], sin_ref[...]) * (dk ** -0.5)
        kr_ref[...] = kr.astype(kr_ref.dtype)
        dk_acc[...] = jnp.zeros_like(dk_acc)
        dv_acc[...] = jnp.zeros_like(dv_acc)
        for qi in range(s // RET_TQ):
            n0, kmax = qi * RET_TQ, (qi + 1) * RET_TQ
            rows = pl.ds(n0, RET_TQ)
            cq, sq = cos_ref[rows, :], sin_ref[rows, :]
            qr = _rot(q_ref[rows, :].astype(F32), cq, sq).astype(MXU_DTYPE)
            dob = do_ref[rows, :]
            mask = _decay_mask(lg, n0, RET_TQ, kmax)
            sc = lax.dot_general(qr, kr_ref[0:kmax, :], NT, preferred_element_type=F32)
            pm = (sc * mask).astype(MXU_DTYPE)
            dv_acc[0:kmax, :] += lax.dot_general(pm, dob, TN, preferred_element_type=F32)
            dp = lax.dot_general(dob, v_ref[0:kmax, :], NT, preferred_element_type=F32)
            ds = (dp * mask).astype(MXU_DTYPE)
            dqr = lax.dot_general(ds, kr_ref[0:kmax, :], NN, preferred_element_type=F32)
            dq_ref[rows, :] = _rot(dqr, cq, -sq).astype(dq_ref.dtype)
            dk_acc[0:kmax, :] += lax.dot_general(ds, qr, TN, preferred_element_type=F32)
        dkr = dk_acc[...] * (dk ** -0.5)
        dk_ref[...] = _rot(dkr, cos_ref[...], -sin_ref[...]).astype(dk_ref.dtype)
        dv_ref[...] = dv_acc[...].astype(dv_ref.dtype)

    t = nb * s
    return pl.pallas_call(
        body, name=name, grid=(nb, h),
        in_specs=_ret_specs(s) + [pl.BlockSpec((s, dv), lambda b, hh: (b, hh))],
        out_specs=[pl.BlockSpec((s, dk), lambda b, hh: (b, hh)), pl.BlockSpec((s, dk), lambda b, hh: (b, hh)),
                   pl.BlockSpec((s, dv), lambda b, hh: (b, hh))],
        out_shape=[jax.ShapeDtypeStruct((t, h * dk), MXU_DTYPE), jax.ShapeDtypeStruct((t, h * dk), MXU_DTYPE),
                   jax.ShapeDtypeStruct((t, h * dv), MXU_DTYPE)],
        scratch_shapes=[pltpu.VMEM((s, dk), MXU_DTYPE), pltpu.VMEM((s, dk), F32), pltpu.VMEM((s, dv), F32)],
        compiler_params=_params(("parallel", "parallel")),
    )(p, p, p, cos2, sin2, log_g, do)


def _conv_grid(t, d, nb):
    s = t // nb
    ns, nc = s // CONV_TS, d // CONV_TC
    return s, ns, nc


def _shifted(pad_ref, sh_ref, offsets):
    n = sh_ref.shape[1]
    for b in sorted({off % SUBLANES for off in offsets} - {0}):
        sh_ref[b - 1] = pad_ref[pl.ds(b, n), :]

    def read(off, r0):
        a, b = off - off % SUBLANES + r0, off % SUBLANES
        return pad_ref[pl.ds(a, CONV_ROWS), :] if b == 0 else sh_ref[b - 1, pl.ds(a, CONV_ROWS), :]

    return read


def _causal_taps(pad_ref, sh_ref, w_ref, k, emit):
    offs = [CONV_PAD - (k - 1) + j for j in range(k)]
    read = _shifted(pad_ref, sh_ref, offs)
    for r0 in range(0, CONV_TS, CONV_ROWS):
        acc = None
        for j in range(k):
            term = w_ref[j:j + 1, :] * read(offs[j], r0)
            acc = term if acc is None else acc + term
        emit(r0, acc)


def _carry_past(pad_ref, s_idx):
    @pl.when(s_idx == 0)
    def _():
        pad_ref[0:CONV_PAD, :] = jnp.zeros((CONV_PAD, pad_ref.shape[1]), F32)

    @pl.when(s_idx > 0)
    def _():
        pad_ref[0:CONV_PAD, :] = pad_ref[CONV_TS:CONV_TS + CONV_PAD, :]


def _carry_future(pad_ref, s_idx):
    @pl.when(s_idx == 0)
    def _():
        pad_ref[CONV_TS:CONV_TS + CONV_PAD, :] = jnp.zeros((CONV_PAD, pad_ref.shape[1]), F32)

    @pl.when(s_idx > 0)
    def _():
        pad_ref[CONV_TS:CONV_TS + CONV_PAD, :] = pad_ref[0:CONV_PAD, :]


def _conv_bwd_taps(pad_ref, sh_ref, w_ref, dw_acc, k, x_rows, emit, mix):
    read = _shifted(pad_ref, sh_ref, range(k))
    for r0 in range(0, CONV_TS, CONV_ROWS):
        ops = x_rows(r0)
        x = mix(ops)
        acc = None
        for j in range(k):
            sh = read(k - 1 - j, r0)
            term = w_ref[j:j + 1, :] * sh
            acc = term if acc is None else acc + term
            prod = x * sh
            part = prod[0:SUBLANES]
            for q in range(SUBLANES, CONV_ROWS, SUBLANES):
                part = part + prod[q:q + SUBLANES]
            dw_acc[j] += part
        emit(r0, ops, acc)


def _conv_bwd_edges(dw_acc, dw_ref, nb, ns, extra=()):
    first = jnp.logical_and(pl.program_id(1) == 0, pl.program_id(2) == 0)
    last = jnp.logical_and(pl.program_id(1) == nb - 1, pl.program_id(2) == ns - 1)

    @pl.when(first)
    def _():
        dw_acc[...] = jnp.zeros_like(dw_acc)
        for r in extra:
            r[...] = jnp.zeros_like(r)

    def finish():
        @pl.when(last)
        def _():
            dw_ref[...] = jnp.sum(dw_acc[...], axis=1)

    return finish


def short_conv_fwd(name, p, blk_b, w, nb):
    t = p.shape[0]
    d = w.shape[1]
    s, ns, nc = _conv_grid(t, d, nb)
    cb = d // CONV_TC

    def body(b_ref, c_ref, x_ref, w_ref, y_ref, cz_ref, pad_ref, sh_ref):
        _carry_past(pad_ref, pl.program_id(2))
        pad_ref[CONV_PAD:CONV_PAD + CONV_TS, :] = c_ref[...].astype(F32) * x_ref[...].astype(F32)

        def emit(r0, cz):
            rows = pl.ds(r0, CONV_ROWS)
            cz_ref[rows, :] = cz
            y_ref[rows, :] = (b_ref[rows, :].astype(F32) * cz).astype(y_ref.dtype)

        _causal_taps(pad_ref, sh_ref, w_ref, SC_KERNEL, emit)

    def pspec(off):
        return pl.BlockSpec((CONV_TS, CONV_TC), lambda c, b, si: (b * ns + si, (blk_b + off) * cb + c))

    ospec = pl.BlockSpec((CONV_TS, CONV_TC), lambda c, b, si: (b * ns + si, c))
    return pl.pallas_call(
        body, name=name, grid=(nc, nb, ns),
        in_specs=[pspec(0), pspec(1), pspec(2), pl.BlockSpec((SC_KERNEL, CONV_TC), lambda c, b, si: (0, c))],
        out_specs=[ospec, ospec],
        out_shape=[jax.ShapeDtypeStruct((t, d), MXU_DTYPE), jax.ShapeDtypeStruct((t, d), F32)],
        scratch_shapes=CONV_SCRATCH,
        compiler_params=_params(("parallel", "arbitrary", "arbitrary")),
    )(p, p, p, w)


def short_conv_bwd(name, p, blk_b, w, cz, dy, nb):
    t = p.shape[0]
    d = w.shape[1]
    s, ns, nc = _conv_grid(t, d, nb)
    cb = d // CONV_TC

    def body(b_ref, c_ref, x_ref, w_ref, cz_ref, dy_ref, db_ref, dc_ref, dx_ref, dw_ref, pad_ref, sh_ref, dw_acc):
        _carry_future(pad_ref, pl.program_id(2))
        dyv = dy_ref[...].astype(F32)
        db_ref[...] = (dyv * cz_ref[...]).astype(db_ref.dtype)
        pad_ref[0:CONV_TS, :] = dyv * b_ref[...].astype(F32)
        finish = _conv_bwd_edges(dw_acc, dw_ref, nb, ns)

        def x_rows(r0):
            rows = pl.ds(r0, CONV_ROWS)
            return c_ref[rows, :].astype(F32), x_ref[rows, :].astype(F32)

        def emit(r0, cx, dz):
            rows = pl.ds(r0, CONV_ROWS)
            dc_ref[rows, :] = (dz * cx[1]).astype(dc_ref.dtype)
            dx_ref[rows, :] = (dz * cx[0]).astype(dx_ref.dtype)

        _conv_bwd_taps(pad_ref, sh_ref, w_ref, dw_acc, SC_KERNEL, x_rows, emit, lambda cx: cx[0] * cx[1])
        finish()

    def row(b, si):
        return b * ns + (ns - 1 - si)

    def pspec(off):
        return pl.BlockSpec((CONV_TS, CONV_TC), lambda c, b, si: (row(b, si), (blk_b + off) * cb + c))

    ospec = pl.BlockSpec((CONV_TS, CONV_TC), lambda c, b, si: (row(b, si), c))
    wspec = pl.BlockSpec((SC_KERNEL, CONV_TC), lambda c, b, si: (0, c))
    return pl.pallas_call(
        body, name=name, grid=(nc, nb, ns),
        in_specs=[pspec(0), pspec(1), pspec(2), wspec, ospec, ospec],
        out_specs=[ospec, ospec, ospec, wspec],
        out_shape=[jax.ShapeDtypeStruct((t, d), MXU_DTYPE)] * 3 + [jax.ShapeDtypeStruct((SC_KERNEL, d), F32)],
        scratch_shapes=CONV_SCRATCH + [pltpu.VMEM((SC_KERNEL, SUBLANES, CONV_TC), F32)],
        compiler_params=_params(("parallel", "arbitrary", "arbitrary")),
    )(p, p, p, w, cz, dy)


def conformer_conv_fwd(name, p, blk_a, w, bias, nb):
    t = p.shape[0]
    d = w.shape[1]
    s, ns, nc = _conv_grid(t, d, nb)
    cb = d // CONV_TC

    def body(a_ref, b_ref, w_ref, bias_ref, u_ref, pad_ref, sh_ref):
        _carry_past(pad_ref, pl.program_id(2))
        pad_ref[CONV_PAD:CONV_PAD + CONV_TS, :] = a_ref[...].astype(F32) * jax.nn.sigmoid(b_ref[...].astype(F32))

        def emit(r0, u):
            u_ref[pl.ds(r0, CONV_ROWS), :] = u + bias_ref[...]

        _causal_taps(pad_ref, sh_ref, w_ref, CF_KERNEL, emit)

    def pspec(off):
        return pl.BlockSpec((CONV_TS, CONV_TC), lambda c, b, si: (b * ns + si, (blk_a + off) * cb + c))

    return pl.pallas_call(
        body, name=name, grid=(nc, nb, ns),
        in_specs=[pspec(0), pspec(1), pl.BlockSpec((CF_KERNEL, CONV_TC), lambda c, b, si: (0, c)),
                  pl.BlockSpec((1, CONV_TC), lambda c, b, si: (0, c))],
        out_specs=pl.BlockSpec((CONV_TS, CONV_TC), lambda c, b, si: (b * ns + si, c)),
        out_shape=jax.ShapeDtypeStruct((t, d), F32),
        scratch_shapes=CONV_SCRATCH,
        compiler_params=_params(("parallel", "arbitrary", "arbitrary")),
    )(p, p, w, bias)


def conformer_conv_bwd(name, p, blk_a, w, du, nb):
    t = p.shape[0]
    d = w.shape[1]
    s, ns, nc = _conv_grid(t, d, nb)
    cb = d // CONV_TC

    def body(a_ref, b_ref, w_ref, du_ref, da_ref, db_ref, dw_ref, dbias_ref, pad_ref, sh_ref, dw_acc):
        _carry_future(pad_ref, pl.program_id(2))
        duv = du_ref[...]
        pad_ref[0:CONV_TS, :] = duv
        finish = _conv_bwd_edges(dw_acc, dw_ref, nb, ns, extra=(dbias_ref,))
        dbias_ref[...] += jnp.sum(duv, axis=0, keepdims=True)

        def x_rows(r0):
            rows = pl.ds(r0, CONV_ROWS)
            return a_ref[rows, :].astype(F32), jax.nn.sigmoid(b_ref[rows, :].astype(F32))

        def emit(r0, asg, du0):
            rows = pl.ds(r0, CONV_ROWS)
            av, sg = asg
            da_ref[rows, :] = (du0 * sg).astype(da_ref.dtype)
            db_ref[rows, :] = (du0 * av * sg * (1.0 - sg)).astype(db_ref.dtype)

        _conv_bwd_taps(pad_ref, sh_ref, w_ref, dw_acc, CF_KERNEL, x_rows, emit, lambda asg: asg[0] * asg[1])
        finish()

    def row(b, si):
        return b * ns + (ns - 1 - si)

    def pspec(off):
        return pl.BlockSpec((CONV_TS, CONV_TC), lambda c, b, si: (row(b, si), (blk_a + off) * cb + c))

    ospec = pl.BlockSpec((CONV_TS, CONV_TC), lambda c, b, si: (row(b, si), c))
    wspec = pl.BlockSpec((CF_KERNEL, CONV_TC), lambda c, b, si: (0, c))
    bspec = pl.BlockSpec((1, CONV_TC), lambda c, b, si: (0, c))
    return pl.pallas_call(
        body, name=name, grid=(nc, nb, ns),
        in_specs=[pspec(0), pspec(1), wspec, ospec],
        out_specs=[ospec, ospec, wspec, bspec],
        out_shape=[jax.ShapeDtypeStruct((t, d), MXU_DTYPE)] * 2
        + [jax.ShapeDtypeStruct((CF_KERNEL, d), F32), jax.ShapeDtypeStruct((1, d), F32)],
        scratch_shapes=CONV_SCRATCH + [pltpu.VMEM((CF_KERNEL, SUBLANES, CONV_TC), F32)],
        compiler_params=_params(("parallel", "arbitrary", "arbitrary")),
    )(p, p, w, du)


BLOCKS = ("ffn1", "mixer", "ffn2")
BLOCK_WEIGHTS = {"ffn1": ("ffn1_w_gu", "ffn1_w_down"), "mixer": ("w_in", "w_ret_o", "w_sc_o", "w_cf_o", "w_o"),
                 "ffn2": ("ffn2_w_gu", "ffn2_w_down")}
BIG = BLOCK_WEIGHTS["ffn1"] + BLOCK_WEIGHTS["mixer"] + BLOCK_WEIGHTS["ffn2"]
MODE = {"ffn1_w_gu": "col", "ffn1_w_down": "row", "w_in": "col", "w_ret_o": "row", "w_sc_o": "row",
        "w_cf_o": "row", "w_o": "row", "ffn2_w_gu": "col", "ffn2_w_down": "row"}
NORM_OF = {"ffn1": 0, "mixer": 2, "ffn2": 4}
BLK_GATE, BLK_SCB, BLK_CFA, BLK_MERGE = 2, 3, 6, 8


def _rope_tables(positions):
    half = RET_QK_DIM // 2
    inv_freq = ROPE_BASE ** (-jnp.arange(half, dtype=F32) / half)
    ang = positions.astype(F32)[..., None] * inv_freq
    cos, sin = jnp.cos(ang), jnp.sin(ang)
    nb, s = positions.shape
    cos2 = jnp.concatenate([cos, cos], axis=-1).reshape(nb * s, RET_QK_DIM)
    sin2 = jnp.concatenate([-sin, sin], axis=-1).reshape(nb * s, RET_QK_DIM)
    return cos2, sin2


def _log_gamma():
    lg = jnp.log(1.0 - 2.0 ** (-5.0 - jnp.arange(RET_HEADS, dtype=F32)))
    return jnp.broadcast_to(lg[:, None, None], (RET_HEADS, 1, RET_QK_DIM))


def _ffn_fwd(xs, w, tag, g_pre, g_post):
    h = rms_fwd("ffn_rms", xs, g_pre)
    gu, a = ffn_up("ffn_up", h, w[tag + "_w_gu"])
    y = mm_fwd("ffn_down", a, w[tag + "_w_down"], "row", F32)
    out = post_fwd("ffn_post", xs, y, g_post, 0.5)
    return out, dict(x=xs, h=h, gu=gu, a=a, y=y, w=w)


def _ffn_bwd(dxs, sv, tag, g_pre, g_post):
    w = sv["w"]
    gu_w, down_w = w[tag + "_w_gu"], w[tag + "_w_down"]
    dy, dg_post = post_bwd("ffn_post_bwd", sv["y"], g_post, dxs, 0.5)
    dgu = ffn_down_dx("ffn_down_dx", dy, down_w, sv["gu"])
    grads = {tag + "_w_down": mm_dw("ffn_down_dw", sv["a"], dy, "row", down_w.shape)}
    dh = mm_dx("ffn_gu_dx", dgu, gu_w, "col", F32)
    grads[tag + "_w_gu"] = mm_dw("ffn_gu_dw", sv["h"], dgu, "col", gu_w.shape)
    dxs, dg_pre = rms_bwd("ffn_rms_bwd", sv["x"], g_pre, dh, dxs)
    return dxs, grads, dg_pre, dg_post


def _mixer_fwd(xs, w, sm, g_pre, g_post, rope, nb, s, mid):
    cos2, sin2, log_g = rope
    d = xs.shape[1]
    gate_blk = (BLK_GATE * d) // RET_V_DIM
    h = rms_fwd("mx_rms", xs, g_pre)
    p = mm_fwd("mx_in", h, w["w_in"], "col", MXU_DTYPE)
    if mid is not None:
        sm = dict(sm, cf_dw_b=sm["cf_dw_b"] + mid(p))
    o = retention_fwd("ret_fwd", p, cos2, sin2, log_g, nb, s)
    ya_in = head_gate_fwd("ret_gate", o, p, gate_blk)
    yb_in, cz = short_conv_fwd("sc_fwd", p, BLK_SCB, sm["sc_conv_w"], nb)
    u1 = conformer_conv_fwd("cf_fwd", p, BLK_CFA, sm["cf_dw_w"], sm["cf_dw_b"], nb)
    yc_in = ln_silu_fwd("cf_ln", u1, sm["cf_ln_g"], sm["cf_ln_b"])
    ya = mm_fwd("mx_proj", ya_in, w["w_ret_o"], "row", F32)
    yb = mm_fwd("mx_proj", yb_in, w["w_sc_o"], "row", F32)
    yc = mm_fwd("mx_proj", yc_in, w["w_cf_o"], "row", F32)
    mg = merge_fwd("mx_merge", p, BLK_MERGE, ya, yb, yc)
    m = mm_fwd("mx_proj", mg, w["w_o"], "row", F32)
    out = post_fwd("mx_post", xs, m, g_post, 1.0)
    return out, dict(x=xs, h=h, p=p, o=o, ya_in=ya_in, yb_in=yb_in, cz=cz, u1=u1, yc_in=yc_in, ya=ya, yb=yb, yc=yc,
                     mg=mg, m=m, w=w)


def _mixer_bwd(dxs, sv, sm, g_pre, g_post, rope, nb, s):
    cos2, sin2, log_g = rope
    w, p = sv["w"], sv["p"]
    d = dxs.shape[1]
    gate_blk = (BLK_GATE * d) // RET_V_DIM
    grads, gsm = {}, {}

    def proj_bwd(wname, a_in, dy, out_dtype):
        grads[wname] = mm_dw("mx_proj_dw", a_in, dy, "row", w[wname].shape)
        return mm_dx("mx_proj_dx", dy, w[wname], "row", out_dtype)

    dm, dg_post = post_bwd("mx_post_bwd", sv["m"], g_post, dxs, 1.0)
    dmg = proj_bwd("w_o", sv["mg"], dm, MXU_DTYPE)
    dg0, dg1, dg2, dya, dyb, dyc = merge_bwd("mx_merge_bwd", p, BLK_MERGE, sv["ya"], sv["yb"], sv["yc"], dmg)
    dya_in = proj_bwd("w_ret_o", sv["ya_in"], dya, MXU_DTYPE)
    dyb_in = proj_bwd("w_sc_o", sv["yb_in"], dyb, MXU_DTYPE)
    dyc_in = proj_bwd("w_cf_o", sv["yc_in"], dyc, MXU_DTYPE)
    do, dgret = head_gate_bwd("ret_gate_bwd", sv["o"], p, gate_blk, dya_in)
    dq, dk, dv = retention_bwd("ret_bwd", p, cos2, sin2, log_g, do, nb, s)
    dscb, dscc, dscx, gsm["sc_conv_w"] = short_conv_bwd("sc_bwd", p, BLK_SCB, sm["sc_conv_w"], sv["cz"], dyb_in, nb)
    du1, dlg, dlb = ln_silu_bwd("cf_ln_bwd", sv["u1"], sm["cf_ln_g"], sm["cf_ln_b"], dyc_in)
    dcfa, dcfb, gsm["cf_dw_w"], dbias = conformer_conv_bwd("cf_bwd", p, BLK_CFA, sm["cf_dw_w"], du1, nb)
    gsm.update(cf_ln_g=dlg[0], cf_ln_b=dlb[0], cf_dw_b=dbias[0])
    dp = concat_cols("mx_dp", [dq, dk, dv, dgret, dscb, dscc, dscx, dcfa, dcfb, dg0, dg1, dg2])
    dh = mm_dx("mx_in_dx", dp, w["w_in"], "col", F32)
    grads["w_in"] = mm_dw("mx_in_dw", sv["h"], dp, "col", w["w_in"].shape)
    dxs, dg_pre = rms_bwd("mx_rms_bwd", sv["x"], g_pre, dh, dxs)
    return dxs, grads, gsm, dg_pre, dg_post


def local_step(x, positions, target, small, fetch, push):
    nb, s, d = x.shape
    t = nb * s
    depth = small["norm_g"].shape[0]
    rope = _rope_tables(positions) + (_log_gamma(),)
    xs = x.reshape(t, d)
    token = [None]

    def gain(l, i):
        g = small["norm_g"][l, i][None, :]
        if token[0] is not None:
            g, token[0] = g + token[0], None
        return g

    def mixer_small(l):
        return dict(sc_conv_w=small["sc_conv_w"][l], cf_dw_w=small["cf_dw_w"][l], cf_dw_b=small["cf_dw_b"][l][None, :],
                    cf_ln_g=small["cf_ln_g"][l][None, :], cf_ln_b=small["cf_ln_b"][l][None, :])

    saved = {}
    for l in range(depth):
        for blk in BLOCKS:
            w, token[0], mid = fetch(l, blk, xs)
            i0 = NORM_OF[blk]
            if blk == "mixer":
                xs, saved[l, blk] = _mixer_fwd(xs, w, mixer_small(l), gain(l, i0), gain(l, i0 + 1), rope, nb, s, mid)
            else:
                xs, saved[l, blk] = _ffn_fwd(xs, w, blk, gain(l, i0), gain(l, i0 + 1))

    dxs, loss = loss_head("loss", xs, target.reshape(t, d))

    dnorm = [[None] * 6 for _ in range(depth)]
    gsmall = {n: [None] * depth for n in ("sc_conv_w", "cf_dw_w", "cf_dw_b", "cf_ln_g", "cf_ln_b")}
    for l in reversed(range(depth)):
        for blk in reversed(BLOCKS):
            i0 = NORM_OF[blk]
            g_post, g_pre = gain(l, i0 + 1), gain(l, i0)
            if blk == "mixer":
                dxs, grads, gsm, dnorm[l][i0], dnorm[l][i0 + 1] = _mixer_bwd(
                    dxs, saved[l, blk], mixer_small(l), g_pre, g_post, rope, nb, s)
                for n, v in gsm.items():
                    gsmall[n][l] = v
            else:
                dxs, grads, dnorm[l][i0], dnorm[l][i0 + 1] = _ffn_bwd(dxs, saved[l, blk], blk, g_pre, g_post)
            token[0] = push(l, blk, grads)

    gs = {n: jnp.stack(v) for n, v in gsmall.items()}
    gs["norm_g"] = jnp.stack([jnp.concatenate(r, axis=0) for r in dnorm])
    return loss, dxs.reshape(nb, s, d), gs


ANY = pl.BlockSpec(memory_space=pl.ANY)
HBM = pl.BlockSpec(memory_space=pltpu.HBM)
SEM = pl.BlockSpec(memory_space=pltpu.SEMAPHORE)
VMEM_WHOLE = pl.BlockSpec(memory_space=pltpu.VMEM)
EFFECT = pltpu.SideEffectType.DATAFLOW_SIDE_EFFECTING
TOKEN = jax.ShapeDtypeStruct((8, 128), F32)


def _other_chips(x, y):
    return [(1 - x, y), (x, 1 - y), (1 - x, 1 - y)]


def _remote(src, dst, send_sem, recv_sem, to):
    return pltpu.make_async_remote_copy(src_ref=src, dst_ref=dst, send_sem=send_sem, recv_sem=recv_sem,
                                        device_id=to, device_id_type=MESH)


def _in_hbm(v):
    return pltpu.with_memory_space_constraint(v, pltpu.HBM)


def place_quarters(ws, layer, ids, after):
    m = len(ws)

    def body(ids_ref, *refs):
        for w_ref, o_ref in zip(refs[:m], refs[m + 1:]):
            o_ref[...] = w_ref[...].astype(o_ref.dtype)

    def spec(w, where):
        return pl.BlockSpec((None, w.shape[1] // STREAM_STEPS, w.shape[2]), where)

    return pl.pallas_call(
        body, name="place_quarters",
        grid_spec=pltpu.PrefetchScalarGridSpec(
            num_scalar_prefetch=1, grid=(STREAM_STEPS,),
            in_specs=[spec(w, lambda i, ids_ref: (layer, i, 0)) for w in ws] + [ANY],
            out_specs=[spec(w, lambda i, ids_ref: (ids_ref[0], i, 0)) for w in ws]),
        out_shape=[jax.ShapeDtypeStruct((N_CHIP,) + w.shape[1:], MXU_DTYPE) for w in ws],
        compiler_params=_params(("parallel",)),
    )(ids, *ws, after)


def _gather_copies(lands, send, recv):
    x, y, c = _axes()
    me = 2 * x + y
    mine, theirs = [], []
    for a, ld in enumerate(lands):
        rh = ld.shape[1] // 2
        rows = pl.ds(c * rh, rh)
        for k, (px, py) in enumerate(_other_chips(x, y)):
            to = (px, py, c)
            mine.append(_remote(ld.at[me, rows, :], ld.at[me, rows, :], send.at[3 * a + k], recv.at[3 * a + k], to))
            got = ld.at[2 * px + py, rows, :]
            theirs.append(_remote(got, got, send.at[3 * a + k], recv.at[3 * a + k], to))
    return mine, theirs


def gather_start(name, groups, after):
    flat = [s for g in groups for s in g]
    n, ng = len(flat), len(groups)
    sizes = [len(g) for g in groups]

    def body(*refs):
        lands = refs[:n]
        sems = refs[n + 1:n + 1 + 2 * ng]
        token = refs[-1]
        at = 0
        for g, m in enumerate(sizes):
            mine, _ = _gather_copies(lands[at:at + m], sems[2 * g], sems[2 * g + 1])
            for cp in mine:
                cp.start()
            at += m
        token[...] = jnp.zeros_like(token)

    sem_shapes = []
    for m in sizes:
        sem_shapes += [pltpu.SemaphoreType.DMA((3 * m,))] * 2
    res = pl.pallas_call(
        body, name=name, in_specs=[HBM] * n + [ANY],
        out_specs=[SEM] * (2 * ng) + [HBM] * n + [VMEM_WHOLE],
        out_shape=sem_shapes + [pltpu.HBM(s.shape, s.dtype) for s in flat] + [TOKEN],
        input_output_aliases={i: 2 * ng + i for i in range(n)},
        compiler_params=pltpu.CompilerParams(has_side_effects=EFFECT),
    )(*[_in_hbm(s) for s in flat], after)
    sems, thru, token = res[:2 * ng], res[2 * ng:2 * ng + n], res[-1]
    out, at = [], 0
    for g, m in enumerate(sizes):
        out.append((sems[2 * g], sems[2 * g + 1], thru[at:at + m]))
        at += m
    return out, token


def gather_wait(lands, send, recv, after):
    m = len(lands)

    def body(*refs):
        mine, theirs = _gather_copies(refs[:m], refs[m], refs[m + 1])
        for cp in mine:
            cp.wait_send()
        for cp in theirs:
            cp.wait_recv()

    return pl.pallas_call(
        body, name="gather_wait", in_specs=[HBM] * m + [SEM, SEM, ANY], out_specs=[HBM] * m,
        out_shape=[pltpu.HBM(l.shape, l.dtype) for l in lands],
        input_output_aliases={i: i for i in range(m)},
        compiler_params=pltpu.CompilerParams(has_side_effects=EFFECT),
    )(*lands, send, recv, after)


def copy_start(name, bufs, copies, ncopy, after=()):
    n, k = len(bufs), len(after)

    def body(*refs):
        for cp in copies(refs[:n], refs[n + k], refs[n + k + 1])[0]:
            cp.start()
        refs[-1][...] = jnp.zeros_like(refs[-1])

    res = pl.pallas_call(
        body, name=name, in_specs=[HBM] * n + [ANY] * k, out_specs=[SEM, SEM] + [HBM] * n + [VMEM_WHOLE],
        out_shape=[pltpu.SemaphoreType.DMA((ncopy,))] * 2 + [pltpu.HBM(b.shape, b.dtype) for b in bufs] + [TOKEN],
        input_output_aliases={i: 2 + i for i in range(n)},
        compiler_params=pltpu.CompilerParams(has_side_effects=EFFECT),
    )(*[_in_hbm(b) for b in bufs], *after)
    return res[0], res[1], list(res[2:2 + n]), res[-1]


def copy_wait(name, bufs, send, recv, copies, after=()):
    n = len(bufs)

    def body(*refs):
        mine, theirs = copies(refs[:n], refs[n], refs[n + 1])
        for cp in mine:
            cp.wait_send()
        for cp in theirs:
            cp.wait_recv()

    return list(pl.pallas_call(
        body, name=name, in_specs=[HBM] * n + [SEM, SEM] + [ANY] * len(after), out_specs=[HBM] * n,
        out_shape=[pltpu.HBM(b.shape, b.dtype) for b in bufs], input_output_aliases={i: i for i in range(n)},
        compiler_params=pltpu.CompilerParams(has_side_effects=EFFECT),
    )(*bufs, send, recv, *after))


def _fill_copies(lands, send, recv):
    x, y, c = _axes()
    sib = (x, y, 1 - c)
    mine, theirs = [], []
    for a, ld in enumerate(lands):
        rh = ld.shape[1] // 2
        for k, (px, py) in enumerate(_other_chips(x, y)):
            got = ld.at[2 * px + py, pl.ds(c * rh, rh), :]
            mine.append(_remote(got, got, send.at[3 * a + k], recv.at[3 * a + k], sib))
            blk = ld.at[2 * px + py, pl.ds((1 - c) * rh, rh), :]
            theirs.append(_remote(blk, blk, send.at[3 * a + k], recv.at[3 * a + k], sib))
    return mine, theirs


def _presum_copies(grads, lands, send, recv):
    x, y, c = _axes()
    cps = []
    for a, (g, ld) in enumerate(zip(grads, lands)):
        rh = g.shape[1] // 2
        cps.append(_remote(g.at[:, pl.ds((1 - c) * rh, rh), :], ld, send.at[a], recv.at[a], (x, y, 1 - c)))
    return cps


def presum_start(grads):
    m = len(grads)

    def body(*refs):
        for cp in _presum_copies(refs[:m], refs[m:2 * m], refs[2 * m], refs[2 * m + 1]):
            cp.start()
        refs[-1][...] = jnp.zeros_like(refs[-1])

    lands = [lax.empty((g.shape[0], g.shape[1] // 2, g.shape[2]), g.dtype) for g in grads]
    res = pl.pallas_call(
        body, name="presum_start", in_specs=[HBM] * (2 * m), out_specs=[SEM, SEM] + [HBM] * (2 * m) + [VMEM_WHOLE],
        out_shape=[pltpu.SemaphoreType.DMA((m,))] * 2 + [pltpu.HBM(g.shape, g.dtype) for g in grads]
        + [pltpu.HBM(l.shape, l.dtype) for l in lands] + [TOKEN],
        input_output_aliases={i: 2 + i for i in range(2 * m)},
        compiler_params=pltpu.CompilerParams(has_side_effects=EFFECT),
    )(*[_in_hbm(g) for g in grads], *[_in_hbm(l) for l in lands])
    return res[0], res[1], res[2:2 + m], res[2 + m:2 + 2 * m], res[-1]


def presum_wait(grads, lands, send, recv, after):
    m = len(grads)

    def body(*refs):
        for cp in _presum_copies(refs[:m], refs[m:2 * m], refs[2 * m], refs[2 * m + 1]):
            cp.wait_send()
            cp.wait_recv()

    res = pl.pallas_call(
        body, name="presum_wait", in_specs=[HBM] * (2 * m) + [SEM, SEM] + [ANY] * len(after),
        out_specs=[HBM] * (2 * m),
        out_shape=[pltpu.HBM(g.shape, g.dtype) for g in grads] + [pltpu.HBM(l.shape, l.dtype) for l in lands],
        input_output_aliases={i: i for i in range(2 * m)},
        compiler_params=pltpu.CompilerParams(has_side_effects=EFFECT),
    )(*grads, *lands, send, recv, *after)
    return res[:m], res[m:]


def add_halves(gs, lands, ids):
    m = len(gs)

    def body(ids_ref, *refs):
        for a_ref, b_ref, o_ref in zip(refs[:m], refs[m:2 * m], refs[2 * m:]):
            o_ref[...] = (a_ref[...].astype(F32) + b_ref[...].astype(F32)).astype(o_ref.dtype)

    def spec(ld, where):
        return pl.BlockSpec((None,) + ld.shape[1:], where)

    return pl.pallas_call(
        body, name="add_halves",
        grid_spec=pltpu.PrefetchScalarGridSpec(
            num_scalar_prefetch=1, grid=(N_CHIP,),
            in_specs=[spec(ld, lambda i, ids_ref: (i, ids_ref[1], 0)) for ld in lands]
            + [spec(ld, lambda i, ids_ref: (i, 0, 0)) for ld in lands],
            out_specs=[spec(ld, lambda i, ids_ref: (i, 0, 0)) for ld in lands]),
        out_shape=[jax.ShapeDtypeStruct(ld.shape, ld.dtype) for ld in lands],
        compiler_params=_params(("parallel",)),
    )(ids, *gs, *lands)


def _scatter_copies(parts, lands, send, recv):
    x, y, c = _axes()
    cps = []
    for a, (pt, ld) in enumerate(zip(parts, lands)):
        for k, (px, py) in enumerate(_other_chips(x, y)):
            cps.append(_remote(pt.at[2 * px + py], ld.at[k], send.at[3 * a + k], recv.at[3 * a + k], (px, py, c)))
    return cps


def scatter_start(parts):
    m = len(parts)

    def body(*refs):
        for cp in _scatter_copies(refs[:m], refs[m:2 * m], refs[2 * m], refs[2 * m + 1]):
            cp.start()
        refs[-1][...] = jnp.zeros_like(refs[-1])

    lands = [lax.empty((N_CHIP - 1,) + p.shape[1:], p.dtype) for p in parts]
    res = pl.pallas_call(
        body, name="scatter_start", in_specs=[HBM] * (2 * m), out_specs=[SEM, SEM] + [HBM] * (2 * m) + [VMEM_WHOLE],
        out_shape=[pltpu.SemaphoreType.DMA((3 * m,))] * 2 + [pltpu.HBM(p.shape, p.dtype) for p in parts]
        + [pltpu.HBM(l.shape, l.dtype) for l in lands] + [TOKEN],
        input_output_aliases={i: 2 + i for i in range(2 * m)},
        compiler_params=pltpu.CompilerParams(has_side_effects=EFFECT),
    )(*[_in_hbm(p) for p in parts], *[_in_hbm(l) for l in lands])
    return res[0], res[1], res[2:2 + m], res[2 + m:2 + 2 * m], res[-1]


def scatter_wait(parts, lands, send, recv, after):
    m = len(parts)

    def body(*refs):
        for cp in _scatter_copies(refs[:m], refs[m:2 * m], refs[2 * m], refs[2 * m + 1]):
            cp.wait_send()
            cp.wait_recv()

    res = pl.pallas_call(
        body, name="scatter_wait", in_specs=[HBM] * (2 * m) + [SEM, SEM] + [ANY] * len(after),
        out_specs=[HBM] * (2 * m),
        out_shape=[pltpu.HBM(p.shape, p.dtype) for p in parts] + [pltpu.HBM(l.shape, l.dtype) for l in lands],
        input_output_aliases={i: i for i in range(2 * m)},
        compiler_params=pltpu.CompilerParams(has_side_effects=EFFECT),
    )(*parts, *lands, send, recv, *after)
    return res[:m], res[m:]


def sum_partials(parts, lands, ids, layer, depth, intos):
    m = len(parts)
    nt = STREAM_STEPS

    def body(ids_ref, *refs):
        for p_ref, l_ref, o_ref in zip(refs[:m], refs[m:2 * m], refs[-m:]):
            acc = p_ref[...].astype(F32)
            for k in range(N_CHIP - 1):
                acc = acc + l_ref[k].astype(F32)
            o_ref[...] = acc

    def rows(p):
        return p.shape[1] // nt

    in_specs = [pl.BlockSpec((None, rows(p), p.shape[2]), lambda i, ids_ref: (ids_ref[0], i, 0)) for p in parts]
    in_specs += [pl.BlockSpec((N_CHIP - 1, rows(p), p.shape[2]), lambda i, ids_ref: (0, i, 0)) for p in parts]
    args = [ids, *parts, *lands]
    aliases = {}
    if intos is not None:
        in_specs += [ANY] * m
        args += list(intos)
        aliases = {1 + 2 * m + a: a for a in range(m)}
    return pl.pallas_call(
        body, name="sum_partials",
        grid_spec=pltpu.PrefetchScalarGridSpec(
            num_scalar_prefetch=1, grid=(nt,), in_specs=in_specs,
            out_specs=[pl.BlockSpec((None, rows(p), p.shape[2]), lambda i, ids_ref: (layer, ids_ref[1] * nt + i, 0))
                       for p in parts]),
        out_shape=[jax.ShapeDtypeStruct((depth, 2 * p.shape[1], p.shape[2]), F32) for p in parts],
        input_output_aliases=aliases, compiler_params=_params(("parallel",)),
    )(*args)


def _final_copies(layer):
    def copies(bufs, send, recv):
        x, y, c = _axes()
        sib = (x, y, 1 - c)
        mine, theirs = [], []
        for a, buf in enumerate(bufs):
            rh = buf.shape[1] // 2
            src = buf.at[layer, pl.ds(c * rh, rh), :]
            mine.append(_remote(src, src, send.at[a], recv.at[a], sib))
            dst = buf.at[layer, pl.ds((1 - c) * rh, rh), :]
            theirs.append(_remote(dst, dst, send.at[a], recv.at[a], sib))
        return mine, theirs

    return copies


def allgather_small(pk):
    def body(in_ref, out_ref, send, recv):
        x, y, c = _axes()
        me = 2 * x + y
        chips = _other_chips(x, y)
        out_ref[pl.ds(me, 1)] = in_ref[...][None]
        cps = []
        for k, (px, py) in enumerate(chips):
            cp = _remote(in_ref, out_ref.at[me], send.at[k], recv.at[k], (px, py, c))
            cp.start()
            cps.append(cp)
        for k, (px, py) in enumerate(chips):
            got = out_ref.at[2 * px + py]
            _remote(got, got, send.at[k], recv.at[k], (px, py, c)).wait_recv()
        for cp in cps:
            cp.wait_send()

    return pl.pallas_call(
        body, name="allgather_small", in_specs=[VMEM_WHOLE], out_specs=VMEM_WHOLE,
        out_shape=jax.ShapeDtypeStruct((N_CHIP,) + pk.shape, pk.dtype),
        scratch_shapes=[pltpu.SemaphoreType.DMA((3,))] * 2,
    )(pk)


N_DEV = 8


def _small_copies(bufs, send, recv):
    g, slots = bufs
    x, y, c = _axes()
    me = 4 * x + 2 * y + c
    mine, theirs = [], []
    for mask in range(1, N_DEV):
        px = 1 - x if mask & 4 else x
        py = 1 - y if mask & 2 else y
        pc = 1 - c if mask & 1 else c
        mine.append(_remote(g, slots.at[me], send.at[mask - 1], recv.at[mask - 1], (px, py, pc)))
        got = slots.at[4 * px + 2 * py + pc]
        theirs.append(_remote(got, got, send.at[mask - 1], recv.at[mask - 1], (px, py, pc)))
    return mine, theirs


def sum_slots(g, slots, me):
    def body(me_ref, g_ref, slots_ref, o_ref):
        acc = None
        for d in range(N_DEV):
            term = jnp.where(me_ref[0] == d, g_ref[...], slots_ref[d])
            acc = term if acc is None else acc + term
        o_ref[...] = acc

    return pl.pallas_call(
        body, name="sum_slots",
        grid_spec=pltpu.PrefetchScalarGridSpec(
            num_scalar_prefetch=1, grid=(1,),
            in_specs=[pl.BlockSpec(g.shape, lambda i, me_ref: (0, 0)),
                      pl.BlockSpec(slots.shape, lambda i, me_ref: (0, 0, 0))],
            out_specs=pl.BlockSpec(g.shape, lambda i, me_ref: (0, 0))),
        out_shape=jax.ShapeDtypeStruct(g.shape, g.dtype),
        compiler_params=_params(("arbitrary",)),
    )(me, g, slots)


def adamw(w, g, m, v):
    shape = w.shape
    cols = shape[-1]
    rows = int(np.prod(shape[:-1]))
    tr = rows
    for cand in (256, 128):
        if rows % cand == 0 and cand * cols * 4 <= 2 * 1024 * 1024:
            tr = cand
            break
    c1 = 1.0 - ADAM_B1 ** ADAM_STEP
    c2 = 1.0 - ADAM_B2 ** ADAM_STEP

    def body(w_ref, g_ref, m_ref, v_ref, d_ref, nm_ref, nv_ref, g_out):
        gv = g_ref[...]
        g_out[...] = gv
        nm = ADAM_B1 * m_ref[...] + (1.0 - ADAM_B1) * gv
        nv = ADAM_B2 * v_ref[...] + (1.0 - ADAM_B2) * jnp.square(gv)
        d_ref[...] = -ADAM_LR * ((nm / c1) / (jnp.sqrt(nv / c2) + ADAM_EPS) + ADAM_WD * w_ref[...])
        nm_ref[...] = nm
        nv_ref[...] = nv

    spec = pl.BlockSpec((tr, cols), lambda i: (i, 0))
    res = pl.pallas_call(
        body, name="adamw", grid=(rows // tr,), in_specs=[spec] * 4, out_specs=[spec] * 4,
        out_shape=[jax.ShapeDtypeStruct((rows, cols), F32)] * 4, compiler_params=_params(("parallel",)),
    )(*[a.reshape(rows, cols) for a in (w, g, m, v)])
    return [r.reshape(shape) for r in res]


WEIGHTS = ("norm_g", "ffn1_w_gu", "ffn1_w_down", "w_in", "w_ret_o", "sc_conv_w", "w_sc_o", "cf_dw_w", "cf_dw_b",
           "cf_ln_g", "cf_ln_b", "w_cf_o", "w_o", "ffn2_w_gu", "ffn2_w_down")
SHARDED_SMALL = ("norm_g", "sc_conv_w", "cf_dw_w")
REPLICATED_SMALL = ("cf_dw_b", "cf_ln_g", "cf_ln_b")

def _pack_rows(parts):
    padded, offs, at = [], [], 0
    for p in parts:
        r = -(-p.shape[0] // SUBLANES) * SUBLANES
        padded.append(jnp.pad(p, ((0, r - p.shape[0]), (0, 0))))
        offs.append(at)
        at += r
    return jnp.concatenate(padded, axis=0), offs


def kernel(x, positions, norm_g, ffn1_w_gu, ffn1_w_down, w_in, w_ret_o, sc_conv_w, w_sc_o, cf_dw_w, cf_dw_b, cf_ln_g, cf_ln_b, w_cf_o, w_o, ffn2_w_gu, ffn2_w_down, loss_target, m_norm_g, m_ffn1_w_gu, m_ffn1_w_down, m_w_in, m_w_ret_o, m_sc_conv_w, m_w_sc_o, m_cf_dw_w, m_cf_dw_b, m_cf_ln_g, m_cf_ln_b, m_w_cf_o, m_w_o, m_ffn2_w_gu, m_ffn2_w_down, v_norm_g, v_ffn1_w_gu, v_ffn1_w_down, v_w_in, v_w_ret_o, v_sc_conv_w, v_w_sc_o, v_cf_dw_w, v_cf_dw_b, v_cf_ln_g, v_cf_ln_b, v_w_cf_o, v_w_o, v_ffn2_w_gu, v_ffn2_w_down):
    wts = dict(zip(WEIGHTS, (norm_g, ffn1_w_gu, ffn1_w_down, w_in, w_ret_o, sc_conv_w, w_sc_o, cf_dw_w, cf_dw_b,
                             cf_ln_g, cf_ln_b, w_cf_o, w_o, ffn2_w_gu, ffn2_w_down)))
    mom = dict(zip(WEIGHTS, (m_norm_g, m_ffn1_w_gu, m_ffn1_w_down, m_w_in, m_w_ret_o, m_sc_conv_w, m_w_sc_o,
                             m_cf_dw_w, m_cf_dw_b, m_cf_ln_g, m_cf_ln_b, m_w_cf_o, m_w_o, m_ffn2_w_gu, m_ffn2_w_down)))
    var = dict(zip(WEIGHTS, (v_norm_g, v_ffn1_w_gu, v_ffn1_w_down, v_w_in, v_w_ret_o, v_sc_conv_w, v_w_sc_o,
                             v_cf_dw_w, v_cf_dw_b, v_cf_ln_g, v_cf_ln_b, v_w_cf_o, v_w_o, v_ffn2_w_gu, v_ffn2_w_down)))
    depth = norm_g.shape[0]
    dq = norm_g.shape[-1]
    d = N_CHIP * dq
    chip = 2 * lax.axis_index("x") + lax.axis_index("y")
    ids = jnp.stack([chip, lax.axis_index("c")]).astype(jnp.int32)

    pk, offs = _pack_rows([wts[n].reshape(-1, dq) for n in SHARDED_SMALL])
    gk4 = allgather_small(pk)
    gk = gk4.transpose(1, 0, 2).reshape(pk.shape[0], d)
    small = {n: wts[n] for n in REPLICATED_SMALL}
    for n, o in zip(SHARDED_SMALL, offs):
        rows = wts[n].shape[0] * wts[n].shape[1]
        small[n] = gk[o:o + rows].reshape(wts[n].shape[:2] + (d,))

    order = [(l, blk) for l in range(depth) for blk in BLOCKS]
    def placed(groups, after):
        return [place_quarters([wts[n] for n in BLOCK_WEIGHTS[blk]], l, ids, after) for l, blk in groups]

    first, token = gather_start("gather_start_first", placed(order[:1], gk4), gk4)
    rest, token = gather_start("gather_start_rest", placed(order[1:], token), token)
    started = dict(zip(order, first + rest))
    small["norm_g"] = small["norm_g"] + token[0:1, 0:1]

    filling = {}

    def fill(group, after):
        send, recv, lands = started[group]
        lands = gather_wait(lands, send, recv, after)
        send, recv, lands, tok = copy_start("fill_start", lands, _fill_copies, 3 * len(lands))
        filling[group] = (send, recv, lands)
        return tok[0:1, 0:1]

    def fetch(l, blk, after):
        at = order.index((l, blk))
        if (l, blk) not in filling:
            fill((l, blk), token if at == 0 else after)
        send, recv, lands = filling.pop((l, blk))
        lands = copy_wait("fill_wait", lands, send, recv, _fill_copies, (after,))
        tok, mid = None, None
        if at == 1:
            mid = functools.partial(fill, order[at + 1])
        elif 1 < at < len(order) - 1:
            tok = fill(order[at + 1], lands[0])
        return dict(zip(BLOCK_WEIGHTS[blk], lands)), tok, mid

    gsum = {n: None for n in BIG}
    presums, scatters, finals = [], [], []

    def scatter_next(after):
        group, gl, lands, send, recv = presums.pop(0)
        gl, lands = presum_wait(gl, lands, send, recv, after)
        send, recv, parts, lands, tok = scatter_start(add_halves(gl, lands, ids))
        scatters.append((group, parts, lands, send, recv))
        return tok

    def sum_next(after):
        (l, blk), parts, lands, send, recv = scatters.pop(0)
        parts, lands = scatter_wait(parts, lands, send, recv, after)
        names = BLOCK_WEIGHTS[blk]
        intos = None if gsum[names[0]] is None else [gsum[n] for n in names]
        sums = sum_partials(parts, lands, ids, l, depth, intos)
        send, recv, sums, tok = copy_start("final_start", sums, _final_copies(l), len(sums))
        gsum.update(zip(names, sums))
        finals.append((names, l, send, recv))
        return tok

    def final_next(after):
        names, l, send, recv = finals.pop(0)
        gsum.update(zip(names, copy_wait("final_wait", [gsum[n] for n in names], send, recv, _final_copies(l), after)))

    def push(l, blk, grads):
        send, recv, gl, lands, tok = presum_start([grads[n] for n in BLOCK_WEIGHTS[blk]])
        if scatters:
            tok = tok + sum_next((gl[0],))
        if presums:
            tok = tok + scatter_next((gl[0],))
        presums.append(((l, blk), gl, lands, send, recv))
        return tok[0:1, 0:1]

    loss, grad_x, gs = local_step(x, positions, loss_target, small, fetch, push)

    names = SHARDED_SMALL + REPLICATED_SMALL
    pg, offs = _pack_rows([gs[n].reshape(-1, d) for n in names])
    s_send, s_recv, s_bufs, tok = copy_start("small_start", [pg, lax.empty((N_DEV,) + pg.shape, pg.dtype)],
                                             _small_copies, N_DEV - 1, (grad_x,))
    tok = sum_next((scatter_next((grad_x, tok)),))
    while len(finals) > 1:
        final_next((tok,))
    pg, slots = copy_wait("small_wait", s_bufs, s_send, s_recv, _small_copies, (tok,))
    me = (2 * chip + lax.axis_index("c")).astype(jnp.int32).reshape(1)
    tot = sum_slots(pg, slots, me)
    grads = {}
    for n, o in zip(names, offs):
        rows = int(np.prod(gs[n].shape[:-1]))
        full = tot[o:o + rows]
        if n in SHARDED_SMALL:
            full = lax.dynamic_slice_in_dim(full, chip * dq, dq, axis=1)
        grads[n] = full.reshape(wts[n].shape)

    last = BLOCK_WEIGHTS[order[0][1]]
    delta, new_m, new_v = {}, {}, {}

    def update(n):
        g = gsum[n] if n in BIG else grads[n]
        delta[n], new_m[n], new_v[n], grads[n] = adamw(wts[n], g, mom[n], var[n])

    for n in names:
        update(n)
    final_next(tuple(delta[n] for n in names))
    for n in BIG:
        if n not in last:
            update(n)
    final_next((sum_next(tuple(delta[n] for n in BIG if n not in last)),))
    for n in last:
        update(n)

    loss_all = lax.psum(loss[0, 0], ("x", "y", "c"))
    return (loss_all, grad_x, *[grads[n] for n in WEIGHTS], *[delta[n] for n in WEIGHTS],
            *[new_m[n] for n in WEIGHTS], *[new_v[n] for n in WEIGHTS])
```

```python
import functools

import jax
import jax.numpy as jnp
import numpy as np
from jax import lax
from jax.experimental import pallas as pl
from jax.experimental.pallas import tpu as pltpu

F32 = jnp.float32
BF16 = jnp.bfloat16
MXU_DTYPE = BF16
VMEM_LIMIT_BYTES = 56 * 1024 * 1024
MESH = pl.DeviceIdType.MESH

N_CHIP = 4
CHUNK = 64
RET_HEADS = 4
RET_QK_DIM = 128
RET_V_DIM = 256
SC_KERNEL = 3
CF_KERNEL = 31
ROPE_BASE = 10000.0
NORM_EPS = 1e-6
LN_EPS = 1e-5
ADAM_LR = 0.001
ADAM_B1 = 0.9
ADAM_B2 = 0.999
ADAM_EPS = 1e-08
ADAM_WD = 0.01
ADAM_STEP = 10

SUBLANES = 8
CONV_PAD = 32
CONV_TS = 128
CONV_TC = 512
CONV_ROWS = 16
CONV_SCRATCH = [pltpu.VMEM((CONV_TS + CONV_PAD, CONV_TC), F32),
                pltpu.VMEM((SUBLANES - 1, CONV_TS + CONV_PAD - SUBLANES, CONV_TC), F32)]
RET_TQ = 512
MM_TM = 1024
MM_TN = 1536
MM_K1 = 1024
MM_W1 = 8 << 20
MM_SLICE = 256
MM_IN_BYTES = 36 << 20
STREAM_STEPS = 2


def _params(sem):
    return pltpu.CompilerParams(dimension_semantics=sem, vmem_limit_bytes=VMEM_LIMIT_BYTES)


def _axes():
    return lax.axis_index("x"), lax.axis_index("y"), lax.axis_index("c")


NN = (((1,), (0,)), ((), ()))
NT = (((1,), (1,)), ((), ()))
TN = (((0,), (0,)), ((), ()))


def _mm(name, a, b, out_shape, out_dtype, grid, a_spec, b_spec, o_spec, dims, acc_shape):
    nk = grid[2]

    def body(a_ref, b_ref, o_ref, *scratch):
        bv = b_ref[...]
        if bv.ndim == 3:
            bv = bv.reshape(-1, bv.shape[-1])
        part = lax.dot_general(a_ref[...], bv, dims, preferred_element_type=F32)

        def put(v):
            o_ref[...] = v.reshape(o_ref.shape).astype(o_ref.dtype)

        if nk == 1:
            put(part)
        else:
            acc = scratch[0]
            k = pl.program_id(2)

            @pl.when(k == 0)
            def _():
                acc[...] = part

            @pl.when(k > 0)
            def _():
                acc[...] += part

            @pl.when(k == nk - 1)
            def _():
                put(acc[...])

    scratch = [pltpu.VMEM(acc_shape, F32)] if nk > 1 else []
    return pl.pallas_call(
        body, name=name, grid=grid, in_specs=[a_spec, b_spec], out_specs=o_spec,
        out_shape=jax.ShapeDtypeStruct(out_shape, out_dtype), scratch_shapes=scratch,
        compiler_params=_params(("parallel", "parallel", "arbitrary")),
    )(a, b)


def _tile(n, target):
    best = None
    for t in range(128, min(n, target) + 1, 128):
        if n % t == 0:
            best = t
    assert best is not None, (n, target)
    return best


def _token_rows(t, width):
    tt = t
    while tt > MM_TM and tt * width * jnp.dtype(MXU_DTYPE).itemsize * 2 > MM_IN_BYTES:
        tt //= 2
    return tt


def mm_fwd(name, a, w4, mode, out_dtype):
    t = a.shape[0]
    _, r, c = w4.shape
    tm = min(t, MM_TM)
    if mode == "col":
        tn = _tile(c, MM_TN)
        npj = c // tn
        grid = (t // tm, N_CHIP * npj, 1)
        a_spec = pl.BlockSpec((tm, r), lambda i, j, k: (i, 0))
        b_spec = pl.BlockSpec((None, r, tn), lambda i, j, k: (j // npj, 0, j % npj))
        o_spec = pl.BlockSpec((tm, tn), lambda i, j, k: (i, j))
        return _mm(name, a, w4, (t, N_CHIP * c), out_dtype, grid, a_spec, b_spec, o_spec, NN, (tm, tn))
    if w4.size * w4.dtype.itemsize <= MM_W1:
        grid = (t // tm, 1, 1)
        a_spec = pl.BlockSpec((tm, N_CHIP * r), lambda i, j, k: (i, 0))
        b_spec = pl.BlockSpec((N_CHIP, r, c), lambda i, j, k: (0, 0, 0))
        o_spec = pl.BlockSpec((tm, c), lambda i, j, k: (i, 0))
        return _mm(name, a, w4, (t, c), out_dtype, grid, a_spec, b_spec, o_spec, NN, (tm, c))
    grid = (t // tm, 1, N_CHIP)
    a_spec = pl.BlockSpec((tm, r), lambda i, j, k: (i, k))
    b_spec = pl.BlockSpec((None, r, c), lambda i, j, k: (k, 0, 0))
    o_spec = pl.BlockSpec((tm, c), lambda i, j, k: (i, 0))
    return _mm(name, a, w4, (t, c), out_dtype, grid, a_spec, b_spec, o_spec, NN, (tm, c))


def mm_dx(name, dy, w4, mode, out_dtype):
    t = dy.shape[-2]
    _, r, c = w4.shape
    tm = min(t, MM_TM)
    if mode == "col":
        tn, npj = c, 1
        hb = N_CHIP // 2 * npj
        grid = (t // tm, 1, N_CHIP * npj)
        if dy.ndim == 3:
            a_spec = pl.BlockSpec((None, tm, tn), lambda i, j, k: (k // hb, i, k % hb))
        else:
            a_spec = pl.BlockSpec((tm, tn), lambda i, j, k: (i, k))
        b_spec = pl.BlockSpec((None, r, tn), lambda i, j, k: (k // npj, 0, k % npj))
        o_spec = pl.BlockSpec((tm, r), lambda i, j, k: (i, 0))
        return _mm(name, dy, w4, (t, r), out_dtype, grid, a_spec, b_spec, o_spec, NT, (tm, r))
    if N_CHIP * r <= MM_K1:
        grid = (t // tm, 1, 1)
        a_spec = pl.BlockSpec((tm, c), lambda i, j, k: (i, 0))
        b_spec = pl.BlockSpec((N_CHIP, r, c), lambda i, j, k: (0, 0, 0))
        o_spec = pl.BlockSpec((tm, N_CHIP * r), lambda i, j, k: (i, 0))
        return _mm(name, dy, w4, (t, N_CHIP * r), out_dtype, grid, a_spec, b_spec, o_spec, NT, (tm, N_CHIP * r))
    grid = (t // tm, N_CHIP, 1)
    a_spec = pl.BlockSpec((tm, c), lambda i, j, k: (i, 0))
    b_spec = pl.BlockSpec((None, r, c), lambda i, j, k: (j, 0, 0))
    o_spec = pl.BlockSpec((tm, r), lambda i, j, k: (i, j))
    return _mm(name, dy, w4, (t, N_CHIP * r), out_dtype, grid, a_spec, b_spec, o_spec, NT, (tm, r))


def mm_dw(name, a, dy, mode, shape3):
    t = a.shape[0]
    _, r, c = shape3
    if mode == "col":
        tn = _tile(c, MM_TN)
        npj = c // tn
        tt = _token_rows(t, r + tn)
        grid = (1, N_CHIP * npj, t // tt)
        a_spec = pl.BlockSpec((tt, r), lambda i, j, k: (k, 0))
        hb = N_CHIP // 2 * npj
        if dy.ndim == 3:
            b_spec = pl.BlockSpec((None, tt, tn), lambda i, j, k: (j // hb, k, j % hb))
        else:
            b_spec = pl.BlockSpec((tt, tn), lambda i, j, k: (k, j))
        o_spec = pl.BlockSpec((None, r, tn), lambda i, j, k: (j // npj, 0, j % npj))
        return _mm(name, a, dy, shape3, MXU_DTYPE, grid, a_spec, b_spec, o_spec, TN, (r, tn))
    if N_CHIP * r <= MM_K1:
        tt = _token_rows(t, N_CHIP * r + c)
        grid = (1, 1, t // tt)
        a_spec = pl.BlockSpec((tt, N_CHIP * r), lambda i, j, k: (k, 0))
        b_spec = pl.BlockSpec((tt, c), lambda i, j, k: (k, 0))
        o_spec = pl.BlockSpec((N_CHIP, r, c), lambda i, j, k: (0, 0, 0))
        return _mm(name, a, dy, shape3, MXU_DTYPE, grid, a_spec, b_spec, o_spec, TN, (N_CHIP * r, c))
    tt = _token_rows(t, r + c)
    grid = (N_CHIP, 1, t // tt)
    a_spec = pl.BlockSpec((tt, r), lambda i, j, k: (k, i))
    b_spec = pl.BlockSpec((tt, c), lambda i, j, k: (k, 0))
    o_spec = pl.BlockSpec((None, r, c), lambda i, j, k: (i, 0, 0))
    return _mm(name, a, dy, shape3, MXU_DTYPE, grid, a_spec, b_spec, o_spec, TN, (r, c))


def _rowwise(name, fn, rows, pars, outs, accs=(), tm=256, ncol=1):
    t = rows[0][0].shape[0]
    nrow, npar, nout = len(rows), len(pars), len(outs)

    def body(*refs):
        vals = [r[...] for r in refs[:nrow + npar]]
        res = fn(*vals)
        out_refs = refs[nrow + npar:nrow + npar + nout]
        acc_refs = refs[nrow + npar + nout:]
        for o, v in zip(out_refs, res[:nout]):
            o[...] = v.astype(o.dtype)
        i = pl.program_id(1)
        for a, v in zip(acc_refs, res[nout:]):
            @pl.when(i == 0)
            def _(a=a, v=v):
                a[...] = v.astype(F32)

            @pl.when(i > 0)
            def _(a=a, v=v):
                a[...] += v.astype(F32)

    in_specs = [pl.BlockSpec((tm, w), functools.partial(lambda j, i, b: (i, b + j), b=b)) for _, w, b in rows]
    for arr, w in pars:
        if w is None:
            in_specs.append(pl.BlockSpec(arr.shape, lambda j, i: (0, 0)))
        else:
            in_specs.append(pl.BlockSpec((1, w), lambda j, i: (0, j)))
    out_specs = [pl.BlockSpec((tm, w), lambda j, i: (i, j)) for _, w, _ in outs]
    out_specs += [pl.BlockSpec((1, w), lambda j, i: (0, j)) for _, w in accs]
    out_shape = [jax.ShapeDtypeStruct((t, tw), dt) for tw, _, dt in outs]
    out_shape += [jax.ShapeDtypeStruct((1, tw), F32) for tw, _ in accs]
    res = pl.pallas_call(
        body, name=name, grid=(ncol, t // tm), in_specs=in_specs, out_specs=out_specs, out_shape=out_shape,
        compiler_params=_params(("parallel", "arbitrary" if accs else "parallel")),
    )(*[r[0] for r in rows], *[p[0] for p in pars])
    return res


def _rms(x, g):
    xf = x.astype(F32)
    return xf * lax.rsqrt(jnp.mean(xf * xf, axis=-1, keepdims=True) + NORM_EPS) * g


def _silu(x):
    return x * jax.nn.sigmoid(x)


def rms_fwd(name, x, g):
    d = x.shape[1]
    return _rowwise(name, lambda x, g: (_rms(x, g),), [(x, d, 0)], [(g, None)], [(d, d, MXU_DTYPE)], tm=512)[0]


def rms_bwd(name, x, g, dh, dres):
    d = x.shape[1]

    def fn(x, dh, dres, g):
        _, vjp = jax.vjp(_rms, x, g)
        dx, dg = vjp(dh.astype(F32))
        return dres + dx, dg

    return _rowwise(name, fn, [(x, d, 0), (dh, d, 0), (dres, d, 0)], [(g, None)], [(d, d, F32)], [(d, d)], tm=256)


def post_fwd(name, x, y, g, scale):
    d = x.shape[1]
    return _rowwise(name, lambda x, y, g: (x + scale * _rms(y, g),), [(x, d, 0), (y, d, 0)], [(g, None)],
                    [(d, d, F32)], tm=512)[0]


def post_bwd(name, y, g, dx, scale):
    d = y.shape[1]

    def fn(y, dx, g):
        _, vjp = jax.vjp(lambda y, g: scale * _rms(y, g), y, g)
        return vjp(dx)

    return _rowwise(name, fn, [(y, d, 0), (dx, d, 0)], [(g, None)], [(d, d, MXU_DTYPE)], [(d, d)], tm=256)


def ffn_up(name, h, w4):
    t = h.shape[0]
    _, r, c = w4.shape
    tm = min(t, MM_TM)
    tn = _tile(c, MM_TM)
    npj = c // tn
    half = N_CHIP // 2

    def body(h_ref, wg_ref, wu_ref, gu_ref, a_ref):
        hv = h_ref[...]
        g = lax.dot_general(hv, wg_ref[...], NN, preferred_element_type=F32)
        u = lax.dot_general(hv, wu_ref[...], NN, preferred_element_type=F32)
        gu_ref[0] = g.astype(gu_ref.dtype)
        gu_ref[1] = u.astype(gu_ref.dtype)
        a_ref[...] = (_silu(g) * u).astype(a_ref.dtype)

    f = half * c
    return pl.pallas_call(
        body, name=name, grid=(t // tm, half * npj),
        in_specs=[pl.BlockSpec((tm, r), lambda i, j: (i, 0)),
                  pl.BlockSpec((None, r, tn), lambda i, j: (j // npj, 0, j % npj)),
                  pl.BlockSpec((None, r, tn), lambda i, j: (half + j // npj, 0, j % npj))],
        out_specs=[pl.BlockSpec((2, tm, tn), lambda i, j: (0, i, j)), pl.BlockSpec((tm, tn), lambda i, j: (i, j))],
        out_shape=[jax.ShapeDtypeStruct((2, t, f), MXU_DTYPE), jax.ShapeDtypeStruct((t, f), MXU_DTYPE)],
        compiler_params=_params(("parallel", "parallel")),
    )(h, w4, w4)


def ffn_down_dx(name, dy, w4, gu):
    t = dy.shape[0]
    _, r, c = w4.shape
    tm = min(t, MM_TM)

    def body(dy_ref, w_ref, gu_ref, o_ref):
        dyv = dy_ref[...]
        for n0 in range(0, r, MM_SLICE):
            cols = pl.ds(n0, MM_SLICE)
            da = lax.dot_general(dyv, w_ref[cols, :], NT, preferred_element_type=F32)
            gate, up = gu_ref[0, :, cols].astype(F32), gu_ref[1, :, cols].astype(F32)
            sg = jax.nn.sigmoid(gate)
            silu = gate * sg
            o_ref[0, :, cols] = (da * up * (sg + silu * (1.0 - sg))).astype(o_ref.dtype)
            o_ref[1, :, cols] = (da * silu).astype(o_ref.dtype)

    return pl.pallas_call(
        body, name=name, grid=(t // tm, N_CHIP),
        in_specs=[pl.BlockSpec((tm, c), lambda i, j: (i, 0)), pl.BlockSpec((None, r, c), lambda i, j: (j, 0, 0)),
                  pl.BlockSpec((2, tm, r), lambda i, j: (0, i, j))],
        out_specs=pl.BlockSpec((2, tm, r), lambda i, j: (0, i, j)),
        out_shape=jax.ShapeDtypeStruct((2, t, N_CHIP * r), MXU_DTYPE),
        compiler_params=_params(("parallel", "parallel")),
    )(dy, w4, gu)


def _head_gate(o, g):
    mu = jnp.mean(o, axis=-1, keepdims=True)
    var = jnp.mean(jnp.square(o - mu), axis=-1, keepdims=True)
    return _silu(g.astype(F32)) * ((o - mu) * lax.rsqrt(var + LN_EPS))


def head_gate_fwd(name, o, p, gate_blk):
    dv = RET_V_DIM
    return _rowwise(name, lambda o, g: (_head_gate(o, g),), [(o, dv, 0), (p, dv, gate_blk)], [],
                    [(RET_HEADS * dv, dv, MXU_DTYPE)], tm=512, ncol=RET_HEADS)[0]


def head_gate_bwd(name, o, p, gate_blk, da):
    dv = RET_V_DIM

    def fn(o, g, da):
        _, vjp = jax.vjp(_head_gate, o, g.astype(F32))
        return vjp(da.astype(F32))

    w = RET_HEADS * dv
    return _rowwise(name, fn, [(o, dv, 0), (p, dv, gate_blk), (da, dv, 0)], [],
                    [(w, dv, MXU_DTYPE), (w, dv, MXU_DTYPE)], tm=512, ncol=RET_HEADS)


def _ln_silu(u, g, b):
    mu = jnp.mean(u, axis=-1, keepdims=True)
    var = jnp.mean(jnp.square(u - mu), axis=-1, keepdims=True)
    return _silu((u - mu) * lax.rsqrt(var + LN_EPS) * g + b)


def ln_silu_fwd(name, u, g, b):
    d = u.shape[1]
    return _rowwise(name, lambda u, g, b: (_ln_silu(u, g, b),), [(u, d, 0)], [(g, None), (b, None)],
                    [(d, d, MXU_DTYPE)], tm=512)[0]


def ln_silu_bwd(name, u, g, b, dc):
    d = u.shape[1]

    def fn(u, dc, g, b):
        _, vjp = jax.vjp(_ln_silu, u, g, b)
        return vjp(dc.astype(F32))

    return _rowwise(name, fn, [(u, d, 0), (dc, d, 0)], [(g, None), (b, None)], [(d, d, F32)], [(d, d), (d, d)],
                    tm=256)


def _merge(g0, g1, g2, ya, yb, yc):
    s = jax.nn.sigmoid
    return s(g0.astype(F32)) * ya + s(g1.astype(F32)) * yb + s(g2.astype(F32)) * yc


def merge_fwd(name, p, blk, ya, yb, yc):
    d = ya.shape[1]
    rows = [(p, d, blk), (p, d, blk + 1), (p, d, blk + 2), (ya, d, 0), (yb, d, 0), (yc, d, 0)]
    return _rowwise(name, lambda *v: (_merge(*v),), rows, [], [(d, d, MXU_DTYPE)], tm=256)[0]


def merge_bwd(name, p, blk, ya, yb, yc, dmg):
    d = ya.shape[1]

    def fn(g0, g1, g2, ya, yb, yc, dmg):
        _, vjp = jax.vjp(_merge, g0.astype(F32), g1.astype(F32), g2.astype(F32), ya, yb, yc)
        return vjp(dmg.astype(F32))

    rows = [(p, d, blk), (p, d, blk + 1), (p, d, blk + 2), (ya, d, 0), (yb, d, 0), (yc, d, 0), (dmg, d, 0)]
    return _rowwise(name, fn, rows, [], [(d, d, MXU_DTYPE)] * 6, tm=256)


def concat_cols(name, pieces):
    t = pieces[0].shape[0]
    widths = [p.shape[1] for p in pieces]
    tm = 256

    def body(*refs):
        o_ref, at = refs[-1], 0
        for r, w in zip(refs[:-1], widths):
            o_ref[:, at:at + w] = r[...]
            at += w

    return pl.pallas_call(
        body, name=name, grid=(t // tm,),
        in_specs=[pl.BlockSpec((tm, w), lambda i: (i, 0)) for w in widths],
        out_specs=pl.BlockSpec((tm, sum(widths)), lambda i: (i, 0)),
        out_shape=jax.ShapeDtypeStruct((t, sum(widths)), pieces[0].dtype),
        compiler_params=_params(("parallel",)),
    )(*pieces)


def loss_head(name, y, target):
    t, d = y.shape
    tm = 512

    def body(y_ref, t_ref, dy_ref, loss_ref):
        err = y_ref[...] - t_ref[...]
        dy_ref[...] = err * (1.0 / d)
        part = jnp.sum(jnp.sum(err * err, axis=1, keepdims=True), axis=0, keepdims=True) * (0.5 / d)

        @pl.when(pl.program_id(0) == 0)
        def _():
            loss_ref[...] = part

        @pl.when(pl.program_id(0) > 0)
        def _():
            loss_ref[...] += part

    return pl.pallas_call(
        body, name=name, grid=(t // tm,),
        in_specs=[pl.BlockSpec((tm, d), lambda i: (i, 0))] * 2,
        out_specs=[pl.BlockSpec((tm, d), lambda i: (i, 0)), pl.BlockSpec((1, 1), lambda i: (0, 0))],
        out_shape=[jax.ShapeDtypeStruct((t, d), F32), jax.ShapeDtypeStruct((1, 1), F32)],
        compiler_params=_params(("arbitrary",)),
    )(y, target)


def _rot(x, cos2, sin2):
    return x * cos2 + pltpu.roll(x, RET_QK_DIM // 2, 1) * sin2


def _decay_mask(lg, n0, rows, cols):
    n = n0 + lax.broadcasted_iota(jnp.int32, (rows, cols), 0)
    m = lax.broadcasted_iota(jnp.int32, (rows, cols), 1)
    shift = CHUNK.bit_length() - 1
    dist = jnp.abs(n - m).astype(F32)
    return jnp.where((m >> shift) <= (n >> shift), jnp.exp(lg * dist), 0.0)


def _ret_specs(s):
    dk, dv, h = RET_QK_DIM, RET_V_DIM, RET_HEADS
    return [
        pl.BlockSpec((s, dk), lambda b, hh: (b, hh)),
        pl.BlockSpec((s, dk), lambda b, hh: (b, h + hh)),
        pl.BlockSpec((s, dv), lambda b, hh: (b, (2 * h * dk) // dv + hh)),
        pl.BlockSpec((s, dk), lambda b, hh: (b, 0)),
        pl.BlockSpec((s, dk), lambda b, hh: (b, 0)),
        pl.BlockSpec((None, 1, dk), lambda b, hh: (hh, 0, 0)),
    ]


def retention_fwd(name, p, cos2, sin2, log_g, nb, s):
    dk, dv, h = RET_QK_DIM, RET_V_DIM, RET_HEADS

    def body(q_ref, k_ref, v_ref, cos_ref, sin_ref, lg_ref, o_ref, kr_ref):
        lg = lg_ref[0:1, 0:1]
        kr = _rot(k_ref[...].astype(F32), cos_ref[...], sin_ref[...]) * (dk ** -0.5)
        kr_ref[...] = kr.astype(kr_ref.dtype)
        for qi in range(s // RET_TQ):
            n0, kmax = qi * RET_TQ, (qi + 1) * RET_TQ
            rows = pl.ds(n0, RET_TQ)
            qr = _rot(q_ref[rows, :].astype(F32), cos_ref[rows, :], sin_ref[rows, :]).astype(MXU_DTYPE)
            sc = lax.dot_general(qr, kr_ref[0:kmax, :], NT, preferred_element_type=F32)
            pm = (sc * _decay_mask(lg, n0, RET_TQ, kmax)).astype(MXU_DTYPE)
            o_ref[rows, :] = lax.dot_general(pm, v_ref[0:kmax, :], NN, preferred_element_type=F32)

    return pl.pallas_call(
        body, name=name, grid=(nb, h), in_specs=_ret_specs(s),
        out_specs=pl.BlockSpec((s, dv), lambda b, hh: (b, hh)),
        out_shape=jax.ShapeDtypeStruct((nb * s, h * dv), F32),
        scratch_shapes=[pltpu.VMEM((s, dk), MXU_DTYPE)],
        compiler_params=_params(("parallel", "parallel")),
    )(p, p, p, cos2, sin2, log_g)


def retention_bwd(name, p, cos2, sin2, log_g, do, nb, s):
    dk, dv, h = RET_QK_DIM, RET_V_DIM, RET_HEADS

    def body(q_ref, k_ref, v_ref, cos_ref, sin_ref, lg_ref, do_ref, dq_ref, dk_ref, dv_ref, kr_ref, dk_acc, dv_acc):
        lg = lg_ref[0:1, 0:1]
        kr = _rot(k_ref[...].astype(F32), cos_ref[...], sin_ref[...]) * (dk ** -0.5)
        kr_ref[...] = kr.astype(kr_ref.dtype)
        dk_acc[...] = jnp.zeros_like(dk_acc)
        dv_acc[...] = jnp.zeros_like(dv_acc)
        for qi in range(s // RET_TQ):
            n0, kmax = qi * RET_TQ, (qi + 1) * RET_TQ
            rows = pl.ds(n0, RET_TQ)
            cq, sq = cos_ref[rows, :], sin_ref[rows, :]
            qr = _rot(q_ref[rows, :].astype(F32), cq, sq).astype(MXU_DTYPE)
            dob = do_ref[rows, :]
            mask = _decay_mask(lg, n0, RET_TQ, kmax)
            sc = lax.dot_general(qr, kr_ref[0:kmax, :], NT, preferred_element_type=F32)
            pm = (sc * mask).astype(MXU_DTYPE)
            dv_acc[0:kmax, :] += lax.dot_general(pm, dob, TN, preferred_element_type=F32)
            dp = lax.dot_general(dob, v_ref[0:kmax, :], NT, preferred_element_type=F32)
            ds = (dp * mask).astype(MXU_DTYPE)
            dqr = lax.dot_general(ds, kr_ref[0:kmax, :], NN, preferred_element_type=F32)
            dq_ref[rows, :] = _rot(dqr, cq, -sq).astype(dq_ref.dtype)
            dk_acc[0:kmax, :] += lax.dot_general(ds, qr, TN, preferred_element_type=F32)
        dkr = dk_acc[...] * (dk ** -0.5)
        dk_ref[...] = _rot(dkr, cos_ref[...], -sin_ref[...]).astype(dk_ref.dtype)
        dv_ref[...] = dv_acc[...].astype(dv_ref.dtype)

    t = nb * s
    return pl.pallas_call(
        body, name=name, grid=(nb, h),
        in_specs=_ret_specs(s) + [pl.BlockSpec((s, dv), lambda b, hh: (b, hh))],
        out_specs=[pl.BlockSpec((s, dk), lambda b, hh: (b, hh)), pl.BlockSpec((s, dk), lambda b, hh: (b, hh)),
                   pl.BlockSpec((s, dv), lambda b, hh: (b, hh))],
        out_shape=[jax.ShapeDtypeStruct((t, h * dk), MXU_DTYPE), jax.ShapeDtypeStruct((t, h * dk), MXU_DTYPE),
                   jax.ShapeDtypeStruct((t, h * dv), MXU_DTYPE)],
        scratch_shapes=[pltpu.VMEM((s, dk), MXU_DTYPE), pltpu.VMEM((s, dk), F32), pltpu.VMEM((s, dv), F32)],
        compiler_params=_params(("parallel", "parallel")),
    )(p, p, p, cos2, sin2, log_g, do)


def _conv_grid(t, d, nb):
    s = t // nb
    ns, nc = s // CONV_TS, d // CONV_TC
    return s, ns, nc


def _shifted(pad_ref, sh_ref, offsets):
    n = sh_ref.shape[1]
    for b in sorted({off % SUBLANES for off in offsets} - {0}):
        sh_ref[b - 1] = pad_ref[pl.ds(b, n), :]

    def read(off, r0):
        a, b = off - off % SUBLANES + r0, off % SUBLANES
        return pad_ref[pl.ds(a, CONV_ROWS), :] if b == 0 else sh_ref[b - 1, pl.ds(a, CONV_ROWS), :]

    return read


def _causal_taps(pad_ref, sh_ref, w_ref, k, emit):
    offs = [CONV_PAD - (k - 1) + j for j in range(k)]
    read = _shifted(pad_ref, sh_ref, offs)
    for r0 in range(0, CONV_TS, CONV_ROWS):
        acc = None
        for j in range(k):
            term = w_ref[j:j + 1, :] * read(offs[j], r0)
            acc = term if acc is None else acc + term
        emit(r0, acc)


def _carry_past(pad_ref, s_idx):
    @pl.when(s_idx == 0)
    def _():
        pad_ref[0:CONV_PAD, :] = jnp.zeros((CONV_PAD, pad_ref.shape[1]), F32)

    @pl.when(s_idx > 0)
    def _():
        pad_ref[0:CONV_PAD, :] = pad_ref[CONV_TS:CONV_TS + CONV_PAD, :]


def _carry_future(pad_ref, s_idx):
    @pl.when(s_idx == 0)
    def _():
        pad_ref[CONV_TS:CONV_TS + CONV_PAD, :] = jnp.zeros((CONV_PAD, pad_ref.shape[1]), F32)

    @pl.when(s_idx > 0)
    def _():
        pad_ref[CONV_TS:CONV_TS + CONV_PAD, :] = pad_ref[0:CONV_PAD, :]


def _conv_bwd_taps(pad_ref, sh_ref, w_ref, dw_acc, k, x_rows, emit, mix):
    read = _shifted(pad_ref, sh_ref, range(k))
    for r0 in range(0, CONV_TS, CONV_ROWS):
        ops = x_rows(r0)
        x = mix(ops)
        acc = None
        for j in range(k):
            sh = read(k - 1 - j, r0)
            term = w_ref[j:j + 1, :] * sh
            acc = term if acc is None else acc + term
            prod = x * sh
            part = prod[0:SUBLANES]
            for q in range(SUBLANES, CONV_ROWS, SUBLANES):
                part = part + prod[q:q + SUBLANES]
            dw_acc[j] += part
        emit(r0, ops, acc)


def _conv_bwd_edges(dw_acc, dw_ref, nb, ns, extra=()):
    first = jnp.logical_and(pl.program_id(1) == 0, pl.program_id(2) == 0)
    last = jnp.logical_and(pl.program_id(1) == nb - 1, pl.program_id(2) == ns - 1)

    @pl.when(first)
    def _():
        dw_acc[...] = jnp.zeros_like(dw_acc)
        for r in extra:
            r[...] = jnp.zeros_like(r)

    def finish():
        @pl.when(last)
        def _():
            dw_ref[...] = jnp.sum(dw_acc[...], axis=1)

    return finish


def short_conv_fwd(name, p, blk_b, w, nb):
    t = p.shape[0]
    d = w.shape[1]
    s, ns, nc = _conv_grid(t, d, nb)
    cb = d // CONV_TC

    def body(b_ref, c_ref, x_ref, w_ref, y_ref, cz_ref, pad_ref, sh_ref):
        _carry_past(pad_ref, pl.program_id(2))
        pad_ref[CONV_PAD:CONV_PAD + CONV_TS, :] = c_ref[...].astype(F32) * x_ref[...].astype(F32)

        def emit(r0, cz):
            rows = pl.ds(r0, CONV_ROWS)
            cz_ref[rows, :] = cz
            y_ref[rows, :] = (b_ref[rows, :].astype(F32) * cz).astype(y_ref.dtype)

        _causal_taps(pad_ref, sh_ref, w_ref, SC_KERNEL, emit)

    def pspec(off):
        return pl.BlockSpec((CONV_TS, CONV_TC), lambda c, b, si: (b * ns + si, (blk_b + off) * cb + c))

    ospec = pl.BlockSpec((CONV_TS, CONV_TC), lambda c, b, si: (b * ns + si, c))
    return pl.pallas_call(
        body, name=name, grid=(nc, nb, ns),
        in_specs=[pspec(0), pspec(1), pspec(2), pl.BlockSpec((SC_KERNEL, CONV_TC), lambda c, b, si: (0, c))],
        out_specs=[ospec, ospec],
        out_shape=[jax.ShapeDtypeStruct((t, d), MXU_DTYPE), jax.ShapeDtypeStruct((t, d), F32)],
        scratch_shapes=CONV_SCRATCH,
        compiler_params=_params(("parallel", "arbitrary", "arbitrary")),
    )(p, p, p, w)


def short_conv_bwd(name, p, blk_b, w, cz, dy, nb):
    t = p.shape[0]
    d = w.shape[1]
    s, ns, nc = _conv_grid(t, d, nb)
    cb = d // CONV_TC

    def body(b_ref, c_ref, x_ref, w_ref, cz_ref, dy_ref, db_ref, dc_ref, dx_ref, dw_ref, pad_ref, sh_ref, dw_acc):
        _carry_future(pad_ref, pl.program_id(2))
        dyv = dy_ref[...].astype(F32)
        db_ref[...] = (dyv * cz_ref[...]).astype(db_ref.dtype)
        pad_ref[0:CONV_TS, :] = dyv * b_ref[...].astype(F32)
        finish = _conv_bwd_edges(dw_acc, dw_ref, nb, ns)

        def x_rows(r0):
            rows = pl.ds(r0, CONV_ROWS)
            return c_ref[rows, :].astype(F32), x_ref[rows, :].astype(F32)

        def emit(r0, cx, dz):
            rows = pl.ds(r0, CONV_ROWS)
            dc_ref[rows, :] = (dz * cx[1]).astype(dc_ref.dtype)
            dx_ref[rows, :] = (dz * cx[0]).astype(dx_ref.dtype)

        _conv_bwd_taps(pad_ref, sh_ref, w_ref, dw_acc, SC_KERNEL, x_rows, emit, lambda cx: cx[0] * cx[1])
        finish()

    def row(b, si):
        return b * ns + (ns - 1 - si)

    def pspec(off):
        return pl.BlockSpec((CONV_TS, CONV_TC), lambda c, b, si: (row(b, si), (blk_b + off) * cb + c))

    ospec = pl.BlockSpec((CONV_TS, CONV_TC), lambda c, b, si: (row(b, si), c))
    wspec = pl.BlockSpec((SC_KERNEL, CONV_TC), lambda c, b, si: (0, c))
    return pl.pallas_call(
        body, name=name, grid=(nc, nb, ns),
        in_specs=[pspec(0), pspec(1), pspec(2), wspec, ospec, ospec],
        out_specs=[ospec, ospec, ospec, wspec],
        out_shape=[jax.ShapeDtypeStruct((t, d), MXU_DTYPE)] * 3 + [jax.ShapeDtypeStruct((SC_KERNEL, d), F32)],
        scratch_shapes=CONV_SCRATCH + [pltpu.VMEM((SC_KERNEL, SUBLANES, CONV_TC), F32)],
        compiler_params=_params(("parallel", "arbitrary", "arbitrary")),
    )(p, p, p, w, cz, dy)


def conformer_conv_fwd(name, p, blk_a, w, bias, nb):
    t = p.shape[0]
    d = w.shape[1]
    s, ns, nc = _conv_grid(t, d, nb)
    cb = d // CONV_TC

    def body(a_ref, b_ref, w_ref, bias_ref, u_ref, pad_ref, sh_ref):
        _carry_past(pad_ref, pl.program_id(2))
        pad_ref[CONV_PAD:CONV_PAD + CONV_TS, :] = a_ref[...].astype(F32) * jax.nn.sigmoid(b_ref[...].astype(F32))

        def emit(r0, u):
            u_ref[pl.ds(r0, CONV_ROWS), :] = u + bias_ref[...]

        _causal_taps(pad_ref, sh_ref, w_ref, CF_KERNEL, emit)

    def pspec(off):
        return pl.BlockSpec((CONV_TS, CONV_TC), lambda c, b, si: (b * ns + si, (blk_a + off) * cb + c))

    return pl.pallas_call(
        body, name=name, grid=(nc, nb, ns),
        in_specs=[pspec(0), pspec(1), pl.BlockSpec((CF_KERNEL, CONV_TC), lambda c, b, si: (0, c)),
                  pl.BlockSpec((1, CONV_TC), lambda c, b, si: (0, c))],
        out_specs=pl.BlockSpec((CONV_TS, CONV_TC), lambda c, b, si: (b * ns + si, c)),
        out_shape=jax.ShapeDtypeStruct((t, d), F32),
        scratch_shapes=CONV_SCRATCH,
        compiler_params=_params(("parallel", "arbitrary", "arbitrary")),
    )(p, p, w, bias)


def conformer_conv_bwd(name, p, blk_a, w, du, nb):
    t = p.shape[0]
    d = w.shape[1]
    s, ns, nc = _conv_grid(t, d, nb)
    cb = d // CONV_TC

    def body(a_ref, b_ref, w_ref, du_ref, da_ref, db_ref, dw_ref, dbias_ref, pad_ref, sh_ref, dw_acc):
        _carry_future(pad_ref, pl.program_id(2))
        duv = du_ref[...]
        pad_ref[0:CONV_TS, :] = duv
        finish = _conv_bwd_edges(dw_acc, dw_ref, nb, ns, extra=(dbias_ref,))
        dbias_ref[...] += jnp.sum(duv, axis=0, keepdims=True)

        def x_rows(r0):
            rows = pl.ds(r0, CONV_ROWS)
            return a_ref[rows, :].astype(F32), jax.nn.sigmoid(b_ref[rows, :].astype(F32))

        def emit(r0, asg, du0):
            rows = pl.ds(r0, CONV_ROWS)
            av, sg = asg
            da_ref[rows, :] = (du0 * sg).astype(da_ref.dtype)
            db_ref[rows, :] = (du0 * av * sg * (1.0 - sg)).astype(db_ref.dtype)

        _conv_bwd_taps(pad_ref, sh_ref, w_ref, dw_acc, CF_KERNEL, x_rows, emit, lambda asg: asg[0] * asg[1])
        finish()

    def row(b, si):
        return b * ns + (ns - 1 - si)

    def pspec(off):
        return pl.BlockSpec((CONV_TS, CONV_TC), lambda c, b, si: (row(b, si), (blk_a + off) * cb + c))

    ospec = pl.BlockSpec((CONV_TS, CONV_TC), lambda c, b, si: (row(b, si), c))
    wspec = pl.BlockSpec((CF_KERNEL, CONV_TC), lambda c, b, si: (0, c))
    bspec = pl.BlockSpec((1, CONV_TC), lambda c, b, si: (0, c))
    return pl.pallas_call(
        body, name=name, grid=(nc, nb, ns),
        in_specs=[pspec(0), pspec(1), wspec, ospec],
        out_specs=[ospec, ospec, wspec, bspec],
        out_shape=[jax.ShapeDtypeStruct((t, d), MXU_DTYPE)] * 2
        + [jax.ShapeDtypeStruct((CF_KERNEL, d), F32), jax.ShapeDtypeStruct((1, d), F32)],
        scratch_shapes=CONV_SCRATCH + [pltpu.VMEM((CF_KERNEL, SUBLANES, CONV_TC), F32)],
        compiler_params=_params(("parallel", "arbitrary", "arbitrary")),
    )(p, p, w, du)


BLOCKS = ("ffn1", "mixer", "ffn2")
BLOCK_WEIGHTS = {"ffn1": ("ffn1_w_gu", "ffn1_w_down"), "mixer": ("w_in", "w_ret_o", "w_sc_o", "w_cf_o", "w_o"),
                 "ffn2": ("ffn2_w_gu", "ffn2_w_down")}
BIG = BLOCK_WEIGHTS["ffn1"] + BLOCK_WEIGHTS["mixer"] + BLOCK_WEIGHTS["ffn2"]
MODE = {"ffn1_w_gu": "col", "ffn1_w_down": "row", "w_in": "col", "w_ret_o": "row", "w_sc_o": "row",
        "w_cf_o": "row", "w_o": "row", "ffn2_w_gu": "col", "ffn2_w_down": "row"}
NORM_OF = {"ffn1": 0, "mixer": 2, "ffn2": 4}
BLK_GATE, BLK_SCB, BLK_CFA, BLK_MERGE = 2, 3, 6, 8


def _rope_tables(positions):
    half = RET_QK_DIM // 2
    inv_freq = ROPE_BASE ** (-jnp.arange(half, dtype=F32) / half)
    ang = positions.astype(F32)[..., None] * inv_freq
    cos, sin = jnp.cos(ang), jnp.sin(ang)
    nb, s = positions.shape
    cos2 = jnp.concatenate([cos, cos], axis=-1).reshape(nb * s, RET_QK_DIM)
    sin2 = jnp.concatenate([-sin, sin], axis=-1).reshape(nb * s, RET_QK_DIM)
    return cos2, sin2


def _log_gamma():
    lg = jnp.log(1.0 - 2.0 ** (-5.0 - jnp.arange(RET_HEADS, dtype=F32)))
    return jnp.broadcast_to(lg[:, None, None], (RET_HEADS, 1, RET_QK_DIM))


def _ffn_fwd(xs, w, tag, g_pre, g_post):
    h = rms_fwd("ffn_rms", xs, g_pre)
    gu, a = ffn_up("ffn_up", h, w[tag + "_w_gu"])
    y = mm_fwd("ffn_down", a, w[tag + "_w_down"], "row", F32)
    out = post_fwd("ffn_post", xs, y, g_post, 0.5)
    return out, dict(x=xs, h=h, gu=gu, a=a, y=y, w=w)


def _pinned(g, token):
    return g if token is None else g + token


def _ffn_bwd(dxs, sv, tag, g_pre, g_post, push):
    w = sv["w"]
    gu_w, down_w = w[tag + "_w_gu"], w[tag + "_w_down"]
    dy, dg_post = post_bwd("ffn_post_bwd", sv["y"], g_post, dxs, 0.5)
    dgu = ffn_down_dx("ffn_down_dx", dy, down_w, sv["gu"])
    grads = {tag + "_w_down": mm_dw("ffn_down_dw", sv["a"], dy, "row", down_w.shape),
             tag + "_w_gu": mm_dw("ffn_gu_dw", sv["h"], dgu, "col", gu_w.shape)}
    g_pre = _pinned(g_pre, push(grads))
    dh = mm_dx("ffn_gu_dx", dgu, gu_w, "col", F32)
    dxs, dg_pre = rms_bwd("ffn_rms_bwd", sv["x"], g_pre, dh, dxs)
    return dxs, dg_pre, dg_post


def _mixer_fwd(xs, w, sm, g_pre, g_post, rope, nb, s, mid):
    cos2, sin2, log_g = rope
    d = xs.shape[1]
    gate_blk = (BLK_GATE * d) // RET_V_DIM
    h = rms_fwd("mx_rms", xs, g_pre)
    p = mm_fwd("mx_in", h, w["w_in"], "col", MXU_DTYPE)
    if mid is not None:
        sm = dict(sm, cf_dw_b=sm["cf_dw_b"] + mid(p))
    o = retention_fwd("ret_fwd", p, cos2, sin2, log_g, nb, s)
    ya_in = head_gate_fwd("ret_gate", o, p, gate_blk)
    yb_in, cz = short_conv_fwd("sc_fwd", p, BLK_SCB, sm["sc_conv_w"], nb)
    u1 = conformer_conv_fwd("cf_fwd", p, BLK_CFA, sm["cf_dw_w"], sm["cf_dw_b"], nb)
    yc_in = ln_silu_fwd("cf_ln", u1, sm["cf_ln_g"], sm["cf_ln_b"])
    ya = mm_fwd("mx_proj", ya_in, w["w_ret_o"], "row", F32)
    yb = mm_fwd("mx_proj", yb_in, w["w_sc_o"], "row", F32)
    yc = mm_fwd("mx_proj", yc_in, w["w_cf_o"], "row", F32)
    mg = merge_fwd("mx_merge", p, BLK_MERGE, ya, yb, yc)
    m = mm_fwd("mx_proj", mg, w["w_o"], "row", F32)
    out = post_fwd("mx_post", xs, m, g_post, 1.0)
    return out, dict(x=xs, h=h, p=p, o=o, ya_in=ya_in, yb_in=yb_in, cz=cz, u1=u1, yc_in=yc_in, ya=ya, yb=yb, yc=yc,
                     mg=mg, m=m, w=w)


def _mixer_bwd(dxs, sv, sm, g_pre, g_post, rope, nb, s, push):
    cos2, sin2, log_g = rope
    w, p = sv["w"], sv["p"]
    d = dxs.shape[1]
    gate_blk = (BLK_GATE * d) // RET_V_DIM
    grads, gsm = {}, {}

    def proj_bwd(wname, a_in, dy, out_dtype):
        grads[wname] = mm_dw("mx_proj_dw", a_in, dy, "row", w[wname].shape)
        return mm_dx("mx_proj_dx", dy, w[wname], "row", out_dtype)

    dm, dg_post = post_bwd("mx_post_bwd", sv["m"], g_post, dxs, 1.0)
    dmg = proj_bwd("w_o", sv["mg"], dm, MXU_DTYPE)
    dg0, dg1, dg2, dya, dyb, dyc = merge_bwd("mx_merge_bwd", p, BLK_MERGE, sv["ya"], sv["yb"], sv["yc"], dmg)
    dya_in = proj_bwd("w_ret_o", sv["ya_in"], dya, MXU_DTYPE)
    dyb_in = proj_bwd("w_sc_o", sv["yb_in"], dyb, MXU_DTYPE)
    dyc_in = proj_bwd("w_cf_o", sv["yc_in"], dyc, MXU_DTYPE)
    do, dgret = head_gate_bwd("ret_gate_bwd", sv["o"], p, gate_blk, dya_in)
    dq, dk, dv = retention_bwd("ret_bwd", p, cos2, sin2, log_g, do, nb, s)
    dscb, dscc, dscx, gsm["sc_conv_w"] = short_conv_bwd("sc_bwd", p, BLK_SCB, sm["sc_conv_w"], sv["cz"], dyb_in, nb)
    du1, dlg, dlb = ln_silu_bwd("cf_ln_bwd", sv["u1"], sm["cf_ln_g"], sm["cf_ln_b"], dyc_in)
    dcfa, dcfb, gsm["cf_dw_w"], dbias = conformer_conv_bwd("cf_bwd", p, BLK_CFA, sm["cf_dw_w"], du1, nb)
    gsm.update(cf_ln_g=dlg[0], cf_ln_b=dlb[0], cf_dw_b=dbias[0])
    dp = concat_cols("mx_dp", [dq, dk, dv, dgret, dscb, dscc, dscx, dcfa, dcfb, dg0, dg1, dg2])
    grads["w_in"] = mm_dw("mx_in_dw", sv["h"], dp, "col", w["w_in"].shape)
    g_pre = _pinned(g_pre, push(grads))
    dh = mm_dx("mx_in_dx", dp, w["w_in"], "col", F32)
    dxs, dg_pre = rms_bwd("mx_rms_bwd", sv["x"], g_pre, dh, dxs)
    return dxs, gsm, dg_pre, dg_post


def local_step(x, positions, target, small, fetch, push):
    nb, s, d = x.shape
    t = nb * s
    depth = small["norm_g"].shape[0]
    rope = _rope_tables(positions) + (_log_gamma(),)
    xs = x.reshape(t, d)
    token = [None]

    def gain(l, i):
        g = small["norm_g"][l, i][None, :]
        if token[0] is not None:
            g, token[0] = g + token[0], None
        return g

    def mixer_small(l):
        return dict(sc_conv_w=small["sc_conv_w"][l], cf_dw_w=small["cf_dw_w"][l], cf_dw_b=small["cf_dw_b"][l][None, :],
                    cf_ln_g=small["cf_ln_g"][l][None, :], cf_ln_b=small["cf_ln_b"][l][None, :])

    saved = {}
    for l in range(depth):
        for blk in BLOCKS:
            w, token[0], mid = fetch(l, blk, xs)
            i0 = NORM_OF[blk]
            if blk == "mixer":
                xs, saved[l, blk] = _mixer_fwd(xs, w, mixer_small(l), gain(l, i0), gain(l, i0 + 1), rope, nb, s, mid)
            else:
                xs, saved[l, blk] = _ffn_fwd(xs, w, blk, gain(l, i0), gain(l, i0 + 1))

    dxs, loss = loss_head("loss", xs, target.reshape(t, d))

    dnorm = [[None] * 6 for _ in range(depth)]
    gsmall = {n: [None] * depth for n in ("sc_conv_w", "cf_dw_w", "cf_dw_b", "cf_ln_g", "cf_ln_b")}
    for l in reversed(range(depth)):
        for blk in reversed(BLOCKS):
            i0 = NORM_OF[blk]
            g_post, g_pre = gain(l, i0 + 1), gain(l, i0)
            put = functools.partial(push, l, blk)
            if blk == "mixer":
                dxs, gsm, dnorm[l][i0], dnorm[l][i0 + 1] = _mixer_bwd(
                    dxs, saved[l, blk], mixer_small(l), g_pre, g_post, rope, nb, s, put)
                for n, v in gsm.items():
                    gsmall[n][l] = v
            else:
                dxs, dnorm[l][i0], dnorm[l][i0 + 1] = _ffn_bwd(dxs, saved[l, blk], blk, g_pre, g_post, put)

    gs = {n: jnp.stack(v) for n, v in gsmall.items()}
    gs["norm_g"] = jnp.stack([jnp.concatenate(r, axis=0) for r in dnorm])
    return loss, dxs.reshape(nb, s, d), gs


ANY = pl.BlockSpec(memory_space=pl.ANY)
HBM = pl.BlockSpec(memory_space=pltpu.HBM)
SEM = pl.BlockSpec(memory_space=pltpu.SEMAPHORE)
VMEM_WHOLE = pl.BlockSpec(memory_space=pltpu.VMEM)
EFFECT = pltpu.SideEffectType.DATAFLOW_SIDE_EFFECTING
TOKEN = jax.ShapeDtypeStruct((8, 128), F32)


def _other_chips(x, y):
    return [(1 - x, y), (x, 1 - y), (1 - x, 1 - y)]


def _remote(src, dst, send_sem, recv_sem, to):
    return pltpu.make_async_remote_copy(src_ref=src, dst_ref=dst, send_sem=send_sem, recv_sem=recv_sem,
                                        device_id=to, device_id_type=MESH)


def _in_hbm(v):
    return pltpu.with_memory_space_constraint(v, pltpu.HBM)


def place_quarters(ws, layer, ids, after):
    m = len(ws)

    def body(ids_ref, *refs):
        for w_ref, o_ref in zip(refs[:m], refs[m + 1:]):
            o_ref[...] = w_ref[...].astype(o_ref.dtype)

    def spec(w, where):
        return pl.BlockSpec((None, w.shape[1] // STREAM_STEPS, w.shape[2]), where)

    return pl.pallas_call(
        body, name="place_quarters",
        grid_spec=pltpu.PrefetchScalarGridSpec(
            num_scalar_prefetch=1, grid=(STREAM_STEPS,),
            in_specs=[spec(w, lambda i, ids_ref: (layer, i, 0)) for w in ws] + [ANY],
            out_specs=[spec(w, lambda i, ids_ref: (ids_ref[0], i, 0)) for w in ws]),
        out_shape=[jax.ShapeDtypeStruct((N_CHIP,) + w.shape[1:], MXU_DTYPE) for w in ws],
        compiler_params=_params(("parallel",)),
    )(ids, *ws, after)


def _gather_copies(lands, send, recv):
    x, y, c = _axes()
    me = 2 * x + y
    mine, theirs = [], []
    for a, ld in enumerate(lands):
        rh = ld.shape[1] // 2
        rows = pl.ds(c * rh, rh)
        for k, (px, py) in enumerate(_other_chips(x, y)):
            to = (px, py, c)
            mine.append(_remote(ld.at[me, rows, :], ld.at[me, rows, :], send.at[3 * a + k], recv.at[3 * a + k], to))
            got = ld.at[2 * px + py, rows, :]
            theirs.append(_remote(got, got, send.at[3 * a + k], recv.at[3 * a + k], to))
    return mine, theirs


def gather_start(name, groups, after):
    flat = [s for g in groups for s in g]
    n, ng = len(flat), len(groups)
    sizes = [len(g) for g in groups]

    def body(*refs):
        lands = refs[:n]
        sems = refs[n + 1:n + 1 + 2 * ng]
        token = refs[-1]
        at = 0
        for g, m in enumerate(sizes):
            mine, _ = _gather_copies(lands[at:at + m], sems[2 * g], sems[2 * g + 1])
            for cp in mine:
                cp.start()
            at += m
        token[...] = jnp.zeros_like(token)

    sem_shapes = []
    for m in sizes:
        sem_shapes += [pltpu.SemaphoreType.DMA((3 * m,))] * 2
    res = pl.pallas_call(
        body, name=name, in_specs=[HBM] * n + [ANY],
        out_specs=[SEM] * (2 * ng) + [HBM] * n + [VMEM_WHOLE],
        out_shape=sem_shapes + [pltpu.HBM(s.shape, s.dtype) for s in flat] + [TOKEN],
        input_output_aliases={i: 2 * ng + i for i in range(n)},
        compiler_params=pltpu.CompilerParams(has_side_effects=EFFECT),
    )(*[_in_hbm(s) for s in flat], after)
    sems, thru, token = res[:2 * ng], res[2 * ng:2 * ng + n], res[-1]
    out, at = [], 0
    for g, m in enumerate(sizes):
        out.append((sems[2 * g], sems[2 * g + 1], thru[at:at + m]))
        at += m
    return out, token


def gather_wait(lands, send, recv, after):
    m = len(lands)

    def body(*refs):
        mine, theirs = _gather_copies(refs[:m], refs[m], refs[m + 1])
        for cp in mine:
            cp.wait_send()
        for cp in theirs:
            cp.wait_recv()

    return pl.pallas_call(
        body, name="gather_wait", in_specs=[HBM] * m + [SEM, SEM, ANY], out_specs=[HBM] * m,
        out_shape=[pltpu.HBM(l.shape, l.dtype) for l in lands],
        input_output_aliases={i: i for i in range(m)},
        compiler_params=pltpu.CompilerParams(has_side_effects=EFFECT),
    )(*lands, send, recv, after)


def copy_start(name, bufs, copies, ncopy, after=()):
    n, k = len(bufs), len(after)

    def body(*refs):
        for cp in copies(refs[:n], refs[n + k], refs[n + k + 1])[0]:
            cp.start()
        refs[-1][...] = jnp.zeros_like(refs[-1])

    res = pl.pallas_call(
        body, name=name, in_specs=[HBM] * n + [ANY] * k, out_specs=[SEM, SEM] + [HBM] * n + [VMEM_WHOLE],
        out_shape=[pltpu.SemaphoreType.DMA((ncopy,))] * 2 + [pltpu.HBM(b.shape, b.dtype) for b in bufs] + [TOKEN],
        input_output_aliases={i: 2 + i for i in range(n)},
        compiler_params=pltpu.CompilerParams(has_side_effects=EFFECT),
    )(*[_in_hbm(b) for b in bufs], *after)
    return res[0], res[1], list(res[2:2 + n]), res[-1]


def copy_wait(name, bufs, send, recv, copies, after=()):
    n = len(bufs)

    def body(*refs):
        mine, theirs = copies(refs[:n], refs[n], refs[n + 1])
        for cp in mine:
            cp.wait_send()
        for cp in theirs:
            cp.wait_recv()

    return list(pl.pallas_call(
        body, name=name, in_specs=[HBM] * n + [SEM, SEM] + [ANY] * len(after), out_specs=[HBM] * n,
        out_shape=[pltpu.HBM(b.shape, b.dtype) for b in bufs], input_output_aliases={i: i for i in range(n)},
        compiler_params=pltpu.CompilerParams(has_side_effects=EFFECT),
    )(*bufs, send, recv, *after))


def _fill_copies(lands, send, recv):
    x, y, c = _axes()
    sib = (x, y, 1 - c)
    mine, theirs = [], []
    for a, ld in enumerate(lands):
        rh = ld.shape[1] // 2
        for k, (px, py) in enumerate(_other_chips(x, y)):
            got = ld.at[2 * px + py, pl.ds(c * rh, rh), :]
            mine.append(_remote(got, got, send.at[3 * a + k], recv.at[3 * a + k], sib))
            blk = ld.at[2 * px + py, pl.ds((1 - c) * rh, rh), :]
            theirs.append(_remote(blk, blk, send.at[3 * a + k], recv.at[3 * a + k], sib))
    return mine, theirs


def _presum_copies(grads, lands, send, recv):
    x, y, c = _axes()
    cps = []
    for a, (g, ld) in enumerate(zip(grads, lands)):
        rh = g.shape[1] // 2
        cps.append(_remote(g.at[:, pl.ds((1 - c) * rh, rh), :], ld, send.at[a], recv.at[a], (x, y, 1 - c)))
    return cps


def presum_start(grads):
    m = len(grads)

    def body(*refs):
        for cp in _presum_copies(refs[:m], refs[m:2 * m], refs[2 * m], refs[2 * m + 1]):
            cp.start()
        refs[-1][...] = jnp.zeros_like(refs[-1])

    lands = [lax.empty((g.shape[0], g.shape[1] // 2, g.shape[2]), g.dtype) for g in grads]
    res = pl.pallas_call(
        body, name="presum_start", in_specs=[HBM] * (2 * m), out_specs=[SEM, SEM] + [HBM] * (2 * m) + [VMEM_WHOLE],
        out_shape=[pltpu.SemaphoreType.DMA((m,))] * 2 + [pltpu.HBM(g.shape, g.dtype) for g in grads]
        + [pltpu.HBM(l.shape, l.dtype) for l in lands] + [TOKEN],
        input_output_aliases={i: 2 + i for i in range(2 * m)},
        compiler_params=pltpu.CompilerParams(has_side_effects=EFFECT),
    )(*[_in_hbm(g) for g in grads], *[_in_hbm(l) for l in lands])
    return res[0], res[1], res[2:2 + m], res[2 + m:2 + 2 * m], res[-1]


def presum_wait(grads, lands, send, recv, after):
    m = len(grads)

    def body(*refs):
        for cp in _presum_copies(refs[:m], refs[m:2 * m], refs[2 * m], refs[2 * m + 1]):
            cp.wait_send()
            cp.wait_recv()

    res = pl.pallas_call(
        body, name="presum_wait", in_specs=[HBM] * (2 * m) + [SEM, SEM] + [ANY] * len(after),
        out_specs=[HBM] * (2 * m),
        out_shape=[pltpu.HBM(g.shape, g.dtype) for g in grads] + [pltpu.HBM(l.shape, l.dtype) for l in lands],
        input_output_aliases={i: i for i in range(2 * m)},
        compiler_params=pltpu.CompilerParams(has_side_effects=EFFECT),
    )(*grads, *lands, send, recv, *after)
    return res[:m], res[m:]


def add_halves(gs, lands, ids):
    m = len(gs)

    def body(ids_ref, *refs):
        for a_ref, b_ref, o_ref in zip(refs[:m], refs[m:2 * m], refs[2 * m:]):
            o_ref[...] = (a_ref[...].astype(F32) + b_ref[...].astype(F32)).astype(o_ref.dtype)

    def spec(ld, where):
        return pl.BlockSpec((None,) + ld.shape[1:], where)

    return pl.pallas_call(
        body, name="add_halves",
        grid_spec=pltpu.PrefetchScalarGridSpec(
            num_scalar_prefetch=1, grid=(N_CHIP,),
            in_specs=[spec(ld, lambda i, ids_ref: (i, ids_ref[1], 0)) for ld in lands]
            + [spec(ld, lambda i, ids_ref: (i, 0, 0)) for ld in lands],
            out_specs=[spec(ld, lambda i, ids_ref: (i, 0, 0)) for ld in lands]),
        out_shape=[jax.ShapeDtypeStruct(ld.shape, ld.dtype) for ld in lands],
        compiler_params=_params(("parallel",)),
    )(ids, *gs, *lands)


def _scatter_copies(parts, lands, send, recv):
    x, y, c = _axes()
    cps = []
    for a, (pt, ld) in enumerate(zip(parts, lands)):
        for k, (px, py) in enumerate(_other_chips(x, y)):
            cps.append(_remote(pt.at[2 * px + py], ld.at[k], send.at[3 * a + k], recv.at[3 * a + k], (px, py, c)))
    return cps


def scatter_start(parts):
    m = len(parts)

    def body(*refs):
        for cp in _scatter_copies(refs[:m], refs[m:2 * m], refs[2 * m], refs[2 * m + 1]):
            cp.start()
        refs[-1][...] = jnp.zeros_like(refs[-1])

    lands = [lax.empty((N_CHIP - 1,) + p.shape[1:], p.dtype) for p in parts]
    res = pl.pallas_call(
        body, name="scatter_start", in_specs=[HBM] * (2 * m), out_specs=[SEM, SEM] + [HBM] * (2 * m) + [VMEM_WHOLE],
        out_shape=[pltpu.SemaphoreType.DMA((3 * m,))] * 2 + [pltpu.HBM(p.shape, p.dtype) for p in parts]
        + [pltpu.HBM(l.shape, l.dtype) for l in lands] + [TOKEN],
        input_output_aliases={i: 2 + i for i in range(2 * m)},
        compiler_params=pltpu.CompilerParams(has_side_effects=EFFECT),
    )(*[_in_hbm(p) for p in parts], *[_in_hbm(l) for l in lands])
    return res[0], res[1], res[2:2 + m], res[2 + m:2 + 2 * m], res[-1]


def scatter_wait(parts, lands, send, recv, after):
    m = len(parts)

    def body(*refs):
        for cp in _scatter_copies(refs[:m], refs[m:2 * m], refs[2 * m], refs[2 * m + 1]):
            cp.wait_send()
            cp.wait_recv()

    res = pl.pallas_call(
        body, name="scatter_wait", in_specs=[HBM] * (2 * m) + [SEM, SEM] + [ANY] * len(after),
        out_specs=[HBM] * (2 * m),
        out_shape=[pltpu.HBM(p.shape, p.dtype) for p in parts] + [pltpu.HBM(l.shape, l.dtype) for l in lands],
        input_output_aliases={i: i for i in range(2 * m)},
        compiler_params=pltpu.CompilerParams(has_side_effects=EFFECT),
    )(*parts, *lands, send, recv, *after)
    return res[:m], res[m:]


def sum_partials(parts, lands, ids, layer, depth, intos):
    m = len(parts)
    nt = STREAM_STEPS

    def body(ids_ref, *refs):
        for p_ref, l_ref, o_ref in zip(refs[:m], refs[m:2 * m], refs[-m:]):
            acc = p_ref[...].astype(F32)
            for k in range(N_CHIP - 1):
                acc = acc + l_ref[k].astype(F32)
            o_ref[...] = acc

    def rows(p):
        return p.shape[1] // nt

    in_specs = [pl.BlockSpec((None, rows(p), p.shape[2]), lambda i, ids_ref: (ids_ref[0], i, 0)) for p in parts]
    in_specs += [pl.BlockSpec((N_CHIP - 1, rows(p), p.shape[2]), lambda i, ids_ref: (0, i, 0)) for p in parts]
    args = [ids, *parts, *lands]
    aliases = {}
    if intos is not None:
        in_specs += [ANY] * m
        args += list(intos)
        aliases = {1 + 2 * m + a: a for a in range(m)}
    return pl.pallas_call(
        body, name="sum_partials",
        grid_spec=pltpu.PrefetchScalarGridSpec(
            num_scalar_prefetch=1, grid=(nt,), in_specs=in_specs,
            out_specs=[pl.BlockSpec((None, rows(p), p.shape[2]), lambda i, ids_ref: (layer, ids_ref[1] * nt + i, 0))
                       for p in parts]),
        out_shape=[jax.ShapeDtypeStruct((depth, 2 * p.shape[1], p.shape[2]), F32) for p in parts],
        input_output_aliases=aliases, compiler_params=_params(("parallel",)),
    )(*args)


def _final_copies(layer):
    def copies(bufs, send, recv):
        x, y, c = _axes()
        sib = (x, y, 1 - c)
        mine, theirs = [], []
        for a, buf in enumerate(bufs):
            rh = buf.shape[1] // 2
            src = buf.at[layer, pl.ds(c * rh, rh), :]
            mine.append(_remote(src, src, send.at[a], recv.at[a], sib))
            dst = buf.at[layer, pl.ds((1 - c) * rh, rh), :]
            theirs.append(_remote(dst, dst, send.at[a], recv.at[a], sib))
        return mine, theirs

    return copies


def allgather_small(pk):
    def body(in_ref, out_ref, send, recv):
        x, y, c = _axes()
        me = 2 * x + y
        chips = _other_chips(x, y)
        out_ref[pl.ds(me, 1)] = in_ref[...][None]
        cps = []
        for k, (px, py) in enumerate(chips):
            cp = _remote(in_ref, out_ref.at[me], send.at[k], recv.at[k], (px, py, c))
            cp.start()
            cps.append(cp)
        for k, (px, py) in enumerate(chips):
            got = out_ref.at[2 * px + py]
            _remote(got, got, send.at[k], recv.at[k], (px, py, c)).wait_recv()
        for cp in cps:
            cp.wait_send()

    return pl.pallas_call(
        body, name="allgather_small", in_specs=[VMEM_WHOLE], out_specs=VMEM_WHOLE,
        out_shape=jax.ShapeDtypeStruct((N_CHIP,) + pk.shape, pk.dtype),
        scratch_shapes=[pltpu.SemaphoreType.DMA((3,))] * 2,
    )(pk)


N_DEV = 8


def _small_copies(bufs, send, recv):
    g, slots = bufs
    x, y, c = _axes()
    me = 4 * x + 2 * y + c
    mine, theirs = [], []
    for mask in range(1, N_DEV):
        px = 1 - x if mask & 4 else x
        py = 1 - y if mask & 2 else y
        pc = 1 - c if mask & 1 else c
        mine.append(_remote(g, slots.at[me], send.at[mask - 1], recv.at[mask - 1], (px, py, pc)))
        got = slots.at[4 * px + 2 * py + pc]
        theirs.append(_remote(got, got, send.at[mask - 1], recv.at[mask - 1], (px, py, pc)))
    return mine, theirs


def sum_slots(g, slots, me):
    def body(me_ref, g_ref, slots_ref, o_ref):
        acc = None
        for d in range(N_DEV):
            term = jnp.where(me_ref[0] == d, g_ref[...], slots_ref[d])
            acc = term if acc is None else acc + term
        o_ref[...] = acc

    return pl.pallas_call(
        body, name="sum_slots",
        grid_spec=pltpu.PrefetchScalarGridSpec(
            num_scalar_prefetch=1, grid=(1,),
            in_specs=[pl.BlockSpec(g.shape, lambda i, me_ref: (0, 0)),
                      pl.BlockSpec(slots.shape, lambda i, me_ref: (0, 0, 0))],
            out_specs=pl.BlockSpec(g.shape, lambda i, me_ref: (0, 0))),
        out_shape=jax.ShapeDtypeStruct(g.shape, g.dtype),
        compiler_params=_params(("arbitrary",)),
    )(me, g, slots)


def adamw(w, g, m, v, layer=None, intos=None):
    shape = w.shape
    cols = shape[-1]
    rows = int(np.prod(shape[:-1]))
    span = rows if layer is None else rows // shape[0]
    tr = span
    for cand in (256, 128):
        if span % cand == 0 and cand * cols * 4 <= 2 * 1024 * 1024:
            tr = cand
            break
    first = 0 if layer is None else layer * (span // tr)
    c1 = 1.0 - ADAM_B1 ** ADAM_STEP
    c2 = 1.0 - ADAM_B2 ** ADAM_STEP

    def body(w_ref, g_ref, m_ref, v_ref, *rest):
        d_ref, nm_ref, nv_ref, g_out = rest[-4:]
        gv = g_ref[...]
        g_out[...] = gv
        nm = ADAM_B1 * m_ref[...] + (1.0 - ADAM_B1) * gv
        nv = ADAM_B2 * v_ref[...] + (1.0 - ADAM_B2) * jnp.square(gv)
        d_ref[...] = -ADAM_LR * ((nm / c1) / (jnp.sqrt(nv / c2) + ADAM_EPS) + ADAM_WD * w_ref[...])
        nm_ref[...] = nm
        nv_ref[...] = nv

    spec = pl.BlockSpec((tr, cols), lambda i: (first + i, 0))
    args = [a.reshape(rows, cols) for a in (w, g, m, v)]
    in_specs, aliases = [spec] * 4, {}
    if intos is not None:
        args += [a.reshape(rows, cols) for a in intos]
        in_specs += [ANY] * 4
        aliases = {4 + k: k for k in range(4)}
    res = pl.pallas_call(
        body, name="adamw", grid=(span // tr,), in_specs=in_specs, out_specs=[spec] * 4,
        out_shape=[jax.ShapeDtypeStruct((rows, cols), F32)] * 4, input_output_aliases=aliases,
        compiler_params=_params(("parallel",)),
    )(*args)
    return [r.reshape(shape) for r in res]


WEIGHTS = ("norm_g", "ffn1_w_gu", "ffn1_w_down", "w_in", "w_ret_o", "sc_conv_w", "w_sc_o", "cf_dw_w", "cf_dw_b",
           "cf_ln_g", "cf_ln_b", "w_cf_o", "w_o", "ffn2_w_gu", "ffn2_w_down")
SHARDED_SMALL = ("norm_g", "sc_conv_w", "cf_dw_w")
REPLICATED_SMALL = ("cf_dw_b", "cf_ln_g", "cf_ln_b")

def _pack_rows(parts):
    padded, offs, at = [], [], 0
    for p in parts:
        r = -(-p.shape[0] // SUBLANES) * SUBLANES
        padded.append(jnp.pad(p, ((0, r - p.shape[0]), (0, 0))))
        offs.append(at)
        at += r
    return jnp.concatenate(padded, axis=0), offs


def kernel(x, positions, norm_g, ffn1_w_gu, ffn1_w_down, w_in, w_ret_o, sc_conv_w, w_sc_o, cf_dw_w, cf_dw_b, cf_ln_g, cf_ln_b, w_cf_o, w_o, ffn2_w_gu, ffn2_w_down, loss_target, m_norm_g, m_ffn1_w_gu, m_ffn1_w_down, m_w_in, m_w_ret_o, m_sc_conv_w, m_w_sc_o, m_cf_dw_w, m_cf_dw_b, m_cf_ln_g, m_cf_ln_b, m_w_cf_o, m_w_o, m_ffn2_w_gu, m_ffn2_w_down, v_norm_g, v_ffn1_w_gu, v_ffn1_w_down, v_w_in, v_w_ret_o, v_sc_conv_w, v_w_sc_o, v_cf_dw_w, v_cf_dw_b, v_cf_ln_g, v_cf_ln_b, v_w_cf_o, v_w_o, v_ffn2_w_gu, v_ffn2_w_down):
    wts = dict(zip(WEIGHTS, (norm_g, ffn1_w_gu, ffn1_w_down, w_in, w_ret_o, sc_conv_w, w_sc_o, cf_dw_w, cf_dw_b,
                             cf_ln_g, cf_ln_b, w_cf_o, w_o, ffn2_w_gu, ffn2_w_down)))
    mom = dict(zip(WEIGHTS, (m_norm_g, m_ffn1_w_gu, m_ffn1_w_down, m_w_in, m_w_ret_o, m_sc_conv_w, m_w_sc_o,
                             m_cf_dw_w, m_cf_dw_b, m_cf_ln_g, m_cf_ln_b, m_w_cf_o, m_w_o, m_ffn2_w_gu, m_ffn2_w_down)))
    var = dict(zip(WEIGHTS, (v_norm_g, v_ffn1_w_gu, v_ffn1_w_down, v_w_in, v_w_ret_o, v_sc_conv_w, v_w_sc_o,
                             v_cf_dw_w, v_cf_dw_b, v_cf_ln_g, v_cf_ln_b, v_w_cf_o, v_w_o, v_ffn2_w_gu, v_ffn2_w_down)))
    depth = norm_g.shape[0]
    dq = norm_g.shape[-1]
    d = N_CHIP * dq
    chip = 2 * lax.axis_index("x") + lax.axis_index("y")
    ids = jnp.stack([chip, lax.axis_index("c")]).astype(jnp.int32)

    pk, offs = _pack_rows([wts[n].reshape(-1, dq) for n in SHARDED_SMALL])
    gk4 = allgather_small(pk)
    gk = gk4.transpose(1, 0, 2).reshape(pk.shape[0], d)
    small = {n: wts[n] for n in REPLICATED_SMALL}
    for n, o in zip(SHARDED_SMALL, offs):
        rows = wts[n].shape[0] * wts[n].shape[1]
        small[n] = gk[o:o + rows].reshape(wts[n].shape[:2] + (d,))

    order = [(l, blk) for l in range(depth) for blk in BLOCKS]
    def placed(groups, after):
        return [place_quarters([wts[n] for n in BLOCK_WEIGHTS[blk]], l, ids, after) for l, blk in groups]

    first, token = gather_start("gather_start_first", placed(order[:1], gk4), gk4)
    rest, token = gather_start("gather_start_rest", placed(order[1:], token), token)
    started = dict(zip(order, first + rest))
    small["norm_g"] = small["norm_g"] + token[0:1, 0:1]

    filling = {}

    def fill(group, after):
        send, recv, lands = started[group]
        lands = gather_wait(lands, send, recv, after)
        send, recv, lands, tok = copy_start("fill_start", lands, _fill_copies, 3 * len(lands))
        filling[group] = (send, recv, lands)
        return tok[0:1, 0:1]

    def fetch(l, blk, after):
        at = order.index((l, blk))
        if (l, blk) not in filling:
            fill((l, blk), token if at == 0 else after)
        send, recv, lands = filling.pop((l, blk))
        lands = copy_wait("fill_wait", lands, send, recv, _fill_copies, (after,))
        tok, mid = None, None
        if at == 1:
            mid = functools.partial(fill, order[at + 1])
        elif 1 < at < len(order) - 1:
            tok = fill(order[at + 1], lands[0])
        return dict(zip(BLOCK_WEIGHTS[blk], lands)), tok, mid

    gsum = {n: None for n in BIG}
    presums, scatters, finals = [], [], []

    def scatter_next(after):
        group, gl, lands, send, recv = presums.pop(0)
        gl, lands = presum_wait(gl, lands, send, recv, after)
        send, recv, parts, lands, tok = scatter_start(add_halves(gl, lands, ids))
        scatters.append((group, parts, lands, send, recv))
        return tok

    def sum_next(after):
        (l, blk), parts, lands, send, recv = scatters.pop(0)
        parts, lands = scatter_wait(parts, lands, send, recv, after)
        names = BLOCK_WEIGHTS[blk]
        intos = None if gsum[names[0]] is None else [gsum[n] for n in names]
        sums = sum_partials(parts, lands, ids, l, depth, intos)
        send, recv, sums, tok = copy_start("final_start", sums, _final_copies(l), len(sums))
        gsum.update(zip(names, sums))
        finals.append((names, l, send, recv))
        return tok

    def final_next(after):
        names, l, send, recv = finals.pop(0)
        gsum.update(zip(names, copy_wait("final_wait", [gsum[n] for n in names], send, recv, _final_copies(l), after)))

    def push(l, blk, grads):
        send, recv, gl, lands, tok = presum_start([grads[n] for n in BLOCK_WEIGHTS[blk]])
        if scatters:
            tok = tok + sum_next((gl[0],))
        if presums:
            tok = tok + scatter_next((gl[0],))
        presums.append(((l, blk), gl, lands, send, recv))
        return tok[0:1, 0:1]

    loss, grad_x, gs = local_step(x, positions, loss_target, small, fetch, push)

    names = SHARDED_SMALL + REPLICATED_SMALL
    pg, offs = _pack_rows([gs[n].reshape(-1, d) for n in names])
    s_send, s_recv, s_bufs, tok = copy_start("small_start", [pg, lax.empty((N_DEV,) + pg.shape, pg.dtype)],
                                             _small_copies, N_DEV - 1, (grad_x,))
    tok = scatter_next((grad_x, tok))

    delta, new_m, new_v, grads = {}, {}, {}, {}

    def update(n, layer=None):
        g = gsum[n] if n in BIG else grads[n]
        prev = [delta[n], new_m[n], new_v[n], grads[n]] if layer is not None and n in delta else None
        delta[n], new_m[n], new_v[n], grads[n] = adamw(wts[n], g, mom[n], var[n], layer, prev)

    while finals and finals[0][1] > 0:
        done, l = finals[0][:2]
        final_next((tok,))
        for n in done:
            update(n, l)
    upper = tuple(delta[n] for n in BIG if n in delta)
    pg, slots = copy_wait("small_wait", s_bufs, s_send, s_recv, _small_copies, upper + (tok,))
    me = (2 * chip + lax.axis_index("c")).astype(jnp.int32).reshape(1)
    tot = sum_slots(pg, slots, me)
    for n, o in zip(names, offs):
        rows = int(np.prod(gs[n].shape[:-1]))
        full = tot[o:o + rows]
        if n in SHARDED_SMALL:
            full = lax.dynamic_slice_in_dim(full, chip * dq, dq, axis=1)
        grads[n] = full.reshape(wts[n].shape)

    for n in names:
        update(n)
    after = tuple(delta[n] for n in names)
    while scatters or finals:
        if scatters:
            after = (sum_next(after),)
        done, l = finals[0][:2]
        final_next(after)
        for n in done:
            update(n, l)
        after = tuple(delta[n] for n in done)

    loss_all = lax.psum(loss[0, 0], ("x", "y", "c"))
    return (loss_all, grad_x, *[grads[n] for n in WEIGHTS], *[delta[n] for n in WEIGHTS],
            *[new_m[n] for n in WEIGHTS], *[new_v[n] for n in WEIGHTS])
```

```python
import functools

import jax
import jax.numpy as jnp
import numpy as np
from jax import lax
from jax.experimental import pallas as pl
from jax.experimental.pallas import tpu as pltpu

F32 = jnp.float32
BF16 = jnp.bfloat16
MXU_DTYPE = BF16
VMEM_LIMIT_BYTES = 56 * 1024 * 1024
MESH = pl.DeviceIdType.MESH

N_CHIP = 4
CHUNK = 64
RET_HEADS = 4
RET_QK_DIM = 128
RET_V_DIM = 256
SC_KERNEL = 3
CF_KERNEL = 31
ROPE_BASE = 10000.0
NORM_EPS = 1e-6
LN_EPS = 1e-5
ADAM_LR = 0.001
ADAM_B1 = 0.9
ADAM_B2 = 0.999
ADAM_EPS = 1e-08
ADAM_WD = 0.01
ADAM_STEP = 10

SUBLANES = 8
CONV_PAD = 32
CONV_TS = 128
CONV_TC = 512
CONV_ROWS = 16
CONV_SCRATCH = [pltpu.VMEM((CONV_TS + CONV_PAD, CONV_TC), F32),
                pltpu.VMEM((SUBLANES - 1, CONV_TS + CONV_PAD - SUBLANES, CONV_TC), F32)]
RET_TQ = 512
MM_TM = 1024
MM_TN = 1536
MM_K1 = 1024
MM_W1 = 8 << 20
MM_SLICE = 256
MM_IN_BYTES = 36 << 20
STREAM_STEPS = 2


def _params(sem):
    return pltpu.CompilerParams(dimension_semantics=sem, vmem_limit_bytes=VMEM_LIMIT_BYTES)


def _axes():
    return lax.axis_index("x"), lax.axis_index("y"), lax.axis_index("c")


NN = (((1,), (0,)), ((), ()))
NT = (((1,), (1,)), ((), ()))
TN = (((0,), (0,)), ((), ()))


def _mm(name, a, b, out_shape, out_dtype, grid, a_spec, b_spec, o_spec, dims, acc_shape):
    nk = grid[2]

    def body(a_ref, b_ref, o_ref, *scratch):
        bv = b_ref[...]
        if bv.ndim == 3:
            bv = bv.reshape(-1, bv.shape[-1])
        part = lax.dot_general(a_ref[...], bv, dims, preferred_element_type=F32)

        def put(v):
            o_ref[...] = v.reshape(o_ref.shape).astype(o_ref.dtype)

        if nk == 1:
            put(part)
        else:
            acc = scratch[0]
            k = pl.program_id(2)

            @pl.when(k == 0)
            def _():
                acc[...] = part

            @pl.when(k > 0)
            def _():
                acc[...] += part

            @pl.when(k == nk - 1)
            def _():
                put(acc[...])

    scratch = [pltpu.VMEM(acc_shape, F32)] if nk > 1 else []
    return pl.pallas_call(
        body, name=name, grid=grid, in_specs=[a_spec, b_spec], out_specs=o_spec,
        out_shape=jax.ShapeDtypeStruct(out_shape, out_dtype), scratch_shapes=scratch,
        compiler_params=_params(("parallel", "parallel", "arbitrary")),
    )(a, b)


def _tile(n, target):
    best = None
    for t in range(128, min(n, target) + 1, 128):
        if n % t == 0:
            best = t
    assert best is not None, (n, target)
    return best


def _token_rows(t, width):
    tt = t
    while tt > MM_TM and tt * width * jnp.dtype(MXU_DTYPE).itemsize * 2 > MM_IN_BYTES:
        tt //= 2
    return tt


def mm_fwd(name, a, w4, mode, out_dtype):
    t = a.shape[0]
    _, r, c = w4.shape
    tm = min(t, MM_TM)
    if mode == "col":
        tn = _tile(c, MM_TN)
        npj = c // tn
        grid = (t // tm, N_CHIP * npj, 1)
        a_spec = pl.BlockSpec((tm, r), lambda i, j, k: (i, 0))
        b_spec = pl.BlockSpec((None, r, tn), lambda i, j, k: (j // npj, 0, j % npj))
        o_spec = pl.BlockSpec((tm, tn), lambda i, j, k: (i, j))
        return _mm(name, a, w4, (t, N_CHIP * c), out_dtype, grid, a_spec, b_spec, o_spec, NN, (tm, tn))
    if w4.size * w4.dtype.itemsize <= MM_W1:
        grid = (t // tm, 1, 1)
        a_spec = pl.BlockSpec((tm, N_CHIP * r), lambda i, j, k: (i, 0))
        b_spec = pl.BlockSpec((N_CHIP, r, c), lambda i, j, k: (0, 0, 0))
        o_spec = pl.BlockSpec((tm, c), lambda i, j, k: (i, 0))
        return _mm(name, a, w4, (t, c), out_dtype, grid, a_spec, b_spec, o_spec, NN, (tm, c))
    grid = (t // tm, 1, N_CHIP)
    a_spec = pl.BlockSpec((tm, r), lambda i, j, k: (i, k))
    b_spec = pl.BlockSpec((None, r, c), lambda i, j, k: (k, 0, 0))
    o_spec = pl.BlockSpec((tm, c), lambda i, j, k: (i, 0))
    return _mm(name, a, w4, (t, c), out_dtype, grid, a_spec, b_spec, o_spec, NN, (tm, c))


def mm_dx(name, dy, w4, mode, out_dtype):
    t = dy.shape[-2]
    _, r, c = w4.shape
    tm = min(t, MM_TM)
    if mode == "col":
        tn, npj = c, 1
        hb = N_CHIP // 2 * npj
        grid = (t // tm, 1, N_CHIP * npj)
        if dy.ndim == 3:
            a_spec = pl.BlockSpec((None, tm, tn), lambda i, j, k: (k // hb, i, k % hb))
        else:
            a_spec = pl.BlockSpec((tm, tn), lambda i, j, k: (i, k))
        b_spec = pl.BlockSpec((None, r, tn), lambda i, j, k: (k // npj, 0, k % npj))
        o_spec = pl.BlockSpec((tm, r), lambda i, j, k: (i, 0))
        return _mm(name, dy, w4, (t, r), out_dtype, grid, a_spec, b_spec, o_spec, NT, (tm, r))
    if N_CHIP * r <= MM_K1:
        grid = (t // tm, 1, 1)
        a_spec = pl.BlockSpec((tm, c), lambda i, j, k: (i, 0))
        b_spec = pl.BlockSpec((N_CHIP, r, c), lambda i, j, k: (0, 0, 0))
        o_spec = pl.BlockSpec((tm, N_CHIP * r), lambda i, j, k: (i, 0))
        return _mm(name, dy, w4, (t, N_CHIP * r), out_dtype, grid, a_spec, b_spec, o_spec, NT, (tm, N_CHIP * r))
    grid = (t // tm, N_CHIP, 1)
    a_spec = pl.BlockSpec((tm, c), lambda i, j, k: (i, 0))
    b_spec = pl.BlockSpec((None, r, c), lambda i, j, k: (j, 0, 0))
    o_spec = pl.BlockSpec((tm, r), lambda i, j, k: (i, j))
    return _mm(name, dy, w4, (t, N_CHIP * r), out_dtype, grid, a_spec, b_spec, o_spec, NT, (tm, r))


def mm_dw(name, a, dy, mode, shape3):
    t = a.shape[0]
    _, r, c = shape3
    if mode == "col":
        tn = _tile(c, MM_TN)
        npj = c // tn
        tt = _token_rows(t, r + tn)
        grid = (1, N_CHIP * npj, t // tt)
        a_spec = pl.BlockSpec((tt, r), lambda i, j, k: (k, 0))
        hb = N_CHIP // 2 * npj
        if dy.ndim == 3:
            b_spec = pl.BlockSpec((None, tt, tn), lambda i, j, k: (j // hb, k, j % hb))
        else:
            b_spec = pl.BlockSpec((tt, tn), lambda i, j, k: (k, j))
        o_spec = pl.BlockSpec((None, r, tn), lambda i, j, k: (j // npj, 0, j % npj))
        return _mm(name, a, dy, shape3, MXU_DTYPE, grid, a_spec, b_spec, o_spec, TN, (r, tn))
    if N_CHIP * r <= MM_K1:
        tt = _token_rows(t, N_CHIP * r + c)
        grid = (1, 1, t // tt)
        a_spec = pl.BlockSpec((tt, N_CHIP * r), lambda i, j, k: (k, 0))
        b_spec = pl.BlockSpec((tt, c), lambda i, j, k: (k, 0))
        o_spec = pl.BlockSpec((N_CHIP, r, c), lambda i, j, k: (0, 0, 0))
        return _mm(name, a, dy, shape3, MXU_DTYPE, grid, a_spec, b_spec, o_spec, TN, (N_CHIP * r, c))
    tt = _token_rows(t, r + c)
    grid = (N_CHIP, 1, t // tt)
    a_spec = pl.BlockSpec((tt, r), lambda i, j, k: (k, i))
    b_spec = pl.BlockSpec((tt, c), lambda i, j, k: (k, 0))
    o_spec = pl.BlockSpec((None, r, c), lambda i, j, k: (i, 0, 0))
    return _mm(name, a, dy, shape3, MXU_DTYPE, grid, a_spec, b_spec, o_spec, TN, (r, c))


def mm_dx_norms(name, dy, w4, x, g_pre, dres, prev):
    t = dy.shape[-2]
    _, r, c = w4.shape
    tm = min(t, MM_TM // 2)
    hb = N_CHIP // 2
    chained = prev is not None
    nk = N_CHIP

    def body(dy_ref, w_ref, x_ref, dres_ref, g_ref, *rest):
        if chained:
            y_ref, gp_ref, dx_ref, dg_ref, dyp_ref, dgp_ref, acc = rest
        else:
            dx_ref, dg_ref, acc = rest
        i, k = pl.program_id(0), pl.program_id(1)
        part = lax.dot_general(dy_ref[...], w_ref[...], NT, preferred_element_type=F32)

        @pl.when(k == 0)
        def _():
            acc[...] = part

        @pl.when(k > 0)
        def _():
            acc[...] += part

        def add_to(ref, v):
            @pl.when(i == 0)
            def _():
                ref[...] = v

            @pl.when(i > 0)
            def _():
                ref[...] += v

        @pl.when(k == nk - 1)
        def _():
            _, vjp = jax.vjp(_rms, x_ref[...], g_ref[...])
            dx, dg = vjp(acc[...])
            dxs = dres_ref[...] + dx
            dx_ref[...] = dxs
            add_to(dg_ref, dg)
            if chained:
                _, vjp_prev = jax.vjp(lambda y, g: prev[2] * _rms(y, g), y_ref[...], gp_ref[...])
                dyp, dgp = vjp_prev(dxs)
                dyp_ref[...] = dyp.astype(dyp_ref.dtype)
                add_to(dgp_ref, dgp)

    if dy.ndim == 3:
        dy_spec = pl.BlockSpec((None, tm, c), lambda i, k: (k // hb, i, k % hb))
    else:
        dy_spec = pl.BlockSpec((tm, c), lambda i, k: (i, k))
    rows = pl.BlockSpec((tm, r), lambda i, k: (i, 0))
    gain = pl.BlockSpec((1, r), lambda i, k: (0, 0))
    in_specs = [dy_spec, pl.BlockSpec((None, r, c), lambda i, k: (k, 0, 0)), rows, rows, gain]
    args = [dy, w4, x, dres, g_pre]
    out_specs = [rows, gain]
    out_shape = [jax.ShapeDtypeStruct((t, r), F32), jax.ShapeDtypeStruct((1, r), F32)]
    if chained:
        in_specs += [rows, gain]
        args += [prev[0], prev[1]]
        out_specs += [rows, gain]
        out_shape += [jax.ShapeDtypeStruct((t, r), MXU_DTYPE), jax.ShapeDtypeStruct((1, r), F32)]
    res = pl.pallas_call(
        body, name=name, grid=(t // tm, nk), in_specs=in_specs, out_specs=out_specs, out_shape=out_shape,
        scratch_shapes=[pltpu.VMEM((tm, r), F32)], compiler_params=_params(("arbitrary", "arbitrary")),
    )(*args)
    return tuple(res) if chained else (res[0], res[1], None, None)


def _rowwise(name, fn, rows, pars, outs, accs=(), tm=256, ncol=1):
    t = rows[0][0].shape[0]
    nrow, npar, nout = len(rows), len(pars), len(outs)

    def body(*refs):
        vals = [r[...] for r in refs[:nrow + npar]]
        res = fn(*vals)
        out_refs = refs[nrow + npar:nrow + npar + nout]
        acc_refs = refs[nrow + npar + nout:]
        for o, v in zip(out_refs, res[:nout]):
            o[...] = v.astype(o.dtype)
        i = pl.program_id(1)
        for a, v in zip(acc_refs, res[nout:]):
            @pl.when(i == 0)
            def _(a=a, v=v):
                a[...] = v.astype(F32)

            @pl.when(i > 0)
            def _(a=a, v=v):
                a[...] += v.astype(F32)

    in_specs = [pl.BlockSpec((tm, w), functools.partial(lambda j, i, b: (i, b + j), b=b)) for _, w, b in rows]
    for arr, w in pars:
        if w is None:
            in_specs.append(pl.BlockSpec(arr.shape, lambda j, i: (0, 0)))
        else:
            in_specs.append(pl.BlockSpec((1, w), lambda j, i: (0, j)))
    out_specs = [pl.BlockSpec((tm, w), lambda j, i: (i, j)) for _, w, _ in outs]
    out_specs += [pl.BlockSpec((1, w), lambda j, i: (0, j)) for _, w in accs]
    out_shape = [jax.ShapeDtypeStruct((t, tw), dt) for tw, _, dt in outs]
    out_shape += [jax.ShapeDtypeStruct((1, tw), F32) for tw, _ in accs]
    res = pl.pallas_call(
        body, name=name, grid=(ncol, t // tm), in_specs=in_specs, out_specs=out_specs, out_shape=out_shape,
        compiler_params=_params(("parallel", "arbitrary" if accs else "parallel")),
    )(*[r[0] for r in rows], *[p[0] for p in pars])
    return res


def _rms(x, g):
    xf = x.astype(F32)
    return xf * lax.rsqrt(jnp.mean(xf * xf, axis=-1, keepdims=True) + NORM_EPS) * g


def _silu(x):
    return x * jax.nn.sigmoid(x)


def rms_fwd(name, x, g):
    d = x.shape[1]
    return _rowwise(name, lambda x, g: (_rms(x, g),), [(x, d, 0)], [(g, None)], [(d, d, MXU_DTYPE)], tm=512)[0]


def rms_bwd(name, x, g, dh, dres):
    d = x.shape[1]

    def fn(x, dh, dres, g):
        _, vjp = jax.vjp(_rms, x, g)
        dx, dg = vjp(dh.astype(F32))
        return dres + dx, dg

    return _rowwise(name, fn, [(x, d, 0), (dh, d, 0), (dres, d, 0)], [(g, None)], [(d, d, F32)], [(d, d)], tm=256)


def mm_post(name, a, w4, x, g_post, scale, g_next):
    t = a.shape[0]
    _, r, c = w4.shape
    tm = min(t, MM_TM // 2)
    chained = g_next is not None

    def body(a_ref, w_ref, x_ref, gp_ref, *rest):
        gn_ref, y_ref, xn_ref, h_ref = rest if chained else (None,) + rest + (None,)
        y = lax.dot_general(a_ref[...], w_ref[...].reshape(N_CHIP * r, c), NN, preferred_element_type=F32)
        y_ref[...] = y
        xn = x_ref[...] + scale * _rms(y, gp_ref[...])
        xn_ref[...] = xn
        if chained:
            h_ref[...] = _rms(xn, gn_ref[...]).astype(h_ref.dtype)

    def rows(width):
        return pl.BlockSpec((tm, width), lambda i: (i, 0))

    gain = pl.BlockSpec((1, c), lambda i: (0, 0))
    in_specs = [rows(N_CHIP * r), pl.BlockSpec((N_CHIP, r, c), lambda i: (0, 0, 0)), rows(c), gain]
    args = [a, w4, x, g_post]
    out_specs, out_shape = [rows(c), rows(c)], [jax.ShapeDtypeStruct((t, c), F32)] * 2
    if chained:
        in_specs.append(gain)
        args.append(g_next)
        out_specs.append(rows(c))
        out_shape.append(jax.ShapeDtypeStruct((t, c), MXU_DTYPE))
    res = pl.pallas_call(
        body, name=name, grid=(t // tm,), in_specs=in_specs, out_specs=out_specs, out_shape=out_shape,
        compiler_params=_params(("parallel",)),
    )(*args)
    return res[0], res[1], (res[2] if chained else None)


def post_bwd(name, y, g, dx, scale):
    d = y.shape[1]

    def fn(y, dx, g):
        _, vjp = jax.vjp(lambda y, g: scale * _rms(y, g), y, g)
        return vjp(dx)

    return _rowwise(name, fn, [(y, d, 0), (dx, d, 0)], [(g, None)], [(d, d, MXU_DTYPE)], [(d, d)], tm=256)


def ffn_up(name, h, w4):
    t = h.shape[0]
    _, r, c = w4.shape
    tm = min(t, MM_TM)
    tn = _tile(c, MM_TM)
    npj = c // tn
    half = N_CHIP // 2

    def body(h_ref, wg_ref, wu_ref, gu_ref, a_ref):
        hv = h_ref[...]
        g = lax.dot_general(hv, wg_ref[...], NN, preferred_element_type=F32)
        u = lax.dot_general(hv, wu_ref[...], NN, preferred_element_type=F32)
        gu_ref[0] = g.astype(gu_ref.dtype)
        gu_ref[1] = u.astype(gu_ref.dtype)
        a_ref[...] = (_silu(g) * u).astype(a_ref.dtype)

    f = half * c
    return pl.pallas_call(
        body, name=name, grid=(t // tm, half * npj),
        in_specs=[pl.BlockSpec((tm, r), lambda i, j: (i, 0)),
                  pl.BlockSpec((None, r, tn), lambda i, j: (j // npj, 0, j % npj)),
                  pl.BlockSpec((None, r, tn), lambda i, j: (half + j // npj, 0, j % npj))],
        out_specs=[pl.BlockSpec((2, tm, tn), lambda i, j: (0, i, j)), pl.BlockSpec((tm, tn), lambda i, j: (i, j))],
        out_shape=[jax.ShapeDtypeStruct((2, t, f), MXU_DTYPE), jax.ShapeDtypeStruct((t, f), MXU_DTYPE)],
        compiler_params=_params(("parallel", "parallel")),
    )(h, w4, w4)


def ffn_down_dx(name, dy, w4, gu):
    t = dy.shape[0]
    _, r, c = w4.shape
    tm = min(t, MM_TM)

    def body(dy_ref, w_ref, gu_ref, o_ref):
        dyv = dy_ref[...]
        for n0 in range(0, r, MM_SLICE):
            cols = pl.ds(n0, MM_SLICE)
            da = lax.dot_general(dyv, w_ref[cols, :], NT, preferred_element_type=F32)
            gate, up = gu_ref[0, :, cols].astype(F32), gu_ref[1, :, cols].astype(F32)
            sg = jax.nn.sigmoid(gate)
            silu = gate * sg
            o_ref[0, :, cols] = (da * up * (sg + silu * (1.0 - sg))).astype(o_ref.dtype)
            o_ref[1, :, cols] = (da * silu).astype(o_ref.dtype)

    return pl.pallas_call(
        body, name=name, grid=(t // tm, N_CHIP),
        in_specs=[pl.BlockSpec((tm, c), lambda i, j: (i, 0)), pl.BlockSpec((None, r, c), lambda i, j: (j, 0, 0)),
                  pl.BlockSpec((2, tm, r), lambda i, j: (0, i, j))],
        out_specs=pl.BlockSpec((2, tm, r), lambda i, j: (0, i, j)),
        out_shape=jax.ShapeDtypeStruct((2, t, N_CHIP * r), MXU_DTYPE),
        compiler_params=_params(("parallel", "parallel")),
    )(dy, w4, gu)


def _head_gate(o, g):
    mu = jnp.mean(o, axis=-1, keepdims=True)
    var = jnp.mean(jnp.square(o - mu), axis=-1, keepdims=True)
    return _silu(g.astype(F32)) * ((o - mu) * lax.rsqrt(var + LN_EPS))


def head_gate_fwd(name, o, p, gate_blk):
    dv = RET_V_DIM
    return _rowwise(name, lambda o, g: (_head_gate(o, g),), [(o, dv, 0), (p, dv, gate_blk)], [],
                    [(RET_HEADS * dv, dv, MXU_DTYPE)], tm=512, ncol=RET_HEADS)[0]


def head_gate_bwd(name, o, p, gate_blk, da):
    dv = RET_V_DIM

    def fn(o, g, da):
        _, vjp = jax.vjp(_head_gate, o, g.astype(F32))
        return vjp(da.astype(F32))

    w = RET_HEADS * dv
    return _rowwise(name, fn, [(o, dv, 0), (p, dv, gate_blk), (da, dv, 0)], [],
                    [(w, dv, MXU_DTYPE), (w, dv, MXU_DTYPE)], tm=512, ncol=RET_HEADS)


def _ln_silu(u, g, b):
    mu = jnp.mean(u, axis=-1, keepdims=True)
    var = jnp.mean(jnp.square(u - mu), axis=-1, keepdims=True)
    return _silu((u - mu) * lax.rsqrt(var + LN_EPS) * g + b)


def ln_silu_fwd(name, u, g, b):
    d = u.shape[1]
    return _rowwise(name, lambda u, g, b: (_ln_silu(u, g, b),), [(u, d, 0)], [(g, None), (b, None)],
                    [(d, d, MXU_DTYPE)], tm=512)[0]


def ln_silu_bwd(name, u, g, b, dc):
    d = u.shape[1]

    def fn(u, dc, g, b):
        _, vjp = jax.vjp(_ln_silu, u, g, b)
        return vjp(dc.astype(F32))

    return _rowwise(name, fn, [(u, d, 0), (dc, d, 0)], [(g, None), (b, None)], [(d, d, F32)], [(d, d), (d, d)],
                    tm=256)


def _merge(g0, g1, g2, ya, yb, yc):
    s = jax.nn.sigmoid
    return s(g0.astype(F32)) * ya + s(g1.astype(F32)) * yb + s(g2.astype(F32)) * yc


def merge_fwd(name, p, blk, ya, yb, yc):
    d = ya.shape[1]
    rows = [(p, d, blk), (p, d, blk + 1), (p, d, blk + 2), (ya, d, 0), (yb, d, 0), (yc, d, 0)]
    return _rowwise(name, lambda *v: (_merge(*v),), rows, [], [(d, d, MXU_DTYPE)], tm=256)[0]


def merge_bwd(name, p, blk, ya, yb, yc, dmg):
    d = ya.shape[1]

    def fn(g0, g1, g2, ya, yb, yc, dmg):
        _, vjp = jax.vjp(_merge, g0.astype(F32), g1.astype(F32), g2.astype(F32), ya, yb, yc)
        return vjp(dmg.astype(F32))

    rows = [(p, d, blk), (p, d, blk + 1), (p, d, blk + 2), (ya, d, 0), (yb, d, 0), (yc, d, 0), (dmg, d, 0)]
    return _rowwise(name, fn, rows, [], [(d, d, MXU_DTYPE)] * 6, tm=256)


def concat_cols(name, pieces):
    t = pieces[0].shape[0]
    widths = [p.shape[1] for p in pieces]
    tm = 256

    def body(*refs):
        o_ref, at = refs[-1], 0
        for r, w in zip(refs[:-1], widths):
            o_ref[:, at:at + w] = r[...]
            at += w

    return pl.pallas_call(
        body, name=name, grid=(t // tm,),
        in_specs=[pl.BlockSpec((tm, w), lambda i: (i, 0)) for w in widths],
        out_specs=pl.BlockSpec((tm, sum(widths)), lambda i: (i, 0)),
        out_shape=jax.ShapeDtypeStruct((t, sum(widths)), pieces[0].dtype),
        compiler_params=_params(("parallel",)),
    )(*pieces)


def loss_head(name, y, target):
    t, d = y.shape
    tm = 512

    def body(y_ref, t_ref, dy_ref, loss_ref):
        err = y_ref[...] - t_ref[...]
        dy_ref[...] = err * (1.0 / d)
        part = jnp.sum(jnp.sum(err * err, axis=1, keepdims=True), axis=0, keepdims=True) * (0.5 / d)

        @pl.when(pl.program_id(0) == 0)
        def _():
            loss_ref[...] = part

        @pl.when(pl.program_id(0) > 0)
        def _():
            loss_ref[...] += part

    return pl.pallas_call(
        body, name=name, grid=(t // tm,),
        in_specs=[pl.BlockSpec((tm, d), lambda i: (i, 0))] * 2,
        out_specs=[pl.BlockSpec((tm, d), lambda i: (i, 0)), pl.BlockSpec((1, 1), lambda i: (0, 0))],
        out_shape=[jax.ShapeDtypeStruct((t, d), F32), jax.ShapeDtypeStruct((1, 1), F32)],
        compiler_params=_params(("arbitrary",)),
    )(y, target)


def _rot(x, cos2, sin2):
    return x * cos2 + pltpu.roll(x, RET_QK_DIM // 2, 1) * sin2


def _decay_mask(lg, n0, rows, cols):
    n = n0 + lax.broadcasted_iota(jnp.int32, (rows, cols), 0)
    m = lax.broadcasted_iota(jnp.int32, (rows, cols), 1)
    shift = CHUNK.bit_length() - 1
    dist = jnp.abs(n - m).astype(F32)
    return jnp.where((m >> shift) <= (n >> shift), jnp.exp(lg * dist), 0.0)


def _ret_specs(s):
    dk, dv, h = RET_QK_DIM, RET_V_DIM, RET_HEADS
    return [
        pl.BlockSpec((s, dk), lambda b, hh: (b, hh)),
        pl.BlockSpec((s, dk), lambda b, hh: (b, h + hh)),
        pl.BlockSpec((s, dv), lambda b, hh: (b, (2 * h * dk) // dv + hh)),
        pl.BlockSpec((s, dk), lambda b, hh: (b, 0)),
        pl.BlockSpec((s, dk), lambda b, hh: (b, 0)),
        pl.BlockSpec((None, 1, dk), lambda b, hh: (hh, 0, 0)),
    ]


def retention_fwd(name, p, cos2, sin2, log_g, nb, s):
    dk, dv, h = RET_QK_DIM, RET_V_DIM, RET_HEADS

    def body(q_ref, k_ref, v_ref, cos_ref, sin_ref, lg_ref, o_ref, kr_ref):
        lg = lg_ref[0:1, 0:1]
        kr = _rot(k_ref[...].astype(F32), cos_ref[...], sin_ref[...]) * (dk ** -0.5)
        kr_ref[...] = kr.astype(kr_ref.dtype)
        for qi in range(s // RET_TQ):
            n0, kmax = qi * RET_TQ, (qi + 1) * RET_TQ
            rows = pl.ds(n0, RET_TQ)
            qr = _rot(q_ref[rows, :].astype(F32), cos_ref[rows, :], sin_ref[rows, :]).astype(MXU_DTYPE)
            sc = lax.dot_general(qr, kr_ref[0:kmax, :], NT, preferred_element_type=F32)
            pm = (sc * _decay_mask(lg, n0, RET_TQ, kmax)).astype(MXU_DTYPE)
            o_ref[rows, :] = lax.dot_general(pm, v_ref[0:kmax, :], NN, preferred_element_type=F32)

    return pl.pallas_call(
        body, name=name, grid=(nb, h), in_specs=_ret_specs(s),
        out_specs=pl.BlockSpec((s, dv), lambda b, hh: (b, hh)),
        out_shape=jax.ShapeDtypeStruct((nb * s, h * dv), F32),
        scratch_shapes=[pltpu.VMEM((s, dk), MXU_DTYPE)],
        compiler_params=_params(("parallel", "parallel")),
    )(p, p, p, cos2, sin2, log_g)


def retention_bwd(name, p, cos2, sin2, log_g, do, nb, s):
    dk, dv, h = RET_QK_DIM, RET_V_DIM, RET_HEADS

    def body(q_ref, k_ref, v_ref, cos_ref, sin_ref, lg_ref, do_ref, dq_ref, dk_ref, dv_ref, kr_ref, dk_acc, dv_acc):
        lg = lg_ref[0:1, 0:1]
        kr = _rot(k_ref[...].astype(F32), cos_ref[...], sin_ref[...]) * (dk ** -0.5)
        kr_ref[...] = kr.astype(kr_ref.dtype)
        dk_acc[...] = jnp.zeros_like(dk_acc)
        dv_acc[...] = jnp.zeros_like(dv_acc)
        for qi in range(s // RET_TQ):
            n0, kmax = qi * RET_TQ, (qi + 1) * RET_TQ
            rows = pl.ds(n0, RET_TQ)
            cq, sq = cos_ref[rows, :], sin_ref[rows, :]
            qr = _rot(q_ref[rows, :].astype(F32), cq, sq).astype(MXU_DTYPE)
            dob = do_ref[rows, :]
            mask = _decay_mask(lg, n0, RET_TQ, kmax)
            sc = lax.dot_general(qr, kr_ref[0:kmax, :], NT, preferred_element_type=F32)
            pm = (sc * mask).astype(MXU_DTYPE)
            dv_acc[0:kmax, :] += lax.dot_general(pm, dob, TN, preferred_element_type=F32)
            dp = lax.dot_general(dob, v_ref[0:kmax, :], NT, preferred_element_type=F32)
            ds = (dp * mask).astype(MXU_DTYPE)
            dqr = lax.dot_general(ds, kr_ref[0:kmax, :], NN, preferred_element_type=F32)
            dq_ref[rows, :] = _rot(dqr, cq, -sq).astype(dq_ref.dtype)
            dk_acc[0:kmax, :] += lax.dot_general(ds, qr, TN, preferred_element_type=F32)
        dkr = dk_acc[...] * (dk ** -0.5)
        dk_ref[...] = _rot(dkr, cos_ref[...], -sin_ref[...]).astype(dk_ref.dtype)
        dv_ref[...] = dv_acc[...].astype(dv_ref.dtype)

    t = nb * s
    return pl.pallas_call(
        body, name=name, grid=(nb, h),
        in_specs=_ret_specs(s) + [pl.BlockSpec((s, dv), lambda b, hh: (b, hh))],
        out_specs=[pl.BlockSpec((s, dk), lambda b, hh: (b, hh)), pl.BlockSpec((s, dk), lambda b, hh: (b, hh)),
                   pl.BlockSpec((s, dv), lambda b, hh: (b, hh))],
        out_shape=[jax.ShapeDtypeStruct((t, h * dk), MXU_DTYPE), jax.ShapeDtypeStruct((t, h * dk), MXU_DTYPE),
                   jax.ShapeDtypeStruct((t, h * dv), MXU_DTYPE)],
        scratch_shapes=[pltpu.VMEM((s, dk), MXU_DTYPE), pltpu.VMEM((s, dk), F32), pltpu.VMEM((s, dv), F32)],
        compiler_params=_params(("parallel", "parallel")),
    )(p, p, p, cos2, sin2, log_g, do)


def _conv_grid(t, d, nb):
    s = t // nb
    ns, nc = s // CONV_TS, d // CONV_TC
    return s, ns, nc


def _shifted(pad_ref, sh_ref, offsets):
    n = sh_ref.shape[1]
    for b in sorted({off % SUBLANES for off in offsets} - {0}):
        sh_ref[b - 1] = pad_ref[pl.ds(b, n), :]

    def read(off, r0):
        a, b = off - off % SUBLANES + r0, off % SUBLANES
        return pad_ref[pl.ds(a, CONV_ROWS), :] if b == 0 else sh_ref[b - 1, pl.ds(a, CONV_ROWS), :]

    return read


def _causal_taps(pad_ref, sh_ref, w_ref, k, emit):
    offs = [CONV_PAD - (k - 1) + j for j in range(k)]
    read = _shifted(pad_ref, sh_ref, offs)
    for r0 in range(0, CONV_TS, CONV_ROWS):
        acc = None
        for j in range(k):
            term = w_ref[j:j + 1, :] * read(offs[j], r0)
            acc = term if acc is None else acc + term
        emit(r0, acc)


def _carry_past(pad_ref, s_idx):
    @pl.when(s_idx == 0)
    def _():
        pad_ref[0:CONV_PAD, :] = jnp.zeros((CONV_PAD, pad_ref.shape[1]), F32)

    @pl.when(s_idx > 0)
    def _():
        pad_ref[0:CONV_PAD, :] = pad_ref[CONV_TS:CONV_TS + CONV_PAD, :]


def _carry_future(pad_ref, s_idx):
    @pl.when(s_idx == 0)
    def _():
        pad_ref[CONV_TS:CONV_TS + CONV_PAD, :] = jnp.zeros((CONV_PAD, pad_ref.shape[1]), F32)

    @pl.when(s_idx > 0)
    def _():
        pad_ref[CONV_TS:CONV_TS + CONV_PAD, :] = pad_ref[0:CONV_PAD, :]


def _conv_bwd_taps(pad_ref, sh_ref, w_ref, dw_acc, k, x_rows, emit, mix):
    read = _shifted(pad_ref, sh_ref, range(k))
    for r0 in range(0, CONV_TS, CONV_ROWS):
        ops = x_rows(r0)
        x = mix(ops)
        acc = None
        for j in range(k):
            sh = read(k - 1 - j, r0)
            term = w_ref[j:j + 1, :] * sh
            acc = term if acc is None else acc + term
            prod = x * sh
            part = prod[0:SUBLANES]
            for q in range(SUBLANES, CONV_ROWS, SUBLANES):
                part = part + prod[q:q + SUBLANES]
            dw_acc[j] += part
        emit(r0, ops, acc)


def _conv_bwd_edges(dw_acc, dw_ref, nb, ns, extra=()):
    first = jnp.logical_and(pl.program_id(1) == 0, pl.program_id(2) == 0)
    last = jnp.logical_and(pl.program_id(1) == nb - 1, pl.program_id(2) == ns - 1)

    @pl.when(first)
    def _():
        dw_acc[...] = jnp.zeros_like(dw_acc)
        for r in extra:
            r[...] = jnp.zeros_like(r)

    def finish():
        @pl.when(last)
        def _():
            dw_ref[...] = jnp.sum(dw_acc[...], axis=1)

    return finish


def short_conv_fwd(name, p, blk_b, w, nb):
    t = p.shape[0]
    d = w.shape[1]
    s, ns, nc = _conv_grid(t, d, nb)
    cb = d // CONV_TC

    def body(b_ref, c_ref, x_ref, w_ref, y_ref, cz_ref, pad_ref, sh_ref):
        _carry_past(pad_ref, pl.program_id(2))
        pad_ref[CONV_PAD:CONV_PAD + CONV_TS, :] = c_ref[...].astype(F32) * x_ref[...].astype(F32)

        def emit(r0, cz):
            rows = pl.ds(r0, CONV_ROWS)
            cz_ref[rows, :] = cz
            y_ref[rows, :] = (b_ref[rows, :].astype(F32) * cz).astype(y_ref.dtype)

        _causal_taps(pad_ref, sh_ref, w_ref, SC_KERNEL, emit)

    def pspec(off):
        return pl.BlockSpec((CONV_TS, CONV_TC), lambda c, b, si: (b * ns + si, (blk_b + off) * cb + c))

    ospec = pl.BlockSpec((CONV_TS, CONV_TC), lambda c, b, si: (b * ns + si, c))
    return pl.pallas_call(
        body, name=name, grid=(nc, nb, ns),
        in_specs=[pspec(0), pspec(1), pspec(2), pl.BlockSpec((SC_KERNEL, CONV_TC), lambda c, b, si: (0, c))],
        out_specs=[ospec, ospec],
        out_shape=[jax.ShapeDtypeStruct((t, d), MXU_DTYPE), jax.ShapeDtypeStruct((t, d), F32)],
        scratch_shapes=CONV_SCRATCH,
        compiler_params=_params(("parallel", "arbitrary", "arbitrary")),
    )(p, p, p, w)


def short_conv_bwd(name, p, blk_b, w, cz, dy, nb):
    t = p.shape[0]
    d = w.shape[1]
    s, ns, nc = _conv_grid(t, d, nb)
    cb = d // CONV_TC

    def body(b_ref, c_ref, x_ref, w_ref, cz_ref, dy_ref, db_ref, dc_ref, dx_ref, dw_ref, pad_ref, sh_ref, dw_acc):
        _carry_future(pad_ref, pl.program_id(2))
        dyv = dy_ref[...].astype(F32)
        db_ref[...] = (dyv * cz_ref[...]).astype(db_ref.dtype)
        pad_ref[0:CONV_TS, :] = dyv * b_ref[...].astype(F32)
        finish = _conv_bwd_edges(dw_acc, dw_ref, nb, ns)

        def x_rows(r0):
            rows = pl.ds(r0, CONV_ROWS)
            return c_ref[rows, :].astype(F32), x_ref[rows, :].astype(F32)

        def emit(r0, cx, dz):
            rows = pl.ds(r0, CONV_ROWS)
            dc_ref[rows, :] = (dz * cx[1]).astype(dc_ref.dtype)
            dx_ref[rows, :] = (dz * cx[0]).astype(dx_ref.dtype)

        _conv_bwd_taps(pad_ref, sh_ref, w_ref, dw_acc, SC_KERNEL, x_rows, emit, lambda cx: cx[0] * cx[1])
        finish()

    def row(b, si):
        return b * ns + (ns - 1 - si)

    def pspec(off):
        return pl.BlockSpec((CONV_TS, CONV_TC), lambda c, b, si: (row(b, si), (blk_b + off) * cb + c))

    ospec = pl.BlockSpec((CONV_TS, CONV_TC), lambda c, b, si: (row(b, si), c))
    wspec = pl.BlockSpec((SC_KERNEL, CONV_TC), lambda c, b, si: (0, c))
    return pl.pallas_call(
        body, name=name, grid=(nc, nb, ns),
        in_specs=[pspec(0), pspec(1), pspec(2), wspec, ospec, ospec],
        out_specs=[ospec, ospec, ospec, wspec],
        out_shape=[jax.ShapeDtypeStruct((t, d), MXU_DTYPE)] * 3 + [jax.ShapeDtypeStruct((SC_KERNEL, d), F32)],
        scratch_shapes=CONV_SCRATCH + [pltpu.VMEM((SC_KERNEL, SUBLANES, CONV_TC), F32)],
        compiler_params=_params(("parallel", "arbitrary", "arbitrary")),
    )(p, p, p, w, cz, dy)


def conformer_conv_fwd(name, p, blk_a, w, bias, nb):
    t = p.shape[0]
    d = w.shape[1]
    s, ns, nc = _conv_grid(t, d, nb)
    cb = d // CONV_TC

    def body(a_ref, b_ref, w_ref, bias_ref, u_ref, pad_ref, sh_ref):
        _carry_past(pad_ref, pl.program_id(2))
        pad_ref[CONV_PAD:CONV_PAD + CONV_TS, :] = a_ref[...].astype(F32) * jax.nn.sigmoid(b_ref[...].astype(F32))

        def emit(r0, u):
            u_ref[pl.ds(r0, CONV_ROWS), :] = u + bias_ref[...]

        _causal_taps(pad_ref, sh_ref, w_ref, CF_KERNEL, emit)

    def pspec(off):
        return pl.BlockSpec((CONV_TS, CONV_TC), lambda c, b, si: (b * ns + si, (blk_a + off) * cb + c))

    return pl.pallas_call(
        body, name=name, grid=(nc, nb, ns),
        in_specs=[pspec(0), pspec(1), pl.BlockSpec((CF_KERNEL, CONV_TC), lambda c, b, si: (0, c)),
                  pl.BlockSpec((1, CONV_TC), lambda c, b, si: (0, c))],
        out_specs=pl.BlockSpec((CONV_TS, CONV_TC), lambda c, b, si: (b * ns + si, c)),
        out_shape=jax.ShapeDtypeStruct((t, d), F32),
        scratch_shapes=CONV_SCRATCH,
        compiler_params=_params(("parallel", "arbitrary", "arbitrary")),
    )(p, p, w, bias)


def conformer_conv_bwd(name, p, blk_a, w, du, nb):
    t = p.shape[0]
    d = w.shape[1]
    s, ns, nc = _conv_grid(t, d, nb)
    cb = d // CONV_TC

    def body(a_ref, b_ref, w_ref, du_ref, da_ref, db_ref, dw_ref, dbias_ref, pad_ref, sh_ref, dw_acc):
        _carry_future(pad_ref, pl.program_id(2))
        duv = du_ref[...]
        pad_ref[0:CONV_TS, :] = duv
        finish = _conv_bwd_edges(dw_acc, dw_ref, nb, ns, extra=(dbias_ref,))
        dbias_ref[...] += jnp.sum(duv, axis=0, keepdims=True)

        def x_rows(r0):
            rows = pl.ds(r0, CONV_ROWS)
            return a_ref[rows, :].astype(F32), jax.nn.sigmoid(b_ref[rows, :].astype(F32))

        def emit(r0, asg, du0):
            rows = pl.ds(r0, CONV_ROWS)
            av, sg = asg
            da_ref[rows, :] = (du0 * sg).astype(da_ref.dtype)
            db_ref[rows, :] = (du0 * av * sg * (1.0 - sg)).astype(db_ref.dtype)

        _conv_bwd_taps(pad_ref, sh_ref, w_ref, dw_acc, CF_KERNEL, x_rows, emit, lambda asg: asg[0] * asg[1])
        finish()

    def row(b, si):
        return b * ns + (ns - 1 - si)

    def pspec(off):
        return pl.BlockSpec((CONV_TS, CONV_TC), lambda c, b, si: (row(b, si), (blk_a + off) * cb + c))

    ospec = pl.BlockSpec((CONV_TS, CONV_TC), lambda c, b, si: (row(b, si), c))
    wspec = pl.BlockSpec((CF_KERNEL, CONV_TC), lambda c, b, si: (0, c))
    bspec = pl.BlockSpec((1, CONV_TC), lambda c, b, si: (0, c))
    return pl.pallas_call(
        body, name=name, grid=(nc, nb, ns),
        in_specs=[pspec(0), pspec(1), wspec, ospec],
        out_specs=[ospec, ospec, wspec, bspec],
        out_shape=[jax.ShapeDtypeStruct((t, d), MXU_DTYPE)] * 2
        + [jax.ShapeDtypeStruct((CF_KERNEL, d), F32), jax.ShapeDtypeStruct((1, d), F32)],
        scratch_shapes=CONV_SCRATCH + [pltpu.VMEM((CF_KERNEL, SUBLANES, CONV_TC), F32)],
        compiler_params=_params(("parallel", "arbitrary", "arbitrary")),
    )(p, p, w, du)


BLOCKS = ("ffn1", "mixer", "ffn2")
BLOCK_WEIGHTS = {"ffn1": ("ffn1_w_gu", "ffn1_w_down"), "mixer": ("w_in", "w_ret_o", "w_sc_o", "w_cf_o", "w_o"),
                 "ffn2": ("ffn2_w_gu", "ffn2_w_down")}
BIG = BLOCK_WEIGHTS["ffn1"] + BLOCK_WEIGHTS["mixer"] + BLOCK_WEIGHTS["ffn2"]
MODE = {"ffn1_w_gu": "col", "ffn1_w_down": "row", "w_in": "col", "w_ret_o": "row", "w_sc_o": "row",
        "w_cf_o": "row", "w_o": "row", "ffn2_w_gu": "col", "ffn2_w_down": "row"}
NORM_OF = {"ffn1": 0, "mixer": 2, "ffn2": 4}
BLK_GATE, BLK_SCB, BLK_CFA, BLK_MERGE = 2, 3, 6, 8


def _rope_tables(positions):
    half = RET_QK_DIM // 2
    inv_freq = ROPE_BASE ** (-jnp.arange(half, dtype=F32) / half)
    ang = positions.astype(F32)[..., None] * inv_freq
    cos, sin = jnp.cos(ang), jnp.sin(ang)
    nb, s = positions.shape
    cos2 = jnp.concatenate([cos, cos], axis=-1).reshape(nb * s, RET_QK_DIM)
    sin2 = jnp.concatenate([-sin, sin], axis=-1).reshape(nb * s, RET_QK_DIM)
    return cos2, sin2


def _log_gamma():
    lg = jnp.log(1.0 - 2.0 ** (-5.0 - jnp.arange(RET_HEADS, dtype=F32)))
    return jnp.broadcast_to(lg[:, None, None], (RET_HEADS, 1, RET_QK_DIM))


def _ffn_fwd(xs, h, w, tag, g_post, g_next):
    gu, a = ffn_up("ffn_up", h, w[tag + "_w_gu"])
    y, out, h_next = mm_post("ffn_down", a, w[tag + "_w_down"], xs, g_post, 0.5, g_next)
    return out, h_next, dict(x=xs, h=h, gu=gu, a=a, y=y, w=w)


def _pinned(g, token):
    return g if token is None else g + token


def _ffn_bwd(dxs, dy, sv, tag, g_pre, push, prev):
    w = sv["w"]
    gu_w, down_w = w[tag + "_w_gu"], w[tag + "_w_down"]
    dgu = ffn_down_dx("ffn_down_dx", dy, down_w, sv["gu"])
    grads = {tag + "_w_down": mm_dw("ffn_down_dw", sv["a"], dy, "row", down_w.shape),
             tag + "_w_gu": mm_dw("ffn_gu_dw", sv["h"], dgu, "col", gu_w.shape)}
    g_pre = _pinned(g_pre, push(grads))
    return mm_dx_norms("ffn_gu_dx", dgu, gu_w, sv["x"], g_pre, dxs, prev)


def _mixer_fwd(xs, h, w, sm, g_post, g_next, rope, nb, s, mid):
    cos2, sin2, log_g = rope
    d = xs.shape[1]
    gate_blk = (BLK_GATE * d) // RET_V_DIM
    p = mm_fwd("mx_in", h, w["w_in"], "col", MXU_DTYPE)
    if mid is not None:
        sm = dict(sm, cf_dw_b=sm["cf_dw_b"] + mid(p))
    o = retention_fwd("ret_fwd", p, cos2, sin2, log_g, nb, s)
    ya_in = head_gate_fwd("ret_gate", o, p, gate_blk)
    yb_in, cz = short_conv_fwd("sc_fwd", p, BLK_SCB, sm["sc_conv_w"], nb)
    u1 = conformer_conv_fwd("cf_fwd", p, BLK_CFA, sm["cf_dw_w"], sm["cf_dw_b"], nb)
    yc_in = ln_silu_fwd("cf_ln", u1, sm["cf_ln_g"], sm["cf_ln_b"])
    ya = mm_fwd("mx_proj", ya_in, w["w_ret_o"], "row", F32)
    yb = mm_fwd("mx_proj", yb_in, w["w_sc_o"], "row", F32)
    yc = mm_fwd("mx_proj", yc_in, w["w_cf_o"], "row", F32)
    mg = merge_fwd("mx_merge", p, BLK_MERGE, ya, yb, yc)
    m, out, h_next = mm_post("mx_out", mg, w["w_o"], xs, g_post, 1.0, g_next)
    return out, h_next, dict(x=xs, h=h, p=p, o=o, ya_in=ya_in, yb_in=yb_in, cz=cz, u1=u1, yc_in=yc_in, ya=ya, yb=yb, yc=yc,
                     mg=mg, m=m, w=w)


def _mixer_bwd(dxs, dm, sv, sm, g_pre, rope, nb, s, push, prev):
    cos2, sin2, log_g = rope
    w, p = sv["w"], sv["p"]
    d = dxs.shape[1]
    gate_blk = (BLK_GATE * d) // RET_V_DIM
    grads, gsm = {}, {}

    def proj_bwd(wname, a_in, dy, out_dtype):
        grads[wname] = mm_dw("mx_proj_dw", a_in, dy, "row", w[wname].shape)
        return mm_dx("mx_proj_dx", dy, w[wname], "row", out_dtype)

    dmg = proj_bwd("w_o", sv["mg"], dm, MXU_DTYPE)
    dg0, dg1, dg2, dya, dyb, dyc = merge_bwd("mx_merge_bwd", p, BLK_MERGE, sv["ya"], sv["yb"], sv["yc"], dmg)
    dya_in = proj_bwd("w_ret_o", sv["ya_in"], dya, MXU_DTYPE)
    dyb_in = proj_bwd("w_sc_o", sv["yb_in"], dyb, MXU_DTYPE)
    dyc_in = proj_bwd("w_cf_o", sv["yc_in"], dyc, MXU_DTYPE)
    do, dgret = head_gate_bwd("ret_gate_bwd", sv["o"], p, gate_blk, dya_in)
    dq, dk, dv = retention_bwd("ret_bwd", p, cos2, sin2, log_g, do, nb, s)
    dscb, dscc, dscx, gsm["sc_conv_w"] = short_conv_bwd("sc_bwd", p, BLK_SCB, sm["sc_conv_w"], sv["cz"], dyb_in, nb)
    du1, dlg, dlb = ln_silu_bwd("cf_ln_bwd", sv["u1"], sm["cf_ln_g"], sm["cf_ln_b"], dyc_in)
    dcfa, dcfb, gsm["cf_dw_w"], dbias = conformer_conv_bwd("cf_bwd", p, BLK_CFA, sm["cf_dw_w"], du1, nb)
    gsm.update(cf_ln_g=dlg[0], cf_ln_b=dlb[0], cf_dw_b=dbias[0])
    dp = concat_cols("mx_dp", [dq, dk, dv, dgret, dscb, dscc, dscx, dcfa, dcfb, dg0, dg1, dg2])
    grads["w_in"] = mm_dw("mx_in_dw", sv["h"], dp, "col", w["w_in"].shape)
    g_pre = _pinned(g_pre, push(grads))
    return mm_dx_norms("mx_in_dx", dp, w["w_in"], sv["x"], g_pre, dxs, prev) + (gsm,)


def local_step(x, positions, target, small, fetch, push):
    nb, s, d = x.shape
    t = nb * s
    depth = small["norm_g"].shape[0]
    rope = _rope_tables(positions) + (_log_gamma(),)
    xs = x.reshape(t, d)
    token = [None]

    def gain(l, i):
        g = small["norm_g"][l, i][None, :]
        if token[0] is not None:
            g, token[0] = g + token[0], None
        return g

    def mixer_small(l):
        return dict(sc_conv_w=small["sc_conv_w"][l], cf_dw_w=small["cf_dw_w"][l], cf_dw_b=small["cf_dw_b"][l][None, :],
                    cf_ln_g=small["cf_ln_g"][l][None, :], cf_ln_b=small["cf_ln_b"][l][None, :])

    saved = {}
    order = [(l, blk) for l in range(depth) for blk in BLOCKS]
    h = None
    for at, (l, blk) in enumerate(order):
        w, token[0], mid = fetch(l, blk, xs)
        i0 = NORM_OF[blk]
        if h is None:
            h = rms_fwd("first_rms", xs, gain(l, i0))
        g_post = gain(l, i0 + 1)
        g_next = gain(order[at + 1][0], NORM_OF[order[at + 1][1]]) if at + 1 < len(order) else None
        if blk == "mixer":
            xs, h, saved[l, blk] = _mixer_fwd(xs, h, w, mixer_small(l), g_post, g_next, rope, nb, s, mid)
        else:
            xs, h, saved[l, blk] = _ffn_fwd(xs, h, w, blk, g_post, g_next)

    dxs, loss = loss_head("loss", xs, target.reshape(t, d))

    dnorm = [[None] * 6 for _ in range(depth)]
    gsmall = {n: [None] * depth for n in ("sc_conv_w", "cf_dw_w", "cf_dw_b", "cf_ln_g", "cf_ln_b")}
    def branch(group):
        l, blk = group
        sv = saved[group]
        return (sv["m"], gain(l, NORM_OF[blk] + 1), 1.0) if blk == "mixer" else (sv["y"], gain(l, NORM_OF[blk] + 1), 0.5)

    l, blk = order[-1]
    y, g_post, scale = branch(order[-1])
    dy, dnorm[l][NORM_OF[blk] + 1] = post_bwd("last_post_bwd", y, g_post, dxs, scale)
    for at in reversed(range(len(order))):
        l, blk = order[at]
        i0 = NORM_OF[blk]
        prev = branch(order[at - 1]) if at > 0 else None
        put = functools.partial(push, l, blk)
        if blk == "mixer":
            dxs, dnorm[l][i0], dy, dg_prev, gsm = _mixer_bwd(
                dxs, dy, saved[l, blk], mixer_small(l), gain(l, i0), rope, nb, s, put, prev)
            for n, v in gsm.items():
                gsmall[n][l] = v
        else:
            dxs, dnorm[l][i0], dy, dg_prev = _ffn_bwd(dxs, dy, saved[l, blk], blk, gain(l, i0), put, prev)
        if at > 0:
            dnorm[order[at - 1][0]][NORM_OF[order[at - 1][1]] + 1] = dg_prev

    gs = {n: jnp.stack(v) for n, v in gsmall.items()}
    gs["norm_g"] = jnp.stack([jnp.concatenate(r, axis=0) for r in dnorm])
    return loss, dxs.reshape(nb, s, d), gs


ANY = pl.BlockSpec(memory_space=pl.ANY)
HBM = pl.BlockSpec(memory_space=pltpu.HBM)
SEM = pl.BlockSpec(memory_space=pltpu.SEMAPHORE)
VMEM_WHOLE = pl.BlockSpec(memory_space=pltpu.VMEM)
EFFECT = pltpu.SideEffectType.DATAFLOW_SIDE_EFFECTING
TOKEN = jax.ShapeDtypeStruct((8, 128), F32)


def _other_chips(x, y):
    return [(1 - x, y), (x, 1 - y), (1 - x, 1 - y)]


def _remote(src, dst, send_sem, recv_sem, to):
    return pltpu.make_async_remote_copy(src_ref=src, dst_ref=dst, send_sem=send_sem, recv_sem=recv_sem,
                                        device_id=to, device_id_type=MESH)


def _in_hbm(v):
    return pltpu.with_memory_space_constraint(v, pltpu.HBM)


def place_quarters(ws, layer, ids, after):
    m = len(ws)

    def body(ids_ref, *refs):
        for w_ref, o_ref in zip(refs[:m], refs[m + 1:]):
            o_ref[...] = w_ref[...].astype(o_ref.dtype)

    def spec(w, where):
        return pl.BlockSpec((None, w.shape[1] // STREAM_STEPS, w.shape[2]), where)

    return pl.pallas_call(
        body, name="place_quarters",
        grid_spec=pltpu.PrefetchScalarGridSpec(
            num_scalar_prefetch=1, grid=(STREAM_STEPS,),
            in_specs=[spec(w, lambda i, ids_ref: (layer, i, 0)) for w in ws] + [ANY],
            out_specs=[spec(w, lambda i, ids_ref: (ids_ref[0], i, 0)) for w in ws]),
        out_shape=[jax.ShapeDtypeStruct((N_CHIP,) + w.shape[1:], MXU_DTYPE) for w in ws],
        compiler_params=_params(("parallel",)),
    )(ids, *ws, after)


def _gather_copies(lands, send, recv):
    x, y, c = _axes()
    me = 2 * x + y
    mine, theirs = [], []
    for a, ld in enumerate(lands):
        rh = ld.shape[1] // 2
        rows = pl.ds(c * rh, rh)
        for k, (px, py) in enumerate(_other_chips(x, y)):
            to = (px, py, c)
            mine.append(_remote(ld.at[me, rows, :], ld.at[me, rows, :], send.at[3 * a + k], recv.at[3 * a + k], to))
            got = ld.at[2 * px + py, rows, :]
            theirs.append(_remote(got, got, send.at[3 * a + k], recv.at[3 * a + k], to))
    return mine, theirs


def gather_start(name, groups, after):
    flat = [s for g in groups for s in g]
    n, ng = len(flat), len(groups)
    sizes = [len(g) for g in groups]

    def body(*refs):
        lands = refs[:n]
        sems = refs[n + 1:n + 1 + 2 * ng]
        token = refs[-1]
        at = 0
        for g, m in enumerate(sizes):
            mine, _ = _gather_copies(lands[at:at + m], sems[2 * g], sems[2 * g + 1])
            for cp in mine:
                cp.start()
            at += m
        token[...] = jnp.zeros_like(token)

    sem_shapes = []
    for m in sizes:
        sem_shapes += [pltpu.SemaphoreType.DMA((3 * m,))] * 2
    res = pl.pallas_call(
        body, name=name, in_specs=[HBM] * n + [ANY],
        out_specs=[SEM] * (2 * ng) + [HBM] * n + [VMEM_WHOLE],
        out_shape=sem_shapes + [pltpu.HBM(s.shape, s.dtype) for s in flat] + [TOKEN],
        input_output_aliases={i: 2 * ng + i for i in range(n)},
        compiler_params=pltpu.CompilerParams(has_side_effects=EFFECT),
    )(*[_in_hbm(s) for s in flat], after)
    sems, thru, token = res[:2 * ng], res[2 * ng:2 * ng + n], res[-1]
    out, at = [], 0
    for g, m in enumerate(sizes):
        out.append((sems[2 * g], sems[2 * g + 1], thru[at:at + m]))
        at += m
    return out, token


def gather_wait(lands, send, recv, after):
    m = len(lands)

    def body(*refs):
        mine, theirs = _gather_copies(refs[:m], refs[m], refs[m + 1])
        for cp in mine:
            cp.wait_send()
        for cp in theirs:
            cp.wait_recv()

    return pl.pallas_call(
        body, name="gather_wait", in_specs=[HBM] * m + [SEM, SEM, ANY], out_specs=[HBM] * m,
        out_shape=[pltpu.HBM(l.shape, l.dtype) for l in lands],
        input_output_aliases={i: i for i in range(m)},
        compiler_params=pltpu.CompilerParams(has_side_effects=EFFECT),
    )(*lands, send, recv, after)


def copy_start(name, bufs, copies, ncopy, after=()):
    n, k = len(bufs), len(after)

    def body(*refs):
        for cp in copies(refs[:n], refs[n + k], refs[n + k + 1])[0]:
            cp.start()
        refs[-1][...] = jnp.zeros_like(refs[-1])

    res = pl.pallas_call(
        body, name=name, in_specs=[HBM] * n + [ANY] * k, out_specs=[SEM, SEM] + [HBM] * n + [VMEM_WHOLE],
        out_shape=[pltpu.SemaphoreType.DMA((ncopy,))] * 2 + [pltpu.HBM(b.shape, b.dtype) for b in bufs] + [TOKEN],
        input_output_aliases={i: 2 + i for i in range(n)},
        compiler_params=pltpu.CompilerParams(has_side_effects=EFFECT),
    )(*[_in_hbm(b) for b in bufs], *after)
    return res[0], res[1], list(res[2:2 + n]), res[-1]


def copy_wait(name, bufs, send, recv, copies, after=()):
    n = len(bufs)

    def body(*refs):
        mine, theirs = copies(refs[:n], refs[n], refs[n + 1])
        for cp in mine:
            cp.wait_send()
        for cp in theirs:
            cp.wait_recv()

    return list(pl.pallas_call(
        body, name=name, in_specs=[HBM] * n + [SEM, SEM] + [ANY] * len(after), out_specs=[HBM] * n,
        out_shape=[pltpu.HBM(b.shape, b.dtype) for b in bufs], input_output_aliases={i: i for i in range(n)},
        compiler_params=pltpu.CompilerParams(has_side_effects=EFFECT),
    )(*bufs, send, recv, *after))


def _fill_copies(lands, send, recv):
    x, y, c = _axes()
    sib = (x, y, 1 - c)
    mine, theirs = [], []
    for a, ld in enumerate(lands):
        rh = ld.shape[1] // 2
        for k, (px, py) in enumerate(_other_chips(x, y)):
            got = ld.at[2 * px + py, pl.ds(c * rh, rh), :]
            mine.append(_remote(got, got, send.at[3 * a + k], recv.at[3 * a + k], sib))
            blk = ld.at[2 * px + py, pl.ds((1 - c) * rh, rh), :]
            theirs.append(_remote(blk, blk, send.at[3 * a + k], recv.at[3 * a + k], sib))
    return mine, theirs


def _presum_copies(grads, lands, send, recv):
    x, y, c = _axes()
    cps = []
    for a, (g, ld) in enumerate(zip(grads, lands)):
        rh = g.shape[1] // 2
        cps.append(_remote(g.at[:, pl.ds((1 - c) * rh, rh), :], ld, send.at[a], recv.at[a], (x, y, 1 - c)))
    return cps


def presum_start(grads):
    m = len(grads)

    def body(*refs):
        for cp in _presum_copies(refs[:m], refs[m:2 * m], refs[2 * m], refs[2 * m + 1]):
            cp.start()
        refs[-1][...] = jnp.zeros_like(refs[-1])

    lands = [lax.empty((g.shape[0], g.shape[1] // 2, g.shape[2]), g.dtype) for g in grads]
    res = pl.pallas_call(
        body, name="presum_start", in_specs=[HBM] * (2 * m), out_specs=[SEM, SEM] + [HBM] * (2 * m) + [VMEM_WHOLE],
        out_shape=[pltpu.SemaphoreType.DMA((m,))] * 2 + [pltpu.HBM(g.shape, g.dtype) for g in grads]
        + [pltpu.HBM(l.shape, l.dtype) for l in lands] + [TOKEN],
        input_output_aliases={i: 2 + i for i in range(2 * m)},
        compiler_params=pltpu.CompilerParams(has_side_effects=EFFECT),
    )(*[_in_hbm(g) for g in grads], *[_in_hbm(l) for l in lands])
    return res[0], res[1], res[2:2 + m], res[2 + m:2 + 2 * m], res[-1]


def presum_wait(grads, lands, send, recv, after):
    m = len(grads)

    def body(*refs):
        for cp in _presum_copies(refs[:m], refs[m:2 * m], refs[2 * m], refs[2 * m + 1]):
            cp.wait_send()
            cp.wait_recv()

    res = pl.pallas_call(
        body, name="presum_wait", in_specs=[HBM] * (2 * m) + [SEM, SEM] + [ANY] * len(after),
        out_specs=[HBM] * (2 * m),
        out_shape=[pltpu.HBM(g.shape, g.dtype) for g in grads] + [pltpu.HBM(l.shape, l.dtype) for l in lands],
        input_output_aliases={i: i for i in range(2 * m)},
        compiler_params=pltpu.CompilerParams(has_side_effects=EFFECT),
    )(*grads, *lands, send, recv, *after)
    return res[:m], res[m:]


def add_halves(gs, lands, ids):
    m = len(gs)

    def body(ids_ref, *refs):
        for a_ref, b_ref, o_ref in zip(refs[:m], refs[m:2 * m], refs[2 * m:]):
            o_ref[...] = (a_ref[...].astype(F32) + b_ref[...].astype(F32)).astype(o_ref.dtype)

    def spec(ld, where):
        return pl.BlockSpec((None,) + ld.shape[1:], where)

    return pl.pallas_call(
        body, name="add_halves",
        grid_spec=pltpu.PrefetchScalarGridSpec(
            num_scalar_prefetch=1, grid=(N_CHIP,),
            in_specs=[spec(ld, lambda i, ids_ref: (i, ids_ref[1], 0)) for ld in lands]
            + [spec(ld, lambda i, ids_ref: (i, 0, 0)) for ld in lands],
            out_specs=[spec(ld, lambda i, ids_ref: (i, 0, 0)) for ld in lands]),
        out_shape=[jax.ShapeDtypeStruct(ld.shape, ld.dtype) for ld in lands],
        compiler_params=_params(("parallel",)),
    )(ids, *gs, *lands)


def _scatter_copies(parts, lands, send, recv):
    x, y, c = _axes()
    cps = []
    for a, (pt, ld) in enumerate(zip(parts, lands)):
        for k, (px, py) in enumerate(_other_chips(x, y)):
            cps.append(_remote(pt.at[2 * px + py], ld.at[k], send.at[3 * a + k], recv.at[3 * a + k], (px, py, c)))
    return cps


def scatter_start(parts):
    m = len(parts)

    def body(*refs):
        for cp in _scatter_copies(refs[:m], refs[m:2 * m], refs[2 * m], refs[2 * m + 1]):
            cp.start()
        refs[-1][...] = jnp.zeros_like(refs[-1])

    lands = [lax.empty((N_CHIP - 1,) + p.shape[1:], p.dtype) for p in parts]
    res = pl.pallas_call(
        body, name="scatter_start", in_specs=[HBM] * (2 * m), out_specs=[SEM, SEM] + [HBM] * (2 * m) + [VMEM_WHOLE],
        out_shape=[pltpu.SemaphoreType.DMA((3 * m,))] * 2 + [pltpu.HBM(p.shape, p.dtype) for p in parts]
        + [pltpu.HBM(l.shape, l.dtype) for l in lands] + [TOKEN],
        input_output_aliases={i: 2 + i for i in range(2 * m)},
        compiler_params=pltpu.CompilerParams(has_side_effects=EFFECT),
    )(*[_in_hbm(p) for p in parts], *[_in_hbm(l) for l in lands])
    return res[0], res[1], res[2:2 + m], res[2 + m:2 + 2 * m], res[-1]


def scatter_wait(parts, lands, send, recv, after):
    m = len(parts)

    def body(*refs):
        for cp in _scatter_copies(refs[:m], refs[m:2 * m], refs[2 * m], refs[2 * m + 1]):
            cp.wait_send()
            cp.wait_recv()

    res = pl.pallas_call(
        body, name="scatter_wait", in_specs=[HBM] * (2 * m) + [SEM, SEM] + [ANY] * len(after),
        out_specs=[HBM] * (2 * m),
        out_shape=[pltpu.HBM(p.shape, p.dtype) for p in parts] + [pltpu.HBM(l.shape, l.dtype) for l in lands],
        input_output_aliases={i: i for i in range(2 * m)},
        compiler_params=pltpu.CompilerParams(has_side_effects=EFFECT),
    )(*parts, *lands, send, recv, *after)
    return res[:m], res[m:]


def sum_partials(parts, lands, ids, layer, depth, intos):
    m = len(parts)
    nt = STREAM_STEPS

    def body(ids_ref, *refs):
        for p_ref, l_ref, o_ref in zip(refs[:m], refs[m:2 * m], refs[-m:]):
            acc = p_ref[...].astype(F32)
            for k in range(N_CHIP - 1):
                acc = acc + l_ref[k].astype(F32)
            o_ref[...] = acc

    def rows(p):
        return p.shape[1] // nt

    in_specs = [pl.BlockSpec((None, rows(p), p.shape[2]), lambda i, ids_ref: (ids_ref[0], i, 0)) for p in parts]
    in_specs += [pl.BlockSpec((N_CHIP - 1, rows(p), p.shape[2]), lambda i, ids_ref: (0, i, 0)) for p in parts]
    args = [ids, *parts, *lands]
    aliases = {}
    if intos is not None:
        in_specs += [ANY] * m
        args += list(intos)
        aliases = {1 + 2 * m + a: a for a in range(m)}
    return pl.pallas_call(
        body, name="sum_partials",
        grid_spec=pltpu.PrefetchScalarGridSpec(
            num_scalar_prefetch=1, grid=(nt,), in_specs=in_specs,
            out_specs=[pl.BlockSpec((None, rows(p), p.shape[2]), lambda i, ids_ref: (layer, ids_ref[1] * nt + i, 0))
                       for p in parts]),
        out_shape=[jax.ShapeDtypeStruct((depth, 2 * p.shape[1], p.shape[2]), F32) for p in parts],
        input_output_aliases=aliases, compiler_params=_params(("parallel",)),
    )(*args)


def _final_copies(layer):
    def copies(bufs, send, recv):
        x, y, c = _axes()
        sib = (x, y, 1 - c)
        mine, theirs = [], []
        for a, buf in enumerate(bufs):
            rh = buf.shape[1] // 2
            src = buf.at[layer, pl.ds(c * rh, rh), :]
            mine.append(_remote(src, src, send.at[a], recv.at[a], sib))
            dst = buf.at[layer, pl.ds((1 - c) * rh, rh), :]
            theirs.append(_remote(dst, dst, send.at[a], recv.at[a], sib))
        return mine, theirs

    return copies


def allgather_small(pk):
    def body(in_ref, out_ref, send, recv):
        x, y, c = _axes()
        me = 2 * x + y
        chips = _other_chips(x, y)
        out_ref[pl.ds(me, 1)] = in_ref[...][None]
        cps = []
        for k, (px, py) in enumerate(chips):
            cp = _remote(in_ref, out_ref.at[me], send.at[k], recv.at[k], (px, py, c))
            cp.start()
            cps.append(cp)
        for k, (px, py) in enumerate(chips):
            got = out_ref.at[2 * px + py]
            _remote(got, got, send.at[k], recv.at[k], (px, py, c)).wait_recv()
        for cp in cps:
            cp.wait_send()

    return pl.pallas_call(
        body, name="allgather_small", in_specs=[VMEM_WHOLE], out_specs=VMEM_WHOLE,
        out_shape=jax.ShapeDtypeStruct((N_CHIP,) + pk.shape, pk.dtype),
        scratch_shapes=[pltpu.SemaphoreType.DMA((3,))] * 2,
    )(pk)


N_DEV = 8


def _small_copies(bufs, send, recv):
    g, slots = bufs
    x, y, c = _axes()
    me = 4 * x + 2 * y + c
    mine, theirs = [], []
    for mask in range(1, N_DEV):
        px = 1 - x if mask & 4 else x
        py = 1 - y if mask & 2 else y
        pc = 1 - c if mask & 1 else c
        mine.append(_remote(g, slots.at[me], send.at[mask - 1], recv.at[mask - 1], (px, py, pc)))
        got = slots.at[4 * px + 2 * py + pc]
        theirs.append(_remote(got, got, send.at[mask - 1], recv.at[mask - 1], (px, py, pc)))
    return mine, theirs


def sum_slots(g, slots, me):
    def body(me_ref, g_ref, slots_ref, o_ref):
        acc = None
        for d in range(N_DEV):
            term = jnp.where(me_ref[0] == d, g_ref[...], slots_ref[d])
            acc = term if acc is None else acc + term
        o_ref[...] = acc

    return pl.pallas_call(
        body, name="sum_slots",
        grid_spec=pltpu.PrefetchScalarGridSpec(
            num_scalar_prefetch=1, grid=(1,),
            in_specs=[pl.BlockSpec(g.shape, lambda i, me_ref: (0, 0)),
                      pl.BlockSpec(slots.shape, lambda i, me_ref: (0, 0, 0))],
            out_specs=pl.BlockSpec(g.shape, lambda i, me_ref: (0, 0))),
        out_shape=jax.ShapeDtypeStruct(g.shape, g.dtype),
        compiler_params=_params(("arbitrary",)),
    )(me, g, slots)


def adamw(w, g, m, v, layer=None, intos=None):
    shape = w.shape
    cols = shape[-1]
    rows = int(np.prod(shape[:-1]))
    span = rows if layer is None else rows // shape[0]
    tr = span
    for cand in (256, 128):
        if span % cand == 0 and cand * cols * 4 <= 2 * 1024 * 1024:
            tr = cand
            break
    first = 0 if layer is None else layer * (span // tr)
    c1 = 1.0 - ADAM_B1 ** ADAM_STEP
    c2 = 1.0 - ADAM_B2 ** ADAM_STEP

    def body(w_ref, g_ref, m_ref, v_ref, *rest):
        d_ref, nm_ref, nv_ref, g_out = rest[-4:]
        gv = g_ref[...]
        g_out[...] = gv
        nm = ADAM_B1 * m_ref[...] + (1.0 - ADAM_B1) * gv
        nv = ADAM_B2 * v_ref[...] + (1.0 - ADAM_B2) * jnp.square(gv)
        d_ref[...] = -ADAM_LR * ((nm / c1) / (jnp.sqrt(nv / c2) + ADAM_EPS) + ADAM_WD * w_ref[...])
        nm_ref[...] = nm
        nv_ref[...] = nv

    spec = pl.BlockSpec((tr, cols), lambda i: (first + i, 0))
    args = [a.reshape(rows, cols) for a in (w, g, m, v)]
    in_specs, aliases = [spec] * 4, {}
    if intos is not None:
        args += [a.reshape(rows, cols) for a in intos]
        in_specs += [ANY] * 4
        aliases = {4 + k: k for k in range(4)}
    res = pl.pallas_call(
        body, name="adamw", grid=(span // tr,), in_specs=in_specs, out_specs=[spec] * 4,
        out_shape=[jax.ShapeDtypeStruct((rows, cols), F32)] * 4, input_output_aliases=aliases,
        compiler_params=_params(("parallel",)),
    )(*args)
    return [r.reshape(shape) for r in res]


WEIGHTS = ("norm_g", "ffn1_w_gu", "ffn1_w_down", "w_in", "w_ret_o", "sc_conv_w", "w_sc_o", "cf_dw_w", "cf_dw_b",
           "cf_ln_g", "cf_ln_b", "w_cf_o", "w_o", "ffn2_w_gu", "ffn2_w_down")
SHARDED_SMALL = ("norm_g", "sc_conv_w", "cf_dw_w")
REPLICATED_SMALL = ("cf_dw_b", "cf_ln_g", "cf_ln_b")

def _pack_rows(parts):
    padded, offs, at = [], [], 0
    for p in parts:
        r = -(-p.shape[0] // SUBLANES) * SUBLANES
        padded.append(jnp.pad(p, ((0, r - p.shape[0]), (0, 0))))
        offs.append(at)
        at += r
    return jnp.concatenate(padded, axis=0), offs


def kernel(x, positions, norm_g, ffn1_w_gu, ffn1_w_down, w_in, w_ret_o, sc_conv_w, w_sc_o, cf_dw_w, cf_dw_b, cf_ln_g, cf_ln_b, w_cf_o, w_o, ffn2_w_gu, ffn2_w_down, loss_target, m_norm_g, m_ffn1_w_gu, m_ffn1_w_down, m_w_in, m_w_ret_o, m_sc_conv_w, m_w_sc_o, m_cf_dw_w, m_cf_dw_b, m_cf_ln_g, m_cf_ln_b, m_w_cf_o, m_w_o, m_ffn2_w_gu, m_ffn2_w_down, v_norm_g, v_ffn1_w_gu, v_ffn1_w_down, v_w_in, v_w_ret_o, v_sc_conv_w, v_w_sc_o, v_cf_dw_w, v_cf_dw_b, v_cf_ln_g, v_cf_ln_b, v_w_cf_o, v_w_o, v_ffn2_w_gu, v_ffn2_w_down):
    wts = dict(zip(WEIGHTS, (norm_g, ffn1_w_gu, ffn1_w_down, w_in, w_ret_o, sc_conv_w, w_sc_o, cf_dw_w, cf_dw_b,
                             cf_ln_g, cf_ln_b, w_cf_o, w_o, ffn2_w_gu, ffn2_w_down)))
    mom = dict(zip(WEIGHTS, (m_norm_g, m_ffn1_w_gu, m_ffn1_w_down, m_w_in, m_w_ret_o, m_sc_conv_w, m_w_sc_o,
                             m_cf_dw_w, m_cf_dw_b, m_cf_ln_g, m_cf_ln_b, m_w_cf_o, m_w_o, m_ffn2_w_gu, m_ffn2_w_down)))
    var = dict(zip(WEIGHTS, (v_norm_g, v_ffn1_w_gu, v_ffn1_w_down, v_w_in, v_w_ret_o, v_sc_conv_w, v_w_sc_o,
                             v_cf_dw_w, v_cf_dw_b, v_cf_ln_g, v_cf_ln_b, v_w_cf_o, v_w_o, v_ffn2_w_gu, v_ffn2_w_down)))
    depth = norm_g.shape[0]
    dq = norm_g.shape[-1]
    d = N_CHIP * dq
    chip = 2 * lax.axis_index("x") + lax.axis_index("y")
    ids = jnp.stack([chip, lax.axis_index("c")]).astype(jnp.int32)

    pk, offs = _pack_rows([wts[n].reshape(-1, dq) for n in SHARDED_SMALL])
    gk4 = allgather_small(pk)
    gk = gk4.transpose(1, 0, 2).reshape(pk.shape[0], d)
    small = {n: wts[n] for n in REPLICATED_SMALL}
    for n, o in zip(SHARDED_SMALL, offs):
        rows = wts[n].shape[0] * wts[n].shape[1]
        small[n] = gk[o:o + rows].reshape(wts[n].shape[:2] + (d,))

    order = [(l, blk) for l in range(depth) for blk in BLOCKS]
    def placed(groups, after):
        return [place_quarters([wts[n] for n in BLOCK_WEIGHTS[blk]], l, ids, after) for l, blk in groups]

    first, token = gather_start("gather_start_first", placed(order[:1], gk4), gk4)
    rest, token = gather_start("gather_start_rest", placed(order[1:], token), token)
    started = dict(zip(order, first + rest))
    small["norm_g"] = small["norm_g"] + token[0:1, 0:1]

    filling = {}

    def fill(group, after):
        send, recv, lands = started[group]
        lands = gather_wait(lands, send, recv, after)
        send, recv, lands, tok = copy_start("fill_start", lands, _fill_copies, 3 * len(lands))
        filling[group] = (send, recv, lands)
        return tok[0:1, 0:1]

    def fetch(l, blk, after):
        at = order.index((l, blk))
        if (l, blk) not in filling:
            fill((l, blk), token if at == 0 else after)
        send, recv, lands = filling.pop((l, blk))
        lands = copy_wait("fill_wait", lands, send, recv, _fill_copies, (after,))
        tok, mid = None, None
        if at == 1:
            mid = functools.partial(fill, order[at + 1])
        elif 1 < at < len(order) - 1:
            tok = fill(order[at + 1], lands[0])
        return dict(zip(BLOCK_WEIGHTS[blk], lands)), tok, mid

    gsum = {n: None for n in BIG}
    presums, scatters, finals = [], [], []

    def scatter_next(after):
        group, gl, lands, send, recv = presums.pop(0)
        gl, lands = presum_wait(gl, lands, send, recv, after)
        send, recv, parts, lands, tok = scatter_start(add_halves(gl, lands, ids))
        scatters.append((group, parts, lands, send, recv))
        return tok

    def sum_next(after):
        (l, blk), parts, lands, send, recv = scatters.pop(0)
        parts, lands = scatter_wait(parts, lands, send, recv, after)
        names = BLOCK_WEIGHTS[blk]
        intos = None if gsum[names[0]] is None else [gsum[n] for n in names]
        sums = sum_partials(parts, lands, ids, l, depth, intos)
        send, recv, sums, tok = copy_start("final_start", sums, _final_copies(l), len(sums))
        gsum.update(zip(names, sums))
        finals.append((names, l, send, recv))
        return tok

    def final_next(after):
        names, l, send, recv = finals.pop(0)
        gsum.update(zip(names, copy_wait("final_wait", [gsum[n] for n in names], send, recv, _final_copies(l), after)))

    def push(l, blk, grads):
        send, recv, gl, lands, tok = presum_start([grads[n] for n in BLOCK_WEIGHTS[blk]])
        if scatters:
            tok = tok + sum_next((gl[0],))
        if presums:
            tok = tok + scatter_next((gl[0],))
        presums.append(((l, blk), gl, lands, send, recv))
        return tok[0:1, 0:1]

    loss, grad_x, gs = local_step(x, positions, loss_target, small, fetch, push)

    names = SHARDED_SMALL + REPLICATED_SMALL
    pg, offs = _pack_rows([gs[n].reshape(-1, d) for n in names])
    s_send, s_recv, s_bufs, tok = copy_start("small_start", [pg, lax.empty((N_DEV,) + pg.shape, pg.dtype)],
                                             _small_copies, N_DEV - 1, (grad_x,))
    tok = scatter_next((grad_x, tok))

    delta, new_m, new_v, grads = {}, {}, {}, {}

    def update(n, layer=None):
        g = gsum[n] if n in BIG else grads[n]
        prev = [delta[n], new_m[n], new_v[n], grads[n]] if layer is not None and n in delta else None
        delta[n], new_m[n], new_v[n], grads[n] = adamw(wts[n], g, mom[n], var[n], layer, prev)

    while finals and finals[0][1] > 0:
        done, l = finals[0][:2]
        final_next((tok,))
        for n in done:
            update(n, l)
    upper = tuple(delta[n] for n in BIG if n in delta)
    pg, slots = copy_wait("small_wait", s_bufs, s_send, s_recv, _small_copies, upper + (tok,))
    me = (2 * chip + lax.axis_index("c")).astype(jnp.int32).reshape(1)
    tot = sum_slots(pg, slots, me)
    for n, o in zip(names, offs):
        rows = int(np.prod(gs[n].shape[:-1]))
        full = tot[o:o + rows]
        if n in SHARDED_SMALL:
            full = lax.dynamic_slice_in_dim(full, chip * dq, dq, axis=1)
        grads[n] = full.reshape(wts[n].shape)

    for n in names:
        update(n)
    after = tuple(delta[n] for n in names)
    while scatters or finals:
        if scatters:
            after = (sum_next(after),)
        done, l = finals[0][:2]
        final_next(after)
        for n in done:
            update(n, l)
        after = tuple(delta[n] for n in done)

    loss_all = lax.psum(loss[0, 0], ("x", "y", "c"))
    return (loss_all, grad_x, *[grads[n] for n in WEIGHTS], *[delta[n] for n in WEIGHTS],
            *[new_m[n] for n in WEIGHTS], *[new_v[n] for n in WEIGHTS])
```

```python
import functools

import jax
import jax.numpy as jnp
import numpy as np
from jax import lax
from jax.experimental import pallas as pl
from jax.experimental.pallas import tpu as pltpu

F32 = jnp.float32
BF16 = jnp.bfloat16
MXU_DTYPE = BF16
VMEM_LIMIT_BYTES = 56 * 1024 * 1024
MESH = pl.DeviceIdType.MESH

N_CHIP = 4
CHUNK = 64
RET_HEADS = 4
RET_QK_DIM = 128
RET_V_DIM = 256
SC_KERNEL = 3
CF_KERNEL = 31
ROPE_BASE = 10000.0
NORM_EPS = 1e-6
LN_EPS = 1e-5
ADAM_LR = 0.001
ADAM_B1 = 0.9
ADAM_B2 = 0.999
ADAM_EPS = 1e-08
ADAM_WD = 0.01
ADAM_STEP = 10

SUBLANES = 8
CONV_PAD = 32
CONV_TS = 128
CONV_TC = 512
CONV_ROWS = 32
CONV_TILES = range(0, CONV_ROWS, SUBLANES)
CONV_SCRATCH = [pltpu.VMEM((CONV_TS + CONV_PAD, CONV_TC), F32),
                pltpu.VMEM((SUBLANES - 1, CONV_TS + CONV_PAD - SUBLANES, CONV_TC), F32)]
RET_TQ = 512
MM_TM = 1024
MM_TN = 1536
MM_K1 = 1024
MM_W1 = 8 << 20
MM_SLICE = 256
MM_IN_BYTES = 36 << 20
STREAM_STEPS = 2


def _params(sem):
    return pltpu.CompilerParams(dimension_semantics=sem, vmem_limit_bytes=VMEM_LIMIT_BYTES)


def _axes():
    return lax.axis_index("x"), lax.axis_index("y"), lax.axis_index("c")


NN = (((1,), (0,)), ((), ()))
NT = (((1,), (1,)), ((), ()))
TN = (((0,), (0,)), ((), ()))


def _mm(name, a, b, out_shape, out_dtype, grid, a_spec, b_spec, o_spec, dims, acc_shape):
    nk = grid[2]

    def body(a_ref, b_ref, o_ref, *scratch):
        bv = b_ref[...]
        if bv.ndim == 3:
            bv = bv.reshape(-1, bv.shape[-1])
        part = lax.dot_general(a_ref[...], bv, dims, preferred_element_type=F32)

        def put(v):
            o_ref[...] = v.reshape(o_ref.shape).astype(o_ref.dtype)

        if nk == 1:
            put(part)
        else:
            acc = scratch[0]
            k = pl.program_id(2)

            @pl.when(k == 0)
            def _():
                acc[...] = part

            @pl.when(k > 0)
            def _():
                acc[...] += part

            @pl.when(k == nk - 1)
            def _():
                put(acc[...])

    scratch = [pltpu.VMEM(acc_shape, F32)] if nk > 1 else []
    return pl.pallas_call(
        body, name=name, grid=grid, in_specs=[a_spec, b_spec], out_specs=o_spec,
        out_shape=jax.ShapeDtypeStruct(out_shape, out_dtype), scratch_shapes=scratch,
        compiler_params=_params(("parallel", "parallel", "arbitrary")),
    )(a, b)


def _tile(n, target):
    best = None
    for t in range(128, min(n, target) + 1, 128):
        if n % t == 0:
            best = t
    assert best is not None, (n, target)
    return best


def _token_rows(t, width):
    tt = t
    while tt > MM_TM and tt * width * jnp.dtype(MXU_DTYPE).itemsize * 2 > MM_IN_BYTES:
        tt //= 2
    return tt


def mm_fwd(name, a, w4, mode, out_dtype):
    t = a.shape[0]
    _, r, c = w4.shape
    tm = min(t, MM_TM)
    if mode == "col":
        tn = _tile(c, MM_TN)
        npj = c // tn
        grid = (t // tm, N_CHIP * npj, 1)
        a_spec = pl.BlockSpec((tm, r), lambda i, j, k: (i, 0))
        b_spec = pl.BlockSpec((None, r, tn), lambda i, j, k: (j // npj, 0, j % npj))
        o_spec = pl.BlockSpec((tm, tn), lambda i, j, k: (i, j))
        return _mm(name, a, w4, (t, N_CHIP * c), out_dtype, grid, a_spec, b_spec, o_spec, NN, (tm, tn))
    if w4.size * w4.dtype.itemsize <= MM_W1:
        grid = (t // tm, 1, 1)
        a_spec = pl.BlockSpec((tm, N_CHIP * r), lambda i, j, k: (i, 0))
        b_spec = pl.BlockSpec((N_CHIP, r, c), lambda i, j, k: (0, 0, 0))
        o_spec = pl.BlockSpec((tm, c), lambda i, j, k: (i, 0))
        return _mm(name, a, w4, (t, c), out_dtype, grid, a_spec, b_spec, o_spec, NN, (tm, c))
    grid = (t // tm, 1, N_CHIP)
    a_spec = pl.BlockSpec((tm, r), lambda i, j, k: (i, k))
    b_spec = pl.BlockSpec((None, r, c), lambda i, j, k: (k, 0, 0))
    o_spec = pl.BlockSpec((tm, c), lambda i, j, k: (i, 0))
    return _mm(name, a, w4, (t, c), out_dtype, grid, a_spec, b_spec, o_spec, NN, (tm, c))


def mm_dx(name, dy, w4, mode, out_dtype):
    t = dy.shape[-2]
    _, r, c = w4.shape
    tm = min(t, MM_TM)
    if mode == "col":
        tn, npj = c, 1
        hb = N_CHIP // 2 * npj
        grid = (t // tm, 1, N_CHIP * npj)
        if dy.ndim == 3:
            a_spec = pl.BlockSpec((None, tm, tn), lambda i, j, k: (k // hb, i, k % hb))
        else:
            a_spec = pl.BlockSpec((tm, tn), lambda i, j, k: (i, k))
        b_spec = pl.BlockSpec((None, r, tn), lambda i, j, k: (k // npj, 0, k % npj))
        o_spec = pl.BlockSpec((tm, r), lambda i, j, k: (i, 0))
        return _mm(name, dy, w4, (t, r), out_dtype, grid, a_spec, b_spec, o_spec, NT, (tm, r))
    if N_CHIP * r <= MM_K1:
        grid = (t // tm, 1, 1)
        a_spec = pl.BlockSpec((tm, c), lambda i, j, k: (i, 0))
        b_spec = pl.BlockSpec((N_CHIP, r, c), lambda i, j, k: (0, 0, 0))
        o_spec = pl.BlockSpec((tm, N_CHIP * r), lambda i, j, k: (i, 0))
        return _mm(name, dy, w4, (t, N_CHIP * r), out_dtype, grid, a_spec, b_spec, o_spec, NT, (tm, N_CHIP * r))
    grid = (t // tm, N_CHIP, 1)
    a_spec = pl.BlockSpec((tm, c), lambda i, j, k: (i, 0))
    b_spec = pl.BlockSpec((None, r, c), lambda i, j, k: (j, 0, 0))
    o_spec = pl.BlockSpec((tm, r), lambda i, j, k: (i, j))
    return _mm(name, dy, w4, (t, N_CHIP * r), out_dtype, grid, a_spec, b_spec, o_spec, NT, (tm, r))


def mm_dw(name, a, dy, mode, shape3):
    t = a.shape[0]
    _, r, c = shape3
    if mode == "col":
        tn = _tile(c, MM_TN)
        npj = c // tn
        tt = _token_rows(t, r + tn)
        grid = (1, N_CHIP * npj, t // tt)
        a_spec = pl.BlockSpec((tt, r), lambda i, j, k: (k, 0))
        hb = N_CHIP // 2 * npj
        if dy.ndim == 3:
            b_spec = pl.BlockSpec((None, tt, tn), lambda i, j, k: (j // hb, k, j % hb))
        else:
            b_spec = pl.BlockSpec((tt, tn), lambda i, j, k: (k, j))
        o_spec = pl.BlockSpec((None, r, tn), lambda i, j, k: (j // npj, 0, j % npj))
        return _mm(name, a, dy, shape3, MXU_DTYPE, grid, a_spec, b_spec, o_spec, TN, (r, tn))
    if N_CHIP * r <= MM_K1:
        tt = _token_rows(t, N_CHIP * r + c)
        grid = (1, 1, t // tt)
        a_spec = pl.BlockSpec((tt, N_CHIP * r), lambda i, j, k: (k, 0))
        b_spec = pl.BlockSpec((tt, c), lambda i, j, k: (k, 0))
        o_spec = pl.BlockSpec((N_CHIP, r, c), lambda i, j, k: (0, 0, 0))
        return _mm(name, a, dy, shape3, MXU_DTYPE, grid, a_spec, b_spec, o_spec, TN, (N_CHIP * r, c))
    tt = _token_rows(t, r + c)
    grid = (N_CHIP, 1, t // tt)
    a_spec = pl.BlockSpec((tt, r), lambda i, j, k: (k, i))
    b_spec = pl.BlockSpec((tt, c), lambda i, j, k: (k, 0))
    o_spec = pl.BlockSpec((None, r, c), lambda i, j, k: (i, 0, 0))
    return _mm(name, a, dy, shape3, MXU_DTYPE, grid, a_spec, b_spec, o_spec, TN, (r, c))


def _rms_bwd(x, g, dh):
    r = lax.rsqrt(jnp.mean(x * x, axis=-1, keepdims=True) + NORM_EPS)
    xhat = x * r
    dyg = dh * g
    dx = r * (dyg - xhat * jnp.mean(dyg * xhat, axis=-1, keepdims=True))
    return dx, jnp.sum(dh * xhat, axis=0, keepdims=True)


def mm_dx_norms(name, dy, w4, x, g_pre, dres, prev):
    t = dy.shape[-2]
    _, r, c = w4.shape
    tm = min(t, MM_TM // 2)
    hb = N_CHIP // 2
    chained = prev is not None
    nk = N_CHIP

    def body(dy_ref, w_ref, x_ref, dres_ref, g_ref, *rest):
        if chained:
            y_ref, gp_ref, dx_ref, dg_ref, dyp_ref, dgp_ref, acc = rest
        else:
            dx_ref, dg_ref, acc = rest
        i, k = pl.program_id(0), pl.program_id(1)
        part = lax.dot_general(dy_ref[...], w_ref[...], NT, preferred_element_type=F32)

        @pl.when(k == 0)
        def _():
            acc[...] = part

        @pl.when(k > 0)
        def _():
            acc[...] += part

        def add_to(ref, v):
            @pl.when(i == 0)
            def _():
                ref[...] = v

            @pl.when(i > 0)
            def _():
                ref[...] += v

        @pl.when(k == nk - 1)
        def _():
            dx, dg = _rms_bwd(x_ref[...], g_ref[...], acc[...])
            dxs = dres_ref[...] + dx
            dx_ref[...] = dxs
            add_to(dg_ref, dg)
            if chained:
                dyp, dgp = _rms_bwd(y_ref[...], gp_ref[...], dxs)
                dyp_ref[...] = (prev[2] * dyp).astype(dyp_ref.dtype)
                add_to(dgp_ref, prev[2] * dgp)

    if dy.ndim == 3:
        dy_spec = pl.BlockSpec((None, tm, c), lambda i, k: (k // hb, i, k % hb))
    else:
        dy_spec = pl.BlockSpec((tm, c), lambda i, k: (i, k))
    rows = pl.BlockSpec((tm, r), lambda i, k: (i, 0))
    gain = pl.BlockSpec((1, r), lambda i, k: (0, 0))
    in_specs = [dy_spec, pl.BlockSpec((None, r, c), lambda i, k: (k, 0, 0)), rows, rows, gain]
    args = [dy, w4, x, dres, g_pre]
    out_specs = [rows, gain]
    out_shape = [jax.ShapeDtypeStruct((t, r), F32), jax.ShapeDtypeStruct((1, r), F32)]
    if chained:
        in_specs += [rows, gain]
        args += [prev[0], prev[1]]
        out_specs += [rows, gain]
        out_shape += [jax.ShapeDtypeStruct((t, r), MXU_DTYPE), jax.ShapeDtypeStruct((1, r), F32)]
    res = pl.pallas_call(
        body, name=name, grid=(t // tm, nk), in_specs=in_specs, out_specs=out_specs, out_shape=out_shape,
        scratch_shapes=[pltpu.VMEM((tm, r), F32)], compiler_params=_params(("arbitrary", "arbitrary")),
    )(*args)
    return tuple(res) if chained else (res[0], res[1], None, None)


def _rowwise(name, fn, rows, pars, outs, accs=(), tm=256, ncol=1):
    t = rows[0][0].shape[0]
    nrow, npar, nout = len(rows), len(pars), len(outs)

    def body(*refs):
        vals = [r[...] for r in refs[:nrow + npar]]
        res = fn(*vals)
        out_refs = refs[nrow + npar:nrow + npar + nout]
        acc_refs = refs[nrow + npar + nout:]
        for o, v in zip(out_refs, res[:nout]):
            o[...] = v.astype(o.dtype)
        i = pl.program_id(1)
        for a, v in zip(acc_refs, res[nout:]):
            @pl.when(i == 0)
            def _(a=a, v=v):
                a[...] = v.astype(F32)

            @pl.when(i > 0)
            def _(a=a, v=v):
                a[...] += v.astype(F32)

    in_specs = [pl.BlockSpec((tm, w), functools.partial(lambda j, i, b: (i, b + j), b=b)) for _, w, b in rows]
    for arr, w in pars:
        if w is None:
            in_specs.append(pl.BlockSpec(arr.shape, lambda j, i: (0, 0)))
        else:
            in_specs.append(pl.BlockSpec((1, w), lambda j, i: (0, j)))
    out_specs = [pl.BlockSpec((tm, w), lambda j, i: (i, j)) for _, w, _ in outs]
    out_specs += [pl.BlockSpec((1, w), lambda j, i: (0, j)) for _, w in accs]
    out_shape = [jax.ShapeDtypeStruct((t, tw), dt) for tw, _, dt in outs]
    out_shape += [jax.ShapeDtypeStruct((1, tw), F32) for tw, _ in accs]
    res = pl.pallas_call(
        body, name=name, grid=(ncol, t // tm), in_specs=in_specs, out_specs=out_specs, out_shape=out_shape,
        compiler_params=_params(("parallel", "arbitrary" if accs else "parallel")),
    )(*[r[0] for r in rows], *[p[0] for p in pars])
    return res


def _rms(x, g):
    xf = x.astype(F32)
    return xf * lax.rsqrt(jnp.mean(xf * xf, axis=-1, keepdims=True) + NORM_EPS) * g


def _silu(x):
    return x * jax.nn.sigmoid(x)


def rms_fwd(name, x, g):
    d = x.shape[1]
    return _rowwise(name, lambda x, g: (_rms(x, g),), [(x, d, 0)], [(g, None)], [(d, d, MXU_DTYPE)], tm=512)[0]


def rms_bwd(name, x, g, dh, dres):
    d = x.shape[1]

    def fn(x, dh, dres, g):
        _, vjp = jax.vjp(_rms, x, g)
        dx, dg = vjp(dh.astype(F32))
        return dres + dx, dg

    return _rowwise(name, fn, [(x, d, 0), (dh, d, 0), (dres, d, 0)], [(g, None)], [(d, d, F32)], [(d, d)], tm=256)


def mm_post(name, a, w4, x, g_post, scale, g_next):
    t = a.shape[0]
    _, r, c = w4.shape
    tm = min(t, MM_TM // 2)
    chained = g_next is not None

    def body(a_ref, w_ref, x_ref, gp_ref, *rest):
        gn_ref, y_ref, xn_ref, h_ref = rest if chained else (None,) + rest + (None,)
        y = lax.dot_general(a_ref[...], w_ref[...].reshape(N_CHIP * r, c), NN, preferred_element_type=F32)
        y_ref[...] = y
        xn = x_ref[...] + scale * _rms(y, gp_ref[...])
        xn_ref[...] = xn
        if chained:
            h_ref[...] = _rms(xn, gn_ref[...]).astype(h_ref.dtype)

    def rows(width):
        return pl.BlockSpec((tm, width), lambda i: (i, 0))

    gain = pl.BlockSpec((1, c), lambda i: (0, 0))
    in_specs = [rows(N_CHIP * r), pl.BlockSpec((N_CHIP, r, c), lambda i: (0, 0, 0)), rows(c), gain]
    args = [a, w4, x, g_post]
    out_specs, out_shape = [rows(c), rows(c)], [jax.ShapeDtypeStruct((t, c), F32)] * 2
    if chained:
        in_specs.append(gain)
        args.append(g_next)
        out_specs.append(rows(c))
        out_shape.append(jax.ShapeDtypeStruct((t, c), MXU_DTYPE))
    res = pl.pallas_call(
        body, name=name, grid=(t // tm,), in_specs=in_specs, out_specs=out_specs, out_shape=out_shape,
        compiler_params=_params(("parallel",)),
    )(*args)
    return res[0], res[1], (res[2] if chained else None)


def post_bwd(name, y, g, dx, scale):
    d = y.shape[1]

    def fn(y, dx, g):
        _, vjp = jax.vjp(lambda y, g: scale * _rms(y, g), y, g)
        return vjp(dx)

    return _rowwise(name, fn, [(y, d, 0), (dx, d, 0)], [(g, None)], [(d, d, MXU_DTYPE)], [(d, d)], tm=256)


def ffn_up(name, h, w4):
    t = h.shape[0]
    _, r, c = w4.shape
    tm = min(t, MM_TM)
    tn = _tile(c, MM_TM)
    npj = c // tn
    half = N_CHIP // 2

    def body(h_ref, wg_ref, wu_ref, gu_ref, a_ref):
        hv = h_ref[...]
        g = lax.dot_general(hv, wg_ref[...], NN, preferred_element_type=F32)
        u = lax.dot_general(hv, wu_ref[...], NN, preferred_element_type=F32)
        gu_ref[0] = g.astype(gu_ref.dtype)
        gu_ref[1] = u.astype(gu_ref.dtype)
        a_ref[...] = (_silu(g) * u).astype(a_ref.dtype)

    f = half * c
    return pl.pallas_call(
        body, name=name, grid=(t // tm, half * npj),
        in_specs=[pl.BlockSpec((tm, r), lambda i, j: (i, 0)),
                  pl.BlockSpec((None, r, tn), lambda i, j: (j // npj, 0, j % npj)),
                  pl.BlockSpec((None, r, tn), lambda i, j: (half + j // npj, 0, j % npj))],
        out_specs=[pl.BlockSpec((2, tm, tn), lambda i, j: (0, i, j)), pl.BlockSpec((tm, tn), lambda i, j: (i, j))],
        out_shape=[jax.ShapeDtypeStruct((2, t, f), MXU_DTYPE), jax.ShapeDtypeStruct((t, f), MXU_DTYPE)],
        compiler_params=_params(("parallel", "parallel")),
    )(h, w4, w4)


def ffn_down_dx(name, dy, w4, gu):
    t = dy.shape[0]
    _, r, c = w4.shape
    tm = min(t, MM_TM)

    def body(dy_ref, w_ref, gu_ref, o_ref):
        dyv = dy_ref[...]
        for n0 in range(0, r, MM_SLICE):
            cols = pl.ds(n0, MM_SLICE)
            da = lax.dot_general(dyv, w_ref[cols, :], NT, preferred_element_type=F32)
            gate, up = gu_ref[0, :, cols].astype(F32), gu_ref[1, :, cols].astype(F32)
            sg = jax.nn.sigmoid(gate)
            silu = gate * sg
            o_ref[0, :, cols] = (da * up * (sg + silu * (1.0 - sg))).astype(o_ref.dtype)
            o_ref[1, :, cols] = (da * silu).astype(o_ref.dtype)

    return pl.pallas_call(
        body, name=name, grid=(t // tm, N_CHIP),
        in_specs=[pl.BlockSpec((tm, c), lambda i, j: (i, 0)), pl.BlockSpec((None, r, c), lambda i, j: (j, 0, 0)),
                  pl.BlockSpec((2, tm, r), lambda i, j: (0, i, j))],
        out_specs=pl.BlockSpec((2, tm, r), lambda i, j: (0, i, j)),
        out_shape=jax.ShapeDtypeStruct((2, t, N_CHIP * r), MXU_DTYPE),
        compiler_params=_params(("parallel", "parallel")),
    )(dy, w4, gu)


def _head_gate(o, g):
    mu = jnp.mean(o, axis=-1, keepdims=True)
    var = jnp.mean(jnp.square(o - mu), axis=-1, keepdims=True)
    return _silu(g.astype(F32)) * ((o - mu) * lax.rsqrt(var + LN_EPS))


def head_gate_fwd(name, o, p, gate_blk):
    dv = RET_V_DIM
    return _rowwise(name, lambda o, g: (_head_gate(o, g),), [(o, dv, 0), (p, dv, gate_blk)], [],
                    [(RET_HEADS * dv, dv, MXU_DTYPE)], tm=512, ncol=RET_HEADS)[0]


def head_gate_bwd(name, o, p, gate_blk, da):
    dv = RET_V_DIM

    def fn(o, g, da):
        _, vjp = jax.vjp(_head_gate, o, g.astype(F32))
        return vjp(da.astype(F32))

    w = RET_HEADS * dv
    return _rowwise(name, fn, [(o, dv, 0), (p, dv, gate_blk), (da, dv, 0)], [],
                    [(w, dv, MXU_DTYPE), (w, dv, MXU_DTYPE)], tm=512, ncol=RET_HEADS)


def _ln_silu(u, g, b):
    mu = jnp.mean(u, axis=-1, keepdims=True)
    var = jnp.mean(jnp.square(u - mu), axis=-1, keepdims=True)
    return _silu((u - mu) * lax.rsqrt(var + LN_EPS) * g + b)


def ln_silu_fwd(name, u, g, b):
    d = u.shape[1]
    return _rowwise(name, lambda u, g, b: (_ln_silu(u, g, b),), [(u, d, 0)], [(g, None), (b, None)],
                    [(d, d, MXU_DTYPE)], tm=512)[0]


def ln_silu_bwd(name, u, g, b, dc):
    d = u.shape[1]

    def fn(u, dc, g, b):
        _, vjp = jax.vjp(_ln_silu, u, g, b)
        return vjp(dc.astype(F32))

    return _rowwise(name, fn, [(u, d, 0), (dc, d, 0)], [(g, None), (b, None)], [(d, d, F32)], [(d, d), (d, d)],
                    tm=256)


def _merge(g0, g1, g2, ya, yb, yc):
    s = jax.nn.sigmoid
    return s(g0.astype(F32)) * ya + s(g1.astype(F32)) * yb + s(g2.astype(F32)) * yc


def merge_fwd(name, p, blk, ya, yb, yc):
    d = ya.shape[1]
    rows = [(p, d, blk), (p, d, blk + 1), (p, d, blk + 2), (ya, d, 0), (yb, d, 0), (yc, d, 0)]
    return _rowwise(name, lambda *v: (_merge(*v),), rows, [], [(d, d, MXU_DTYPE)], tm=256)[0]


def merge_bwd(name, p, blk, ya, yb, yc, dmg):
    d = ya.shape[1]

    def fn(g0, g1, g2, ya, yb, yc, dmg):
        _, vjp = jax.vjp(_merge, g0.astype(F32), g1.astype(F32), g2.astype(F32), ya, yb, yc)
        return vjp(dmg.astype(F32))

    rows = [(p, d, blk), (p, d, blk + 1), (p, d, blk + 2), (ya, d, 0), (yb, d, 0), (yc, d, 0), (dmg, d, 0)]
    return _rowwise(name, fn, rows, [], [(d, d, MXU_DTYPE)] * 6, tm=256)


def concat_cols(name, pieces):
    t = pieces[0].shape[0]
    widths = [p.shape[1] for p in pieces]
    tm = 256

    def body(*refs):
        o_ref, at = refs[-1], 0
        for r, w in zip(refs[:-1], widths):
            o_ref[:, at:at + w] = r[...]
            at += w

    return pl.pallas_call(
        body, name=name, grid=(t // tm,),
        in_specs=[pl.BlockSpec((tm, w), lambda i: (i, 0)) for w in widths],
        out_specs=pl.BlockSpec((tm, sum(widths)), lambda i: (i, 0)),
        out_shape=jax.ShapeDtypeStruct((t, sum(widths)), pieces[0].dtype),
        compiler_params=_params(("parallel",)),
    )(*pieces)


def loss_head(name, y, target):
    t, d = y.shape
    tm = 512

    def body(y_ref, t_ref, dy_ref, loss_ref):
        err = y_ref[...] - t_ref[...]
        dy_ref[...] = err * (1.0 / d)
        part = jnp.sum(jnp.sum(err * err, axis=1, keepdims=True), axis=0, keepdims=True) * (0.5 / d)

        @pl.when(pl.program_id(0) == 0)
        def _():
            loss_ref[...] = part

        @pl.when(pl.program_id(0) > 0)
        def _():
            loss_ref[...] += part

    return pl.pallas_call(
        body, name=name, grid=(t // tm,),
        in_specs=[pl.BlockSpec((tm, d), lambda i: (i, 0))] * 2,
        out_specs=[pl.BlockSpec((tm, d), lambda i: (i, 0)), pl.BlockSpec((1, 1), lambda i: (0, 0))],
        out_shape=[jax.ShapeDtypeStruct((t, d), F32), jax.ShapeDtypeStruct((1, 1), F32)],
        compiler_params=_params(("arbitrary",)),
    )(y, target)


def _rot(x, cos2, sin2):
    return x * cos2 + pltpu.roll(x, RET_QK_DIM // 2, 1) * sin2


def _decay_mask(lg, n0, rows, cols):
    n = n0 + lax.broadcasted_iota(jnp.int32, (rows, cols), 0)
    m = lax.broadcasted_iota(jnp.int32, (rows, cols), 1)
    shift = CHUNK.bit_length() - 1
    dist = jnp.abs(n - m).astype(F32)
    return jnp.where((m >> shift) <= (n >> shift), jnp.exp(lg * dist), 0.0)


def _ret_specs(s):
    dk, dv, h = RET_QK_DIM, RET_V_DIM, RET_HEADS
    return [
        pl.BlockSpec((s, dk), lambda b, hh: (b, hh)),
        pl.BlockSpec((s, dk), lambda b, hh: (b, h + hh)),
        pl.BlockSpec((s, dv), lambda b, hh: (b, (2 * h * dk) // dv + hh)),
        pl.BlockSpec((s, dk), lambda b, hh: (b, 0)),
        pl.BlockSpec((s, dk), lambda b, hh: (b, 0)),
        pl.BlockSpec((None, 1, dk), lambda b, hh: (hh, 0, 0)),
    ]


def retention_fwd(name, p, cos2, sin2, log_g, nb, s):
    dk, dv, h = RET_QK_DIM, RET_V_DIM, RET_HEADS

    def body(q_ref, k_ref, v_ref, cos_ref, sin_ref, lg_ref, o_ref, kr_ref):
        lg = lg_ref[0:1, 0:1]
        kr = _rot(k_ref[...].astype(F32), cos_ref[...], sin_ref[...]) * (dk ** -0.5)
        kr_ref[...] = kr.astype(kr_ref.dtype)
        for qi in range(s // RET_TQ):
            n0, kmax = qi * RET_TQ, (qi + 1) * RET_TQ
            rows = pl.ds(n0, RET_TQ)
            qr = _rot(q_ref[rows, :].astype(F32), cos_ref[rows, :], sin_ref[rows, :]).astype(MXU_DTYPE)
            sc = lax.dot_general(qr, kr_ref[0:kmax, :], NT, preferred_element_type=F32)
            pm = (sc * _decay_mask(lg, n0, RET_TQ, kmax)).astype(MXU_DTYPE)
            o_ref[rows, :] = lax.dot_general(pm, v_ref[0:kmax, :], NN, preferred_element_type=F32)

    return pl.pallas_call(
        body, name=name, grid=(nb, h), in_specs=_ret_specs(s),
        out_specs=pl.BlockSpec((s, dv), lambda b, hh: (b, hh)),
        out_shape=jax.ShapeDtypeStruct((nb * s, h * dv), F32),
        scratch_shapes=[pltpu.VMEM((s, dk), MXU_DTYPE)],
        compiler_params=_params(("parallel", "parallel")),
    )(p, p, p, cos2, sin2, log_g)


def retention_bwd(name, p, cos2, sin2, log_g, do, nb, s):
    dk, dv, h = RET_QK_DIM, RET_V_DIM, RET_HEADS

    def body(q_ref, k_ref, v_ref, cos_ref, sin_ref, lg_ref, do_ref, dq_ref, dk_ref, dv_ref, kr_ref, dk_acc, dv_acc):
        lg = lg_ref[0:1, 0:1]
        kr = _rot(k_ref[...].astype(F32), cos_ref[...], sin_ref[...]) * (dk ** -0.5)
        kr_ref[...] = kr.astype(kr_ref.dtype)
        dk_acc[...] = jnp.zeros_like(dk_acc)
        dv_acc[...] = jnp.zeros_like(dv_acc)
        for qi in range(s // RET_TQ):
            n0, kmax = qi * RET_TQ, (qi + 1) * RET_TQ
            rows = pl.ds(n0, RET_TQ)
            cq, sq = cos_ref[rows, :], sin_ref[rows, :]
            qr = _rot(q_ref[rows, :].astype(F32), cq, sq).astype(MXU_DTYPE)
            dob = do_ref[rows, :]
            mask = _decay_mask(lg, n0, RET_TQ, kmax)
            sc = lax.dot_general(qr, kr_ref[0:kmax, :], NT, preferred_element_type=F32)
            pm = (sc * mask).astype(MXU_DTYPE)
            dv_acc[0:kmax, :] += lax.dot_general(pm, dob, TN, preferred_element_type=F32)
            dp = lax.dot_general(dob, v_ref[0:kmax, :], NT, preferred_element_type=F32)
            ds = (dp * mask).astype(MXU_DTYPE)
            dqr = lax.dot_general(ds, kr_ref[0:kmax, :], NN, preferred_element_type=F32)
            dq_ref[rows, :] = _rot(dqr, cq, -sq).astype(dq_ref.dtype)
            dk_acc[0:kmax, :] += lax.dot_general(ds, qr, TN, preferred_element_type=F32)
        dkr = dk_acc[...] * (dk ** -0.5)
        dk_ref[...] = _rot(dkr, cos_ref[...], -sin_ref[...]).astype(dk_ref.dtype)
        dv_ref[...] = dv_acc[...].astype(dv_ref.dtype)

    t = nb * s
    return pl.pallas_call(
        body, name=name, grid=(nb, h),
        in_specs=_ret_specs(s) + [pl.BlockSpec((s, dv), lambda b, hh: (b, hh))],
        out_specs=[pl.BlockSpec((s, dk), lambda b, hh: (b, hh)), pl.BlockSpec((s, dk), lambda b, hh: (b, hh)),
                   pl.BlockSpec((s, dv), lambda b, hh: (b, hh))],
        out_shape=[jax.ShapeDtypeStruct((t, h * dk), MXU_DTYPE), jax.ShapeDtypeStruct((t, h * dk), MXU_DTYPE),
                   jax.ShapeDtypeStruct((t, h * dv), MXU_DTYPE)],
        scratch_shapes=[pltpu.VMEM((s, dk), MXU_DTYPE), pltpu.VMEM((s, dk), F32), pltpu.VMEM((s, dv), F32)],
        compiler_params=_params(("parallel", "parallel")),
    )(p, p, p, cos2, sin2, log_g, do)


def _conv_grid(t, d, nb):
    s = t // nb
    ns, nc = s // CONV_TS, d // CONV_TC
    return s, ns, nc


def _shifted(pad_ref, sh_ref, offsets):
    n = sh_ref.shape[1]
    for b in sorted({off % SUBLANES for off in offsets} - {0}):
        sh_ref[b - 1] = pad_ref[pl.ds(b, n), :]

    def read(off, r0):
        a, b = off - off % SUBLANES + r0, off % SUBLANES
        return pad_ref[pl.ds(a, SUBLANES), :] if b == 0 else sh_ref[b - 1, pl.ds(a, SUBLANES), :]

    return read


def _causal_taps(pad_ref, sh_ref, w_ref, k, emit):
    offs = [CONV_PAD - (k - 1) + j for j in range(k)]
    read = _shifted(pad_ref, sh_ref, offs)
    for r0 in range(0, CONV_TS, CONV_ROWS):
        accs = [None] * len(CONV_TILES)
        for j in range(k):
            wj = w_ref[j]
            for q, dr in enumerate(CONV_TILES):
                term = wj * read(offs[j], r0 + dr)
                accs[q] = term if accs[q] is None else accs[q] + term
        emit(r0, jnp.concatenate(accs, axis=0))


def _tap_tiles(w):
    return jnp.broadcast_to(w[:, None, :], (w.shape[0], SUBLANES, w.shape[1]))


def _tap_spec(k):
    return pl.BlockSpec((k, SUBLANES, CONV_TC), lambda c, b, si: (0, 0, c))


def _carry_past(pad_ref, s_idx):
    @pl.when(s_idx == 0)
    def _():
        pad_ref[0:CONV_PAD, :] = jnp.zeros((CONV_PAD, pad_ref.shape[1]), F32)

    @pl.when(s_idx > 0)
    def _():
        pad_ref[0:CONV_PAD, :] = pad_ref[CONV_TS:CONV_TS + CONV_PAD, :]


def _carry_future(pad_ref, s_idx):
    @pl.when(s_idx == 0)
    def _():
        pad_ref[CONV_TS:CONV_TS + CONV_PAD, :] = jnp.zeros((CONV_PAD, pad_ref.shape[1]), F32)

    @pl.when(s_idx > 0)
    def _():
        pad_ref[CONV_TS:CONV_TS + CONV_PAD, :] = pad_ref[0:CONV_PAD, :]


def _conv_bwd_taps(pad_ref, sh_ref, w_ref, dw_acc, k, x_rows, emit, mix):
    read = _shifted(pad_ref, sh_ref, range(k))
    for r0 in range(0, CONV_TS, CONV_ROWS):
        ops = x_rows(r0)
        x = mix(ops)
        accs = [None] * len(CONV_TILES)
        for j in range(k):
            wj, dwj = w_ref[j], None
            for q, dr in enumerate(CONV_TILES):
                sh = read(k - 1 - j, r0 + dr)
                term = wj * sh
                accs[q] = term if accs[q] is None else accs[q] + term
                prod = x[dr:dr + SUBLANES] * sh
                dwj = prod if dwj is None else dwj + prod
            dw_acc[j] += dwj
        emit(r0, ops, jnp.concatenate(accs, axis=0))


def _conv_bwd_edges(dw_acc, dw_ref, nb, ns, extra=()):
    first = jnp.logical_and(pl.program_id(1) == 0, pl.program_id(2) == 0)
    last = jnp.logical_and(pl.program_id(1) == nb - 1, pl.program_id(2) == ns - 1)

    @pl.when(first)
    def _():
        dw_acc[...] = jnp.zeros_like(dw_acc)
        for r in extra:
            r[...] = jnp.zeros_like(r)

    def finish():
        @pl.when(last)
        def _():
            dw_ref[...] = jnp.sum(dw_acc[...], axis=1)

    return finish


def short_conv_fwd(name, p, blk_b, w, nb):
    t = p.shape[0]
    d = w.shape[1]
    s, ns, nc = _conv_grid(t, d, nb)
    cb = d // CONV_TC

    def body(b_ref, c_ref, x_ref, w_ref, y_ref, cz_ref, pad_ref, sh_ref):
        _carry_past(pad_ref, pl.program_id(2))
        pad_ref[CONV_PAD:CONV_PAD + CONV_TS, :] = c_ref[...].astype(F32) * x_ref[...].astype(F32)

        def emit(r0, cz):
            rows = pl.ds(r0, CONV_ROWS)
            cz_ref[rows, :] = cz
            y_ref[rows, :] = (b_ref[rows, :].astype(F32) * cz).astype(y_ref.dtype)

        _causal_taps(pad_ref, sh_ref, w_ref, SC_KERNEL, emit)

    def pspec(off):
        return pl.BlockSpec((CONV_TS, CONV_TC), lambda c, b, si: (b * ns + si, (blk_b + off) * cb + c))

    ospec = pl.BlockSpec((CONV_TS, CONV_TC), lambda c, b, si: (b * ns + si, c))
    return pl.pallas_call(
        body, name=name, grid=(nc, nb, ns),
        in_specs=[pspec(0), pspec(1), pspec(2), _tap_spec(SC_KERNEL)],
        out_specs=[ospec, ospec],
        out_shape=[jax.ShapeDtypeStruct((t, d), MXU_DTYPE), jax.ShapeDtypeStruct((t, d), F32)],
        scratch_shapes=CONV_SCRATCH,
        compiler_params=_params(("parallel", "arbitrary", "arbitrary")),
    )(p, p, p, _tap_tiles(w))


def short_conv_bwd(name, p, blk_b, w, cz, dy, nb):
    t = p.shape[0]
    d = w.shape[1]
    s, ns, nc = _conv_grid(t, d, nb)
    cb = d // CONV_TC

    def body(b_ref, c_ref, x_ref, w_ref, cz_ref, dy_ref, db_ref, dc_ref, dx_ref, dw_ref, pad_ref, sh_ref, dw_acc):
        _carry_future(pad_ref, pl.program_id(2))
        dyv = dy_ref[...].astype(F32)
        db_ref[...] = (dyv * cz_ref[...]).astype(db_ref.dtype)
        pad_ref[0:CONV_TS, :] = dyv * b_ref[...].astype(F32)
        finish = _conv_bwd_edges(dw_acc, dw_ref, nb, ns)

        def x_rows(r0):
            rows = pl.ds(r0, CONV_ROWS)
            return c_ref[rows, :].astype(F32), x_ref[rows, :].astype(F32)

        def emit(r0, cx, dz):
            rows = pl.ds(r0, CONV_ROWS)
            dc_ref[rows, :] = (dz * cx[1]).astype(dc_ref.dtype)
            dx_ref[rows, :] = (dz * cx[0]).astype(dx_ref.dtype)

        _conv_bwd_taps(pad_ref, sh_ref, w_ref, dw_acc, SC_KERNEL, x_rows, emit, lambda cx: cx[0] * cx[1])
        finish()

    def row(b, si):
        return b * ns + (ns - 1 - si)

    def pspec(off):
        return pl.BlockSpec((CONV_TS, CONV_TC), lambda c, b, si: (row(b, si), (blk_b + off) * cb + c))

    ospec = pl.BlockSpec((CONV_TS, CONV_TC), lambda c, b, si: (row(b, si), c))
    wspec = pl.BlockSpec((SC_KERNEL, CONV_TC), lambda c, b, si: (0, c))
    return pl.pallas_call(
        body, name=name, grid=(nc, nb, ns),
        in_specs=[pspec(0), pspec(1), pspec(2), _tap_spec(SC_KERNEL), ospec, ospec],
        out_specs=[ospec, ospec, ospec, wspec],
        out_shape=[jax.ShapeDtypeStruct((t, d), MXU_DTYPE)] * 3 + [jax.ShapeDtypeStruct((SC_KERNEL, d), F32)],
        scratch_shapes=CONV_SCRATCH + [pltpu.VMEM((SC_KERNEL, SUBLANES, CONV_TC), F32)],
        compiler_params=_params(("parallel", "arbitrary", "arbitrary")),
    )(p, p, p, _tap_tiles(w), cz, dy)


def conformer_conv_fwd(name, p, blk_a, w, bias, nb):
    t = p.shape[0]
    d = w.shape[1]
    s, ns, nc = _conv_grid(t, d, nb)
    cb = d // CONV_TC

    def body(a_ref, b_ref, w_ref, bias_ref, u_ref, pad_ref, sh_ref):
        _carry_past(pad_ref, pl.program_id(2))
        pad_ref[CONV_PAD:CONV_PAD + CONV_TS, :] = a_ref[...].astype(F32) * jax.nn.sigmoid(b_ref[...].astype(F32))

        def emit(r0, u):
            u_ref[pl.ds(r0, CONV_ROWS), :] = u + bias_ref[0:1, :]

        _causal_taps(pad_ref, sh_ref, w_ref, CF_KERNEL, emit)

    def pspec(off):
        return pl.BlockSpec((CONV_TS, CONV_TC), lambda c, b, si: (b * ns + si, (blk_a + off) * cb + c))

    return pl.pallas_call(
        body, name=name, grid=(nc, nb, ns),
        in_specs=[pspec(0), pspec(1), _tap_spec(CF_KERNEL), pl.BlockSpec((SUBLANES, CONV_TC), lambda c, b, si: (0, c))],
        out_specs=pl.BlockSpec((CONV_TS, CONV_TC), lambda c, b, si: (b * ns + si, c)),
        out_shape=jax.ShapeDtypeStruct((t, d), F32),
        scratch_shapes=CONV_SCRATCH,
        compiler_params=_params(("parallel", "arbitrary", "arbitrary")),
    )(p, p, _tap_tiles(w), jnp.broadcast_to(bias, (SUBLANES, d)))


def conformer_conv_bwd(name, p, blk_a, w, du, nb):
    t = p.shape[0]
    d = w.shape[1]
    s, ns, nc = _conv_grid(t, d, nb)
    cb = d // CONV_TC

    def body(a_ref, b_ref, w_ref, du_ref, da_ref, db_ref, dw_ref, dbias_ref, pad_ref, sh_ref, dw_acc):
        _carry_future(pad_ref, pl.program_id(2))
        duv = du_ref[...]
        pad_ref[0:CONV_TS, :] = duv
        finish = _conv_bwd_edges(dw_acc, dw_ref, nb, ns, extra=(dbias_ref,))
        dbias_ref[...] += jnp.sum(duv, axis=0, keepdims=True)

        def x_rows(r0):
            rows = pl.ds(r0, CONV_ROWS)
            return a_ref[rows, :].astype(F32), jax.nn.sigmoid(b_ref[rows, :].astype(F32))

        def emit(r0, asg, du0):
            rows = pl.ds(r0, CONV_ROWS)
            av, sg = asg
            da_ref[rows, :] = (du0 * sg).astype(da_ref.dtype)
            db_ref[rows, :] = (du0 * av * sg * (1.0 - sg)).astype(db_ref.dtype)

        _conv_bwd_taps(pad_ref, sh_ref, w_ref, dw_acc, CF_KERNEL, x_rows, emit, lambda asg: asg[0] * asg[1])
        finish()

    def row(b, si):
        return b * ns + (ns - 1 - si)

    def pspec(off):
        return pl.BlockSpec((CONV_TS, CONV_TC), lambda c, b, si: (row(b, si), (blk_a + off) * cb + c))

    ospec = pl.BlockSpec((CONV_TS, CONV_TC), lambda c, b, si: (row(b, si), c))
    wspec = pl.BlockSpec((CF_KERNEL, CONV_TC), lambda c, b, si: (0, c))
    bspec = pl.BlockSpec((1, CONV_TC), lambda c, b, si: (0, c))
    return pl.pallas_call(
        body, name=name, grid=(nc, nb, ns),
        in_specs=[pspec(0), pspec(1), _tap_spec(CF_KERNEL), ospec],
        out_specs=[ospec, ospec, wspec, bspec],
        out_shape=[jax.ShapeDtypeStruct((t, d), MXU_DTYPE)] * 2
        + [jax.ShapeDtypeStruct((CF_KERNEL, d), F32), jax.ShapeDtypeStruct((1, d), F32)],
        scratch_shapes=CONV_SCRATCH + [pltpu.VMEM((CF_KERNEL, SUBLANES, CONV_TC), F32)],
        compiler_params=_params(("parallel", "arbitrary", "arbitrary")),
    )(p, p, _tap_tiles(w), du)


BLOCKS = ("ffn1", "mixer", "ffn2")
BLOCK_WEIGHTS = {"ffn1": ("ffn1_w_gu", "ffn1_w_down"), "mixer": ("w_in", "w_ret_o", "w_sc_o", "w_cf_o", "w_o"),
                 "ffn2": ("ffn2_w_gu", "ffn2_w_down")}
BIG = BLOCK_WEIGHTS["ffn1"] + BLOCK_WEIGHTS["mixer"] + BLOCK_WEIGHTS["ffn2"]
MODE = {"ffn1_w_gu": "col", "ffn1_w_down": "row", "w_in": "col", "w_ret_o": "row", "w_sc_o": "row",
        "w_cf_o": "row", "w_o": "row", "ffn2_w_gu": "col", "ffn2_w_down": "row"}
NORM_OF = {"ffn1": 0, "mixer": 2, "ffn2": 4}
BLK_GATE, BLK_SCB, BLK_CFA, BLK_MERGE = 2, 3, 6, 8


def _rope_tables(positions):
    half = RET_QK_DIM // 2
    inv_freq = ROPE_BASE ** (-jnp.arange(half, dtype=F32) / half)
    ang = positions.astype(F32)[..., None] * inv_freq
    cos, sin = jnp.cos(ang), jnp.sin(ang)
    nb, s = positions.shape
    cos2 = jnp.concatenate([cos, cos], axis=-1).reshape(nb * s, RET_QK_DIM)
    sin2 = jnp.concatenate([-sin, sin], axis=-1).reshape(nb * s, RET_QK_DIM)
    return cos2, sin2


def _log_gamma():
    lg = jnp.log(1.0 - 2.0 ** (-5.0 - jnp.arange(RET_HEADS, dtype=F32)))
    return jnp.broadcast_to(lg[:, None, None], (RET_HEADS, 1, RET_QK_DIM))


def _ffn_fwd(xs, h, w, tag, g_post, g_next):
    gu, a = ffn_up("ffn_up", h, w[tag + "_w_gu"])
    y, out, h_next = mm_post("ffn_down", a, w[tag + "_w_down"], xs, g_post, 0.5, g_next)
    return out, h_next, dict(x=xs, h=h, gu=gu, a=a, y=y, w=w)


def _pinned(g, token):
    return g if token is None else g + token


def _ffn_bwd(dxs, dy, sv, tag, g_pre, push, prev):
    w = sv["w"]
    gu_w, down_w = w[tag + "_w_gu"], w[tag + "_w_down"]
    dgu = ffn_down_dx("ffn_down_dx", dy, down_w, sv["gu"])
    grads = {tag + "_w_down": mm_dw("ffn_down_dw", sv["a"], dy, "row", down_w.shape),
             tag + "_w_gu": mm_dw("ffn_gu_dw", sv["h"], dgu, "col", gu_w.shape)}
    g_pre = _pinned(g_pre, push(grads))
    return mm_dx_norms("ffn_gu_dx", dgu, gu_w, sv["x"], g_pre, dxs, prev)


def _mixer_fwd(xs, h, w, sm, g_post, g_next, rope, nb, s, mid):
    cos2, sin2, log_g = rope
    d = xs.shape[1]
    gate_blk = (BLK_GATE * d) // RET_V_DIM
    p = mm_fwd("mx_in", h, w["w_in"], "col", MXU_DTYPE)
    if mid is not None:
        sm = dict(sm, cf_dw_b=sm["cf_dw_b"] + mid(p))
    o = retention_fwd("ret_fwd", p, cos2, sin2, log_g, nb, s)
    ya_in = head_gate_fwd("ret_gate", o, p, gate_blk)
    yb_in, cz = short_conv_fwd("sc_fwd", p, BLK_SCB, sm["sc_conv_w"], nb)
    u1 = conformer_conv_fwd("cf_fwd", p, BLK_CFA, sm["cf_dw_w"], sm["cf_dw_b"], nb)
    yc_in = ln_silu_fwd("cf_ln", u1, sm["cf_ln_g"], sm["cf_ln_b"])
    ya = mm_fwd("mx_proj", ya_in, w["w_ret_o"], "row", F32)
    yb = mm_fwd("mx_proj", yb_in, w["w_sc_o"], "row", F32)
    yc = mm_fwd("mx_proj", yc_in, w["w_cf_o"], "row", F32)
    mg = merge_fwd("mx_merge", p, BLK_MERGE, ya, yb, yc)
    m, out, h_next = mm_post("mx_out", mg, w["w_o"], xs, g_post, 1.0, g_next)
    return out, h_next, dict(x=xs, h=h, p=p, o=o, ya_in=ya_in, yb_in=yb_in, cz=cz, u1=u1, yc_in=yc_in, ya=ya, yb=yb, yc=yc,
                     mg=mg, m=m, w=w)


def _mixer_bwd(dxs, dm, sv, sm, g_pre, rope, nb, s, push, prev):
    cos2, sin2, log_g = rope
    w, p = sv["w"], sv["p"]
    d = dxs.shape[1]
    gate_blk = (BLK_GATE * d) // RET_V_DIM
    grads, gsm = {}, {}

    def proj_bwd(wname, a_in, dy, out_dtype):
        grads[wname] = mm_dw("mx_proj_dw", a_in, dy, "row", w[wname].shape)
        return mm_dx("mx_proj_dx", dy, w[wname], "row", out_dtype)

    dmg = proj_bwd("w_o", sv["mg"], dm, MXU_DTYPE)
    dg0, dg1, dg2, dya, dyb, dyc = merge_bwd("mx_merge_bwd", p, BLK_MERGE, sv["ya"], sv["yb"], sv["yc"], dmg)
    dya_in = proj_bwd("w_ret_o", sv["ya_in"], dya, MXU_DTYPE)
    dyb_in = proj_bwd("w_sc_o", sv["yb_in"], dyb, MXU_DTYPE)
    dyc_in = proj_bwd("w_cf_o", sv["yc_in"], dyc, MXU_DTYPE)
    do, dgret = head_gate_bwd("ret_gate_bwd", sv["o"], p, gate_blk, dya_in)
    dq, dk, dv = retention_bwd("ret_bwd", p, cos2, sin2, log_g, do, nb, s)
    dscb, dscc, dscx, gsm["sc_conv_w"] = short_conv_bwd("sc_bwd", p, BLK_SCB, sm["sc_conv_w"], sv["cz"], dyb_in, nb)
    du1, dlg, dlb = ln_silu_bwd("cf_ln_bwd", sv["u1"], sm["cf_ln_g"], sm["cf_ln_b"], dyc_in)
    dcfa, dcfb, gsm["cf_dw_w"], dbias = conformer_conv_bwd("cf_bwd", p, BLK_CFA, sm["cf_dw_w"], du1, nb)
    gsm.update(cf_ln_g=dlg[0], cf_ln_b=dlb[0], cf_dw_b=dbias[0])
    dp = concat_cols("mx_dp", [dq, dk, dv, dgret, dscb, dscc, dscx, dcfa, dcfb, dg0, dg1, dg2])
    grads["w_in"] = mm_dw("mx_in_dw", sv["h"], dp, "col", w["w_in"].shape)
    g_pre = _pinned(g_pre, push(grads))
    return mm_dx_norms("mx_in_dx", dp, w["w_in"], sv["x"], g_pre, dxs, prev) + (gsm,)


def local_step(x, positions, target, small, fetch, push):
    nb, s, d = x.shape
    t = nb * s
    depth = small["norm_g"].shape[0]
    rope = _rope_tables(positions) + (_log_gamma(),)
    xs = x.reshape(t, d)
    token = [None]

    def gain(l, i):
        g = small["norm_g"][l, i][None, :]
        if token[0] is not None:
            g, token[0] = g + token[0], None
        return g

    def mixer_small(l):
        return dict(sc_conv_w=small["sc_conv_w"][l], cf_dw_w=small["cf_dw_w"][l], cf_dw_b=small["cf_dw_b"][l][None, :],
                    cf_ln_g=small["cf_ln_g"][l][None, :], cf_ln_b=small["cf_ln_b"][l][None, :])

    saved = {}
    order = [(l, blk) for l in range(depth) for blk in BLOCKS]
    h = None
    for at, (l, blk) in enumerate(order):
        w, token[0], mid = fetch(l, blk, xs)
        i0 = NORM_OF[blk]
        if h is None:
            h = rms_fwd("first_rms", xs, gain(l, i0))
        g_post = gain(l, i0 + 1)
        g_next = gain(order[at + 1][0], NORM_OF[order[at + 1][1]]) if at + 1 < len(order) else None
        if blk == "mixer":
            xs, h, saved[l, blk] = _mixer_fwd(xs, h, w, mixer_small(l), g_post, g_next, rope, nb, s, mid)
        else:
            xs, h, saved[l, blk] = _ffn_fwd(xs, h, w, blk, g_post, g_next)

    dxs, loss = loss_head("loss", xs, target.reshape(t, d))

    dnorm = [[None] * 6 for _ in range(depth)]
    gsmall = {n: [None] * depth for n in ("sc_conv_w", "cf_dw_w", "cf_dw_b", "cf_ln_g", "cf_ln_b")}
    def branch(group):
        l, blk = group
        sv = saved[group]
        return (sv["m"], gain(l, NORM_OF[blk] + 1), 1.0) if blk == "mixer" else (sv["y"], gain(l, NORM_OF[blk] + 1), 0.5)

    l, blk = order[-1]
    y, g_post, scale = branch(order[-1])
    dy, dnorm[l][NORM_OF[blk] + 1] = post_bwd("last_post_bwd", y, g_post, dxs, scale)
    for at in reversed(range(len(order))):
        l, blk = order[at]
        i0 = NORM_OF[blk]
        prev = branch(order[at - 1]) if at > 0 else None
        put = functools.partial(push, l, blk)
        if blk == "mixer":
            dxs, dnorm[l][i0], dy, dg_prev, gsm = _mixer_bwd(
                dxs, dy, saved[l, blk], mixer_small(l), gain(l, i0), rope, nb, s, put, prev)
            for n, v in gsm.items():
                gsmall[n][l] = v
        else:
            dxs, dnorm[l][i0], dy, dg_prev = _ffn_bwd(dxs, dy, saved[l, blk], blk, gain(l, i0), put, prev)
        if at > 0:
            dnorm[order[at - 1][0]][NORM_OF[order[at - 1][1]] + 1] = dg_prev

    gs = {n: jnp.stack(v) for n, v in gsmall.items()}
    gs["norm_g"] = jnp.stack([jnp.concatenate(r, axis=0) for r in dnorm])
    return loss, dxs.reshape(nb, s, d), gs


ANY = pl.BlockSpec(memory_space=pl.ANY)
HBM = pl.BlockSpec(memory_space=pltpu.HBM)
SEM = pl.BlockSpec(memory_space=pltpu.SEMAPHORE)
VMEM_WHOLE = pl.BlockSpec(memory_space=pltpu.VMEM)
EFFECT = pltpu.SideEffectType.DATAFLOW_SIDE_EFFECTING
TOKEN = jax.ShapeDtypeStruct((8, 128), F32)


def _other_chips(x, y):
    return [(1 - x, y), (x, 1 - y), (1 - x, 1 - y)]


def _remote(src, dst, send_sem, recv_sem, to):
    return pltpu.make_async_remote_copy(src_ref=src, dst_ref=dst, send_sem=send_sem, recv_sem=recv_sem,
                                        device_id=to, device_id_type=MESH)


def _in_hbm(v):
    return pltpu.with_memory_space_constraint(v, pltpu.HBM)


def place_quarters(ws, layer, ids, after):
    m = len(ws)

    def body(ids_ref, *refs):
        for w_ref, o_ref in zip(refs[:m], refs[m + 1:]):
            o_ref[...] = w_ref[...].astype(o_ref.dtype)

    def spec(w, where):
        return pl.BlockSpec((None, w.shape[1] // STREAM_STEPS, w.shape[2]), where)

    return pl.pallas_call(
        body, name="place_quarters",
        grid_spec=pltpu.PrefetchScalarGridSpec(
            num_scalar_prefetch=1, grid=(STREAM_STEPS,),
            in_specs=[spec(w, lambda i, ids_ref: (layer, i, 0)) for w in ws] + [ANY],
            out_specs=[spec(w, lambda i, ids_ref: (ids_ref[0], i, 0)) for w in ws]),
        out_shape=[jax.ShapeDtypeStruct((N_CHIP,) + w.shape[1:], MXU_DTYPE) for w in ws],
        compiler_params=_params(("parallel",)),
    )(ids, *ws, after)


def _gather_copies(lands, send, recv):
    x, y, c = _axes()
    me = 2 * x + y
    mine, theirs = [], []
    for a, ld in enumerate(lands):
        rh = ld.shape[1] // 2
        rows = pl.ds(c * rh, rh)
        for k, (px, py) in enumerate(_other_chips(x, y)):
            to = (px, py, c)
            mine.append(_remote(ld.at[me, rows, :], ld.at[me, rows, :], send.at[3 * a + k], recv.at[3 * a + k], to))
            got = ld.at[2 * px + py, rows, :]
            theirs.append(_remote(got, got, send.at[3 * a + k], recv.at[3 * a + k], to))
    return mine, theirs


def gather_start(name, groups, after):
    flat = [s for g in groups for s in g]
    n, ng = len(flat), len(groups)
    sizes = [len(g) for g in groups]

    def body(*refs):
        lands = refs[:n]
        sems = refs[n + 1:n + 1 + 2 * ng]
        token = refs[-1]
        at = 0
        for g, m in enumerate(sizes):
            mine, _ = _gather_copies(lands[at:at + m], sems[2 * g], sems[2 * g + 1])
            for cp in mine:
                cp.start()
            at += m
        token[...] = jnp.zeros_like(token)

    sem_shapes = []
    for m in sizes:
        sem_shapes += [pltpu.SemaphoreType.DMA((3 * m,))] * 2
    res = pl.pallas_call(
        body, name=name, in_specs=[HBM] * n + [ANY],
        out_specs=[SEM] * (2 * ng) + [HBM] * n + [VMEM_WHOLE],
        out_shape=sem_shapes + [pltpu.HBM(s.shape, s.dtype) for s in flat] + [TOKEN],
        input_output_aliases={i: 2 * ng + i for i in range(n)},
        compiler_params=pltpu.CompilerParams(has_side_effects=EFFECT),
    )(*[_in_hbm(s) for s in flat], after)
    sems, thru, token = res[:2 * ng], res[2 * ng:2 * ng + n], res[-1]
    out, at = [], 0
    for g, m in enumerate(sizes):
        out.append((sems[2 * g], sems[2 * g + 1], thru[at:at + m]))
        at += m
    return out, token


def gather_wait(lands, send, recv, after):
    m = len(lands)

    def body(*refs):
        mine, theirs = _gather_copies(refs[:m], refs[m], refs[m + 1])
        for cp in mine:
            cp.wait_send()
        for cp in theirs:
            cp.wait_recv()

    return pl.pallas_call(
        body, name="gather_wait", in_specs=[HBM] * m + [SEM, SEM, ANY], out_specs=[HBM] * m,
        out_shape=[pltpu.HBM(l.shape, l.dtype) for l in lands],
        input_output_aliases={i: i for i in range(m)},
        compiler_params=pltpu.CompilerParams(has_side_effects=EFFECT),
    )(*lands, send, recv, after)


def copy_start(name, bufs, copies, ncopy, after=()):
    n, k = len(bufs), len(after)

    def body(*refs):
        for cp in copies(refs[:n], refs[n + k], refs[n + k + 1])[0]:
            cp.start()
        refs[-1][...] = jnp.zeros_like(refs[-1])

    res = pl.pallas_call(
        body, name=name, in_specs=[HBM] * n + [ANY] * k, out_specs=[SEM, SEM] + [HBM] * n + [VMEM_WHOLE],
        out_shape=[pltpu.SemaphoreType.DMA((ncopy,))] * 2 + [pltpu.HBM(b.shape, b.dtype) for b in bufs] + [TOKEN],
        input_output_aliases={i: 2 + i for i in range(n)},
        compiler_params=pltpu.CompilerParams(has_side_effects=EFFECT),
    )(*[_in_hbm(b) for b in bufs], *after)
    return res[0], res[1], list(res[2:2 + n]), res[-1]


def copy_wait(name, bufs, send, recv, copies, after=()):
    n = len(bufs)

    def body(*refs):
        mine, theirs = copies(refs[:n], refs[n], refs[n + 1])
        for cp in mine:
            cp.wait_send()
        for cp in theirs:
            cp.wait_recv()

    return list(pl.pallas_call(
        body, name=name, in_specs=[HBM] * n + [SEM, SEM] + [ANY] * len(after), out_specs=[HBM] * n,
        out_shape=[pltpu.HBM(b.shape, b.dtype) for b in bufs], input_output_aliases={i: i for i in range(n)},
        compiler_params=pltpu.CompilerParams(has_side_effects=EFFECT),
    )(*bufs, send, recv, *after))


def _fill_copies(lands, send, recv):
    x, y, c = _axes()
    sib = (x, y, 1 - c)
    mine, theirs = [], []
    for a, ld in enumerate(lands):
        rh = ld.shape[1] // 2
        for k, (px, py) in enumerate(_other_chips(x, y)):
            got = ld.at[2 * px + py, pl.ds(c * rh, rh), :]
            mine.append(_remote(got, got, send.at[3 * a + k], recv.at[3 * a + k], sib))
            blk = ld.at[2 * px + py, pl.ds((1 - c) * rh, rh), :]
            theirs.append(_remote(blk, blk, send.at[3 * a + k], recv.at[3 * a + k], sib))
    return mine, theirs


def _presum_copies(grads, lands, send, recv):
    x, y, c = _axes()
    cps = []
    for a, (g, ld) in enumerate(zip(grads, lands)):
        rh = g.shape[1] // 2
        cps.append(_remote(g.at[:, pl.ds((1 - c) * rh, rh), :], ld, send.at[a], recv.at[a], (x, y, 1 - c)))
    return cps


def presum_start(grads):
    m = len(grads)

    def body(*refs):
        for cp in _presum_copies(refs[:m], refs[m:2 * m], refs[2 * m], refs[2 * m + 1]):
            cp.start()
        refs[-1][...] = jnp.zeros_like(refs[-1])

    lands = [lax.empty((g.shape[0], g.shape[1] // 2, g.shape[2]), g.dtype) for g in grads]
    res = pl.pallas_call(
        body, name="presum_start", in_specs=[HBM] * (2 * m), out_specs=[SEM, SEM] + [HBM] * (2 * m) + [VMEM_WHOLE],
        out_shape=[pltpu.SemaphoreType.DMA((m,))] * 2 + [pltpu.HBM(g.shape, g.dtype) for g in grads]
        + [pltpu.HBM(l.shape, l.dtype) for l in lands] + [TOKEN],
        input_output_aliases={i: 2 + i for i in range(2 * m)},
        compiler_params=pltpu.CompilerParams(has_side_effects=EFFECT),
    )(*[_in_hbm(g) for g in grads], *[_in_hbm(l) for l in lands])
    return res[0], res[1], res[2:2 + m], res[2 + m:2 + 2 * m], res[-1]


def presum_wait(grads, lands, send, recv, after):
    m = len(grads)

    def body(*refs):
        for cp in _presum_copies(refs[:m], refs[m:2 * m], refs[2 * m], refs[2 * m + 1]):
            cp.wait_send()
            cp.wait_recv()

    res = pl.pallas_call(
        body, name="presum_wait", in_specs=[HBM] * (2 * m) + [SEM, SEM] + [ANY] * len(after),
        out_specs=[HBM] * (2 * m),
        out_shape=[pltpu.HBM(g.shape, g.dtype) for g in grads] + [pltpu.HBM(l.shape, l.dtype) for l in lands],
        input_output_aliases={i: i for i in range(2 * m)},
        compiler_params=pltpu.CompilerParams(has_side_effects=EFFECT),
    )(*grads, *lands, send, recv, *after)
    return res[:m], res[m:]


def add_halves(gs, lands, ids):
    m = len(gs)

    def body(ids_ref, *refs):
        for a_ref, b_ref, o_ref in zip(refs[:m], refs[m:2 * m], refs[2 * m:]):
            o_ref[...] = (a_ref[...].astype(F32) + b_ref[...].astype(F32)).astype(o_ref.dtype)

    def spec(ld, where):
        return pl.BlockSpec((None,) + ld.shape[1:], where)

    return pl.pallas_call(
        body, name="add_halves",
        grid_spec=pltpu.PrefetchScalarGridSpec(
            num_scalar_prefetch=1, grid=(N_CHIP,),
            in_specs=[spec(ld, lambda i, ids_ref: (i, ids_ref[1], 0)) for ld in lands]
            + [spec(ld, lambda i, ids_ref: (i, 0, 0)) for ld in lands],
            out_specs=[spec(ld, lambda i, ids_ref: (i, 0, 0)) for ld in lands]),
        out_shape=[jax.ShapeDtypeStruct(ld.shape, ld.dtype) for ld in lands],
        compiler_params=_params(("parallel",)),
    )(ids, *gs, *lands)


def _scatter_copies(parts, lands, send, recv):
    x, y, c = _axes()
    cps = []
    for a, (pt, ld) in enumerate(zip(parts, lands)):
        for k, (px, py) in enumerate(_other_chips(x, y)):
            cps.append(_remote(pt.at[2 * px + py], ld.at[k], send.at[3 * a + k], recv.at[3 * a + k], (px, py, c)))
    return cps


def scatter_start(parts):
    m = len(parts)

    def body(*refs):
        for cp in _scatter_copies(refs[:m], refs[m:2 * m], refs[2 * m], refs[2 * m + 1]):
            cp.start()
        refs[-1][...] = jnp.zeros_like(refs[-1])

    lands = [lax.empty((N_CHIP - 1,) + p.shape[1:], p.dtype) for p in parts]
    res = pl.pallas_call(
        body, name="scatter_start", in_specs=[HBM] * (2 * m), out_specs=[SEM, SEM] + [HBM] * (2 * m) + [VMEM_WHOLE],
        out_shape=[pltpu.SemaphoreType.DMA((3 * m,))] * 2 + [pltpu.HBM(p.shape, p.dtype) for p in parts]
        + [pltpu.HBM(l.shape, l.dtype) for l in lands] + [TOKEN],
        input_output_aliases={i: 2 + i for i in range(2 * m)},
        compiler_params=pltpu.CompilerParams(has_side_effects=EFFECT),
    )(*[_in_hbm(p) for p in parts], *[_in_hbm(l) for l in lands])
    return res[0], res[1], res[2:2 + m], res[2 + m:2 + 2 * m], res[-1]


def scatter_wait(parts, lands, send, recv, after):
    m = len(parts)

    def body(*refs):
        for cp in _scatter_copies(refs[:m], refs[m:2 * m], refs[2 * m], refs[2 * m + 1]):
            cp.wait_send()
            cp.wait_recv()

    res = pl.pallas_call(
        body, name="scatter_wait", in_specs=[HBM] * (2 * m) + [SEM, SEM] + [ANY] * len(after),
        out_specs=[HBM] * (2 * m),
        out_shape=[pltpu.HBM(p.shape, p.dtype) for p in parts] + [pltpu.HBM(l.shape, l.dtype) for l in lands],
        input_output_aliases={i: i for i in range(2 * m)},
        compiler_params=pltpu.CompilerParams(has_side_effects=EFFECT),
    )(*parts, *lands, send, recv, *after)
    return res[:m], res[m:]


def sum_partials(parts, lands, ids, layer, depth, intos):
    m = len(parts)
    nt = STREAM_STEPS

    def body(ids_ref, *refs):
        for p_ref, l_ref, o_ref in zip(refs[:m], refs[m:2 * m], refs[-m:]):
            acc = p_ref[...].astype(F32)
            for k in range(N_CHIP - 1):
                acc = acc + l_ref[k].astype(F32)
            o_ref[...] = acc

    def rows(p):
        return p.shape[1] // nt

    in_specs = [pl.BlockSpec((None, rows(p), p.shape[2]), lambda i, ids_ref: (ids_ref[0], i, 0)) for p in parts]
    in_specs += [pl.BlockSpec((N_CHIP - 1, rows(p), p.shape[2]), lambda i, ids_ref: (0, i, 0)) for p in parts]
    args = [ids, *parts, *lands]
    aliases = {}
    if intos is not None:
        in_specs += [ANY] * m
        args += list(intos)
        aliases = {1 + 2 * m + a: a for a in range(m)}
    return pl.pallas_call(
        body, name="sum_partials",
        grid_spec=pltpu.PrefetchScalarGridSpec(
            num_scalar_prefetch=1, grid=(nt,), in_specs=in_specs,
            out_specs=[pl.BlockSpec((None, rows(p), p.shape[2]), lambda i, ids_ref: (layer, ids_ref[1] * nt + i, 0))
                       for p in parts]),
        out_shape=[jax.ShapeDtypeStruct((depth, 2 * p.shape[1], p.shape[2]), F32) for p in parts],
        input_output_aliases=aliases, compiler_params=_params(("parallel",)),
    )(*args)


def _final_copies(layer):
    def copies(bufs, send, recv):
        x, y, c = _axes()
        sib = (x, y, 1 - c)
        mine, theirs = [], []
        for a, buf in enumerate(bufs):
            rh = buf.shape[1] // 2
            src = buf.at[layer, pl.ds(c * rh, rh), :]
            mine.append(_remote(src, src, send.at[a], recv.at[a], sib))
            dst = buf.at[layer, pl.ds((1 - c) * rh, rh), :]
            theirs.append(_remote(dst, dst, send.at[a], recv.at[a], sib))
        return mine, theirs

    return copies


def allgather_small(pk):
    def body(in_ref, out_ref, send, recv):
        x, y, c = _axes()
        me = 2 * x + y
        chips = _other_chips(x, y)
        out_ref[pl.ds(me, 1)] = in_ref[...][None]
        cps = []
        for k, (px, py) in enumerate(chips):
            cp = _remote(in_ref, out_ref.at[me], send.at[k], recv.at[k], (px, py, c))
            cp.start()
            cps.append(cp)
        for k, (px, py) in enumerate(chips):
            got = out_ref.at[2 * px + py]
            _remote(got, got, send.at[k], recv.at[k], (px, py, c)).wait_recv()
        for cp in cps:
            cp.wait_send()

    return pl.pallas_call(
        body, name="allgather_small", in_specs=[VMEM_WHOLE], out_specs=VMEM_WHOLE,
        out_shape=jax.ShapeDtypeStruct((N_CHIP,) + pk.shape, pk.dtype),
        scratch_shapes=[pltpu.SemaphoreType.DMA((3,))] * 2,
    )(pk)


N_DEV = 8


def _small_copies(bufs, send, recv):
    g, slots = bufs
    x, y, c = _axes()
    me = 4 * x + 2 * y + c
    mine, theirs = [], []
    for mask in range(1, N_DEV):
        px = 1 - x if mask & 4 else x
        py = 1 - y if mask & 2 else y
        pc = 1 - c if mask & 1 else c
        mine.append(_remote(g, slots.at[me], send.at[mask - 1], recv.at[mask - 1], (px, py, pc)))
        got = slots.at[4 * px + 2 * py + pc]
        theirs.append(_remote(got, got, send.at[mask - 1], recv.at[mask - 1], (px, py, pc)))
    return mine, theirs


def sum_slots(g, slots, me):
    def body(me_ref, g_ref, slots_ref, o_ref):
        acc = None
        for d in range(N_DEV):
            term = jnp.where(me_ref[0] == d, g_ref[...], slots_ref[d])
            acc = term if acc is None else acc + term
        o_ref[...] = acc

    return pl.pallas_call(
        body, name="sum_slots",
        grid_spec=pltpu.PrefetchScalarGridSpec(
            num_scalar_prefetch=1, grid=(1,),
            in_specs=[pl.BlockSpec(g.shape, lambda i, me_ref: (0, 0)),
                      pl.BlockSpec(slots.shape, lambda i, me_ref: (0, 0, 0))],
            out_specs=pl.BlockSpec(g.shape, lambda i, me_ref: (0, 0))),
        out_shape=jax.ShapeDtypeStruct(g.shape, g.dtype),
        compiler_params=_params(("arbitrary",)),
    )(me, g, slots)


def adamw(w, g, m, v, layer=None, intos=None):
    shape = w.shape
    cols = shape[-1]
    rows = int(np.prod(shape[:-1]))
    span = rows if layer is None else rows // shape[0]
    tr = span
    for cand in (256, 128):
        if span % cand == 0 and cand * cols * 4 <= 2 * 1024 * 1024:
            tr = cand
            break
    first = 0 if layer is None else layer * (span // tr)
    c1 = 1.0 - ADAM_B1 ** ADAM_STEP
    c2 = 1.0 - ADAM_B2 ** ADAM_STEP

    def body(w_ref, g_ref, m_ref, v_ref, *rest):
        d_ref, nm_ref, nv_ref, g_out = rest[-4:]
        gv = g_ref[...]
        g_out[...] = gv
        nm = ADAM_B1 * m_ref[...] + (1.0 - ADAM_B1) * gv
        nv = ADAM_B2 * v_ref[...] + (1.0 - ADAM_B2) * jnp.square(gv)
        d_ref[...] = -ADAM_LR * ((nm / c1) / (jnp.sqrt(nv / c2) + ADAM_EPS) + ADAM_WD * w_ref[...])
        nm_ref[...] = nm
        nv_ref[...] = nv

    spec = pl.BlockSpec((tr, cols), lambda i: (first + i, 0))
    args = [a.reshape(rows, cols) for a in (w, g, m, v)]
    in_specs, aliases = [spec] * 4, {}
    if intos is not None:
        args += [a.reshape(rows, cols) for a in intos]
        in_specs += [ANY] * 4
        aliases = {4 + k: k for k in range(4)}
    res = pl.pallas_call(
        body, name="adamw", grid=(span // tr,), in_specs=in_specs, out_specs=[spec] * 4,
        out_shape=[jax.ShapeDtypeStruct((rows, cols), F32)] * 4, input_output_aliases=aliases,
        compiler_params=_params(("parallel",)),
    )(*args)
    return [r.reshape(shape) for r in res]


WEIGHTS = ("norm_g", "ffn1_w_gu", "ffn1_w_down", "w_in", "w_ret_o", "sc_conv_w", "w_sc_o", "cf_dw_w", "cf_dw_b",
           "cf_ln_g", "cf_ln_b", "w_cf_o", "w_o", "ffn2_w_gu", "ffn2_w_down")
SHARDED_SMALL = ("norm_g", "sc_conv_w", "cf_dw_w")
REPLICATED_SMALL = ("cf_dw_b", "cf_ln_g", "cf_ln_b")

def _pack_rows(parts):
    padded, offs, at = [], [], 0
    for p in parts:
        r = -(-p.shape[0] // SUBLANES) * SUBLANES
        padded.append(jnp.pad(p, ((0, r - p.shape[0]), (0, 0))))
        offs.append(at)
        at += r
    return jnp.concatenate(padded, axis=0), offs


def kernel(x, positions, norm_g, ffn1_w_gu, ffn1_w_down, w_in, w_ret_o, sc_conv_w, w_sc_o, cf_dw_w, cf_dw_b, cf_ln_g, cf_ln_b, w_cf_o, w_o, ffn2_w_gu, ffn2_w_down, loss_target, m_norm_g, m_ffn1_w_gu, m_ffn1_w_down, m_w_in, m_w_ret_o, m_sc_conv_w, m_w_sc_o, m_cf_dw_w, m_cf_dw_b, m_cf_ln_g, m_cf_ln_b, m_w_cf_o, m_w_o, m_ffn2_w_gu, m_ffn2_w_down, v_norm_g, v_ffn1_w_gu, v_ffn1_w_down, v_w_in, v_w_ret_o, v_sc_conv_w, v_w_sc_o, v_cf_dw_w, v_cf_dw_b, v_cf_ln_g, v_cf_ln_b, v_w_cf_o, v_w_o, v_ffn2_w_gu, v_ffn2_w_down):
    wts = dict(zip(WEIGHTS, (norm_g, ffn1_w_gu, ffn1_w_down, w_in, w_ret_o, sc_conv_w, w_sc_o, cf_dw_w, cf_dw_b,
                             cf_ln_g, cf_ln_b, w_cf_o, w_o, ffn2_w_gu, ffn2_w_down)))
    mom = dict(zip(WEIGHTS, (m_norm_g, m_ffn1_w_gu, m_ffn1_w_down, m_w_in, m_w_ret_o, m_sc_conv_w, m_w_sc_o,
                             m_cf_dw_w, m_cf_dw_b, m_cf_ln_g, m_cf_ln_b, m_w_cf_o, m_w_o, m_ffn2_w_gu, m_ffn2_w_down)))
    var = dict(zip(WEIGHTS, (v_norm_g, v_ffn1_w_gu, v_ffn1_w_down, v_w_in, v_w_ret_o, v_sc_conv_w, v_w_sc_o,
                             v_cf_dw_w, v_cf_dw_b, v_cf_ln_g, v_cf_ln_b, v_w_cf_o, v_w_o, v_ffn2_w_gu, v_ffn2_w_down)))
    depth = norm_g.shape[0]
    dq = norm_g.shape[-1]
    d = N_CHIP * dq
    chip = 2 * lax.axis_index("x") + lax.axis_index("y")
    ids = jnp.stack([chip, lax.axis_index("c")]).astype(jnp.int32)

    pk, offs = _pack_rows([wts[n].reshape(-1, dq) for n in SHARDED_SMALL])
    gk4 = allgather_small(pk)
    gk = gk4.transpose(1, 0, 2).reshape(pk.shape[0], d)
    small = {n: wts[n] for n in REPLICATED_SMALL}
    for n, o in zip(SHARDED_SMALL, offs):
        rows = wts[n].shape[0] * wts[n].shape[1]
        small[n] = gk[o:o + rows].reshape(wts[n].shape[:2] + (d,))

    order = [(l, blk) for l in range(depth) for blk in BLOCKS]
    def placed(groups, after):
        return [place_quarters([wts[n] for n in BLOCK_WEIGHTS[blk]], l, ids, after) for l, blk in groups]

    first, token = gather_start("gather_start_first", placed(order[:1], gk4), gk4)
    rest, token = gather_start("gather_start_rest", placed(order[1:], token), token)
    started = dict(zip(order, first + rest))
    small["norm_g"] = small["norm_g"] + token[0:1, 0:1]

    filling = {}

    def fill(group, after):
        send, recv, lands = started[group]
        lands = gather_wait(lands, send, recv, after)
        send, recv, lands, tok = copy_start("fill_start", lands, _fill_copies, 3 * len(lands))
        filling[group] = (send, recv, lands)
        return tok[0:1, 0:1]

    def fetch(l, blk, after):
        at = order.index((l, blk))
        if (l, blk) not in filling:
            fill((l, blk), token if at == 0 else after)
        send, recv, lands = filling.pop((l, blk))
        lands = copy_wait("fill_wait", lands, send, recv, _fill_copies, (after,))
        tok, mid = None, None
        if at == 1:
            mid = functools.partial(fill, order[at + 1])
        elif 1 < at < len(order) - 1:
            tok = fill(order[at + 1], lands[0])
        return dict(zip(BLOCK_WEIGHTS[blk], lands)), tok, mid

    gsum = {n: None for n in BIG}
    presums, scatters, finals = [], [], []

    def scatter_next(after):
        group, gl, lands, send, recv = presums.pop(0)
        gl, lands = presum_wait(gl, lands, send, recv, after)
        send, recv, parts, lands, tok = scatter_start(add_halves(gl, lands, ids))
        scatters.append((group, parts, lands, send, recv))
        return tok

    def sum_next(after):
        (l, blk), parts, lands, send, recv = scatters.pop(0)
        parts, lands = scatter_wait(parts, lands, send, recv, after)
        names = BLOCK_WEIGHTS[blk]
        intos = None if gsum[names[0]] is None else [gsum[n] for n in names]
        sums = sum_partials(parts, lands, ids, l, depth, intos)
        send, recv, sums, tok = copy_start("final_start", sums, _final_copies(l), len(sums))
        gsum.update(zip(names, sums))
        finals.append((names, l, send, recv))
        return tok

    def final_next(after):
        names, l, send, recv = finals.pop(0)
        gsum.update(zip(names, copy_wait("final_wait", [gsum[n] for n in names], send, recv, _final_copies(l), after)))

    def push(l, blk, grads):
        send, recv, gl, lands, tok = presum_start([grads[n] for n in BLOCK_WEIGHTS[blk]])
        if scatters:
            tok = tok + sum_next((gl[0],))
        if presums:
            tok = tok + scatter_next((gl[0],))
        presums.append(((l, blk), gl, lands, send, recv))
        return tok[0:1, 0:1]

    loss, grad_x, gs = local_step(x, positions, loss_target, small, fetch, push)

    names = SHARDED_SMALL + REPLICATED_SMALL
    pg, offs = _pack_rows([gs[n].reshape(-1, d) for n in names])
    s_send, s_recv, s_bufs, tok = copy_start("small_start", [pg, lax.empty((N_DEV,) + pg.shape, pg.dtype)],
                                             _small_copies, N_DEV - 1, (grad_x,))
    tok = scatter_next((grad_x, tok))

    delta, new_m, new_v, grads = {}, {}, {}, {}

    def update(n, layer=None):
        g = gsum[n] if n in BIG else grads[n]
        prev = [delta[n], new_m[n], new_v[n], grads[n]] if layer is not None and n in delta else None
        delta[n], new_m[n], new_v[n], grads[n] = adamw(wts[n], g, mom[n], var[n], layer, prev)

    while finals and finals[0][1] > 0:
        done, l = finals[0][:2]
        final_next((tok,))
        for n in done:
            update(n, l)
    upper = tuple(delta[n] for n in BIG if n in delta)
    pg, slots = copy_wait("small_wait", s_bufs, s_send, s_recv, _small_copies, upper + (tok,))
    me = (2 * chip + lax.axis_index("c")).astype(jnp.int32).reshape(1)
    tot = sum_slots(pg, slots, me)
    for n, o in zip(names, offs):
        rows = int(np.prod(gs[n].shape[:-1]))
        full = tot[o:o + rows]
        if n in SHARDED_SMALL:
            full = lax.dynamic_slice_in_dim(full, chip * dq, dq, axis=1)
        grads[n] = full.reshape(wts[n].shape)

    for n in names:
        update(n)
    after = tuple(delta[n] for n in names)
    while scatters or finals:
        if scatters:
            after = (sum_next(after),)
        done, l = finals[0][:2]
        final_next(after)
        for n in done:
            update(n, l)
        after = tuple(delta[n] for n in done)

    loss_all = lax.psum(loss[0, 0], ("x", "y", "c"))
    return (loss_all, grad_x, *[grads[n] for n in WEIGHTS], *[delta[n] for n in WEIGHTS],
            *[new_m[n] for n in WEIGHTS], *[new_v[n] for n in WEIGHTS])
```

```python
import functools

import jax
import jax.numpy as jnp
import numpy as np
from jax import lax
from jax.experimental import pallas as pl
from jax.experimental.pallas import tpu as pltpu

F32 = jnp.float32
BF16 = jnp.bfloat16
MXU_DTYPE = BF16
VMEM_LIMIT_BYTES = 56 * 1024 * 1024
MESH = pl.DeviceIdType.MESH

N_CHIP = 4
CHUNK = 64
RET_HEADS = 4
RET_QK_DIM = 128
RET_V_DIM = 256
SC_KERNEL = 3
CF_KERNEL = 31
ROPE_BASE = 10000.0
NORM_EPS = 1e-6
LN_EPS = 1e-5
ADAM_LR = 0.001
ADAM_B1 = 0.9
ADAM_B2 = 0.999
ADAM_EPS = 1e-08
ADAM_WD = 0.01
ADAM_STEP = 10

SUBLANES = 8
CONV_PAD = 32
CONV_TS = 128
CONV_TC = 512
CONV_ROWS = 32
CONV_TILES = range(0, CONV_ROWS, SUBLANES)
CONV_SCRATCH = [pltpu.VMEM((CONV_TS + CONV_PAD, CONV_TC), F32),
                pltpu.VMEM((SUBLANES - 1, CONV_TS + CONV_PAD - SUBLANES, CONV_TC), F32)]
RET_TQ = 512
MM_TM = 1024
MM_TN = 1536
MM_K1 = 1024
MM_W1 = 8 << 20
MM_SLICE = 256
MM_IN_BYTES = 36 << 20
STREAM_STEPS = 2


def _params(sem):
    return pltpu.CompilerParams(dimension_semantics=sem, vmem_limit_bytes=VMEM_LIMIT_BYTES)


def _axes():
    return lax.axis_index("x"), lax.axis_index("y"), lax.axis_index("c")


NN = (((1,), (0,)), ((), ()))
NT = (((1,), (1,)), ((), ()))
TN = (((0,), (0,)), ((), ()))


def _mm(name, a, b, out_shape, out_dtype, grid, a_spec, b_spec, o_spec, dims, acc_shape):
    nk = grid[2]

    def body(a_ref, b_ref, o_ref, *scratch):
        bv = b_ref[...]
        if bv.ndim == 3:
            bv = bv.reshape(-1, bv.shape[-1])
        part = lax.dot_general(a_ref[...], bv, dims, preferred_element_type=F32)

        def put(v):
            o_ref[...] = v.reshape(o_ref.shape).astype(o_ref.dtype)

        if nk == 1:
            put(part)
        else:
            acc = scratch[0]
            k = pl.program_id(2)

            @pl.when(k == 0)
            def _():
                acc[...] = part

            @pl.when(k > 0)
            def _():
                acc[...] += part

            @pl.when(k == nk - 1)
            def _():
                put(acc[...])

    scratch = [pltpu.VMEM(acc_shape, F32)] if nk > 1 else []
    return pl.pallas_call(
        body, name=name, grid=grid, in_specs=[a_spec, b_spec], out_specs=o_spec,
        out_shape=jax.ShapeDtypeStruct(out_shape, out_dtype), scratch_shapes=scratch,
        compiler_params=_params(("parallel", "parallel", "arbitrary")),
    )(a, b)


def _tile(n, target):
    best = None
    for t in range(128, min(n, target) + 1, 128):
        if n % t == 0:
            best = t
    assert best is not None, (n, target)
    return best


def _token_rows(t, width):
    tt = t
    while tt > MM_TM and tt * width * jnp.dtype(MXU_DTYPE).itemsize * 2 > MM_IN_BYTES:
        tt //= 2
    return tt


def mm_fwd(name, a, w4, mode, out_dtype):
    t = a.shape[0]
    _, r, c = w4.shape
    tm = min(t, MM_TM)
    if mode == "col":
        tn = _tile(c, MM_TN)
        npj = c // tn
        grid = (t // tm, N_CHIP * npj, 1)
        a_spec = pl.BlockSpec((tm, r), lambda i, j, k: (i, 0))
        b_spec = pl.BlockSpec((None, r, tn), lambda i, j, k: (j // npj, 0, j % npj))
        o_spec = pl.BlockSpec((tm, tn), lambda i, j, k: (i, j))
        return _mm(name, a, w4, (t, N_CHIP * c), out_dtype, grid, a_spec, b_spec, o_spec, NN, (tm, tn))
    if w4.size * w4.dtype.itemsize <= MM_W1:
        grid = (t // tm, 1, 1)
        a_spec = pl.BlockSpec((tm, N_CHIP * r), lambda i, j, k: (i, 0))
        b_spec = pl.BlockSpec((N_CHIP, r, c), lambda i, j, k: (0, 0, 0))
        o_spec = pl.BlockSpec((tm, c), lambda i, j, k: (i, 0))
        return _mm(name, a, w4, (t, c), out_dtype, grid, a_spec, b_spec, o_spec, NN, (tm, c))
    grid = (t // tm, 1, N_CHIP)
    a_spec = pl.BlockSpec((tm, r), lambda i, j, k: (i, k))
    b_spec = pl.BlockSpec((None, r, c), lambda i, j, k: (k, 0, 0))
    o_spec = pl.BlockSpec((tm, c), lambda i, j, k: (i, 0))
    return _mm(name, a, w4, (t, c), out_dtype, grid, a_spec, b_spec, o_spec, NN, (tm, c))


def mm_dx(name, dy, w4, mode, out_dtype):
    t = dy.shape[-2]
    _, r, c = w4.shape
    tm = min(t, MM_TM)
    if mode == "col":
        tn, npj = c, 1
        hb = N_CHIP // 2 * npj
        grid = (t // tm, 1, N_CHIP * npj)
        if dy.ndim == 3:
            a_spec = pl.BlockSpec((None, tm, tn), lambda i, j, k: (k // hb, i, k % hb))
        else:
            a_spec = pl.BlockSpec((tm, tn), lambda i, j, k: (i, k))
        b_spec = pl.BlockSpec((None, r, tn), lambda i, j, k: (k // npj, 0, k % npj))
        o_spec = pl.BlockSpec((tm, r), lambda i, j, k: (i, 0))
        return _mm(name, dy, w4, (t, r), out_dtype, grid, a_spec, b_spec, o_spec, NT, (tm, r))
    if N_CHIP * r <= MM_K1:
        grid = (t // tm, 1, 1)
        a_spec = pl.BlockSpec((tm, c), lambda i, j, k: (i, 0))
        b_spec = pl.BlockSpec((N_CHIP, r, c), lambda i, j, k: (0, 0, 0))
        o_spec = pl.BlockSpec((tm, N_CHIP * r), lambda i, j, k: (i, 0))
        return _mm(name, dy, w4, (t, N_CHIP * r), out_dtype, grid, a_spec, b_spec, o_spec, NT, (tm, N_CHIP * r))
    grid = (t // tm, N_CHIP, 1)
    a_spec = pl.BlockSpec((tm, c), lambda i, j, k: (i, 0))
    b_spec = pl.BlockSpec((None, r, c), lambda i, j, k: (j, 0, 0))
    o_spec = pl.BlockSpec((tm, r), lambda i, j, k: (i, j))
    return _mm(name, dy, w4, (t, N_CHIP * r), out_dtype, grid, a_spec, b_spec, o_spec, NT, (tm, r))


def mm_dw(name, a, dy, mode, shape3):
    t = a.shape[0]
    _, r, c = shape3
    if mode == "col":
        tn = _tile(c, MM_TN)
        npj = c // tn
        tt = _token_rows(t, r + tn)
        grid = (1, N_CHIP * npj, t // tt)
        a_spec = pl.BlockSpec((tt, r), lambda i, j, k: (k, 0))
        hb = N_CHIP // 2 * npj
        if dy.ndim == 3:
            b_spec = pl.BlockSpec((None, tt, tn), lambda i, j, k: (j // hb, k, j % hb))
        else:
            b_spec = pl.BlockSpec((tt, tn), lambda i, j, k: (k, j))
        o_spec = pl.BlockSpec((None, r, tn), lambda i, j, k: (j // npj, 0, j % npj))
        return _mm(name, a, dy, shape3, MXU_DTYPE, grid, a_spec, b_spec, o_spec, TN, (r, tn))
    if N_CHIP * r <= MM_K1:
        tt = _token_rows(t, N_CHIP * r + c)
        grid = (1, 1, t // tt)
        a_spec = pl.BlockSpec((tt, N_CHIP * r), lambda i, j, k: (k, 0))
        b_spec = pl.BlockSpec((tt, c), lambda i, j, k: (k, 0))
        o_spec = pl.BlockSpec((N_CHIP, r, c), lambda i, j, k: (0, 0, 0))
        return _mm(name, a, dy, shape3, MXU_DTYPE, grid, a_spec, b_spec, o_spec, TN, (N_CHIP * r, c))
    tt = _token_rows(t, r + c)
    grid = (N_CHIP, 1, t // tt)
    a_spec = pl.BlockSpec((tt, r), lambda i, j, k: (k, i))
    b_spec = pl.BlockSpec((tt, c), lambda i, j, k: (k, 0))
    o_spec = pl.BlockSpec((None, r, c), lambda i, j, k: (i, 0, 0))
    return _mm(name, a, dy, shape3, MXU_DTYPE, grid, a_spec, b_spec, o_spec, TN, (r, c))


def _rms_bwd(x, g, dh):
    r = lax.rsqrt(jnp.mean(x * x, axis=-1, keepdims=True) + NORM_EPS)
    xhat = x * r
    dyg = dh * g
    dx = r * (dyg - xhat * jnp.mean(dyg * xhat, axis=-1, keepdims=True))
    return dx, jnp.sum(dh * xhat, axis=0, keepdims=True)


def mm_dx_norms(name, dy, w4, x, g_pre, dres, prev, after):
    t = dy.shape[-2]
    _, r, c = w4.shape
    tm = min(t, MM_TM // 4)
    nt, nk = t // tm, N_CHIP
    hb = N_CHIP // 2
    chained = prev is not None

    def body(dy_ref, w_ref, x_ref, dres_ref, g_ref, *rest):
        rest = rest[1:] if after is not None else rest
        if chained:
            y_ref, gp_ref, dx_ref, dg_ref, dyp_ref, dgp_ref, acc = rest
        else:
            dx_ref, dg_ref, acc = rest
        k, i = pl.program_id(0), pl.program_id(1)
        part = lax.dot_general(dy_ref[...], w_ref[...], NT, preferred_element_type=F32)

        @pl.when(k == 0)
        def _():
            acc[i] = part

        @pl.when(k > 0)
        def _():
            acc[i] += part

        def add_to(ref, v):
            @pl.when(i == 0)
            def _():
                ref[...] = v

            @pl.when(i > 0)
            def _():
                ref[...] += v

        @pl.when(k == nk - 1)
        def _():
            dx, dg = _rms_bwd(x_ref[...], g_ref[...], acc[i])
            dxs = dres_ref[...] + dx
            dx_ref[...] = dxs
            add_to(dg_ref, dg)
            if chained:
                dyp, dgp = _rms_bwd(y_ref[...], gp_ref[...], dxs)
                dyp_ref[...] = (prev[2] * dyp).astype(dyp_ref.dtype)
                add_to(dgp_ref, prev[2] * dgp)

    if dy.ndim == 3:
        dy_spec = pl.BlockSpec((None, tm, c), lambda k, i: (k // hb, i, k % hb))
    else:
        dy_spec = pl.BlockSpec((tm, c), lambda k, i: (i, k))
    rows = pl.BlockSpec((tm, r), lambda k, i: (jnp.where(k == nk - 1, i, 0), 0))
    gain = pl.BlockSpec((1, r), lambda k, i: (0, 0))
    in_specs = [dy_spec, pl.BlockSpec((None, r, c), lambda k, i: (k, 0, 0)), rows, rows, gain]
    args = [dy, w4, x, dres, g_pre]
    if after is not None:
        in_specs.append(pl.BlockSpec(memory_space=pl.ANY))
        args.append(after)
    out_specs = [rows, gain]
    out_shape = [jax.ShapeDtypeStruct((t, r), F32), jax.ShapeDtypeStruct((1, r), F32)]
    if chained:
        in_specs += [rows, gain]
        args += [prev[0], prev[1]]
        out_specs += [rows, gain]
        out_shape += [jax.ShapeDtypeStruct((t, r), MXU_DTYPE), jax.ShapeDtypeStruct((1, r), F32)]
    res = pl.pallas_call(
        body, name=name, grid=(nk, nt), in_specs=in_specs, out_specs=out_specs, out_shape=out_shape,
        scratch_shapes=[pltpu.VMEM((nt, tm, r), F32)], compiler_params=_params(("arbitrary", "arbitrary")),
    )(*args)
    return tuple(res) if chained else (res[0], res[1], None, None)


def _rowwise(name, fn, rows, pars, outs, accs=(), tm=256, ncol=1):
    t = rows[0][0].shape[0]
    nrow, npar, nout = len(rows), len(pars), len(outs)

    def body(*refs):
        vals = [r[...] for r in refs[:nrow + npar]]
        res = fn(*vals)
        out_refs = refs[nrow + npar:nrow + npar + nout]
        acc_refs = refs[nrow + npar + nout:]
        for o, v in zip(out_refs, res[:nout]):
            o[...] = v.astype(o.dtype)
        i = pl.program_id(1)
        for a, v in zip(acc_refs, res[nout:]):
            @pl.when(i == 0)
            def _(a=a, v=v):
                a[...] = v.astype(F32)

            @pl.when(i > 0)
            def _(a=a, v=v):
                a[...] += v.astype(F32)

    in_specs = [pl.BlockSpec((tm, w), functools.partial(lambda j, i, b: (i, b + j), b=b)) for _, w, b in rows]
    for arr, w in pars:
        if w is None:
            in_specs.append(pl.BlockSpec(arr.shape, lambda j, i: (0, 0)))
        else:
            in_specs.append(pl.BlockSpec((1, w), lambda j, i: (0, j)))
    out_specs = [pl.BlockSpec((tm, w), lambda j, i: (i, j)) for _, w, _ in outs]
    out_specs += [pl.BlockSpec((1, w), lambda j, i: (0, j)) for _, w in accs]
    out_shape = [jax.ShapeDtypeStruct((t, tw), dt) for tw, _, dt in outs]
    out_shape += [jax.ShapeDtypeStruct((1, tw), F32) for tw, _ in accs]
    res = pl.pallas_call(
        body, name=name, grid=(ncol, t // tm), in_specs=in_specs, out_specs=out_specs, out_shape=out_shape,
        compiler_params=_params(("parallel", "arbitrary" if accs else "parallel")),
    )(*[r[0] for r in rows], *[p[0] for p in pars])
    return res


def _rms(x, g):
    xf = x.astype(F32)
    return xf * lax.rsqrt(jnp.mean(xf * xf, axis=-1, keepdims=True) + NORM_EPS) * g


def _silu(x):
    return x * jax.nn.sigmoid(x)


def rms_fwd(name, x, g):
    d = x.shape[1]
    return _rowwise(name, lambda x, g: (_rms(x, g),), [(x, d, 0)], [(g, None)], [(d, d, MXU_DTYPE)], tm=512)[0]


def rms_bwd(name, x, g, dh, dres):
    d = x.shape[1]

    def fn(x, dh, dres, g):
        _, vjp = jax.vjp(_rms, x, g)
        dx, dg = vjp(dh.astype(F32))
        return dres + dx, dg

    return _rowwise(name, fn, [(x, d, 0), (dh, d, 0), (dres, d, 0)], [(g, None)], [(d, d, F32)], [(d, d)], tm=256)


def mm_post(name, a, w4, x, g_post, scale, g_next):
    t = a.shape[0]
    _, r, c = w4.shape
    tm = min(t, MM_TM // 2)
    chained = g_next is not None

    def body(a_ref, w_ref, x_ref, gp_ref, *rest):
        gn_ref, y_ref, xn_ref, h_ref = rest if chained else (None,) + rest + (None,)
        y = lax.dot_general(a_ref[...], w_ref[...].reshape(N_CHIP * r, c), NN, preferred_element_type=F32)
        y_ref[...] = y
        xn = x_ref[...] + scale * _rms(y, gp_ref[...])
        xn_ref[...] = xn
        if chained:
            h_ref[...] = _rms(xn, gn_ref[...]).astype(h_ref.dtype)

    def rows(width):
        return pl.BlockSpec((tm, width), lambda i: (i, 0))

    gain = pl.BlockSpec((1, c), lambda i: (0, 0))
    in_specs = [rows(N_CHIP * r), pl.BlockSpec((N_CHIP, r, c), lambda i: (0, 0, 0)), rows(c), gain]
    args = [a, w4, x, g_post]
    out_specs, out_shape = [rows(c), rows(c)], [jax.ShapeDtypeStruct((t, c), F32)] * 2
    if chained:
        in_specs.append(gain)
        args.append(g_next)
        out_specs.append(rows(c))
        out_shape.append(jax.ShapeDtypeStruct((t, c), MXU_DTYPE))
    res = pl.pallas_call(
        body, name=name, grid=(t // tm,), in_specs=in_specs, out_specs=out_specs, out_shape=out_shape,
        compiler_params=_params(("parallel",)),
    )(*args)
    return res[0], res[1], (res[2] if chained else None)


def post_bwd(name, y, g, dx, scale):
    d = y.shape[1]

    def fn(y, dx, g):
        _, vjp = jax.vjp(lambda y, g: scale * _rms(y, g), y, g)
        return vjp(dx)

    return _rowwise(name, fn, [(y, d, 0), (dx, d, 0)], [(g, None)], [(d, d, MXU_DTYPE)], [(d, d)], tm=256)


def ffn_up(name, h, w4):
    t = h.shape[0]
    _, r, c = w4.shape
    tm = min(t, MM_TM)
    tn = _tile(c, MM_TM)
    npj = c // tn
    half = N_CHIP // 2

    def body(h_ref, wg_ref, wu_ref, gu_ref, a_ref):
        hv = h_ref[...]
        g = lax.dot_general(hv, wg_ref[...], NN, preferred_element_type=F32)
        u = lax.dot_general(hv, wu_ref[...], NN, preferred_element_type=F32)
        gu_ref[0] = g.astype(gu_ref.dtype)
        gu_ref[1] = u.astype(gu_ref.dtype)
        a_ref[...] = (_silu(g) * u).astype(a_ref.dtype)

    f = half * c
    return pl.pallas_call(
        body, name=name, grid=(t // tm, half * npj),
        in_specs=[pl.BlockSpec((tm, r), lambda i, j: (i, 0)),
                  pl.BlockSpec((None, r, tn), lambda i, j: (j // npj, 0, j % npj)),
                  pl.BlockSpec((None, r, tn), lambda i, j: (half + j // npj, 0, j % npj))],
        out_specs=[pl.BlockSpec((2, tm, tn), lambda i, j: (0, i, j)), pl.BlockSpec((tm, tn), lambda i, j: (i, j))],
        out_shape=[jax.ShapeDtypeStruct((2, t, f), MXU_DTYPE), jax.ShapeDtypeStruct((t, f), MXU_DTYPE)],
        compiler_params=_params(("parallel", "parallel")),
    )(h, w4, w4)


def ffn_down_dx(name, dy, w4, gu):
    t = dy.shape[0]
    _, r, c = w4.shape
    tm = min(t, MM_TM)

    def body(dy_ref, w_ref, gu_ref, o_ref):
        dyv = dy_ref[...]
        for n0 in range(0, r, MM_SLICE):
            cols = pl.ds(n0, MM_SLICE)
            da = lax.dot_general(dyv, w_ref[cols, :], NT, preferred_element_type=F32)
            gate, up = gu_ref[0, :, cols].astype(F32), gu_ref[1, :, cols].astype(F32)
            sg = jax.nn.sigmoid(gate)
            silu = gate * sg
            o_ref[0, :, cols] = (da * up * (sg + silu * (1.0 - sg))).astype(o_ref.dtype)
            o_ref[1, :, cols] = (da * silu).astype(o_ref.dtype)

    return pl.pallas_call(
        body, name=name, grid=(t // tm, N_CHIP),
        in_specs=[pl.BlockSpec((tm, c), lambda i, j: (i, 0)), pl.BlockSpec((None, r, c), lambda i, j: (j, 0, 0)),
                  pl.BlockSpec((2, tm, r), lambda i, j: (0, i, j))],
        out_specs=pl.BlockSpec((2, tm, r), lambda i, j: (0, i, j)),
        out_shape=jax.ShapeDtypeStruct((2, t, N_CHIP * r), MXU_DTYPE),
        compiler_params=_params(("parallel", "parallel")),
    )(dy, w4, gu)


def _head_gate(o, g):
    mu = jnp.mean(o, axis=-1, keepdims=True)
    var = jnp.mean(jnp.square(o - mu), axis=-1, keepdims=True)
    return _silu(g.astype(F32)) * ((o - mu) * lax.rsqrt(var + LN_EPS))


def head_gate_fwd(name, o, p, gate_blk):
    dv = RET_V_DIM
    return _rowwise(name, lambda o, g: (_head_gate(o, g),), [(o, dv, 0), (p, dv, gate_blk)], [],
                    [(RET_HEADS * dv, dv, MXU_DTYPE)], tm=512, ncol=RET_HEADS)[0]


def head_gate_bwd(name, o, p, gate_blk, da):
    dv = RET_V_DIM

    def fn(o, g, da):
        _, vjp = jax.vjp(_head_gate, o, g.astype(F32))
        return vjp(da.astype(F32))

    w = RET_HEADS * dv
    return _rowwise(name, fn, [(o, dv, 0), (p, dv, gate_blk), (da, dv, 0)], [],
                    [(w, dv, MXU_DTYPE), (w, dv, MXU_DTYPE)], tm=512, ncol=RET_HEADS)


def _ln_silu(u, g, b):
    mu = jnp.mean(u, axis=-1, keepdims=True)
    var = jnp.mean(jnp.square(u - mu), axis=-1, keepdims=True)
    return _silu((u - mu) * lax.rsqrt(var + LN_EPS) * g + b)


def ln_silu_fwd(name, u, g, b):
    d = u.shape[1]
    return _rowwise(name, lambda u, g, b: (_ln_silu(u, g, b),), [(u, d, 0)], [(g, None), (b, None)],
                    [(d, d, MXU_DTYPE)], tm=512)[0]


def ln_silu_bwd(name, u, g, b, dc):
    d = u.shape[1]

    def fn(u, dc, g, b):
        _, vjp = jax.vjp(_ln_silu, u, g, b)
        return vjp(dc.astype(F32))

    return _rowwise(name, fn, [(u, d, 0), (dc, d, 0)], [(g, None), (b, None)], [(d, d, F32)], [(d, d), (d, d)],
                    tm=256)


def _merge(g0, g1, g2, ya, yb, yc):
    s = jax.nn.sigmoid
    return s(g0.astype(F32)) * ya + s(g1.astype(F32)) * yb + s(g2.astype(F32)) * yc


def merge_fwd(name, p, blk, ya, yb, yc):
    d = ya.shape[1]
    rows = [(p, d, blk), (p, d, blk + 1), (p, d, blk + 2), (ya, d, 0), (yb, d, 0), (yc, d, 0)]
    return _rowwise(name, lambda *v: (_merge(*v),), rows, [], [(d, d, MXU_DTYPE)], tm=256)[0]


def merge_bwd(name, p, blk, ya, yb, yc, dmg):
    d = ya.shape[1]

    def fn(g0, g1, g2, ya, yb, yc, dmg):
        _, vjp = jax.vjp(_merge, g0.astype(F32), g1.astype(F32), g2.astype(F32), ya, yb, yc)
        return vjp(dmg.astype(F32))

    rows = [(p, d, blk), (p, d, blk + 1), (p, d, blk + 2), (ya, d, 0), (yb, d, 0), (yc, d, 0), (dmg, d, 0)]
    return _rowwise(name, fn, rows, [], [(d, d, MXU_DTYPE)] * 6, tm=256)


def concat_cols(name, pieces):
    t = pieces[0].shape[0]
    widths = [p.shape[1] for p in pieces]
    tm = 256

    def body(*refs):
        o_ref, at = refs[-1], 0
        for r, w in zip(refs[:-1], widths):
            o_ref[:, at:at + w] = r[...]
            at += w

    return pl.pallas_call(
        body, name=name, grid=(t // tm,),
        in_specs=[pl.BlockSpec((tm, w), lambda i: (i, 0)) for w in widths],
        out_specs=pl.BlockSpec((tm, sum(widths)), lambda i: (i, 0)),
        out_shape=jax.ShapeDtypeStruct((t, sum(widths)), pieces[0].dtype),
        compiler_params=_params(("parallel",)),
    )(*pieces)


def loss_head(name, y, target):
    t, d = y.shape
    tm = 512

    def body(y_ref, t_ref, dy_ref, loss_ref):
        err = y_ref[...] - t_ref[...]
        dy_ref[...] = err * (1.0 / d)
        part = jnp.sum(jnp.sum(err * err, axis=1, keepdims=True), axis=0, keepdims=True) * (0.5 / d)

        @pl.when(pl.program_id(0) == 0)
        def _():
            loss_ref[...] = part

        @pl.when(pl.program_id(0) > 0)
        def _():
            loss_ref[...] += part

    return pl.pallas_call(
        body, name=name, grid=(t // tm,),
        in_specs=[pl.BlockSpec((tm, d), lambda i: (i, 0))] * 2,
        out_specs=[pl.BlockSpec((tm, d), lambda i: (i, 0)), pl.BlockSpec((1, 1), lambda i: (0, 0))],
        out_shape=[jax.ShapeDtypeStruct((t, d), F32), jax.ShapeDtypeStruct((1, 1), F32)],
        compiler_params=_params(("arbitrary",)),
    )(y, target)


def _rot(x, cos2, sin2):
    return x * cos2 + pltpu.roll(x, RET_QK_DIM // 2, 1) * sin2


def _decay_mask(lg, n0, rows, cols):
    n = n0 + lax.broadcasted_iota(jnp.int32, (rows, cols), 0)
    m = lax.broadcasted_iota(jnp.int32, (rows, cols), 1)
    shift = CHUNK.bit_length() - 1
    dist = jnp.abs(n - m).astype(F32)
    return jnp.where((m >> shift) <= (n >> shift), jnp.exp(lg * dist), 0.0)


def _ret_specs(s):
    dk, dv, h = RET_QK_DIM, RET_V_DIM, RET_HEADS
    return [
        pl.BlockSpec((s, dk), lambda b, hh: (b, hh)),
        pl.BlockSpec((s, dk), lambda b, hh: (b, h + hh)),
        pl.BlockSpec((s, dv), lambda b, hh: (b, (2 * h * dk) // dv + hh)),
        pl.BlockSpec((s, dk), lambda b, hh: (b, 0)),
        pl.BlockSpec((s, dk), lambda b, hh: (b, 0)),
        pl.BlockSpec((None, 1, dk), lambda b, hh: (hh, 0, 0)),
    ]


def retention_fwd(name, p, cos2, sin2, log_g, nb, s):
    dk, dv, h = RET_QK_DIM, RET_V_DIM, RET_HEADS

    def body(q_ref, k_ref, v_ref, cos_ref, sin_ref, lg_ref, o_ref, kr_ref):
        lg = lg_ref[0:1, 0:1]
        kr = _rot(k_ref[...].astype(F32), cos_ref[...], sin_ref[...]) * (dk ** -0.5)
        kr_ref[...] = kr.astype(kr_ref.dtype)
        for qi in range(s // RET_TQ):
            n0, kmax = qi * RET_TQ, (qi + 1) * RET_TQ
            rows = pl.ds(n0, RET_TQ)
            qr = _rot(q_ref[rows, :].astype(F32), cos_ref[rows, :], sin_ref[rows, :]).astype(MXU_DTYPE)
            sc = lax.dot_general(qr, kr_ref[0:kmax, :], NT, preferred_element_type=F32)
            pm = (sc * _decay_mask(lg, n0, RET_TQ, kmax)).astype(MXU_DTYPE)
            o_ref[rows, :] = lax.dot_general(pm, v_ref[0:kmax, :], NN, preferred_element_type=F32)

    return pl.pallas_call(
        body, name=name, grid=(nb, h), in_specs=_ret_specs(s),
        out_specs=pl.BlockSpec((s, dv), lambda b, hh: (b, hh)),
        out_shape=jax.ShapeDtypeStruct((nb * s, h * dv), F32),
        scratch_shapes=[pltpu.VMEM((s, dk), MXU_DTYPE)],
        compiler_params=_params(("parallel", "parallel")),
    )(p, p, p, cos2, sin2, log_g)


def retention_bwd(name, p, cos2, sin2, log_g, do, nb, s):
    dk, dv, h = RET_QK_DIM, RET_V_DIM, RET_HEADS

    def body(q_ref, k_ref, v_ref, cos_ref, sin_ref, lg_ref, do_ref, dq_ref, dk_ref, dv_ref, kr_ref, dk_acc, dv_acc):
        lg = lg_ref[0:1, 0:1]
        kr = _rot(k_ref[...].astype(F32), cos_ref[...], sin_ref[...]) * (dk ** -0.5)
        kr_ref[...] = kr.astype(kr_ref.dtype)
        dk_acc[...] = jnp.zeros_like(dk_acc)
        dv_acc[...] = jnp.zeros_like(dv_acc)
        for qi in range(s // RET_TQ):
            n0, kmax = qi * RET_TQ, (qi + 1) * RET_TQ
            rows = pl.ds(n0, RET_TQ)
            cq, sq = cos_ref[rows, :], sin_ref[rows, :]
            qr = _rot(q_ref[rows, :].astype(F32), cq, sq).astype(MXU_DTYPE)
            dob = do_ref[rows, :]
            mask = _decay_mask(lg, n0, RET_TQ, kmax)
            sc = lax.dot_general(qr, kr_ref[0:kmax, :], NT, preferred_element_type=F32)
            pm = (sc * mask).astype(MXU_DTYPE)
            dv_acc[0:kmax, :] += lax.dot_general(pm, dob, TN, preferred_element_type=F32)
            dp = lax.dot_general(dob, v_ref[0:kmax, :], NT, preferred_element_type=F32)
            ds = (dp * mask).astype(MXU_DTYPE)
            dqr = lax.dot_general(ds, kr_ref[0:kmax, :], NN, preferred_element_type=F32)
            dq_ref[rows, :] = _rot(dqr, cq, -sq).astype(dq_ref.dtype)
            dk_acc[0:kmax, :] += lax.dot_general(ds, qr, TN, preferred_element_type=F32)
        dkr = dk_acc[...] * (dk ** -0.5)
        dk_ref[...] = _rot(dkr, cos_ref[...], -sin_ref[...]).astype(dk_ref.dtype)
        dv_ref[...] = dv_acc[...].astype(dv_ref.dtype)

    t = nb * s
    return pl.pallas_call(
        body, name=name, grid=(nb, h),
        in_specs=_ret_specs(s) + [pl.BlockSpec((s, dv), lambda b, hh: (b, hh))],
        out_specs=[pl.BlockSpec((s, dk), lambda b, hh: (b, hh)), pl.BlockSpec((s, dk), lambda b, hh: (b, hh)),
                   pl.BlockSpec((s, dv), lambda b, hh: (b, hh))],
        out_shape=[jax.ShapeDtypeStruct((t, h * dk), MXU_DTYPE), jax.ShapeDtypeStruct((t, h * dk), MXU_DTYPE),
                   jax.ShapeDtypeStruct((t, h * dv), MXU_DTYPE)],
        scratch_shapes=[pltpu.VMEM((s, dk), MXU_DTYPE), pltpu.VMEM((s, dk), F32), pltpu.VMEM((s, dv), F32)],
        compiler_params=_params(("parallel", "parallel")),
    )(p, p, p, cos2, sin2, log_g, do)


def _conv_grid(t, d, nb):
    s = t // nb
    ns, nc = s // CONV_TS, d // CONV_TC
    return s, ns, nc


def _shifted(pad_ref, sh_ref, offsets):
    n = sh_ref.shape[1]
    for b in sorted({off % SUBLANES for off in offsets} - {0}):
        sh_ref[b - 1] = pad_ref[pl.ds(b, n), :]

    def read(off, r0):
        a, b = off - off % SUBLANES + r0, off % SUBLANES
        return pad_ref[pl.ds(a, SUBLANES), :] if b == 0 else sh_ref[b - 1, pl.ds(a, SUBLANES), :]

    return read


def _causal_taps(pad_ref, sh_ref, w_ref, k, emit):
    offs = [CONV_PAD - (k - 1) + j for j in range(k)]
    read = _shifted(pad_ref, sh_ref, offs)
    for r0 in range(0, CONV_TS, CONV_ROWS):
        accs = [None] * len(CONV_TILES)
        for j in range(k):
            wj = w_ref[j]
            for q, dr in enumerate(CONV_TILES):
                term = wj * read(offs[j], r0 + dr)
                accs[q] = term if accs[q] is None else accs[q] + term
        emit(r0, jnp.concatenate(accs, axis=0))


def _tap_tiles(w):
    return jnp.broadcast_to(w[:, None, :], (w.shape[0], SUBLANES, w.shape[1]))


def _tap_spec(k):
    return pl.BlockSpec((k, SUBLANES, CONV_TC), lambda c, b, si: (0, 0, c))


def _carry_past(pad_ref, s_idx):
    @pl.when(s_idx == 0)
    def _():
        pad_ref[0:CONV_PAD, :] = jnp.zeros((CONV_PAD, pad_ref.shape[1]), F32)

    @pl.when(s_idx > 0)
    def _():
        pad_ref[0:CONV_PAD, :] = pad_ref[CONV_TS:CONV_TS + CONV_PAD, :]


def _carry_future(pad_ref, s_idx):
    @pl.when(s_idx == 0)
    def _():
        pad_ref[CONV_TS:CONV_TS + CONV_PAD, :] = jnp.zeros((CONV_PAD, pad_ref.shape[1]), F32)

    @pl.when(s_idx > 0)
    def _():
        pad_ref[CONV_TS:CONV_TS + CONV_PAD, :] = pad_ref[0:CONV_PAD, :]


def _conv_bwd_taps(pad_ref, sh_ref, w_ref, dw_acc, k, x_rows, emit, mix):
    read = _shifted(pad_ref, sh_ref, range(k))
    for r0 in range(0, CONV_TS, CONV_ROWS):
        ops = x_rows(r0)
        x = mix(ops)
        accs = [None] * len(CONV_TILES)
        for j in range(k):
            wj, dwj = w_ref[j], None
            for q, dr in enumerate(CONV_TILES):
                sh = read(k - 1 - j, r0 + dr)
                term = wj * sh
                accs[q] = term if accs[q] is None else accs[q] + term
                prod = x[dr:dr + SUBLANES] * sh
                dwj = prod if dwj is None else dwj + prod
            dw_acc[j] += dwj
        emit(r0, ops, jnp.concatenate(accs, axis=0))


def _conv_bwd_edges(dw_acc, dw_ref, nb, ns, extra=()):
    first = jnp.logical_and(pl.program_id(1) == 0, pl.program_id(2) == 0)
    last = jnp.logical_and(pl.program_id(1) == nb - 1, pl.program_id(2) == ns - 1)

    @pl.when(first)
    def _():
        dw_acc[...] = jnp.zeros_like(dw_acc)
        for r in extra:
            r[...] = jnp.zeros_like(r)

    def finish():
        @pl.when(last)
        def _():
            dw_ref[...] = jnp.sum(dw_acc[...], axis=1)

    return finish


def short_conv_fwd(name, p, blk_b, w, nb):
    t = p.shape[0]
    d = w.shape[1]
    s, ns, nc = _conv_grid(t, d, nb)
    cb = d // CONV_TC

    def body(b_ref, c_ref, x_ref, w_ref, y_ref, cz_ref, pad_ref, sh_ref):
        _carry_past(pad_ref, pl.program_id(2))
        pad_ref[CONV_PAD:CONV_PAD + CONV_TS, :] = c_ref[...].astype(F32) * x_ref[...].astype(F32)

        def emit(r0, cz):
            rows = pl.ds(r0, CONV_ROWS)
            cz_ref[rows, :] = cz
            y_ref[rows, :] = (b_ref[rows, :].astype(F32) * cz).astype(y_ref.dtype)

        _causal_taps(pad_ref, sh_ref, w_ref, SC_KERNEL, emit)

    def pspec(off):
        return pl.BlockSpec((CONV_TS, CONV_TC), lambda c, b, si: (b * ns + si, (blk_b + off) * cb + c))

    ospec = pl.BlockSpec((CONV_TS, CONV_TC), lambda c, b, si: (b * ns + si, c))
    return pl.pallas_call(
        body, name=name, grid=(nc, nb, ns),
        in_specs=[pspec(0), pspec(1), pspec(2), _tap_spec(SC_KERNEL)],
        out_specs=[ospec, ospec],
        out_shape=[jax.ShapeDtypeStruct((t, d), MXU_DTYPE), jax.ShapeDtypeStruct((t, d), F32)],
        scratch_shapes=CONV_SCRATCH,
        compiler_params=_params(("parallel", "arbitrary", "arbitrary")),
    )(p, p, p, _tap_tiles(w))


def short_conv_bwd(name, p, blk_b, w, cz, dy, nb):
    t = p.shape[0]
    d = w.shape[1]
    s, ns, nc = _conv_grid(t, d, nb)
    cb = d // CONV_TC

    def body(b_ref, c_ref, x_ref, w_ref, cz_ref, dy_ref, db_ref, dc_ref, dx_ref, dw_ref, pad_ref, sh_ref, dw_acc):
        _carry_future(pad_ref, pl.program_id(2))
        dyv = dy_ref[...].astype(F32)
        db_ref[...] = (dyv * cz_ref[...]).astype(db_ref.dtype)
        pad_ref[0:CONV_TS, :] = dyv * b_ref[...].astype(F32)
        finish = _conv_bwd_edges(dw_acc, dw_ref, nb, ns)

        def x_rows(r0):
            rows = pl.ds(r0, CONV_ROWS)
            return c_ref[rows, :].astype(F32), x_ref[rows, :].astype(F32)

        def emit(r0, cx, dz):
            rows = pl.ds(r0, CONV_ROWS)
            dc_ref[rows, :] = (dz * cx[1]).astype(dc_ref.dtype)
            dx_ref[rows, :] = (dz * cx[0]).astype(dx_ref.dtype)

        _conv_bwd_taps(pad_ref, sh_ref, w_ref, dw_acc, SC_KERNEL, x_rows, emit, lambda cx: cx[0] * cx[1])
        finish()

    def row(b, si):
        return b * ns + (ns - 1 - si)

    def pspec(off):
        return pl.BlockSpec((CONV_TS, CONV_TC), lambda c, b, si: (row(b, si), (blk_b + off) * cb + c))

    ospec = pl.BlockSpec((CONV_TS, CONV_TC), lambda c, b, si: (row(b, si), c))
    wspec = pl.BlockSpec((SC_KERNEL, CONV_TC), lambda c, b, si: (0, c))
    return pl.pallas_call(
        body, name=name, grid=(nc, nb, ns),
        in_specs=[pspec(0), pspec(1), pspec(2), _tap_spec(SC_KERNEL), ospec, ospec],
        out_specs=[ospec, ospec, ospec, wspec],
        out_shape=[jax.ShapeDtypeStruct((t, d), MXU_DTYPE)] * 3 + [jax.ShapeDtypeStruct((SC_KERNEL, d), F32)],
        scratch_shapes=CONV_SCRATCH + [pltpu.VMEM((SC_KERNEL, SUBLANES, CONV_TC), F32)],
        compiler_params=_params(("parallel", "arbitrary", "arbitrary")),
    )(p, p, p, _tap_tiles(w), cz, dy)


def conformer_conv_fwd(name, p, blk_a, w, bias, nb):
    t = p.shape[0]
    d = w.shape[1]
    s, ns, nc = _conv_grid(t, d, nb)
    cb = d // CONV_TC

    def body(a_ref, b_ref, w_ref, bias_ref, u_ref, pad_ref, sh_ref):
        _carry_past(pad_ref, pl.program_id(2))
        pad_ref[CONV_PAD:CONV_PAD + CONV_TS, :] = a_ref[...].astype(F32) * jax.nn.sigmoid(b_ref[...].astype(F32))

        def emit(r0, u):
            u_ref[pl.ds(r0, CONV_ROWS), :] = u + bias_ref[0:1, :]

        _causal_taps(pad_ref, sh_ref, w_ref, CF_KERNEL, emit)

    def pspec(off):
        return pl.BlockSpec((CONV_TS, CONV_TC), lambda c, b, si: (b * ns + si, (blk_a + off) * cb + c))

    return pl.pallas_call(
        body, name=name, grid=(nc, nb, ns),
        in_specs=[pspec(0), pspec(1), _tap_spec(CF_KERNEL), pl.BlockSpec((SUBLANES, CONV_TC), lambda c, b, si: (0, c))],
        out_specs=pl.BlockSpec((CONV_TS, CONV_TC), lambda c, b, si: (b * ns + si, c)),
        out_shape=jax.ShapeDtypeStruct((t, d), F32),
        scratch_shapes=CONV_SCRATCH,
        compiler_params=_params(("parallel", "arbitrary", "arbitrary")),
    )(p, p, _tap_tiles(w), jnp.broadcast_to(bias, (SUBLANES, d)))


def conformer_conv_bwd(name, p, blk_a, w, du, nb):
    t = p.shape[0]
    d = w.shape[1]
    s, ns, nc = _conv_grid(t, d, nb)
    cb = d // CONV_TC

    def body(a_ref, b_ref, w_ref, du_ref, da_ref, db_ref, dw_ref, dbias_ref, pad_ref, sh_ref, dw_acc):
        _carry_future(pad_ref, pl.program_id(2))
        duv = du_ref[...]
        pad_ref[0:CONV_TS, :] = duv
        finish = _conv_bwd_edges(dw_acc, dw_ref, nb, ns, extra=(dbias_ref,))
        dbias_ref[...] += jnp.sum(duv, axis=0, keepdims=True)

        def x_rows(r0):
            rows = pl.ds(r0, CONV_ROWS)
            return a_ref[rows, :].astype(F32), jax.nn.sigmoid(b_ref[rows, :].astype(F32))

        def emit(r0, asg, du0):
            rows = pl.ds(r0, CONV_ROWS)
            av, sg = asg
            da_ref[rows, :] = (du0 * sg).astype(da_ref.dtype)
            db_ref[rows, :] = (du0 * av * sg * (1.0 - sg)).astype(db_ref.dtype)

        _conv_bwd_taps(pad_ref, sh_ref, w_ref, dw_acc, CF_KERNEL, x_rows, emit, lambda asg: asg[0] * asg[1])
        finish()

    def row(b, si):
        return b * ns + (ns - 1 - si)

    def pspec(off):
        return pl.BlockSpec((CONV_TS, CONV_TC), lambda c, b, si: (row(b, si), (blk_a + off) * cb + c))

    ospec = pl.BlockSpec((CONV_TS, CONV_TC), lambda c, b, si: (row(b, si), c))
    wspec = pl.BlockSpec((CF_KERNEL, CONV_TC), lambda c, b, si: (0, c))
    bspec = pl.BlockSpec((1, CONV_TC), lambda c, b, si: (0, c))
    return pl.pallas_call(
        body, name=name, grid=(nc, nb, ns),
        in_specs=[pspec(0), pspec(1), _tap_spec(CF_KERNEL), ospec],
        out_specs=[ospec, ospec, wspec, bspec],
        out_shape=[jax.ShapeDtypeStruct((t, d), MXU_DTYPE)] * 2
        + [jax.ShapeDtypeStruct((CF_KERNEL, d), F32), jax.ShapeDtypeStruct((1, d), F32)],
        scratch_shapes=CONV_SCRATCH + [pltpu.VMEM((CF_KERNEL, SUBLANES, CONV_TC), F32)],
        compiler_params=_params(("parallel", "arbitrary", "arbitrary")),
    )(p, p, _tap_tiles(w), du)


BLOCKS = ("ffn1", "mixer", "ffn2")
BLOCK_WEIGHTS = {"ffn1": ("ffn1_w_gu", "ffn1_w_down"), "mixer": ("w_in", "w_ret_o", "w_sc_o", "w_cf_o", "w_o"),
                 "ffn2": ("ffn2_w_gu", "ffn2_w_down")}
BIG = BLOCK_WEIGHTS["ffn1"] + BLOCK_WEIGHTS["mixer"] + BLOCK_WEIGHTS["ffn2"]
MODE = {"ffn1_w_gu": "col", "ffn1_w_down": "row", "w_in": "col", "w_ret_o": "row", "w_sc_o": "row",
        "w_cf_o": "row", "w_o": "row", "ffn2_w_gu": "col", "ffn2_w_down": "row"}
NORM_OF = {"ffn1": 0, "mixer": 2, "ffn2": 4}
BLK_GATE, BLK_SCB, BLK_CFA, BLK_MERGE = 2, 3, 6, 8


def _rope_tables(positions):
    half = RET_QK_DIM // 2
    inv_freq = ROPE_BASE ** (-jnp.arange(half, dtype=F32) / half)
    ang = positions.astype(F32)[..., None] * inv_freq
    cos, sin = jnp.cos(ang), jnp.sin(ang)
    nb, s = positions.shape
    cos2 = jnp.concatenate([cos, cos], axis=-1).reshape(nb * s, RET_QK_DIM)
    sin2 = jnp.concatenate([-sin, sin], axis=-1).reshape(nb * s, RET_QK_DIM)
    return cos2, sin2


def _log_gamma():
    lg = jnp.log(1.0 - 2.0 ** (-5.0 - jnp.arange(RET_HEADS, dtype=F32)))
    return jnp.broadcast_to(lg[:, None, None], (RET_HEADS, 1, RET_QK_DIM))


def _ffn_fwd(xs, h, w, tag, g_post, g_next):
    gu, a = ffn_up("ffn_up", h, w[tag + "_w_gu"])
    y, out, h_next = mm_post("ffn_down", a, w[tag + "_w_down"], xs, g_post, 0.5, g_next)
    return out, h_next, dict(x=xs, h=h, gu=gu, a=a, y=y, w=w)


def _ffn_bwd(dxs, dy, sv, tag, g_pre, push, prev):
    w = sv["w"]
    gu_w, down_w = w[tag + "_w_gu"], w[tag + "_w_down"]
    dgu = ffn_down_dx("ffn_down_dx", dy, down_w, sv["gu"])
    grads = {tag + "_w_down": mm_dw("ffn_down_dw", sv["a"], dy, "row", down_w.shape),
             tag + "_w_gu": mm_dw("ffn_gu_dw", sv["h"], dgu, "col", gu_w.shape)}
    return mm_dx_norms("ffn_gu_dx", dgu, gu_w, sv["x"], g_pre, dxs, prev, push(grads))


def _mixer_fwd(xs, h, w, sm, g_post, g_next, rope, nb, s, mid):
    cos2, sin2, log_g = rope
    d = xs.shape[1]
    gate_blk = (BLK_GATE * d) // RET_V_DIM
    p = mm_fwd("mx_in", h, w["w_in"], "col", MXU_DTYPE)
    if mid is not None:
        sm = dict(sm, cf_dw_b=sm["cf_dw_b"] + mid(p))
    o = retention_fwd("ret_fwd", p, cos2, sin2, log_g, nb, s)
    ya_in = head_gate_fwd("ret_gate", o, p, gate_blk)
    yb_in, cz = short_conv_fwd("sc_fwd", p, BLK_SCB, sm["sc_conv_w"], nb)
    u1 = conformer_conv_fwd("cf_fwd", p, BLK_CFA, sm["cf_dw_w"], sm["cf_dw_b"], nb)
    yc_in = ln_silu_fwd("cf_ln", u1, sm["cf_ln_g"], sm["cf_ln_b"])
    ya = mm_fwd("mx_proj", ya_in, w["w_ret_o"], "row", F32)
    yb = mm_fwd("mx_proj", yb_in, w["w_sc_o"], "row", F32)
    yc = mm_fwd("mx_proj", yc_in, w["w_cf_o"], "row", F32)
    mg = merge_fwd("mx_merge", p, BLK_MERGE, ya, yb, yc)
    m, out, h_next = mm_post("mx_out", mg, w["w_o"], xs, g_post, 1.0, g_next)
    return out, h_next, dict(x=xs, h=h, p=p, o=o, ya_in=ya_in, yb_in=yb_in, cz=cz, u1=u1, yc_in=yc_in, ya=ya, yb=yb, yc=yc,
                     mg=mg, m=m, w=w)


def _mixer_bwd(dxs, dm, sv, sm, g_pre, rope, nb, s, push, prev):
    cos2, sin2, log_g = rope
    w, p = sv["w"], sv["p"]
    d = dxs.shape[1]
    gate_blk = (BLK_GATE * d) // RET_V_DIM
    grads, gsm = {}, {}

    def proj_bwd(wname, a_in, dy, out_dtype):
        grads[wname] = mm_dw("mx_proj_dw", a_in, dy, "row", w[wname].shape)
        return mm_dx("mx_proj_dx", dy, w[wname], "row", out_dtype)

    dmg = proj_bwd("w_o", sv["mg"], dm, MXU_DTYPE)
    dg0, dg1, dg2, dya, dyb, dyc = merge_bwd("mx_merge_bwd", p, BLK_MERGE, sv["ya"], sv["yb"], sv["yc"], dmg)
    dya_in = proj_bwd("w_ret_o", sv["ya_in"], dya, MXU_DTYPE)
    dyb_in = proj_bwd("w_sc_o", sv["yb_in"], dyb, MXU_DTYPE)
    dyc_in = proj_bwd("w_cf_o", sv["yc_in"], dyc, MXU_DTYPE)
    do, dgret = head_gate_bwd("ret_gate_bwd", sv["o"], p, gate_blk, dya_in)
    dq, dk, dv = retention_bwd("ret_bwd", p, cos2, sin2, log_g, do, nb, s)
    dscb, dscc, dscx, gsm["sc_conv_w"] = short_conv_bwd("sc_bwd", p, BLK_SCB, sm["sc_conv_w"], sv["cz"], dyb_in, nb)
    du1, dlg, dlb = ln_silu_bwd("cf_ln_bwd", sv["u1"], sm["cf_ln_g"], sm["cf_ln_b"], dyc_in)
    dcfa, dcfb, gsm["cf_dw_w"], dbias = conformer_conv_bwd("cf_bwd", p, BLK_CFA, sm["cf_dw_w"], du1, nb)
    gsm.update(cf_ln_g=dlg[0], cf_ln_b=dlb[0], cf_dw_b=dbias[0])
    dp = concat_cols("mx_dp", [dq, dk, dv, dgret, dscb, dscc, dscx, dcfa, dcfb, dg0, dg1, dg2])
    grads["w_in"] = mm_dw("mx_in_dw", sv["h"], dp, "col", w["w_in"].shape)
    return mm_dx_norms("mx_in_dx", dp, w["w_in"], sv["x"], g_pre, dxs, prev, push(grads)) + (gsm,)


def local_step(x, positions, target, small, fetch, push):
    nb, s, d = x.shape
    t = nb * s
    depth = small["norm_g"].shape[0]
    rope = _rope_tables(positions) + (_log_gamma(),)
    xs = x.reshape(t, d)
    token = [None]

    def gain(l, i):
        g = small["norm_g"][l, i][None, :]
        if token[0] is not None:
            g, token[0] = g + token[0], None
        return g

    def mixer_small(l):
        return dict(sc_conv_w=small["sc_conv_w"][l], cf_dw_w=small["cf_dw_w"][l], cf_dw_b=small["cf_dw_b"][l][None, :],
                    cf_ln_g=small["cf_ln_g"][l][None, :], cf_ln_b=small["cf_ln_b"][l][None, :])

    saved = {}
    order = [(l, blk) for l in range(depth) for blk in BLOCKS]
    h = None
    for at, (l, blk) in enumerate(order):
        w, token[0], mid = fetch(l, blk, xs)
        i0 = NORM_OF[blk]
        if h is None:
            h = rms_fwd("first_rms", xs, gain(l, i0))
        g_post = gain(l, i0 + 1)
        g_next = gain(order[at + 1][0], NORM_OF[order[at + 1][1]]) if at + 1 < len(order) else None
        if blk == "mixer":
            xs, h, saved[l, blk] = _mixer_fwd(xs, h, w, mixer_small(l), g_post, g_next, rope, nb, s, mid)
        else:
            xs, h, saved[l, blk] = _ffn_fwd(xs, h, w, blk, g_post, g_next)

    dxs, loss = loss_head("loss", xs, target.reshape(t, d))

    dnorm = [[None] * 6 for _ in range(depth)]
    gsmall = {n: [None] * depth for n in ("sc_conv_w", "cf_dw_w", "cf_dw_b", "cf_ln_g", "cf_ln_b")}
    def branch(group):
        l, blk = group
        sv = saved[group]
        return (sv["m"], gain(l, NORM_OF[blk] + 1), 1.0) if blk == "mixer" else (sv["y"], gain(l, NORM_OF[blk] + 1), 0.5)

    l, blk = order[-1]
    y, g_post, scale = branch(order[-1])
    dy, dnorm[l][NORM_OF[blk] + 1] = post_bwd("last_post_bwd", y, g_post, dxs, scale)
    for at in reversed(range(len(order))):
        l, blk = order[at]
        i0 = NORM_OF[blk]
        prev = branch(order[at - 1]) if at > 0 else None
        put = functools.partial(push, l, blk)
        if blk == "mixer":
            dxs, dnorm[l][i0], dy, dg_prev, gsm = _mixer_bwd(
                dxs, dy, saved[l, blk], mixer_small(l), gain(l, i0), rope, nb, s, put, prev)
            for n, v in gsm.items():
                gsmall[n][l] = v
        else:
            dxs, dnorm[l][i0], dy, dg_prev = _ffn_bwd(dxs, dy, saved[l, blk], blk, gain(l, i0), put, prev)
        if at > 0:
            dnorm[order[at - 1][0]][NORM_OF[order[at - 1][1]] + 1] = dg_prev

    gs = {n: jnp.stack(v) for n, v in gsmall.items()}
    gs["norm_g"] = jnp.stack([jnp.concatenate(r, axis=0) for r in dnorm])
    return loss, dxs.reshape(nb, s, d), gs


ANY = pl.BlockSpec(memory_space=pl.ANY)
HBM = pl.BlockSpec(memory_space=pltpu.HBM)
SEM = pl.BlockSpec(memory_space=pltpu.SEMAPHORE)
VMEM_WHOLE = pl.BlockSpec(memory_space=pltpu.VMEM)
EFFECT = pltpu.SideEffectType.DATAFLOW_SIDE_EFFECTING
TOKEN = jax.ShapeDtypeStruct((8, 128), F32)


def _other_chips(x, y):
    return [(1 - x, y), (x, 1 - y), (1 - x, 1 - y)]


def _remote(src, dst, send_sem, recv_sem, to):
    return pltpu.make_async_remote_copy(src_ref=src, dst_ref=dst, send_sem=send_sem, recv_sem=recv_sem,
                                        device_id=to, device_id_type=MESH)


def _in_hbm(v):
    return pltpu.with_memory_space_constraint(v, pltpu.HBM)


def place_quarters(ws, layer, ids, after):
    m = len(ws)

    def body(ids_ref, *refs):
        for w_ref, o_ref in zip(refs[:m], refs[m + 1:]):
            o_ref[...] = w_ref[...].astype(o_ref.dtype)

    def spec(w, where):
        return pl.BlockSpec((None, w.shape[1] // STREAM_STEPS, w.shape[2]), where)

    return pl.pallas_call(
        body, name="place_quarters",
        grid_spec=pltpu.PrefetchScalarGridSpec(
            num_scalar_prefetch=1, grid=(STREAM_STEPS,),
            in_specs=[spec(w, lambda i, ids_ref: (layer, i, 0)) for w in ws] + [ANY],
            out_specs=[spec(w, lambda i, ids_ref: (ids_ref[0], i, 0)) for w in ws]),
        out_shape=[jax.ShapeDtypeStruct((N_CHIP,) + w.shape[1:], MXU_DTYPE) for w in ws],
        compiler_params=_params(("parallel",)),
    )(ids, *ws, after)


def _gather_copies(lands, send, recv):
    x, y, c = _axes()
    me = 2 * x + y
    mine, theirs = [], []
    for a, ld in enumerate(lands):
        rh = ld.shape[1] // 2
        rows = pl.ds(c * rh, rh)
        for k, (px, py) in enumerate(_other_chips(x, y)):
            to = (px, py, c)
            mine.append(_remote(ld.at[me, rows, :], ld.at[me, rows, :], send.at[3 * a + k], recv.at[3 * a + k], to))
            got = ld.at[2 * px + py, rows, :]
            theirs.append(_remote(got, got, send.at[3 * a + k], recv.at[3 * a + k], to))
    return mine, theirs


def gather_start(name, groups, after):
    flat = [s for g in groups for s in g]
    n, ng = len(flat), len(groups)
    sizes = [len(g) for g in groups]

    def body(*refs):
        lands = refs[:n]
        sems = refs[n + 1:n + 1 + 2 * ng]
        token = refs[-1]
        at = 0
        for g, m in enumerate(sizes):
            mine, _ = _gather_copies(lands[at:at + m], sems[2 * g], sems[2 * g + 1])
            for cp in mine:
                cp.start()
            at += m
        token[...] = jnp.zeros_like(token)

    sem_shapes = []
    for m in sizes:
        sem_shapes += [pltpu.SemaphoreType.DMA((3 * m,))] * 2
    res = pl.pallas_call(
        body, name=name, in_specs=[HBM] * n + [ANY],
        out_specs=[SEM] * (2 * ng) + [HBM] * n + [VMEM_WHOLE],
        out_shape=sem_shapes + [pltpu.HBM(s.shape, s.dtype) for s in flat] + [TOKEN],
        input_output_aliases={i: 2 * ng + i for i in range(n)},
        compiler_params=pltpu.CompilerParams(has_side_effects=EFFECT),
    )(*[_in_hbm(s) for s in flat], after)
    sems, thru, token = res[:2 * ng], res[2 * ng:2 * ng + n], res[-1]
    out, at = [], 0
    for g, m in enumerate(sizes):
        out.append((sems[2 * g], sems[2 * g + 1], thru[at:at + m]))
        at += m
    return out, token


def gather_wait(lands, send, recv, after):
    m = len(lands)

    def body(*refs):
        mine, theirs = _gather_copies(refs[:m], refs[m], refs[m + 1])
        for cp in mine:
            cp.wait_send()
        for cp in theirs:
            cp.wait_recv()

    return pl.pallas_call(
        body, name="gather_wait", in_specs=[HBM] * m + [SEM, SEM, ANY], out_specs=[HBM] * m,
        out_shape=[pltpu.HBM(l.shape, l.dtype) for l in lands],
        input_output_aliases={i: i for i in range(m)},
        compiler_params=pltpu.CompilerParams(has_side_effects=EFFECT),
    )(*lands, send, recv, after)


def copy_start(name, bufs, copies, ncopy, after=()):
    n, k = len(bufs), len(after)

    def body(*refs):
        for cp in copies(refs[:n], refs[n + k], refs[n + k + 1])[0]:
            cp.start()
        refs[-1][...] = jnp.zeros_like(refs[-1])

    res = pl.pallas_call(
        body, name=name, in_specs=[HBM] * n + [ANY] * k, out_specs=[SEM, SEM] + [HBM] * n + [VMEM_WHOLE],
        out_shape=[pltpu.SemaphoreType.DMA((ncopy,))] * 2 + [pltpu.HBM(b.shape, b.dtype) for b in bufs] + [TOKEN],
        input_output_aliases={i: 2 + i for i in range(n)},
        compiler_params=pltpu.CompilerParams(has_side_effects=EFFECT),
    )(*[_in_hbm(b) for b in bufs], *after)
    return res[0], res[1], list(res[2:2 + n]), res[-1]


def copy_wait(name, bufs, send, recv, copies, after=()):
    n = len(bufs)

    def body(*refs):
        mine, theirs = copies(refs[:n], refs[n], refs[n + 1])
        for cp in mine:
            cp.wait_send()
        for cp in theirs:
            cp.wait_recv()

    return list(pl.pallas_call(
        body, name=name, in_specs=[HBM] * n + [SEM, SEM] + [ANY] * len(after), out_specs=[HBM] * n,
        out_shape=[pltpu.HBM(b.shape, b.dtype) for b in bufs], input_output_aliases={i: i for i in range(n)},
        compiler_params=pltpu.CompilerParams(has_side_effects=EFFECT),
    )(*bufs, send, recv, *after))


def _fill_copies(lands, send, recv):
    x, y, c = _axes()
    sib = (x, y, 1 - c)
    mine, theirs = [], []
    for a, ld in enumerate(lands):
        rh = ld.shape[1] // 2
        for k, (px, py) in enumerate(_other_chips(x, y)):
            got = ld.at[2 * px + py, pl.ds(c * rh, rh), :]
            mine.append(_remote(got, got, send.at[3 * a + k], recv.at[3 * a + k], sib))
            blk = ld.at[2 * px + py, pl.ds((1 - c) * rh, rh), :]
            theirs.append(_remote(blk, blk, send.at[3 * a + k], recv.at[3 * a + k], sib))
    return mine, theirs


def _presum_copies(grads, lands, send, recv):
    x, y, c = _axes()
    cps = []
    for a, (g, ld) in enumerate(zip(grads, lands)):
        rh = g.shape[1] // 2
        cps.append(_remote(g.at[:, pl.ds((1 - c) * rh, rh), :], ld, send.at[a], recv.at[a], (x, y, 1 - c)))
    return cps


def presum_start(grads):
    m = len(grads)

    def body(*refs):
        for cp in _presum_copies(refs[:m], refs[m:2 * m], refs[2 * m], refs[2 * m + 1]):
            cp.start()
        refs[-1][...] = jnp.zeros_like(refs[-1])

    lands = [lax.empty((g.shape[0], g.shape[1] // 2, g.shape[2]), g.dtype) for g in grads]
    res = pl.pallas_call(
        body, name="presum_start", in_specs=[HBM] * (2 * m), out_specs=[SEM, SEM] + [HBM] * (2 * m) + [VMEM_WHOLE],
        out_shape=[pltpu.SemaphoreType.DMA((m,))] * 2 + [pltpu.HBM(g.shape, g.dtype) for g in grads]
        + [pltpu.HBM(l.shape, l.dtype) for l in lands] + [TOKEN],
        input_output_aliases={i: 2 + i for i in range(2 * m)},
        compiler_params=pltpu.CompilerParams(has_side_effects=EFFECT),
    )(*[_in_hbm(g) for g in grads], *[_in_hbm(l) for l in lands])
    return res[0], res[1], res[2:2 + m], res[2 + m:2 + 2 * m], res[-1]


def presum_wait(grads, lands, send, recv, after):
    m = len(grads)

    def body(*refs):
        for cp in _presum_copies(refs[:m], refs[m:2 * m], refs[2 * m], refs[2 * m + 1]):
            cp.wait_send()
            cp.wait_recv()

    res = pl.pallas_call(
        body, name="presum_wait", in_specs=[HBM] * (2 * m) + [SEM, SEM] + [ANY] * len(after),
        out_specs=[HBM] * (2 * m),
        out_shape=[pltpu.HBM(g.shape, g.dtype) for g in grads] + [pltpu.HBM(l.shape, l.dtype) for l in lands],
        input_output_aliases={i: i for i in range(2 * m)},
        compiler_params=pltpu.CompilerParams(has_side_effects=EFFECT),
    )(*grads, *lands, send, recv, *after)
    return res[:m], res[m:]


def add_halves(gs, lands, ids):
    m = len(gs)

    def body(ids_ref, *refs):
        for a_ref, b_ref, o_ref in zip(refs[:m], refs[m:2 * m], refs[2 * m:]):
            o_ref[...] = (a_ref[...].astype(F32) + b_ref[...].astype(F32)).astype(o_ref.dtype)

    def spec(ld, where):
        return pl.BlockSpec((None,) + ld.shape[1:], where)

    return pl.pallas_call(
        body, name="add_halves",
        grid_spec=pltpu.PrefetchScalarGridSpec(
            num_scalar_prefetch=1, grid=(N_CHIP,),
            in_specs=[spec(ld, lambda i, ids_ref: (i, ids_ref[1], 0)) for ld in lands]
            + [spec(ld, lambda i, ids_ref: (i, 0, 0)) for ld in lands],
            out_specs=[spec(ld, lambda i, ids_ref: (i, 0, 0)) for ld in lands]),
        out_shape=[jax.ShapeDtypeStruct(ld.shape, ld.dtype) for ld in lands],
        compiler_params=_params(("parallel",)),
    )(ids, *gs, *lands)


def _scatter_copies(parts, lands, send, recv):
    x, y, c = _axes()
    cps = []
    for a, (pt, ld) in enumerate(zip(parts, lands)):
        for k, (px, py) in enumerate(_other_chips(x, y)):
            cps.append(_remote(pt.at[2 * px + py], ld.at[k], send.at[3 * a + k], recv.at[3 * a + k], (px, py, c)))
    return cps


def scatter_start(parts):
    m = len(parts)

    def body(*refs):
        for cp in _scatter_copies(refs[:m], refs[m:2 * m], refs[2 * m], refs[2 * m + 1]):
            cp.start()
        refs[-1][...] = jnp.zeros_like(refs[-1])

    lands = [lax.empty((N_CHIP - 1,) + p.shape[1:], p.dtype) for p in parts]
    res = pl.pallas_call(
        body, name="scatter_start", in_specs=[HBM] * (2 * m), out_specs=[SEM, SEM] + [HBM] * (2 * m) + [VMEM_WHOLE],
        out_shape=[pltpu.SemaphoreType.DMA((3 * m,))] * 2 + [pltpu.HBM(p.shape, p.dtype) for p in parts]
        + [pltpu.HBM(l.shape, l.dtype) for l in lands] + [TOKEN],
        input_output_aliases={i: 2 + i for i in range(2 * m)},
        compiler_params=pltpu.CompilerParams(has_side_effects=EFFECT),
    )(*[_in_hbm(p) for p in parts], *[_in_hbm(l) for l in lands])
    return res[0], res[1], res[2:2 + m], res[2 + m:2 + 2 * m], res[-1]


def scatter_wait(parts, lands, send, recv, after):
    m = len(parts)

    def body(*refs):
        for cp in _scatter_copies(refs[:m], refs[m:2 * m], refs[2 * m], refs[2 * m + 1]):
            cp.wait_send()
            cp.wait_recv()

    res = pl.pallas_call(
        body, name="scatter_wait", in_specs=[HBM] * (2 * m) + [SEM, SEM] + [ANY] * len(after),
        out_specs=[HBM] * (2 * m),
        out_shape=[pltpu.HBM(p.shape, p.dtype) for p in parts] + [pltpu.HBM(l.shape, l.dtype) for l in lands],
        input_output_aliases={i: i for i in range(2 * m)},
        compiler_params=pltpu.CompilerParams(has_side_effects=EFFECT),
    )(*parts, *lands, send, recv, *after)
    return res[:m], res[m:]


def sum_partials(parts, lands, ids, layer, depth, intos):
    m = len(parts)
    nt = STREAM_STEPS

    def body(ids_ref, *refs):
        for p_ref, l_ref, o_ref in zip(refs[:m], refs[m:2 * m], refs[-m:]):
            acc = p_ref[...].astype(F32)
            for k in range(N_CHIP - 1):
                acc = acc + l_ref[k].astype(F32)
            o_ref[...] = acc

    def rows(p):
        return p.shape[1] // nt

    in_specs = [pl.BlockSpec((None, rows(p), p.shape[2]), lambda i, ids_ref: (ids_ref[0], i, 0)) for p in parts]
    in_specs += [pl.BlockSpec((N_CHIP - 1, rows(p), p.shape[2]), lambda i, ids_ref: (0, i, 0)) for p in parts]
    args = [ids, *parts, *lands]
    aliases = {}
    if intos is not None:
        in_specs += [ANY] * m
        args += list(intos)
        aliases = {1 + 2 * m + a: a for a in range(m)}
    return pl.pallas_call(
        body, name="sum_partials",
        grid_spec=pltpu.PrefetchScalarGridSpec(
            num_scalar_prefetch=1, grid=(nt,), in_specs=in_specs,
            out_specs=[pl.BlockSpec((None, rows(p), p.shape[2]), lambda i, ids_ref: (layer, ids_ref[1] * nt + i, 0))
                       for p in parts]),
        out_shape=[jax.ShapeDtypeStruct((depth, 2 * p.shape[1], p.shape[2]), F32) for p in parts],
        input_output_aliases=aliases, compiler_params=_params(("parallel",)),
    )(*args)


def _final_copies(layer):
    def copies(bufs, send, recv):
        x, y, c = _axes()
        sib = (x, y, 1 - c)
        mine, theirs = [], []
        for a, buf in enumerate(bufs):
            rh = buf.shape[1] // 2
            src = buf.at[layer, pl.ds(c * rh, rh), :]
            mine.append(_remote(src, src, send.at[a], recv.at[a], sib))
            dst = buf.at[layer, pl.ds((1 - c) * rh, rh), :]
            theirs.append(_remote(dst, dst, send.at[a], recv.at[a], sib))
        return mine, theirs

    return copies


def allgather_small(pk):
    def body(in_ref, out_ref, send, recv):
        x, y, c = _axes()
        me = 2 * x + y
        chips = _other_chips(x, y)
        out_ref[pl.ds(me, 1)] = in_ref[...][None]
        cps = []
        for k, (px, py) in enumerate(chips):
            cp = _remote(in_ref, out_ref.at[me], send.at[k], recv.at[k], (px, py, c))
            cp.start()
            cps.append(cp)
        for k, (px, py) in enumerate(chips):
            got = out_ref.at[2 * px + py]
            _remote(got, got, send.at[k], recv.at[k], (px, py, c)).wait_recv()
        for cp in cps:
            cp.wait_send()

    return pl.pallas_call(
        body, name="allgather_small", in_specs=[VMEM_WHOLE], out_specs=VMEM_WHOLE,
        out_shape=jax.ShapeDtypeStruct((N_CHIP,) + pk.shape, pk.dtype),
        scratch_shapes=[pltpu.SemaphoreType.DMA((3,))] * 2,
    )(pk)


N_DEV = 8


def _small_copies(bufs, send, recv):
    g, slots = bufs
    x, y, c = _axes()
    me = 4 * x + 2 * y + c
    mine, theirs = [], []
    for mask in range(1, N_DEV):
        px = 1 - x if mask & 4 else x
        py = 1 - y if mask & 2 else y
        pc = 1 - c if mask & 1 else c
        mine.append(_remote(g, slots.at[me], send.at[mask - 1], recv.at[mask - 1], (px, py, pc)))
        got = slots.at[4 * px + 2 * py + pc]
        theirs.append(_remote(got, got, send.at[mask - 1], recv.at[mask - 1], (px, py, pc)))
    return mine, theirs


def sum_slots(g, slots, me):
    def body(me_ref, g_ref, slots_ref, o_ref):
        acc = None
        for d in range(N_DEV):
            term = jnp.where(me_ref[0] == d, g_ref[...], slots_ref[d])
            acc = term if acc is None else acc + term
        o_ref[...] = acc

    return pl.pallas_call(
        body, name="sum_slots",
        grid_spec=pltpu.PrefetchScalarGridSpec(
            num_scalar_prefetch=1, grid=(1,),
            in_specs=[pl.BlockSpec(g.shape, lambda i, me_ref: (0, 0)),
                      pl.BlockSpec(slots.shape, lambda i, me_ref: (0, 0, 0))],
            out_specs=pl.BlockSpec(g.shape, lambda i, me_ref: (0, 0))),
        out_shape=jax.ShapeDtypeStruct(g.shape, g.dtype),
        compiler_params=_params(("arbitrary",)),
    )(me, g, slots)


def adamw(w, g, m, v, layer=None, intos=None):
    shape = w.shape
    cols = shape[-1]
    rows = int(np.prod(shape[:-1]))
    span = rows if layer is None else rows // shape[0]
    tr = span
    for cand in (256, 128):
        if span % cand == 0 and cand * cols * 4 <= 2 * 1024 * 1024:
            tr = cand
            break
    first = 0 if layer is None else layer * (span // tr)
    c1 = 1.0 - ADAM_B1 ** ADAM_STEP
    c2 = 1.0 - ADAM_B2 ** ADAM_STEP

    def body(w_ref, g_ref, m_ref, v_ref, *rest):
        d_ref, nm_ref, nv_ref, g_out = rest[-4:]
        gv = g_ref[...]
        g_out[...] = gv
        nm = ADAM_B1 * m_ref[...] + (1.0 - ADAM_B1) * gv
        nv = ADAM_B2 * v_ref[...] + (1.0 - ADAM_B2) * jnp.square(gv)
        d_ref[...] = -ADAM_LR * ((nm / c1) / (jnp.sqrt(nv / c2) + ADAM_EPS) + ADAM_WD * w_ref[...])
        nm_ref[...] = nm
        nv_ref[...] = nv

    spec = pl.BlockSpec((tr, cols), lambda i: (first + i, 0))
    args = [a.reshape(rows, cols) for a in (w, g, m, v)]
    in_specs, aliases = [spec] * 4, {}
    if intos is not None:
        args += [a.reshape(rows, cols) for a in intos]
        in_specs += [ANY] * 4
        aliases = {4 + k: k for k in range(4)}
    res = pl.pallas_call(
        body, name="adamw", grid=(span // tr,), in_specs=in_specs, out_specs=[spec] * 4,
        out_shape=[jax.ShapeDtypeStruct((rows, cols), F32)] * 4, input_output_aliases=aliases,
        compiler_params=_params(("parallel",)),
    )(*args)
    return [r.reshape(shape) for r in res]


WEIGHTS = ("norm_g", "ffn1_w_gu", "ffn1_w_down", "w_in", "w_ret_o", "sc_conv_w", "w_sc_o", "cf_dw_w", "cf_dw_b",
           "cf_ln_g", "cf_ln_b", "w_cf_o", "w_o", "ffn2_w_gu", "ffn2_w_down")
SHARDED_SMALL = ("norm_g", "sc_conv_w", "cf_dw_w")
REPLICATED_SMALL = ("cf_dw_b", "cf_ln_g", "cf_ln_b")

def _pack_rows(parts):
    padded, offs, at = [], [], 0
    for p in parts:
        r = -(-p.shape[0] // SUBLANES) * SUBLANES
        padded.append(jnp.pad(p, ((0, r - p.shape[0]), (0, 0))))
        offs.append(at)
        at += r
    return jnp.concatenate(padded, axis=0), offs


def kernel(x, positions, norm_g, ffn1_w_gu, ffn1_w_down, w_in, w_ret_o, sc_conv_w, w_sc_o, cf_dw_w, cf_dw_b, cf_ln_g, cf_ln_b, w_cf_o, w_o, ffn2_w_gu, ffn2_w_down, loss_target, m_norm_g, m_ffn1_w_gu, m_ffn1_w_down, m_w_in, m_w_ret_o, m_sc_conv_w, m_w_sc_o, m_cf_dw_w, m_cf_dw_b, m_cf_ln_g, m_cf_ln_b, m_w_cf_o, m_w_o, m_ffn2_w_gu, m_ffn2_w_down, v_norm_g, v_ffn1_w_gu, v_ffn1_w_down, v_w_in, v_w_ret_o, v_sc_conv_w, v_w_sc_o, v_cf_dw_w, v_cf_dw_b, v_cf_ln_g, v_cf_ln_b, v_w_cf_o, v_w_o, v_ffn2_w_gu, v_ffn2_w_down):
    wts = dict(zip(WEIGHTS, (norm_g, ffn1_w_gu, ffn1_w_down, w_in, w_ret_o, sc_conv_w, w_sc_o, cf_dw_w, cf_dw_b,
                             cf_ln_g, cf_ln_b, w_cf_o, w_o, ffn2_w_gu, ffn2_w_down)))
    mom = dict(zip(WEIGHTS, (m_norm_g, m_ffn1_w_gu, m_ffn1_w_down, m_w_in, m_w_ret_o, m_sc_conv_w, m_w_sc_o,
                             m_cf_dw_w, m_cf_dw_b, m_cf_ln_g, m_cf_ln_b, m_w_cf_o, m_w_o, m_ffn2_w_gu, m_ffn2_w_down)))
    var = dict(zip(WEIGHTS, (v_norm_g, v_ffn1_w_gu, v_ffn1_w_down, v_w_in, v_w_ret_o, v_sc_conv_w, v_w_sc_o,
                             v_cf_dw_w, v_cf_dw_b, v_cf_ln_g, v_cf_ln_b, v_w_cf_o, v_w_o, v_ffn2_w_gu, v_ffn2_w_down)))
    depth = norm_g.shape[0]
    dq = norm_g.shape[-1]
    d = N_CHIP * dq
    chip = 2 * lax.axis_index("x") + lax.axis_index("y")
    ids = jnp.stack([chip, lax.axis_index("c")]).astype(jnp.int32)

    pk, offs = _pack_rows([wts[n].reshape(-1, dq) for n in SHARDED_SMALL])
    gk4 = allgather_small(pk)
    gk = gk4.transpose(1, 0, 2).reshape(pk.shape[0], d)
    small = {n: wts[n] for n in REPLICATED_SMALL}
    for n, o in zip(SHARDED_SMALL, offs):
        rows = wts[n].shape[0] * wts[n].shape[1]
        small[n] = gk[o:o + rows].reshape(wts[n].shape[:2] + (d,))

    order = [(l, blk) for l in range(depth) for blk in BLOCKS]
    def placed(groups, after):
        return [place_quarters([wts[n] for n in BLOCK_WEIGHTS[blk]], l, ids, after) for l, blk in groups]

    first, token = gather_start("gather_start_first", placed(order[:1], gk4), gk4)
    rest, token = gather_start("gather_start_rest", placed(order[1:], token), token)
    started = dict(zip(order, first + rest))
    small["norm_g"] = small["norm_g"] + token[0:1, 0:1]

    filling = {}

    def fill(group, after):
        send, recv, lands = started[group]
        lands = gather_wait(lands, send, recv, after)
        send, recv, lands, tok = copy_start("fill_start", lands, _fill_copies, 3 * len(lands))
        filling[group] = (send, recv, lands)
        return tok[0:1, 0:1]

    def fetch(l, blk, after):
        at = order.index((l, blk))
        if (l, blk) not in filling:
            fill((l, blk), token if at == 0 else after)
        send, recv, lands = filling.pop((l, blk))
        lands = copy_wait("fill_wait", lands, send, recv, _fill_copies, (after,))
        tok, mid = None, None
        if at == 1:
            mid = functools.partial(fill, order[at + 1])
        elif 1 < at < len(order) - 1:
            tok = fill(order[at + 1], lands[0])
        return dict(zip(BLOCK_WEIGHTS[blk], lands)), tok, mid

    gsum = {n: None for n in BIG}
    presums, scatters, finals = [], [], []

    def scatter_next(after):
        group, gl, lands, send, recv = presums.pop(0)
        gl, lands = presum_wait(gl, lands, send, recv, after)
        send, recv, parts, lands, tok = scatter_start(add_halves(gl, lands, ids))
        scatters.append((group, parts, lands, send, recv))
        return tok

    def sum_next(after):
        (l, blk), parts, lands, send, recv = scatters.pop(0)
        parts, lands = scatter_wait(parts, lands, send, recv, after)
        names = BLOCK_WEIGHTS[blk]
        intos = None if gsum[names[0]] is None else [gsum[n] for n in names]
        sums = sum_partials(parts, lands, ids, l, depth, intos)
        send, recv, sums, tok = copy_start("final_start", sums, _final_copies(l), len(sums))
        gsum.update(zip(names, sums))
        finals.append((names, l, send, recv))
        return tok

    def final_next(after):
        names, l, send, recv = finals.pop(0)
        gsum.update(zip(names, copy_wait("final_wait", [gsum[n] for n in names], send, recv, _final_copies(l), after)))

    def push(l, blk, grads):
        send, recv, gl, lands, tok = presum_start([grads[n] for n in BLOCK_WEIGHTS[blk]])
        if scatters:
            tok = tok + sum_next((gl[0],))
        if presums:
            tok = tok + scatter_next((gl[0],))
        presums.append(((l, blk), gl, lands, send, recv))
        return tok[0:1, 0:1]

    loss, grad_x, gs = local_step(x, positions, loss_target, small, fetch, push)

    names = SHARDED_SMALL + REPLICATED_SMALL
    pg, offs = _pack_rows([gs[n].reshape(-1, d) for n in names])
    s_send, s_recv, s_bufs, tok = copy_start("small_start", [pg, lax.empty((N_DEV,) + pg.shape, pg.dtype)],
                                             _small_copies, N_DEV - 1, (grad_x,))
    tok = scatter_next((grad_x, tok))

    delta, new_m, new_v, grads = {}, {}, {}, {}

    def update(n, layer=None):
        g = gsum[n] if n in BIG else grads[n]
        prev = [delta[n], new_m[n], new_v[n], grads[n]] if layer is not None and n in delta else None
        delta[n], new_m[n], new_v[n], grads[n] = adamw(wts[n], g, mom[n], var[n], layer, prev)

    while finals and finals[0][1] > 0:
        done, l = finals[0][:2]
        final_next((tok,))
        for n in done:
            update(n, l)
    upper = tuple(delta[n] for n in BIG if n in delta)
    pg, slots = copy_wait("small_wait", s_bufs, s_send, s_recv, _small_copies, upper + (tok,))
    me = (2 * chip + lax.axis_index("c")).astype(jnp.int32).reshape(1)
    tot = sum_slots(pg, slots, me)
    for n, o in zip(names, offs):
        rows = int(np.prod(gs[n].shape[:-1]))
        full = tot[o:o + rows]
        if n in SHARDED_SMALL:
            full = lax.dynamic_slice_in_dim(full, chip * dq, dq, axis=1)
        grads[n] = full.reshape(wts[n].shape)

    for n in names:
        update(n)
    after = tuple(delta[n] for n in names)
    while scatters or finals:
        if scatters:
            after = (sum_next(after),)
        done, l = finals[0][:2]
        final_next(after)
        for n in done:
            update(n, l)
        after = tuple(delta[n] for n in done)

    loss_all = lax.psum(loss[0, 0], ("x", "y", "c"))
    return (loss_all, grad_x, *[grads[n] for n in WEIGHTS], *[delta[n] for n in WEIGHTS],
            *[new_m[n] for n in WEIGHTS], *[new_v[n] for n in WEIGHTS])
```

```python
import functools

import jax
import jax.numpy as jnp
import numpy as np
from jax import lax
from jax.experimental import pallas as pl
from jax.experimental.pallas import tpu as pltpu

F32 = jnp.float32
BF16 = jnp.bfloat16
MXU_DTYPE = BF16
VMEM_LIMIT_BYTES = 56 * 1024 * 1024
MESH = pl.DeviceIdType.MESH

N_CHIP = 4
CHUNK = 64
RET_HEADS = 4
RET_QK_DIM = 128
RET_V_DIM = 256
SC_KERNEL = 3
CF_KERNEL = 31
ROPE_BASE = 10000.0
NORM_EPS = 1e-6
LN_EPS = 1e-5
ADAM_LR = 0.001
ADAM_B1 = 0.9
ADAM_B2 = 0.999
ADAM_EPS = 1e-08
ADAM_WD = 0.01
ADAM_STEP = 10

SUBLANES = 8
CONV_PAD = 32
CONV_TS = 128
CONV_TC = 512
CONV_ROWS = 32
CONV_TILES = range(0, CONV_ROWS, SUBLANES)
SC_TS = 512


def _conv_scratch(ts):
    return [pltpu.VMEM((ts + CONV_PAD, CONV_TC), F32),
            pltpu.VMEM((SUBLANES - 1, ts + CONV_PAD - SUBLANES, CONV_TC), F32)]
RET_TQ = 512
MM_TM = 1024
MM_TN = 1536
MM_K1 = 1024
MM_W1 = 8 << 20
MM_SLICE = 256
MM_IN_BYTES = 36 << 20
STREAM_STEPS = 2


def _params(sem):
    return pltpu.CompilerParams(dimension_semantics=sem, vmem_limit_bytes=VMEM_LIMIT_BYTES)


def _axes():
    return lax.axis_index("x"), lax.axis_index("y"), lax.axis_index("c")


NN = (((1,), (0,)), ((), ()))
NT = (((1,), (1,)), ((), ()))
TN = (((0,), (0,)), ((), ()))


def _mm(name, a, b, out_shape, out_dtype, grid, a_spec, b_spec, o_spec, dims, acc_shape):
    nk = grid[2]

    def body(a_ref, b_ref, o_ref, *scratch):
        bv = b_ref[...]
        if bv.ndim == 3:
            bv = bv.reshape(-1, bv.shape[-1])
        part = lax.dot_general(a_ref[...], bv, dims, preferred_element_type=F32)

        def put(v):
            o_ref[...] = v.reshape(o_ref.shape).astype(o_ref.dtype)

        if nk == 1:
            put(part)
        else:
            acc = scratch[0]
            k = pl.program_id(2)

            @pl.when(k == 0)
            def _():
                acc[...] = part

            @pl.when(k > 0)
            def _():
                acc[...] += part

            @pl.when(k == nk - 1)
            def _():
                put(acc[...])

    scratch = [pltpu.VMEM(acc_shape, F32)] if nk > 1 else []
    return pl.pallas_call(
        body, name=name, grid=grid, in_specs=[a_spec, b_spec], out_specs=o_spec,
        out_shape=jax.ShapeDtypeStruct(out_shape, out_dtype), scratch_shapes=scratch,
        compiler_params=_params(("parallel", "parallel", "arbitrary")),
    )(a, b)


def _tile(n, target):
    best = None
    for t in range(128, min(n, target) + 1, 128):
        if n % t == 0:
            best = t
    assert best is not None, (n, target)
    return best


def _token_rows(t, width):
    tt = t
    while tt > MM_TM and tt * width * jnp.dtype(MXU_DTYPE).itemsize * 2 > MM_IN_BYTES:
        tt //= 2
    return tt


def mm_fwd(name, a, w4, mode, out_dtype):
    t = a.shape[0]
    _, r, c = w4.shape
    tm = min(t, MM_TM)
    if mode == "col":
        tn = _tile(c, MM_TN)
        npj = c // tn
        grid = (t // tm, N_CHIP * npj, 1)
        a_spec = pl.BlockSpec((tm, r), lambda i, j, k: (i, 0))
        b_spec = pl.BlockSpec((None, r, tn), lambda i, j, k: (j // npj, 0, j % npj))
        o_spec = pl.BlockSpec((tm, tn), lambda i, j, k: (i, j))
        return _mm(name, a, w4, (t, N_CHIP * c), out_dtype, grid, a_spec, b_spec, o_spec, NN, (tm, tn))
    if w4.size * w4.dtype.itemsize <= MM_W1:
        grid = (t // tm, 1, 1)
        a_spec = pl.BlockSpec((tm, N_CHIP * r), lambda i, j, k: (i, 0))
        b_spec = pl.BlockSpec((N_CHIP, r, c), lambda i, j, k: (0, 0, 0))
        o_spec = pl.BlockSpec((tm, c), lambda i, j, k: (i, 0))
        return _mm(name, a, w4, (t, c), out_dtype, grid, a_spec, b_spec, o_spec, NN, (tm, c))
    grid = (t // tm, 1, N_CHIP)
    a_spec = pl.BlockSpec((tm, r), lambda i, j, k: (i, k))
    b_spec = pl.BlockSpec((None, r, c), lambda i, j, k: (k, 0, 0))
    o_spec = pl.BlockSpec((tm, c), lambda i, j, k: (i, 0))
    return _mm(name, a, w4, (t, c), out_dtype, grid, a_spec, b_spec, o_spec, NN, (tm, c))


def mm_dx(name, dy, w4, mode, out_dtype):
    t = dy.shape[-2]
    _, r, c = w4.shape
    tm = min(t, MM_TM)
    if mode == "col":
        tn, npj = c, 1
        hb = N_CHIP // 2 * npj
        grid = (t // tm, 1, N_CHIP * npj)
        if dy.ndim == 3:
            a_spec = pl.BlockSpec((None, tm, tn), lambda i, j, k: (k // hb, i, k % hb))
        else:
            a_spec = pl.BlockSpec((tm, tn), lambda i, j, k: (i, k))
        b_spec = pl.BlockSpec((None, r, tn), lambda i, j, k: (k // npj, 0, k % npj))
        o_spec = pl.BlockSpec((tm, r), lambda i, j, k: (i, 0))
        return _mm(name, dy, w4, (t, r), out_dtype, grid, a_spec, b_spec, o_spec, NT, (tm, r))
    if N_CHIP * r <= MM_K1:
        grid = (t // tm, 1, 1)
        a_spec = pl.BlockSpec((tm, c), lambda i, j, k: (i, 0))
        b_spec = pl.BlockSpec((N_CHIP, r, c), lambda i, j, k: (0, 0, 0))
        o_spec = pl.BlockSpec((tm, N_CHIP * r), lambda i, j, k: (i, 0))
        return _mm(name, dy, w4, (t, N_CHIP * r), out_dtype, grid, a_spec, b_spec, o_spec, NT, (tm, N_CHIP * r))
    grid = (t // tm, N_CHIP, 1)
    a_spec = pl.BlockSpec((tm, c), lambda i, j, k: (i, 0))
    b_spec = pl.BlockSpec((None, r, c), lambda i, j, k: (j, 0, 0))
    o_spec = pl.BlockSpec((tm, r), lambda i, j, k: (i, j))
    return _mm(name, dy, w4, (t, N_CHIP * r), out_dtype, grid, a_spec, b_spec, o_spec, NT, (tm, r))


def mm_dw(name, a, dy, mode, shape3):
    t = a.shape[0]
    _, r, c = shape3
    if mode == "col":
        tn = _tile(c, MM_TN)
        npj = c // tn
        tt = _token_rows(t, r + tn)
        grid = (1, N_CHIP * npj, t // tt)
        a_spec = pl.BlockSpec((tt, r), lambda i, j, k: (k, 0))
        hb = N_CHIP // 2 * npj
        if dy.ndim == 3:
            b_spec = pl.BlockSpec((None, tt, tn), lambda i, j, k: (j // hb, k, j % hb))
        else:
            b_spec = pl.BlockSpec((tt, tn), lambda i, j, k: (k, j))
        o_spec = pl.BlockSpec((None, r, tn), lambda i, j, k: (j // npj, 0, j % npj))
        return _mm(name, a, dy, shape3, MXU_DTYPE, grid, a_spec, b_spec, o_spec, TN, (r, tn))
    if N_CHIP * r <= MM_K1:
        tt = _token_rows(t, N_CHIP * r + c)
        grid = (1, 1, t // tt)
        a_spec = pl.BlockSpec((tt, N_CHIP * r), lambda i, j, k: (k, 0))
        b_spec = pl.BlockSpec((tt, c), lambda i, j, k: (k, 0))
        o_spec = pl.BlockSpec((N_CHIP, r, c), lambda i, j, k: (0, 0, 0))
        return _mm(name, a, dy, shape3, MXU_DTYPE, grid, a_spec, b_spec, o_spec, TN, (N_CHIP * r, c))
    tt = _token_rows(t, r + c)
    grid = (N_CHIP, 1, t // tt)
    a_spec = pl.BlockSpec((tt, r), lambda i, j, k: (k, i))
    b_spec = pl.BlockSpec((tt, c), lambda i, j, k: (k, 0))
    o_spec = pl.BlockSpec((None, r, c), lambda i, j, k: (i, 0, 0))
    return _mm(name, a, dy, shape3, MXU_DTYPE, grid, a_spec, b_spec, o_spec, TN, (r, c))


def _rms_bwd(x, g, dh):
    r = lax.rsqrt(jnp.mean(x * x, axis=-1, keepdims=True) + NORM_EPS)
    xhat = x * r
    dyg = dh * g
    dx = r * (dyg - xhat * jnp.mean(dyg * xhat, axis=-1, keepdims=True))
    return dx, jnp.sum(dh * xhat, axis=0, keepdims=True)


def mm_dx_norms(name, dy, w4, x, g_pre, dres, prev, after):
    t = dy.shape[-2]
    _, r, c = w4.shape
    tm = min(t, MM_TM // 2)
    nt, nk = t // tm, N_CHIP
    hb = N_CHIP // 2
    chained = prev is not None

    def body(dy_ref, w_ref, x_ref, dres_ref, g_ref, *rest):
        rest = rest[1:] if after is not None else rest
        if chained:
            y_ref, gp_ref, dx_ref, dg_ref, dyp_ref, dgp_ref, acc = rest
        else:
            dx_ref, dg_ref, acc = rest
        i, k = pl.program_id(0), pl.program_id(1)
        part = lax.dot_general(dy_ref[...], w_ref[...], NT, preferred_element_type=F32)

        @pl.when(k == 0)
        def _():
            acc[...] = part

        @pl.when(k > 0)
        def _():
            acc[...] += part

        def add_to(ref, v):
            @pl.when(i == 0)
            def _():
                ref[...] = v

            @pl.when(i > 0)
            def _():
                ref[...] += v

        @pl.when(k == nk - 1)
        def _():
            dx, dg = _rms_bwd(x_ref[...], g_ref[...], acc[...])
            dxs = dres_ref[...] + dx
            dx_ref[...] = dxs
            add_to(dg_ref, dg)
            if chained:
                dyp, dgp = _rms_bwd(y_ref[...], gp_ref[...], dxs)
                dyp_ref[...] = (prev[2] * dyp).astype(dyp_ref.dtype)
                add_to(dgp_ref, prev[2] * dgp)

    if dy.ndim == 3:
        dy_spec = pl.BlockSpec((None, tm, c), lambda i, k: (k // hb, i, k % hb))
    else:
        dy_spec = pl.BlockSpec((tm, c), lambda i, k: (i, k))
    rows = pl.BlockSpec((tm, r), lambda i, k: (i, 0))
    gain = pl.BlockSpec((1, r), lambda i, k: (0, 0))
    in_specs = [dy_spec, pl.BlockSpec((None, r, c), lambda i, k: (k, 0, 0)), rows, rows, gain]
    args = [dy, w4, x, dres, g_pre]
    if after is not None:
        in_specs.append(pl.BlockSpec(memory_space=pl.ANY))
        args.append(after)
    out_specs = [rows, gain]
    out_shape = [jax.ShapeDtypeStruct((t, r), F32), jax.ShapeDtypeStruct((1, r), F32)]
    if chained:
        in_specs += [rows, gain]
        args += [prev[0], prev[1]]
        out_specs += [rows, gain]
        out_shape += [jax.ShapeDtypeStruct((t, r), MXU_DTYPE), jax.ShapeDtypeStruct((1, r), F32)]
    res = pl.pallas_call(
        body, name=name, grid=(nt, nk), in_specs=in_specs, out_specs=out_specs, out_shape=out_shape,
        scratch_shapes=[pltpu.VMEM((tm, r), F32)], compiler_params=_params(("arbitrary", "arbitrary")),
    )(*args)
    return tuple(res) if chained else (res[0], res[1], None, None)


def _rowwise(name, fn, rows, pars, outs, accs=(), tm=256, ncol=1):
    t = rows[0][0].shape[0]
    nrow, npar, nout = len(rows), len(pars), len(outs)

    def body(*refs):
        vals = [r[...] for r in refs[:nrow + npar]]
        res = fn(*vals)
        out_refs = refs[nrow + npar:nrow + npar + nout]
        acc_refs = refs[nrow + npar + nout:]
        for o, v in zip(out_refs, res[:nout]):
            o[...] = v.astype(o.dtype)
        i = pl.program_id(1)
        for a, v in zip(acc_refs, res[nout:]):
            @pl.when(i == 0)
            def _(a=a, v=v):
                a[...] = v.astype(F32)

            @pl.when(i > 0)
            def _(a=a, v=v):
                a[...] += v.astype(F32)

    in_specs = [pl.BlockSpec((tm, w), functools.partial(lambda j, i, b: (i, b + j), b=b)) for _, w, b in rows]
    for arr, w in pars:
        if w is None:
            in_specs.append(pl.BlockSpec(arr.shape, lambda j, i: (0, 0)))
        else:
            in_specs.append(pl.BlockSpec((1, w), lambda j, i: (0, j)))
    out_specs = [pl.BlockSpec((tm, w), lambda j, i: (i, j)) for _, w, _ in outs]
    out_specs += [pl.BlockSpec((1, w), lambda j, i: (0, j)) for _, w in accs]
    out_shape = [jax.ShapeDtypeStruct((t, tw), dt) for tw, _, dt in outs]
    out_shape += [jax.ShapeDtypeStruct((1, tw), F32) for tw, _ in accs]
    res = pl.pallas_call(
        body, name=name, grid=(ncol, t // tm), in_specs=in_specs, out_specs=out_specs, out_shape=out_shape,
        compiler_params=_params(("parallel", "arbitrary" if accs else "parallel")),
    )(*[r[0] for r in rows], *[p[0] for p in pars])
    return res


def _rms(x, g):
    xf = x.astype(F32)
    return xf * lax.rsqrt(jnp.mean(xf * xf, axis=-1, keepdims=True) + NORM_EPS) * g


def _silu(x):
    return x * jax.nn.sigmoid(x)


def rms_fwd(name, x, g):
    d = x.shape[1]
    return _rowwise(name, lambda x, g: (_rms(x, g),), [(x, d, 0)], [(g, None)], [(d, d, MXU_DTYPE)], tm=512)[0]


def rms_bwd(name, x, g, dh, dres):
    d = x.shape[1]

    def fn(x, dh, dres, g):
        _, vjp = jax.vjp(_rms, x, g)
        dx, dg = vjp(dh.astype(F32))
        return dres + dx, dg

    return _rowwise(name, fn, [(x, d, 0), (dh, d, 0), (dres, d, 0)], [(g, None)], [(d, d, F32)], [(d, d)], tm=256)


def mm_post(name, a, w4, x, g_post, scale, g_next):
    t = a.shape[0]
    _, r, c = w4.shape
    tm = min(t, MM_TM // 2)
    chained = g_next is not None

    def body(a_ref, w_ref, x_ref, gp_ref, *rest):
        gn_ref, y_ref, xn_ref, h_ref = rest if chained else (None,) + rest + (None,)
        y = lax.dot_general(a_ref[...], w_ref[...].reshape(N_CHIP * r, c), NN, preferred_element_type=F32)
        y_ref[...] = y
        xn = x_ref[...] + scale * _rms(y, gp_ref[...])
        xn_ref[...] = xn
        if chained:
            h_ref[...] = _rms(xn, gn_ref[...]).astype(h_ref.dtype)

    def rows(width):
        return pl.BlockSpec((tm, width), lambda i: (i, 0))

    gain = pl.BlockSpec((1, c), lambda i: (0, 0))
    in_specs = [rows(N_CHIP * r), pl.BlockSpec((N_CHIP, r, c), lambda i: (0, 0, 0)), rows(c), gain]
    args = [a, w4, x, g_post]
    out_specs, out_shape = [rows(c), rows(c)], [jax.ShapeDtypeStruct((t, c), F32)] * 2
    if chained:
        in_specs.append(gain)
        args.append(g_next)
        out_specs.append(rows(c))
        out_shape.append(jax.ShapeDtypeStruct((t, c), MXU_DTYPE))
    res = pl.pallas_call(
        body, name=name, grid=(t // tm,), in_specs=in_specs, out_specs=out_specs, out_shape=out_shape,
        compiler_params=_params(("parallel",)),
    )(*args)
    return res[0], res[1], (res[2] if chained else None)


def post_bwd(name, y, g, dx, scale):
    d = y.shape[1]

    def fn(y, dx, g):
        _, vjp = jax.vjp(lambda y, g: scale * _rms(y, g), y, g)
        return vjp(dx)

    return _rowwise(name, fn, [(y, d, 0), (dx, d, 0)], [(g, None)], [(d, d, MXU_DTYPE)], [(d, d)], tm=256)


def ffn_up(name, h, w4):
    t = h.shape[0]
    _, r, c = w4.shape
    tm = min(t, MM_TM)
    tn = _tile(c, MM_TM)
    npj = c // tn
    half = N_CHIP // 2

    def body(h_ref, wg_ref, wu_ref, gu_ref, a_ref):
        hv = h_ref[...]
        g = lax.dot_general(hv, wg_ref[...], NN, preferred_element_type=F32)
        u = lax.dot_general(hv, wu_ref[...], NN, preferred_element_type=F32)
        gu_ref[0] = g.astype(gu_ref.dtype)
        gu_ref[1] = u.astype(gu_ref.dtype)
        a_ref[...] = (_silu(g) * u).astype(a_ref.dtype)

    f = half * c
    return pl.pallas_call(
        body, name=name, grid=(t // tm, half * npj),
        in_specs=[pl.BlockSpec((tm, r), lambda i, j: (i, 0)),
                  pl.BlockSpec((None, r, tn), lambda i, j: (j // npj, 0, j % npj)),
                  pl.BlockSpec((None, r, tn), lambda i, j: (half + j // npj, 0, j % npj))],
        out_specs=[pl.BlockSpec((2, tm, tn), lambda i, j: (0, i, j)), pl.BlockSpec((tm, tn), lambda i, j: (i, j))],
        out_shape=[jax.ShapeDtypeStruct((2, t, f), MXU_DTYPE), jax.ShapeDtypeStruct((t, f), MXU_DTYPE)],
        compiler_params=_params(("parallel", "parallel")),
    )(h, w4, w4)


def ffn_down_dx(name, dy, w4, gu):
    t = dy.shape[0]
    _, r, c = w4.shape
    tm = min(t, MM_TM)

    def body(dy_ref, w_ref, gu_ref, o_ref):
        dyv = dy_ref[...]
        for n0 in range(0, r, MM_SLICE):
            cols = pl.ds(n0, MM_SLICE)
            da = lax.dot_general(dyv, w_ref[cols, :], NT, preferred_element_type=F32)
            gate, up = gu_ref[0, :, cols].astype(F32), gu_ref[1, :, cols].astype(F32)
            sg = jax.nn.sigmoid(gate)
            silu = gate * sg
            o_ref[0, :, cols] = (da * up * (sg + silu * (1.0 - sg))).astype(o_ref.dtype)
            o_ref[1, :, cols] = (da * silu).astype(o_ref.dtype)

    return pl.pallas_call(
        body, name=name, grid=(t // tm, N_CHIP),
        in_specs=[pl.BlockSpec((tm, c), lambda i, j: (i, 0)), pl.BlockSpec((None, r, c), lambda i, j: (j, 0, 0)),
                  pl.BlockSpec((2, tm, r), lambda i, j: (0, i, j))],
        out_specs=pl.BlockSpec((2, tm, r), lambda i, j: (0, i, j)),
        out_shape=jax.ShapeDtypeStruct((2, t, N_CHIP * r), MXU_DTYPE),
        compiler_params=_params(("parallel", "parallel")),
    )(dy, w4, gu)


def _head_gate(o, g):
    mu = jnp.mean(o, axis=-1, keepdims=True)
    var = jnp.mean(jnp.square(o - mu), axis=-1, keepdims=True)
    return _silu(g.astype(F32)) * ((o - mu) * lax.rsqrt(var + LN_EPS))


def head_gate_fwd(name, o, p, gate_blk):
    dv = RET_V_DIM
    return _rowwise(name, lambda o, g: (_head_gate(o, g),), [(o, dv, 0), (p, dv, gate_blk)], [],
                    [(RET_HEADS * dv, dv, MXU_DTYPE)], tm=512, ncol=RET_HEADS)[0]


def head_gate_bwd(name, o, p, gate_blk, da):
    dv = RET_V_DIM

    def fn(o, g, da):
        _, vjp = jax.vjp(_head_gate, o, g.astype(F32))
        return vjp(da.astype(F32))

    w = RET_HEADS * dv
    return _rowwise(name, fn, [(o, dv, 0), (p, dv, gate_blk), (da, dv, 0)], [],
                    [(w, dv, MXU_DTYPE), (w, dv, MXU_DTYPE)], tm=512, ncol=RET_HEADS)


def _ln_silu(u, g, b):
    mu = jnp.mean(u, axis=-1, keepdims=True)
    var = jnp.mean(jnp.square(u - mu), axis=-1, keepdims=True)
    return _silu((u - mu) * lax.rsqrt(var + LN_EPS) * g + b)


def ln_silu_fwd(name, u, g, b):
    d = u.shape[1]
    return _rowwise(name, lambda u, g, b: (_ln_silu(u, g, b),), [(u, d, 0)], [(g, None), (b, None)],
                    [(d, d, MXU_DTYPE)], tm=512)[0]


def ln_silu_bwd(name, u, g, b, dc):
    d = u.shape[1]

    def fn(u, dc, g, b):
        _, vjp = jax.vjp(_ln_silu, u, g, b)
        return vjp(dc.astype(F32))

    return _rowwise(name, fn, [(u, d, 0), (dc, d, 0)], [(g, None), (b, None)], [(d, d, F32)], [(d, d), (d, d)],
                    tm=256)


def _merge(g0, g1, g2, ya, yb, yc):
    s = jax.nn.sigmoid
    return s(g0.astype(F32)) * ya + s(g1.astype(F32)) * yb + s(g2.astype(F32)) * yc


def merge_fwd(name, p, blk, ya, yb, yc):
    d = ya.shape[1]
    rows = [(p, d, blk), (p, d, blk + 1), (p, d, blk + 2), (ya, d, 0), (yb, d, 0), (yc, d, 0)]
    return _rowwise(name, lambda *v: (_merge(*v),), rows, [], [(d, d, MXU_DTYPE)], tm=256)[0]


def merge_bwd(name, p, blk, ya, yb, yc, dmg):
    d = ya.shape[1]

    def fn(g0, g1, g2, ya, yb, yc, dmg):
        _, vjp = jax.vjp(_merge, g0.astype(F32), g1.astype(F32), g2.astype(F32), ya, yb, yc)
        return vjp(dmg.astype(F32))

    rows = [(p, d, blk), (p, d, blk + 1), (p, d, blk + 2), (ya, d, 0), (yb, d, 0), (yc, d, 0), (dmg, d, 0)]
    return _rowwise(name, fn, rows, [], [(d, d, MXU_DTYPE)] * 6, tm=256)


def concat_cols(name, pieces):
    t = pieces[0].shape[0]
    widths = [p.shape[1] for p in pieces]
    tm = 256

    def body(*refs):
        o_ref, at = refs[-1], 0
        for r, w in zip(refs[:-1], widths):
            o_ref[:, at:at + w] = r[...]
            at += w

    return pl.pallas_call(
        body, name=name, grid=(t // tm,),
        in_specs=[pl.BlockSpec((tm, w), lambda i: (i, 0)) for w in widths],
        out_specs=pl.BlockSpec((tm, sum(widths)), lambda i: (i, 0)),
        out_shape=jax.ShapeDtypeStruct((t, sum(widths)), pieces[0].dtype),
        compiler_params=_params(("parallel",)),
    )(*pieces)


def loss_head(name, y, target):
    t, d = y.shape
    tm = 512

    def body(y_ref, t_ref, dy_ref, loss_ref):
        err = y_ref[...] - t_ref[...]
        dy_ref[...] = err * (1.0 / d)
        part = jnp.sum(jnp.sum(err * err, axis=1, keepdims=True), axis=0, keepdims=True) * (0.5 / d)

        @pl.when(pl.program_id(0) == 0)
        def _():
            loss_ref[...] = part

        @pl.when(pl.program_id(0) > 0)
        def _():
            loss_ref[...] += part

    return pl.pallas_call(
        body, name=name, grid=(t // tm,),
        in_specs=[pl.BlockSpec((tm, d), lambda i: (i, 0))] * 2,
        out_specs=[pl.BlockSpec((tm, d), lambda i: (i, 0)), pl.BlockSpec((1, 1), lambda i: (0, 0))],
        out_shape=[jax.ShapeDtypeStruct((t, d), F32), jax.ShapeDtypeStruct((1, 1), F32)],
        compiler_params=_params(("arbitrary",)),
    )(y, target)


def _rot(x, cos2, sin2):
    return x * cos2 + pltpu.roll(x, RET_QK_DIM // 2, 1) * sin2


def _decay_mask(lg, n0, rows, cols):
    n = n0 + lax.broadcasted_iota(jnp.int32, (rows, cols), 0)
    m = lax.broadcasted_iota(jnp.int32, (rows, cols), 1)
    shift = CHUNK.bit_length() - 1
    dist = jnp.abs(n - m).astype(F32)
    return jnp.where((m >> shift) <= (n >> shift), jnp.exp(lg * dist), 0.0)


def _ret_specs(s):
    dk, dv, h = RET_QK_DIM, RET_V_DIM, RET_HEADS
    return [
        pl.BlockSpec((s, dk), lambda b, hh: (b, hh)),
        pl.BlockSpec((s, dk), lambda b, hh: (b, h + hh)),
        pl.BlockSpec((s, dv), lambda b, hh: (b, (2 * h * dk) // dv + hh)),
        pl.BlockSpec((s, dk), lambda b, hh: (b, 0)),
        pl.BlockSpec((s, dk), lambda b, hh: (b, 0)),
        pl.BlockSpec((None, 1, dk), lambda b, hh: (hh, 0, 0)),
    ]


def retention_fwd(name, p, cos2, sin2, log_g, nb, s):
    dk, dv, h = RET_QK_DIM, RET_V_DIM, RET_HEADS

    def body(q_ref, k_ref, v_ref, cos_ref, sin_ref, lg_ref, o_ref, kr_ref):
        lg = lg_ref[0:1, 0:1]
        kr = _rot(k_ref[...].astype(F32), cos_ref[...], sin_ref[...]) * (dk ** -0.5)
        kr_ref[...] = kr.astype(kr_ref.dtype)
        for qi in range(s // RET_TQ):
            n0, kmax = qi * RET_TQ, (qi + 1) * RET_TQ
            rows = pl.ds(n0, RET_TQ)
            qr = _rot(q_ref[rows, :].astype(F32), cos_ref[rows, :], sin_ref[rows, :]).astype(MXU_DTYPE)
            sc = lax.dot_general(qr, kr_ref[0:kmax, :], NT, preferred_element_type=F32)
            pm = (sc * _decay_mask(lg, n0, RET_TQ, kmax)).astype(MXU_DTYPE)
            o_ref[rows, :] = lax.dot_general(pm, v_ref[0:kmax, :], NN, preferred_element_type=F32)

    return pl.pallas_call(
        body, name=name, grid=(nb, h), in_specs=_ret_specs(s),
        out_specs=pl.BlockSpec((s, dv), lambda b, hh: (b, hh)),
        out_shape=jax.ShapeDtypeStruct((nb * s, h * dv), F32),
        scratch_shapes=[pltpu.VMEM((s, dk), MXU_DTYPE)],
        compiler_params=_params(("parallel", "parallel")),
    )(p, p, p, cos2, sin2, log_g)


def retention_bwd(name, p, cos2, sin2, log_g, do, nb, s):
    dk, dv, h = RET_QK_DIM, RET_V_DIM, RET_HEADS

    def body(q_ref, k_ref, v_ref, cos_ref, sin_ref, lg_ref, do_ref, dq_ref, dk_ref, dv_ref, kr_ref, dk_acc, dv_acc):
        lg = lg_ref[0:1, 0:1]
        kr = _rot(k_ref[...].astype(F32), cos_ref[...], sin_ref[...]) * (dk ** -0.5)
        kr_ref[...] = kr.astype(kr_ref.dtype)
        dk_acc[...] = jnp.zeros_like(dk_acc)
        dv_acc[...] = jnp.zeros_like(dv_acc)
        for qi in range(s // RET_TQ):
            n0, kmax = qi * RET_TQ, (qi + 1) * RET_TQ
            rows = pl.ds(n0, RET_TQ)
            cq, sq = cos_ref[rows, :], sin_ref[rows, :]
            qr = _rot(q_ref[rows, :].astype(F32), cq, sq).astype(MXU_DTYPE)
            dob = do_ref[rows, :]
            mask = _decay_mask(lg, n0, RET_TQ, kmax)
            sc = lax.dot_general(qr, kr_ref[0:kmax, :], NT, preferred_element_type=F32)
            pm = (sc * mask).astype(MXU_DTYPE)
            dv_acc[0:kmax, :] += lax.dot_general(pm, dob, TN, preferred_element_type=F32)
            dp = lax.dot_general(dob, v_ref[0:kmax, :], NT, preferred_element_type=F32)
            ds = (dp * mask).astype(MXU_DTYPE)
            dqr = lax.dot_general(ds, kr_ref[0:kmax, :], NN, preferred_element_type=F32)
            dq_ref[rows, :] = _rot(dqr, cq, -sq).astype(dq_ref.dtype)
            dk_acc[0:kmax, :] += lax.dot_general(ds, qr, TN, preferred_element_type=F32)
        dkr = dk_acc[...] * (dk ** -0.5)
        dk_ref[...] = _rot(dkr, cos_ref[...], -sin_ref[...]).astype(dk_ref.dtype)
        dv_ref[...] = dv_acc[...].astype(dv_ref.dtype)

    t = nb * s
    return pl.pallas_call(
        body, name=name, grid=(nb, h),
        in_specs=_ret_specs(s) + [pl.BlockSpec((s, dv), lambda b, hh: (b, hh))],
        out_specs=[pl.BlockSpec((s, dk), lambda b, hh: (b, hh)), pl.BlockSpec((s, dk), lambda b, hh: (b, hh)),
                   pl.BlockSpec((s, dv), lambda b, hh: (b, hh))],
        out_shape=[jax.ShapeDtypeStruct((t, h * dk), MXU_DTYPE), jax.ShapeDtypeStruct((t, h * dk), MXU_DTYPE),
                   jax.ShapeDtypeStruct((t, h * dv), MXU_DTYPE)],
        scratch_shapes=[pltpu.VMEM((s, dk), MXU_DTYPE), pltpu.VMEM((s, dk), F32), pltpu.VMEM((s, dv), F32)],
        compiler_params=_params(("parallel", "parallel")),
    )(p, p, p, cos2, sin2, log_g, do)


def _conv_grid(t, d, nb, ts):
    s = t // nb
    ns, nc = s // ts, d // CONV_TC
    return s, ns, nc


def _shifted(pad_ref, sh_ref, offsets):
    n = sh_ref.shape[1]
    for b in sorted({off % SUBLANES for off in offsets} - {0}):
        sh_ref[b - 1] = pad_ref[pl.ds(b, n), :]

    def read(off, r0):
        a, b = off - off % SUBLANES + r0, off % SUBLANES
        return pad_ref[pl.ds(a, SUBLANES), :] if b == 0 else sh_ref[b - 1, pl.ds(a, SUBLANES), :]

    return read


def _causal_taps(pad_ref, sh_ref, w_ref, k, emit):
    offs = [CONV_PAD - (k - 1) + j for j in range(k)]
    read = _shifted(pad_ref, sh_ref, offs)
    for r0 in range(0, pad_ref.shape[0] - CONV_PAD, CONV_ROWS):
        accs = [None] * len(CONV_TILES)
        for j in range(k):
            wj = w_ref[j]
            for q, dr in enumerate(CONV_TILES):
                term = wj * read(offs[j], r0 + dr)
                accs[q] = term if accs[q] is None else accs[q] + term
        emit(r0, jnp.concatenate(accs, axis=0))


def _tap_tiles(w):
    return jnp.broadcast_to(w[:, None, :], (w.shape[0], SUBLANES, w.shape[1]))


def _tap_spec(k):
    return pl.BlockSpec((k, SUBLANES, CONV_TC), lambda c, b, si: (0, 0, c))


def _carry_past(pad_ref, s_idx):
    ts = pad_ref.shape[0] - CONV_PAD

    @pl.when(s_idx == 0)
    def _():
        pad_ref[0:CONV_PAD, :] = jnp.zeros((CONV_PAD, pad_ref.shape[1]), F32)

    @pl.when(s_idx > 0)
    def _():
        pad_ref[0:CONV_PAD, :] = pad_ref[ts:ts + CONV_PAD, :]


def _carry_future(pad_ref, s_idx):
    ts = pad_ref.shape[0] - CONV_PAD

    @pl.when(s_idx == 0)
    def _():
        pad_ref[ts:ts + CONV_PAD, :] = jnp.zeros((CONV_PAD, pad_ref.shape[1]), F32)

    @pl.when(s_idx > 0)
    def _():
        pad_ref[ts:ts + CONV_PAD, :] = pad_ref[0:CONV_PAD, :]


def _conv_bwd_taps(pad_ref, sh_ref, w_ref, dw_acc, k, x_rows, emit, mix):
    read = _shifted(pad_ref, sh_ref, range(k))
    for r0 in range(0, pad_ref.shape[0] - CONV_PAD, CONV_ROWS):
        ops = x_rows(r0)
        x = mix(ops)
        accs = [None] * len(CONV_TILES)
        for j in range(k):
            wj, dwj = w_ref[j], None
            for q, dr in enumerate(CONV_TILES):
                sh = read(k - 1 - j, r0 + dr)
                term = wj * sh
                accs[q] = term if accs[q] is None else accs[q] + term
                prod = x[dr:dr + SUBLANES] * sh
                dwj = prod if dwj is None else dwj + prod
            dw_acc[j] += dwj
        emit(r0, ops, jnp.concatenate(accs, axis=0))


def _conv_bwd_edges(dw_acc, dw_ref, nb, ns, extra=()):
    first = jnp.logical_and(pl.program_id(1) == 0, pl.program_id(2) == 0)
    last = jnp.logical_and(pl.program_id(1) == nb - 1, pl.program_id(2) == ns - 1)

    @pl.when(first)
    def _():
        dw_acc[...] = jnp.zeros_like(dw_acc)
        for r in extra:
            r[...] = jnp.zeros_like(r)

    def finish():
        @pl.when(last)
        def _():
            dw_ref[...] = jnp.sum(dw_acc[...], axis=1)

    return finish


def short_conv_fwd(name, p, blk_b, w, nb):
    t = p.shape[0]
    d = w.shape[1]
    ts = SC_TS
    s, ns, nc = _conv_grid(t, d, nb, ts)
    cb = d // CONV_TC

    def body(b_ref, c_ref, x_ref, w_ref, y_ref, cz_ref, pad_ref, sh_ref):
        _carry_past(pad_ref, pl.program_id(2))
        pad_ref[CONV_PAD:CONV_PAD + ts, :] = c_ref[...].astype(F32) * x_ref[...].astype(F32)

        def emit(r0, cz):
            rows = pl.ds(r0, CONV_ROWS)
            cz_ref[rows, :] = cz
            y_ref[rows, :] = (b_ref[rows, :].astype(F32) * cz).astype(y_ref.dtype)

        _causal_taps(pad_ref, sh_ref, w_ref, SC_KERNEL, emit)

    def pspec(off):
        return pl.BlockSpec((ts, CONV_TC), lambda c, b, si: (b * ns + si, (blk_b + off) * cb + c))

    ospec = pl.BlockSpec((ts, CONV_TC), lambda c, b, si: (b * ns + si, c))
    return pl.pallas_call(
        body, name=name, grid=(nc, nb, ns),
        in_specs=[pspec(0), pspec(1), pspec(2), _tap_spec(SC_KERNEL)],
        out_specs=[ospec, ospec],
        out_shape=[jax.ShapeDtypeStruct((t, d), MXU_DTYPE), jax.ShapeDtypeStruct((t, d), F32)],
        scratch_shapes=_conv_scratch(ts),
        compiler_params=_params(("parallel", "arbitrary", "arbitrary")),
    )(p, p, p, _tap_tiles(w))


def short_conv_bwd(name, p, blk_b, w, cz, dy, nb):
    t = p.shape[0]
    d = w.shape[1]
    ts = SC_TS
    s, ns, nc = _conv_grid(t, d, nb, ts)
    cb = d // CONV_TC

    def body(b_ref, c_ref, x_ref, w_ref, cz_ref, dy_ref, db_ref, dc_ref, dx_ref, dw_ref, pad_ref, sh_ref, dw_acc):
        _carry_future(pad_ref, pl.program_id(2))
        dyv = dy_ref[...].astype(F32)
        db_ref[...] = (dyv * cz_ref[...]).astype(db_ref.dtype)
        pad_ref[0:ts, :] = dyv * b_ref[...].astype(F32)
        finish = _conv_bwd_edges(dw_acc, dw_ref, nb, ns)

        def x_rows(r0):
            rows = pl.ds(r0, CONV_ROWS)
            return c_ref[rows, :].astype(F32), x_ref[rows, :].astype(F32)

        def emit(r0, cx, dz):
            rows = pl.ds(r0, CONV_ROWS)
            dc_ref[rows, :] = (dz * cx[1]).astype(dc_ref.dtype)
            dx_ref[rows, :] = (dz * cx[0]).astype(dx_ref.dtype)

        _conv_bwd_taps(pad_ref, sh_ref, w_ref, dw_acc, SC_KERNEL, x_rows, emit, lambda cx: cx[0] * cx[1])
        finish()

    def row(b, si):
        return b * ns + (ns - 1 - si)

    def pspec(off):
        return pl.BlockSpec((ts, CONV_TC), lambda c, b, si: (row(b, si), (blk_b + off) * cb + c))

    ospec = pl.BlockSpec((ts, CONV_TC), lambda c, b, si: (row(b, si), c))
    wspec = pl.BlockSpec((SC_KERNEL, CONV_TC), lambda c, b, si: (0, c))
    return pl.pallas_call(
        body, name=name, grid=(nc, nb, ns),
        in_specs=[pspec(0), pspec(1), pspec(2), _tap_spec(SC_KERNEL), ospec, ospec],
        out_specs=[ospec, ospec, ospec, wspec],
        out_shape=[jax.ShapeDtypeStruct((t, d), MXU_DTYPE)] * 3 + [jax.ShapeDtypeStruct((SC_KERNEL, d), F32)],
        scratch_shapes=_conv_scratch(ts) + [pltpu.VMEM((SC_KERNEL, SUBLANES, CONV_TC), F32)],
        compiler_params=_params(("parallel", "arbitrary", "arbitrary")),
    )(p, p, p, _tap_tiles(w), cz, dy)


def conformer_conv_fwd(name, p, blk_a, w, bias, nb):
    t = p.shape[0]
    d = w.shape[1]
    ts = CONV_TS
    s, ns, nc = _conv_grid(t, d, nb, ts)
    cb = d // CONV_TC

    def body(a_ref, b_ref, w_ref, bias_ref, u_ref, pad_ref, sh_ref):
        _carry_past(pad_ref, pl.program_id(2))
        pad_ref[CONV_PAD:CONV_PAD + ts, :] = a_ref[...].astype(F32) * jax.nn.sigmoid(b_ref[...].astype(F32))

        def emit(r0, u):
            u_ref[pl.ds(r0, CONV_ROWS), :] = u + bias_ref[0:1, :]

        _causal_taps(pad_ref, sh_ref, w_ref, CF_KERNEL, emit)

    def pspec(off):
        return pl.BlockSpec((ts, CONV_TC), lambda c, b, si: (b * ns + si, (blk_a + off) * cb + c))

    return pl.pallas_call(
        body, name=name, grid=(nc, nb, ns),
        in_specs=[pspec(0), pspec(1), _tap_spec(CF_KERNEL), pl.BlockSpec((SUBLANES, CONV_TC), lambda c, b, si: (0, c))],
        out_specs=pl.BlockSpec((ts, CONV_TC), lambda c, b, si: (b * ns + si, c)),
        out_shape=jax.ShapeDtypeStruct((t, d), F32),
        scratch_shapes=_conv_scratch(ts),
        compiler_params=_params(("parallel", "arbitrary", "arbitrary")),
    )(p, p, _tap_tiles(w), jnp.broadcast_to(bias, (SUBLANES, d)))


def conformer_conv_bwd(name, p, blk_a, w, du, nb):
    t = p.shape[0]
    d = w.shape[1]
    ts = CONV_TS
    s, ns, nc = _conv_grid(t, d, nb, ts)
    cb = d // CONV_TC

    def body(a_ref, b_ref, w_ref, du_ref, da_ref, db_ref, dw_ref, dbias_ref, pad_ref, sh_ref, dw_acc):
        _carry_future(pad_ref, pl.program_id(2))
        duv = du_ref[...]
        pad_ref[0:ts, :] = duv
        finish = _conv_bwd_edges(dw_acc, dw_ref, nb, ns, extra=(dbias_ref,))
        dbias_ref[...] += jnp.sum(duv, axis=0, keepdims=True)

        def x_rows(r0):
            rows = pl.ds(r0, CONV_ROWS)
            return a_ref[rows, :].astype(F32), jax.nn.sigmoid(b_ref[rows, :].astype(F32))

        def emit(r0, asg, du0):
            rows = pl.ds(r0, CONV_ROWS)
            av, sg = asg
            da_ref[rows, :] = (du0 * sg).astype(da_ref.dtype)
            db_ref[rows, :] = (du0 * av * sg * (1.0 - sg)).astype(db_ref.dtype)

        _conv_bwd_taps(pad_ref, sh_ref, w_ref, dw_acc, CF_KERNEL, x_rows, emit, lambda asg: asg[0] * asg[1])
        finish()

    def row(b, si):
        return b * ns + (ns - 1 - si)

    def pspec(off):
        return pl.BlockSpec((ts, CONV_TC), lambda c, b, si: (row(b, si), (blk_a + off) * cb + c))

    ospec = pl.BlockSpec((ts, CONV_TC), lambda c, b, si: (row(b, si), c))
    wspec = pl.BlockSpec((CF_KERNEL, CONV_TC), lambda c, b, si: (0, c))
    bspec = pl.BlockSpec((1, CONV_TC), lambda c, b, si: (0, c))
    return pl.pallas_call(
        body, name=name, grid=(nc, nb, ns),
        in_specs=[pspec(0), pspec(1), _tap_spec(CF_KERNEL), ospec],
        out_specs=[ospec, ospec, wspec, bspec],
        out_shape=[jax.ShapeDtypeStruct((t, d), MXU_DTYPE)] * 2
        + [jax.ShapeDtypeStruct((CF_KERNEL, d), F32), jax.ShapeDtypeStruct((1, d), F32)],
        scratch_shapes=_conv_scratch(ts) + [pltpu.VMEM((CF_KERNEL, SUBLANES, CONV_TC), F32)],
        compiler_params=_params(("parallel", "arbitrary", "arbitrary")),
    )(p, p, _tap_tiles(w), du)


BLOCKS = ("ffn1", "mixer", "ffn2")
BLOCK_WEIGHTS = {"ffn1": ("ffn1_w_gu", "ffn1_w_down"), "mixer": ("w_in", "w_ret_o", "w_sc_o", "w_cf_o", "w_o"),
                 "ffn2": ("ffn2_w_gu", "ffn2_w_down")}
BIG = BLOCK_WEIGHTS["ffn1"] + BLOCK_WEIGHTS["mixer"] + BLOCK_WEIGHTS["ffn2"]
MODE = {"ffn1_w_gu": "col", "ffn1_w_down": "row", "w_in": "col", "w_ret_o": "row", "w_sc_o": "row",
        "w_cf_o": "row", "w_o": "row", "ffn2_w_gu": "col", "ffn2_w_down": "row"}
NORM_OF = {"ffn1": 0, "mixer": 2, "ffn2": 4}
BLK_GATE, BLK_SCB, BLK_CFA, BLK_MERGE = 2, 3, 6, 8


def _rope_tables(positions):
    half = RET_QK_DIM // 2
    inv_freq = ROPE_BASE ** (-jnp.arange(half, dtype=F32) / half)
    ang = positions.astype(F32)[..., None] * inv_freq
    cos, sin = jnp.cos(ang), jnp.sin(ang)
    nb, s = positions.shape
    cos2 = jnp.concatenate([cos, cos], axis=-1).reshape(nb * s, RET_QK_DIM)
    sin2 = jnp.concatenate([-sin, sin], axis=-1).reshape(nb * s, RET_QK_DIM)
    return cos2, sin2


def _log_gamma():
    lg = jnp.log(1.0 - 2.0 ** (-5.0 - jnp.arange(RET_HEADS, dtype=F32)))
    return jnp.broadcast_to(lg[:, None, None], (RET_HEADS, 1, RET_QK_DIM))


def _ffn_fwd(xs, h, w, tag, g_post, g_next):
    gu, a = ffn_up("ffn_up", h, w[tag + "_w_gu"])
    y, out, h_next = mm_post("ffn_down", a, w[tag + "_w_down"], xs, g_post, 0.5, g_next)
    return out, h_next, dict(x=xs, h=h, gu=gu, a=a, y=y, w=w)


def _ffn_bwd(dxs, dy, sv, tag, g_pre, push, prev):
    w = sv["w"]
    gu_w, down_w = w[tag + "_w_gu"], w[tag + "_w_down"]
    dgu = ffn_down_dx("ffn_down_dx", dy, down_w, sv["gu"])
    grads = {tag + "_w_down": mm_dw("ffn_down_dw", sv["a"], dy, "row", down_w.shape),
             tag + "_w_gu": mm_dw("ffn_gu_dw", sv["h"], dgu, "col", gu_w.shape)}
    return mm_dx_norms("ffn_gu_dx", dgu, gu_w, sv["x"], g_pre, dxs, prev, push(grads))


def _mixer_fwd(xs, h, w, sm, g_post, g_next, rope, nb, s, mid):
    cos2, sin2, log_g = rope
    d = xs.shape[1]
    gate_blk = (BLK_GATE * d) // RET_V_DIM
    p = mm_fwd("mx_in", h, w["w_in"], "col", MXU_DTYPE)
    if mid is not None:
        sm = dict(sm, cf_dw_b=sm["cf_dw_b"] + mid(p))
    o = retention_fwd("ret_fwd", p, cos2, sin2, log_g, nb, s)
    ya_in = head_gate_fwd("ret_gate", o, p, gate_blk)
    yb_in, cz = short_conv_fwd("sc_fwd", p, BLK_SCB, sm["sc_conv_w"], nb)
    u1 = conformer_conv_fwd("cf_fwd", p, BLK_CFA, sm["cf_dw_w"], sm["cf_dw_b"], nb)
    yc_in = ln_silu_fwd("cf_ln", u1, sm["cf_ln_g"], sm["cf_ln_b"])
    ya = mm_fwd("mx_proj", ya_in, w["w_ret_o"], "row", F32)
    yb = mm_fwd("mx_proj", yb_in, w["w_sc_o"], "row", F32)
    yc = mm_fwd("mx_proj", yc_in, w["w_cf_o"], "row", F32)
    mg = merge_fwd("mx_merge", p, BLK_MERGE, ya, yb, yc)
    m, out, h_next = mm_post("mx_out", mg, w["w_o"], xs, g_post, 1.0, g_next)
    return out, h_next, dict(x=xs, h=h, p=p, o=o, ya_in=ya_in, yb_in=yb_in, cz=cz, u1=u1, yc_in=yc_in, ya=ya, yb=yb, yc=yc,
                     mg=mg, m=m, w=w)


def _mixer_bwd(dxs, dm, sv, sm, g_pre, rope, nb, s, push, prev):
    cos2, sin2, log_g = rope
    w, p = sv["w"], sv["p"]
    d = dxs.shape[1]
    gate_blk = (BLK_GATE * d) // RET_V_DIM
    grads, gsm = {}, {}

    def proj_bwd(wname, a_in, dy, out_dtype):
        grads[wname] = mm_dw("mx_proj_dw", a_in, dy, "row", w[wname].shape)
        return mm_dx("mx_proj_dx", dy, w[wname], "row", out_dtype)

    dmg = proj_bwd("w_o", sv["mg"], dm, MXU_DTYPE)
    dg0, dg1, dg2, dya, dyb, dyc = merge_bwd("mx_merge_bwd", p, BLK_MERGE, sv["ya"], sv["yb"], sv["yc"], dmg)
    dya_in = proj_bwd("w_ret_o", sv["ya_in"], dya, MXU_DTYPE)
    dyb_in = proj_bwd("w_sc_o", sv["yb_in"], dyb, MXU_DTYPE)
    dyc_in = proj_bwd("w_cf_o", sv["yc_in"], dyc, MXU_DTYPE)
    do, dgret = head_gate_bwd("ret_gate_bwd", sv["o"], p, gate_blk, dya_in)
    dq, dk, dv = retention_bwd("ret_bwd", p, cos2, sin2, log_g, do, nb, s)
    dscb, dscc, dscx, gsm["sc_conv_w"] = short_conv_bwd("sc_bwd", p, BLK_SCB, sm["sc_conv_w"], sv["cz"], dyb_in, nb)
    du1, dlg, dlb = ln_silu_bwd("cf_ln_bwd", sv["u1"], sm["cf_ln_g"], sm["cf_ln_b"], dyc_in)
    dcfa, dcfb, gsm["cf_dw_w"], dbias = conformer_conv_bwd("cf_bwd", p, BLK_CFA, sm["cf_dw_w"], du1, nb)
    gsm.update(cf_ln_g=dlg[0], cf_ln_b=dlb[0], cf_dw_b=dbias[0])
    dp = concat_cols("mx_dp", [dq, dk, dv, dgret, dscb, dscc, dscx, dcfa, dcfb, dg0, dg1, dg2])
    grads["w_in"] = mm_dw("mx_in_dw", sv["h"], dp, "col", w["w_in"].shape)
    return mm_dx_norms("mx_in_dx", dp, w["w_in"], sv["x"], g_pre, dxs, prev, push(grads)) + (gsm,)


def local_step(x, positions, target, small, fetch, push):
    nb, s, d = x.shape
    t = nb * s
    depth = small["norm_g"].shape[0]
    rope = _rope_tables(positions) + (_log_gamma(),)
    xs = x.reshape(t, d)
    token = [None]

    def gain(l, i):
        g = small["norm_g"][l, i][None, :]
        if token[0] is not None:
            g, token[0] = g + token[0], None
        return g

    def mixer_small(l):
        return dict(sc_conv_w=small["sc_conv_w"][l], cf_dw_w=small["cf_dw_w"][l], cf_dw_b=small["cf_dw_b"][l][None, :],
                    cf_ln_g=small["cf_ln_g"][l][None, :], cf_ln_b=small["cf_ln_b"][l][None, :])

    saved = {}
    order = [(l, blk) for l in range(depth) for blk in BLOCKS]
    h = None
    for at, (l, blk) in enumerate(order):
        w, token[0], mid = fetch(l, blk, xs)
        i0 = NORM_OF[blk]
        if h is None:
            h = rms_fwd("first_rms", xs, gain(l, i0))
        g_post = gain(l, i0 + 1)
        g_next = gain(order[at + 1][0], NORM_OF[order[at + 1][1]]) if at + 1 < len(order) else None
        if blk == "mixer":
            xs, h, saved[l, blk] = _mixer_fwd(xs, h, w, mixer_small(l), g_post, g_next, rope, nb, s, mid)
        else:
            xs, h, saved[l, blk] = _ffn_fwd(xs, h, w, blk, g_post, g_next)

    dxs, loss = loss_head("loss", xs, target.reshape(t, d))

    dnorm = [[None] * 6 for _ in range(depth)]
    gsmall = {n: [None] * depth for n in ("sc_conv_w", "cf_dw_w", "cf_dw_b", "cf_ln_g", "cf_ln_b")}
    def branch(group):
        l, blk = group
        sv = saved[group]
        return (sv["m"], gain(l, NORM_OF[blk] + 1), 1.0) if blk == "mixer" else (sv["y"], gain(l, NORM_OF[blk] + 1), 0.5)

    l, blk = order[-1]
    y, g_post, scale = branch(order[-1])
    dy, dnorm[l][NORM_OF[blk] + 1] = post_bwd("last_post_bwd", y, g_post, dxs, scale)
    for at in reversed(range(len(order))):
        l, blk = order[at]
        i0 = NORM_OF[blk]
        prev = branch(order[at - 1]) if at > 0 else None
        put = functools.partial(push, l, blk)
        if blk == "mixer":
            dxs, dnorm[l][i0], dy, dg_prev, gsm = _mixer_bwd(
                dxs, dy, saved[l, blk], mixer_small(l), gain(l, i0), rope, nb, s, put, prev)
            for n, v in gsm.items():
                gsmall[n][l] = v
        else:
            dxs, dnorm[l][i0], dy, dg_prev = _ffn_bwd(dxs, dy, saved[l, blk], blk, gain(l, i0), put, prev)
        if at > 0:
            dnorm[order[at - 1][0]][NORM_OF[order[at - 1][1]] + 1] = dg_prev

    gs = {n: jnp.stack(v) for n, v in gsmall.items()}
    gs["norm_g"] = jnp.stack([jnp.concatenate(r, axis=0) for r in dnorm])
    return loss, dxs.reshape(nb, s, d), gs


ANY = pl.BlockSpec(memory_space=pl.ANY)
HBM = pl.BlockSpec(memory_space=pltpu.HBM)
SEM = pl.BlockSpec(memory_space=pltpu.SEMAPHORE)
VMEM_WHOLE = pl.BlockSpec(memory_space=pltpu.VMEM)
EFFECT = pltpu.SideEffectType.DATAFLOW_SIDE_EFFECTING
TOKEN = jax.ShapeDtypeStruct((8, 128), F32)


def _other_chips(x, y):
    return [(1 - x, y), (x, 1 - y), (1 - x, 1 - y)]


def _remote(src, dst, send_sem, recv_sem, to):
    return pltpu.make_async_remote_copy(src_ref=src, dst_ref=dst, send_sem=send_sem, recv_sem=recv_sem,
                                        device_id=to, device_id_type=MESH)


def _in_hbm(v):
    return pltpu.with_memory_space_constraint(v, pltpu.HBM)


def place_quarters(ws, layer, ids, after):
    m = len(ws)

    def body(ids_ref, *refs):
        for w_ref, o_ref in zip(refs[:m], refs[m + 1:]):
            o_ref[...] = w_ref[...].astype(o_ref.dtype)

    def spec(w, where):
        return pl.BlockSpec((None, w.shape[1] // STREAM_STEPS, w.shape[2]), where)

    return pl.pallas_call(
        body, name="place_quarters",
        grid_spec=pltpu.PrefetchScalarGridSpec(
            num_scalar_prefetch=1, grid=(STREAM_STEPS,),
            in_specs=[spec(w, lambda i, ids_ref: (layer, i, 0)) for w in ws] + [ANY],
            out_specs=[spec(w, lambda i, ids_ref: (ids_ref[0], i, 0)) for w in ws]),
        out_shape=[jax.ShapeDtypeStruct((N_CHIP,) + w.shape[1:], MXU_DTYPE) for w in ws],
        compiler_params=_params(("parallel",)),
    )(ids, *ws, after)


def _gather_copies(lands, send, recv):
    x, y, c = _axes()
    me = 2 * x + y
    mine, theirs = [], []
    for a, ld in enumerate(lands):
        rh = ld.shape[1] // 2
        rows = pl.ds(c * rh, rh)
        for k, (px, py) in enumerate(_other_chips(x, y)):
            to = (px, py, c)
            mine.append(_remote(ld.at[me, rows, :], ld.at[me, rows, :], send.at[3 * a + k], recv.at[3 * a + k], to))
            got = ld.at[2 * px + py, rows, :]
            theirs.append(_remote(got, got, send.at[3 * a + k], recv.at[3 * a + k], to))
    return mine, theirs


def gather_start(name, groups, after):
    flat = [s for g in groups for s in g]
    n, ng = len(flat), len(groups)
    sizes = [len(g) for g in groups]

    def body(*refs):
        lands = refs[:n]
        sems = refs[n + 1:n + 1 + 2 * ng]
        token = refs[-1]
        at = 0
        for g, m in enumerate(sizes):
            mine, _ = _gather_copies(lands[at:at + m], sems[2 * g], sems[2 * g + 1])
            for cp in mine:
                cp.start()
            at += m
        token[...] = jnp.zeros_like(token)

    sem_shapes = []
    for m in sizes:
        sem_shapes += [pltpu.SemaphoreType.DMA((3 * m,))] * 2
    res = pl.pallas_call(
        body, name=name, in_specs=[HBM] * n + [ANY],
        out_specs=[SEM] * (2 * ng) + [HBM] * n + [VMEM_WHOLE],
        out_shape=sem_shapes + [pltpu.HBM(s.shape, s.dtype) for s in flat] + [TOKEN],
        input_output_aliases={i: 2 * ng + i for i in range(n)},
        compiler_params=pltpu.CompilerParams(has_side_effects=EFFECT),
    )(*[_in_hbm(s) for s in flat], after)
    sems, thru, token = res[:2 * ng], res[2 * ng:2 * ng + n], res[-1]
    out, at = [], 0
    for g, m in enumerate(sizes):
        out.append((sems[2 * g], sems[2 * g + 1], thru[at:at + m]))
        at += m
    return out, token


def gather_wait(lands, send, recv, after):
    m = len(lands)

    def body(*refs):
        mine, theirs = _gather_copies(refs[:m], refs[m], refs[m + 1])
        for cp in mine:
            cp.wait_send()
        for cp in theirs:
            cp.wait_recv()

    return pl.pallas_call(
        body, name="gather_wait", in_specs=[HBM] * m + [SEM, SEM, ANY], out_specs=[HBM] * m,
        out_shape=[pltpu.HBM(l.shape, l.dtype) for l in lands],
        input_output_aliases={i: i for i in range(m)},
        compiler_params=pltpu.CompilerParams(has_side_effects=EFFECT),
    )(*lands, send, recv, after)


def copy_start(name, bufs, copies, ncopy, after=()):
    n, k = len(bufs), len(after)

    def body(*refs):
        for cp in copies(refs[:n], refs[n + k], refs[n + k + 1])[0]:
            cp.start()
        refs[-1][...] = jnp.zeros_like(refs[-1])

    res = pl.pallas_call(
        body, name=name, in_specs=[HBM] * n + [ANY] * k, out_specs=[SEM, SEM] + [HBM] * n + [VMEM_WHOLE],
        out_shape=[pltpu.SemaphoreType.DMA((ncopy,))] * 2 + [pltpu.HBM(b.shape, b.dtype) for b in bufs] + [TOKEN],
        input_output_aliases={i: 2 + i for i in range(n)},
        compiler_params=pltpu.CompilerParams(has_side_effects=EFFECT),
    )(*[_in_hbm(b) for b in bufs], *after)
    return res[0], res[1], list(res[2:2 + n]), res[-1]


def copy_wait(name, bufs, send, recv, copies, after=()):
    n = len(bufs)

    def body(*refs):
        mine, theirs = copies(refs[:n], refs[n], refs[n + 1])
        for cp in mine:
            cp.wait_send()
        for cp in theirs:
            cp.wait_recv()

    return list(pl.pallas_call(
        body, name=name, in_specs=[HBM] * n + [SEM, SEM] + [ANY] * len(after), out_specs=[HBM] * n,
        out_shape=[pltpu.HBM(b.shape, b.dtype) for b in bufs], input_output_aliases={i: i for i in range(n)},
        compiler_params=pltpu.CompilerParams(has_side_effects=EFFECT),
    )(*bufs, send, recv, *after))


def _fill_copies(lands, send, recv):
    x, y, c = _axes()
    sib = (x, y, 1 - c)
    mine, theirs = [], []
    for a, ld in enumerate(lands):
        rh = ld.shape[1] // 2
        for k, (px, py) in enumerate(_other_chips(x, y)):
            got = ld.at[2 * px + py, pl.ds(c * rh, rh), :]
            mine.append(_remote(got, got, send.at[3 * a + k], recv.at[3 * a + k], sib))
            blk = ld.at[2 * px + py, pl.ds((1 - c) * rh, rh), :]
            theirs.append(_remote(blk, blk, send.at[3 * a + k], recv.at[3 * a + k], sib))
    return mine, theirs


def _presum_copies(grads, lands, send, recv):
    x, y, c = _axes()
    cps = []
    for a, (g, ld) in enumerate(zip(grads, lands)):
        rh = g.shape[1] // 2
        cps.append(_remote(g.at[:, pl.ds((1 - c) * rh, rh), :], ld, send.at[a], recv.at[a], (x, y, 1 - c)))
    return cps


def presum_start(grads):
    m = len(grads)

    def body(*refs):
        for cp in _presum_copies(refs[:m], refs[m:2 * m], refs[2 * m], refs[2 * m + 1]):
            cp.start()
        refs[-1][...] = jnp.zeros_like(refs[-1])

    lands = [lax.empty((g.shape[0], g.shape[1] // 2, g.shape[2]), g.dtype) for g in grads]
    res = pl.pallas_call(
        body, name="presum_start", in_specs=[HBM] * (2 * m), out_specs=[SEM, SEM] + [HBM] * (2 * m) + [VMEM_WHOLE],
        out_shape=[pltpu.SemaphoreType.DMA((m,))] * 2 + [pltpu.HBM(g.shape, g.dtype) for g in grads]
        + [pltpu.HBM(l.shape, l.dtype) for l in lands] + [TOKEN],
        input_output_aliases={i: 2 + i for i in range(2 * m)},
        compiler_params=pltpu.CompilerParams(has_side_effects=EFFECT),
    )(*[_in_hbm(g) for g in grads], *[_in_hbm(l) for l in lands])
    return res[0], res[1], res[2:2 + m], res[2 + m:2 + 2 * m], res[-1]


def presum_wait(grads, lands, send, recv, after):
    m = len(grads)

    def body(*refs):
        for cp in _presum_copies(refs[:m], refs[m:2 * m], refs[2 * m], refs[2 * m + 1]):
            cp.wait_send()
            cp.wait_recv()

    res = pl.pallas_call(
        body, name="presum_wait", in_specs=[HBM] * (2 * m) + [SEM, SEM] + [ANY] * len(after),
        out_specs=[HBM] * (2 * m),
        out_shape=[pltpu.HBM(g.shape, g.dtype) for g in grads] + [pltpu.HBM(l.shape, l.dtype) for l in lands],
        input_output_aliases={i: i for i in range(2 * m)},
        compiler_params=pltpu.CompilerParams(has_side_effects=EFFECT),
    )(*grads, *lands, send, recv, *after)
    return res[:m], res[m:]


def add_halves(gs, lands, ids):
    m = len(gs)

    def body(ids_ref, *refs):
        for a_ref, b_ref, o_ref in zip(refs[:m], refs[m:2 * m], refs[2 * m:]):
            o_ref[...] = (a_ref[...].astype(F32) + b_ref[...].astype(F32)).astype(o_ref.dtype)

    def spec(ld, where):
        return pl.BlockSpec((None,) + ld.shape[1:], where)

    return pl.pallas_call(
        body, name="add_halves",
        grid_spec=pltpu.PrefetchScalarGridSpec(
            num_scalar_prefetch=1, grid=(N_CHIP,),
            in_specs=[spec(ld, lambda i, ids_ref: (i, ids_ref[1], 0)) for ld in lands]
            + [spec(ld, lambda i, ids_ref: (i, 0, 0)) for ld in lands],
            out_specs=[spec(ld, lambda i, ids_ref: (i, 0, 0)) for ld in lands]),
        out_shape=[jax.ShapeDtypeStruct(ld.shape, ld.dtype) for ld in lands],
        compiler_params=_params(("parallel",)),
    )(ids, *gs, *lands)


def _scatter_copies(parts, lands, send, recv):
    x, y, c = _axes()
    cps = []
    for a, (pt, ld) in enumerate(zip(parts, lands)):
        for k, (px, py) in enumerate(_other_chips(x, y)):
            cps.append(_remote(pt.at[2 * px + py], ld.at[k], send.at[3 * a + k], recv.at[3 * a + k], (px, py, c)))
    return cps


def scatter_start(parts):
    m = len(parts)

    def body(*refs):
        for cp in _scatter_copies(refs[:m], refs[m:2 * m], refs[2 * m], refs[2 * m + 1]):
            cp.start()
        refs[-1][...] = jnp.zeros_like(refs[-1])

    lands = [lax.empty((N_CHIP - 1,) + p.shape[1:], p.dtype) for p in parts]
    res = pl.pallas_call(
        body, name="scatter_start", in_specs=[HBM] * (2 * m), out_specs=[SEM, SEM] + [HBM] * (2 * m) + [VMEM_WHOLE],
        out_shape=[pltpu.SemaphoreType.DMA((3 * m,))] * 2 + [pltpu.HBM(p.shape, p.dtype) for p in parts]
        + [pltpu.HBM(l.shape, l.dtype) for l in lands] + [TOKEN],
        input_output_aliases={i: 2 + i for i in range(2 * m)},
        compiler_params=pltpu.CompilerParams(has_side_effects=EFFECT),
    )(*[_in_hbm(p) for p in parts], *[_in_hbm(l) for l in lands])
    return res[0], res[1], res[2:2 + m], res[2 + m:2 + 2 * m], res[-1]


def scatter_wait(parts, lands, send, recv, after):
    m = len(parts)

    def body(*refs):
        for cp in _scatter_copies(refs[:m], refs[m:2 * m], refs[2 * m], refs[2 * m + 1]):
            cp.wait_send()
            cp.wait_recv()

    res = pl.pallas_call(
        body, name="scatter_wait", in_specs=[HBM] * (2 * m) + [SEM, SEM] + [ANY] * len(after),
        out_specs=[HBM] * (2 * m),
        out_shape=[pltpu.HBM(p.shape, p.dtype) for p in parts] + [pltpu.HBM(l.shape, l.dtype) for l in lands],
        input_output_aliases={i: i for i in range(2 * m)},
        compiler_params=pltpu.CompilerParams(has_side_effects=EFFECT),
    )(*parts, *lands, send, recv, *after)
    return res[:m], res[m:]


def sum_partials(parts, lands, ids, layer, depth, intos):
    m = len(parts)
    nt = STREAM_STEPS

    def body(ids_ref, *refs):
        for p_ref, l_ref, o_ref in zip(refs[:m], refs[m:2 * m], refs[-m:]):
            acc = p_ref[...].astype(F32)
            for k in range(N_CHIP - 1):
                acc = acc + l_ref[k].astype(F32)
            o_ref[...] = acc

    def rows(p):
        return p.shape[1] // nt

    in_specs = [pl.BlockSpec((None, rows(p), p.shape[2]), lambda i, ids_ref: (ids_ref[0], i, 0)) for p in parts]
    in_specs += [pl.BlockSpec((N_CHIP - 1, rows(p), p.shape[2]), lambda i, ids_ref: (0, i, 0)) for p in parts]
    args = [ids, *parts, *lands]
    aliases = {}
    if intos is not None:
        in_specs += [ANY] * m
        args += list(intos)
        aliases = {1 + 2 * m + a: a for a in range(m)}
    return pl.pallas_call(
        body, name="sum_partials",
        grid_spec=pltpu.PrefetchScalarGridSpec(
            num_scalar_prefetch=1, grid=(nt,), in_specs=in_specs,
            out_specs=[pl.BlockSpec((None, rows(p), p.shape[2]), lambda i, ids_ref: (layer, ids_ref[1] * nt + i, 0))
                       for p in parts]),
        out_shape=[jax.ShapeDtypeStruct((depth, 2 * p.shape[1], p.shape[2]), F32) for p in parts],
        input_output_aliases=aliases, compiler_params=_params(("parallel",)),
    )(*args)


def _final_copies(layer):
    def copies(bufs, send, recv):
        x, y, c = _axes()
        sib = (x, y, 1 - c)
        mine, theirs = [], []
        for a, buf in enumerate(bufs):
            rh = buf.shape[1] // 2
            src = buf.at[layer, pl.ds(c * rh, rh), :]
            mine.append(_remote(src, src, send.at[a], recv.at[a], sib))
            dst = buf.at[layer, pl.ds((1 - c) * rh, rh), :]
            theirs.append(_remote(dst, dst, send.at[a], recv.at[a], sib))
        return mine, theirs

    return copies


def allgather_small(pk):
    def body(in_ref, out_ref, send, recv):
        x, y, c = _axes()
        me = 2 * x + y
        chips = _other_chips(x, y)
        out_ref[pl.ds(me, 1)] = in_ref[...][None]
        cps = []
        for k, (px, py) in enumerate(chips):
            cp = _remote(in_ref, out_ref.at[me], send.at[k], recv.at[k], (px, py, c))
            cp.start()
            cps.append(cp)
        for k, (px, py) in enumerate(chips):
            got = out_ref.at[2 * px + py]
            _remote(got, got, send.at[k], recv.at[k], (px, py, c)).wait_recv()
        for cp in cps:
            cp.wait_send()

    return pl.pallas_call(
        body, name="allgather_small", in_specs=[VMEM_WHOLE], out_specs=VMEM_WHOLE,
        out_shape=jax.ShapeDtypeStruct((N_CHIP,) + pk.shape, pk.dtype),
        scratch_shapes=[pltpu.SemaphoreType.DMA((3,))] * 2,
    )(pk)


N_DEV = 8


def _small_copies(bufs, send, recv):
    g, slots = bufs
    x, y, c = _axes()
    me = 4 * x + 2 * y + c
    mine, theirs = [], []
    for mask in range(1, N_DEV):
        px = 1 - x if mask & 4 else x
        py = 1 - y if mask & 2 else y
        pc = 1 - c if mask & 1 else c
        mine.append(_remote(g, slots.at[me], send.at[mask - 1], recv.at[mask - 1], (px, py, pc)))
        got = slots.at[4 * px + 2 * py + pc]
        theirs.append(_remote(got, got, send.at[mask - 1], recv.at[mask - 1], (px, py, pc)))
    return mine, theirs


def sum_slots(g, slots, me):
    def body(me_ref, g_ref, slots_ref, o_ref):
        acc = None
        for d in range(N_DEV):
            term = jnp.where(me_ref[0] == d, g_ref[...], slots_ref[d])
            acc = term if acc is None else acc + term
        o_ref[...] = acc

    return pl.pallas_call(
        body, name="sum_slots",
        grid_spec=pltpu.PrefetchScalarGridSpec(
            num_scalar_prefetch=1, grid=(1,),
            in_specs=[pl.BlockSpec(g.shape, lambda i, me_ref: (0, 0)),
                      pl.BlockSpec(slots.shape, lambda i, me_ref: (0, 0, 0))],
            out_specs=pl.BlockSpec(g.shape, lambda i, me_ref: (0, 0))),
        out_shape=jax.ShapeDtypeStruct(g.shape, g.dtype),
        compiler_params=_params(("arbitrary",)),
    )(me, g, slots)


def adamw(w, g, m, v, layer=None, intos=None):
    shape = w.shape
    cols = shape[-1]
    rows = int(np.prod(shape[:-1]))
    span = rows if layer is None else rows // shape[0]
    tr = span
    for cand in (256, 128):
        if span % cand == 0 and cand * cols * 4 <= 2 * 1024 * 1024:
            tr = cand
            break
    first = 0 if layer is None else layer * (span // tr)
    c1 = 1.0 - ADAM_B1 ** ADAM_STEP
    c2 = 1.0 - ADAM_B2 ** ADAM_STEP

    def body(w_ref, g_ref, m_ref, v_ref, *rest):
        d_ref, nm_ref, nv_ref, g_out = rest[-4:]
        gv = g_ref[...]
        g_out[...] = gv
        nm = ADAM_B1 * m_ref[...] + (1.0 - ADAM_B1) * gv
        nv = ADAM_B2 * v_ref[...] + (1.0 - ADAM_B2) * jnp.square(gv)
        d_ref[...] = -ADAM_LR * ((nm / c1) / (jnp.sqrt(nv / c2) + ADAM_EPS) + ADAM_WD * w_ref[...])
        nm_ref[...] = nm
        nv_ref[...] = nv

    spec = pl.BlockSpec((tr, cols), lambda i: (first + i, 0))
    args = [a.reshape(rows, cols) for a in (w, g, m, v)]
    in_specs, aliases = [spec] * 4, {}
    if intos is not None:
        args += [a.reshape(rows, cols) for a in intos]
        in_specs += [ANY] * 4
        aliases = {4 + k: k for k in range(4)}
    res = pl.pallas_call(
        body, name="adamw", grid=(span // tr,), in_specs=in_specs, out_specs=[spec] * 4,
        out_shape=[jax.ShapeDtypeStruct((rows, cols), F32)] * 4, input_output_aliases=aliases,
        compiler_params=_params(("parallel",)),
    )(*args)
    return [r.reshape(shape) for r in res]


WEIGHTS = ("norm_g", "ffn1_w_gu", "ffn1_w_down", "w_in", "w_ret_o", "sc_conv_w", "w_sc_o", "cf_dw_w", "cf_dw_b",
           "cf_ln_g", "cf_ln_b", "w_cf_o", "w_o", "ffn2_w_gu", "ffn2_w_down")
SHARDED_SMALL = ("norm_g", "sc_conv_w", "cf_dw_w")
REPLICATED_SMALL = ("cf_dw_b", "cf_ln_g", "cf_ln_b")

def _pack_rows(parts):
    padded, offs, at = [], [], 0
    for p in parts:
        r = -(-p.shape[0] // SUBLANES) * SUBLANES
        padded.append(jnp.pad(p, ((0, r - p.shape[0]), (0, 0))))
        offs.append(at)
        at += r
    return jnp.concatenate(padded, axis=0), offs


def kernel(x, positions, norm_g, ffn1_w_gu, ffn1_w_down, w_in, w_ret_o, sc_conv_w, w_sc_o, cf_dw_w, cf_dw_b, cf_ln_g, cf_ln_b, w_cf_o, w_o, ffn2_w_gu, ffn2_w_down, loss_target, m_norm_g, m_ffn1_w_gu, m_ffn1_w_down, m_w_in, m_w_ret_o, m_sc_conv_w, m_w_sc_o, m_cf_dw_w, m_cf_dw_b, m_cf_ln_g, m_cf_ln_b, m_w_cf_o, m_w_o, m_ffn2_w_gu, m_ffn2_w_down, v_norm_g, v_ffn1_w_gu, v_ffn1_w_down, v_w_in, v_w_ret_o, v_sc_conv_w, v_w_sc_o, v_cf_dw_w, v_cf_dw_b, v_cf_ln_g, v_cf_ln_b, v_w_cf_o, v_w_o, v_ffn2_w_gu, v_ffn2_w_down):
    wts = dict(zip(WEIGHTS, (norm_g, ffn1_w_gu, ffn1_w_down, w_in, w_ret_o, sc_conv_w, w_sc_o, cf_dw_w, cf_dw_b,
                             cf_ln_g, cf_ln_b, w_cf_o, w_o, ffn2_w_gu, ffn2_w_down)))
    mom = dict(zip(WEIGHTS, (m_norm_g, m_ffn1_w_gu, m_ffn1_w_down, m_w_in, m_w_ret_o, m_sc_conv_w, m_w_sc_o,
                             m_cf_dw_w, m_cf_dw_b, m_cf_ln_g, m_cf_ln_b, m_w_cf_o, m_w_o, m_ffn2_w_gu, m_ffn2_w_down)))
    var = dict(zip(WEIGHTS, (v_norm_g, v_ffn1_w_gu, v_ffn1_w_down, v_w_in, v_w_ret_o, v_sc_conv_w, v_w_sc_o,
                             v_cf_dw_w, v_cf_dw_b, v_cf_ln_g, v_cf_ln_b, v_w_cf_o, v_w_o, v_ffn2_w_gu, v_ffn2_w_down)))
    depth = norm_g.shape[0]
    dq = norm_g.shape[-1]
    d = N_CHIP * dq
    chip = 2 * lax.axis_index("x") + lax.axis_index("y")
    ids = jnp.stack([chip, lax.axis_index("c")]).astype(jnp.int32)

    pk, offs = _pack_rows([wts[n].reshape(-1, dq) for n in SHARDED_SMALL])
    gk4 = allgather_small(pk)
    gk = gk4.transpose(1, 0, 2).reshape(pk.shape[0], d)
    small = {n: wts[n] for n in REPLICATED_SMALL}
    for n, o in zip(SHARDED_SMALL, offs):
        rows = wts[n].shape[0] * wts[n].shape[1]
        small[n] = gk[o:o + rows].reshape(wts[n].shape[:2] + (d,))

    order = [(l, blk) for l in range(depth) for blk in BLOCKS]
    def placed(groups, after):
        return [place_quarters([wts[n] for n in BLOCK_WEIGHTS[blk]], l, ids, after) for l, blk in groups]

    first, token = gather_start("gather_start_first", placed(order[:1], gk4), gk4)
    rest, token = gather_start("gather_start_rest", placed(order[1:], token), token)
    started = dict(zip(order, first + rest))
    small["norm_g"] = small["norm_g"] + token[0:1, 0:1]

    filling = {}

    def fill(group, after):
        send, recv, lands = started[group]
        lands = gather_wait(lands, send, recv, after)
        send, recv, lands, tok = copy_start("fill_start", lands, _fill_copies, 3 * len(lands))
        filling[group] = (send, recv, lands)
        return tok[0:1, 0:1]

    def fetch(l, blk, after):
        at = order.index((l, blk))
        if (l, blk) not in filling:
            fill((l, blk), token if at == 0 else after)
        send, recv, lands = filling.pop((l, blk))
        lands = copy_wait("fill_wait", lands, send, recv, _fill_copies, (after,))
        tok, mid = None, None
        if at == 1:
            mid = functools.partial(fill, order[at + 1])
        elif 1 < at < len(order) - 1:
            tok = fill(order[at + 1], lands[0])
        return dict(zip(BLOCK_WEIGHTS[blk], lands)), tok, mid

    gsum = {n: None for n in BIG}
    presums, scatters, finals = [], [], []

    def scatter_next(after):
        group, gl, lands, send, recv = presums.pop(0)
        gl, lands = presum_wait(gl, lands, send, recv, after)
        send, recv, parts, lands, tok = scatter_start(add_halves(gl, lands, ids))
        scatters.append((group, parts, lands, send, recv))
        return tok

    def sum_next(after):
        (l, blk), parts, lands, send, recv = scatters.pop(0)
        parts, lands = scatter_wait(parts, lands, send, recv, after)
        names = BLOCK_WEIGHTS[blk]
        intos = None if gsum[names[0]] is None else [gsum[n] for n in names]
        sums = sum_partials(parts, lands, ids, l, depth, intos)
        send, recv, sums, tok = copy_start("final_start", sums, _final_copies(l), len(sums))
        gsum.update(zip(names, sums))
        finals.append((names, l, send, recv))
        return tok

    def final_next(after):
        names, l, send, recv = finals.pop(0)
        gsum.update(zip(names, copy_wait("final_wait", [gsum[n] for n in names], send, recv, _final_copies(l), after)))

    def push(l, blk, grads):
        send, recv, gl, lands, tok = presum_start([grads[n] for n in BLOCK_WEIGHTS[blk]])
        if scatters:
            tok = tok + sum_next((gl[0],))
        if presums:
            tok = tok + scatter_next((gl[0],))
        presums.append(((l, blk), gl, lands, send, recv))
        return tok[0:1, 0:1]

    loss, grad_x, gs = local_step(x, positions, loss_target, small, fetch, push)

    names = SHARDED_SMALL + REPLICATED_SMALL
    pg, offs = _pack_rows([gs[n].reshape(-1, d) for n in names])
    s_send, s_recv, s_bufs, tok = copy_start("small_start", [pg, lax.empty((N_DEV,) + pg.shape, pg.dtype)],
                                             _small_copies, N_DEV - 1, (grad_x,))
    tok = scatter_next((grad_x, tok))

    delta, new_m, new_v, grads = {}, {}, {}, {}

    def update(n, layer=None):
        g = gsum[n] if n in BIG else grads[n]
        prev = [delta[n], new_m[n], new_v[n], grads[n]] if layer is not None and n in delta else None
        delta[n], new_m[n], new_v[n], grads[n] = adamw(wts[n], g, mom[n], var[n], layer, prev)

    while finals and finals[0][1] > 0:
        done, l = finals[0][:2]
        final_next((tok,))
        for n in done:
            update(n, l)
    upper = tuple(delta[n] for n in BIG if n in delta)
    pg, slots = copy_wait("small_wait", s_bufs, s_send, s_recv, _small_copies, upper + (tok,))
    me = (2 * chip + lax.axis_index("c")).astype(jnp.int32).reshape(1)
    tot = sum_slots(pg, slots, me)
    for n, o in zip(names, offs):
        rows = int(np.prod(gs[n].shape[:-1]))
        full = tot[o:o + rows]
        if n in SHARDED_SMALL:
            full = lax.dynamic_slice_in_dim(full, chip * dq, dq, axis=1)
        grads[n] = full.reshape(wts[n].shape)

    for n in names:
        update(n)
    after = tuple(delta[n] for n in names)
    while scatters or finals:
        if scatters:
            after = (sum_next(after),)
        done, l = finals[0][:2]
        final_next(after)
        for n in done:
            update(n, l)
        after = tuple(delta[n] for n in done)

    loss_all = lax.psum(loss[0, 0], ("x", "y", "c"))
    return (loss_all, grad_x, *[grads[n] for n in WEIGHTS], *[delta[n] for n in WEIGHTS],
            *[new_m[n] for n in WEIGHTS], *[new_v[n] for n in WEIGHTS])
```

```python
import functools

import jax
import jax.numpy as jnp
import numpy as np
from jax import lax
from jax.experimental import pallas as pl
from jax.experimental.pallas import tpu as pltpu

F32 = jnp.float32
BF16 = jnp.bfloat16
MXU_DTYPE = BF16
VMEM_LIMIT_BYTES = 56 * 1024 * 1024
MESH = pl.DeviceIdType.MESH

N_CHIP = 4
CHUNK = 64
RET_HEADS = 4
RET_QK_DIM = 128
RET_V_DIM = 256
SC_KERNEL = 3
CF_KERNEL = 31
ROPE_BASE = 10000.0
NORM_EPS = 1e-6
LN_EPS = 1e-5
ADAM_LR = 0.001
ADAM_B1 = 0.9
ADAM_B2 = 0.999
ADAM_EPS = 1e-08
ADAM_WD = 0.01
ADAM_STEP = 10

SUBLANES = 8
CONV_PAD = 32
CONV_TS = 128
CONV_TC = 512
CONV_ROWS = 32
CONV_TILES = range(0, CONV_ROWS, SUBLANES)
SC_TS = 512


def _conv_scratch(ts):
    return [pltpu.VMEM((ts + CONV_PAD, CONV_TC), F32),
            pltpu.VMEM((SUBLANES - 1, ts + CONV_PAD - SUBLANES, CONV_TC), F32)]
RET_TQ = 512
MM_TM = 1024
MM_TN = 1536
MM_K1 = 1024
MM_W1 = 8 << 20
MM_SLICE = 256
MM_IN_BYTES = 36 << 20
STREAM_STEPS = 2


def _params(sem):
    return pltpu.CompilerParams(dimension_semantics=sem, vmem_limit_bytes=VMEM_LIMIT_BYTES)


def _axes():
    return lax.axis_index("x"), lax.axis_index("y"), lax.axis_index("c")


NN = (((1,), (0,)), ((), ()))
NT = (((1,), (1,)), ((), ()))
TN = (((0,), (0,)), ((), ()))


def _mm(name, a, b, out_shape, out_dtype, grid, a_spec, b_spec, o_spec, dims, acc_shape):
    nk = grid[2]

    def body(a_ref, b_ref, o_ref, *scratch):
        bv = b_ref[...]
        if bv.ndim == 3:
            bv = bv.reshape(-1, bv.shape[-1])
        part = lax.dot_general(a_ref[...], bv, dims, preferred_element_type=F32)

        def put(v):
            o_ref[...] = v.reshape(o_ref.shape).astype(o_ref.dtype)

        if nk == 1:
            put(part)
        else:
            acc = scratch[0]
            k = pl.program_id(2)

            @pl.when(k == 0)
            def _():
                acc[...] = part

            @pl.when(k > 0)
            def _():
                acc[...] += part

            @pl.when(k == nk - 1)
            def _():
                put(acc[...])

    scratch = [pltpu.VMEM(acc_shape, F32)] if nk > 1 else []
    return pl.pallas_call(
        body, name=name, grid=grid, in_specs=[a_spec, b_spec], out_specs=o_spec,
        out_shape=jax.ShapeDtypeStruct(out_shape, out_dtype), scratch_shapes=scratch,
        compiler_params=_params(("parallel", "parallel", "arbitrary")),
    )(a, b)


def _tile(n, target):
    best = None
    for t in range(128, min(n, target) + 1, 128):
        if n % t == 0:
            best = t
    assert best is not None, (n, target)
    return best


def _token_rows(t, width):
    tt = t
    while tt > MM_TM and tt * width * jnp.dtype(MXU_DTYPE).itemsize * 2 > MM_IN_BYTES:
        tt //= 2
    return tt


def mm_fwd(name, a, w4, mode, out_dtype):
    t = a.shape[0]
    _, r, c = w4.shape
    tm = min(t, MM_TM)
    if mode == "col":
        tn = _tile(c, MM_TN)
        npj = c // tn
        grid = (t // tm, N_CHIP * npj, 1)
        a_spec = pl.BlockSpec((tm, r), lambda i, j, k: (i, 0))
        b_spec = pl.BlockSpec((None, r, tn), lambda i, j, k: (j // npj, 0, j % npj))
        o_spec = pl.BlockSpec((tm, tn), lambda i, j, k: (i, j))
        return _mm(name, a, w4, (t, N_CHIP * c), out_dtype, grid, a_spec, b_spec, o_spec, NN, (tm, tn))
    if w4.size * w4.dtype.itemsize <= MM_W1:
        grid = (t // tm, 1, 1)
        a_spec = pl.BlockSpec((tm, N_CHIP * r), lambda i, j, k: (i, 0))
        b_spec = pl.BlockSpec((N_CHIP, r, c), lambda i, j, k: (0, 0, 0))
        o_spec = pl.BlockSpec((tm, c), lambda i, j, k: (i, 0))
        return _mm(name, a, w4, (t, c), out_dtype, grid, a_spec, b_spec, o_spec, NN, (tm, c))
    grid = (t // tm, 1, N_CHIP)
    a_spec = pl.BlockSpec((tm, r), lambda i, j, k: (i, k))
    b_spec = pl.BlockSpec((None, r, c), lambda i, j, k: (k, 0, 0))
    o_spec = pl.BlockSpec((tm, c), lambda i, j, k: (i, 0))
    return _mm(name, a, w4, (t, c), out_dtype, grid, a_spec, b_spec, o_spec, NN, (tm, c))


def mm_dx(name, dy, w4, mode, out_dtype):
    t = dy.shape[-2]
    _, r, c = w4.shape
    tm = min(t, MM_TM)
    if mode == "col":
        tn, npj = c, 1
        hb = N_CHIP // 2 * npj
        grid = (t // tm, 1, N_CHIP * npj)
        if dy.ndim == 3:
            a_spec = pl.BlockSpec((None, tm, tn), lambda i, j, k: (k // hb, i, k % hb))
        else:
            a_spec = pl.BlockSpec((tm, tn), lambda i, j, k: (i, k))
        b_spec = pl.BlockSpec((None, r, tn), lambda i, j, k: (k // npj, 0, k % npj))
        o_spec = pl.BlockSpec((tm, r), lambda i, j, k: (i, 0))
        return _mm(name, dy, w4, (t, r), out_dtype, grid, a_spec, b_spec, o_spec, NT, (tm, r))
    if N_CHIP * r <= MM_K1:
        grid = (t // tm, 1, 1)
        a_spec = pl.BlockSpec((tm, c), lambda i, j, k: (i, 0))
        b_spec = pl.BlockSpec((N_CHIP, r, c), lambda i, j, k: (0, 0, 0))
        o_spec = pl.BlockSpec((tm, N_CHIP * r), lambda i, j, k: (i, 0))
        return _mm(name, dy, w4, (t, N_CHIP * r), out_dtype, grid, a_spec, b_spec, o_spec, NT, (tm, N_CHIP * r))
    grid = (t // tm, N_CHIP, 1)
    a_spec = pl.BlockSpec((tm, c), lambda i, j, k: (i, 0))
    b_spec = pl.BlockSpec((None, r, c), lambda i, j, k: (j, 0, 0))
    o_spec = pl.BlockSpec((tm, r), lambda i, j, k: (i, j))
    return _mm(name, dy, w4, (t, N_CHIP * r), out_dtype, grid, a_spec, b_spec, o_spec, NT, (tm, r))


def mm_dw(name, a, dy, mode, shape3):
    t = a.shape[0]
    _, r, c = shape3
    if mode == "col":
        tn = _tile(c, MM_TN)
        npj = c // tn
        tt = _token_rows(t, r + tn)
        grid = (1, N_CHIP * npj, t // tt)
        a_spec = pl.BlockSpec((tt, r), lambda i, j, k: (k, 0))
        hb = N_CHIP // 2 * npj
        if dy.ndim == 3:
            b_spec = pl.BlockSpec((None, tt, tn), lambda i, j, k: (j // hb, k, j % hb))
        else:
            b_spec = pl.BlockSpec((tt, tn), lambda i, j, k: (k, j))
        o_spec = pl.BlockSpec((None, r, tn), lambda i, j, k: (j // npj, 0, j % npj))
        return _mm(name, a, dy, shape3, MXU_DTYPE, grid, a_spec, b_spec, o_spec, TN, (r, tn))
    if N_CHIP * r <= MM_K1:
        tt = _token_rows(t, N_CHIP * r + c)
        grid = (1, 1, t // tt)
        a_spec = pl.BlockSpec((tt, N_CHIP * r), lambda i, j, k: (k, 0))
        b_spec = pl.BlockSpec((tt, c), lambda i, j, k: (k, 0))
        o_spec = pl.BlockSpec((N_CHIP, r, c), lambda i, j, k: (0, 0, 0))
        return _mm(name, a, dy, shape3, MXU_DTYPE, grid, a_spec, b_spec, o_spec, TN, (N_CHIP * r, c))
    tt = _token_rows(t, r + c)
    grid = (N_CHIP, 1, t // tt)
    a_spec = pl.BlockSpec((tt, r), lambda i, j, k: (k, i))
    b_spec = pl.BlockSpec((tt, c), lambda i, j, k: (k, 0))
    o_spec = pl.BlockSpec((None, r, c), lambda i, j, k: (i, 0, 0))
    return _mm(name, a, dy, shape3, MXU_DTYPE, grid, a_spec, b_spec, o_spec, TN, (r, c))


def _rms_bwd(x, g, dh):
    r = lax.rsqrt(jnp.mean(x * x, axis=-1, keepdims=True) + NORM_EPS)
    xhat = x * r
    dyg = dh * g
    dx = r * (dyg - xhat * jnp.mean(dyg * xhat, axis=-1, keepdims=True))
    return dx, jnp.sum(dh * xhat, axis=0, keepdims=True)


def mm_dx_norms(name, dy, w4, x, g_pre, dres, prev, after):
    t = dy.shape[-2]
    _, r, c = w4.shape
    tm = min(t, MM_TM // 2)
    nt, nk = t // tm, N_CHIP
    hb = N_CHIP // 2
    chained = prev is not None

    def body(dy_ref, w_ref, x_ref, dres_ref, g_ref, *rest):
        rest = rest[1:] if after is not None else rest
        if chained:
            y_ref, gp_ref, dx_ref, dg_ref, dyp_ref, dgp_ref, acc = rest
        else:
            dx_ref, dg_ref, acc = rest
        i, k = pl.program_id(0), pl.program_id(1)
        part = lax.dot_general(dy_ref[...], w_ref[...], NT, preferred_element_type=F32)

        @pl.when(k == 0)
        def _():
            acc[...] = part

        @pl.when(k > 0)
        def _():
            acc[...] += part

        def add_to(ref, v):
            @pl.when(i == 0)
            def _():
                ref[...] = v

            @pl.when(i > 0)
            def _():
                ref[...] += v

        @pl.when(k == nk - 1)
        def _():
            dx, dg = _rms_bwd(x_ref[...], g_ref[...], acc[...])
            dxs = dres_ref[...] + dx
            dx_ref[...] = dxs
            add_to(dg_ref, dg)
            if chained:
                dyp, dgp = _rms_bwd(y_ref[...], gp_ref[...], dxs)
                dyp_ref[...] = (prev[2] * dyp).astype(dyp_ref.dtype)
                add_to(dgp_ref, prev[2] * dgp)

    if dy.ndim == 3:
        dy_spec = pl.BlockSpec((None, tm, c), lambda i, k: (k // hb, i, k % hb))
    else:
        dy_spec = pl.BlockSpec((tm, c), lambda i, k: (i, k))
    rows = pl.BlockSpec((tm, r), lambda i, k: (i, 0))
    gain = pl.BlockSpec((1, r), lambda i, k: (0, 0))
    in_specs = [dy_spec, pl.BlockSpec((None, r, c), lambda i, k: (k, 0, 0)), rows, rows, gain]
    args = [dy, w4, x, dres, g_pre]
    if after is not None:
        in_specs.append(pl.BlockSpec(memory_space=pl.ANY))
        args.append(after)
    out_specs = [rows, gain]
    out_shape = [jax.ShapeDtypeStruct((t, r), F32), jax.ShapeDtypeStruct((1, r), F32)]
    if chained:
        in_specs += [rows, gain]
        args += [prev[0], prev[1]]
        out_specs += [rows, gain]
        out_shape += [jax.ShapeDtypeStruct((t, r), MXU_DTYPE), jax.ShapeDtypeStruct((1, r), F32)]
    res = pl.pallas_call(
        body, name=name, grid=(nt, nk), in_specs=in_specs, out_specs=out_specs, out_shape=out_shape,
        scratch_shapes=[pltpu.VMEM((tm, r), F32)], compiler_params=_params(("arbitrary", "arbitrary")),
    )(*args)
    return tuple(res) if chained else (res[0], res[1], None, None)


def _rowwise(name, fn, rows, pars, outs, accs=(), tm=256, ncol=1):
    t = rows[0][0].shape[0]
    nrow, npar, nout = len(rows), len(pars), len(outs)

    def body(*refs):
        vals = [r[...] for r in refs[:nrow + npar]]
        res = fn(*vals)
        out_refs = refs[nrow + npar:nrow + npar + nout]
        acc_refs = refs[nrow + npar + nout:]
        for o, v in zip(out_refs, res[:nout]):
            o[...] = v.astype(o.dtype)
        i = pl.program_id(1)
        for a, v in zip(acc_refs, res[nout:]):
            @pl.when(i == 0)
            def _(a=a, v=v):
                a[...] = v.astype(F32)

            @pl.when(i > 0)
            def _(a=a, v=v):
                a[...] += v.astype(F32)

    in_specs = [pl.BlockSpec((tm, w), functools.partial(lambda j, i, b: (i, b + j), b=b)) for _, w, b in rows]
    for arr, w in pars:
        if w is None:
            in_specs.append(pl.BlockSpec(arr.shape, lambda j, i: (0, 0)))
        else:
            in_specs.append(pl.BlockSpec((1, w), lambda j, i: (0, j)))
    out_specs = [pl.BlockSpec((tm, w), lambda j, i: (i, j)) for _, w, _ in outs]
    out_specs += [pl.BlockSpec((1, w), lambda j, i: (0, j)) for _, w in accs]
    out_shape = [jax.ShapeDtypeStruct((t, tw), dt) for tw, _, dt in outs]
    out_shape += [jax.ShapeDtypeStruct((1, tw), F32) for tw, _ in accs]
    res = pl.pallas_call(
        body, name=name, grid=(ncol, t // tm), in_specs=in_specs, out_specs=out_specs, out_shape=out_shape,
        compiler_params=_params(("parallel", "arbitrary" if accs else "parallel")),
    )(*[r[0] for r in rows], *[p[0] for p in pars])
    return res


def _rms(x, g):
    xf = x.astype(F32)
    return xf * lax.rsqrt(jnp.mean(xf * xf, axis=-1, keepdims=True) + NORM_EPS) * g


def _silu(x):
    return x * jax.nn.sigmoid(x)


def rms_fwd(name, x, g):
    d = x.shape[1]
    return _rowwise(name, lambda x, g: (_rms(x, g),), [(x, d, 0)], [(g, None)], [(d, d, MXU_DTYPE)], tm=512)[0]


def rms_bwd(name, x, g, dh, dres):
    d = x.shape[1]

    def fn(x, dh, dres, g):
        _, vjp = jax.vjp(_rms, x, g)
        dx, dg = vjp(dh.astype(F32))
        return dres + dx, dg

    return _rowwise(name, fn, [(x, d, 0), (dh, d, 0), (dres, d, 0)], [(g, None)], [(d, d, F32)], [(d, d)], tm=256)


def mm_post(name, a, w4, x, g_post, scale, g_next):
    t = a.shape[0]
    _, r, c = w4.shape
    tm = min(t, MM_TM // 2)
    chained = g_next is not None

    def body(a_ref, w_ref, x_ref, gp_ref, *rest):
        gn_ref, y_ref, xn_ref, h_ref = rest if chained else (None,) + rest + (None,)
        y = lax.dot_general(a_ref[...], w_ref[...].reshape(N_CHIP * r, c), NN, preferred_element_type=F32)
        y_ref[...] = y
        xn = x_ref[...] + scale * _rms(y, gp_ref[...])
        xn_ref[...] = xn
        if chained:
            h_ref[...] = _rms(xn, gn_ref[...]).astype(h_ref.dtype)

    def rows(width):
        return pl.BlockSpec((tm, width), lambda i: (i, 0))

    gain = pl.BlockSpec((1, c), lambda i: (0, 0))
    in_specs = [rows(N_CHIP * r), pl.BlockSpec((N_CHIP, r, c), lambda i: (0, 0, 0)), rows(c), gain]
    args = [a, w4, x, g_post]
    out_specs, out_shape = [rows(c), rows(c)], [jax.ShapeDtypeStruct((t, c), F32)] * 2
    if chained:
        in_specs.append(gain)
        args.append(g_next)
        out_specs.append(rows(c))
        out_shape.append(jax.ShapeDtypeStruct((t, c), MXU_DTYPE))
    res = pl.pallas_call(
        body, name=name, grid=(t // tm,), in_specs=in_specs, out_specs=out_specs, out_shape=out_shape,
        compiler_params=_params(("parallel",)),
    )(*args)
    return res[0], res[1], (res[2] if chained else None)


def post_bwd(name, y, g, dx, scale):
    d = y.shape[1]

    def fn(y, dx, g):
        _, vjp = jax.vjp(lambda y, g: scale * _rms(y, g), y, g)
        return vjp(dx)

    return _rowwise(name, fn, [(y, d, 0), (dx, d, 0)], [(g, None)], [(d, d, MXU_DTYPE)], [(d, d)], tm=256)


def ffn_up(name, h, w4):
    t = h.shape[0]
    _, r, c = w4.shape
    tm = min(t, MM_TM)
    tn = _tile(c, MM_TM)
    npj = c // tn
    half = N_CHIP // 2

    def body(h_ref, wg_ref, wu_ref, gu_ref, a_ref):
        hv = h_ref[...]
        g = lax.dot_general(hv, wg_ref[...], NN, preferred_element_type=F32)
        u = lax.dot_general(hv, wu_ref[...], NN, preferred_element_type=F32)
        gu_ref[0] = g.astype(gu_ref.dtype)
        gu_ref[1] = u.astype(gu_ref.dtype)
        a_ref[...] = (_silu(g) * u).astype(a_ref.dtype)

    f = half * c
    return pl.pallas_call(
        body, name=name, grid=(t // tm, half * npj),
        in_specs=[pl.BlockSpec((tm, r), lambda i, j: (i, 0)),
                  pl.BlockSpec((None, r, tn), lambda i, j: (j // npj, 0, j % npj)),
                  pl.BlockSpec((None, r, tn), lambda i, j: (half + j // npj, 0, j % npj))],
        out_specs=[pl.BlockSpec((2, tm, tn), lambda i, j: (0, i, j)), pl.BlockSpec((tm, tn), lambda i, j: (i, j))],
        out_shape=[jax.ShapeDtypeStruct((2, t, f), MXU_DTYPE), jax.ShapeDtypeStruct((t, f), MXU_DTYPE)],
        compiler_params=_params(("parallel", "parallel")),
    )(h, w4, w4)


def ffn_down_dx(name, dy, w4, gu):
    t = dy.shape[0]
    _, r, c = w4.shape
    tm = min(t, MM_TM)

    def body(dy_ref, w_ref, gu_ref, o_ref):
        dyv = dy_ref[...]
        for n0 in range(0, r, MM_SLICE):
            cols = pl.ds(n0, MM_SLICE)
            da = lax.dot_general(dyv, w_ref[cols, :], NT, preferred_element_type=F32)
            gate, up = gu_ref[0, :, cols].astype(F32), gu_ref[1, :, cols].astype(F32)
            sg = jax.nn.sigmoid(gate)
            silu = gate * sg
            o_ref[0, :, cols] = (da * up * (sg + silu * (1.0 - sg))).astype(o_ref.dtype)
            o_ref[1, :, cols] = (da * silu).astype(o_ref.dtype)

    return pl.pallas_call(
        body, name=name, grid=(t // tm, N_CHIP),
        in_specs=[pl.BlockSpec((tm, c), lambda i, j: (i, 0)), pl.BlockSpec((None, r, c), lambda i, j: (j, 0, 0)),
                  pl.BlockSpec((2, tm, r), lambda i, j: (0, i, j))],
        out_specs=pl.BlockSpec((2, tm, r), lambda i, j: (0, i, j)),
        out_shape=jax.ShapeDtypeStruct((2, t, N_CHIP * r), MXU_DTYPE),
        compiler_params=_params(("parallel", "parallel")),
    )(dy, w4, gu)


def _head_gate(o, g):
    mu = jnp.mean(o, axis=-1, keepdims=True)
    var = jnp.mean(jnp.square(o - mu), axis=-1, keepdims=True)
    return _silu(g.astype(F32)) * ((o - mu) * lax.rsqrt(var + LN_EPS))


def head_gate_fwd(name, o, p, gate_blk):
    dv = RET_V_DIM
    return _rowwise(name, lambda o, g: (_head_gate(o, g),), [(o, dv, 0), (p, dv, gate_blk)], [],
                    [(RET_HEADS * dv, dv, MXU_DTYPE)], tm=min(o.shape[0], 2048), ncol=RET_HEADS)[0]


def head_gate_bwd(name, o, p, gate_blk, da):
    dv = RET_V_DIM

    def fn(o, g, da):
        _, vjp = jax.vjp(_head_gate, o, g.astype(F32))
        return vjp(da.astype(F32))

    w = RET_HEADS * dv
    return _rowwise(name, fn, [(o, dv, 0), (p, dv, gate_blk), (da, dv, 0)], [],
                    [(w, dv, MXU_DTYPE), (w, dv, MXU_DTYPE)], tm=min(o.shape[0], 2048), ncol=RET_HEADS)


def _ln_silu(u, g, b):
    mu = jnp.mean(u, axis=-1, keepdims=True)
    var = jnp.mean(jnp.square(u - mu), axis=-1, keepdims=True)
    return _silu((u - mu) * lax.rsqrt(var + LN_EPS) * g + b)


def ln_silu_fwd(name, u, g, b):
    d = u.shape[1]
    return _rowwise(name, lambda u, g, b: (_ln_silu(u, g, b),), [(u, d, 0)], [(g, None), (b, None)],
                    [(d, d, MXU_DTYPE)], tm=512)[0]


def ln_silu_bwd(name, u, g, b, dc):
    d = u.shape[1]

    def fn(u, dc, g, b):
        _, vjp = jax.vjp(_ln_silu, u, g, b)
        return vjp(dc.astype(F32))

    return _rowwise(name, fn, [(u, d, 0), (dc, d, 0)], [(g, None), (b, None)], [(d, d, F32)], [(d, d), (d, d)],
                    tm=256)


def _merge(g0, g1, g2, ya, yb, yc):
    s = jax.nn.sigmoid
    return s(g0.astype(F32)) * ya + s(g1.astype(F32)) * yb + s(g2.astype(F32)) * yc


def merge_fwd(name, p, blk, ya, yb, yc):
    d = ya.shape[1]
    rows = [(p, d, blk), (p, d, blk + 1), (p, d, blk + 2), (ya, d, 0), (yb, d, 0), (yc, d, 0)]
    return _rowwise(name, lambda *v: (_merge(*v),), rows, [], [(d, d, MXU_DTYPE)], tm=256)[0]


def merge_bwd(name, p, blk, ya, yb, yc, dmg):
    d = ya.shape[1]

    def fn(g0, g1, g2, ya, yb, yc, dmg):
        _, vjp = jax.vjp(_merge, g0.astype(F32), g1.astype(F32), g2.astype(F32), ya, yb, yc)
        return vjp(dmg.astype(F32))

    rows = [(p, d, blk), (p, d, blk + 1), (p, d, blk + 2), (ya, d, 0), (yb, d, 0), (yc, d, 0), (dmg, d, 0)]
    return _rowwise(name, fn, rows, [], [(d, d, MXU_DTYPE)] * 6, tm=256)


def concat_cols(name, pieces):
    t = pieces[0].shape[0]
    widths = [p.shape[1] for p in pieces]
    tm = 256

    def body(*refs):
        o_ref, at = refs[-1], 0
        for r, w in zip(refs[:-1], widths):
            o_ref[:, at:at + w] = r[...]
            at += w

    return pl.pallas_call(
        body, name=name, grid=(t // tm,),
        in_specs=[pl.BlockSpec((tm, w), lambda i: (i, 0)) for w in widths],
        out_specs=pl.BlockSpec((tm, sum(widths)), lambda i: (i, 0)),
        out_shape=jax.ShapeDtypeStruct((t, sum(widths)), pieces[0].dtype),
        compiler_params=_params(("parallel",)),
    )(*pieces)


def loss_head(name, y, target):
    t, d = y.shape
    tm = 512

    def body(y_ref, t_ref, dy_ref, loss_ref):
        err = y_ref[...] - t_ref[...]
        dy_ref[...] = err * (1.0 / d)
        part = jnp.sum(jnp.sum(err * err, axis=1, keepdims=True), axis=0, keepdims=True) * (0.5 / d)

        @pl.when(pl.program_id(0) == 0)
        def _():
            loss_ref[...] = part

        @pl.when(pl.program_id(0) > 0)
        def _():
            loss_ref[...] += part

    return pl.pallas_call(
        body, name=name, grid=(t // tm,),
        in_specs=[pl.BlockSpec((tm, d), lambda i: (i, 0))] * 2,
        out_specs=[pl.BlockSpec((tm, d), lambda i: (i, 0)), pl.BlockSpec((1, 1), lambda i: (0, 0))],
        out_shape=[jax.ShapeDtypeStruct((t, d), F32), jax.ShapeDtypeStruct((1, 1), F32)],
        compiler_params=_params(("arbitrary",)),
    )(y, target)


def _rot(x, cos2, sin2):
    return x * cos2 + pltpu.roll(x, RET_QK_DIM // 2, 1) * sin2


def _decay_mask(lg, n0, rows, cols):
    n = n0 + lax.broadcasted_iota(jnp.int32, (rows, cols), 0)
    m = lax.broadcasted_iota(jnp.int32, (rows, cols), 1)
    shift = CHUNK.bit_length() - 1
    dist = jnp.abs(n - m).astype(F32)
    return jnp.where((m >> shift) <= (n >> shift), jnp.exp(lg * dist), 0.0)


def _ret_specs(s):
    dk, dv, h = RET_QK_DIM, RET_V_DIM, RET_HEADS
    return [
        pl.BlockSpec((s, dk), lambda b, hh: (b, hh)),
        pl.BlockSpec((s, dk), lambda b, hh: (b, h + hh)),
        pl.BlockSpec((s, dv), lambda b, hh: (b, (2 * h * dk) // dv + hh)),
        pl.BlockSpec((s, dk), lambda b, hh: (b, 0)),
        pl.BlockSpec((s, dk), lambda b, hh: (b, 0)),
        pl.BlockSpec((None, 1, dk), lambda b, hh: (hh, 0, 0)),
    ]


def retention_fwd(name, p, cos2, sin2, log_g, nb, s):
    dk, dv, h = RET_QK_DIM, RET_V_DIM, RET_HEADS

    def body(q_ref, k_ref, v_ref, cos_ref, sin_ref, lg_ref, o_ref, kr_ref):
        lg = lg_ref[0:1, 0:1]
        kr = _rot(k_ref[...].astype(F32), cos_ref[...], sin_ref[...]) * (dk ** -0.5)
        kr_ref[...] = kr.astype(kr_ref.dtype)
        for qi in range(s // RET_TQ):
            n0, kmax = qi * RET_TQ, (qi + 1) * RET_TQ
            rows = pl.ds(n0, RET_TQ)
            qr = _rot(q_ref[rows, :].astype(F32), cos_ref[rows, :], sin_ref[rows, :]).astype(MXU_DTYPE)
            sc = lax.dot_general(qr, kr_ref[0:kmax, :], NT, preferred_element_type=F32)
            pm = (sc * _decay_mask(lg, n0, RET_TQ, kmax)).astype(MXU_DTYPE)
            o_ref[rows, :] = lax.dot_general(pm, v_ref[0:kmax, :], NN, preferred_element_type=F32)

    return pl.pallas_call(
        body, name=name, grid=(nb, h), in_specs=_ret_specs(s),
        out_specs=pl.BlockSpec((s, dv), lambda b, hh: (b, hh)),
        out_shape=jax.ShapeDtypeStruct((nb * s, h * dv), F32),
        scratch_shapes=[pltpu.VMEM((s, dk), MXU_DTYPE)],
        compiler_params=_params(("parallel", "parallel")),
    )(p, p, p, cos2, sin2, log_g)


def retention_bwd(name, p, cos2, sin2, log_g, do, nb, s):
    dk, dv, h = RET_QK_DIM, RET_V_DIM, RET_HEADS

    def body(q_ref, k_ref, v_ref, cos_ref, sin_ref, lg_ref, do_ref, dq_ref, dk_ref, dv_ref, kr_ref, dk_acc, dv_acc):
        lg = lg_ref[0:1, 0:1]
        kr = _rot(k_ref[...].astype(F32), cos_ref[...], sin_ref[...]) * (dk ** -0.5)
        kr_ref[...] = kr.astype(kr_ref.dtype)
        dk_acc[...] = jnp.zeros_like(dk_acc)
        dv_acc[...] = jnp.zeros_like(dv_acc)
        for qi in range(s // RET_TQ):
            n0, kmax = qi * RET_TQ, (qi + 1) * RET_TQ
            rows = pl.ds(n0, RET_TQ)
            cq, sq = cos_ref[rows, :], sin_ref[rows, :]
            qr = _rot(q_ref[rows, :].astype(F32), cq, sq).astype(MXU_DTYPE)
            dob = do_ref[rows, :]
            mask = _decay_mask(lg, n0, RET_TQ, kmax)
            sc = lax.dot_general(qr, kr_ref[0:kmax, :], NT, preferred_element_type=F32)
            pm = (sc * mask).astype(MXU_DTYPE)
            dv_acc[0:kmax, :] += lax.dot_general(pm, dob, TN, preferred_element_type=F32)
            dp = lax.dot_general(dob, v_ref[0:kmax, :], NT, preferred_element_type=F32)
            ds = (dp * mask).astype(MXU_DTYPE)
            dqr = lax.dot_general(ds, kr_ref[0:kmax, :], NN, preferred_element_type=F32)
            dq_ref[rows, :] = _rot(dqr, cq, -sq).astype(dq_ref.dtype)
            dk_acc[0:kmax, :] += lax.dot_general(ds, qr, TN, preferred_element_type=F32)
        dkr = dk_acc[...] * (dk ** -0.5)
        dk_ref[...] = _rot(dkr, cos_ref[...], -sin_ref[...]).astype(dk_ref.dtype)
        dv_ref[...] = dv_acc[...].astype(dv_ref.dtype)

    t = nb * s
    return pl.pallas_call(
        body, name=name, grid=(nb, h),
        in_specs=_ret_specs(s) + [pl.BlockSpec((s, dv), lambda b, hh: (b, hh))],
        out_specs=[pl.BlockSpec((s, dk), lambda b, hh: (b, hh)), pl.BlockSpec((s, dk), lambda b, hh: (b, hh)),
                   pl.BlockSpec((s, dv), lambda b, hh: (b, hh))],
        out_shape=[jax.ShapeDtypeStruct((t, h * dk), MXU_DTYPE), jax.ShapeDtypeStruct((t, h * dk), MXU_DTYPE),
                   jax.ShapeDtypeStruct((t, h * dv), MXU_DTYPE)],
        scratch_shapes=[pltpu.VMEM((s, dk), MXU_DTYPE), pltpu.VMEM((s, dk), F32), pltpu.VMEM((s, dv), F32)],
        compiler_params=_params(("parallel", "parallel")),
    )(p, p, p, cos2, sin2, log_g, do)


def _conv_grid(t, d, nb, ts):
    s = t // nb
    ns, nc = s // ts, d // CONV_TC
    return s, ns, nc


def _shifted(pad_ref, sh_ref, offsets):
    n = sh_ref.shape[1]
    for b in sorted({off % SUBLANES for off in offsets} - {0}):
        sh_ref[b - 1] = pad_ref[pl.ds(b, n), :]

    def read(off, r0):
        a, b = off - off % SUBLANES + r0, off % SUBLANES
        return pad_ref[pl.ds(a, SUBLANES), :] if b == 0 else sh_ref[b - 1, pl.ds(a, SUBLANES), :]

    return read


def _causal_taps(pad_ref, sh_ref, w_ref, k, emit):
    offs = [CONV_PAD - (k - 1) + j for j in range(k)]
    read = _shifted(pad_ref, sh_ref, offs)
    for r0 in range(0, pad_ref.shape[0] - CONV_PAD, CONV_ROWS):
        accs = [None] * len(CONV_TILES)
        for j in range(k):
            wj = w_ref[j]
            for q, dr in enumerate(CONV_TILES):
                term = wj * read(offs[j], r0 + dr)
                accs[q] = term if accs[q] is None else accs[q] + term
        emit(r0, jnp.concatenate(accs, axis=0))


def _tap_tiles(w):
    return jnp.broadcast_to(w[:, None, :], (w.shape[0], SUBLANES, w.shape[1]))


def _tap_spec(k):
    return pl.BlockSpec((k, SUBLANES, CONV_TC), lambda c, b, si: (0, 0, c))


def _carry_past(pad_ref, s_idx):
    ts = pad_ref.shape[0] - CONV_PAD

    @pl.when(s_idx == 0)
    def _():
        pad_ref[0:CONV_PAD, :] = jnp.zeros((CONV_PAD, pad_ref.shape[1]), F32)

    @pl.when(s_idx > 0)
    def _():
        pad_ref[0:CONV_PAD, :] = pad_ref[ts:ts + CONV_PAD, :]


def _carry_future(pad_ref, s_idx):
    ts = pad_ref.shape[0] - CONV_PAD

    @pl.when(s_idx == 0)
    def _():
        pad_ref[ts:ts + CONV_PAD, :] = jnp.zeros((CONV_PAD, pad_ref.shape[1]), F32)

    @pl.when(s_idx > 0)
    def _():
        pad_ref[ts:ts + CONV_PAD, :] = pad_ref[0:CONV_PAD, :]


def _conv_bwd_taps(pad_ref, sh_ref, w_ref, dw_acc, k, x_rows, emit, mix):
    read = _shifted(pad_ref, sh_ref, range(k))
    for r0 in range(0, pad_ref.shape[0] - CONV_PAD, CONV_ROWS):
        ops = x_rows(r0)
        x = mix(ops)
        accs = [None] * len(CONV_TILES)
        for j in range(k):
            wj, dwj = w_ref[j], None
            for q, dr in enumerate(CONV_TILES):
                sh = read(k - 1 - j, r0 + dr)
                term = wj * sh
                accs[q] = term if accs[q] is None else accs[q] + term
                prod = x[dr:dr + SUBLANES] * sh
                dwj = prod if dwj is None else dwj + prod
            dw_acc[j] += dwj
        emit(r0, ops, jnp.concatenate(accs, axis=0))


def _conv_bwd_edges(dw_acc, dw_ref, nb, ns, extra=()):
    first = jnp.logical_and(pl.program_id(1) == 0, pl.program_id(2) == 0)
    last = jnp.logical_and(pl.program_id(1) == nb - 1, pl.program_id(2) == ns - 1)

    @pl.when(first)
    def _():
        dw_acc[...] = jnp.zeros_like(dw_acc)
        for r in extra:
            r[...] = jnp.zeros_like(r)

    def finish():
        @pl.when(last)
        def _():
            dw_ref[...] = jnp.sum(dw_acc[...], axis=1)

    return finish


def short_conv_fwd(name, p, blk_b, w, nb):
    t = p.shape[0]
    d = w.shape[1]
    ts = SC_TS
    s, ns, nc = _conv_grid(t, d, nb, ts)
    cb = d // CONV_TC

    def body(b_ref, c_ref, x_ref, w_ref, y_ref, cz_ref, pad_ref, sh_ref):
        _carry_past(pad_ref, pl.program_id(2))
        pad_ref[CONV_PAD:CONV_PAD + ts, :] = c_ref[...].astype(F32) * x_ref[...].astype(F32)

        def emit(r0, cz):
            rows = pl.ds(r0, CONV_ROWS)
            cz_ref[rows, :] = cz
            y_ref[rows, :] = (b_ref[rows, :].astype(F32) * cz).astype(y_ref.dtype)

        _causal_taps(pad_ref, sh_ref, w_ref, SC_KERNEL, emit)

    def pspec(off):
        return pl.BlockSpec((ts, CONV_TC), lambda c, b, si: (b * ns + si, (blk_b + off) * cb + c))

    ospec = pl.BlockSpec((ts, CONV_TC), lambda c, b, si: (b * ns + si, c))
    return pl.pallas_call(
        body, name=name, grid=(nc, nb, ns),
        in_specs=[pspec(0), pspec(1), pspec(2), _tap_spec(SC_KERNEL)],
        out_specs=[ospec, ospec],
        out_shape=[jax.ShapeDtypeStruct((t, d), MXU_DTYPE), jax.ShapeDtypeStruct((t, d), F32)],
        scratch_shapes=_conv_scratch(ts),
        compiler_params=_params(("parallel", "arbitrary", "arbitrary")),
    )(p, p, p, _tap_tiles(w))


def short_conv_bwd(name, p, blk_b, w, cz, dy, nb):
    t = p.shape[0]
    d = w.shape[1]
    ts = SC_TS
    s, ns, nc = _conv_grid(t, d, nb, ts)
    cb = d // CONV_TC

    def body(b_ref, c_ref, x_ref, w_ref, cz_ref, dy_ref, db_ref, dc_ref, dx_ref, dw_ref, pad_ref, sh_ref, dw_acc):
        _carry_future(pad_ref, pl.program_id(2))
        dyv = dy_ref[...].astype(F32)
        db_ref[...] = (dyv * cz_ref[...]).astype(db_ref.dtype)
        pad_ref[0:ts, :] = dyv * b_ref[...].astype(F32)
        finish = _conv_bwd_edges(dw_acc, dw_ref, nb, ns)

        def x_rows(r0):
            rows = pl.ds(r0, CONV_ROWS)
            return c_ref[rows, :].astype(F32), x_ref[rows, :].astype(F32)

        def emit(r0, cx, dz):
            rows = pl.ds(r0, CONV_ROWS)
            dc_ref[rows, :] = (dz * cx[1]).astype(dc_ref.dtype)
            dx_ref[rows, :] = (dz * cx[0]).astype(dx_ref.dtype)

        _conv_bwd_taps(pad_ref, sh_ref, w_ref, dw_acc, SC_KERNEL, x_rows, emit, lambda cx: cx[0] * cx[1])
        finish()

    def row(b, si):
        return b * ns + (ns - 1 - si)

    def pspec(off):
        return pl.BlockSpec((ts, CONV_TC), lambda c, b, si: (row(b, si), (blk_b + off) * cb + c))

    ospec = pl.BlockSpec((ts, CONV_TC), lambda c, b, si: (row(b, si), c))
    wspec = pl.BlockSpec((SC_KERNEL, CONV_TC), lambda c, b, si: (0, c))
    return pl.pallas_call(
        body, name=name, grid=(nc, nb, ns),
        in_specs=[pspec(0), pspec(1), pspec(2), _tap_spec(SC_KERNEL), ospec, ospec],
        out_specs=[ospec, ospec, ospec, wspec],
        out_shape=[jax.ShapeDtypeStruct((t, d), MXU_DTYPE)] * 3 + [jax.ShapeDtypeStruct((SC_KERNEL, d), F32)],
        scratch_shapes=_conv_scratch(ts) + [pltpu.VMEM((SC_KERNEL, SUBLANES, CONV_TC), F32)],
        compiler_params=_params(("parallel", "arbitrary", "arbitrary")),
    )(p, p, p, _tap_tiles(w), cz, dy)


def conformer_conv_fwd(name, p, blk_a, w, bias, nb):
    t = p.shape[0]
    d = w.shape[1]
    ts = CONV_TS
    s, ns, nc = _conv_grid(t, d, nb, ts)
    cb = d // CONV_TC

    def body(a_ref, b_ref, w_ref, bias_ref, u_ref, pad_ref, sh_ref):
        _carry_past(pad_ref, pl.program_id(2))
        pad_ref[CONV_PAD:CONV_PAD + ts, :] = a_ref[...].astype(F32) * jax.nn.sigmoid(b_ref[...].astype(F32))

        def emit(r0, u):
            u_ref[pl.ds(r0, CONV_ROWS), :] = u + bias_ref[0:1, :]

        _causal_taps(pad_ref, sh_ref, w_ref, CF_KERNEL, emit)

    def pspec(off):
        return pl.BlockSpec((ts, CONV_TC), lambda c, b, si: (b * ns + si, (blk_a + off) * cb + c))

    return pl.pallas_call(
        body, name=name, grid=(nc, nb, ns),
        in_specs=[pspec(0), pspec(1), _tap_spec(CF_KERNEL), pl.BlockSpec((SUBLANES, CONV_TC), lambda c, b, si: (0, c))],
        out_specs=pl.BlockSpec((ts, CONV_TC), lambda c, b, si: (b * ns + si, c)),
        out_shape=jax.ShapeDtypeStruct((t, d), F32),
        scratch_shapes=_conv_scratch(ts),
        compiler_params=_params(("parallel", "arbitrary", "arbitrary")),
    )(p, p, _tap_tiles(w), jnp.broadcast_to(bias, (SUBLANES, d)))


def conformer_conv_bwd(name, p, blk_a, w, du, nb):
    t = p.shape[0]
    d = w.shape[1]
    ts = CONV_TS
    s, ns, nc = _conv_grid(t, d, nb, ts)
    cb = d // CONV_TC

    def body(a_ref, b_ref, w_ref, du_ref, da_ref, db_ref, dw_ref, dbias_ref, pad_ref, sh_ref, dw_acc):
        _carry_future(pad_ref, pl.program_id(2))
        duv = du_ref[...]
        pad_ref[0:ts, :] = duv
        finish = _conv_bwd_edges(dw_acc, dw_ref, nb, ns, extra=(dbias_ref,))
        dbias_ref[...] += jnp.sum(duv, axis=0, keepdims=True)

        def x_rows(r0):
            rows = pl.ds(r0, CONV_ROWS)
            return a_ref[rows, :].astype(F32), jax.nn.sigmoid(b_ref[rows, :].astype(F32))

        def emit(r0, asg, du0):
            rows = pl.ds(r0, CONV_ROWS)
            av, sg = asg
            da_ref[rows, :] = (du0 * sg).astype(da_ref.dtype)
            db_ref[rows, :] = (du0 * av * sg * (1.0 - sg)).astype(db_ref.dtype)

        _conv_bwd_taps(pad_ref, sh_ref, w_ref, dw_acc, CF_KERNEL, x_rows, emit, lambda asg: asg[0] * asg[1])
        finish()

    def row(b, si):
        return b * ns + (ns - 1 - si)

    def pspec(off):
        return pl.BlockSpec((ts, CONV_TC), lambda c, b, si: (row(b, si), (blk_a + off) * cb + c))

    ospec = pl.BlockSpec((ts, CONV_TC), lambda c, b, si: (row(b, si), c))
    wspec = pl.BlockSpec((CF_KERNEL, CONV_TC), lambda c, b, si: (0, c))
    bspec = pl.BlockSpec((1, CONV_TC), lambda c, b, si: (0, c))
    return pl.pallas_call(
        body, name=name, grid=(nc, nb, ns),
        in_specs=[pspec(0), pspec(1), _tap_spec(CF_KERNEL), ospec],
        out_specs=[ospec, ospec, wspec, bspec],
        out_shape=[jax.ShapeDtypeStruct((t, d), MXU_DTYPE)] * 2
        + [jax.ShapeDtypeStruct((CF_KERNEL, d), F32), jax.ShapeDtypeStruct((1, d), F32)],
        scratch_shapes=_conv_scratch(ts) + [pltpu.VMEM((CF_KERNEL, SUBLANES, CONV_TC), F32)],
        compiler_params=_params(("parallel", "arbitrary", "arbitrary")),
    )(p, p, _tap_tiles(w), du)


BLOCKS = ("ffn1", "mixer", "ffn2")
BLOCK_WEIGHTS = {"ffn1": ("ffn1_w_gu", "ffn1_w_down"), "mixer": ("w_in", "w_ret_o", "w_sc_o", "w_cf_o", "w_o"),
                 "ffn2": ("ffn2_w_gu", "ffn2_w_down")}
BIG = BLOCK_WEIGHTS["ffn1"] + BLOCK_WEIGHTS["mixer"] + BLOCK_WEIGHTS["ffn2"]
MODE = {"ffn1_w_gu": "col", "ffn1_w_down": "row", "w_in": "col", "w_ret_o": "row", "w_sc_o": "row",
        "w_cf_o": "row", "w_o": "row", "ffn2_w_gu": "col", "ffn2_w_down": "row"}
NORM_OF = {"ffn1": 0, "mixer": 2, "ffn2": 4}
BLK_GATE, BLK_SCB, BLK_CFA, BLK_MERGE = 2, 3, 6, 8


def _rope_tables(positions):
    half = RET_QK_DIM // 2
    inv_freq = ROPE_BASE ** (-jnp.arange(half, dtype=F32) / half)
    ang = positions.astype(F32)[..., None] * inv_freq
    cos, sin = jnp.cos(ang), jnp.sin(ang)
    nb, s = positions.shape
    cos2 = jnp.concatenate([cos, cos], axis=-1).reshape(nb * s, RET_QK_DIM)
    sin2 = jnp.concatenate([-sin, sin], axis=-1).reshape(nb * s, RET_QK_DIM)
    return cos2, sin2


def _log_gamma():
    lg = jnp.log(1.0 - 2.0 ** (-5.0 - jnp.arange(RET_HEADS, dtype=F32)))
    return jnp.broadcast_to(lg[:, None, None], (RET_HEADS, 1, RET_QK_DIM))


def _ffn_fwd(xs, h, w, tag, g_post, g_next):
    gu, a = ffn_up("ffn_up", h, w[tag + "_w_gu"])
    y, out, h_next = mm_post("ffn_down", a, w[tag + "_w_down"], xs, g_post, 0.5, g_next)
    return out, h_next, dict(x=xs, h=h, gu=gu, a=a, y=y, w=w)


def _ffn_bwd(dxs, dy, sv, tag, g_pre, push, prev):
    w = sv["w"]
    gu_w, down_w = w[tag + "_w_gu"], w[tag + "_w_down"]
    dgu = ffn_down_dx("ffn_down_dx", dy, down_w, sv["gu"])
    grads = {tag + "_w_down": mm_dw("ffn_down_dw", sv["a"], dy, "row", down_w.shape),
             tag + "_w_gu": mm_dw("ffn_gu_dw", sv["h"], dgu, "col", gu_w.shape)}
    return mm_dx_norms("ffn_gu_dx", dgu, gu_w, sv["x"], g_pre, dxs, prev, push(grads))


def _mixer_fwd(xs, h, w, sm, g_post, g_next, rope, nb, s, mid):
    cos2, sin2, log_g = rope
    d = xs.shape[1]
    gate_blk = (BLK_GATE * d) // RET_V_DIM
    p = mm_fwd("mx_in", h, w["w_in"], "col", MXU_DTYPE)
    if mid is not None:
        sm = dict(sm, cf_dw_b=sm["cf_dw_b"] + mid(p))
    o = retention_fwd("ret_fwd", p, cos2, sin2, log_g, nb, s)
    ya_in = head_gate_fwd("ret_gate", o, p, gate_blk)
    yb_in, cz = short_conv_fwd("sc_fwd", p, BLK_SCB, sm["sc_conv_w"], nb)
    u1 = conformer_conv_fwd("cf_fwd", p, BLK_CFA, sm["cf_dw_w"], sm["cf_dw_b"], nb)
    yc_in = ln_silu_fwd("cf_ln", u1, sm["cf_ln_g"], sm["cf_ln_b"])
    ya = mm_fwd("mx_proj", ya_in, w["w_ret_o"], "row", F32)
    yb = mm_fwd("mx_proj", yb_in, w["w_sc_o"], "row", F32)
    yc = mm_fwd("mx_proj", yc_in, w["w_cf_o"], "row", F32)
    mg = merge_fwd("mx_merge", p, BLK_MERGE, ya, yb, yc)
    m, out, h_next = mm_post("mx_out", mg, w["w_o"], xs, g_post, 1.0, g_next)
    return out, h_next, dict(x=xs, h=h, p=p, o=o, ya_in=ya_in, yb_in=yb_in, cz=cz, u1=u1, yc_in=yc_in, ya=ya, yb=yb, yc=yc,
                     mg=mg, m=m, w=w)


def _mixer_bwd(dxs, dm, sv, sm, g_pre, rope, nb, s, push, prev):
    cos2, sin2, log_g = rope
    w, p = sv["w"], sv["p"]
    d = dxs.shape[1]
    gate_blk = (BLK_GATE * d) // RET_V_DIM
    grads, gsm = {}, {}

    def proj_bwd(wname, a_in, dy, out_dtype):
        grads[wname] = mm_dw("mx_proj_dw", a_in, dy, "row", w[wname].shape)
        return mm_dx("mx_proj_dx", dy, w[wname], "row", out_dtype)

    dmg = proj_bwd("w_o", sv["mg"], dm, MXU_DTYPE)
    dg0, dg1, dg2, dya, dyb, dyc = merge_bwd("mx_merge_bwd", p, BLK_MERGE, sv["ya"], sv["yb"], sv["yc"], dmg)
    dya_in = proj_bwd("w_ret_o", sv["ya_in"], dya, MXU_DTYPE)
    dyb_in = proj_bwd("w_sc_o", sv["yb_in"], dyb, MXU_DTYPE)
    dyc_in = proj_bwd("w_cf_o", sv["yc_in"], dyc, MXU_DTYPE)
    do, dgret = head_gate_bwd("ret_gate_bwd", sv["o"], p, gate_blk, dya_in)
    dq, dk, dv = retention_bwd("ret_bwd", p, cos2, sin2, log_g, do, nb, s)
    dscb, dscc, dscx, gsm["sc_conv_w"] = short_conv_bwd("sc_bwd", p, BLK_SCB, sm["sc_conv_w"], sv["cz"], dyb_in, nb)
    du1, dlg, dlb = ln_silu_bwd("cf_ln_bwd", sv["u1"], sm["cf_ln_g"], sm["cf_ln_b"], dyc_in)
    dcfa, dcfb, gsm["cf_dw_w"], dbias = conformer_conv_bwd("cf_bwd", p, BLK_CFA, sm["cf_dw_w"], du1, nb)
    gsm.update(cf_ln_g=dlg[0], cf_ln_b=dlb[0], cf_dw_b=dbias[0])
    dp = concat_cols("mx_dp", [dq, dk, dv, dgret, dscb, dscc, dscx, dcfa, dcfb, dg0, dg1, dg2])
    grads["w_in"] = mm_dw("mx_in_dw", sv["h"], dp, "col", w["w_in"].shape)
    return mm_dx_norms("mx_in_dx", dp, w["w_in"], sv["x"], g_pre, dxs, prev, push(grads)) + (gsm,)


def local_step(x, positions, target, small, fetch, push):
    nb, s, d = x.shape
    t = nb * s
    depth = small["norm_g"].shape[0]
    rope = _rope_tables(positions) + (_log_gamma(),)
    xs = x.reshape(t, d)
    token = [None]

    def gain(l, i):
        g = small["norm_g"][l, i][None, :]
        if token[0] is not None:
            g, token[0] = g + token[0], None
        return g

    def mixer_small(l):
        return dict(sc_conv_w=small["sc_conv_w"][l], cf_dw_w=small["cf_dw_w"][l], cf_dw_b=small["cf_dw_b"][l][None, :],
                    cf_ln_g=small["cf_ln_g"][l][None, :], cf_ln_b=small["cf_ln_b"][l][None, :])

    saved = {}
    order = [(l, blk) for l in range(depth) for blk in BLOCKS]
    h = None
    for at, (l, blk) in enumerate(order):
        w, token[0], mid = fetch(l, blk, xs)
        i0 = NORM_OF[blk]
        if h is None:
            h = rms_fwd("first_rms", xs, gain(l, i0))
        g_post = gain(l, i0 + 1)
        g_next = gain(order[at + 1][0], NORM_OF[order[at + 1][1]]) if at + 1 < len(order) else None
        if blk == "mixer":
            xs, h, saved[l, blk] = _mixer_fwd(xs, h, w, mixer_small(l), g_post, g_next, rope, nb, s, mid)
        else:
            xs, h, saved[l, blk] = _ffn_fwd(xs, h, w, blk, g_post, g_next)

    dxs, loss = loss_head("loss", xs, target.reshape(t, d))

    dnorm = [[None] * 6 for _ in range(depth)]
    gsmall = {n: [None] * depth for n in ("sc_conv_w", "cf_dw_w", "cf_dw_b", "cf_ln_g", "cf_ln_b")}
    def branch(group):
        l, blk = group
        sv = saved[group]
        return (sv["m"], gain(l, NORM_OF[blk] + 1), 1.0) if blk == "mixer" else (sv["y"], gain(l, NORM_OF[blk] + 1), 0.5)

    l, blk = order[-1]
    y, g_post, scale = branch(order[-1])
    dy, dnorm[l][NORM_OF[blk] + 1] = post_bwd("last_post_bwd", y, g_post, dxs, scale)
    for at in reversed(range(len(order))):
        l, blk = order[at]
        i0 = NORM_OF[blk]
        prev = branch(order[at - 1]) if at > 0 else None
        put = functools.partial(push, l, blk)
        if blk == "mixer":
            dxs, dnorm[l][i0], dy, dg_prev, gsm = _mixer_bwd(
                dxs, dy, saved[l, blk], mixer_small(l), gain(l, i0), rope, nb, s, put, prev)
            for n, v in gsm.items():
                gsmall[n][l] = v
        else:
            dxs, dnorm[l][i0], dy, dg_prev = _ffn_bwd(dxs, dy, saved[l, blk], blk, gain(l, i0), put, prev)
        if at > 0:
            dnorm[order[at - 1][0]][NORM_OF[order[at - 1][1]] + 1] = dg_prev

    gs = {n: jnp.stack(v) for n, v in gsmall.items()}
    gs["norm_g"] = jnp.stack([jnp.concatenate(r, axis=0) for r in dnorm])
    return loss, dxs.reshape(nb, s, d), gs


ANY = pl.BlockSpec(memory_space=pl.ANY)
HBM = pl.BlockSpec(memory_space=pltpu.HBM)
SEM = pl.BlockSpec(memory_space=pltpu.SEMAPHORE)
VMEM_WHOLE = pl.BlockSpec(memory_space=pltpu.VMEM)
EFFECT = pltpu.SideEffectType.DATAFLOW_SIDE_EFFECTING
TOKEN = jax.ShapeDtypeStruct((8, 128), F32)


def _other_chips(x, y):
    return [(1 - x, y), (x, 1 - y), (1 - x, 1 - y)]


def _remote(src, dst, send_sem, recv_sem, to):
    return pltpu.make_async_remote_copy(src_ref=src, dst_ref=dst, send_sem=send_sem, recv_sem=recv_sem,
                                        device_id=to, device_id_type=MESH)


def _in_hbm(v):
    return pltpu.with_memory_space_constraint(v, pltpu.HBM)


def place_quarters(ws, layer, ids, after):
    m = len(ws)

    def body(ids_ref, *refs):
        for w_ref, o_ref in zip(refs[:m], refs[m + 1:]):
            o_ref[...] = w_ref[...].astype(o_ref.dtype)

    def spec(w, where):
        return pl.BlockSpec((None, w.shape[1] // STREAM_STEPS, w.shape[2]), where)

    return pl.pallas_call(
        body, name="place_quarters",
        grid_spec=pltpu.PrefetchScalarGridSpec(
            num_scalar_prefetch=1, grid=(STREAM_STEPS,),
            in_specs=[spec(w, lambda i, ids_ref: (layer, i, 0)) for w in ws] + [ANY],
            out_specs=[spec(w, lambda i, ids_ref: (ids_ref[0], i, 0)) for w in ws]),
        out_shape=[jax.ShapeDtypeStruct((N_CHIP,) + w.shape[1:], MXU_DTYPE) for w in ws],
        compiler_params=_params(("parallel",)),
    )(ids, *ws, after)


def _gather_copies(lands, send, recv):
    x, y, c = _axes()
    me = 2 * x + y
    mine, theirs = [], []
    for a, ld in enumerate(lands):
        rh = ld.shape[1] // 2
        rows = pl.ds(c * rh, rh)
        for k, (px, py) in enumerate(_other_chips(x, y)):
            to = (px, py, c)
            mine.append(_remote(ld.at[me, rows, :], ld.at[me, rows, :], send.at[3 * a + k], recv.at[3 * a + k], to))
            got = ld.at[2 * px + py, rows, :]
            theirs.append(_remote(got, got, send.at[3 * a + k], recv.at[3 * a + k], to))
    return mine, theirs


def gather_start(name, groups, after):
    flat = [s for g in groups for s in g]
    n, ng = len(flat), len(groups)
    sizes = [len(g) for g in groups]

    def body(*refs):
        lands = refs[:n]
        sems = refs[n + 1:n + 1 + 2 * ng]
        token = refs[-1]
        at = 0
        for g, m in enumerate(sizes):
            mine, _ = _gather_copies(lands[at:at + m], sems[2 * g], sems[2 * g + 1])
            for cp in mine:
                cp.start()
            at += m
        token[...] = jnp.zeros_like(token)

    sem_shapes = []
    for m in sizes:
        sem_shapes += [pltpu.SemaphoreType.DMA((3 * m,))] * 2
    res = pl.pallas_call(
        body, name=name, in_specs=[HBM] * n + [ANY],
        out_specs=[SEM] * (2 * ng) + [HBM] * n + [VMEM_WHOLE],
        out_shape=sem_shapes + [pltpu.HBM(s.shape, s.dtype) for s in flat] + [TOKEN],
        input_output_aliases={i: 2 * ng + i for i in range(n)},
        compiler_params=pltpu.CompilerParams(has_side_effects=EFFECT),
    )(*[_in_hbm(s) for s in flat], after)
    sems, thru, token = res[:2 * ng], res[2 * ng:2 * ng + n], res[-1]
    out, at = [], 0
    for g, m in enumerate(sizes):
        out.append((sems[2 * g], sems[2 * g + 1], thru[at:at + m]))
        at += m
    return out, token


def gather_wait(lands, send, recv, after):
    m = len(lands)

    def body(*refs):
        mine, theirs = _gather_copies(refs[:m], refs[m], refs[m + 1])
        for cp in mine:
            cp.wait_send()
        for cp in theirs:
            cp.wait_recv()

    return pl.pallas_call(
        body, name="gather_wait", in_specs=[HBM] * m + [SEM, SEM, ANY], out_specs=[HBM] * m,
        out_shape=[pltpu.HBM(l.shape, l.dtype) for l in lands],
        input_output_aliases={i: i for i in range(m)},
        compiler_params=pltpu.CompilerParams(has_side_effects=EFFECT),
    )(*lands, send, recv, after)


def copy_start(name, families, after=()):
    sizes = [len(f[0]) for f in families]
    n, k, nf = sum(sizes), len(after), len(families)

    def body(*refs):
        at = 0
        for f, (_, copies, _) in enumerate(families):
            for cp in copies(refs[at:at + sizes[f]], refs[n + k + 2 * f], refs[n + k + 2 * f + 1])[0]:
                cp.start()
            at += sizes[f]
        refs[-1][...] = jnp.zeros_like(refs[-1])

    flat = [b for f in families for b in f[0]]
    sems = [pltpu.SemaphoreType.DMA((f[2],)) for f in families for _ in range(2)]
    res = pl.pallas_call(
        body, name=name, in_specs=[HBM] * n + [ANY] * k, out_specs=[SEM] * (2 * nf) + [HBM] * n + [VMEM_WHOLE],
        out_shape=sems + [pltpu.HBM(b.shape, b.dtype) for b in flat] + [TOKEN],
        input_output_aliases={i: 2 * nf + i for i in range(n)},
        compiler_params=pltpu.CompilerParams(has_side_effects=EFFECT),
    )(*[_in_hbm(b) for b in flat], *after)
    out, at = [], 2 * nf
    for f in range(nf):
        out.append((res[2 * f], res[2 * f + 1], list(res[at:at + sizes[f]])))
        at += sizes[f]
    return out, res[-1]


def copy_wait(name, bufs, send, recv, copies, after=()):
    n = len(bufs)

    def body(*refs):
        mine, theirs = copies(refs[:n], refs[n], refs[n + 1])
        for cp in mine:
            cp.wait_send()
        for cp in theirs:
            cp.wait_recv()

    return list(pl.pallas_call(
        body, name=name, in_specs=[HBM] * n + [SEM, SEM] + [ANY] * len(after), out_specs=[HBM] * n,
        out_shape=[pltpu.HBM(b.shape, b.dtype) for b in bufs], input_output_aliases={i: i for i in range(n)},
        compiler_params=pltpu.CompilerParams(has_side_effects=EFFECT),
    )(*bufs, send, recv, *after))


def _fill_copies(lands, send, recv):
    x, y, c = _axes()
    sib = (x, y, 1 - c)
    mine, theirs = [], []
    for a, ld in enumerate(lands):
        rh = ld.shape[1] // 2
        for k, (px, py) in enumerate(_other_chips(x, y)):
            got = ld.at[2 * px + py, pl.ds(c * rh, rh), :]
            mine.append(_remote(got, got, send.at[3 * a + k], recv.at[3 * a + k], sib))
            blk = ld.at[2 * px + py, pl.ds((1 - c) * rh, rh), :]
            theirs.append(_remote(blk, blk, send.at[3 * a + k], recv.at[3 * a + k], sib))
    return mine, theirs


def _presum_copies(grads, lands, send, recv):
    x, y, c = _axes()
    cps = []
    for a, (g, ld) in enumerate(zip(grads, lands)):
        rh = g.shape[1] // 2
        cps.append(_remote(g.at[:, pl.ds((1 - c) * rh, rh), :], ld, send.at[a], recv.at[a], (x, y, 1 - c)))
    return cps


def presum_wait(grads, lands, send, recv, after):
    m = len(grads)

    def body(*refs):
        for cp in _presum_copies(refs[:m], refs[m:2 * m], refs[2 * m], refs[2 * m + 1]):
            cp.wait_send()
            cp.wait_recv()

    res = pl.pallas_call(
        body, name="presum_wait", in_specs=[HBM] * (2 * m) + [SEM, SEM] + [ANY] * len(after),
        out_specs=[HBM] * (2 * m),
        out_shape=[pltpu.HBM(g.shape, g.dtype) for g in grads] + [pltpu.HBM(l.shape, l.dtype) for l in lands],
        input_output_aliases={i: i for i in range(2 * m)},
        compiler_params=pltpu.CompilerParams(has_side_effects=EFFECT),
    )(*grads, *lands, send, recv, *after)
    return res[:m], res[m:]


def add_halves(gs, lands, ids):
    m = len(gs)

    def body(ids_ref, *refs):
        for a_ref, b_ref, o_ref in zip(refs[:m], refs[m:2 * m], refs[2 * m:]):
            o_ref[...] = (a_ref[...].astype(F32) + b_ref[...].astype(F32)).astype(o_ref.dtype)

    def spec(ld, where):
        return pl.BlockSpec((None,) + ld.shape[1:], where)

    return pl.pallas_call(
        body, name="add_halves",
        grid_spec=pltpu.PrefetchScalarGridSpec(
            num_scalar_prefetch=1, grid=(N_CHIP,),
            in_specs=[spec(ld, lambda i, ids_ref: (i, ids_ref[1], 0)) for ld in lands]
            + [spec(ld, lambda i, ids_ref: (i, 0, 0)) for ld in lands],
            out_specs=[spec(ld, lambda i, ids_ref: (i, 0, 0)) for ld in lands]),
        out_shape=[jax.ShapeDtypeStruct(ld.shape, ld.dtype) for ld in lands],
        compiler_params=_params(("parallel",)),
    )(ids, *gs, *lands)


def _scatter_copies(parts, lands, send, recv):
    x, y, c = _axes()
    cps = []
    for a, (pt, ld) in enumerate(zip(parts, lands)):
        for k, (px, py) in enumerate(_other_chips(x, y)):
            cps.append(_remote(pt.at[2 * px + py], ld.at[k], send.at[3 * a + k], recv.at[3 * a + k], (px, py, c)))
    return cps


def scatter_wait(parts, lands, send, recv, after):
    m = len(parts)

    def body(*refs):
        for cp in _scatter_copies(refs[:m], refs[m:2 * m], refs[2 * m], refs[2 * m + 1]):
            cp.wait_send()
            cp.wait_recv()

    res = pl.pallas_call(
        body, name="scatter_wait", in_specs=[HBM] * (2 * m) + [SEM, SEM] + [ANY] * len(after),
        out_specs=[HBM] * (2 * m),
        out_shape=[pltpu.HBM(p.shape, p.dtype) for p in parts] + [pltpu.HBM(l.shape, l.dtype) for l in lands],
        input_output_aliases={i: i for i in range(2 * m)},
        compiler_params=pltpu.CompilerParams(has_side_effects=EFFECT),
    )(*parts, *lands, send, recv, *after)
    return res[:m], res[m:]


def sum_partials(parts, lands, ids, layer, depth, intos):
    m = len(parts)
    nt = STREAM_STEPS

    def body(ids_ref, *refs):
        for p_ref, l_ref, o_ref in zip(refs[:m], refs[m:2 * m], refs[-m:]):
            acc = p_ref[...].astype(F32)
            for k in range(N_CHIP - 1):
                acc = acc + l_ref[k].astype(F32)
            o_ref[...] = acc

    def rows(p):
        return p.shape[1] // nt

    in_specs = [pl.BlockSpec((None, rows(p), p.shape[2]), lambda i, ids_ref: (ids_ref[0], i, 0)) for p in parts]
    in_specs += [pl.BlockSpec((N_CHIP - 1, rows(p), p.shape[2]), lambda i, ids_ref: (0, i, 0)) for p in parts]
    args = [ids, *parts, *lands]
    aliases = {}
    if intos is not None:
        in_specs += [ANY] * m
        args += list(intos)
        aliases = {1 + 2 * m + a: a for a in range(m)}
    return pl.pallas_call(
        body, name="sum_partials",
        grid_spec=pltpu.PrefetchScalarGridSpec(
            num_scalar_prefetch=1, grid=(nt,), in_specs=in_specs,
            out_specs=[pl.BlockSpec((None, rows(p), p.shape[2]), lambda i, ids_ref: (layer, ids_ref[1] * nt + i, 0))
                       for p in parts]),
        out_shape=[jax.ShapeDtypeStruct((depth, 2 * p.shape[1], p.shape[2]), F32) for p in parts],
        input_output_aliases=aliases, compiler_params=_params(("parallel",)),
    )(*args)


def _final_copies(layer):
    def copies(bufs, send, recv):
        x, y, c = _axes()
        sib = (x, y, 1 - c)
        mine, theirs = [], []
        for a, buf in enumerate(bufs):
            rh = buf.shape[1] // 2
            src = buf.at[layer, pl.ds(c * rh, rh), :]
            mine.append(_remote(src, src, send.at[a], recv.at[a], sib))
            dst = buf.at[layer, pl.ds((1 - c) * rh, rh), :]
            theirs.append(_remote(dst, dst, send.at[a], recv.at[a], sib))
        return mine, theirs

    return copies


def allgather_small(pk):
    def body(in_ref, out_ref, send, recv):
        x, y, c = _axes()
        me = 2 * x + y
        chips = _other_chips(x, y)
        out_ref[pl.ds(me, 1)] = in_ref[...][None]
        cps = []
        for k, (px, py) in enumerate(chips):
            cp = _remote(in_ref, out_ref.at[me], send.at[k], recv.at[k], (px, py, c))
            cp.start()
            cps.append(cp)
        for k, (px, py) in enumerate(chips):
            got = out_ref.at[2 * px + py]
            _remote(got, got, send.at[k], recv.at[k], (px, py, c)).wait_recv()
        for cp in cps:
            cp.wait_send()

    return pl.pallas_call(
        body, name="allgather_small", in_specs=[VMEM_WHOLE], out_specs=VMEM_WHOLE,
        out_shape=jax.ShapeDtypeStruct((N_CHIP,) + pk.shape, pk.dtype),
        scratch_shapes=[pltpu.SemaphoreType.DMA((3,))] * 2,
    )(pk)


N_DEV = 8


def _small_copies(bufs, send, recv):
    g, slots = bufs
    x, y, c = _axes()
    me = 4 * x + 2 * y + c
    mine, theirs = [], []
    for mask in range(1, N_DEV):
        px = 1 - x if mask & 4 else x
        py = 1 - y if mask & 2 else y
        pc = 1 - c if mask & 1 else c
        mine.append(_remote(g, slots.at[me], send.at[mask - 1], recv.at[mask - 1], (px, py, pc)))
        got = slots.at[4 * px + 2 * py + pc]
        theirs.append(_remote(got, got, send.at[mask - 1], recv.at[mask - 1], (px, py, pc)))
    return mine, theirs


def sum_slots(g, slots, me):
    def body(me_ref, g_ref, slots_ref, o_ref):
        acc = None
        for d in range(N_DEV):
            term = jnp.where(me_ref[0] == d, g_ref[...], slots_ref[d])
            acc = term if acc is None else acc + term
        o_ref[...] = acc

    return pl.pallas_call(
        body, name="sum_slots",
        grid_spec=pltpu.PrefetchScalarGridSpec(
            num_scalar_prefetch=1, grid=(1,),
            in_specs=[pl.BlockSpec(g.shape, lambda i, me_ref: (0, 0)),
                      pl.BlockSpec(slots.shape, lambda i, me_ref: (0, 0, 0))],
            out_specs=pl.BlockSpec(g.shape, lambda i, me_ref: (0, 0))),
        out_shape=jax.ShapeDtypeStruct(g.shape, g.dtype),
        compiler_params=_params(("arbitrary",)),
    )(me, g, slots)


def adamw(w, g, m, v, layer=None, intos=None):
    shape = w.shape
    cols = shape[-1]
    rows = int(np.prod(shape[:-1]))
    span = rows if layer is None else rows // shape[0]
    tr = span
    for cand in (256, 128):
        if span % cand == 0 and cand * cols * 4 <= 2 * 1024 * 1024:
            tr = cand
            break
    first = 0 if layer is None else layer * (span // tr)
    c1 = 1.0 - ADAM_B1 ** ADAM_STEP
    c2 = 1.0 - ADAM_B2 ** ADAM_STEP

    def body(w_ref, g_ref, m_ref, v_ref, *rest):
        d_ref, nm_ref, nv_ref, g_out = rest[-4:]
        gv = g_ref[...]
        g_out[...] = gv
        nm = ADAM_B1 * m_ref[...] + (1.0 - ADAM_B1) * gv
        nv = ADAM_B2 * v_ref[...] + (1.0 - ADAM_B2) * jnp.square(gv)
        d_ref[...] = -ADAM_LR * ((nm / c1) / (jnp.sqrt(nv / c2) + ADAM_EPS) + ADAM_WD * w_ref[...])
        nm_ref[...] = nm
        nv_ref[...] = nv

    spec = pl.BlockSpec((tr, cols), lambda i: (first + i, 0))
    args = [a.reshape(rows, cols) for a in (w, g, m, v)]
    in_specs, aliases = [spec] * 4, {}
    if intos is not None:
        args += [a.reshape(rows, cols) for a in intos]
        in_specs += [ANY] * 4
        aliases = {4 + k: k for k in range(4)}
    res = pl.pallas_call(
        body, name="adamw", grid=(span // tr,), in_specs=in_specs, out_specs=[spec] * 4,
        out_shape=[jax.ShapeDtypeStruct((rows, cols), F32)] * 4, input_output_aliases=aliases,
        compiler_params=_params(("parallel",)),
    )(*args)
    return [r.reshape(shape) for r in res]


WEIGHTS = ("norm_g", "ffn1_w_gu", "ffn1_w_down", "w_in", "w_ret_o", "sc_conv_w", "w_sc_o", "cf_dw_w", "cf_dw_b",
           "cf_ln_g", "cf_ln_b", "w_cf_o", "w_o", "ffn2_w_gu", "ffn2_w_down")
SHARDED_SMALL = ("norm_g", "sc_conv_w", "cf_dw_w")
REPLICATED_SMALL = ("cf_dw_b", "cf_ln_g", "cf_ln_b")

def _pack_rows(parts):
    padded, offs, at = [], [], 0
    for p in parts:
        r = -(-p.shape[0] // SUBLANES) * SUBLANES
        padded.append(jnp.pad(p, ((0, r - p.shape[0]), (0, 0))))
        offs.append(at)
        at += r
    return jnp.concatenate(padded, axis=0), offs


def kernel(x, positions, norm_g, ffn1_w_gu, ffn1_w_down, w_in, w_ret_o, sc_conv_w, w_sc_o, cf_dw_w, cf_dw_b, cf_ln_g, cf_ln_b, w_cf_o, w_o, ffn2_w_gu, ffn2_w_down, loss_target, m_norm_g, m_ffn1_w_gu, m_ffn1_w_down, m_w_in, m_w_ret_o, m_sc_conv_w, m_w_sc_o, m_cf_dw_w, m_cf_dw_b, m_cf_ln_g, m_cf_ln_b, m_w_cf_o, m_w_o, m_ffn2_w_gu, m_ffn2_w_down, v_norm_g, v_ffn1_w_gu, v_ffn1_w_down, v_w_in, v_w_ret_o, v_sc_conv_w, v_w_sc_o, v_cf_dw_w, v_cf_dw_b, v_cf_ln_g, v_cf_ln_b, v_w_cf_o, v_w_o, v_ffn2_w_gu, v_ffn2_w_down):
    wts = dict(zip(WEIGHTS, (norm_g, ffn1_w_gu, ffn1_w_down, w_in, w_ret_o, sc_conv_w, w_sc_o, cf_dw_w, cf_dw_b,
                             cf_ln_g, cf_ln_b, w_cf_o, w_o, ffn2_w_gu, ffn2_w_down)))
    mom = dict(zip(WEIGHTS, (m_norm_g, m_ffn1_w_gu, m_ffn1_w_down, m_w_in, m_w_ret_o, m_sc_conv_w, m_w_sc_o,
                             m_cf_dw_w, m_cf_dw_b, m_cf_ln_g, m_cf_ln_b, m_w_cf_o, m_w_o, m_ffn2_w_gu, m_ffn2_w_down)))
    var = dict(zip(WEIGHTS, (v_norm_g, v_ffn1_w_gu, v_ffn1_w_down, v_w_in, v_w_ret_o, v_sc_conv_w, v_w_sc_o,
                             v_cf_dw_w, v_cf_dw_b, v_cf_ln_g, v_cf_ln_b, v_w_cf_o, v_w_o, v_ffn2_w_gu, v_ffn2_w_down)))
    depth = norm_g.shape[0]
    dq = norm_g.shape[-1]
    d = N_CHIP * dq
    chip = 2 * lax.axis_index("x") + lax.axis_index("y")
    ids = jnp.stack([chip, lax.axis_index("c")]).astype(jnp.int32)

    pk, offs = _pack_rows([wts[n].reshape(-1, dq) for n in SHARDED_SMALL])
    gk4 = allgather_small(pk)
    gk = gk4.transpose(1, 0, 2).reshape(pk.shape[0], d)
    small = {n: wts[n] for n in REPLICATED_SMALL}
    for n, o in zip(SHARDED_SMALL, offs):
        rows = wts[n].shape[0] * wts[n].shape[1]
        small[n] = gk[o:o + rows].reshape(wts[n].shape[:2] + (d,))

    order = [(l, blk) for l in range(depth) for blk in BLOCKS]
    def placed(groups, after):
        return [place_quarters([wts[n] for n in BLOCK_WEIGHTS[blk]], l, ids, after) for l, blk in groups]

    first, token = gather_start("gather_start_first", placed(order[:1], gk4), gk4)
    rest, token = gather_start("gather_start_rest", placed(order[1:], token), token)
    started = dict(zip(order, first + rest))
    small["norm_g"] = small["norm_g"] + token[0:1, 0:1]

    filling = {}

    def fill(group, after):
        send, recv, lands = started[group]
        lands = gather_wait(lands, send, recv, after)
        started_fill, tok = copy_start("fill_start", [(lands, _fill_copies, 3 * len(lands))])
        filling[group] = started_fill[0]
        return tok[0:1, 0:1]

    def fetch(l, blk, after):
        at = order.index((l, blk))
        if (l, blk) not in filling:
            fill((l, blk), token if at == 0 else after)
        send, recv, lands = filling.pop((l, blk))
        lands = copy_wait("fill_wait", lands, send, recv, _fill_copies, (after,))
        tok, mid = None, None
        if at == 1:
            mid = functools.partial(fill, order[at + 1])
        elif 1 < at < len(order) - 1:
            tok = fill(order[at + 1], lands[0])
        return dict(zip(BLOCK_WEIGHTS[blk], lands)), tok, mid

    gsum = {n: None for n in BIG}
    presums, scatters, finals = [], [], []

    def scatter_ready(after):
        group, gl, lands, send, recv = presums.pop(0)
        gl, lands = presum_wait(gl, lands, send, recv, after)
        parts = list(add_halves(gl, lands, ids))
        m = len(parts)
        lands = [lax.empty((N_CHIP - 1,) + p.shape[1:], p.dtype) for p in parts]
        family = (parts + lands, lambda refs, sd, rv: (_scatter_copies(refs[:m], refs[m:], sd, rv),) * 2, 3 * m)
        return family, lambda sd, rv, bufs: scatters.append((group, bufs[:m], bufs[m:], sd, rv))

    def final_ready(after):
        (l, blk), parts, lands, send, recv = scatters.pop(0)
        parts, lands = scatter_wait(parts, lands, send, recv, after)
        names = BLOCK_WEIGHTS[blk]
        intos = None if gsum[names[0]] is None else [gsum[n] for n in names]
        sums = list(sum_partials(parts, lands, ids, l, depth, intos))

        def note(sd, rv, bufs):
            gsum.update(zip(names, bufs))
            finals.append((names, l, sd, rv))

        return (sums, _final_copies(l), len(sums)), note

    def start_all(name, ready, after=()):
        started, tok = copy_start(name, [family for family, _ in ready], after)
        for (_, note), (sd, rv, bufs) in zip(ready, started):
            note(sd, rv, bufs)
        return tok

    def scatter_next(after):
        return start_all("scatter_start", [scatter_ready(after)])

    def sum_next(after):
        return start_all("final_start", [final_ready(after)])

    def final_next(after):
        names, l, send, recv = finals.pop(0)
        gsum.update(zip(names, copy_wait("final_wait", [gsum[n] for n in names], send, recv, _final_copies(l), after)))

    def push(l, blk, grads):
        gl = [grads[n] for n in BLOCK_WEIGHTS[blk]]
        m = len(gl)
        lands = [lax.empty((g.shape[0], g.shape[1] // 2, g.shape[2]), g.dtype) for g in gl]
        ready = [((gl + lands, lambda refs, sd, rv: (_presum_copies(refs[:m], refs[m:], sd, rv),) * 2, m),
                  lambda sd, rv, bufs: presums.append(((l, blk), bufs[:m], bufs[m:], sd, rv)))]
        if scatters:
            ready.append(final_ready((gl[0],)))
        if presums:
            ready.append(scatter_ready((gl[0],)))
        return start_all("push_start", ready)[0:1, 0:1]

    loss, grad_x, gs = local_step(x, positions, loss_target, small, fetch, push)

    names = SHARDED_SMALL + REPLICATED_SMALL
    pg, offs = _pack_rows([gs[n].reshape(-1, d) for n in names])
    small_bufs = [pg, lax.empty((N_DEV,) + pg.shape, pg.dtype)]
    ((s_send, s_recv, s_bufs),), tok = copy_start("small_start", [(small_bufs, _small_copies, N_DEV - 1)], (grad_x,))
    tok = scatter_next((grad_x, tok))

    delta, new_m, new_v, grads = {}, {}, {}, {}

    def update(n, layer=None):
        g = gsum[n] if n in BIG else grads[n]
        prev = [delta[n], new_m[n], new_v[n], grads[n]] if layer is not None and n in delta else None
        delta[n], new_m[n], new_v[n], grads[n] = adamw(wts[n], g, mom[n], var[n], layer, prev)

    while finals and finals[0][1] > 0:
        done, l = finals[0][:2]
        final_next((tok,))
        for n in done:
            update(n, l)
    upper = tuple(delta[n] for n in BIG if n in delta)
    pg, slots = copy_wait("small_wait", s_bufs, s_send, s_recv, _small_copies, upper + (tok,))
    me = (2 * chip + lax.axis_index("c")).astype(jnp.int32).reshape(1)
    tot = sum_slots(pg, slots, me)
    for n, o in zip(names, offs):
        rows = int(np.prod(gs[n].shape[:-1]))
        full = tot[o:o + rows]
        if n in SHARDED_SMALL:
            full = lax.dynamic_slice_in_dim(full, chip * dq, dq, axis=1)
        grads[n] = full.reshape(wts[n].shape)

    for n in names:
        update(n)
    after = tuple(delta[n] for n in names)
    while scatters or finals:
        if scatters:
            after = (sum_next(after),)
        done, l = finals[0][:2]
        final_next(after)
        for n in done:
            update(n, l)
        after = tuple(delta[n] for n in done)

    loss_all = lax.psum(loss[0, 0], ("x", "y", "c"))
    return (loss_all, grad_x, *[grads[n] for n in WEIGHTS], *[delta[n] for n in WEIGHTS],
            *[new_m[n] for n in WEIGHTS], *[new_v[n] for n in WEIGHTS])
```

```python
import functools

import jax
import jax.numpy as jnp
import numpy as np
from jax import lax
from jax.experimental import pallas as pl
from jax.experimental.pallas import tpu as pltpu

F32 = jnp.float32
BF16 = jnp.bfloat16
MXU_DTYPE = BF16
VMEM_LIMIT_BYTES = 56 * 1024 * 1024
MESH = pl.DeviceIdType.MESH

N_CHIP = 4
CHUNK = 64
RET_HEADS = 4
RET_QK_DIM = 128
RET_V_DIM = 256
SC_KERNEL = 3
CF_KERNEL = 31
ROPE_BASE = 10000.0
NORM_EPS = 1e-6
LN_EPS = 1e-5
ADAM_LR = 0.001
ADAM_B1 = 0.9
ADAM_B2 = 0.999
ADAM_EPS = 1e-08
ADAM_WD = 0.01
ADAM_STEP = 10

SUBLANES = 8
CONV_PAD = 32
CONV_TS = 256
CONV_TC = 512
CONV_ROWS = 32
CONV_TILES = range(0, CONV_ROWS, SUBLANES)
SC_TS = 512


def _conv_scratch(ts):
    return [pltpu.VMEM((ts + CONV_PAD, CONV_TC), F32),
            pltpu.VMEM((SUBLANES - 1, ts + CONV_PAD - SUBLANES, CONV_TC), F32)]
RET_TQ = 512
MM_TM = 1024
MM_TN = 1536
MM_K1 = 1024
MM_W1 = 8 << 20
MM_SLICE = 256
MM_IN_BYTES = 36 << 20
STREAM_STEPS = 2


def _params(sem):
    return pltpu.CompilerParams(dimension_semantics=sem, vmem_limit_bytes=VMEM_LIMIT_BYTES)


def _axes():
    return lax.axis_index("x"), lax.axis_index("y"), lax.axis_index("c")


NN = (((1,), (0,)), ((), ()))
NT = (((1,), (1,)), ((), ()))
TN = (((0,), (0,)), ((), ()))


def _mm(name, a, b, out_shape, out_dtype, grid, a_spec, b_spec, o_spec, dims, acc_shape):
    nk = grid[2]

    def body(a_ref, b_ref, o_ref, *scratch):
        bv = b_ref[...]
        if bv.ndim == 3:
            bv = bv.reshape(-1, bv.shape[-1])
        part = lax.dot_general(a_ref[...], bv, dims, preferred_element_type=F32)

        def put(v):
            o_ref[...] = v.reshape(o_ref.shape).astype(o_ref.dtype)

        if nk == 1:
            put(part)
        else:
            acc = scratch[0]
            k = pl.program_id(2)

            @pl.when(k == 0)
            def _():
                acc[...] = part

            @pl.when(k > 0)
            def _():
                acc[...] += part

            @pl.when(k == nk - 1)
            def _():
                put(acc[...])

    scratch = [pltpu.VMEM(acc_shape, F32)] if nk > 1 else []
    return pl.pallas_call(
        body, name=name, grid=grid, in_specs=[a_spec, b_spec], out_specs=o_spec,
        out_shape=jax.ShapeDtypeStruct(out_shape, out_dtype), scratch_shapes=scratch,
        compiler_params=_params(("parallel", "parallel", "arbitrary")),
    )(a, b)


def _tile(n, target):
    best = None
    for t in range(128, min(n, target) + 1, 128):
        if n % t == 0:
            best = t
    assert best is not None, (n, target)
    return best


def _token_rows(t, width):
    tt = t
    while tt > MM_TM and tt * width * jnp.dtype(MXU_DTYPE).itemsize * 2 > MM_IN_BYTES:
        tt //= 2
    return tt


def mm_fwd(name, a, w4, mode, out_dtype):
    t = a.shape[0]
    _, r, c = w4.shape
    tm = min(t, MM_TM)
    if mode == "col":
        tn = _tile(c, MM_TN)
        npj = c // tn
        grid = (t // tm, N_CHIP * npj, 1)
        a_spec = pl.BlockSpec((tm, r), lambda i, j, k: (i, 0))
        b_spec = pl.BlockSpec((None, r, tn), lambda i, j, k: (j // npj, 0, j % npj))
        o_spec = pl.BlockSpec((tm, tn), lambda i, j, k: (i, j))
        return _mm(name, a, w4, (t, N_CHIP * c), out_dtype, grid, a_spec, b_spec, o_spec, NN, (tm, tn))
    if w4.size * w4.dtype.itemsize <= MM_W1:
        grid = (t // tm, 1, 1)
        a_spec = pl.BlockSpec((tm, N_CHIP * r), lambda i, j, k: (i, 0))
        b_spec = pl.BlockSpec((N_CHIP, r, c), lambda i, j, k: (0, 0, 0))
        o_spec = pl.BlockSpec((tm, c), lambda i, j, k: (i, 0))
        return _mm(name, a, w4, (t, c), out_dtype, grid, a_spec, b_spec, o_spec, NN, (tm, c))
    grid = (t // tm, 1, N_CHIP)
    a_spec = pl.BlockSpec((tm, r), lambda i, j, k: (i, k))
    b_spec = pl.BlockSpec((None, r, c), lambda i, j, k: (k, 0, 0))
    o_spec = pl.BlockSpec((tm, c), lambda i, j, k: (i, 0))
    return _mm(name, a, w4, (t, c), out_dtype, grid, a_spec, b_spec, o_spec, NN, (tm, c))


def mm_dx(name, dy, w4, mode, out_dtype):
    t = dy.shape[-2]
    _, r, c = w4.shape
    tm = min(t, MM_TM)
    if mode == "col":
        tn, npj = c, 1
        hb = N_CHIP // 2 * npj
        grid = (t // tm, 1, N_CHIP * npj)
        if dy.ndim == 3:
            a_spec = pl.BlockSpec((None, tm, tn), lambda i, j, k: (k // hb, i, k % hb))
        else:
            a_spec = pl.BlockSpec((tm, tn), lambda i, j, k: (i, k))
        b_spec = pl.BlockSpec((None, r, tn), lambda i, j, k: (k // npj, 0, k % npj))
        o_spec = pl.BlockSpec((tm, r), lambda i, j, k: (i, 0))
        return _mm(name, dy, w4, (t, r), out_dtype, grid, a_spec, b_spec, o_spec, NT, (tm, r))
    if N_CHIP * r <= MM_K1:
        grid = (t // tm, 1, 1)
        a_spec = pl.BlockSpec((tm, c), lambda i, j, k: (i, 0))
        b_spec = pl.BlockSpec((N_CHIP, r, c), lambda i, j, k: (0, 0, 0))
        o_spec = pl.BlockSpec((tm, N_CHIP * r), lambda i, j, k: (i, 0))
        return _mm(name, dy, w4, (t, N_CHIP * r), out_dtype, grid, a_spec, b_spec, o_spec, NT, (tm, N_CHIP * r))
    grid = (t // tm, N_CHIP, 1)
    a_spec = pl.BlockSpec((tm, c), lambda i, j, k: (i, 0))
    b_spec = pl.BlockSpec((None, r, c), lambda i, j, k: (j, 0, 0))
    o_spec = pl.BlockSpec((tm, r), lambda i, j, k: (i, j))
    return _mm(name, dy, w4, (t, N_CHIP * r), out_dtype, grid, a_spec, b_spec, o_spec, NT, (tm, r))


def mm_dw(name, a, dy, mode, shape3):
    t = a.shape[0]
    _, r, c = shape3
    if mode == "col":
        tn = _tile(c, MM_TN)
        npj = c // tn
        tt = _token_rows(t, r + tn)
        grid = (1, N_CHIP * npj, t // tt)
        a_spec = pl.BlockSpec((tt, r), lambda i, j, k: (k, 0))
        hb = N_CHIP // 2 * npj
        if dy.ndim == 3:
            b_spec = pl.BlockSpec((None, tt, tn), lambda i, j, k: (j // hb, k, j % hb))
        else:
            b_spec = pl.BlockSpec((tt, tn), lambda i, j, k: (k, j))
        o_spec = pl.BlockSpec((None, r, tn), lambda i, j, k: (j // npj, 0, j % npj))
        return _mm(name, a, dy, shape3, MXU_DTYPE, grid, a_spec, b_spec, o_spec, TN, (r, tn))
    if N_CHIP * r <= MM_K1:
        tt = min(_token_rows(t, N_CHIP * r + c), max(t // 4, MM_TM))
        grid = (1, 1, t // tt)
        a_spec = pl.BlockSpec((tt, N_CHIP * r), lambda i, j, k: (k, 0))
        b_spec = pl.BlockSpec((tt, c), lambda i, j, k: (k, 0))
        o_spec = pl.BlockSpec((N_CHIP, r, c), lambda i, j, k: (0, 0, 0))
        return _mm(name, a, dy, shape3, MXU_DTYPE, grid, a_spec, b_spec, o_spec, TN, (N_CHIP * r, c))
    tt = _token_rows(t, r + c)
    grid = (N_CHIP, 1, t // tt)
    a_spec = pl.BlockSpec((tt, r), lambda i, j, k: (k, i))
    b_spec = pl.BlockSpec((tt, c), lambda i, j, k: (k, 0))
    o_spec = pl.BlockSpec((None, r, c), lambda i, j, k: (i, 0, 0))
    return _mm(name, a, dy, shape3, MXU_DTYPE, grid, a_spec, b_spec, o_spec, TN, (r, c))


def _rms_bwd(x, g, dh):
    r = lax.rsqrt(jnp.mean(x * x, axis=-1, keepdims=True) + NORM_EPS)
    xhat = x * r
    dyg = dh * g
    dx = r * (dyg - xhat * jnp.mean(dyg * xhat, axis=-1, keepdims=True))
    return dx, jnp.sum(dh * xhat, axis=0, keepdims=True)


def mm_dx_norms(name, dy, w4, x, g_pre, dres, prev, after):
    t = dy.shape[-2]
    _, r, c = w4.shape
    tm = min(t, MM_TM // 2)
    nt, nk = t // tm, N_CHIP
    hb = N_CHIP // 2
    chained = prev is not None

    def body(dy_ref, w_ref, x_ref, dres_ref, g_ref, *rest):
        rest = rest[1:] if after is not None else rest
        if chained:
            y_ref, gp_ref, dx_ref, dg_ref, dyp_ref, dgp_ref, acc = rest
        else:
            dx_ref, dg_ref, acc = rest
        i, k = pl.program_id(0), pl.program_id(1)
        part = lax.dot_general(dy_ref[...], w_ref[...], NT, preferred_element_type=F32)

        @pl.when(k == 0)
        def _():
            acc[...] = part

        @pl.when(k > 0)
        def _():
            acc[...] += part

        def add_to(ref, v):
            @pl.when(i == 0)
            def _():
                ref[...] = v

            @pl.when(i > 0)
            def _():
                ref[...] += v

        @pl.when(k == nk - 1)
        def _():
            dx, dg = _rms_bwd(x_ref[...], g_ref[...], acc[...])
            dxs = dres_ref[...] + dx
            dx_ref[...] = dxs
            add_to(dg_ref, dg)
            if chained:
                dyp, dgp = _rms_bwd(y_ref[...], gp_ref[...], dxs)
                dyp_ref[...] = (prev[2] * dyp).astype(dyp_ref.dtype)
                add_to(dgp_ref, prev[2] * dgp)

    if dy.ndim == 3:
        dy_spec = pl.BlockSpec((None, tm, c), lambda i, k: (k // hb, i, k % hb))
    else:
        dy_spec = pl.BlockSpec((tm, c), lambda i, k: (i, k))
    rows = pl.BlockSpec((tm, r), lambda i, k: (i, 0))
    gain = pl.BlockSpec((1, r), lambda i, k: (0, 0))
    in_specs = [dy_spec, pl.BlockSpec((None, r, c), lambda i, k: (k, 0, 0)), rows, rows, gain]
    args = [dy, w4, x, dres, g_pre]
    if after is not None:
        in_specs.append(pl.BlockSpec(memory_space=pl.ANY))
        args.append(after)
    out_specs = [rows, gain]
    out_shape = [jax.ShapeDtypeStruct((t, r), F32), jax.ShapeDtypeStruct((1, r), F32)]
    if chained:
        in_specs += [rows, gain]
        args += [prev[0], prev[1]]
        out_specs += [rows, gain]
        out_shape += [jax.ShapeDtypeStruct((t, r), MXU_DTYPE), jax.ShapeDtypeStruct((1, r), F32)]
    res = pl.pallas_call(
        body, name=name, grid=(nt, nk), in_specs=in_specs, out_specs=out_specs, out_shape=out_shape,
        scratch_shapes=[pltpu.VMEM((tm, r), F32)], compiler_params=_params(("arbitrary", "arbitrary")),
    )(*args)
    return tuple(res) if chained else (res[0], res[1], None, None)


def _rowwise(name, fn, rows, pars, outs, accs=(), tm=256, ncol=1):
    t = rows[0][0].shape[0]
    nrow, npar, nout = len(rows), len(pars), len(outs)

    def body(*refs):
        vals = [r[...] for r in refs[:nrow + npar]]
        res = fn(*vals)
        out_refs = refs[nrow + npar:nrow + npar + nout]
        acc_refs = refs[nrow + npar + nout:]
        for o, v in zip(out_refs, res[:nout]):
            o[...] = v.astype(o.dtype)
        i = pl.program_id(1)
        for a, v in zip(acc_refs, res[nout:]):
            @pl.when(i == 0)
            def _(a=a, v=v):
                a[...] = v.astype(F32)

            @pl.when(i > 0)
            def _(a=a, v=v):
                a[...] += v.astype(F32)

    in_specs = [pl.BlockSpec((tm, w), functools.partial(lambda j, i, b: (i, b + j), b=b)) for _, w, b in rows]
    for arr, w in pars:
        if w is None:
            in_specs.append(pl.BlockSpec(arr.shape, lambda j, i: (0, 0)))
        else:
            in_specs.append(pl.BlockSpec((1, w), lambda j, i: (0, j)))
    out_specs = [pl.BlockSpec((tm, w), lambda j, i: (i, j)) for _, w, _ in outs]
    out_specs += [pl.BlockSpec((1, w), lambda j, i: (0, j)) for _, w in accs]
    out_shape = [jax.ShapeDtypeStruct((t, tw), dt) for tw, _, dt in outs]
    out_shape += [jax.ShapeDtypeStruct((1, tw), F32) for tw, _ in accs]
    res = pl.pallas_call(
        body, name=name, grid=(ncol, t // tm), in_specs=in_specs, out_specs=out_specs, out_shape=out_shape,
        compiler_params=_params(("parallel", "arbitrary" if accs else "parallel")),
    )(*[r[0] for r in rows], *[p[0] for p in pars])
    return res


def _rms(x, g):
    xf = x.astype(F32)
    return xf * lax.rsqrt(jnp.mean(xf * xf, axis=-1, keepdims=True) + NORM_EPS) * g


def _silu(x):
    return x * jax.nn.sigmoid(x)


def rms_fwd(name, x, g):
    d = x.shape[1]
    return _rowwise(name, lambda x, g: (_rms(x, g),), [(x, d, 0)], [(g, None)], [(d, d, MXU_DTYPE)], tm=512)[0]


def rms_bwd(name, x, g, dh, dres):
    d = x.shape[1]

    def fn(x, dh, dres, g):
        _, vjp = jax.vjp(_rms, x, g)
        dx, dg = vjp(dh.astype(F32))
        return dres + dx, dg

    return _rowwise(name, fn, [(x, d, 0), (dh, d, 0), (dres, d, 0)], [(g, None)], [(d, d, F32)], [(d, d)], tm=256)


def mm_post(name, a, w4, x, g_post, scale, g_next):
    t = a.shape[0]
    _, r, c = w4.shape
    tm = min(t, MM_TM // 2)
    chained = g_next is not None

    def body(a_ref, w_ref, x_ref, gp_ref, *rest):
        gn_ref, y_ref, xn_ref, h_ref = rest if chained else (None,) + rest + (None,)
        y = lax.dot_general(a_ref[...], w_ref[...].reshape(N_CHIP * r, c), NN, preferred_element_type=F32)
        y_ref[...] = y
        xn = x_ref[...] + scale * _rms(y, gp_ref[...])
        xn_ref[...] = xn
        if chained:
            h_ref[...] = _rms(xn, gn_ref[...]).astype(h_ref.dtype)

    def rows(width):
        return pl.BlockSpec((tm, width), lambda i: (i, 0))

    gain = pl.BlockSpec((1, c), lambda i: (0, 0))
    in_specs = [rows(N_CHIP * r), pl.BlockSpec((N_CHIP, r, c), lambda i: (0, 0, 0)), rows(c), gain]
    args = [a, w4, x, g_post]
    out_specs, out_shape = [rows(c), rows(c)], [jax.ShapeDtypeStruct((t, c), F32)] * 2
    if chained:
        in_specs.append(gain)
        args.append(g_next)
        out_specs.append(rows(c))
        out_shape.append(jax.ShapeDtypeStruct((t, c), MXU_DTYPE))
    res = pl.pallas_call(
        body, name=name, grid=(t // tm,), in_specs=in_specs, out_specs=out_specs, out_shape=out_shape,
        compiler_params=_params(("parallel",)),
    )(*args)
    return res[0], res[1], (res[2] if chained else None)


def post_bwd(name, y, g, dx, scale):
    d = y.shape[1]

    def fn(y, dx, g):
        _, vjp = jax.vjp(lambda y, g: scale * _rms(y, g), y, g)
        return vjp(dx)

    return _rowwise(name, fn, [(y, d, 0), (dx, d, 0)], [(g, None)], [(d, d, MXU_DTYPE)], [(d, d)], tm=256)


def ffn_up(name, h, w4):
    t = h.shape[0]
    _, r, c = w4.shape
    tm = min(t, MM_TM)
    tn = _tile(c, MM_TM)
    npj = c // tn
    half = N_CHIP // 2

    def body(h_ref, wg_ref, wu_ref, gu_ref, a_ref):
        hv = h_ref[...]
        g = lax.dot_general(hv, wg_ref[...], NN, preferred_element_type=F32)
        u = lax.dot_general(hv, wu_ref[...], NN, preferred_element_type=F32)
        gu_ref[0] = g.astype(gu_ref.dtype)
        gu_ref[1] = u.astype(gu_ref.dtype)
        a_ref[...] = (_silu(g) * u).astype(a_ref.dtype)

    f = half * c
    return pl.pallas_call(
        body, name=name, grid=(t // tm, half * npj),
        in_specs=[pl.BlockSpec((tm, r), lambda i, j: (i, 0)),
                  pl.BlockSpec((None, r, tn), lambda i, j: (j // npj, 0, j % npj)),
                  pl.BlockSpec((None, r, tn), lambda i, j: (half + j // npj, 0, j % npj))],
        out_specs=[pl.BlockSpec((2, tm, tn), lambda i, j: (0, i, j)), pl.BlockSpec((tm, tn), lambda i, j: (i, j))],
        out_shape=[jax.ShapeDtypeStruct((2, t, f), MXU_DTYPE), jax.ShapeDtypeStruct((t, f), MXU_DTYPE)],
        compiler_params=_params(("parallel", "parallel")),
    )(h, w4, w4)


def ffn_down_dx(name, dy, w4, gu):
    t = dy.shape[0]
    _, r, c = w4.shape
    tm = min(t, MM_TM)

    def body(dy_ref, w_ref, gu_ref, o_ref):
        dyv = dy_ref[...]
        for n0 in range(0, r, MM_SLICE):
            cols = pl.ds(n0, MM_SLICE)
            da = lax.dot_general(dyv, w_ref[cols, :], NT, preferred_element_type=F32)
            gate, up = gu_ref[0, :, cols].astype(F32), gu_ref[1, :, cols].astype(F32)
            sg = jax.nn.sigmoid(gate)
            silu = gate * sg
            o_ref[0, :, cols] = (da * up * (sg + silu * (1.0 - sg))).astype(o_ref.dtype)
            o_ref[1, :, cols] = (da * silu).astype(o_ref.dtype)

    return pl.pallas_call(
        body, name=name, grid=(t // tm, N_CHIP),
        in_specs=[pl.BlockSpec((tm, c), lambda i, j: (i, 0)), pl.BlockSpec((None, r, c), lambda i, j: (j, 0, 0)),
                  pl.BlockSpec((2, tm, r), lambda i, j: (0, i, j))],
        out_specs=pl.BlockSpec((2, tm, r), lambda i, j: (0, i, j)),
        out_shape=jax.ShapeDtypeStruct((2, t, N_CHIP * r), MXU_DTYPE),
        compiler_params=_params(("parallel", "parallel")),
    )(dy, w4, gu)


def _head_gate(o, g):
    mu = jnp.mean(o, axis=-1, keepdims=True)
    var = jnp.mean(jnp.square(o - mu), axis=-1, keepdims=True)
    return _silu(g.astype(F32)) * ((o - mu) * lax.rsqrt(var + LN_EPS))


def head_gate_fwd(name, o, p, gate_blk):
    dv = RET_V_DIM
    return _rowwise(name, lambda o, g: (_head_gate(o, g),), [(o, dv, 0), (p, dv, gate_blk)], [],
                    [(RET_HEADS * dv, dv, MXU_DTYPE)], tm=min(o.shape[0], 2048), ncol=RET_HEADS)[0]


def head_gate_bwd(name, o, p, gate_blk, da):
    dv = RET_V_DIM

    def fn(o, g, da):
        _, vjp = jax.vjp(_head_gate, o, g.astype(F32))
        return vjp(da.astype(F32))

    w = RET_HEADS * dv
    return _rowwise(name, fn, [(o, dv, 0), (p, dv, gate_blk), (da, dv, 0)], [],
                    [(w, dv, MXU_DTYPE), (w, dv, MXU_DTYPE)], tm=min(o.shape[0], 2048), ncol=RET_HEADS)


def _ln_silu(u, g, b):
    mu = jnp.mean(u, axis=-1, keepdims=True)
    var = jnp.mean(jnp.square(u - mu), axis=-1, keepdims=True)
    return _silu((u - mu) * lax.rsqrt(var + LN_EPS) * g + b)


def ln_silu_fwd(name, u, g, b):
    d = u.shape[1]
    return _rowwise(name, lambda u, g, b: (_ln_silu(u, g, b),), [(u, d, 0)], [(g, None), (b, None)],
                    [(d, d, MXU_DTYPE)], tm=512)[0]


def ln_silu_bwd(name, u, g, b, dc):
    d = u.shape[1]

    def fn(u, dc, g, b):
        _, vjp = jax.vjp(_ln_silu, u, g, b)
        return vjp(dc.astype(F32))

    return _rowwise(name, fn, [(u, d, 0), (dc, d, 0)], [(g, None), (b, None)], [(d, d, F32)], [(d, d), (d, d)],
                    tm=256)


def _merge(g0, g1, g2, ya, yb, yc):
    s = jax.nn.sigmoid
    return s(g0.astype(F32)) * ya + s(g1.astype(F32)) * yb + s(g2.astype(F32)) * yc


def merge_fwd(name, p, blk, ya, yb, yc):
    d = ya.shape[1]
    rows = [(p, d, blk), (p, d, blk + 1), (p, d, blk + 2), (ya, d, 0), (yb, d, 0), (yc, d, 0)]
    return _rowwise(name, lambda *v: (_merge(*v),), rows, [], [(d, d, MXU_DTYPE)], tm=256)[0]


def merge_bwd(name, p, blk, ya, yb, yc, dmg):
    d = ya.shape[1]

    def fn(g0, g1, g2, ya, yb, yc, dmg):
        _, vjp = jax.vjp(_merge, g0.astype(F32), g1.astype(F32), g2.astype(F32), ya, yb, yc)
        return vjp(dmg.astype(F32))

    rows = [(p, d, blk), (p, d, blk + 1), (p, d, blk + 2), (ya, d, 0), (yb, d, 0), (yc, d, 0), (dmg, d, 0)]
    return _rowwise(name, fn, rows, [], [(d, d, MXU_DTYPE)] * 6, tm=256)


def concat_cols(name, pieces):
    t = pieces[0].shape[0]
    widths = [p.shape[1] for p in pieces]
    tm = 256

    def body(*refs):
        o_ref, at = refs[-1], 0
        for r, w in zip(refs[:-1], widths):
            o_ref[:, at:at + w] = r[...]
            at += w

    return pl.pallas_call(
        body, name=name, grid=(t // tm,),
        in_specs=[pl.BlockSpec((tm, w), lambda i: (i, 0)) for w in widths],
        out_specs=pl.BlockSpec((tm, sum(widths)), lambda i: (i, 0)),
        out_shape=jax.ShapeDtypeStruct((t, sum(widths)), pieces[0].dtype),
        compiler_params=_params(("parallel",)),
    )(*pieces)


def loss_head(name, y, target):
    t, d = y.shape
    tm = 512

    def body(y_ref, t_ref, dy_ref, loss_ref):
        err = y_ref[...] - t_ref[...]
        dy_ref[...] = err * (1.0 / d)
        part = jnp.sum(jnp.sum(err * err, axis=1, keepdims=True), axis=0, keepdims=True) * (0.5 / d)

        @pl.when(pl.program_id(0) == 0)
        def _():
            loss_ref[...] = part

        @pl.when(pl.program_id(0) > 0)
        def _():
            loss_ref[...] += part

    return pl.pallas_call(
        body, name=name, grid=(t // tm,),
        in_specs=[pl.BlockSpec((tm, d), lambda i: (i, 0))] * 2,
        out_specs=[pl.BlockSpec((tm, d), lambda i: (i, 0)), pl.BlockSpec((1, 1), lambda i: (0, 0))],
        out_shape=[jax.ShapeDtypeStruct((t, d), F32), jax.ShapeDtypeStruct((1, 1), F32)],
        compiler_params=_params(("arbitrary",)),
    )(y, target)


def _rot(x, cos2, sin2):
    return x * cos2 + pltpu.roll(x, RET_QK_DIM // 2, 1) * sin2


def _decay_mask(lg, n0, rows, cols):
    n = n0 + lax.broadcasted_iota(jnp.int32, (rows, cols), 0)
    m = lax.broadcasted_iota(jnp.int32, (rows, cols), 1)
    shift = CHUNK.bit_length() - 1
    dist = jnp.abs(n - m).astype(F32)
    return jnp.where((m >> shift) <= (n >> shift), jnp.exp(lg * dist), 0.0)


def _ret_specs(s):
    dk, dv, h = RET_QK_DIM, RET_V_DIM, RET_HEADS
    return [
        pl.BlockSpec((s, dk), lambda b, hh: (b, hh)),
        pl.BlockSpec((s, dk), lambda b, hh: (b, h + hh)),
        pl.BlockSpec((s, dv), lambda b, hh: (b, (2 * h * dk) // dv + hh)),
        pl.BlockSpec((s, dk), lambda b, hh: (b, 0)),
        pl.BlockSpec((s, dk), lambda b, hh: (b, 0)),
        pl.BlockSpec((None, 1, dk), lambda b, hh: (hh, 0, 0)),
    ]


def retention_fwd(name, p, cos2, sin2, log_g, nb, s):
    dk, dv, h = RET_QK_DIM, RET_V_DIM, RET_HEADS

    def body(q_ref, k_ref, v_ref, cos_ref, sin_ref, lg_ref, o_ref, kr_ref):
        lg = lg_ref[0:1, 0:1]
        kr = _rot(k_ref[...].astype(F32), cos_ref[...], sin_ref[...]) * (dk ** -0.5)
        kr_ref[...] = kr.astype(kr_ref.dtype)
        for qi in range(s // RET_TQ):
            n0, kmax = qi * RET_TQ, (qi + 1) * RET_TQ
            rows = pl.ds(n0, RET_TQ)
            qr = _rot(q_ref[rows, :].astype(F32), cos_ref[rows, :], sin_ref[rows, :]).astype(MXU_DTYPE)
            sc = lax.dot_general(qr, kr_ref[0:kmax, :], NT, preferred_element_type=F32)
            pm = (sc * _decay_mask(lg, n0, RET_TQ, kmax)).astype(MXU_DTYPE)
            o_ref[rows, :] = lax.dot_general(pm, v_ref[0:kmax, :], NN, preferred_element_type=F32)

    return pl.pallas_call(
        body, name=name, grid=(nb, h), in_specs=_ret_specs(s),
        out_specs=pl.BlockSpec((s, dv), lambda b, hh: (b, hh)),
        out_shape=jax.ShapeDtypeStruct((nb * s, h * dv), F32),
        scratch_shapes=[pltpu.VMEM((s, dk), MXU_DTYPE)],
        compiler_params=_params(("parallel", "parallel")),
    )(p, p, p, cos2, sin2, log_g)


def retention_bwd(name, p, cos2, sin2, log_g, do, nb, s):
    dk, dv, h = RET_QK_DIM, RET_V_DIM, RET_HEADS

    def body(q_ref, k_ref, v_ref, cos_ref, sin_ref, lg_ref, do_ref, dq_ref, dk_ref, dv_ref, kr_ref, dk_acc, dv_acc):
        lg = lg_ref[0:1, 0:1]
        kr = _rot(k_ref[...].astype(F32), cos_ref[...], sin_ref[...]) * (dk ** -0.5)
        kr_ref[...] = kr.astype(kr_ref.dtype)
        dk_acc[...] = jnp.zeros_like(dk_acc)
        dv_acc[...] = jnp.zeros_like(dv_acc)
        for qi in range(s // RET_TQ):
            n0, kmax = qi * RET_TQ, (qi + 1) * RET_TQ
            rows = pl.ds(n0, RET_TQ)
            cq, sq = cos_ref[rows, :], sin_ref[rows, :]
            qr = _rot(q_ref[rows, :].astype(F32), cq, sq).astype(MXU_DTYPE)
            dob = do_ref[rows, :]
            mask = _decay_mask(lg, n0, RET_TQ, kmax)
            sc = lax.dot_general(qr, kr_ref[0:kmax, :], NT, preferred_element_type=F32)
            pm = (sc * mask).astype(MXU_DTYPE)
            dv_acc[0:kmax, :] += lax.dot_general(pm, dob, TN, preferred_element_type=F32)
            dp = lax.dot_general(dob, v_ref[0:kmax, :], NT, preferred_element_type=F32)
            ds = (dp * mask).astype(MXU_DTYPE)
            dqr = lax.dot_general(ds, kr_ref[0:kmax, :], NN, preferred_element_type=F32)
            dq_ref[rows, :] = _rot(dqr, cq, -sq).astype(dq_ref.dtype)
            dk_acc[0:kmax, :] += lax.dot_general(ds, qr, TN, preferred_element_type=F32)
        dkr = dk_acc[...] * (dk ** -0.5)
        dk_ref[...] = _rot(dkr, cos_ref[...], -sin_ref[...]).astype(dk_ref.dtype)
        dv_ref[...] = dv_acc[...].astype(dv_ref.dtype)

    t = nb * s
    return pl.pallas_call(
        body, name=name, grid=(nb, h),
        in_specs=_ret_specs(s) + [pl.BlockSpec((s, dv), lambda b, hh: (b, hh))],
        out_specs=[pl.BlockSpec((s, dk), lambda b, hh: (b, hh)), pl.BlockSpec((s, dk), lambda b, hh: (b, hh)),
                   pl.BlockSpec((s, dv), lambda b, hh: (b, hh))],
        out_shape=[jax.ShapeDtypeStruct((t, h * dk), MXU_DTYPE), jax.ShapeDtypeStruct((t, h * dk), MXU_DTYPE),
                   jax.ShapeDtypeStruct((t, h * dv), MXU_DTYPE)],
        scratch_shapes=[pltpu.VMEM((s, dk), MXU_DTYPE), pltpu.VMEM((s, dk), F32), pltpu.VMEM((s, dv), F32)],
        compiler_params=_params(("parallel", "parallel")),
    )(p, p, p, cos2, sin2, log_g, do)


def _conv_grid(t, d, nb, ts):
    s = t // nb
    ns, nc = s // ts, d // CONV_TC
    return s, ns, nc


def _shifted(pad_ref, sh_ref, offsets):
    n = sh_ref.shape[1]
    for b in sorted({off % SUBLANES for off in offsets} - {0}):
        sh_ref[b - 1] = pad_ref[pl.ds(b, n), :]

    def read(off, r0):
        a, b = off - off % SUBLANES + r0, off % SUBLANES
        return pad_ref[pl.ds(a, SUBLANES), :] if b == 0 else sh_ref[b - 1, pl.ds(a, SUBLANES), :]

    return read


def _causal_taps(pad_ref, sh_ref, w_ref, k, emit):
    offs = [CONV_PAD - (k - 1) + j for j in range(k)]
    read = _shifted(pad_ref, sh_ref, offs)
    for r0 in range(0, pad_ref.shape[0] - CONV_PAD, CONV_ROWS):
        accs = [None] * len(CONV_TILES)
        for j in range(k):
            wj = w_ref[j]
            for q, dr in enumerate(CONV_TILES):
                term = wj * read(offs[j], r0 + dr)
                accs[q] = term if accs[q] is None else accs[q] + term
        emit(r0, jnp.concatenate(accs, axis=0))


def _tap_tiles(w):
    return jnp.broadcast_to(w[:, None, :], (w.shape[0], SUBLANES, w.shape[1]))


def _tap_spec(k):
    return pl.BlockSpec((k, SUBLANES, CONV_TC), lambda c, b, si: (0, 0, c))


def _carry_past(pad_ref, s_idx):
    ts = pad_ref.shape[0] - CONV_PAD

    @pl.when(s_idx == 0)
    def _():
        pad_ref[0:CONV_PAD, :] = jnp.zeros((CONV_PAD, pad_ref.shape[1]), F32)

    @pl.when(s_idx > 0)
    def _():
        pad_ref[0:CONV_PAD, :] = pad_ref[ts:ts + CONV_PAD, :]


def _carry_future(pad_ref, s_idx):
    ts = pad_ref.shape[0] - CONV_PAD

    @pl.when(s_idx == 0)
    def _():
        pad_ref[ts:ts + CONV_PAD, :] = jnp.zeros((CONV_PAD, pad_ref.shape[1]), F32)

    @pl.when(s_idx > 0)
    def _():
        pad_ref[ts:ts + CONV_PAD, :] = pad_ref[0:CONV_PAD, :]


def _conv_bwd_taps(pad_ref, sh_ref, w_ref, dw_acc, k, x_rows, emit, mix):
    read = _shifted(pad_ref, sh_ref, range(k))
    for r0 in range(0, pad_ref.shape[0] - CONV_PAD, CONV_ROWS):
        ops = x_rows(r0)
        x = mix(ops)
        accs = [None] * len(CONV_TILES)
        for j in range(k):
            wj, dwj = w_ref[j], None
            for q, dr in enumerate(CONV_TILES):
                sh = read(k - 1 - j, r0 + dr)
                term = wj * sh
                accs[q] = term if accs[q] is None else accs[q] + term
                prod = x[dr:dr + SUBLANES] * sh
                dwj = prod if dwj is None else dwj + prod
            dw_acc[j] += dwj
        emit(r0, ops, jnp.concatenate(accs, axis=0))


def _conv_bwd_edges(dw_acc, dw_ref, nb, ns, extra=()):
    first = jnp.logical_and(pl.program_id(1) == 0, pl.program_id(2) == 0)
    last = jnp.logical_and(pl.program_id(1) == nb - 1, pl.program_id(2) == ns - 1)

    @pl.when(first)
    def _():
        dw_acc[...] = jnp.zeros_like(dw_acc)
        for r in extra:
            r[...] = jnp.zeros_like(r)

    def finish():
        @pl.when(last)
        def _():
            dw_ref[...] = jnp.sum(dw_acc[...], axis=1)

    return finish


def short_conv_fwd(name, p, blk_b, w, nb):
    t = p.shape[0]
    d = w.shape[1]
    ts = SC_TS
    s, ns, nc = _conv_grid(t, d, nb, ts)
    cb = d // CONV_TC

    def body(b_ref, c_ref, x_ref, w_ref, y_ref, cz_ref, pad_ref, sh_ref):
        _carry_past(pad_ref, pl.program_id(2))
        pad_ref[CONV_PAD:CONV_PAD + ts, :] = c_ref[...].astype(F32) * x_ref[...].astype(F32)

        def emit(r0, cz):
            rows = pl.ds(r0, CONV_ROWS)
            cz_ref[rows, :] = cz
            y_ref[rows, :] = (b_ref[rows, :].astype(F32) * cz).astype(y_ref.dtype)

        _causal_taps(pad_ref, sh_ref, w_ref, SC_KERNEL, emit)

    def pspec(off):
        return pl.BlockSpec((ts, CONV_TC), lambda c, b, si: (b * ns + si, (blk_b + off) * cb + c))

    ospec = pl.BlockSpec((ts, CONV_TC), lambda c, b, si: (b * ns + si, c))
    return pl.pallas_call(
        body, name=name, grid=(nc, nb, ns),
        in_specs=[pspec(0), pspec(1), pspec(2), _tap_spec(SC_KERNEL)],
        out_specs=[ospec, ospec],
        out_shape=[jax.ShapeDtypeStruct((t, d), MXU_DTYPE), jax.ShapeDtypeStruct((t, d), F32)],
        scratch_shapes=_conv_scratch(ts),
        compiler_params=_params(("parallel", "arbitrary", "arbitrary")),
    )(p, p, p, _tap_tiles(w))


def short_conv_bwd(name, p, blk_b, w, cz, dy, nb):
    t = p.shape[0]
    d = w.shape[1]
    ts = SC_TS
    s, ns, nc = _conv_grid(t, d, nb, ts)
    cb = d // CONV_TC

    def body(b_ref, c_ref, x_ref, w_ref, cz_ref, dy_ref, db_ref, dc_ref, dx_ref, dw_ref, pad_ref, sh_ref, dw_acc):
        _carry_future(pad_ref, pl.program_id(2))
        dyv = dy_ref[...].astype(F32)
        db_ref[...] = (dyv * cz_ref[...]).astype(db_ref.dtype)
        pad_ref[0:ts, :] = dyv * b_ref[...].astype(F32)
        finish = _conv_bwd_edges(dw_acc, dw_ref, nb, ns)

        def x_rows(r0):
            rows = pl.ds(r0, CONV_ROWS)
            return c_ref[rows, :].astype(F32), x_ref[rows, :].astype(F32)

        def emit(r0, cx, dz):
            rows = pl.ds(r0, CONV_ROWS)
            dc_ref[rows, :] = (dz * cx[1]).astype(dc_ref.dtype)
            dx_ref[rows, :] = (dz * cx[0]).astype(dx_ref.dtype)

        _conv_bwd_taps(pad_ref, sh_ref, w_ref, dw_acc, SC_KERNEL, x_rows, emit, lambda cx: cx[0] * cx[1])
        finish()

    def row(b, si):
        return b * ns + (ns - 1 - si)

    def pspec(off):
        return pl.BlockSpec((ts, CONV_TC), lambda c, b, si: (row(b, si), (blk_b + off) * cb + c))

    ospec = pl.BlockSpec((ts, CONV_TC), lambda c, b, si: (row(b, si), c))
    wspec = pl.BlockSpec((SC_KERNEL, CONV_TC), lambda c, b, si: (0, c))
    return pl.pallas_call(
        body, name=name, grid=(nc, nb, ns),
        in_specs=[pspec(0), pspec(1), pspec(2), _tap_spec(SC_KERNEL), ospec, ospec],
        out_specs=[ospec, ospec, ospec, wspec],
        out_shape=[jax.ShapeDtypeStruct((t, d), MXU_DTYPE)] * 3 + [jax.ShapeDtypeStruct((SC_KERNEL, d), F32)],
        scratch_shapes=_conv_scratch(ts) + [pltpu.VMEM((SC_KERNEL, SUBLANES, CONV_TC), F32)],
        compiler_params=_params(("parallel", "arbitrary", "arbitrary")),
    )(p, p, p, _tap_tiles(w), cz, dy)


def conformer_conv_fwd(name, p, blk_a, w, bias, nb):
    t = p.shape[0]
    d = w.shape[1]
    ts = CONV_TS
    s, ns, nc = _conv_grid(t, d, nb, ts)
    cb = d // CONV_TC

    def body(a_ref, b_ref, w_ref, bias_ref, u_ref, pad_ref, sh_ref):
        _carry_past(pad_ref, pl.program_id(2))
        pad_ref[CONV_PAD:CONV_PAD + ts, :] = a_ref[...].astype(F32) * jax.nn.sigmoid(b_ref[...].astype(F32))

        def emit(r0, u):
            u_ref[pl.ds(r0, CONV_ROWS), :] = u + bias_ref[0:1, :]

        _causal_taps(pad_ref, sh_ref, w_ref, CF_KERNEL, emit)

    def pspec(off):
        return pl.BlockSpec((ts, CONV_TC), lambda c, b, si: (b * ns + si, (blk_a + off) * cb + c))

    return pl.pallas_call(
        body, name=name, grid=(nc, nb, ns),
        in_specs=[pspec(0), pspec(1), _tap_spec(CF_KERNEL), pl.BlockSpec((SUBLANES, CONV_TC), lambda c, b, si: (0, c))],
        out_specs=pl.BlockSpec((ts, CONV_TC), lambda c, b, si: (b * ns + si, c)),
        out_shape=jax.ShapeDtypeStruct((t, d), F32),
        scratch_shapes=_conv_scratch(ts),
        compiler_params=_params(("parallel", "arbitrary", "arbitrary")),
    )(p, p, _tap_tiles(w), jnp.broadcast_to(bias, (SUBLANES, d)))


def conformer_conv_bwd(name, p, blk_a, w, du, nb):
    t = p.shape[0]
    d = w.shape[1]
    ts = CONV_TS
    s, ns, nc = _conv_grid(t, d, nb, ts)
    cb = d // CONV_TC

    def body(a_ref, b_ref, w_ref, du_ref, da_ref, db_ref, dw_ref, dbias_ref, pad_ref, sh_ref, dw_acc):
        _carry_future(pad_ref, pl.program_id(2))
        duv = du_ref[...]
        pad_ref[0:ts, :] = duv
        finish = _conv_bwd_edges(dw_acc, dw_ref, nb, ns, extra=(dbias_ref,))
        dbias_ref[...] += jnp.sum(duv, axis=0, keepdims=True)

        def x_rows(r0):
            rows = pl.ds(r0, CONV_ROWS)
            return a_ref[rows, :].astype(F32), jax.nn.sigmoid(b_ref[rows, :].astype(F32))

        def emit(r0, asg, du0):
            rows = pl.ds(r0, CONV_ROWS)
            av, sg = asg
            da_ref[rows, :] = (du0 * sg).astype(da_ref.dtype)
            db_ref[rows, :] = (du0 * av * sg * (1.0 - sg)).astype(db_ref.dtype)

        _conv_bwd_taps(pad_ref, sh_ref, w_ref, dw_acc, CF_KERNEL, x_rows, emit, lambda asg: asg[0] * asg[1])
        finish()

    def row(b, si):
        return b * ns + (ns - 1 - si)

    def pspec(off):
        return pl.BlockSpec((ts, CONV_TC), lambda c, b, si: (row(b, si), (blk_a + off) * cb + c))

    ospec = pl.BlockSpec((ts, CONV_TC), lambda c, b, si: (row(b, si), c))
    wspec = pl.BlockSpec((CF_KERNEL, CONV_TC), lambda c, b, si: (0, c))
    bspec = pl.BlockSpec((1, CONV_TC), lambda c, b, si: (0, c))
    return pl.pallas_call(
        body, name=name, grid=(nc, nb, ns),
        in_specs=[pspec(0), pspec(1), _tap_spec(CF_KERNEL), ospec],
        out_specs=[ospec, ospec, wspec, bspec],
        out_shape=[jax.ShapeDtypeStruct((t, d), MXU_DTYPE)] * 2
        + [jax.ShapeDtypeStruct((CF_KERNEL, d), F32), jax.ShapeDtypeStruct((1, d), F32)],
        scratch_shapes=_conv_scratch(ts) + [pltpu.VMEM((CF_KERNEL, SUBLANES, CONV_TC), F32)],
        compiler_params=_params(("parallel", "arbitrary", "arbitrary")),
    )(p, p, _tap_tiles(w), du)


BLOCKS = ("ffn1", "mixer", "ffn2")
BLOCK_WEIGHTS = {"ffn1": ("ffn1_w_gu", "ffn1_w_down"), "mixer": ("w_in", "w_ret_o", "w_sc_o", "w_cf_o", "w_o"),
                 "ffn2": ("ffn2_w_gu", "ffn2_w_down")}
BIG = BLOCK_WEIGHTS["ffn1"] + BLOCK_WEIGHTS["mixer"] + BLOCK_WEIGHTS["ffn2"]
MODE = {"ffn1_w_gu": "col", "ffn1_w_down": "row", "w_in": "col", "w_ret_o": "row", "w_sc_o": "row",
        "w_cf_o": "row", "w_o": "row", "ffn2_w_gu": "col", "ffn2_w_down": "row"}
NORM_OF = {"ffn1": 0, "mixer": 2, "ffn2": 4}
BLK_GATE, BLK_SCB, BLK_CFA, BLK_MERGE = 2, 3, 6, 8


def _rope_tables(positions):
    half = RET_QK_DIM // 2
    inv_freq = ROPE_BASE ** (-jnp.arange(half, dtype=F32) / half)
    ang = positions.astype(F32)[..., None] * inv_freq
    cos, sin = jnp.cos(ang), jnp.sin(ang)
    nb, s = positions.shape
    cos2 = jnp.concatenate([cos, cos], axis=-1).reshape(nb * s, RET_QK_DIM)
    sin2 = jnp.concatenate([-sin, sin], axis=-1).reshape(nb * s, RET_QK_DIM)
    return cos2, sin2


def _log_gamma():
    lg = jnp.log(1.0 - 2.0 ** (-5.0 - jnp.arange(RET_HEADS, dtype=F32)))
    return jnp.broadcast_to(lg[:, None, None], (RET_HEADS, 1, RET_QK_DIM))


def _ffn_fwd(xs, h, w, tag, g_post, g_next):
    gu, a = ffn_up("ffn_up", h, w[tag + "_w_gu"])
    y, out, h_next = mm_post("ffn_down", a, w[tag + "_w_down"], xs, g_post, 0.5, g_next)
    return out, h_next, dict(x=xs, h=h, gu=gu, a=a, y=y, w=w)


def _ffn_bwd(dxs, dy, sv, tag, g_pre, push, prev):
    w = sv["w"]
    gu_w, down_w = w[tag + "_w_gu"], w[tag + "_w_down"]
    dgu = ffn_down_dx("ffn_down_dx", dy, down_w, sv["gu"])
    grads = {tag + "_w_down": mm_dw("ffn_down_dw", sv["a"], dy, "row", down_w.shape),
             tag + "_w_gu": mm_dw("ffn_gu_dw", sv["h"], dgu, "col", gu_w.shape)}
    return mm_dx_norms("ffn_gu_dx", dgu, gu_w, sv["x"], g_pre, dxs, prev, push(grads))


def _mixer_fwd(xs, h, w, sm, g_post, g_next, rope, nb, s, mid):
    cos2, sin2, log_g = rope
    d = xs.shape[1]
    gate_blk = (BLK_GATE * d) // RET_V_DIM
    p = mm_fwd("mx_in", h, w["w_in"], "col", MXU_DTYPE)
    if mid is not None:
        sm = dict(sm, cf_dw_b=sm["cf_dw_b"] + mid(p))
    o = retention_fwd("ret_fwd", p, cos2, sin2, log_g, nb, s)
    ya_in = head_gate_fwd("ret_gate", o, p, gate_blk)
    yb_in, cz = short_conv_fwd("sc_fwd", p, BLK_SCB, sm["sc_conv_w"], nb)
    u1 = conformer_conv_fwd("cf_fwd", p, BLK_CFA, sm["cf_dw_w"], sm["cf_dw_b"], nb)
    yc_in = ln_silu_fwd("cf_ln", u1, sm["cf_ln_g"], sm["cf_ln_b"])
    ya = mm_fwd("mx_proj", ya_in, w["w_ret_o"], "row", F32)
    yb = mm_fwd("mx_proj", yb_in, w["w_sc_o"], "row", F32)
    yc = mm_fwd("mx_proj", yc_in, w["w_cf_o"], "row", F32)
    mg = merge_fwd("mx_merge", p, BLK_MERGE, ya, yb, yc)
    m, out, h_next = mm_post("mx_out", mg, w["w_o"], xs, g_post, 1.0, g_next)
    return out, h_next, dict(x=xs, h=h, p=p, o=o, ya_in=ya_in, yb_in=yb_in, cz=cz, u1=u1, yc_in=yc_in, ya=ya, yb=yb, yc=yc,
                     mg=mg, m=m, w=w)


def _mixer_bwd(dxs, dm, sv, sm, g_pre, rope, nb, s, push, prev):
    cos2, sin2, log_g = rope
    w, p = sv["w"], sv["p"]
    d = dxs.shape[1]
    gate_blk = (BLK_GATE * d) // RET_V_DIM
    grads, gsm = {}, {}

    def proj_bwd(wname, a_in, dy, out_dtype):
        grads[wname] = mm_dw("mx_proj_dw", a_in, dy, "row", w[wname].shape)
        return mm_dx("mx_proj_dx", dy, w[wname], "row", out_dtype)

    dmg = proj_bwd("w_o", sv["mg"], dm, MXU_DTYPE)
    dg0, dg1, dg2, dya, dyb, dyc = merge_bwd("mx_merge_bwd", p, BLK_MERGE, sv["ya"], sv["yb"], sv["yc"], dmg)
    dya_in = proj_bwd("w_ret_o", sv["ya_in"], dya, MXU_DTYPE)
    dyb_in = proj_bwd("w_sc_o", sv["yb_in"], dyb, MXU_DTYPE)
    dyc_in = proj_bwd("w_cf_o", sv["yc_in"], dyc, MXU_DTYPE)
    do, dgret = head_gate_bwd("ret_gate_bwd", sv["o"], p, gate_blk, dya_in)
    dq, dk, dv = retention_bwd("ret_bwd", p, cos2, sin2, log_g, do, nb, s)
    dscb, dscc, dscx, gsm["sc_conv_w"] = short_conv_bwd("sc_bwd", p, BLK_SCB, sm["sc_conv_w"], sv["cz"], dyb_in, nb)
    du1, dlg, dlb = ln_silu_bwd("cf_ln_bwd", sv["u1"], sm["cf_ln_g"], sm["cf_ln_b"], dyc_in)
    dcfa, dcfb, gsm["cf_dw_w"], dbias = conformer_conv_bwd("cf_bwd", p, BLK_CFA, sm["cf_dw_w"], du1, nb)
    gsm.update(cf_ln_g=dlg[0], cf_ln_b=dlb[0], cf_dw_b=dbias[0])
    dp = concat_cols("mx_dp", [dq, dk, dv, dgret, dscb, dscc, dscx, dcfa, dcfb, dg0, dg1, dg2])
    grads["w_in"] = mm_dw("mx_in_dw", sv["h"], dp, "col", w["w_in"].shape)
    return mm_dx_norms("mx_in_dx", dp, w["w_in"], sv["x"], g_pre, dxs, prev, push(grads)) + (gsm,)


def local_step(x, positions, target, small, fetch, push):
    nb, s, d = x.shape
    t = nb * s
    depth = small["norm_g"].shape[0]
    rope = _rope_tables(positions) + (_log_gamma(),)
    xs = x.reshape(t, d)
    token = [None]

    def gain(l, i):
        g = small["norm_g"][l, i][None, :]
        if token[0] is not None:
            g, token[0] = g + token[0], None
        return g

    def mixer_small(l):
        return dict(sc_conv_w=small["sc_conv_w"][l], cf_dw_w=small["cf_dw_w"][l], cf_dw_b=small["cf_dw_b"][l][None, :],
                    cf_ln_g=small["cf_ln_g"][l][None, :], cf_ln_b=small["cf_ln_b"][l][None, :])

    saved = {}
    order = [(l, blk) for l in range(depth) for blk in BLOCKS]
    h = None
    for at, (l, blk) in enumerate(order):
        w, token[0], mid = fetch(l, blk, xs)
        i0 = NORM_OF[blk]
        if h is None:
            h = rms_fwd("first_rms", xs, gain(l, i0))
        g_post = gain(l, i0 + 1)
        g_next = gain(order[at + 1][0], NORM_OF[order[at + 1][1]]) if at + 1 < len(order) else None
        if blk == "mixer":
            xs, h, saved[l, blk] = _mixer_fwd(xs, h, w, mixer_small(l), g_post, g_next, rope, nb, s, mid)
        else:
            xs, h, saved[l, blk] = _ffn_fwd(xs, h, w, blk, g_post, g_next)

    dxs, loss = loss_head("loss", xs, target.reshape(t, d))

    dnorm = [[None] * 6 for _ in range(depth)]
    gsmall = {n: [None] * depth for n in ("sc_conv_w", "cf_dw_w", "cf_dw_b", "cf_ln_g", "cf_ln_b")}
    def branch(group):
        l, blk = group
        sv = saved[group]
        return (sv["m"], gain(l, NORM_OF[blk] + 1), 1.0) if blk == "mixer" else (sv["y"], gain(l, NORM_OF[blk] + 1), 0.5)

    l, blk = order[-1]
    y, g_post, scale = branch(order[-1])
    dy, dnorm[l][NORM_OF[blk] + 1] = post_bwd("last_post_bwd", y, g_post, dxs, scale)
    for at in reversed(range(len(order))):
        l, blk = order[at]
        i0 = NORM_OF[blk]
        prev = branch(order[at - 1]) if at > 0 else None
        put = functools.partial(push, l, blk)
        if blk == "mixer":
            dxs, dnorm[l][i0], dy, dg_prev, gsm = _mixer_bwd(
                dxs, dy, saved[l, blk], mixer_small(l), gain(l, i0), rope, nb, s, put, prev)
            for n, v in gsm.items():
                gsmall[n][l] = v
        else:
            dxs, dnorm[l][i0], dy, dg_prev = _ffn_bwd(dxs, dy, saved[l, blk], blk, gain(l, i0), put, prev)
        if at > 0:
            dnorm[order[at - 1][0]][NORM_OF[order[at - 1][1]] + 1] = dg_prev

    gs = {n: jnp.stack(v) for n, v in gsmall.items()}
    gs["norm_g"] = jnp.stack([jnp.concatenate(r, axis=0) for r in dnorm])
    return loss, dxs.reshape(nb, s, d), gs


ANY = pl.BlockSpec(memory_space=pl.ANY)
HBM = pl.BlockSpec(memory_space=pltpu.HBM)
SEM = pl.BlockSpec(memory_space=pltpu.SEMAPHORE)
VMEM_WHOLE = pl.BlockSpec(memory_space=pltpu.VMEM)
EFFECT = pltpu.SideEffectType.DATAFLOW_SIDE_EFFECTING
TOKEN = jax.ShapeDtypeStruct((8, 128), F32)


def _other_chips(x, y):
    return [(1 - x, y), (x, 1 - y), (1 - x, 1 - y)]


def _remote(src, dst, send_sem, recv_sem, to):
    return pltpu.make_async_remote_copy(src_ref=src, dst_ref=dst, send_sem=send_sem, recv_sem=recv_sem,
                                        device_id=to, device_id_type=MESH)


def _in_hbm(v):
    return pltpu.with_memory_space_constraint(v, pltpu.HBM)


def place_quarters(ws, layer, ids, after):
    m = len(ws)

    def body(ids_ref, *refs):
        for w_ref, o_ref in zip(refs[:m], refs[m + 1:]):
            o_ref[...] = w_ref[...].astype(o_ref.dtype)

    def spec(w, where):
        return pl.BlockSpec((None, w.shape[1] // STREAM_STEPS, w.shape[2]), where)

    return pl.pallas_call(
        body, name="place_quarters",
        grid_spec=pltpu.PrefetchScalarGridSpec(
            num_scalar_prefetch=1, grid=(STREAM_STEPS,),
            in_specs=[spec(w, lambda i, ids_ref: (layer, i, 0)) for w in ws] + [ANY],
            out_specs=[spec(w, lambda i, ids_ref: (ids_ref[0], i, 0)) for w in ws]),
        out_shape=[jax.ShapeDtypeStruct((N_CHIP,) + w.shape[1:], MXU_DTYPE) for w in ws],
        compiler_params=_params(("parallel",)),
    )(ids, *ws, after)


def _gather_copies(lands, send, recv):
    x, y, c = _axes()
    me = 2 * x + y
    mine, theirs = [], []
    for a, ld in enumerate(lands):
        rh = ld.shape[1] // 2
        rows = pl.ds(c * rh, rh)
        for k, (px, py) in enumerate(_other_chips(x, y)):
            to = (px, py, c)
            mine.append(_remote(ld.at[me, rows, :], ld.at[me, rows, :], send.at[3 * a + k], recv.at[3 * a + k], to))
            got = ld.at[2 * px + py, rows, :]
            theirs.append(_remote(got, got, send.at[3 * a + k], recv.at[3 * a + k], to))
    return mine, theirs


def gather_start(name, groups, after):
    flat = [s for g in groups for s in g]
    n, ng = len(flat), len(groups)
    sizes = [len(g) for g in groups]

    def body(*refs):
        lands = refs[:n]
        sems = refs[n + 1:n + 1 + 2 * ng]
        token = refs[-1]
        at = 0
        for g, m in enumerate(sizes):
            mine, _ = _gather_copies(lands[at:at + m], sems[2 * g], sems[2 * g + 1])
            for cp in mine:
                cp.start()
            at += m
        token[...] = jnp.zeros_like(token)

    sem_shapes = []
    for m in sizes:
        sem_shapes += [pltpu.SemaphoreType.DMA((3 * m,))] * 2
    res = pl.pallas_call(
        body, name=name, in_specs=[HBM] * n + [ANY],
        out_specs=[SEM] * (2 * ng) + [HBM] * n + [VMEM_WHOLE],
        out_shape=sem_shapes + [pltpu.HBM(s.shape, s.dtype) for s in flat] + [TOKEN],
        input_output_aliases={i: 2 * ng + i for i in range(n)},
        compiler_params=pltpu.CompilerParams(has_side_effects=EFFECT),
    )(*[_in_hbm(s) for s in flat], after)
    sems, thru, token = res[:2 * ng], res[2 * ng:2 * ng + n], res[-1]
    out, at = [], 0
    for g, m in enumerate(sizes):
        out.append((sems[2 * g], sems[2 * g + 1], thru[at:at + m]))
        at += m
    return out, token


def gather_wait(lands, send, recv, after):
    m = len(lands)

    def body(*refs):
        mine, theirs = _gather_copies(refs[:m], refs[m], refs[m + 1])
        for cp in mine:
            cp.wait_send()
        for cp in theirs:
            cp.wait_recv()

    return pl.pallas_call(
        body, name="gather_wait", in_specs=[HBM] * m + [SEM, SEM, ANY], out_specs=[HBM] * m,
        out_shape=[pltpu.HBM(l.shape, l.dtype) for l in lands],
        input_output_aliases={i: i for i in range(m)},
        compiler_params=pltpu.CompilerParams(has_side_effects=EFFECT),
    )(*lands, send, recv, after)


def copy_start(name, families, after=()):
    sizes = [len(f[0]) for f in families]
    n, k, nf = sum(sizes), len(after), len(families)

    def body(*refs):
        at = 0
        for f, (_, copies, _) in enumerate(families):
            for cp in copies(refs[at:at + sizes[f]], refs[n + k + 2 * f], refs[n + k + 2 * f + 1])[0]:
                cp.start()
            at += sizes[f]
        refs[-1][...] = jnp.zeros_like(refs[-1])

    flat = [b for f in families for b in f[0]]
    sems = [pltpu.SemaphoreType.DMA((f[2],)) for f in families for _ in range(2)]
    res = pl.pallas_call(
        body, name=name, in_specs=[HBM] * n + [ANY] * k, out_specs=[SEM] * (2 * nf) + [HBM] * n + [VMEM_WHOLE],
        out_shape=sems + [pltpu.HBM(b.shape, b.dtype) for b in flat] + [TOKEN],
        input_output_aliases={i: 2 * nf + i for i in range(n)},
        compiler_params=pltpu.CompilerParams(has_side_effects=EFFECT),
    )(*[_in_hbm(b) for b in flat], *after)
    out, at = [], 2 * nf
    for f in range(nf):
        out.append((res[2 * f], res[2 * f + 1], list(res[at:at + sizes[f]])))
        at += sizes[f]
    return out, res[-1]


def copy_wait(name, bufs, send, recv, copies, after=()):
    n = len(bufs)

    def body(*refs):
        mine, theirs = copies(refs[:n], refs[n], refs[n + 1])
        for cp in mine:
            cp.wait_send()
        for cp in theirs:
            cp.wait_recv()

    return list(pl.pallas_call(
        body, name=name, in_specs=[HBM] * n + [SEM, SEM] + [ANY] * len(after), out_specs=[HBM] * n,
        out_shape=[pltpu.HBM(b.shape, b.dtype) for b in bufs], input_output_aliases={i: i for i in range(n)},
        compiler_params=pltpu.CompilerParams(has_side_effects=EFFECT),
    )(*bufs, send, recv, *after))


def _fill_copies(lands, send, recv):
    x, y, c = _axes()
    sib = (x, y, 1 - c)
    mine, theirs = [], []
    for a, ld in enumerate(lands):
        rh = ld.shape[1] // 2
        for k, (px, py) in enumerate(_other_chips(x, y)):
            got = ld.at[2 * px + py, pl.ds(c * rh, rh), :]
            mine.append(_remote(got, got, send.at[3 * a + k], recv.at[3 * a + k], sib))
            blk = ld.at[2 * px + py, pl.ds((1 - c) * rh, rh), :]
            theirs.append(_remote(blk, blk, send.at[3 * a + k], recv.at[3 * a + k], sib))
    return mine, theirs


def _presum_copies(grads, lands, send, recv):
    x, y, c = _axes()
    cps = []
    for a, (g, ld) in enumerate(zip(grads, lands)):
        rh = g.shape[1] // 2
        cps.append(_remote(g.at[:, pl.ds((1 - c) * rh, rh), :], ld, send.at[a], recv.at[a], (x, y, 1 - c)))
    return cps


def presum_wait(grads, lands, send, recv, after):
    m = len(grads)

    def body(*refs):
        for cp in _presum_copies(refs[:m], refs[m:2 * m], refs[2 * m], refs[2 * m + 1]):
            cp.wait_send()
            cp.wait_recv()

    res = pl.pallas_call(
        body, name="presum_wait", in_specs=[HBM] * (2 * m) + [SEM, SEM] + [ANY] * len(after),
        out_specs=[HBM] * (2 * m),
        out_shape=[pltpu.HBM(g.shape, g.dtype) for g in grads] + [pltpu.HBM(l.shape, l.dtype) for l in lands],
        input_output_aliases={i: i for i in range(2 * m)},
        compiler_params=pltpu.CompilerParams(has_side_effects=EFFECT),
    )(*grads, *lands, send, recv, *after)
    return res[:m], res[m:]


def add_halves(gs, lands, ids):
    m = len(gs)

    def body(ids_ref, *refs):
        for a_ref, b_ref, o_ref in zip(refs[:m], refs[m:2 * m], refs[2 * m:]):
            o_ref[...] = (a_ref[...].astype(F32) + b_ref[...].astype(F32)).astype(o_ref.dtype)

    def spec(ld, where):
        return pl.BlockSpec((None,) + ld.shape[1:], where)

    return pl.pallas_call(
        body, name="add_halves",
        grid_spec=pltpu.PrefetchScalarGridSpec(
            num_scalar_prefetch=1, grid=(N_CHIP,),
            in_specs=[spec(ld, lambda i, ids_ref: (i, ids_ref[1], 0)) for ld in lands]
            + [spec(ld, lambda i, ids_ref: (i, 0, 0)) for ld in lands],
            out_specs=[spec(ld, lambda i, ids_ref: (i, 0, 0)) for ld in lands]),
        out_shape=[jax.ShapeDtypeStruct(ld.shape, ld.dtype) for ld in lands],
        compiler_params=_params(("parallel",)),
    )(ids, *gs, *lands)


def _scatter_copies(parts, lands, send, recv):
    x, y, c = _axes()
    cps = []
    for a, (pt, ld) in enumerate(zip(parts, lands)):
        for k, (px, py) in enumerate(_other_chips(x, y)):
            cps.append(_remote(pt.at[2 * px + py], ld.at[k], send.at[3 * a + k], recv.at[3 * a + k], (px, py, c)))
    return cps


def scatter_wait(parts, lands, send, recv, after):
    m = len(parts)

    def body(*refs):
        for cp in _scatter_copies(refs[:m], refs[m:2 * m], refs[2 * m], refs[2 * m + 1]):
            cp.wait_send()
            cp.wait_recv()

    res = pl.pallas_call(
        body, name="scatter_wait", in_specs=[HBM] * (2 * m) + [SEM, SEM] + [ANY] * len(after),
        out_specs=[HBM] * (2 * m),
        out_shape=[pltpu.HBM(p.shape, p.dtype) for p in parts] + [pltpu.HBM(l.shape, l.dtype) for l in lands],
        input_output_aliases={i: i for i in range(2 * m)},
        compiler_params=pltpu.CompilerParams(has_side_effects=EFFECT),
    )(*parts, *lands, send, recv, *after)
    return res[:m], res[m:]


def sum_partials(parts, lands, ids, layer, depth, intos):
    m = len(parts)
    nt = STREAM_STEPS

    def body(ids_ref, *refs):
        for p_ref, l_ref, o_ref in zip(refs[:m], refs[m:2 * m], refs[-m:]):
            acc = p_ref[...].astype(F32)
            for k in range(N_CHIP - 1):
                acc = acc + l_ref[k].astype(F32)
            o_ref[...] = acc

    def rows(p):
        return p.shape[1] // nt

    in_specs = [pl.BlockSpec((None, rows(p), p.shape[2]), lambda i, ids_ref: (ids_ref[0], i, 0)) for p in parts]
    in_specs += [pl.BlockSpec((N_CHIP - 1, rows(p), p.shape[2]), lambda i, ids_ref: (0, i, 0)) for p in parts]
    args = [ids, *parts, *lands]
    aliases = {}
    if intos is not None:
        in_specs += [ANY] * m
        args += list(intos)
        aliases = {1 + 2 * m + a: a for a in range(m)}
    return pl.pallas_call(
        body, name="sum_partials",
        grid_spec=pltpu.PrefetchScalarGridSpec(
            num_scalar_prefetch=1, grid=(nt,), in_specs=in_specs,
            out_specs=[pl.BlockSpec((None, rows(p), p.shape[2]), lambda i, ids_ref: (layer, ids_ref[1] * nt + i, 0))
                       for p in parts]),
        out_shape=[jax.ShapeDtypeStruct((depth, 2 * p.shape[1], p.shape[2]), F32) for p in parts],
        input_output_aliases=aliases, compiler_params=_params(("parallel",)),
    )(*args)


def _final_copies(layer):
    def copies(bufs, send, recv):
        x, y, c = _axes()
        sib = (x, y, 1 - c)
        mine, theirs = [], []
        for a, buf in enumerate(bufs):
            rh = buf.shape[1] // 2
            src = buf.at[layer, pl.ds(c * rh, rh), :]
            mine.append(_remote(src, src, send.at[a], recv.at[a], sib))
            dst = buf.at[layer, pl.ds((1 - c) * rh, rh), :]
            theirs.append(_remote(dst, dst, send.at[a], recv.at[a], sib))
        return mine, theirs

    return copies


def allgather_small(pk):
    def body(in_ref, out_ref, send, recv):
        x, y, c = _axes()
        me = 2 * x + y
        chips = _other_chips(x, y)
        out_ref[pl.ds(me, 1)] = in_ref[...][None]
        cps = []
        for k, (px, py) in enumerate(chips):
            cp = _remote(in_ref, out_ref.at[me], send.at[k], recv.at[k], (px, py, c))
            cp.start()
            cps.append(cp)
        for k, (px, py) in enumerate(chips):
            got = out_ref.at[2 * px + py]
            _remote(got, got, send.at[k], recv.at[k], (px, py, c)).wait_recv()
        for cp in cps:
            cp.wait_send()

    return pl.pallas_call(
        body, name="allgather_small", in_specs=[VMEM_WHOLE], out_specs=VMEM_WHOLE,
        out_shape=jax.ShapeDtypeStruct((N_CHIP,) + pk.shape, pk.dtype),
        scratch_shapes=[pltpu.SemaphoreType.DMA((3,))] * 2,
    )(pk)


N_DEV = 8


def _small_copies(bufs, send, recv):
    g, slots = bufs
    x, y, c = _axes()
    me = 4 * x + 2 * y + c
    mine, theirs = [], []
    for mask in range(1, N_DEV):
        px = 1 - x if mask & 4 else x
        py = 1 - y if mask & 2 else y
        pc = 1 - c if mask & 1 else c
        mine.append(_remote(g, slots.at[me], send.at[mask - 1], recv.at[mask - 1], (px, py, pc)))
        got = slots.at[4 * px + 2 * py + pc]
        theirs.append(_remote(got, got, send.at[mask - 1], recv.at[mask - 1], (px, py, pc)))
    return mine, theirs


def sum_slots(g, slots, me):
    def body(me_ref, g_ref, slots_ref, o_ref):
        acc = None
        for d in range(N_DEV):
            term = jnp.where(me_ref[0] == d, g_ref[...], slots_ref[d])
            acc = term if acc is None else acc + term
        o_ref[...] = acc

    return pl.pallas_call(
        body, name="sum_slots",
        grid_spec=pltpu.PrefetchScalarGridSpec(
            num_scalar_prefetch=1, grid=(1,),
            in_specs=[pl.BlockSpec(g.shape, lambda i, me_ref: (0, 0)),
                      pl.BlockSpec(slots.shape, lambda i, me_ref: (0, 0, 0))],
            out_specs=pl.BlockSpec(g.shape, lambda i, me_ref: (0, 0))),
        out_shape=jax.ShapeDtypeStruct(g.shape, g.dtype),
        compiler_params=_params(("arbitrary",)),
    )(me, g, slots)


def adamw(w, g, m, v, layer=None, intos=None):
    shape = w.shape
    cols = shape[-1]
    rows = int(np.prod(shape[:-1]))
    span = rows if layer is None else rows // shape[0]
    tr = span
    for cand in (256, 128):
        if span % cand == 0 and cand * cols * 4 <= 2 * 1024 * 1024:
            tr = cand
            break
    first = 0 if layer is None else layer * (span // tr)
    c1 = 1.0 - ADAM_B1 ** ADAM_STEP
    c2 = 1.0 - ADAM_B2 ** ADAM_STEP

    def body(w_ref, g_ref, m_ref, v_ref, *rest):
        d_ref, nm_ref, nv_ref, g_out = rest[-4:]
        gv = g_ref[...]
        g_out[...] = gv
        nm = ADAM_B1 * m_ref[...] + (1.0 - ADAM_B1) * gv
        nv = ADAM_B2 * v_ref[...] + (1.0 - ADAM_B2) * jnp.square(gv)
        d_ref[...] = -ADAM_LR * ((nm / c1) / (jnp.sqrt(nv / c2) + ADAM_EPS) + ADAM_WD * w_ref[...])
        nm_ref[...] = nm
        nv_ref[...] = nv

    spec = pl.BlockSpec((tr, cols), lambda i: (first + i, 0))
    args = [a.reshape(rows, cols) for a in (w, g, m, v)]
    in_specs, aliases = [spec] * 4, {}
    if intos is not None:
        args += [a.reshape(rows, cols) for a in intos]
        in_specs += [ANY] * 4
        aliases = {4 + k: k for k in range(4)}
    res = pl.pallas_call(
        body, name="adamw", grid=(span // tr,), in_specs=in_specs, out_specs=[spec] * 4,
        out_shape=[jax.ShapeDtypeStruct((rows, cols), F32)] * 4, input_output_aliases=aliases,
        compiler_params=_params(("parallel",)),
    )(*args)
    return [r.reshape(shape) for r in res]


WEIGHTS = ("norm_g", "ffn1_w_gu", "ffn1_w_down", "w_in", "w_ret_o", "sc_conv_w", "w_sc_o", "cf_dw_w", "cf_dw_b",
           "cf_ln_g", "cf_ln_b", "w_cf_o", "w_o", "ffn2_w_gu", "ffn2_w_down")
SHARDED_SMALL = ("norm_g", "sc_conv_w", "cf_dw_w")
REPLICATED_SMALL = ("cf_dw_b", "cf_ln_g", "cf_ln_b")

def _pack_rows(parts):
    padded, offs, at = [], [], 0
    for p in parts:
        r = -(-p.shape[0] // SUBLANES) * SUBLANES
        padded.append(jnp.pad(p, ((0, r - p.shape[0]), (0, 0))))
        offs.append(at)
        at += r
    return jnp.concatenate(padded, axis=0), offs


def kernel(x, positions, norm_g, ffn1_w_gu, ffn1_w_down, w_in, w_ret_o, sc_conv_w, w_sc_o, cf_dw_w, cf_dw_b, cf_ln_g, cf_ln_b, w_cf_o, w_o, ffn2_w_gu, ffn2_w_down, loss_target, m_norm_g, m_ffn1_w_gu, m_ffn1_w_down, m_w_in, m_w_ret_o, m_sc_conv_w, m_w_sc_o, m_cf_dw_w, m_cf_dw_b, m_cf_ln_g, m_cf_ln_b, m_w_cf_o, m_w_o, m_ffn2_w_gu, m_ffn2_w_down, v_norm_g, v_ffn1_w_gu, v_ffn1_w_down, v_w_in, v_w_ret_o, v_sc_conv_w, v_w_sc_o, v_cf_dw_w, v_cf_dw_b, v_cf_ln_g, v_cf_ln_b, v_w_cf_o, v_w_o, v_ffn2_w_gu, v_ffn2_w_down):
    wts = dict(zip(WEIGHTS, (norm_g, ffn1_w_gu, ffn1_w_down, w_in, w_ret_o, sc_conv_w, w_sc_o, cf_dw_w, cf_dw_b,
                             cf_ln_g, cf_ln_b, w_cf_o, w_o, ffn2_w_gu, ffn2_w_down)))
    mom = dict(zip(WEIGHTS, (m_norm_g, m_ffn1_w_gu, m_ffn1_w_down, m_w_in, m_w_ret_o, m_sc_conv_w, m_w_sc_o,
                             m_cf_dw_w, m_cf_dw_b, m_cf_ln_g, m_cf_ln_b, m_w_cf_o, m_w_o, m_ffn2_w_gu, m_ffn2_w_down)))
    var = dict(zip(WEIGHTS, (v_norm_g, v_ffn1_w_gu, v_ffn1_w_down, v_w_in, v_w_ret_o, v_sc_conv_w, v_w_sc_o,
                             v_cf_dw_w, v_cf_dw_b, v_cf_ln_g, v_cf_ln_b, v_w_cf_o, v_w_o, v_ffn2_w_gu, v_ffn2_w_down)))
    depth = norm_g.shape[0]
    dq = norm_g.shape[-1]
    d = N_CHIP * dq
    chip = 2 * lax.axis_index("x") + lax.axis_index("y")
    ids = jnp.stack([chip, lax.axis_index("c")]).astype(jnp.int32)

    pk, offs = _pack_rows([wts[n].reshape(-1, dq) for n in SHARDED_SMALL])
    gk4 = allgather_small(pk)
    gk = gk4.transpose(1, 0, 2).reshape(pk.shape[0], d)
    small = {n: wts[n] for n in REPLICATED_SMALL}
    for n, o in zip(SHARDED_SMALL, offs):
        rows = wts[n].shape[0] * wts[n].shape[1]
        small[n] = gk[o:o + rows].reshape(wts[n].shape[:2] + (d,))

    order = [(l, blk) for l in range(depth) for blk in BLOCKS]
    def placed(groups, after):
        return [place_quarters([wts[n] for n in BLOCK_WEIGHTS[blk]], l, ids, after) for l, blk in groups]

    first, token = gather_start("gather_start_first", placed(order[:1], gk4), gk4)
    rest, token = gather_start("gather_start_rest", placed(order[1:], token), token)
    started = dict(zip(order, first + rest))
    small["norm_g"] = small["norm_g"] + token[0:1, 0:1]

    filling = {}

    def fill(group, after):
        send, recv, lands = started[group]
        lands = gather_wait(lands, send, recv, after)
        started_fill, tok = copy_start("fill_start", [(lands, _fill_copies, 3 * len(lands))])
        filling[group] = started_fill[0]
        return tok[0:1, 0:1]

    def fetch(l, blk, after):
        at = order.index((l, blk))
        if (l, blk) not in filling:
            fill((l, blk), token if at == 0 else after)
        send, recv, lands = filling.pop((l, blk))
        lands = copy_wait("fill_wait", lands, send, recv, _fill_copies, (after,))
        tok, mid = None, None
        if at == 1:
            mid = functools.partial(fill, order[at + 1])
        elif 1 < at < len(order) - 1:
            tok = fill(order[at + 1], lands[0])
        return dict(zip(BLOCK_WEIGHTS[blk], lands)), tok, mid

    gsum = {n: None for n in BIG}
    presums, scatters, finals = [], [], []

    def scatter_ready(after):
        group, gl, lands, send, recv = presums.pop(0)
        gl, lands = presum_wait(gl, lands, send, recv, after)
        parts = list(add_halves(gl, lands, ids))
        m = len(parts)
        lands = [lax.empty((N_CHIP - 1,) + p.shape[1:], p.dtype) for p in parts]
        family = (parts + lands, lambda refs, sd, rv: (_scatter_copies(refs[:m], refs[m:], sd, rv),) * 2, 3 * m)
        return family, lambda sd, rv, bufs: scatters.append((group, bufs[:m], bufs[m:], sd, rv))

    def final_ready(after):
        (l, blk), parts, lands, send, recv = scatters.pop(0)
        parts, lands = scatter_wait(parts, lands, send, recv, after)
        names = BLOCK_WEIGHTS[blk]
        intos = None if gsum[names[0]] is None else [gsum[n] for n in names]
        sums = list(sum_partials(parts, lands, ids, l, depth, intos))

        def note(sd, rv, bufs):
            gsum.update(zip(names, bufs))
            finals.append((names, l, sd, rv))

        return (sums, _final_copies(l), len(sums)), note

    def start_all(name, ready, after=()):
        started, tok = copy_start(name, [family for family, _ in ready], after)
        for (_, note), (sd, rv, bufs) in zip(ready, started):
            note(sd, rv, bufs)
        return tok

    def scatter_next(after):
        return start_all("scatter_start", [scatter_ready(after)])

    def sum_next(after):
        return start_all("final_start", [final_ready(after)])

    def final_next(after):
        names, l, send, recv = finals.pop(0)
        gsum.update(zip(names, copy_wait("final_wait", [gsum[n] for n in names], send, recv, _final_copies(l), after)))

    def push(l, blk, grads):
        gl = [grads[n] for n in BLOCK_WEIGHTS[blk]]
        m = len(gl)
        lands = [lax.empty((g.shape[0], g.shape[1] // 2, g.shape[2]), g.dtype) for g in gl]
        ready = [((gl + lands, lambda refs, sd, rv: (_presum_copies(refs[:m], refs[m:], sd, rv),) * 2, m),
                  lambda sd, rv, bufs: presums.append(((l, blk), bufs[:m], bufs[m:], sd, rv)))]
        if scatters:
            ready.append(final_ready((gl[0],)))
        if presums:
            ready.append(scatter_ready((gl[0],)))
        return start_all("push_start", ready)[0:1, 0:1]

    loss, grad_x, gs = local_step(x, positions, loss_target, small, fetch, push)

    names = SHARDED_SMALL + REPLICATED_SMALL
    pg, offs = _pack_rows([gs[n].reshape(-1, d) for n in names])
    small_bufs = [pg, lax.empty((N_DEV,) + pg.shape, pg.dtype)]
    ((s_send, s_recv, s_bufs),), tok = copy_start("small_start", [(small_bufs, _small_copies, N_DEV - 1)], (grad_x,))
    tok = scatter_next((grad_x, tok))

    delta, new_m, new_v, grads = {}, {}, {}, {}

    def update(n, layer=None):
        g = gsum[n] if n in BIG else grads[n]
        prev = [delta[n], new_m[n], new_v[n], grads[n]] if layer is not None and n in delta else None
        delta[n], new_m[n], new_v[n], grads[n] = adamw(wts[n], g, mom[n], var[n], layer, prev)

    while finals and finals[0][1] > 0:
        done, l = finals[0][:2]
        final_next((tok,))
        for n in done:
            update(n, l)
    upper = tuple(delta[n] for n in BIG if n in delta)
    pg, slots = copy_wait("small_wait", s_bufs, s_send, s_recv, _small_copies, upper + (tok,))
    me = (2 * chip + lax.axis_index("c")).astype(jnp.int32).reshape(1)
    tot = sum_slots(pg, slots, me)
    for n, o in zip(names, offs):
        rows = int(np.prod(gs[n].shape[:-1]))
        full = tot[o:o + rows]
        if n in SHARDED_SMALL:
            full = lax.dynamic_slice_in_dim(full, chip * dq, dq, axis=1)
        grads[n] = full.reshape(wts[n].shape)

    for n in names:
        update(n)
    after = tuple(delta[n] for n in names)
    while scatters or finals:
        if scatters:
            after = (sum_next(after),)
        done, l = finals[0][:2]
        final_next(after)
        for n in done:
            update(n, l)
        after = tuple(delta[n] for n in done)

    loss_all = lax.psum(loss[0, 0], ("x", "y", "c"))
    return (loss_all, grad_x, *[grads[n] for n in WEIGHTS], *[delta[n] for n in WEIGHTS],
            *[new_m[n] for n in WEIGHTS], *[new_v[n] for n in WEIGHTS])
```

```python
import functools

import jax
import jax.numpy as jnp
import numpy as np
from jax import lax
from jax.experimental import pallas as pl
from jax.experimental.pallas import tpu as pltpu

F32 = jnp.float32
BF16 = jnp.bfloat16
MXU_DTYPE = BF16
VMEM_LIMIT_BYTES = 56 * 1024 * 1024
MESH = pl.DeviceIdType.MESH

N_CHIP = 4
CHUNK = 64
RET_HEADS = 4
RET_QK_DIM = 128
RET_V_DIM = 256
SC_KERNEL = 3
CF_KERNEL = 31
ROPE_BASE = 10000.0
NORM_EPS = 1e-6
LN_EPS = 1e-5
ADAM_LR = 0.001
ADAM_B1 = 0.9
ADAM_B2 = 0.999
ADAM_EPS = 1e-08
ADAM_WD = 0.01
ADAM_STEP = 10

SUBLANES = 8
CONV_PAD = 32
CONV_TS = 256
CONV_TC = 512
CONV_ROWS = 32
CONV_TILES = range(0, CONV_ROWS, SUBLANES)
SC_TS = 512


def _conv_scratch(ts):
    return [pltpu.VMEM((ts + CONV_PAD, CONV_TC), F32),
            pltpu.VMEM((SUBLANES - 1, ts + CONV_PAD - SUBLANES, CONV_TC), F32)]
RET_TQ = 512
MM_TM = 1024
MM_TN = 1536
MM_K1 = 1024
MM_W1 = 8 << 20
MM_SLICE = 256
MM_IN_BYTES = 36 << 20
STREAM_STEPS = 2


def _params(sem):
    return pltpu.CompilerParams(dimension_semantics=sem, vmem_limit_bytes=VMEM_LIMIT_BYTES)


def _axes():
    return lax.axis_index("x"), lax.axis_index("y"), lax.axis_index("c")


NN = (((1,), (0,)), ((), ()))
NT = (((1,), (1,)), ((), ()))
TN = (((0,), (0,)), ((), ()))


def _mm(name, a, b, out_shape, out_dtype, grid, a_spec, b_spec, o_spec, dims, acc_shape):
    nk = grid[2]

    def body(a_ref, b_ref, o_ref, *scratch):
        bv = b_ref[...]
        if bv.ndim == 3:
            bv = bv.reshape(-1, bv.shape[-1])
        part = lax.dot_general(a_ref[...], bv, dims, preferred_element_type=F32)

        def put(v):
            o_ref[...] = v.reshape(o_ref.shape).astype(o_ref.dtype)

        if nk == 1:
            put(part)
        else:
            acc = scratch[0]
            k = pl.program_id(2)

            @pl.when(k == 0)
            def _():
                acc[...] = part

            @pl.when(k > 0)
            def _():
                acc[...] += part

            @pl.when(k == nk - 1)
            def _():
                put(acc[...])

    scratch = [pltpu.VMEM(acc_shape, F32)] if nk > 1 else []
    return pl.pallas_call(
        body, name=name, grid=grid, in_specs=[a_spec, b_spec], out_specs=o_spec,
        out_shape=jax.ShapeDtypeStruct(out_shape, out_dtype), scratch_shapes=scratch,
        compiler_params=_params(("parallel", "parallel", "arbitrary")),
    )(a, b)


def _tile(n, target):
    best = None
    for t in range(128, min(n, target) + 1, 128):
        if n % t == 0:
            best = t
    assert best is not None, (n, target)
    return best


def _token_rows(t, width):
    tt = t
    while tt > MM_TM and tt * width * jnp.dtype(MXU_DTYPE).itemsize * 2 > MM_IN_BYTES:
        tt //= 2
    return tt


def mm_fwd(name, a, w4, mode, out_dtype):
    t = a.shape[0]
    _, r, c = w4.shape
    tm = min(t, MM_TM)
    if mode == "col":
        tn = _tile(c, MM_TN)
        npj = c // tn
        grid = (t // tm, N_CHIP * npj, 1)
        a_spec = pl.BlockSpec((tm, r), lambda i, j, k: (i, 0))
        b_spec = pl.BlockSpec((None, r, tn), lambda i, j, k: (j // npj, 0, j % npj))
        o_spec = pl.BlockSpec((tm, tn), lambda i, j, k: (i, j))
        return _mm(name, a, w4, (t, N_CHIP * c), out_dtype, grid, a_spec, b_spec, o_spec, NN, (tm, tn))
    if w4.size * w4.dtype.itemsize <= MM_W1:
        grid = (t // tm, 1, 1)
        a_spec = pl.BlockSpec((tm, N_CHIP * r), lambda i, j, k: (i, 0))
        b_spec = pl.BlockSpec((N_CHIP, r, c), lambda i, j, k: (0, 0, 0))
        o_spec = pl.BlockSpec((tm, c), lambda i, j, k: (i, 0))
        return _mm(name, a, w4, (t, c), out_dtype, grid, a_spec, b_spec, o_spec, NN, (tm, c))
    grid = (t // tm, 1, N_CHIP)
    a_spec = pl.BlockSpec((tm, r), lambda i, j, k: (i, k))
    b_spec = pl.BlockSpec((None, r, c), lambda i, j, k: (k, 0, 0))
    o_spec = pl.BlockSpec((tm, c), lambda i, j, k: (i, 0))
    return _mm(name, a, w4, (t, c), out_dtype, grid, a_spec, b_spec, o_spec, NN, (tm, c))


def mm_dx(name, dy, w4, mode, out_dtype):
    t = dy.shape[-2]
    _, r, c = w4.shape
    tm = min(t, MM_TM)
    if mode == "col":
        tn, npj = c, 1
        hb = N_CHIP // 2 * npj
        grid = (t // tm, 1, N_CHIP * npj)
        if dy.ndim == 3:
            a_spec = pl.BlockSpec((None, tm, tn), lambda i, j, k: (k // hb, i, k % hb))
        else:
            a_spec = pl.BlockSpec((tm, tn), lambda i, j, k: (i, k))
        b_spec = pl.BlockSpec((None, r, tn), lambda i, j, k: (k // npj, 0, k % npj))
        o_spec = pl.BlockSpec((tm, r), lambda i, j, k: (i, 0))
        return _mm(name, dy, w4, (t, r), out_dtype, grid, a_spec, b_spec, o_spec, NT, (tm, r))
    if N_CHIP * r <= MM_K1:
        grid = (t // tm, 1, 1)
        a_spec = pl.BlockSpec((tm, c), lambda i, j, k: (i, 0))
        b_spec = pl.BlockSpec((N_CHIP, r, c), lambda i, j, k: (0, 0, 0))
        o_spec = pl.BlockSpec((tm, N_CHIP * r), lambda i, j, k: (i, 0))
        return _mm(name, dy, w4, (t, N_CHIP * r), out_dtype, grid, a_spec, b_spec, o_spec, NT, (tm, N_CHIP * r))
    grid = (t // tm, N_CHIP, 1)
    a_spec = pl.BlockSpec((tm, c), lambda i, j, k: (i, 0))
    b_spec = pl.BlockSpec((None, r, c), lambda i, j, k: (j, 0, 0))
    o_spec = pl.BlockSpec((tm, r), lambda i, j, k: (i, j))
    return _mm(name, dy, w4, (t, N_CHIP * r), out_dtype, grid, a_spec, b_spec, o_spec, NT, (tm, r))


def mm_dw(name, a, dy, mode, shape3):
    t = a.shape[0]
    _, r, c = shape3
    if mode == "col":
        tn = _tile(c, MM_TN)
        npj = c // tn
        tt = _token_rows(t, r + tn)
        grid = (1, N_CHIP * npj, t // tt)
        a_spec = pl.BlockSpec((tt, r), lambda i, j, k: (k, 0))
        hb = N_CHIP // 2 * npj
        if dy.ndim == 3:
            b_spec = pl.BlockSpec((None, tt, tn), lambda i, j, k: (j // hb, k, j % hb))
        else:
            b_spec = pl.BlockSpec((tt, tn), lambda i, j, k: (k, j))
        o_spec = pl.BlockSpec((None, r, tn), lambda i, j, k: (j // npj, 0, j % npj))
        return _mm(name, a, dy, shape3, MXU_DTYPE, grid, a_spec, b_spec, o_spec, TN, (r, tn))
    if N_CHIP * r <= MM_K1:
        tt = min(_token_rows(t, N_CHIP * r + c), max(t // 4, MM_TM))
        grid = (1, 1, t // tt)
        a_spec = pl.BlockSpec((tt, N_CHIP * r), lambda i, j, k: (k, 0))
        b_spec = pl.BlockSpec((tt, c), lambda i, j, k: (k, 0))
        o_spec = pl.BlockSpec((N_CHIP, r, c), lambda i, j, k: (0, 0, 0))
        return _mm(name, a, dy, shape3, MXU_DTYPE, grid, a_spec, b_spec, o_spec, TN, (N_CHIP * r, c))
    tt = _token_rows(t, r + c)
    grid = (N_CHIP, 1, t // tt)
    a_spec = pl.BlockSpec((tt, r), lambda i, j, k: (k, i))
    b_spec = pl.BlockSpec((tt, c), lambda i, j, k: (k, 0))
    o_spec = pl.BlockSpec((None, r, c), lambda i, j, k: (i, 0, 0))
    return _mm(name, a, dy, shape3, MXU_DTYPE, grid, a_spec, b_spec, o_spec, TN, (r, c))


def _rms_bwd(x, g, dh):
    r = lax.rsqrt(jnp.mean(x * x, axis=-1, keepdims=True) + NORM_EPS)
    xhat = x * r
    dyg = dh * g
    dx = r * (dyg - xhat * jnp.mean(dyg * xhat, axis=-1, keepdims=True))
    return dx, jnp.sum(dh * xhat, axis=0, keepdims=True)


def mm_dx_norms(name, dy, w4, x, g_pre, dres, prev, after):
    t = dy.shape[-2]
    _, r, c = w4.shape
    tm = min(t, MM_TM // 2)
    nt, nk = t // tm, N_CHIP
    hb = N_CHIP // 2
    chained = prev is not None

    def body(dy_ref, w_ref, x_ref, dres_ref, g_ref, *rest):
        rest = rest[1:] if after is not None else rest
        if chained:
            y_ref, gp_ref, dx_ref, dg_ref, dyp_ref, dgp_ref, acc = rest
        else:
            dx_ref, dg_ref, acc = rest
        i, k = pl.program_id(0), pl.program_id(1)
        part = lax.dot_general(dy_ref[...], w_ref[...], NT, preferred_element_type=F32)

        @pl.when(k == 0)
        def _():
            acc[...] = part

        @pl.when(k > 0)
        def _():
            acc[...] += part

        def add_to(ref, v):
            @pl.when(i == 0)
            def _():
                ref[...] = v

            @pl.when(i > 0)
            def _():
                ref[...] += v

        @pl.when(k == nk - 1)
        def _():
            dx, dg = _rms_bwd(x_ref[...], g_ref[...], acc[...])
            dxs = dres_ref[...] + dx
            dx_ref[...] = dxs
            add_to(dg_ref, dg)
            if chained:
                dyp, dgp = _rms_bwd(y_ref[...], gp_ref[...], dxs)
                dyp_ref[...] = (prev[2] * dyp).astype(dyp_ref.dtype)
                add_to(dgp_ref, prev[2] * dgp)

    if dy.ndim == 3:
        dy_spec = pl.BlockSpec((None, tm, c), lambda i, k: (k // hb, i, k % hb))
    else:
        dy_spec = pl.BlockSpec((tm, c), lambda i, k: (i, k))
    rows = pl.BlockSpec((tm, r), lambda i, k: (i, 0))
    gain = pl.BlockSpec((1, r), lambda i, k: (0, 0))
    in_specs = [dy_spec, pl.BlockSpec((None, r, c), lambda i, k: (k, 0, 0)), rows, rows, gain]
    args = [dy, w4, x, dres, g_pre]
    if after is not None:
        in_specs.append(pl.BlockSpec(memory_space=pl.ANY))
        args.append(after)
    out_specs = [rows, gain]
    out_shape = [jax.ShapeDtypeStruct((t, r), F32), jax.ShapeDtypeStruct((1, r), F32)]
    if chained:
        in_specs += [rows, gain]
        args += [prev[0], prev[1]]
        out_specs += [rows, gain]
        out_shape += [jax.ShapeDtypeStruct((t, r), MXU_DTYPE), jax.ShapeDtypeStruct((1, r), F32)]
    res = pl.pallas_call(
        body, name=name, grid=(nt, nk), in_specs=in_specs, out_specs=out_specs, out_shape=out_shape,
        scratch_shapes=[pltpu.VMEM((tm, r), F32)], compiler_params=_params(("arbitrary", "arbitrary")),
    )(*args)
    return tuple(res) if chained else (res[0], res[1], None, None)


def _rowwise(name, fn, rows, pars, outs, accs=(), tm=256, ncol=1):
    t = rows[0][0].shape[0]
    nrow, npar, nout = len(rows), len(pars), len(outs)

    def body(*refs):
        vals = [r[...] for r in refs[:nrow + npar]]
        res = fn(*vals)
        out_refs = refs[nrow + npar:nrow + npar + nout]
        acc_refs = refs[nrow + npar + nout:]
        for o, v in zip(out_refs, res[:nout]):
            o[...] = v.astype(o.dtype)
        i = pl.program_id(1)
        for a, v in zip(acc_refs, res[nout:]):
            @pl.when(i == 0)
            def _(a=a, v=v):
                a[...] = v.astype(F32)

            @pl.when(i > 0)
            def _(a=a, v=v):
                a[...] += v.astype(F32)

    in_specs = [pl.BlockSpec((tm, w), functools.partial(lambda j, i, b: (i, b + j), b=b)) for _, w, b in rows]
    for arr, w in pars:
        if w is None:
            in_specs.append(pl.BlockSpec(arr.shape, lambda j, i: (0, 0)))
        else:
            in_specs.append(pl.BlockSpec((1, w), lambda j, i: (0, j)))
    out_specs = [pl.BlockSpec((tm, w), lambda j, i: (i, j)) for _, w, _ in outs]
    out_specs += [pl.BlockSpec((1, w), lambda j, i: (0, j)) for _, w in accs]
    out_shape = [jax.ShapeDtypeStruct((t, tw), dt) for tw, _, dt in outs]
    out_shape += [jax.ShapeDtypeStruct((1, tw), F32) for tw, _ in accs]
    res = pl.pallas_call(
        body, name=name, grid=(ncol, t // tm), in_specs=in_specs, out_specs=out_specs, out_shape=out_shape,
        compiler_params=_params(("parallel", "arbitrary" if accs else "parallel")),
    )(*[r[0] for r in rows], *[p[0] for p in pars])
    return res


def _rms(x, g):
    xf = x.astype(F32)
    return xf * lax.rsqrt(jnp.mean(xf * xf, axis=-1, keepdims=True) + NORM_EPS) * g


def _silu(x):
    return x * jax.nn.sigmoid(x)


def rms_fwd(name, x, g):
    d = x.shape[1]
    return _rowwise(name, lambda x, g: (_rms(x, g),), [(x, d, 0)], [(g, None)], [(d, d, MXU_DTYPE)], tm=512)[0]


def rms_bwd(name, x, g, dh, dres):
    d = x.shape[1]

    def fn(x, dh, dres, g):
        _, vjp = jax.vjp(_rms, x, g)
        dx, dg = vjp(dh.astype(F32))
        return dres + dx, dg

    return _rowwise(name, fn, [(x, d, 0), (dh, d, 0), (dres, d, 0)], [(g, None)], [(d, d, F32)], [(d, d)], tm=256)


def mm_post(name, a, w4, x, g_post, scale, g_next):
    t = a.shape[0]
    _, r, c = w4.shape
    tm = min(t, MM_TM // 2)
    chained = g_next is not None

    def body(a_ref, w_ref, x_ref, gp_ref, *rest):
        gn_ref, y_ref, xn_ref, h_ref = rest if chained else (None,) + rest + (None,)
        y = lax.dot_general(a_ref[...], w_ref[...].reshape(N_CHIP * r, c), NN, preferred_element_type=F32)
        y_ref[...] = y
        xn = x_ref[...] + scale * _rms(y, gp_ref[...])
        xn_ref[...] = xn
        if chained:
            h_ref[...] = _rms(xn, gn_ref[...]).astype(h_ref.dtype)

    def rows(width):
        return pl.BlockSpec((tm, width), lambda i: (i, 0))

    gain = pl.BlockSpec((1, c), lambda i: (0, 0))
    in_specs = [rows(N_CHIP * r), pl.BlockSpec((N_CHIP, r, c), lambda i: (0, 0, 0)), rows(c), gain]
    args = [a, w4, x, g_post]
    out_specs, out_shape = [rows(c), rows(c)], [jax.ShapeDtypeStruct((t, c), F32)] * 2
    if chained:
        in_specs.append(gain)
        args.append(g_next)
        out_specs.append(rows(c))
        out_shape.append(jax.ShapeDtypeStruct((t, c), MXU_DTYPE))
    res = pl.pallas_call(
        body, name=name, grid=(t // tm,), in_specs=in_specs, out_specs=out_specs, out_shape=out_shape,
        compiler_params=_params(("parallel",)),
    )(*args)
    return res[0], res[1], (res[2] if chained else None)


def post_bwd(name, y, g, dx, scale):
    d = y.shape[1]

    def fn(y, dx, g):
        _, vjp = jax.vjp(lambda y, g: scale * _rms(y, g), y, g)
        return vjp(dx)

    return _rowwise(name, fn, [(y, d, 0), (dx, d, 0)], [(g, None)], [(d, d, MXU_DTYPE)], [(d, d)], tm=256)


def ffn_up(name, h, w4):
    t = h.shape[0]
    _, r, c = w4.shape
    tm = min(t, MM_TM)
    tn = _tile(c, MM_TM)
    npj = c // tn
    half = N_CHIP // 2

    def body(h_ref, wg_ref, wu_ref, gu_ref, a_ref):
        hv = h_ref[...]
        g = lax.dot_general(hv, wg_ref[...], NN, preferred_element_type=F32)
        u = lax.dot_general(hv, wu_ref[...], NN, preferred_element_type=F32)
        gu_ref[0] = g.astype(gu_ref.dtype)
        gu_ref[1] = u.astype(gu_ref.dtype)
        a_ref[...] = (_silu(g) * u).astype(a_ref.dtype)

    f = half * c
    return pl.pallas_call(
        body, name=name, grid=(t // tm, half * npj),
        in_specs=[pl.BlockSpec((tm, r), lambda i, j: (i, 0)),
                  pl.BlockSpec((None, r, tn), lambda i, j: (j // npj, 0, j % npj)),
                  pl.BlockSpec((None, r, tn), lambda i, j: (half + j // npj, 0, j % npj))],
        out_specs=[pl.BlockSpec((2, tm, tn), lambda i, j: (0, i, j)), pl.BlockSpec((tm, tn), lambda i, j: (i, j))],
        out_shape=[jax.ShapeDtypeStruct((2, t, f), MXU_DTYPE), jax.ShapeDtypeStruct((t, f), MXU_DTYPE)],
        compiler_params=_params(("parallel", "parallel")),
    )(h, w4, w4)


def ffn_down_dx(name, dy, w4, gu):
    t = dy.shape[0]
    _, r, c = w4.shape
    tm = min(t, MM_TM)

    def body(dy_ref, w_ref, gu_ref, o_ref):
        dyv = dy_ref[...]
        for n0 in range(0, r, MM_SLICE):
            cols = pl.ds(n0, MM_SLICE)
            da = lax.dot_general(dyv, w_ref[cols, :], NT, preferred_element_type=F32)
            gate, up = gu_ref[0, :, cols].astype(F32), gu_ref[1, :, cols].astype(F32)
            sg = jax.nn.sigmoid(gate)
            silu = gate * sg
            o_ref[0, :, cols] = (da * up * (sg + silu * (1.0 - sg))).astype(o_ref.dtype)
            o_ref[1, :, cols] = (da * silu).astype(o_ref.dtype)

    return pl.pallas_call(
        body, name=name, grid=(t // tm, N_CHIP),
        in_specs=[pl.BlockSpec((tm, c), lambda i, j: (i, 0)), pl.BlockSpec((None, r, c), lambda i, j: (j, 0, 0)),
                  pl.BlockSpec((2, tm, r), lambda i, j: (0, i, j))],
        out_specs=pl.BlockSpec((2, tm, r), lambda i, j: (0, i, j)),
        out_shape=jax.ShapeDtypeStruct((2, t, N_CHIP * r), MXU_DTYPE),
        compiler_params=_params(("parallel", "parallel")),
    )(dy, w4, gu)


def _head_gate(o, g):
    mu = jnp.mean(o, axis=-1, keepdims=True)
    var = jnp.mean(jnp.square(o - mu), axis=-1, keepdims=True)
    return _silu(g.astype(F32)) * ((o - mu) * lax.rsqrt(var + LN_EPS))


def head_gate_fwd(name, o, p, gate_blk):
    dv = RET_V_DIM
    return _rowwise(name, lambda o, g: (_head_gate(o, g),), [(o, dv, 0), (p, dv, gate_blk)], [],
                    [(RET_HEADS * dv, dv, MXU_DTYPE)], tm=min(o.shape[0], 2048), ncol=RET_HEADS)[0]


def head_gate_bwd(name, o, p, gate_blk, da):
    dv = RET_V_DIM

    def fn(o, g, da):
        _, vjp = jax.vjp(_head_gate, o, g.astype(F32))
        return vjp(da.astype(F32))

    w = RET_HEADS * dv
    return _rowwise(name, fn, [(o, dv, 0), (p, dv, gate_blk), (da, dv, 0)], [],
                    [(w, dv, MXU_DTYPE), (w, dv, MXU_DTYPE)], tm=min(o.shape[0], 2048), ncol=RET_HEADS)


def _ln_silu(u, g, b):
    mu = jnp.mean(u, axis=-1, keepdims=True)
    var = jnp.mean(jnp.square(u - mu), axis=-1, keepdims=True)
    return _silu((u - mu) * lax.rsqrt(var + LN_EPS) * g + b)


def ln_silu_fwd(name, u, g, b):
    d = u.shape[1]
    return _rowwise(name, lambda u, g, b: (_ln_silu(u, g, b),), [(u, d, 0)], [(g, None), (b, None)],
                    [(d, d, MXU_DTYPE)], tm=512)[0]


def ln_silu_bwd(name, u, g, b, dc):
    d = u.shape[1]

    def fn(u, dc, g, b):
        _, vjp = jax.vjp(_ln_silu, u, g, b)
        return vjp(dc.astype(F32))

    return _rowwise(name, fn, [(u, d, 0), (dc, d, 0)], [(g, None), (b, None)], [(d, d, F32)], [(d, d), (d, d)],
                    tm=256)


def _merge(g0, g1, g2, ya, yb, yc):
    s = jax.nn.sigmoid
    return s(g0.astype(F32)) * ya + s(g1.astype(F32)) * yb + s(g2.astype(F32)) * yc


def merge_fwd(name, p, blk, ya, yb, yc):
    d = ya.shape[1]
    rows = [(p, d, blk), (p, d, blk + 1), (p, d, blk + 2), (ya, d, 0), (yb, d, 0), (yc, d, 0)]
    return _rowwise(name, lambda *v: (_merge(*v),), rows, [], [(d, d, MXU_DTYPE)], tm=256)[0]


def merge_bwd(name, p, blk, ya, yb, yc, dmg):
    d = ya.shape[1]

    def fn(g0, g1, g2, ya, yb, yc, dmg):
        _, vjp = jax.vjp(_merge, g0.astype(F32), g1.astype(F32), g2.astype(F32), ya, yb, yc)
        return vjp(dmg.astype(F32))

    rows = [(p, d, blk), (p, d, blk + 1), (p, d, blk + 2), (ya, d, 0), (yb, d, 0), (yc, d, 0), (dmg, d, 0)]
    return _rowwise(name, fn, rows, [], [(d, d, MXU_DTYPE)] * 6, tm=256)


def concat_cols(name, pieces):
    t = pieces[0].shape[0]
    widths = [p.shape[1] for p in pieces]
    tm = 256

    def body(*refs):
        o_ref, at = refs[-1], 0
        for r, w in zip(refs[:-1], widths):
            o_ref[:, at:at + w] = r[...]
            at += w

    return pl.pallas_call(
        body, name=name, grid=(t // tm,),
        in_specs=[pl.BlockSpec((tm, w), lambda i: (i, 0)) for w in widths],
        out_specs=pl.BlockSpec((tm, sum(widths)), lambda i: (i, 0)),
        out_shape=jax.ShapeDtypeStruct((t, sum(widths)), pieces[0].dtype),
        compiler_params=_params(("parallel",)),
    )(*pieces)


def loss_head(name, y, target):
    t, d = y.shape
    tm = 512

    def body(y_ref, t_ref, dy_ref, loss_ref):
        err = y_ref[...] - t_ref[...]
        dy_ref[...] = err * (1.0 / d)
        part = jnp.sum(jnp.sum(err * err, axis=1, keepdims=True), axis=0, keepdims=True) * (0.5 / d)

        @pl.when(pl.program_id(0) == 0)
        def _():
            loss_ref[...] = part

        @pl.when(pl.program_id(0) > 0)
        def _():
            loss_ref[...] += part

    return pl.pallas_call(
        body, name=name, grid=(t // tm,),
        in_specs=[pl.BlockSpec((tm, d), lambda i: (i, 0))] * 2,
        out_specs=[pl.BlockSpec((tm, d), lambda i: (i, 0)), pl.BlockSpec((1, 1), lambda i: (0, 0))],
        out_shape=[jax.ShapeDtypeStruct((t, d), F32), jax.ShapeDtypeStruct((1, 1), F32)],
        compiler_params=_params(("arbitrary",)),
    )(y, target)


def _rot(x, cos2, sin2):
    return x * cos2 + pltpu.roll(x, RET_QK_DIM // 2, 1) * sin2


def _decay_mask(lg, n0, rows, cols):
    n = n0 + lax.broadcasted_iota(jnp.int32, (rows, cols), 0)
    m = lax.broadcasted_iota(jnp.int32, (rows, cols), 1)
    shift = CHUNK.bit_length() - 1
    dist = jnp.abs(n - m).astype(F32)
    return jnp.where((m >> shift) <= (n >> shift), jnp.exp(lg * dist), 0.0)


def _ret_specs(s):
    dk, dv, h = RET_QK_DIM, RET_V_DIM, RET_HEADS
    return [
        pl.BlockSpec((s, dk), lambda b, hh: (b, hh)),
        pl.BlockSpec((s, dk), lambda b, hh: (b, h + hh)),
        pl.BlockSpec((s, dv), lambda b, hh: (b, (2 * h * dk) // dv + hh)),
        pl.BlockSpec((s, dk), lambda b, hh: (b, 0)),
        pl.BlockSpec((s, dk), lambda b, hh: (b, 0)),
        pl.BlockSpec((None, 1, dk), lambda b, hh: (hh, 0, 0)),
    ]


def retention_fwd(name, p, cos2, sin2, log_g, nb, s):
    dk, dv, h = RET_QK_DIM, RET_V_DIM, RET_HEADS

    def body(q_ref, k_ref, v_ref, cos_ref, sin_ref, lg_ref, o_ref, kr_ref):
        lg = lg_ref[0:1, 0:1]
        kr = _rot(k_ref[...].astype(F32), cos_ref[...], sin_ref[...]) * (dk ** -0.5)
        kr_ref[...] = kr.astype(kr_ref.dtype)
        for qi in range(s // RET_TQ):
            n0, kmax = qi * RET_TQ, (qi + 1) * RET_TQ
            rows = pl.ds(n0, RET_TQ)
            qr = _rot(q_ref[rows, :].astype(F32), cos_ref[rows, :], sin_ref[rows, :]).astype(MXU_DTYPE)
            sc = lax.dot_general(qr, kr_ref[0:kmax, :], NT, preferred_element_type=F32)
            pm = (sc * _decay_mask(lg, n0, RET_TQ, kmax)).astype(MXU_DTYPE)
            o_ref[rows, :] = lax.dot_general(pm, v_ref[0:kmax, :], NN, preferred_element_type=F32)

    return pl.pallas_call(
        body, name=name, grid=(nb, h), in_specs=_ret_specs(s),
        out_specs=pl.BlockSpec((s, dv), lambda b, hh: (b, hh)),
        out_shape=jax.ShapeDtypeStruct((nb * s, h * dv), F32),
        scratch_shapes=[pltpu.VMEM((s, dk), MXU_DTYPE)],
        compiler_params=_params(("parallel", "parallel")),
    )(p, p, p, cos2, sin2, log_g)


def retention_bwd(name, p, cos2, sin2, log_g, do, nb, s):
    dk, dv, h = RET_QK_DIM, RET_V_DIM, RET_HEADS

    def body(q_ref, k_ref, v_ref, cos_ref, sin_ref, lg_ref, do_ref, dq_ref, dk_ref, dv_ref, kr_ref, dk_acc, dv_acc):
        lg = lg_ref[0:1, 0:1]
        kr = _rot(k_ref[...].astype(F32), cos_ref[...], sin_ref[...]) * (dk ** -0.5)
        kr_ref[...] = kr.astype(kr_ref.dtype)
        dk_acc[...] = jnp.zeros_like(dk_acc)
        dv_acc[...] = jnp.zeros_like(dv_acc)
        for qi in range(s // RET_TQ):
            n0, kmax = qi * RET_TQ, (qi + 1) * RET_TQ
            rows = pl.ds(n0, RET_TQ)
            cq, sq = cos_ref[rows, :], sin_ref[rows, :]
            qr = _rot(q_ref[rows, :].astype(F32), cq, sq).astype(MXU_DTYPE)
            dob = do_ref[rows, :]
            mask = _decay_mask(lg, n0, RET_TQ, kmax)
            sc = lax.dot_general(qr, kr_ref[0:kmax, :], NT, preferred_element_type=F32)
            pm = (sc * mask).astype(MXU_DTYPE)
            dv_acc[0:kmax, :] += lax.dot_general(pm, dob, TN, preferred_element_type=F32)
            dp = lax.dot_general(dob, v_ref[0:kmax, :], NT, preferred_element_type=F32)
            ds = (dp * mask).astype(MXU_DTYPE)
            dqr = lax.dot_general(ds, kr_ref[0:kmax, :], NN, preferred_element_type=F32)
            dq_ref[rows, :] = _rot(dqr, cq, -sq).astype(dq_ref.dtype)
            dk_acc[0:kmax, :] += lax.dot_general(ds, qr, TN, preferred_element_type=F32)
        dkr = dk_acc[...] * (dk ** -0.5)
        dk_ref[...] = _rot(dkr, cos_ref[...], -sin_ref[...]).astype(dk_ref.dtype)
        dv_ref[...] = dv_acc[...].astype(dv_ref.dtype)

    t = nb * s
    return pl.pallas_call(
        body, name=name, grid=(nb, h),
        in_specs=_ret_specs(s) + [pl.BlockSpec((s, dv), lambda b, hh: (b, hh))],
        out_specs=[pl.BlockSpec((s, dk), lambda b, hh: (b, hh)), pl.BlockSpec((s, dk), lambda b, hh: (b, hh)),
                   pl.BlockSpec((s, dv), lambda b, hh: (b, hh))],
        out_shape=[jax.ShapeDtypeStruct((t, h * dk), MXU_DTYPE), jax.ShapeDtypeStruct((t, h * dk), MXU_DTYPE),
                   jax.ShapeDtypeStruct((t, h * dv), MXU_DTYPE)],
        scratch_shapes=[pltpu.VMEM((s, dk), MXU_DTYPE), pltpu.VMEM((s, dk), F32), pltpu.VMEM((s, dv), F32)],
        compiler_params=_params(("parallel", "parallel")),
    )(p, p, p, cos2, sin2, log_g, do)


def _conv_grid(t, d, nb, ts):
    s = t // nb
    ns, nc = s // ts, d // CONV_TC
    return s, ns, nc


def _shifted(pad_ref, sh_ref, offsets):
    n = sh_ref.shape[1]
    for b in sorted({off % SUBLANES for off in offsets} - {0}):
        sh_ref[b - 1] = pad_ref[pl.ds(b, n), :]

    def read(off, r0):
        a, b = off - off % SUBLANES + r0, off % SUBLANES
        return pad_ref[pl.ds(a, SUBLANES), :] if b == 0 else sh_ref[b - 1, pl.ds(a, SUBLANES), :]

    return read


def _causal_taps(pad_ref, sh_ref, w_ref, k, emit):
    offs = [CONV_PAD - (k - 1) + j for j in range(k)]
    read = _shifted(pad_ref, sh_ref, offs)
    for r0 in range(0, pad_ref.shape[0] - CONV_PAD, CONV_ROWS):
        accs = [None] * len(CONV_TILES)
        for j in range(k):
            wj = w_ref[j]
            for q, dr in enumerate(CONV_TILES):
                term = wj * read(offs[j], r0 + dr)
                accs[q] = term if accs[q] is None else accs[q] + term
        emit(r0, jnp.concatenate(accs, axis=0))


def _tap_tiles(w):
    return jnp.broadcast_to(w[:, None, :], (w.shape[0], SUBLANES, w.shape[1]))


def _tap_spec(k):
    return pl.BlockSpec((k, SUBLANES, CONV_TC), lambda c, b, si: (0, 0, c))


def _carry_past(pad_ref, s_idx):
    ts = pad_ref.shape[0] - CONV_PAD

    @pl.when(s_idx == 0)
    def _():
        pad_ref[0:CONV_PAD, :] = jnp.zeros((CONV_PAD, pad_ref.shape[1]), F32)

    @pl.when(s_idx > 0)
    def _():
        pad_ref[0:CONV_PAD, :] = pad_ref[ts:ts + CONV_PAD, :]


def _carry_future(pad_ref, s_idx):
    ts = pad_ref.shape[0] - CONV_PAD

    @pl.when(s_idx == 0)
    def _():
        pad_ref[ts:ts + CONV_PAD, :] = jnp.zeros((CONV_PAD, pad_ref.shape[1]), F32)

    @pl.when(s_idx > 0)
    def _():
        pad_ref[ts:ts + CONV_PAD, :] = pad_ref[0:CONV_PAD, :]


def _conv_bwd_taps(pad_ref, sh_ref, w_ref, dw_acc, k, x_rows, emit, mix):
    read = _shifted(pad_ref, sh_ref, range(k))
    for r0 in range(0, pad_ref.shape[0] - CONV_PAD, CONV_ROWS):
        ops = x_rows(r0)
        x = mix(ops)
        accs = [None] * len(CONV_TILES)
        for j in range(k):
            wj, dwj = w_ref[j], None
            for q, dr in enumerate(CONV_TILES):
                sh = read(k - 1 - j, r0 + dr)
                term = wj * sh
                accs[q] = term if accs[q] is None else accs[q] + term
                prod = x[dr:dr + SUBLANES] * sh
                dwj = prod if dwj is None else dwj + prod
            dw_acc[j] += dwj
        emit(r0, ops, jnp.concatenate(accs, axis=0))


def _conv_bwd_edges(dw_acc, dw_ref, nb, ns, extra=()):
    first = jnp.logical_and(pl.program_id(1) == 0, pl.program_id(2) == 0)
    last = jnp.logical_and(pl.program_id(1) == nb - 1, pl.program_id(2) == ns - 1)

    @pl.when(first)
    def _():
        dw_acc[...] = jnp.zeros_like(dw_acc)
        for r in extra:
            r[...] = jnp.zeros_like(r)

    def finish():
        @pl.when(last)
        def _():
            dw_ref[...] = jnp.sum(dw_acc[...], axis=1)

    return finish


def short_conv_fwd(name, p, blk_b, w, nb):
    t = p.shape[0]
    d = w.shape[1]
    ts = SC_TS
    s, ns, nc = _conv_grid(t, d, nb, ts)
    cb = d // CONV_TC

    def body(b_ref, c_ref, x_ref, w_ref, y_ref, cz_ref, pad_ref, sh_ref):
        _carry_past(pad_ref, pl.program_id(2))
        pad_ref[CONV_PAD:CONV_PAD + ts, :] = c_ref[...].astype(F32) * x_ref[...].astype(F32)

        def emit(r0, cz):
            rows = pl.ds(r0, CONV_ROWS)
            cz_ref[rows, :] = cz
            y_ref[rows, :] = (b_ref[rows, :].astype(F32) * cz).astype(y_ref.dtype)

        _causal_taps(pad_ref, sh_ref, w_ref, SC_KERNEL, emit)

    def pspec(off):
        return pl.BlockSpec((ts, CONV_TC), lambda c, b, si: (b * ns + si, (blk_b + off) * cb + c))

    ospec = pl.BlockSpec((ts, CONV_TC), lambda c, b, si: (b * ns + si, c))
    return pl.pallas_call(
        body, name=name, grid=(nc, nb, ns),
        in_specs=[pspec(0), pspec(1), pspec(2), _tap_spec(SC_KERNEL)],
        out_specs=[ospec, ospec],
        out_shape=[jax.ShapeDtypeStruct((t, d), MXU_DTYPE), jax.ShapeDtypeStruct((t, d), F32)],
        scratch_shapes=_conv_scratch(ts),
        compiler_params=_params(("parallel", "arbitrary", "arbitrary")),
    )(p, p, p, _tap_tiles(w))


def short_conv_bwd(name, p, blk_b, w, cz, dy, nb):
    t = p.shape[0]
    d = w.shape[1]
    ts = SC_TS
    s, ns, nc = _conv_grid(t, d, nb, ts)
    cb = d // CONV_TC

    def body(b_ref, c_ref, x_ref, w_ref, cz_ref, dy_ref, db_ref, dc_ref, dx_ref, dw_ref, pad_ref, sh_ref, dw_acc):
        _carry_future(pad_ref, pl.program_id(2))
        dyv = dy_ref[...].astype(F32)
        db_ref[...] = (dyv * cz_ref[...]).astype(db_ref.dtype)
        pad_ref[0:ts, :] = dyv * b_ref[...].astype(F32)
        finish = _conv_bwd_edges(dw_acc, dw_ref, nb, ns)

        def x_rows(r0):
            rows = pl.ds(r0, CONV_ROWS)
            return c_ref[rows, :].astype(F32), x_ref[rows, :].astype(F32)

        def emit(r0, cx, dz):
            rows = pl.ds(r0, CONV_ROWS)
            dc_ref[rows, :] = (dz * cx[1]).astype(dc_ref.dtype)
            dx_ref[rows, :] = (dz * cx[0]).astype(dx_ref.dtype)

        _conv_bwd_taps(pad_ref, sh_ref, w_ref, dw_acc, SC_KERNEL, x_rows, emit, lambda cx: cx[0] * cx[1])
        finish()

    def row(b, si):
        return b * ns + (ns - 1 - si)

    def pspec(off):
        return pl.BlockSpec((ts, CONV_TC), lambda c, b, si: (row(b, si), (blk_b + off) * cb + c))

    ospec = pl.BlockSpec((ts, CONV_TC), lambda c, b, si: (row(b, si), c))
    wspec = pl.BlockSpec((SC_KERNEL, CONV_TC), lambda c, b, si: (0, c))
    return pl.pallas_call(
        body, name=name, grid=(nc, nb, ns),
        in_specs=[pspec(0), pspec(1), pspec(2), _tap_spec(SC_KERNEL), ospec, ospec],
        out_specs=[ospec, ospec, ospec, wspec],
        out_shape=[jax.ShapeDtypeStruct((t, d), MXU_DTYPE)] * 3 + [jax.ShapeDtypeStruct((SC_KERNEL, d), F32)],
        scratch_shapes=_conv_scratch(ts) + [pltpu.VMEM((SC_KERNEL, SUBLANES, CONV_TC), F32)],
        compiler_params=_params(("parallel", "arbitrary", "arbitrary")),
    )(p, p, p, _tap_tiles(w), cz, dy)


def conformer_conv_fwd(name, p, blk_a, w, bias, nb):
    t = p.shape[0]
    d = w.shape[1]
    ts = CONV_TS
    s, ns, nc = _conv_grid(t, d, nb, ts)
    cb = d // CONV_TC

    def body(a_ref, b_ref, w_ref, bias_ref, u_ref, pad_ref, sh_ref):
        _carry_past(pad_ref, pl.program_id(2))
        pad_ref[CONV_PAD:CONV_PAD + ts, :] = a_ref[...].astype(F32) * jax.nn.sigmoid(b_ref[...].astype(F32))

        def emit(r0, u):
            u_ref[pl.ds(r0, CONV_ROWS), :] = u + bias_ref[0:1, :]

        _causal_taps(pad_ref, sh_ref, w_ref, CF_KERNEL, emit)

    def pspec(off):
        return pl.BlockSpec((ts, CONV_TC), lambda c, b, si: (b * ns + si, (blk_a + off) * cb + c))

    return pl.pallas_call(
        body, name=name, grid=(nc, nb, ns),
        in_specs=[pspec(0), pspec(1), _tap_spec(CF_KERNEL), pl.BlockSpec((SUBLANES, CONV_TC), lambda c, b, si: (0, c))],
        out_specs=pl.BlockSpec((ts, CONV_TC), lambda c, b, si: (b * ns + si, c)),
        out_shape=jax.ShapeDtypeStruct((t, d), F32),
        scratch_shapes=_conv_scratch(ts),
        compiler_params=_params(("parallel", "arbitrary", "arbitrary")),
    )(p, p, _tap_tiles(w), jnp.broadcast_to(bias, (SUBLANES, d)))


def conformer_conv_bwd(name, p, blk_a, w, du, nb):
    t = p.shape[0]
    d = w.shape[1]
    ts = CONV_TS
    s, ns, nc = _conv_grid(t, d, nb, ts)
    cb = d // CONV_TC

    def body(a_ref, b_ref, w_ref, du_ref, da_ref, db_ref, dw_ref, dbias_ref, pad_ref, sh_ref, dw_acc):
        _carry_future(pad_ref, pl.program_id(2))
        duv = du_ref[...]
        pad_ref[0:ts, :] = duv
        finish = _conv_bwd_edges(dw_acc, dw_ref, nb, ns, extra=(dbias_ref,))
        dbias_ref[...] += jnp.sum(duv, axis=0, keepdims=True)

        def x_rows(r0):
            rows = pl.ds(r0, CONV_ROWS)
            return a_ref[rows, :].astype(F32), jax.nn.sigmoid(b_ref[rows, :].astype(F32))

        def emit(r0, asg, du0):
            rows = pl.ds(r0, CONV_ROWS)
            av, sg = asg
            da_ref[rows, :] = (du0 * sg).astype(da_ref.dtype)
            db_ref[rows, :] = (du0 * av * sg * (1.0 - sg)).astype(db_ref.dtype)

        _conv_bwd_taps(pad_ref, sh_ref, w_ref, dw_acc, CF_KERNEL, x_rows, emit, lambda asg: asg[0] * asg[1])
        finish()

    def row(b, si):
        return b * ns + (ns - 1 - si)

    def pspec(off):
        return pl.BlockSpec((ts, CONV_TC), lambda c, b, si: (row(b, si), (blk_a + off) * cb + c))

    ospec = pl.BlockSpec((ts, CONV_TC), lambda c, b, si: (row(b, si), c))
    wspec = pl.BlockSpec((CF_KERNEL, CONV_TC), lambda c, b, si: (0, c))
    bspec = pl.BlockSpec((1, CONV_TC), lambda c, b, si: (0, c))
    return pl.pallas_call(
        body, name=name, grid=(nc, nb, ns),
        in_specs=[pspec(0), pspec(1), _tap_spec(CF_KERNEL), ospec],
        out_specs=[ospec, ospec, wspec, bspec],
        out_shape=[jax.ShapeDtypeStruct((t, d), MXU_DTYPE)] * 2
        + [jax.ShapeDtypeStruct((CF_KERNEL, d), F32), jax.ShapeDtypeStruct((1, d), F32)],
        scratch_shapes=_conv_scratch(ts) + [pltpu.VMEM((CF_KERNEL, SUBLANES, CONV_TC), F32)],
        compiler_params=_params(("parallel", "arbitrary", "arbitrary")),
    )(p, p, _tap_tiles(w), du)


BLOCKS = ("ffn1", "mixer", "ffn2")
BLOCK_WEIGHTS = {"ffn1": ("ffn1_w_gu", "ffn1_w_down"), "mixer": ("w_in", "w_ret_o", "w_sc_o", "w_cf_o", "w_o"),
                 "ffn2": ("ffn2_w_gu", "ffn2_w_down")}
BIG = BLOCK_WEIGHTS["ffn1"] + BLOCK_WEIGHTS["mixer"] + BLOCK_WEIGHTS["ffn2"]
MODE = {"ffn1_w_gu": "col", "ffn1_w_down": "row", "w_in": "col", "w_ret_o": "row", "w_sc_o": "row",
        "w_cf_o": "row", "w_o": "row", "ffn2_w_gu": "col", "ffn2_w_down": "row"}
NORM_OF = {"ffn1": 0, "mixer": 2, "ffn2": 4}
BLK_GATE, BLK_SCB, BLK_CFA, BLK_MERGE = 2, 3, 6, 8


def _rope_tables(positions):
    half = RET_QK_DIM // 2
    inv_freq = ROPE_BASE ** (-jnp.arange(half, dtype=F32) / half)
    ang = positions.astype(F32)[..., None] * inv_freq
    cos, sin = jnp.cos(ang), jnp.sin(ang)
    nb, s = positions.shape
    cos2 = jnp.concatenate([cos, cos], axis=-1).reshape(nb * s, RET_QK_DIM)
    sin2 = jnp.concatenate([-sin, sin], axis=-1).reshape(nb * s, RET_QK_DIM)
    return cos2, sin2


def _log_gamma():
    lg = jnp.log(1.0 - 2.0 ** (-5.0 - jnp.arange(RET_HEADS, dtype=F32)))
    return jnp.broadcast_to(lg[:, None, None], (RET_HEADS, 1, RET_QK_DIM))


def _ffn_fwd(xs, h, w, tag, g_post, g_next):
    gu, a = ffn_up("ffn_up", h, w[tag + "_w_gu"])
    down = w[tag + "_w_down"]
    if callable(down):
        w = dict(w, **{tag + "_w_down": down(a)})
    y, out, h_next = mm_post("ffn_down", a, w[tag + "_w_down"], xs, g_post, 0.5, g_next)
    return out, h_next, dict(x=xs, h=h, gu=gu, a=a, y=y, w=w)


def _ffn_bwd(dxs, dy, sv, tag, g_pre, push, prev):
    w = sv["w"]
    gu_w, down_w = w[tag + "_w_gu"], w[tag + "_w_down"]
    dgu = ffn_down_dx("ffn_down_dx", dy, down_w, sv["gu"])
    grads = {tag + "_w_down": mm_dw("ffn_down_dw", sv["a"], dy, "row", down_w.shape),
             tag + "_w_gu": mm_dw("ffn_gu_dw", sv["h"], dgu, "col", gu_w.shape)}
    return mm_dx_norms("ffn_gu_dx", dgu, gu_w, sv["x"], g_pre, dxs, prev, push(grads))


def _mixer_fwd(xs, h, w, sm, g_post, g_next, rope, nb, s, mid):
    cos2, sin2, log_g = rope
    d = xs.shape[1]
    gate_blk = (BLK_GATE * d) // RET_V_DIM
    p = mm_fwd("mx_in", h, w["w_in"], "col", MXU_DTYPE)
    if mid is not None:
        sm = dict(sm, cf_dw_b=sm["cf_dw_b"] + mid(p))
    o = retention_fwd("ret_fwd", p, cos2, sin2, log_g, nb, s)
    ya_in = head_gate_fwd("ret_gate", o, p, gate_blk)
    yb_in, cz = short_conv_fwd("sc_fwd", p, BLK_SCB, sm["sc_conv_w"], nb)
    u1 = conformer_conv_fwd("cf_fwd", p, BLK_CFA, sm["cf_dw_w"], sm["cf_dw_b"], nb)
    yc_in = ln_silu_fwd("cf_ln", u1, sm["cf_ln_g"], sm["cf_ln_b"])
    ya = mm_fwd("mx_proj", ya_in, w["w_ret_o"], "row", F32)
    yb = mm_fwd("mx_proj", yb_in, w["w_sc_o"], "row", F32)
    yc = mm_fwd("mx_proj", yc_in, w["w_cf_o"], "row", F32)
    mg = merge_fwd("mx_merge", p, BLK_MERGE, ya, yb, yc)
    m, out, h_next = mm_post("mx_out", mg, w["w_o"], xs, g_post, 1.0, g_next)
    return out, h_next, dict(x=xs, h=h, p=p, o=o, ya_in=ya_in, yb_in=yb_in, cz=cz, u1=u1, yc_in=yc_in, ya=ya, yb=yb, yc=yc,
                     mg=mg, m=m, w=w)


def _mixer_bwd(dxs, dm, sv, sm, g_pre, rope, nb, s, push, prev):
    cos2, sin2, log_g = rope
    w, p = sv["w"], sv["p"]
    d = dxs.shape[1]
    gate_blk = (BLK_GATE * d) // RET_V_DIM
    grads, gsm = {}, {}

    def proj_bwd(wname, a_in, dy, out_dtype):
        grads[wname] = mm_dw("mx_proj_dw", a_in, dy, "row", w[wname].shape)
        return mm_dx("mx_proj_dx", dy, w[wname], "row", out_dtype)

    dmg = proj_bwd("w_o", sv["mg"], dm, MXU_DTYPE)
    dg0, dg1, dg2, dya, dyb, dyc = merge_bwd("mx_merge_bwd", p, BLK_MERGE, sv["ya"], sv["yb"], sv["yc"], dmg)
    dya_in = proj_bwd("w_ret_o", sv["ya_in"], dya, MXU_DTYPE)
    dyb_in = proj_bwd("w_sc_o", sv["yb_in"], dyb, MXU_DTYPE)
    dyc_in = proj_bwd("w_cf_o", sv["yc_in"], dyc, MXU_DTYPE)
    do, dgret = head_gate_bwd("ret_gate_bwd", sv["o"], p, gate_blk, dya_in)
    dq, dk, dv = retention_bwd("ret_bwd", p, cos2, sin2, log_g, do, nb, s)
    dscb, dscc, dscx, gsm["sc_conv_w"] = short_conv_bwd("sc_bwd", p, BLK_SCB, sm["sc_conv_w"], sv["cz"], dyb_in, nb)
    du1, dlg, dlb = ln_silu_bwd("cf_ln_bwd", sv["u1"], sm["cf_ln_g"], sm["cf_ln_b"], dyc_in)
    dcfa, dcfb, gsm["cf_dw_w"], dbias = conformer_conv_bwd("cf_bwd", p, BLK_CFA, sm["cf_dw_w"], du1, nb)
    gsm.update(cf_ln_g=dlg[0], cf_ln_b=dlb[0], cf_dw_b=dbias[0])
    dp = concat_cols("mx_dp", [dq, dk, dv, dgret, dscb, dscc, dscx, dcfa, dcfb, dg0, dg1, dg2])
    grads["w_in"] = mm_dw("mx_in_dw", sv["h"], dp, "col", w["w_in"].shape)
    return mm_dx_norms("mx_in_dx", dp, w["w_in"], sv["x"], g_pre, dxs, prev, push(grads)) + (gsm,)


def local_step(x, positions, target, small, fetch, push):
    nb, s, d = x.shape
    t = nb * s
    depth = small["norm_g"].shape[0]
    rope = _rope_tables(positions) + (_log_gamma(),)
    xs = x.reshape(t, d)
    token = [None]

    def gain(l, i):
        g = small["norm_g"][l, i][None, :]
        if token[0] is not None:
            g, token[0] = g + token[0], None
        return g

    def mixer_small(l):
        return dict(sc_conv_w=small["sc_conv_w"][l], cf_dw_w=small["cf_dw_w"][l], cf_dw_b=small["cf_dw_b"][l][None, :],
                    cf_ln_g=small["cf_ln_g"][l][None, :], cf_ln_b=small["cf_ln_b"][l][None, :])

    saved = {}
    order = [(l, blk) for l in range(depth) for blk in BLOCKS]
    h = None
    for at, (l, blk) in enumerate(order):
        w, token[0], mid = fetch(l, blk, xs)
        i0 = NORM_OF[blk]
        if h is None:
            h = rms_fwd("first_rms", xs, gain(l, i0))
        g_post = gain(l, i0 + 1)
        g_next = gain(order[at + 1][0], NORM_OF[order[at + 1][1]]) if at + 1 < len(order) else None
        if blk == "mixer":
            xs, h, saved[l, blk] = _mixer_fwd(xs, h, w, mixer_small(l), g_post, g_next, rope, nb, s, mid)
        else:
            xs, h, saved[l, blk] = _ffn_fwd(xs, h, w, blk, g_post, g_next)

    dxs, loss = loss_head("loss", xs, target.reshape(t, d))

    dnorm = [[None] * 6 for _ in range(depth)]
    gsmall = {n: [None] * depth for n in ("sc_conv_w", "cf_dw_w", "cf_dw_b", "cf_ln_g", "cf_ln_b")}
    def branch(group):
        l, blk = group
        sv = saved[group]
        return (sv["m"], gain(l, NORM_OF[blk] + 1), 1.0) if blk == "mixer" else (sv["y"], gain(l, NORM_OF[blk] + 1), 0.5)

    l, blk = order[-1]
    y, g_post, scale = branch(order[-1])
    dy, dnorm[l][NORM_OF[blk] + 1] = post_bwd("last_post_bwd", y, g_post, dxs, scale)
    for at in reversed(range(len(order))):
        l, blk = order[at]
        i0 = NORM_OF[blk]
        prev = branch(order[at - 1]) if at > 0 else None
        put = functools.partial(push, l, blk)
        if blk == "mixer":
            dxs, dnorm[l][i0], dy, dg_prev, gsm = _mixer_bwd(
                dxs, dy, saved[l, blk], mixer_small(l), gain(l, i0), rope, nb, s, put, prev)
            for n, v in gsm.items():
                gsmall[n][l] = v
        else:
            dxs, dnorm[l][i0], dy, dg_prev = _ffn_bwd(dxs, dy, saved[l, blk], blk, gain(l, i0), put, prev)
        if at > 0:
            dnorm[order[at - 1][0]][NORM_OF[order[at - 1][1]] + 1] = dg_prev

    gs = {n: jnp.stack(v) for n, v in gsmall.items()}
    gs["norm_g"] = jnp.stack([jnp.concatenate(r, axis=0) for r in dnorm])
    return loss, dxs.reshape(nb, s, d), gs


ANY = pl.BlockSpec(memory_space=pl.ANY)
HBM = pl.BlockSpec(memory_space=pltpu.HBM)
SEM = pl.BlockSpec(memory_space=pltpu.SEMAPHORE)
VMEM_WHOLE = pl.BlockSpec(memory_space=pltpu.VMEM)
EFFECT = pltpu.SideEffectType.DATAFLOW_SIDE_EFFECTING
TOKEN = jax.ShapeDtypeStruct((8, 128), F32)


def _other_chips(x, y):
    return [(1 - x, y), (x, 1 - y), (1 - x, 1 - y)]


def _remote(src, dst, send_sem, recv_sem, to):
    return pltpu.make_async_remote_copy(src_ref=src, dst_ref=dst, send_sem=send_sem, recv_sem=recv_sem,
                                        device_id=to, device_id_type=MESH)


def _in_hbm(v):
    return pltpu.with_memory_space_constraint(v, pltpu.HBM)


def place_quarters(ws, layer, ids, after):
    m = len(ws)

    def body(ids_ref, *refs):
        for w_ref, o_ref in zip(refs[:m], refs[m + 1:]):
            o_ref[...] = w_ref[...].astype(o_ref.dtype)

    def spec(w, where):
        return pl.BlockSpec((None, w.shape[1] // STREAM_STEPS, w.shape[2]), where)

    return pl.pallas_call(
        body, name="place_quarters",
        grid_spec=pltpu.PrefetchScalarGridSpec(
            num_scalar_prefetch=1, grid=(STREAM_STEPS,),
            in_specs=[spec(w, lambda i, ids_ref: (layer, i, 0)) for w in ws] + [ANY],
            out_specs=[spec(w, lambda i, ids_ref: (ids_ref[0], i, 0)) for w in ws]),
        out_shape=[jax.ShapeDtypeStruct((N_CHIP,) + w.shape[1:], MXU_DTYPE) for w in ws],
        compiler_params=_params(("parallel",)),
    )(ids, *ws, after)


def _gather_copies(lands, send, recv):
    x, y, c = _axes()
    me = 2 * x + y
    mine, theirs = [], []
    for a, ld in enumerate(lands):
        rh = ld.shape[1] // 2
        rows = pl.ds(c * rh, rh)
        for k, (px, py) in enumerate(_other_chips(x, y)):
            to = (px, py, c)
            mine.append(_remote(ld.at[me, rows, :], ld.at[me, rows, :], send.at[3 * a + k], recv.at[3 * a + k], to))
            got = ld.at[2 * px + py, rows, :]
            theirs.append(_remote(got, got, send.at[3 * a + k], recv.at[3 * a + k], to))
    return mine, theirs


def gather_start(name, groups, after):
    flat = [s for g in groups for s in g]
    n, ng = len(flat), len(groups)
    sizes = [len(g) for g in groups]

    def body(*refs):
        lands = refs[:n]
        sems = refs[n + 1:n + 1 + 2 * ng]
        token = refs[-1]
        at = 0
        for g, m in enumerate(sizes):
            mine, _ = _gather_copies(lands[at:at + m], sems[2 * g], sems[2 * g + 1])
            for cp in mine:
                cp.start()
            at += m
        token[...] = jnp.zeros_like(token)

    sem_shapes = []
    for m in sizes:
        sem_shapes += [pltpu.SemaphoreType.DMA((3 * m,))] * 2
    res = pl.pallas_call(
        body, name=name, in_specs=[HBM] * n + [ANY],
        out_specs=[SEM] * (2 * ng) + [HBM] * n + [VMEM_WHOLE],
        out_shape=sem_shapes + [pltpu.HBM(s.shape, s.dtype) for s in flat] + [TOKEN],
        input_output_aliases={i: 2 * ng + i for i in range(n)},
        compiler_params=pltpu.CompilerParams(has_side_effects=EFFECT),
    )(*[_in_hbm(s) for s in flat], after)
    sems, thru, token = res[:2 * ng], res[2 * ng:2 * ng + n], res[-1]
    out, at = [], 0
    for g, m in enumerate(sizes):
        out.append((sems[2 * g], sems[2 * g + 1], thru[at:at + m]))
        at += m
    return out, token


def gather_wait(lands, send, recv, after):
    m = len(lands)

    def body(*refs):
        mine, theirs = _gather_copies(refs[:m], refs[m], refs[m + 1])
        for cp in mine:
            cp.wait_send()
        for cp in theirs:
            cp.wait_recv()

    return pl.pallas_call(
        body, name="gather_wait", in_specs=[HBM] * m + [SEM, SEM, ANY], out_specs=[HBM] * m,
        out_shape=[pltpu.HBM(l.shape, l.dtype) for l in lands],
        input_output_aliases={i: i for i in range(m)},
        compiler_params=pltpu.CompilerParams(has_side_effects=EFFECT),
    )(*lands, send, recv, after)


def copy_start(name, families, after=()):
    sizes = [len(f[0]) for f in families]
    n, k, nf = sum(sizes), len(after), len(families)

    def body(*refs):
        at = 0
        for f, (_, copies, _) in enumerate(families):
            for cp in copies(refs[at:at + sizes[f]], refs[n + k + 2 * f], refs[n + k + 2 * f + 1])[0]:
                cp.start()
            at += sizes[f]
        refs[-1][...] = jnp.zeros_like(refs[-1])

    flat = [b for f in families for b in f[0]]
    sems = [pltpu.SemaphoreType.DMA((f[2],)) for f in families for _ in range(2)]
    res = pl.pallas_call(
        body, name=name, in_specs=[HBM] * n + [ANY] * k, out_specs=[SEM] * (2 * nf) + [HBM] * n + [VMEM_WHOLE],
        out_shape=sems + [pltpu.HBM(b.shape, b.dtype) for b in flat] + [TOKEN],
        input_output_aliases={i: 2 * nf + i for i in range(n)},
        compiler_params=pltpu.CompilerParams(has_side_effects=EFFECT),
    )(*[_in_hbm(b) for b in flat], *after)
    out, at = [], 2 * nf
    for f in range(nf):
        out.append((res[2 * f], res[2 * f + 1], list(res[at:at + sizes[f]])))
        at += sizes[f]
    return out, res[-1]


def copy_wait(name, bufs, send, recv, copies, after=()):
    n = len(bufs)

    def body(*refs):
        mine, theirs = copies(refs[:n], refs[n], refs[n + 1])
        for cp in mine:
            cp.wait_send()
        for cp in theirs:
            cp.wait_recv()

    return list(pl.pallas_call(
        body, name=name, in_specs=[HBM] * n + [SEM, SEM] + [ANY] * len(after), out_specs=[HBM] * n,
        out_shape=[pltpu.HBM(b.shape, b.dtype) for b in bufs], input_output_aliases={i: i for i in range(n)},
        compiler_params=pltpu.CompilerParams(has_side_effects=EFFECT),
    )(*bufs, send, recv, *after))


def _fill_copies(lands, send, recv):
    x, y, c = _axes()
    sib = (x, y, 1 - c)
    mine, theirs = [], []
    for a, ld in enumerate(lands):
        rh = ld.shape[1] // 2
        for k, (px, py) in enumerate(_other_chips(x, y)):
            got = ld.at[2 * px + py, pl.ds(c * rh, rh), :]
            mine.append(_remote(got, got, send.at[3 * a + k], recv.at[3 * a + k], sib))
            blk = ld.at[2 * px + py, pl.ds((1 - c) * rh, rh), :]
            theirs.append(_remote(blk, blk, send.at[3 * a + k], recv.at[3 * a + k], sib))
    return mine, theirs


def _presum_copies(grads, lands, send, recv):
    x, y, c = _axes()
    cps = []
    for a, (g, ld) in enumerate(zip(grads, lands)):
        rh = g.shape[1] // 2
        cps.append(_remote(g.at[:, pl.ds((1 - c) * rh, rh), :], ld, send.at[a], recv.at[a], (x, y, 1 - c)))
    return cps


def presum_wait(grads, lands, send, recv, after):
    m = len(grads)

    def body(*refs):
        for cp in _presum_copies(refs[:m], refs[m:2 * m], refs[2 * m], refs[2 * m + 1]):
            cp.wait_send()
            cp.wait_recv()

    res = pl.pallas_call(
        body, name="presum_wait", in_specs=[HBM] * (2 * m) + [SEM, SEM] + [ANY] * len(after),
        out_specs=[HBM] * (2 * m),
        out_shape=[pltpu.HBM(g.shape, g.dtype) for g in grads] + [pltpu.HBM(l.shape, l.dtype) for l in lands],
        input_output_aliases={i: i for i in range(2 * m)},
        compiler_params=pltpu.CompilerParams(has_side_effects=EFFECT),
    )(*grads, *lands, send, recv, *after)
    return res[:m], res[m:]


def add_halves(gs, lands, ids):
    m = len(gs)

    def body(ids_ref, *refs):
        for a_ref, b_ref, o_ref in zip(refs[:m], refs[m:2 * m], refs[2 * m:]):
            o_ref[...] = (a_ref[...].astype(F32) + b_ref[...].astype(F32)).astype(o_ref.dtype)

    def spec(ld, where):
        return pl.BlockSpec((None,) + ld.shape[1:], where)

    return pl.pallas_call(
        body, name="add_halves",
        grid_spec=pltpu.PrefetchScalarGridSpec(
            num_scalar_prefetch=1, grid=(N_CHIP,),
            in_specs=[spec(ld, lambda i, ids_ref: (i, ids_ref[1], 0)) for ld in lands]
            + [spec(ld, lambda i, ids_ref: (i, 0, 0)) for ld in lands],
            out_specs=[spec(ld, lambda i, ids_ref: (i, 0, 0)) for ld in lands]),
        out_shape=[jax.ShapeDtypeStruct(ld.shape, ld.dtype) for ld in lands],
        compiler_params=_params(("parallel",)),
    )(ids, *gs, *lands)


def _scatter_copies(parts, lands, send, recv):
    x, y, c = _axes()
    cps = []
    for a, (pt, ld) in enumerate(zip(parts, lands)):
        for k, (px, py) in enumerate(_other_chips(x, y)):
            cps.append(_remote(pt.at[2 * px + py], ld.at[k], send.at[3 * a + k], recv.at[3 * a + k], (px, py, c)))
    return cps


def scatter_wait(parts, lands, send, recv, after):
    m = len(parts)

    def body(*refs):
        for cp in _scatter_copies(refs[:m], refs[m:2 * m], refs[2 * m], refs[2 * m + 1]):
            cp.wait_send()
            cp.wait_recv()

    res = pl.pallas_call(
        body, name="scatter_wait", in_specs=[HBM] * (2 * m) + [SEM, SEM] + [ANY] * len(after),
        out_specs=[HBM] * (2 * m),
        out_shape=[pltpu.HBM(p.shape, p.dtype) for p in parts] + [pltpu.HBM(l.shape, l.dtype) for l in lands],
        input_output_aliases={i: i for i in range(2 * m)},
        compiler_params=pltpu.CompilerParams(has_side_effects=EFFECT),
    )(*parts, *lands, send, recv, *after)
    return res[:m], res[m:]


def sum_partials(parts, lands, ids, layer, depth, intos):
    m = len(parts)
    nt = STREAM_STEPS

    def body(ids_ref, *refs):
        for p_ref, l_ref, o_ref in zip(refs[:m], refs[m:2 * m], refs[-m:]):
            acc = p_ref[...].astype(F32)
            for k in range(N_CHIP - 1):
                acc = acc + l_ref[k].astype(F32)
            o_ref[...] = acc

    def rows(p):
        return p.shape[1] // nt

    in_specs = [pl.BlockSpec((None, rows(p), p.shape[2]), lambda i, ids_ref: (ids_ref[0], i, 0)) for p in parts]
    in_specs += [pl.BlockSpec((N_CHIP - 1, rows(p), p.shape[2]), lambda i, ids_ref: (0, i, 0)) for p in parts]
    args = [ids, *parts, *lands]
    aliases = {}
    if intos is not None:
        in_specs += [ANY] * m
        args += list(intos)
        aliases = {1 + 2 * m + a: a for a in range(m)}
    return pl.pallas_call(
        body, name="sum_partials",
        grid_spec=pltpu.PrefetchScalarGridSpec(
            num_scalar_prefetch=1, grid=(nt,), in_specs=in_specs,
            out_specs=[pl.BlockSpec((None, rows(p), p.shape[2]), lambda i, ids_ref: (layer, ids_ref[1] * nt + i, 0))
                       for p in parts]),
        out_shape=[jax.ShapeDtypeStruct((depth, 2 * p.shape[1], p.shape[2]), F32) for p in parts],
        input_output_aliases=aliases, compiler_params=_params(("parallel",)),
    )(*args)


def _final_copies(layer):
    def copies(bufs, send, recv):
        x, y, c = _axes()
        sib = (x, y, 1 - c)
        mine, theirs = [], []
        for a, buf in enumerate(bufs):
            rh = buf.shape[1] // 2
            src = buf.at[layer, pl.ds(c * rh, rh), :]
            mine.append(_remote(src, src, send.at[a], recv.at[a], sib))
            dst = buf.at[layer, pl.ds((1 - c) * rh, rh), :]
            theirs.append(_remote(dst, dst, send.at[a], recv.at[a], sib))
        return mine, theirs

    return copies


def allgather_small(pk):
    def body(in_ref, out_ref, send, recv):
        x, y, c = _axes()
        me = 2 * x + y
        chips = _other_chips(x, y)
        out_ref[pl.ds(me, 1)] = in_ref[...][None]
        cps = []
        for k, (px, py) in enumerate(chips):
            cp = _remote(in_ref, out_ref.at[me], send.at[k], recv.at[k], (px, py, c))
            cp.start()
            cps.append(cp)
        for k, (px, py) in enumerate(chips):
            got = out_ref.at[2 * px + py]
            _remote(got, got, send.at[k], recv.at[k], (px, py, c)).wait_recv()
        for cp in cps:
            cp.wait_send()

    return pl.pallas_call(
        body, name="allgather_small", in_specs=[VMEM_WHOLE], out_specs=VMEM_WHOLE,
        out_shape=jax.ShapeDtypeStruct((N_CHIP,) + pk.shape, pk.dtype),
        scratch_shapes=[pltpu.SemaphoreType.DMA((3,))] * 2,
    )(pk)


N_DEV = 8


def _small_copies(bufs, send, recv):
    g, slots = bufs
    x, y, c = _axes()
    me = 4 * x + 2 * y + c
    mine, theirs = [], []
    for mask in range(1, N_DEV):
        px = 1 - x if mask & 4 else x
        py = 1 - y if mask & 2 else y
        pc = 1 - c if mask & 1 else c
        mine.append(_remote(g, slots.at[me], send.at[mask - 1], recv.at[mask - 1], (px, py, pc)))
        got = slots.at[4 * px + 2 * py + pc]
        theirs.append(_remote(got, got, send.at[mask - 1], recv.at[mask - 1], (px, py, pc)))
    return mine, theirs


def sum_slots(g, slots, me):
    def body(me_ref, g_ref, slots_ref, o_ref):
        acc = None
        for d in range(N_DEV):
            term = jnp.where(me_ref[0] == d, g_ref[...], slots_ref[d])
            acc = term if acc is None else acc + term
        o_ref[...] = acc

    return pl.pallas_call(
        body, name="sum_slots",
        grid_spec=pltpu.PrefetchScalarGridSpec(
            num_scalar_prefetch=1, grid=(1,),
            in_specs=[pl.BlockSpec(g.shape, lambda i, me_ref: (0, 0)),
                      pl.BlockSpec(slots.shape, lambda i, me_ref: (0, 0, 0))],
            out_specs=pl.BlockSpec(g.shape, lambda i, me_ref: (0, 0))),
        out_shape=jax.ShapeDtypeStruct(g.shape, g.dtype),
        compiler_params=_params(("arbitrary",)),
    )(me, g, slots)


def adamw(w, g, m, v, layer=None, intos=None):
    shape = w.shape
    cols = shape[-1]
    rows = int(np.prod(shape[:-1]))
    span = rows if layer is None else rows // shape[0]
    tr = span
    for cand in (256, 128):
        if span % cand == 0 and cand * cols * 4 <= 2 * 1024 * 1024:
            tr = cand
            break
    first = 0 if layer is None else layer * (span // tr)
    c1 = 1.0 - ADAM_B1 ** ADAM_STEP
    c2 = 1.0 - ADAM_B2 ** ADAM_STEP

    def body(w_ref, g_ref, m_ref, v_ref, *rest):
        d_ref, nm_ref, nv_ref, g_out = rest[-4:]
        gv = g_ref[...]
        g_out[...] = gv
        nm = ADAM_B1 * m_ref[...] + (1.0 - ADAM_B1) * gv
        nv = ADAM_B2 * v_ref[...] + (1.0 - ADAM_B2) * jnp.square(gv)
        d_ref[...] = -ADAM_LR * ((nm / c1) / (jnp.sqrt(nv / c2) + ADAM_EPS) + ADAM_WD * w_ref[...])
        nm_ref[...] = nm
        nv_ref[...] = nv

    spec = pl.BlockSpec((tr, cols), lambda i: (first + i, 0))
    args = [a.reshape(rows, cols) for a in (w, g, m, v)]
    in_specs, aliases = [spec] * 4, {}
    if intos is not None:
        args += [a.reshape(rows, cols) for a in intos]
        in_specs += [ANY] * 4
        aliases = {4 + k: k for k in range(4)}
    res = pl.pallas_call(
        body, name="adamw", grid=(span // tr,), in_specs=in_specs, out_specs=[spec] * 4,
        out_shape=[jax.ShapeDtypeStruct((rows, cols), F32)] * 4, input_output_aliases=aliases,
        compiler_params=_params(("parallel",)),
    )(*args)
    return [r.reshape(shape) for r in res]


WEIGHTS = ("norm_g", "ffn1_w_gu", "ffn1_w_down", "w_in", "w_ret_o", "sc_conv_w", "w_sc_o", "cf_dw_w", "cf_dw_b",
           "cf_ln_g", "cf_ln_b", "w_cf_o", "w_o", "ffn2_w_gu", "ffn2_w_down")
SHARDED_SMALL = ("norm_g", "sc_conv_w", "cf_dw_w")
REPLICATED_SMALL = ("cf_dw_b", "cf_ln_g", "cf_ln_b")

def _pack_rows(parts):
    padded, offs, at = [], [], 0
    for p in parts:
        r = -(-p.shape[0] // SUBLANES) * SUBLANES
        padded.append(jnp.pad(p, ((0, r - p.shape[0]), (0, 0))))
        offs.append(at)
        at += r
    return jnp.concatenate(padded, axis=0), offs


def kernel(x, positions, norm_g, ffn1_w_gu, ffn1_w_down, w_in, w_ret_o, sc_conv_w, w_sc_o, cf_dw_w, cf_dw_b, cf_ln_g, cf_ln_b, w_cf_o, w_o, ffn2_w_gu, ffn2_w_down, loss_target, m_norm_g, m_ffn1_w_gu, m_ffn1_w_down, m_w_in, m_w_ret_o, m_sc_conv_w, m_w_sc_o, m_cf_dw_w, m_cf_dw_b, m_cf_ln_g, m_cf_ln_b, m_w_cf_o, m_w_o, m_ffn2_w_gu, m_ffn2_w_down, v_norm_g, v_ffn1_w_gu, v_ffn1_w_down, v_w_in, v_w_ret_o, v_sc_conv_w, v_w_sc_o, v_cf_dw_w, v_cf_dw_b, v_cf_ln_g, v_cf_ln_b, v_w_cf_o, v_w_o, v_ffn2_w_gu, v_ffn2_w_down):
    wts = dict(zip(WEIGHTS, (norm_g, ffn1_w_gu, ffn1_w_down, w_in, w_ret_o, sc_conv_w, w_sc_o, cf_dw_w, cf_dw_b,
                             cf_ln_g, cf_ln_b, w_cf_o, w_o, ffn2_w_gu, ffn2_w_down)))
    mom = dict(zip(WEIGHTS, (m_norm_g, m_ffn1_w_gu, m_ffn1_w_down, m_w_in, m_w_ret_o, m_sc_conv_w, m_w_sc_o,
                             m_cf_dw_w, m_cf_dw_b, m_cf_ln_g, m_cf_ln_b, m_w_cf_o, m_w_o, m_ffn2_w_gu, m_ffn2_w_down)))
    var = dict(zip(WEIGHTS, (v_norm_g, v_ffn1_w_gu, v_ffn1_w_down, v_w_in, v_w_ret_o, v_sc_conv_w, v_w_sc_o,
                             v_cf_dw_w, v_cf_dw_b, v_cf_ln_g, v_cf_ln_b, v_w_cf_o, v_w_o, v_ffn2_w_gu, v_ffn2_w_down)))
    depth = norm_g.shape[0]
    dq = norm_g.shape[-1]
    d = N_CHIP * dq
    chip = 2 * lax.axis_index("x") + lax.axis_index("y")
    ids = jnp.stack([chip, lax.axis_index("c")]).astype(jnp.int32)

    pk, offs = _pack_rows([wts[n].reshape(-1, dq) for n in SHARDED_SMALL])
    gk4 = allgather_small(pk)
    gk = gk4.transpose(1, 0, 2).reshape(pk.shape[0], d)
    small = {n: wts[n] for n in REPLICATED_SMALL}
    for n, o in zip(SHARDED_SMALL, offs):
        rows = wts[n].shape[0] * wts[n].shape[1]
        small[n] = gk[o:o + rows].reshape(wts[n].shape[:2] + (d,))

    order = [(l, blk) for l in range(depth) for blk in BLOCKS]
    def placed(groups, after):
        return [place_quarters([wts[n] for n in names], l, ids, after) for l, names in groups]

    l0, blk0 = order[0]
    gather_order = [(l0, (n,)) for n in BLOCK_WEIGHTS[blk0]] + [(l, BLOCK_WEIGHTS[blk]) for l, blk in order[1:]]
    first, token = gather_start("gather_start_first", placed(gather_order[:1], gk4), gk4)
    rest, token = gather_start("gather_start_rest", placed(gather_order[1:], token), token)
    started = dict(zip(gather_order, first + rest))
    small["norm_g"] = small["norm_g"] + token[0:1, 0:1]

    filling = {}

    def fill(group, after):
        send, recv, lands = started[group]
        lands = gather_wait(lands, send, recv, after)
        started_fill, tok = copy_start("fill_start", [(lands, _fill_copies, 3 * len(lands))])
        filling[group] = started_fill[0]
        return tok[0:1, 0:1]

    def filled(group, after):
        if group not in filling:
            fill(group, after)
        send, recv, lands = filling.pop(group)
        return copy_wait("fill_wait", lands, send, recv, _fill_copies, (after,))

    def fetch(l, blk, after):
        at = order.index((l, blk))
        if at == 0:
            names = BLOCK_WEIGHTS[blk]
            lazy = {n: functools.partial(lambda n, behind: filled((l, (n,)), behind)[0], n) for n in names[1:]}
            return dict(lazy, **{names[0]: filled((l, names[:1]), token)[0]}), None, None
        lands = filled((l, BLOCK_WEIGHTS[blk]), after)
        tok, mid = None, None
        ahead = (order[at + 1][0], BLOCK_WEIGHTS[order[at + 1][1]]) if at + 1 < len(order) else None
        if at == 1:
            mid = functools.partial(fill, ahead)
        elif ahead is not None:
            tok = fill(ahead, lands[0])
        return dict(zip(BLOCK_WEIGHTS[blk], lands)), tok, mid

    gsum = {n: None for n in BIG}
    presums, scatters, finals = [], [], []

    def scatter_ready(after):
        group, gl, lands, send, recv = presums.pop(0)
        gl, lands = presum_wait(gl, lands, send, recv, after)
        parts = list(add_halves(gl, lands, ids))
        m = len(parts)
        lands = [lax.empty((N_CHIP - 1,) + p.shape[1:], p.dtype) for p in parts]
        family = (parts + lands, lambda refs, sd, rv: (_scatter_copies(refs[:m], refs[m:], sd, rv),) * 2, 3 * m)
        return family, lambda sd, rv, bufs: scatters.append((group, bufs[:m], bufs[m:], sd, rv))

    def final_ready(after):
        (l, blk), parts, lands, send, recv = scatters.pop(0)
        parts, lands = scatter_wait(parts, lands, send, recv, after)
        names = BLOCK_WEIGHTS[blk]
        intos = None if gsum[names[0]] is None else [gsum[n] for n in names]
        sums = list(sum_partials(parts, lands, ids, l, depth, intos))

        def note(sd, rv, bufs):
            gsum.update(zip(names, bufs))
            finals.append((names, l, sd, rv))

        return (sums, _final_copies(l), len(sums)), note

    def start_all(name, ready, after=()):
        started, tok = copy_start(name, [family for family, _ in ready], after)
        for (_, note), (sd, rv, bufs) in zip(ready, started):
            note(sd, rv, bufs)
        return tok

    def scatter_next(after):
        return start_all("scatter_start", [scatter_ready(after)])

    def sum_next(after):
        return start_all("final_start", [final_ready(after)])

    def final_next(after):
        names, l, send, recv = finals.pop(0)
        gsum.update(zip(names, copy_wait("final_wait", [gsum[n] for n in names], send, recv, _final_copies(l), after)))

    def push(l, blk, grads):
        gl = [grads[n] for n in BLOCK_WEIGHTS[blk]]
        m = len(gl)
        lands = [lax.empty((g.shape[0], g.shape[1] // 2, g.shape[2]), g.dtype) for g in gl]
        ready = [((gl + lands, lambda refs, sd, rv: (_presum_copies(refs[:m], refs[m:], sd, rv),) * 2, m),
                  lambda sd, rv, bufs: presums.append(((l, blk), bufs[:m], bufs[m:], sd, rv)))]
        if scatters:
            ready.append(final_ready((gl[0],)))
        if presums:
            ready.append(scatter_ready((gl[0],)))
        return start_all("push_start", ready)[0:1, 0:1]

    loss, grad_x, gs = local_step(x, positions, loss_target, small, fetch, push)

    names = SHARDED_SMALL + REPLICATED_SMALL
    pg, offs = _pack_rows([gs[n].reshape(-1, d) for n in names])
    small_bufs = [pg, lax.empty((N_DEV,) + pg.shape, pg.dtype)]
    ((s_send, s_recv, s_bufs),), tok = copy_start("small_start", [(small_bufs, _small_copies, N_DEV - 1)], (grad_x,))
    tok = scatter_next((grad_x, tok))

    delta, new_m, new_v, grads = {}, {}, {}, {}

    def update(n, layer=None):
        g = gsum[n] if n in BIG else grads[n]
        prev = [delta[n], new_m[n], new_v[n], grads[n]] if layer is not None and n in delta else None
        delta[n], new_m[n], new_v[n], grads[n] = adamw(wts[n], g, mom[n], var[n], layer, prev)

    while finals and finals[0][1] > 0:
        done, l = finals[0][:2]
        final_next((tok,))
        for n in done:
            update(n, l)
    upper = tuple(delta[n] for n in BIG if n in delta)
    pg, slots = copy_wait("small_wait", s_bufs, s_send, s_recv, _small_copies, upper + (tok,))
    me = (2 * chip + lax.axis_index("c")).astype(jnp.int32).reshape(1)
    tot = sum_slots(pg, slots, me)
    for n, o in zip(names, offs):
        rows = int(np.prod(gs[n].shape[:-1]))
        full = tot[o:o + rows]
        if n in SHARDED_SMALL:
            full = lax.dynamic_slice_in_dim(full, chip * dq, dq, axis=1)
        grads[n] = full.reshape(wts[n].shape)

    for n in names:
        update(n)
    after = tuple(delta[n] for n in names)
    while scatters or finals:
        if scatters:
            after = (sum_next(after),)
        done, l = finals[0][:2]
        final_next(after)
        for n in done:
            update(n, l)
        after = tuple(delta[n] for n in done)

    loss_all = lax.psum(loss[0, 0], ("x", "y", "c"))
    return (loss_all, grad_x, *[grads[n] for n in WEIGHTS], *[delta[n] for n in WEIGHTS],
            *[new_m[n] for n in WEIGHTS], *[new_v[n] for n in WEIGHTS])
```

```python
import functools

import jax
import jax.numpy as jnp
import numpy as np
from jax import lax
from jax.experimental import pallas as pl
from jax.experimental.pallas import tpu as pltpu

F32 = jnp.float32
BF16 = jnp.bfloat16
MXU_DTYPE = BF16
VMEM_LIMIT_BYTES = 56 * 1024 * 1024
MESH = pl.DeviceIdType.MESH

N_CHIP = 4
CHUNK = 64
RET_HEADS = 4
RET_QK_DIM = 128
RET_V_DIM = 256
SC_KERNEL = 3
CF_KERNEL = 31
ROPE_BASE = 10000.0
NORM_EPS = 1e-6
LN_EPS = 1e-5
ADAM_LR = 0.001
ADAM_B1 = 0.9
ADAM_B2 = 0.999
ADAM_EPS = 1e-08
ADAM_WD = 0.01
ADAM_STEP = 10

SUBLANES = 8
CONV_PAD = 32
CONV_TS = 256
CONV_TC = 512
CONV_ROWS = 32
CONV_TILES = range(0, CONV_ROWS, SUBLANES)
SC_TS = 512


def _conv_scratch(ts):
    return [pltpu.VMEM((ts + CONV_PAD, CONV_TC), F32),
            pltpu.VMEM((SUBLANES - 1, ts + CONV_PAD - SUBLANES, CONV_TC), F32)]
RET_TQ = 512
MM_TM = 1024
MM_TN = 1536
MM_K1 = 1024
MM_W1 = 8 << 20
MM_SLICE = 256
MM_IN_BYTES = 36 << 20
STREAM_STEPS = 2


def _params(sem):
    return pltpu.CompilerParams(dimension_semantics=sem, vmem_limit_bytes=VMEM_LIMIT_BYTES)


def _axes():
    return lax.axis_index("x"), lax.axis_index("y"), lax.axis_index("c")


NN = (((1,), (0,)), ((), ()))
NT = (((1,), (1,)), ((), ()))
TN = (((0,), (0,)), ((), ()))


def _mm(name, a, b, out_shape, out_dtype, grid, a_spec, b_spec, o_spec, dims, acc_shape):
    nk = grid[2]

    def body(a_ref, b_ref, o_ref, *scratch):
        bv = b_ref[...]
        if bv.ndim == 3:
            bv = bv.reshape(-1, bv.shape[-1])
        part = lax.dot_general(a_ref[...], bv, dims, preferred_element_type=F32)

        def put(v):
            o_ref[...] = v.reshape(o_ref.shape).astype(o_ref.dtype)

        if nk == 1:
            put(part)
        else:
            acc = scratch[0]
            k = pl.program_id(2)

            @pl.when(k == 0)
            def _():
                acc[...] = part

            @pl.when(k > 0)
            def _():
                acc[...] += part

            @pl.when(k == nk - 1)
            def _():
                put(acc[...])

    scratch = [pltpu.VMEM(acc_shape, F32)] if nk > 1 else []
    return pl.pallas_call(
        body, name=name, grid=grid, in_specs=[a_spec, b_spec], out_specs=o_spec,
        out_shape=jax.ShapeDtypeStruct(out_shape, out_dtype), scratch_shapes=scratch,
        compiler_params=_params(("parallel", "parallel", "arbitrary")),
    )(a, b)


def _tile(n, target):
    best = None
    for t in range(128, min(n, target) + 1, 128):
        if n % t == 0:
            best = t
    assert best is not None, (n, target)
    return best


def _token_rows(t, width):
    tt = t
    while tt > MM_TM and tt * width * jnp.dtype(MXU_DTYPE).itemsize * 2 > MM_IN_BYTES:
        tt //= 2
    return tt


def mm_fwd(name, a, w4, mode, out_dtype):
    t = a.shape[0]
    _, r, c = w4.shape
    tm = min(t, MM_TM)
    if mode == "col":
        tn = _tile(c, MM_TN)
        npj = c // tn
        grid = (t // tm, N_CHIP * npj, 1)
        a_spec = pl.BlockSpec((tm, r), lambda i, j, k: (i, 0))
        b_spec = pl.BlockSpec((None, r, tn), lambda i, j, k: (j // npj, 0, j % npj))
        o_spec = pl.BlockSpec((tm, tn), lambda i, j, k: (i, j))
        return _mm(name, a, w4, (t, N_CHIP * c), out_dtype, grid, a_spec, b_spec, o_spec, NN, (tm, tn))
    if w4.size * w4.dtype.itemsize <= MM_W1:
        grid = (t // tm, 1, 1)
        a_spec = pl.BlockSpec((tm, N_CHIP * r), lambda i, j, k: (i, 0))
        b_spec = pl.BlockSpec((N_CHIP, r, c), lambda i, j, k: (0, 0, 0))
        o_spec = pl.BlockSpec((tm, c), lambda i, j, k: (i, 0))
        return _mm(name, a, w4, (t, c), out_dtype, grid, a_spec, b_spec, o_spec, NN, (tm, c))
    grid = (t // tm, 1, N_CHIP)
    a_spec = pl.BlockSpec((tm, r), lambda i, j, k: (i, k))
    b_spec = pl.BlockSpec((None, r, c), lambda i, j, k: (k, 0, 0))
    o_spec = pl.BlockSpec((tm, c), lambda i, j, k: (i, 0))
    return _mm(name, a, w4, (t, c), out_dtype, grid, a_spec, b_spec, o_spec, NN, (tm, c))


def mm_dx(name, dy, w4, mode, out_dtype):
    t = dy.shape[-2]
    _, r, c = w4.shape
    tm = min(t, MM_TM)
    if mode == "col":
        tn, npj = c, 1
        hb = N_CHIP // 2 * npj
        grid = (t // tm, 1, N_CHIP * npj)
        if dy.ndim == 3:
            a_spec = pl.BlockSpec((None, tm, tn), lambda i, j, k: (k // hb, i, k % hb))
        else:
            a_spec = pl.BlockSpec((tm, tn), lambda i, j, k: (i, k))
        b_spec = pl.BlockSpec((None, r, tn), lambda i, j, k: (k // npj, 0, k % npj))
        o_spec = pl.BlockSpec((tm, r), lambda i, j, k: (i, 0))
        return _mm(name, dy, w4, (t, r), out_dtype, grid, a_spec, b_spec, o_spec, NT, (tm, r))
    if N_CHIP * r <= MM_K1:
        grid = (t // tm, 1, 1)
        a_spec = pl.BlockSpec((tm, c), lambda i, j, k: (i, 0))
        b_spec = pl.BlockSpec((N_CHIP, r, c), lambda i, j, k: (0, 0, 0))
        o_spec = pl.BlockSpec((tm, N_CHIP * r), lambda i, j, k: (i, 0))
        return _mm(name, dy, w4, (t, N_CHIP * r), out_dtype, grid, a_spec, b_spec, o_spec, NT, (tm, N_CHIP * r))
    grid = (t // tm, N_CHIP, 1)
    a_spec = pl.BlockSpec((tm, c), lambda i, j, k: (i, 0))
    b_spec = pl.BlockSpec((None, r, c), lambda i, j, k: (j, 0, 0))
    o_spec = pl.BlockSpec((tm, r), lambda i, j, k: (i, j))
    return _mm(name, dy, w4, (t, N_CHIP * r), out_dtype, grid, a_spec, b_spec, o_spec, NT, (tm, r))


def mm_dw(name, a, dy, mode, shape3):
    t = a.shape[0]
    _, r, c = shape3
    if mode == "col":
        tn = _tile(c, MM_TN)
        npj = c // tn
        tt = _token_rows(t, r + tn)
        grid = (1, N_CHIP * npj, t // tt)
        a_spec = pl.BlockSpec((tt, r), lambda i, j, k: (k, 0))
        hb = N_CHIP // 2 * npj
        if dy.ndim == 3:
            b_spec = pl.BlockSpec((None, tt, tn), lambda i, j, k: (j // hb, k, j % hb))
        else:
            b_spec = pl.BlockSpec((tt, tn), lambda i, j, k: (k, j))
        o_spec = pl.BlockSpec((None, r, tn), lambda i, j, k: (j // npj, 0, j % npj))
        return _mm(name, a, dy, shape3, MXU_DTYPE, grid, a_spec, b_spec, o_spec, TN, (r, tn))
    if N_CHIP * r <= MM_K1:
        tt = min(_token_rows(t, N_CHIP * r + c), max(t // 4, MM_TM))
        grid = (1, 1, t // tt)
        a_spec = pl.BlockSpec((tt, N_CHIP * r), lambda i, j, k: (k, 0))
        b_spec = pl.BlockSpec((tt, c), lambda i, j, k: (k, 0))
        o_spec = pl.BlockSpec((N_CHIP, r, c), lambda i, j, k: (0, 0, 0))
        return _mm(name, a, dy, shape3, MXU_DTYPE, grid, a_spec, b_spec, o_spec, TN, (N_CHIP * r, c))
    tt = _token_rows(t, r + c)
    grid = (N_CHIP, 1, t // tt)
    a_spec = pl.BlockSpec((tt, r), lambda i, j, k: (k, i))
    b_spec = pl.BlockSpec((tt, c), lambda i, j, k: (k, 0))
    o_spec = pl.BlockSpec((None, r, c), lambda i, j, k: (i, 0, 0))
    return _mm(name, a, dy, shape3, MXU_DTYPE, grid, a_spec, b_spec, o_spec, TN, (r, c))


def _rms_bwd(x, g, dh):
    r = lax.rsqrt(jnp.mean(x * x, axis=-1, keepdims=True) + NORM_EPS)
    xhat = x * r
    dyg = dh * g
    dx = r * (dyg - xhat * jnp.mean(dyg * xhat, axis=-1, keepdims=True))
    return dx, jnp.sum(dh * xhat, axis=0, keepdims=True)


def mm_dx_norms(name, dy, w4, x, g_pre, dres, prev, after):
    t = dy.shape[-2]
    _, r, c = w4.shape
    tm = min(t, MM_TM // 2)
    nt, nk = t // tm, N_CHIP
    hb = N_CHIP // 2
    chained = prev is not None

    def body(dy_ref, w_ref, x_ref, dres_ref, g_ref, *rest):
        rest = rest[1:] if after is not None else rest
        if chained:
            y_ref, gp_ref, dx_ref, dg_ref, dyp_ref, dgp_ref, acc = rest
        else:
            dx_ref, dg_ref, acc = rest
        i, k = pl.program_id(0), pl.program_id(1)
        part = lax.dot_general(dy_ref[...], w_ref[...], NT, preferred_element_type=F32)

        @pl.when(k == 0)
        def _():
            acc[...] = part

        @pl.when(k > 0)
        def _():
            acc[...] += part

        def add_to(ref, v):
            @pl.when(i == 0)
            def _():
                ref[...] = v

            @pl.when(i > 0)
            def _():
                ref[...] += v

        @pl.when(k == nk - 1)
        def _():
            dx, dg = _rms_bwd(x_ref[...], g_ref[...], acc[...])
            dxs = dres_ref[...] + dx
            dx_ref[...] = dxs
            add_to(dg_ref, dg)
            if chained:
                dyp, dgp = _rms_bwd(y_ref[...], gp_ref[...], dxs)
                dyp_ref[...] = (prev[2] * dyp).astype(dyp_ref.dtype)
                add_to(dgp_ref, prev[2] * dgp)

    if dy.ndim == 3:
        dy_spec = pl.BlockSpec((None, tm, c), lambda i, k: (k // hb, i, k % hb))
    else:
        dy_spec = pl.BlockSpec((tm, c), lambda i, k: (i, k))
    rows = pl.BlockSpec((tm, r), lambda i, k: (i, 0))
    gain = pl.BlockSpec((1, r), lambda i, k: (0, 0))
    in_specs = [dy_spec, pl.BlockSpec((None, r, c), lambda i, k: (k, 0, 0)), rows, rows, gain]
    args = [dy, w4, x, dres, g_pre]
    if after is not None:
        in_specs.append(pl.BlockSpec(memory_space=pl.ANY))
        args.append(after)
    out_specs = [rows, gain]
    out_shape = [jax.ShapeDtypeStruct((t, r), F32), jax.ShapeDtypeStruct((1, r), F32)]
    if chained:
        in_specs += [rows, gain]
        args += [prev[0], prev[1]]
        out_specs += [rows, gain]
        out_shape += [jax.ShapeDtypeStruct((t, r), MXU_DTYPE), jax.ShapeDtypeStruct((1, r), F32)]
    res = pl.pallas_call(
        body, name=name, grid=(nt, nk), in_specs=in_specs, out_specs=out_specs, out_shape=out_shape,
        scratch_shapes=[pltpu.VMEM((tm, r), F32)], compiler_params=_params(("arbitrary", "arbitrary")),
    )(*args)
    return tuple(res) if chained else (res[0], res[1], None, None)


def _rowwise(name, fn, rows, pars, outs, accs=(), tm=256, ncol=1):
    t = rows[0][0].shape[0]
    nrow, npar, nout = len(rows), len(pars), len(outs)

    def body(*refs):
        vals = [r[...] for r in refs[:nrow + npar]]
        res = fn(*vals)
        out_refs = refs[nrow + npar:nrow + npar + nout]
        acc_refs = refs[nrow + npar + nout:]
        for o, v in zip(out_refs, res[:nout]):
            o[...] = v.astype(o.dtype)
        i = pl.program_id(1)
        for a, v in zip(acc_refs, res[nout:]):
            @pl.when(i == 0)
            def _(a=a, v=v):
                a[...] = v.astype(F32)

            @pl.when(i > 0)
            def _(a=a, v=v):
                a[...] += v.astype(F32)

    in_specs = [pl.BlockSpec((tm, w), functools.partial(lambda j, i, b: (i, b + j), b=b)) for _, w, b in rows]
    for arr, w in pars:
        if w is None:
            in_specs.append(pl.BlockSpec(arr.shape, lambda j, i: (0, 0)))
        else:
            in_specs.append(pl.BlockSpec((1, w), lambda j, i: (0, j)))
    out_specs = [pl.BlockSpec((tm, w), lambda j, i: (i, j)) for _, w, _ in outs]
    out_specs += [pl.BlockSpec((1, w), lambda j, i: (0, j)) for _, w in accs]
    out_shape = [jax.ShapeDtypeStruct((t, tw), dt) for tw, _, dt in outs]
    out_shape += [jax.ShapeDtypeStruct((1, tw), F32) for tw, _ in accs]
    res = pl.pallas_call(
        body, name=name, grid=(ncol, t // tm), in_specs=in_specs, out_specs=out_specs, out_shape=out_shape,
        compiler_params=_params(("parallel", "arbitrary" if accs else "parallel")),
    )(*[r[0] for r in rows], *[p[0] for p in pars])
    return res


def _rms(x, g):
    xf = x.astype(F32)
    return xf * lax.rsqrt(jnp.mean(xf * xf, axis=-1, keepdims=True) + NORM_EPS) * g


def _silu(x):
    return x * jax.nn.sigmoid(x)


def rms_fwd(name, x, g):
    d = x.shape[1]
    return _rowwise(name, lambda x, g: (_rms(x, g),), [(x, d, 0)], [(g, None)], [(d, d, MXU_DTYPE)], tm=512)[0]


def rms_bwd(name, x, g, dh, dres):
    d = x.shape[1]

    def fn(x, dh, dres, g):
        _, vjp = jax.vjp(_rms, x, g)
        dx, dg = vjp(dh.astype(F32))
        return dres + dx, dg

    return _rowwise(name, fn, [(x, d, 0), (dh, d, 0), (dres, d, 0)], [(g, None)], [(d, d, F32)], [(d, d)], tm=256)


def mm_post(name, a, w4, x, g_post, scale, g_next):
    t = a.shape[0]
    _, r, c = w4.shape
    tm = min(t, MM_TM // 2)
    chained = g_next is not None

    def body(a_ref, w_ref, x_ref, gp_ref, *rest):
        gn_ref, y_ref, xn_ref, h_ref = rest if chained else (None,) + rest + (None,)
        y = lax.dot_general(a_ref[...], w_ref[...].reshape(N_CHIP * r, c), NN, preferred_element_type=F32)
        y_ref[...] = y
        xn = x_ref[...] + scale * _rms(y, gp_ref[...])
        xn_ref[...] = xn
        if chained:
            h_ref[...] = _rms(xn, gn_ref[...]).astype(h_ref.dtype)

    def rows(width):
        return pl.BlockSpec((tm, width), lambda i: (i, 0))

    gain = pl.BlockSpec((1, c), lambda i: (0, 0))
    in_specs = [rows(N_CHIP * r), pl.BlockSpec((N_CHIP, r, c), lambda i: (0, 0, 0)), rows(c), gain]
    args = [a, w4, x, g_post]
    out_specs, out_shape = [rows(c), rows(c)], [jax.ShapeDtypeStruct((t, c), F32)] * 2
    if chained:
        in_specs.append(gain)
        args.append(g_next)
        out_specs.append(rows(c))
        out_shape.append(jax.ShapeDtypeStruct((t, c), MXU_DTYPE))
    res = pl.pallas_call(
        body, name=name, grid=(t // tm,), in_specs=in_specs, out_specs=out_specs, out_shape=out_shape,
        compiler_params=_params(("parallel",)),
    )(*args)
    return res[0], res[1], (res[2] if chained else None)


def post_bwd(name, y, g, dx, scale):
    d = y.shape[1]

    def fn(y, dx, g):
        _, vjp = jax.vjp(lambda y, g: scale * _rms(y, g), y, g)
        return vjp(dx)

    return _rowwise(name, fn, [(y, d, 0), (dx, d, 0)], [(g, None)], [(d, d, MXU_DTYPE)], [(d, d)], tm=256)


def ffn_up(name, h, w4):
    t = h.shape[0]
    _, r, c = w4.shape
    tm = min(t, MM_TM)
    tn = _tile(c, MM_TM)
    npj = c // tn
    half = N_CHIP // 2

    def body(h_ref, wg_ref, wu_ref, gu_ref, a_ref):
        hv = h_ref[...]
        g = lax.dot_general(hv, wg_ref[...], NN, preferred_element_type=F32)
        u = lax.dot_general(hv, wu_ref[...], NN, preferred_element_type=F32)
        sg = jax.nn.sigmoid(g)
        silu = g * sg
        gu_ref[0] = (u * (sg + silu * (1.0 - sg))).astype(gu_ref.dtype)
        gu_ref[1] = silu.astype(gu_ref.dtype)
        a_ref[...] = (silu * u).astype(a_ref.dtype)

    f = half * c
    return pl.pallas_call(
        body, name=name, grid=(t // tm, half * npj),
        in_specs=[pl.BlockSpec((tm, r), lambda i, j: (i, 0)),
                  pl.BlockSpec((None, r, tn), lambda i, j: (j // npj, 0, j % npj)),
                  pl.BlockSpec((None, r, tn), lambda i, j: (half + j // npj, 0, j % npj))],
        out_specs=[pl.BlockSpec((2, tm, tn), lambda i, j: (0, i, j)), pl.BlockSpec((tm, tn), lambda i, j: (i, j))],
        out_shape=[jax.ShapeDtypeStruct((2, t, f), MXU_DTYPE), jax.ShapeDtypeStruct((t, f), MXU_DTYPE)],
        compiler_params=_params(("parallel", "parallel")),
    )(h, w4, w4)


def ffn_down_dx(name, dy, w4, gu):
    t = dy.shape[0]
    _, r, c = w4.shape
    tm = min(t, MM_TM)

    def body(dy_ref, w_ref, gu_ref, o_ref):
        dyv = dy_ref[...]
        for n0 in range(0, r, MM_SLICE):
            cols = pl.ds(n0, MM_SLICE)
            da = lax.dot_general(dyv, w_ref[cols, :], NT, preferred_element_type=F32)
            o_ref[0, :, cols] = (da * gu_ref[0, :, cols].astype(F32)).astype(o_ref.dtype)
            o_ref[1, :, cols] = (da * gu_ref[1, :, cols].astype(F32)).astype(o_ref.dtype)

    return pl.pallas_call(
        body, name=name, grid=(t // tm, N_CHIP),
        in_specs=[pl.BlockSpec((tm, c), lambda i, j: (i, 0)), pl.BlockSpec((None, r, c), lambda i, j: (j, 0, 0)),
                  pl.BlockSpec((2, tm, r), lambda i, j: (0, i, j))],
        out_specs=pl.BlockSpec((2, tm, r), lambda i, j: (0, i, j)),
        out_shape=jax.ShapeDtypeStruct((2, t, N_CHIP * r), MXU_DTYPE),
        compiler_params=_params(("parallel", "parallel")),
    )(dy, w4, gu)


def _head_gate(o, g):
    mu = jnp.mean(o, axis=-1, keepdims=True)
    var = jnp.mean(jnp.square(o - mu), axis=-1, keepdims=True)
    return _silu(g.astype(F32)) * ((o - mu) * lax.rsqrt(var + LN_EPS))


def head_gate_fwd(name, o, p, gate_blk):
    dv = RET_V_DIM
    return _rowwise(name, lambda o, g: (_head_gate(o, g),), [(o, dv, 0), (p, dv, gate_blk)], [],
                    [(RET_HEADS * dv, dv, MXU_DTYPE)], tm=min(o.shape[0], 2048), ncol=RET_HEADS)[0]


def head_gate_bwd(name, o, p, gate_blk, da):
    dv = RET_V_DIM

    def fn(o, g, da):
        _, vjp = jax.vjp(_head_gate, o, g.astype(F32))
        return vjp(da.astype(F32))

    w = RET_HEADS * dv
    return _rowwise(name, fn, [(o, dv, 0), (p, dv, gate_blk), (da, dv, 0)], [],
                    [(w, dv, MXU_DTYPE), (w, dv, MXU_DTYPE)], tm=min(o.shape[0], 2048), ncol=RET_HEADS)


def _ln_silu(u, g, b):
    mu = jnp.mean(u, axis=-1, keepdims=True)
    var = jnp.mean(jnp.square(u - mu), axis=-1, keepdims=True)
    return _silu((u - mu) * lax.rsqrt(var + LN_EPS) * g + b)


def ln_silu_fwd(name, u, g, b):
    d = u.shape[1]
    return _rowwise(name, lambda u, g, b: (_ln_silu(u, g, b),), [(u, d, 0)], [(g, None), (b, None)],
                    [(d, d, MXU_DTYPE)], tm=512)[0]


def ln_silu_bwd(name, u, g, b, dc):
    d = u.shape[1]

    def fn(u, dc, g, b):
        _, vjp = jax.vjp(_ln_silu, u, g, b)
        return vjp(dc.astype(F32))

    return _rowwise(name, fn, [(u, d, 0), (dc, d, 0)], [(g, None), (b, None)], [(d, d, F32)], [(d, d), (d, d)],
                    tm=256)


def _merge(g0, g1, g2, ya, yb, yc):
    s = jax.nn.sigmoid
    return s(g0.astype(F32)) * ya + s(g1.astype(F32)) * yb + s(g2.astype(F32)) * yc


def merge_fwd(name, p, blk, ya, yb, yc):
    d = ya.shape[1]
    rows = [(p, d, blk), (p, d, blk + 1), (p, d, blk + 2), (ya, d, 0), (yb, d, 0), (yc, d, 0)]
    return _rowwise(name, lambda *v: (_merge(*v),), rows, [], [(d, d, MXU_DTYPE)], tm=256)[0]


def merge_bwd(name, p, blk, ya, yb, yc, dmg):
    d = ya.shape[1]

    def fn(g0, g1, g2, ya, yb, yc, dmg):
        _, vjp = jax.vjp(_merge, g0.astype(F32), g1.astype(F32), g2.astype(F32), ya, yb, yc)
        return vjp(dmg.astype(F32))

    rows = [(p, d, blk), (p, d, blk + 1), (p, d, blk + 2), (ya, d, 0), (yb, d, 0), (yc, d, 0), (dmg, d, 0)]
    return _rowwise(name, fn, rows, [], [(d, d, MXU_DTYPE)] * 6, tm=256)


def concat_cols(name, pieces):
    t = pieces[0].shape[0]
    widths = [p.shape[1] for p in pieces]
    tm = 256

    def body(*refs):
        o_ref, at = refs[-1], 0
        for r, w in zip(refs[:-1], widths):
            o_ref[:, at:at + w] = r[...]
            at += w

    return pl.pallas_call(
        body, name=name, grid=(t // tm,),
        in_specs=[pl.BlockSpec((tm, w), lambda i: (i, 0)) for w in widths],
        out_specs=pl.BlockSpec((tm, sum(widths)), lambda i: (i, 0)),
        out_shape=jax.ShapeDtypeStruct((t, sum(widths)), pieces[0].dtype),
        compiler_params=_params(("parallel",)),
    )(*pieces)


def loss_head(name, y, target):
    t, d = y.shape
    tm = 512

    def body(y_ref, t_ref, dy_ref, loss_ref):
        err = y_ref[...] - t_ref[...]
        dy_ref[...] = err * (1.0 / d)
        part = jnp.sum(jnp.sum(err * err, axis=1, keepdims=True), axis=0, keepdims=True) * (0.5 / d)

        @pl.when(pl.program_id(0) == 0)
        def _():
            loss_ref[...] = part

        @pl.when(pl.program_id(0) > 0)
        def _():
            loss_ref[...] += part

    return pl.pallas_call(
        body, name=name, grid=(t // tm,),
        in_specs=[pl.BlockSpec((tm, d), lambda i: (i, 0))] * 2,
        out_specs=[pl.BlockSpec((tm, d), lambda i: (i, 0)), pl.BlockSpec((1, 1), lambda i: (0, 0))],
        out_shape=[jax.ShapeDtypeStruct((t, d), F32), jax.ShapeDtypeStruct((1, 1), F32)],
        compiler_params=_params(("arbitrary",)),
    )(y, target)


def _rot(x, cos2, sin2):
    return x * cos2 + pltpu.roll(x, RET_QK_DIM // 2, 1) * sin2


def _decay_mask(lg, n0, rows, cols):
    n = n0 + lax.broadcasted_iota(jnp.int32, (rows, cols), 0)
    m = lax.broadcasted_iota(jnp.int32, (rows, cols), 1)
    shift = CHUNK.bit_length() - 1
    dist = jnp.abs(n - m).astype(F32)
    return jnp.where((m >> shift) <= (n >> shift), jnp.exp(lg * dist), 0.0)


def _ret_specs(s):
    dk, dv, h = RET_QK_DIM, RET_V_DIM, RET_HEADS
    return [
        pl.BlockSpec((s, dk), lambda b, hh: (b, hh)),
        pl.BlockSpec((s, dk), lambda b, hh: (b, h + hh)),
        pl.BlockSpec((s, dv), lambda b, hh: (b, (2 * h * dk) // dv + hh)),
        pl.BlockSpec((s, dk), lambda b, hh: (b, 0)),
        pl.BlockSpec((s, dk), lambda b, hh: (b, 0)),
        pl.BlockSpec((None, 1, dk), lambda b, hh: (hh, 0, 0)),
    ]


def retention_fwd(name, p, cos2, sin2, log_g, nb, s):
    dk, dv, h = RET_QK_DIM, RET_V_DIM, RET_HEADS

    def body(q_ref, k_ref, v_ref, cos_ref, sin_ref, lg_ref, o_ref, kr_ref):
        lg = lg_ref[0:1, 0:1]
        kr = _rot(k_ref[...].astype(F32), cos_ref[...], sin_ref[...]) * (dk ** -0.5)
        kr_ref[...] = kr.astype(kr_ref.dtype)
        for qi in range(s // RET_TQ):
            n0, kmax = qi * RET_TQ, (qi + 1) * RET_TQ
            rows = pl.ds(n0, RET_TQ)
            qr = _rot(q_ref[rows, :].astype(F32), cos_ref[rows, :], sin_ref[rows, :]).astype(MXU_DTYPE)
            sc = lax.dot_general(qr, kr_ref[0:kmax, :], NT, preferred_element_type=F32)
            pm = (sc * _decay_mask(lg, n0, RET_TQ, kmax)).astype(MXU_DTYPE)
            o_ref[rows, :] = lax.dot_general(pm, v_ref[0:kmax, :], NN, preferred_element_type=F32)

    return pl.pallas_call(
        body, name=name, grid=(nb, h), in_specs=_ret_specs(s),
        out_specs=pl.BlockSpec((s, dv), lambda b, hh: (b, hh)),
        out_shape=jax.ShapeDtypeStruct((nb * s, h * dv), F32),
        scratch_shapes=[pltpu.VMEM((s, dk), MXU_DTYPE)],
        compiler_params=_params(("parallel", "parallel")),
    )(p, p, p, cos2, sin2, log_g)


def retention_bwd(name, p, cos2, sin2, log_g, do, nb, s):
    dk, dv, h = RET_QK_DIM, RET_V_DIM, RET_HEADS

    def body(q_ref, k_ref, v_ref, cos_ref, sin_ref, lg_ref, do_ref, dq_ref, dk_ref, dv_ref, kr_ref, dk_acc, dv_acc):
        lg = lg_ref[0:1, 0:1]
        kr = _rot(k_ref[...].astype(F32), cos_ref[...], sin_ref[...]) * (dk ** -0.5)
        kr_ref[...] = kr.astype(kr_ref.dtype)
        dk_acc[...] = jnp.zeros_like(dk_acc)
        dv_acc[...] = jnp.zeros_like(dv_acc)
        for qi in range(s // RET_TQ):
            n0, kmax = qi * RET_TQ, (qi + 1) * RET_TQ
            rows = pl.ds(n0, RET_TQ)
            cq, sq = cos_ref[rows, :], sin_ref[rows, :]
            qr = _rot(q_ref[rows, :].astype(F32), cq, sq).astype(MXU_DTYPE)
            dob = do_ref[rows, :]
            mask = _decay_mask(lg, n0, RET_TQ, kmax)
            sc = lax.dot_general(qr, kr_ref[0:kmax, :], NT, preferred_element_type=F32)
            pm = (sc * mask).astype(MXU_DTYPE)
            dv_acc[0:kmax, :] += lax.dot_general(pm, dob, TN, preferred_element_type=F32)
            dp = lax.dot_general(dob, v_ref[0:kmax, :], NT, preferred_element_type=F32)
            ds = (dp * mask).astype(MXU_DTYPE)
            dqr = lax.dot_general(ds, kr_ref[0:kmax, :], NN, preferred_element_type=F32)
            dq_ref[rows, :] = _rot(dqr, cq, -sq).astype(dq_ref.dtype)
            dk_acc[0:kmax, :] += lax.dot_general(ds, qr, TN, preferred_element_type=F32)
        dkr = dk_acc[...] * (dk ** -0.5)
        dk_ref[...] = _rot(dkr, cos_ref[...], -sin_ref[...]).astype(dk_ref.dtype)
        dv_ref[...] = dv_acc[...].astype(dv_ref.dtype)

    t = nb * s
    return pl.pallas_call(
        body, name=name, grid=(nb, h),
        in_specs=_ret_specs(s) + [pl.BlockSpec((s, dv), lambda b, hh: (b, hh))],
        out_specs=[pl.BlockSpec((s, dk), lambda b, hh: (b, hh)), pl.BlockSpec((s, dk), lambda b, hh: (b, hh)),
                   pl.BlockSpec((s, dv), lambda b, hh: (b, hh))],
        out_shape=[jax.ShapeDtypeStruct((t, h * dk), MXU_DTYPE), jax.ShapeDtypeStruct((t, h * dk), MXU_DTYPE),
                   jax.ShapeDtypeStruct((t, h * dv), MXU_DTYPE)],
        scratch_shapes=[pltpu.VMEM((s, dk), MXU_DTYPE), pltpu.VMEM((s, dk), F32), pltpu.VMEM((s, dv), F32)],
        compiler_params=_params(("parallel", "parallel")),
    )(p, p, p, cos2, sin2, log_g, do)


def _conv_grid(t, d, nb, ts):
    s = t // nb
    ns, nc = s // ts, d // CONV_TC
    return s, ns, nc


def _shifted(pad_ref, sh_ref, offsets):
    n = sh_ref.shape[1]
    for b in sorted({off % SUBLANES for off in offsets} - {0}):
        sh_ref[b - 1] = pad_ref[pl.ds(b, n), :]

    def read(off, r0):
        a, b = off - off % SUBLANES + r0, off % SUBLANES
        return pad_ref[pl.ds(a, SUBLANES), :] if b == 0 else sh_ref[b - 1, pl.ds(a, SUBLANES), :]

    return read


def _causal_taps(pad_ref, sh_ref, w_ref, k, emit):
    offs = [CONV_PAD - (k - 1) + j for j in range(k)]
    read = _shifted(pad_ref, sh_ref, offs)
    for r0 in range(0, pad_ref.shape[0] - CONV_PAD, CONV_ROWS):
        accs = [None] * len(CONV_TILES)
        for j in range(k):
            wj = w_ref[j]
            for q, dr in enumerate(CONV_TILES):
                term = wj * read(offs[j], r0 + dr)
                accs[q] = term if accs[q] is None else accs[q] + term
        emit(r0, jnp.concatenate(accs, axis=0))


def _tap_tiles(w):
    return jnp.broadcast_to(w[:, None, :], (w.shape[0], SUBLANES, w.shape[1]))


def _tap_spec(k):
    return pl.BlockSpec((k, SUBLANES, CONV_TC), lambda c, b, si: (0, 0, c))


def _carry_past(pad_ref, s_idx):
    ts = pad_ref.shape[0] - CONV_PAD

    @pl.when(s_idx == 0)
    def _():
        pad_ref[0:CONV_PAD, :] = jnp.zeros((CONV_PAD, pad_ref.shape[1]), F32)

    @pl.when(s_idx > 0)
    def _():
        pad_ref[0:CONV_PAD, :] = pad_ref[ts:ts + CONV_PAD, :]


def _carry_future(pad_ref, s_idx):
    ts = pad_ref.shape[0] - CONV_PAD

    @pl.when(s_idx == 0)
    def _():
        pad_ref[ts:ts + CONV_PAD, :] = jnp.zeros((CONV_PAD, pad_ref.shape[1]), F32)

    @pl.when(s_idx > 0)
    def _():
        pad_ref[ts:ts + CONV_PAD, :] = pad_ref[0:CONV_PAD, :]


def _conv_bwd_taps(pad_ref, sh_ref, w_ref, dw_acc, k, x_rows, emit, mix):
    read = _shifted(pad_ref, sh_ref, range(k))
    for r0 in range(0, pad_ref.shape[0] - CONV_PAD, CONV_ROWS):
        ops = x_rows(r0)
        x = mix(ops)
        accs = [None] * len(CONV_TILES)
        for j in range(k):
            wj, dwj = w_ref[j], None
            for q, dr in enumerate(CONV_TILES):
                sh = read(k - 1 - j, r0 + dr)
                term = wj * sh
                accs[q] = term if accs[q] is None else accs[q] + term
                prod = x[dr:dr + SUBLANES] * sh
                dwj = prod if dwj is None else dwj + prod
            dw_acc[j] += dwj
        emit(r0, ops, jnp.concatenate(accs, axis=0))


def _conv_bwd_edges(dw_acc, dw_ref, nb, ns, extra=()):
    first = jnp.logical_and(pl.program_id(1) == 0, pl.program_id(2) == 0)
    last = jnp.logical_and(pl.program_id(1) == nb - 1, pl.program_id(2) == ns - 1)

    @pl.when(first)
    def _():
        dw_acc[...] = jnp.zeros_like(dw_acc)
        for r in extra:
            r[...] = jnp.zeros_like(r)

    def finish():
        @pl.when(last)
        def _():
            dw_ref[...] = jnp.sum(dw_acc[...], axis=1)

    return finish


def short_conv_fwd(name, p, blk_b, w, nb):
    t = p.shape[0]
    d = w.shape[1]
    ts = SC_TS
    s, ns, nc = _conv_grid(t, d, nb, ts)
    cb = d // CONV_TC

    def body(b_ref, c_ref, x_ref, w_ref, y_ref, cz_ref, pad_ref, sh_ref):
        _carry_past(pad_ref, pl.program_id(2))
        pad_ref[CONV_PAD:CONV_PAD + ts, :] = c_ref[...].astype(F32) * x_ref[...].astype(F32)

        def emit(r0, cz):
            rows = pl.ds(r0, CONV_ROWS)
            cz_ref[rows, :] = cz
            y_ref[rows, :] = (b_ref[rows, :].astype(F32) * cz).astype(y_ref.dtype)

        _causal_taps(pad_ref, sh_ref, w_ref, SC_KERNEL, emit)

    def pspec(off):
        return pl.BlockSpec((ts, CONV_TC), lambda c, b, si: (b * ns + si, (blk_b + off) * cb + c))

    ospec = pl.BlockSpec((ts, CONV_TC), lambda c, b, si: (b * ns + si, c))
    return pl.pallas_call(
        body, name=name, grid=(nc, nb, ns),
        in_specs=[pspec(0), pspec(1), pspec(2), _tap_spec(SC_KERNEL)],
        out_specs=[ospec, ospec],
        out_shape=[jax.ShapeDtypeStruct((t, d), MXU_DTYPE), jax.ShapeDtypeStruct((t, d), F32)],
        scratch_shapes=_conv_scratch(ts),
        compiler_params=_params(("parallel", "arbitrary", "arbitrary")),
    )(p, p, p, _tap_tiles(w))


def short_conv_bwd(name, p, blk_b, w, cz, dy, nb):
    t = p.shape[0]
    d = w.shape[1]
    ts = SC_TS
    s, ns, nc = _conv_grid(t, d, nb, ts)
    cb = d // CONV_TC

    def body(b_ref, c_ref, x_ref, w_ref, cz_ref, dy_ref, db_ref, dc_ref, dx_ref, dw_ref, pad_ref, sh_ref, dw_acc):
        _carry_future(pad_ref, pl.program_id(2))
        dyv = dy_ref[...].astype(F32)
        db_ref[...] = (dyv * cz_ref[...]).astype(db_ref.dtype)
        pad_ref[0:ts, :] = dyv * b_ref[...].astype(F32)
        finish = _conv_bwd_edges(dw_acc, dw_ref, nb, ns)

        def x_rows(r0):
            rows = pl.ds(r0, CONV_ROWS)
            return c_ref[rows, :].astype(F32), x_ref[rows, :].astype(F32)

        def emit(r0, cx, dz):
            rows = pl.ds(r0, CONV_ROWS)
            dc_ref[rows, :] = (dz * cx[1]).astype(dc_ref.dtype)
            dx_ref[rows, :] = (dz * cx[0]).astype(dx_ref.dtype)

        _conv_bwd_taps(pad_ref, sh_ref, w_ref, dw_acc, SC_KERNEL, x_rows, emit, lambda cx: cx[0] * cx[1])
        finish()

    def row(b, si):
        return b * ns + (ns - 1 - si)

    def pspec(off):
        return pl.BlockSpec((ts, CONV_TC), lambda c, b, si: (row(b, si), (blk_b + off) * cb + c))

    ospec = pl.BlockSpec((ts, CONV_TC), lambda c, b, si: (row(b, si), c))
    wspec = pl.BlockSpec((SC_KERNEL, CONV_TC), lambda c, b, si: (0, c))
    return pl.pallas_call(
        body, name=name, grid=(nc, nb, ns),
        in_specs=[pspec(0), pspec(1), pspec(2), _tap_spec(SC_KERNEL), ospec, ospec],
        out_specs=[ospec, ospec, ospec, wspec],
        out_shape=[jax.ShapeDtypeStruct((t, d), MXU_DTYPE)] * 3 + [jax.ShapeDtypeStruct((SC_KERNEL, d), F32)],
        scratch_shapes=_conv_scratch(ts) + [pltpu.VMEM((SC_KERNEL, SUBLANES, CONV_TC), F32)],
        compiler_params=_params(("parallel", "arbitrary", "arbitrary")),
    )(p, p, p, _tap_tiles(w), cz, dy)


def conformer_conv_fwd(name, p, blk_a, w, bias, nb):
    t = p.shape[0]
    d = w.shape[1]
    ts = CONV_TS
    s, ns, nc = _conv_grid(t, d, nb, ts)
    cb = d // CONV_TC

    def body(a_ref, b_ref, w_ref, bias_ref, u_ref, pad_ref, sh_ref):
        _carry_past(pad_ref, pl.program_id(2))
        pad_ref[CONV_PAD:CONV_PAD + ts, :] = a_ref[...].astype(F32) * jax.nn.sigmoid(b_ref[...].astype(F32))

        def emit(r0, u):
            u_ref[pl.ds(r0, CONV_ROWS), :] = u + bias_ref[0:1, :]

        _causal_taps(pad_ref, sh_ref, w_ref, CF_KERNEL, emit)

    def pspec(off):
        return pl.BlockSpec((ts, CONV_TC), lambda c, b, si: (b * ns + si, (blk_a + off) * cb + c))

    return pl.pallas_call(
        body, name=name, grid=(nc, nb, ns),
        in_specs=[pspec(0), pspec(1), _tap_spec(CF_KERNEL), pl.BlockSpec((SUBLANES, CONV_TC), lambda c, b, si: (0, c))],
        out_specs=pl.BlockSpec((ts, CONV_TC), lambda c, b, si: (b * ns + si, c)),
        out_shape=jax.ShapeDtypeStruct((t, d), F32),
        scratch_shapes=_conv_scratch(ts),
        compiler_params=_params(("parallel", "arbitrary", "arbitrary")),
    )(p, p, _tap_tiles(w), jnp.broadcast_to(bias, (SUBLANES, d)))


def conformer_conv_bwd(name, p, blk_a, w, du, nb):
    t = p.shape[0]
    d = w.shape[1]
    ts = CONV_TS
    s, ns, nc = _conv_grid(t, d, nb, ts)
    cb = d // CONV_TC

    def body(a_ref, b_ref, w_ref, du_ref, da_ref, db_ref, dw_ref, dbias_ref, pad_ref, sh_ref, dw_acc):
        _carry_future(pad_ref, pl.program_id(2))
        duv = du_ref[...]
        pad_ref[0:ts, :] = duv
        finish = _conv_bwd_edges(dw_acc, dw_ref, nb, ns, extra=(dbias_ref,))
        dbias_ref[...] += jnp.sum(duv, axis=0, keepdims=True)

        def x_rows(r0):
            rows = pl.ds(r0, CONV_ROWS)
            return a_ref[rows, :].astype(F32), jax.nn.sigmoid(b_ref[rows, :].astype(F32))

        def emit(r0, asg, du0):
            rows = pl.ds(r0, CONV_ROWS)
            av, sg = asg
            da_ref[rows, :] = (du0 * sg).astype(da_ref.dtype)
            db_ref[rows, :] = (du0 * av * sg * (1.0 - sg)).astype(db_ref.dtype)

        _conv_bwd_taps(pad_ref, sh_ref, w_ref, dw_acc, CF_KERNEL, x_rows, emit, lambda asg: asg[0] * asg[1])
        finish()

    def row(b, si):
        return b * ns + (ns - 1 - si)

    def pspec(off):
        return pl.BlockSpec((ts, CONV_TC), lambda c, b, si: (row(b, si), (blk_a + off) * cb + c))

    ospec = pl.BlockSpec((ts, CONV_TC), lambda c, b, si: (row(b, si), c))
    wspec = pl.BlockSpec((CF_KERNEL, CONV_TC), lambda c, b, si: (0, c))
    bspec = pl.BlockSpec((1, CONV_TC), lambda c, b, si: (0, c))
    return pl.pallas_call(
        body, name=name, grid=(nc, nb, ns),
        in_specs=[pspec(0), pspec(1), _tap_spec(CF_KERNEL), ospec],
        out_specs=[ospec, ospec, wspec, bspec],
        out_shape=[jax.ShapeDtypeStruct((t, d), MXU_DTYPE)] * 2
        + [jax.ShapeDtypeStruct((CF_KERNEL, d), F32), jax.ShapeDtypeStruct((1, d), F32)],
        scratch_shapes=_conv_scratch(ts) + [pltpu.VMEM((CF_KERNEL, SUBLANES, CONV_TC), F32)],
        compiler_params=_params(("parallel", "arbitrary", "arbitrary")),
    )(p, p, _tap_tiles(w), du)


BLOCKS = ("ffn1", "mixer", "ffn2")
BLOCK_WEIGHTS = {"ffn1": ("ffn1_w_gu", "ffn1_w_down"), "mixer": ("w_in", "w_ret_o", "w_sc_o", "w_cf_o", "w_o"),
                 "ffn2": ("ffn2_w_gu", "ffn2_w_down")}
BIG = BLOCK_WEIGHTS["ffn1"] + BLOCK_WEIGHTS["mixer"] + BLOCK_WEIGHTS["ffn2"]
MODE = {"ffn1_w_gu": "col", "ffn1_w_down": "row", "w_in": "col", "w_ret_o": "row", "w_sc_o": "row",
        "w_cf_o": "row", "w_o": "row", "ffn2_w_gu": "col", "ffn2_w_down": "row"}
NORM_OF = {"ffn1": 0, "mixer": 2, "ffn2": 4}
BLK_GATE, BLK_SCB, BLK_CFA, BLK_MERGE = 2, 3, 6, 8


def _rope_tables(positions):
    half = RET_QK_DIM // 2
    inv_freq = ROPE_BASE ** (-jnp.arange(half, dtype=F32) / half)
    ang = positions.astype(F32)[..., None] * inv_freq
    cos, sin = jnp.cos(ang), jnp.sin(ang)
    nb, s = positions.shape
    cos2 = jnp.concatenate([cos, cos], axis=-1).reshape(nb * s, RET_QK_DIM)
    sin2 = jnp.concatenate([-sin, sin], axis=-1).reshape(nb * s, RET_QK_DIM)
    return cos2, sin2


def _log_gamma():
    lg = jnp.log(1.0 - 2.0 ** (-5.0 - jnp.arange(RET_HEADS, dtype=F32)))
    return jnp.broadcast_to(lg[:, None, None], (RET_HEADS, 1, RET_QK_DIM))


def _ffn_fwd(xs, h, w, tag, g_post, g_next):
    gu, a = ffn_up("ffn_up", h, w[tag + "_w_gu"])
    y, out, h_next = mm_post("ffn_down", a, w[tag + "_w_down"], xs, g_post, 0.5, g_next)
    return out, h_next, dict(x=xs, h=h, gu=gu, a=a, y=y, w=w)


def _ffn_bwd(dxs, dy, sv, tag, g_pre, push, prev):
    w = sv["w"]
    gu_w, down_w = w[tag + "_w_gu"], w[tag + "_w_down"]
    dgu = ffn_down_dx("ffn_down_dx", dy, down_w, sv["gu"])
    grads = {tag + "_w_down": mm_dw("ffn_down_dw", sv["a"], dy, "row", down_w.shape),
             tag + "_w_gu": mm_dw("ffn_gu_dw", sv["h"], dgu, "col", gu_w.shape)}
    return mm_dx_norms("ffn_gu_dx", dgu, gu_w, sv["x"], g_pre, dxs, prev, push(grads))


def _mixer_fwd(xs, h, w, sm, g_post, g_next, rope, nb, s, mid):
    cos2, sin2, log_g = rope
    d = xs.shape[1]
    gate_blk = (BLK_GATE * d) // RET_V_DIM
    p = mm_fwd("mx_in", h, w["w_in"], "col", MXU_DTYPE)
    if mid is not None:
        sm = dict(sm, cf_dw_b=sm["cf_dw_b"] + mid(p))
    o = retention_fwd("ret_fwd", p, cos2, sin2, log_g, nb, s)
    ya_in = head_gate_fwd("ret_gate", o, p, gate_blk)
    yb_in, cz = short_conv_fwd("sc_fwd", p, BLK_SCB, sm["sc_conv_w"], nb)
    u1 = conformer_conv_fwd("cf_fwd", p, BLK_CFA, sm["cf_dw_w"], sm["cf_dw_b"], nb)
    yc_in = ln_silu_fwd("cf_ln", u1, sm["cf_ln_g"], sm["cf_ln_b"])
    ya = mm_fwd("mx_proj", ya_in, w["w_ret_o"], "row", F32)
    yb = mm_fwd("mx_proj", yb_in, w["w_sc_o"], "row", F32)
    yc = mm_fwd("mx_proj", yc_in, w["w_cf_o"], "row", F32)
    mg = merge_fwd("mx_merge", p, BLK_MERGE, ya, yb, yc)
    m, out, h_next = mm_post("mx_out", mg, w["w_o"], xs, g_post, 1.0, g_next)
    return out, h_next, dict(x=xs, h=h, p=p, o=o, ya_in=ya_in, yb_in=yb_in, cz=cz, u1=u1, yc_in=yc_in, ya=ya, yb=yb, yc=yc,
                     mg=mg, m=m, w=w)


def _mixer_bwd(dxs, dm, sv, sm, g_pre, rope, nb, s, push, prev):
    cos2, sin2, log_g = rope
    w, p = sv["w"], sv["p"]
    d = dxs.shape[1]
    gate_blk = (BLK_GATE * d) // RET_V_DIM
    grads, gsm = {}, {}

    def proj_bwd(wname, a_in, dy, out_dtype):
        grads[wname] = mm_dw("mx_proj_dw", a_in, dy, "row", w[wname].shape)
        return mm_dx("mx_proj_dx", dy, w[wname], "row", out_dtype)

    dmg = proj_bwd("w_o", sv["mg"], dm, MXU_DTYPE)
    dg0, dg1, dg2, dya, dyb, dyc = merge_bwd("mx_merge_bwd", p, BLK_MERGE, sv["ya"], sv["yb"], sv["yc"], dmg)
    dya_in = proj_bwd("w_ret_o", sv["ya_in"], dya, MXU_DTYPE)
    dyb_in = proj_bwd("w_sc_o", sv["yb_in"], dyb, MXU_DTYPE)
    dyc_in = proj_bwd("w_cf_o", sv["yc_in"], dyc, MXU_DTYPE)
    do, dgret = head_gate_bwd("ret_gate_bwd", sv["o"], p, gate_blk, dya_in)
    dq, dk, dv = retention_bwd("ret_bwd", p, cos2, sin2, log_g, do, nb, s)
    dscb, dscc, dscx, gsm["sc_conv_w"] = short_conv_bwd("sc_bwd", p, BLK_SCB, sm["sc_conv_w"], sv["cz"], dyb_in, nb)
    du1, dlg, dlb = ln_silu_bwd("cf_ln_bwd", sv["u1"], sm["cf_ln_g"], sm["cf_ln_b"], dyc_in)
    dcfa, dcfb, gsm["cf_dw_w"], dbias = conformer_conv_bwd("cf_bwd", p, BLK_CFA, sm["cf_dw_w"], du1, nb)
    gsm.update(cf_ln_g=dlg[0], cf_ln_b=dlb[0], cf_dw_b=dbias[0])
    dp = concat_cols("mx_dp", [dq, dk, dv, dgret, dscb, dscc, dscx, dcfa, dcfb, dg0, dg1, dg2])
    grads["w_in"] = mm_dw("mx_in_dw", sv["h"], dp, "col", w["w_in"].shape)
    return mm_dx_norms("mx_in_dx", dp, w["w_in"], sv["x"], g_pre, dxs, prev, push(grads)) + (gsm,)


def local_step(x, positions, target, small, fetch, push):
    nb, s, d = x.shape
    t = nb * s
    depth = small["norm_g"].shape[0]
    rope = _rope_tables(positions) + (_log_gamma(),)
    xs = x.reshape(t, d)
    token = [None]

    def gain(l, i):
        g = small["norm_g"][l, i][None, :]
        if token[0] is not None:
            g, token[0] = g + token[0], None
        return g

    def mixer_small(l):
        return dict(sc_conv_w=small["sc_conv_w"][l], cf_dw_w=small["cf_dw_w"][l], cf_dw_b=small["cf_dw_b"][l][None, :],
                    cf_ln_g=small["cf_ln_g"][l][None, :], cf_ln_b=small["cf_ln_b"][l][None, :])

    saved = {}
    order = [(l, blk) for l in range(depth) for blk in BLOCKS]
    h = None
    for at, (l, blk) in enumerate(order):
        w, token[0], mid = fetch(l, blk, xs)
        i0 = NORM_OF[blk]
        if h is None:
            h = rms_fwd("first_rms", xs, gain(l, i0))
        g_post = gain(l, i0 + 1)
        g_next = gain(order[at + 1][0], NORM_OF[order[at + 1][1]]) if at + 1 < len(order) else None
        if blk == "mixer":
            xs, h, saved[l, blk] = _mixer_fwd(xs, h, w, mixer_small(l), g_post, g_next, rope, nb, s, mid)
        else:
            xs, h, saved[l, blk] = _ffn_fwd(xs, h, w, blk, g_post, g_next)

    dxs, loss = loss_head("loss", xs, target.reshape(t, d))

    dnorm = [[None] * 6 for _ in range(depth)]
    gsmall = {n: [None] * depth for n in ("sc_conv_w", "cf_dw_w", "cf_dw_b", "cf_ln_g", "cf_ln_b")}
    def branch(group):
        l, blk = group
        sv = saved[group]
        return (sv["m"], gain(l, NORM_OF[blk] + 1), 1.0) if blk == "mixer" else (sv["y"], gain(l, NORM_OF[blk] + 1), 0.5)

    l, blk = order[-1]
    y, g_post, scale = branch(order[-1])
    dy, dnorm[l][NORM_OF[blk] + 1] = post_bwd("last_post_bwd", y, g_post, dxs, scale)
    for at in reversed(range(len(order))):
        l, blk = order[at]
        i0 = NORM_OF[blk]
        prev = branch(order[at - 1]) if at > 0 else None
        put = functools.partial(push, l, blk)
        if blk == "mixer":
            dxs, dnorm[l][i0], dy, dg_prev, gsm = _mixer_bwd(
                dxs, dy, saved[l, blk], mixer_small(l), gain(l, i0), rope, nb, s, put, prev)
            for n, v in gsm.items():
                gsmall[n][l] = v
        else:
            dxs, dnorm[l][i0], dy, dg_prev = _ffn_bwd(dxs, dy, saved[l, blk], blk, gain(l, i0), put, prev)
        if at > 0:
            dnorm[order[at - 1][0]][NORM_OF[order[at - 1][1]] + 1] = dg_prev

    gs = {n: jnp.stack(v) for n, v in gsmall.items()}
    gs["norm_g"] = jnp.stack([jnp.concatenate(r, axis=0) for r in dnorm])
    return loss, dxs.reshape(nb, s, d), gs


ANY = pl.BlockSpec(memory_space=pl.ANY)
HBM = pl.BlockSpec(memory_space=pltpu.HBM)
SEM = pl.BlockSpec(memory_space=pltpu.SEMAPHORE)
VMEM_WHOLE = pl.BlockSpec(memory_space=pltpu.VMEM)
EFFECT = pltpu.SideEffectType.DATAFLOW_SIDE_EFFECTING
TOKEN = jax.ShapeDtypeStruct((8, 128), F32)


def _other_chips(x, y):
    return [(1 - x, y), (x, 1 - y), (1 - x, 1 - y)]


def _remote(src, dst, send_sem, recv_sem, to):
    return pltpu.make_async_remote_copy(src_ref=src, dst_ref=dst, send_sem=send_sem, recv_sem=recv_sem,
                                        device_id=to, device_id_type=MESH)


def _in_hbm(v):
    return pltpu.with_memory_space_constraint(v, pltpu.HBM)


def place_quarters(ws, layer, ids, after):
    m = len(ws)

    def body(ids_ref, *refs):
        for w_ref, o_ref in zip(refs[:m], refs[m + 1:]):
            o_ref[...] = w_ref[...].astype(o_ref.dtype)

    def spec(w, where):
        return pl.BlockSpec((None, w.shape[1] // STREAM_STEPS, w.shape[2]), where)

    return pl.pallas_call(
        body, name="place_quarters",
        grid_spec=pltpu.PrefetchScalarGridSpec(
            num_scalar_prefetch=1, grid=(STREAM_STEPS,),
            in_specs=[spec(w, lambda i, ids_ref: (layer, i, 0)) for w in ws] + [ANY],
            out_specs=[spec(w, lambda i, ids_ref: (ids_ref[0], i, 0)) for w in ws]),
        out_shape=[jax.ShapeDtypeStruct((N_CHIP,) + w.shape[1:], MXU_DTYPE) for w in ws],
        compiler_params=_params(("parallel",)),
    )(ids, *ws, after)


def _gather_copies(lands, send, recv):
    x, y, c = _axes()
    me = 2 * x + y
    mine, theirs = [], []
    for a, ld in enumerate(lands):
        rh = ld.shape[1] // 2
        rows = pl.ds(c * rh, rh)
        for k, (px, py) in enumerate(_other_chips(x, y)):
            to = (px, py, c)
            mine.append(_remote(ld.at[me, rows, :], ld.at[me, rows, :], send.at[3 * a + k], recv.at[3 * a + k], to))
            got = ld.at[2 * px + py, rows, :]
            theirs.append(_remote(got, got, send.at[3 * a + k], recv.at[3 * a + k], to))
    return mine, theirs


def gather_start(name, groups, after):
    flat = [s for g in groups for s in g]
    n, ng = len(flat), len(groups)
    sizes = [len(g) for g in groups]

    def body(*refs):
        lands = refs[:n]
        sems = refs[n + 1:n + 1 + 2 * ng]
        token = refs[-1]
        at = 0
        for g, m in enumerate(sizes):
            mine, _ = _gather_copies(lands[at:at + m], sems[2 * g], sems[2 * g + 1])
            for cp in mine:
                cp.start()
            at += m
        token[...] = jnp.zeros_like(token)

    sem_shapes = []
    for m in sizes:
        sem_shapes += [pltpu.SemaphoreType.DMA((3 * m,))] * 2
    res = pl.pallas_call(
        body, name=name, in_specs=[HBM] * n + [ANY],
        out_specs=[SEM] * (2 * ng) + [HBM] * n + [VMEM_WHOLE],
        out_shape=sem_shapes + [pltpu.HBM(s.shape, s.dtype) for s in flat] + [TOKEN],
        input_output_aliases={i: 2 * ng + i for i in range(n)},
        compiler_params=pltpu.CompilerParams(has_side_effects=EFFECT),
    )(*[_in_hbm(s) for s in flat], after)
    sems, thru, token = res[:2 * ng], res[2 * ng:2 * ng + n], res[-1]
    out, at = [], 0
    for g, m in enumerate(sizes):
        out.append((sems[2 * g], sems[2 * g + 1], thru[at:at + m]))
        at += m
    return out, token


def gather_wait(lands, send, recv, after):
    m = len(lands)

    def body(*refs):
        mine, theirs = _gather_copies(refs[:m], refs[m], refs[m + 1])
        for cp in mine:
            cp.wait_send()
        for cp in theirs:
            cp.wait_recv()

    return pl.pallas_call(
        body, name="gather_wait", in_specs=[HBM] * m + [SEM, SEM, ANY], out_specs=[HBM] * m,
        out_shape=[pltpu.HBM(l.shape, l.dtype) for l in lands],
        input_output_aliases={i: i for i in range(m)},
        compiler_params=pltpu.CompilerParams(has_side_effects=EFFECT),
    )(*lands, send, recv, after)


def copy_start(name, families, after=()):
    sizes = [len(f[0]) for f in families]
    n, k, nf = sum(sizes), len(after), len(families)

    def body(*refs):
        at = 0
        for f, (_, copies, _) in enumerate(families):
            for cp in copies(refs[at:at + sizes[f]], refs[n + k + 2 * f], refs[n + k + 2 * f + 1])[0]:
                cp.start()
            at += sizes[f]
        refs[-1][...] = jnp.zeros_like(refs[-1])

    flat = [b for f in families for b in f[0]]
    sems = [pltpu.SemaphoreType.DMA((f[2],)) for f in families for _ in range(2)]
    res = pl.pallas_call(
        body, name=name, in_specs=[HBM] * n + [ANY] * k, out_specs=[SEM] * (2 * nf) + [HBM] * n + [VMEM_WHOLE],
        out_shape=sems + [pltpu.HBM(b.shape, b.dtype) for b in flat] + [TOKEN],
        input_output_aliases={i: 2 * nf + i for i in range(n)},
        compiler_params=pltpu.CompilerParams(has_side_effects=EFFECT),
    )(*[_in_hbm(b) for b in flat], *after)
    out, at = [], 2 * nf
    for f in range(nf):
        out.append((res[2 * f], res[2 * f + 1], list(res[at:at + sizes[f]])))
        at += sizes[f]
    return out, res[-1]


def copy_wait(name, bufs, send, recv, copies, after=()):
    n = len(bufs)

    def body(*refs):
        mine, theirs = copies(refs[:n], refs[n], refs[n + 1])
        for cp in mine:
            cp.wait_send()
        for cp in theirs:
            cp.wait_recv()

    return list(pl.pallas_call(
        body, name=name, in_specs=[HBM] * n + [SEM, SEM] + [ANY] * len(after), out_specs=[HBM] * n,
        out_shape=[pltpu.HBM(b.shape, b.dtype) for b in bufs], input_output_aliases={i: i for i in range(n)},
        compiler_params=pltpu.CompilerParams(has_side_effects=EFFECT),
    )(*bufs, send, recv, *after))


def _fill_copies(lands, send, recv):
    x, y, c = _axes()
    sib = (x, y, 1 - c)
    mine, theirs = [], []
    for a, ld in enumerate(lands):
        rh = ld.shape[1] // 2
        for k, (px, py) in enumerate(_other_chips(x, y)):
            got = ld.at[2 * px + py, pl.ds(c * rh, rh), :]
            mine.append(_remote(got, got, send.at[3 * a + k], recv.at[3 * a + k], sib))
            blk = ld.at[2 * px + py, pl.ds((1 - c) * rh, rh), :]
            theirs.append(_remote(blk, blk, send.at[3 * a + k], recv.at[3 * a + k], sib))
    return mine, theirs


def _presum_copies(grads, lands, send, recv):
    x, y, c = _axes()
    cps = []
    for a, (g, ld) in enumerate(zip(grads, lands)):
        rh = g.shape[1] // 2
        cps.append(_remote(g.at[:, pl.ds((1 - c) * rh, rh), :], ld, send.at[a], recv.at[a], (x, y, 1 - c)))
    return cps


def presum_wait(grads, lands, send, recv, after):
    m = len(grads)

    def body(*refs):
        for cp in _presum_copies(refs[:m], refs[m:2 * m], refs[2 * m], refs[2 * m + 1]):
            cp.wait_send()
            cp.wait_recv()

    res = pl.pallas_call(
        body, name="presum_wait", in_specs=[HBM] * (2 * m) + [SEM, SEM] + [ANY] * len(after),
        out_specs=[HBM] * (2 * m),
        out_shape=[pltpu.HBM(g.shape, g.dtype) for g in grads] + [pltpu.HBM(l.shape, l.dtype) for l in lands],
        input_output_aliases={i: i for i in range(2 * m)},
        compiler_params=pltpu.CompilerParams(has_side_effects=EFFECT),
    )(*grads, *lands, send, recv, *after)
    return res[:m], res[m:]


def add_halves(gs, lands, ids):
    m = len(gs)

    def body(ids_ref, *refs):
        for a_ref, b_ref, o_ref in zip(refs[:m], refs[m:2 * m], refs[2 * m:]):
            o_ref[...] = (a_ref[...].astype(F32) + b_ref[...].astype(F32)).astype(o_ref.dtype)

    def spec(ld, where):
        return pl.BlockSpec((None,) + ld.shape[1:], where)

    return pl.pallas_call(
        body, name="add_halves",
        grid_spec=pltpu.PrefetchScalarGridSpec(
            num_scalar_prefetch=1, grid=(N_CHIP,),
            in_specs=[spec(ld, lambda i, ids_ref: (i, ids_ref[1], 0)) for ld in lands]
            + [spec(ld, lambda i, ids_ref: (i, 0, 0)) for ld in lands],
            out_specs=[spec(ld, lambda i, ids_ref: (i, 0, 0)) for ld in lands]),
        out_shape=[jax.ShapeDtypeStruct(ld.shape, ld.dtype) for ld in lands],
        compiler_params=_params(("parallel",)),
    )(ids, *gs, *lands)


def _scatter_copies(parts, lands, send, recv):
    x, y, c = _axes()
    cps = []
    for a, (pt, ld) in enumerate(zip(parts, lands)):
        for k, (px, py) in enumerate(_other_chips(x, y)):
            cps.append(_remote(pt.at[2 * px + py], ld.at[k], send.at[3 * a + k], recv.at[3 * a + k], (px, py, c)))
    return cps


def scatter_wait(parts, lands, send, recv, after):
    m = len(parts)

    def body(*refs):
        for cp in _scatter_copies(refs[:m], refs[m:2 * m], refs[2 * m], refs[2 * m + 1]):
            cp.wait_send()
            cp.wait_recv()

    res = pl.pallas_call(
        body, name="scatter_wait", in_specs=[HBM] * (2 * m) + [SEM, SEM] + [ANY] * len(after),
        out_specs=[HBM] * (2 * m),
        out_shape=[pltpu.HBM(p.shape, p.dtype) for p in parts] + [pltpu.HBM(l.shape, l.dtype) for l in lands],
        input_output_aliases={i: i for i in range(2 * m)},
        compiler_params=pltpu.CompilerParams(has_side_effects=EFFECT),
    )(*parts, *lands, send, recv, *after)
    return res[:m], res[m:]


def sum_partials(parts, lands, ids, layer, depth, intos):
    m = len(parts)
    nt = STREAM_STEPS

    def body(ids_ref, *refs):
        for p_ref, l_ref, o_ref in zip(refs[:m], refs[m:2 * m], refs[-m:]):
            acc = p_ref[...].astype(F32)
            for k in range(N_CHIP - 1):
                acc = acc + l_ref[k].astype(F32)
            o_ref[...] = acc

    def rows(p):
        return p.shape[1] // nt

    in_specs = [pl.BlockSpec((None, rows(p), p.shape[2]), lambda i, ids_ref: (ids_ref[0], i, 0)) for p in parts]
    in_specs += [pl.BlockSpec((N_CHIP - 1, rows(p), p.shape[2]), lambda i, ids_ref: (0, i, 0)) for p in parts]
    args = [ids, *parts, *lands]
    aliases = {}
    if intos is not None:
        in_specs += [ANY] * m
        args += list(intos)
        aliases = {1 + 2 * m + a: a for a in range(m)}
    return pl.pallas_call(
        body, name="sum_partials",
        grid_spec=pltpu.PrefetchScalarGridSpec(
            num_scalar_prefetch=1, grid=(nt,), in_specs=in_specs,
            out_specs=[pl.BlockSpec((None, rows(p), p.shape[2]), lambda i, ids_ref: (layer, ids_ref[1] * nt + i, 0))
                       for p in parts]),
        out_shape=[jax.ShapeDtypeStruct((depth, 2 * p.shape[1], p.shape[2]), F32) for p in parts],
        input_output_aliases=aliases, compiler_params=_params(("parallel",)),
    )(*args)


def _final_copies(layer):
    def copies(bufs, send, recv):
        x, y, c = _axes()
        sib = (x, y, 1 - c)
        mine, theirs = [], []
        for a, buf in enumerate(bufs):
            rh = buf.shape[1] // 2
            src = buf.at[layer, pl.ds(c * rh, rh), :]
            mine.append(_remote(src, src, send.at[a], recv.at[a], sib))
            dst = buf.at[layer, pl.ds((1 - c) * rh, rh), :]
            theirs.append(_remote(dst, dst, send.at[a], recv.at[a], sib))
        return mine, theirs

    return copies


def allgather_small(pk):
    def body(in_ref, out_ref, send, recv):
        x, y, c = _axes()
        me = 2 * x + y
        chips = _other_chips(x, y)
        out_ref[pl.ds(me, 1)] = in_ref[...][None]
        cps = []
        for k, (px, py) in enumerate(chips):
            cp = _remote(in_ref, out_ref.at[me], send.at[k], recv.at[k], (px, py, c))
            cp.start()
            cps.append(cp)
        for k, (px, py) in enumerate(chips):
            got = out_ref.at[2 * px + py]
            _remote(got, got, send.at[k], recv.at[k], (px, py, c)).wait_recv()
        for cp in cps:
            cp.wait_send()

    return pl.pallas_call(
        body, name="allgather_small", in_specs=[VMEM_WHOLE], out_specs=VMEM_WHOLE,
        out_shape=jax.ShapeDtypeStruct((N_CHIP,) + pk.shape, pk.dtype),
        scratch_shapes=[pltpu.SemaphoreType.DMA((3,))] * 2,
    )(pk)


N_DEV = 8


def _small_copies(bufs, send, recv):
    g, slots = bufs
    x, y, c = _axes()
    me = 4 * x + 2 * y + c
    mine, theirs = [], []
    for mask in range(1, N_DEV):
        px = 1 - x if mask & 4 else x
        py = 1 - y if mask & 2 else y
        pc = 1 - c if mask & 1 else c
        mine.append(_remote(g, slots.at[me], send.at[mask - 1], recv.at[mask - 1], (px, py, pc)))
        got = slots.at[4 * px + 2 * py + pc]
        theirs.append(_remote(got, got, send.at[mask - 1], recv.at[mask - 1], (px, py, pc)))
    return mine, theirs


def sum_slots(g, slots, me):
    def body(me_ref, g_ref, slots_ref, o_ref):
        acc = None
        for d in range(N_DEV):
            term = jnp.where(me_ref[0] == d, g_ref[...], slots_ref[d])
            acc = term if acc is None else acc + term
        o_ref[...] = acc

    return pl.pallas_call(
        body, name="sum_slots",
        grid_spec=pltpu.PrefetchScalarGridSpec(
            num_scalar_prefetch=1, grid=(1,),
            in_specs=[pl.BlockSpec(g.shape, lambda i, me_ref: (0, 0)),
                      pl.BlockSpec(slots.shape, lambda i, me_ref: (0, 0, 0))],
            out_specs=pl.BlockSpec(g.shape, lambda i, me_ref: (0, 0))),
        out_shape=jax.ShapeDtypeStruct(g.shape, g.dtype),
        compiler_params=_params(("arbitrary",)),
    )(me, g, slots)


def adamw(w, g, m, v, layer=None, intos=None):
    shape = w.shape
    cols = shape[-1]
    rows = int(np.prod(shape[:-1]))
    span = rows if layer is None else rows // shape[0]
    tr = span
    for cand in (256, 128):
        if span % cand == 0 and cand * cols * 4 <= 2 * 1024 * 1024:
            tr = cand
            break
    first = 0 if layer is None else layer * (span // tr)
    c1 = 1.0 - ADAM_B1 ** ADAM_STEP
    c2 = 1.0 - ADAM_B2 ** ADAM_STEP

    def body(w_ref, g_ref, m_ref, v_ref, *rest):
        d_ref, nm_ref, nv_ref, g_out = rest[-4:]
        gv = g_ref[...]
        g_out[...] = gv
        nm = ADAM_B1 * m_ref[...] + (1.0 - ADAM_B1) * gv
        nv = ADAM_B2 * v_ref[...] + (1.0 - ADAM_B2) * jnp.square(gv)
        d_ref[...] = -ADAM_LR * ((nm / c1) / (jnp.sqrt(nv / c2) + ADAM_EPS) + ADAM_WD * w_ref[...])
        nm_ref[...] = nm
        nv_ref[...] = nv

    spec = pl.BlockSpec((tr, cols), lambda i: (first + i, 0))
    args = [a.reshape(rows, cols) for a in (w, g, m, v)]
    in_specs, aliases = [spec] * 4, {}
    if intos is not None:
        args += [a.reshape(rows, cols) for a in intos]
        in_specs += [ANY] * 4
        aliases = {4 + k: k for k in range(4)}
    res = pl.pallas_call(
        body, name="adamw", grid=(span // tr,), in_specs=in_specs, out_specs=[spec] * 4,
        out_shape=[jax.ShapeDtypeStruct((rows, cols), F32)] * 4, input_output_aliases=aliases,
        compiler_params=_params(("parallel",)),
    )(*args)
    return [r.reshape(shape) for r in res]


WEIGHTS = ("norm_g", "ffn1_w_gu", "ffn1_w_down", "w_in", "w_ret_o", "sc_conv_w", "w_sc_o", "cf_dw_w", "cf_dw_b",
           "cf_ln_g", "cf_ln_b", "w_cf_o", "w_o", "ffn2_w_gu", "ffn2_w_down")
SHARDED_SMALL = ("norm_g", "sc_conv_w", "cf_dw_w")
REPLICATED_SMALL = ("cf_dw_b", "cf_ln_g", "cf_ln_b")

def _pack_rows(parts):
    padded, offs, at = [], [], 0
    for p in parts:
        r = -(-p.shape[0] // SUBLANES) * SUBLANES
        padded.append(jnp.pad(p, ((0, r - p.shape[0]), (0, 0))))
        offs.append(at)
        at += r
    return jnp.concatenate(padded, axis=0), offs


def kernel(x, positions, norm_g, ffn1_w_gu, ffn1_w_down, w_in, w_ret_o, sc_conv_w, w_sc_o, cf_dw_w, cf_dw_b, cf_ln_g, cf_ln_b, w_cf_o, w_o, ffn2_w_gu, ffn2_w_down, loss_target, m_norm_g, m_ffn1_w_gu, m_ffn1_w_down, m_w_in, m_w_ret_o, m_sc_conv_w, m_w_sc_o, m_cf_dw_w, m_cf_dw_b, m_cf_ln_g, m_cf_ln_b, m_w_cf_o, m_w_o, m_ffn2_w_gu, m_ffn2_w_down, v_norm_g, v_ffn1_w_gu, v_ffn1_w_down, v_w_in, v_w_ret_o, v_sc_conv_w, v_w_sc_o, v_cf_dw_w, v_cf_dw_b, v_cf_ln_g, v_cf_ln_b, v_w_cf_o, v_w_o, v_ffn2_w_gu, v_ffn2_w_down):
    wts = dict(zip(WEIGHTS, (norm_g, ffn1_w_gu, ffn1_w_down, w_in, w_ret_o, sc_conv_w, w_sc_o, cf_dw_w, cf_dw_b,
                             cf_ln_g, cf_ln_b, w_cf_o, w_o, ffn2_w_gu, ffn2_w_down)))
    mom = dict(zip(WEIGHTS, (m_norm_g, m_ffn1_w_gu, m_ffn1_w_down, m_w_in, m_w_ret_o, m_sc_conv_w, m_w_sc_o,
                             m_cf_dw_w, m_cf_dw_b, m_cf_ln_g, m_cf_ln_b, m_w_cf_o, m_w_o, m_ffn2_w_gu, m_ffn2_w_down)))
    var = dict(zip(WEIGHTS, (v_norm_g, v_ffn1_w_gu, v_ffn1_w_down, v_w_in, v_w_ret_o, v_sc_conv_w, v_w_sc_o,
                             v_cf_dw_w, v_cf_dw_b, v_cf_ln_g, v_cf_ln_b, v_w_cf_o, v_w_o, v_ffn2_w_gu, v_ffn2_w_down)))
    depth = norm_g.shape[0]
    dq = norm_g.shape[-1]
    d = N_CHIP * dq
    chip = 2 * lax.axis_index("x") + lax.axis_index("y")
    ids = jnp.stack([chip, lax.axis_index("c")]).astype(jnp.int32)

    pk, offs = _pack_rows([wts[n].reshape(-1, dq) for n in SHARDED_SMALL])
    gk4 = allgather_small(pk)
    gk = gk4.transpose(1, 0, 2).reshape(pk.shape[0], d)
    small = {n: wts[n] for n in REPLICATED_SMALL}
    for n, o in zip(SHARDED_SMALL, offs):
        rows = wts[n].shape[0] * wts[n].shape[1]
        small[n] = gk[o:o + rows].reshape(wts[n].shape[:2] + (d,))

    order = [(l, blk) for l in range(depth) for blk in BLOCKS]
    def placed(groups, after):
        return [place_quarters([wts[n] for n in BLOCK_WEIGHTS[blk]], l, ids, after) for l, blk in groups]

    first, token = gather_start("gather_start_first", placed(order[:1], gk4), gk4)
    rest, token = gather_start("gather_start_rest", placed(order[1:], token), token)
    started = dict(zip(order, first + rest))
    small["norm_g"] = small["norm_g"] + token[0:1, 0:1]

    filling = {}

    def fill(group, after):
        send, recv, lands = started[group]
        lands = gather_wait(lands, send, recv, after)
        started_fill, tok = copy_start("fill_start", [(lands, _fill_copies, 3 * len(lands))])
        filling[group] = started_fill[0]
        return tok[0:1, 0:1]

    def fetch(l, blk, after):
        at = order.index((l, blk))
        if (l, blk) not in filling:
            fill((l, blk), token if at == 0 else after)
        send, recv, lands = filling.pop((l, blk))
        lands = copy_wait("fill_wait", lands, send, recv, _fill_copies, (after,))
        tok, mid = None, None
        if at == 1:
            mid = functools.partial(fill, order[at + 1])
        elif 1 < at < len(order) - 1:
            tok = fill(order[at + 1], lands[0])
        return dict(zip(BLOCK_WEIGHTS[blk], lands)), tok, mid

    gsum = {n: None for n in BIG}
    presums, scatters, finals = [], [], []

    def scatter_ready(after):
        group, gl, lands, send, recv = presums.pop(0)
        gl, lands = presum_wait(gl, lands, send, recv, after)
        parts = list(add_halves(gl, lands, ids))
        m = len(parts)
        lands = [lax.empty((N_CHIP - 1,) + p.shape[1:], p.dtype) for p in parts]
        family = (parts + lands, lambda refs, sd, rv: (_scatter_copies(refs[:m], refs[m:], sd, rv),) * 2, 3 * m)
        return family, lambda sd, rv, bufs: scatters.append((group, bufs[:m], bufs[m:], sd, rv))

    def final_ready(after):
        (l, blk), parts, lands, send, recv = scatters.pop(0)
        parts, lands = scatter_wait(parts, lands, send, recv, after)
        names = BLOCK_WEIGHTS[blk]
        intos = None if gsum[names[0]] is None else [gsum[n] for n in names]
        sums = list(sum_partials(parts, lands, ids, l, depth, intos))

        def note(sd, rv, bufs):
            gsum.update(zip(names, bufs))
            finals.append((names, l, sd, rv))

        return (sums, _final_copies(l), len(sums)), note

    def start_all(name, ready, after=()):
        started, tok = copy_start(name, [family for family, _ in ready], after)
        for (_, note), (sd, rv, bufs) in zip(ready, started):
            note(sd, rv, bufs)
        return tok

    def scatter_next(after):
        return start_all("scatter_start", [scatter_ready(after)])

    def sum_next(after):
        return start_all("final_start", [final_ready(after)])

    def final_next(after):
        names, l, send, recv = finals.pop(0)
        gsum.update(zip(names, copy_wait("final_wait", [gsum[n] for n in names], send, recv, _final_copies(l), after)))

    def push(l, blk, grads):
        gl = [grads[n] for n in BLOCK_WEIGHTS[blk]]
        m = len(gl)
        lands = [lax.empty((g.shape[0], g.shape[1] // 2, g.shape[2]), g.dtype) for g in gl]
        ready = [((gl + lands, lambda refs, sd, rv: (_presum_copies(refs[:m], refs[m:], sd, rv),) * 2, m),
                  lambda sd, rv, bufs: presums.append(((l, blk), bufs[:m], bufs[m:], sd, rv)))]
        if scatters:
            ready.append(final_ready((gl[0],)))
        if presums:
            ready.append(scatter_ready((gl[0],)))
        return start_all("push_start", ready)[0:1, 0:1]

    loss, grad_x, gs = local_step(x, positions, loss_target, small, fetch, push)

    names = SHARDED_SMALL + REPLICATED_SMALL
    pg, offs = _pack_rows([gs[n].reshape(-1, d) for n in names])
    small_bufs = [pg, lax.empty((N_DEV,) + pg.shape, pg.dtype)]
    ((s_send, s_recv, s_bufs),), tok = copy_start("small_start", [(small_bufs, _small_copies, N_DEV - 1)], (grad_x,))
    tok = scatter_next((grad_x, tok))

    delta, new_m, new_v, grads = {}, {}, {}, {}

    def update(n, layer=None):
        g = gsum[n] if n in BIG else grads[n]
        prev = [delta[n], new_m[n], new_v[n], grads[n]] if layer is not None and n in delta else None
        delta[n], new_m[n], new_v[n], grads[n] = adamw(wts[n], g, mom[n], var[n], layer, prev)

    while finals and finals[0][1] > 0:
        done, l = finals[0][:2]
        final_next((tok,))
        for n in done:
            update(n, l)
    upper = tuple(delta[n] for n in BIG if n in delta)
    pg, slots = copy_wait("small_wait", s_bufs, s_send, s_recv, _small_copies, upper + (tok,))
    me = (2 * chip + lax.axis_index("c")).astype(jnp.int32).reshape(1)
    tot = sum_slots(pg, slots, me)
    for n, o in zip(names, offs):
        rows = int(np.prod(gs[n].shape[:-1]))
        full = tot[o:o + rows]
        if n in SHARDED_SMALL:
            full = lax.dynamic_slice_in_dim(full, chip * dq, dq, axis=1)
        grads[n] = full.reshape(wts[n].shape)

    for n in names:
        update(n)
    after = tuple(delta[n] for n in names)
    while scatters or finals:
        if scatters:
            after = (sum_next(after),)
        done, l = finals[0][:2]
        final_next(after)
        for n in done:
            update(n, l)
        after = tuple(delta[n] for n in done)

    loss_all = lax.psum(loss[0, 0], ("x", "y", "c"))
    return (loss_all, grad_x, *[grads[n] for n in WEIGHTS], *[delta[n] for n in WEIGHTS],
            *[new_m[n] for n in WEIGHTS], *[new_v[n] for n in WEIGHTS])
```

```python
import functools

import jax
import jax.numpy as jnp
import numpy as np
from jax import lax
from jax.experimental import pallas as pl
from jax.experimental.pallas import tpu as pltpu

F32 = jnp.float32
BF16 = jnp.bfloat16
MXU_DTYPE = BF16
VMEM_LIMIT_BYTES = 56 * 1024 * 1024
MESH = pl.DeviceIdType.MESH

N_CHIP = 4
CHUNK = 64
RET_HEADS = 4
RET_QK_DIM = 128
RET_V_DIM = 256
SC_KERNEL = 3
CF_KERNEL = 31
ROPE_BASE = 10000.0
NORM_EPS = 1e-6
LN_EPS = 1e-5
ADAM_LR = 0.001
ADAM_B1 = 0.9
ADAM_B2 = 0.999
ADAM_EPS = 1e-08
ADAM_WD = 0.01
ADAM_STEP = 10

SUBLANES = 8
CONV_PAD = 32
CONV_TS = 256
CONV_TC = 512
CONV_ROWS = 32
CONV_TILES = range(0, CONV_ROWS, SUBLANES)
SC_TS = 512


def _conv_scratch(ts):
    return [pltpu.VMEM((ts + CONV_PAD, CONV_TC), F32),
            pltpu.VMEM((SUBLANES - 1, ts + CONV_PAD - SUBLANES, CONV_TC), F32)]
RET_TQ = 512
MM_TM = 1024
MM_TN = 1536
MM_K1 = 1024
MM_W1 = 8 << 20
MM_SLICE = 256
MM_IN_BYTES = 36 << 20
STREAM_STEPS = 2


def _params(sem):
    return pltpu.CompilerParams(dimension_semantics=sem, vmem_limit_bytes=VMEM_LIMIT_BYTES)


def _axes():
    return lax.axis_index("x"), lax.axis_index("y"), lax.axis_index("c")


NN = (((1,), (0,)), ((), ()))
NT = (((1,), (1,)), ((), ()))
TN = (((0,), (0,)), ((), ()))


def _mm(name, a, b, out_shape, out_dtype, grid, a_spec, b_spec, o_spec, dims, acc_shape):
    nk = grid[2]

    def body(a_ref, b_ref, o_ref, *scratch):
        bv = b_ref[...]
        if bv.ndim == 3:
            bv = bv.reshape(-1, bv.shape[-1])
        part = lax.dot_general(a_ref[...], bv, dims, preferred_element_type=F32)

        def put(v):
            o_ref[...] = v.reshape(o_ref.shape).astype(o_ref.dtype)

        if nk == 1:
            put(part)
        else:
            acc = scratch[0]
            k = pl.program_id(2)

            @pl.when(k == 0)
            def _():
                acc[...] = part

            @pl.when(k > 0)
            def _():
                acc[...] += part

            @pl.when(k == nk - 1)
            def _():
                put(acc[...])

    scratch = [pltpu.VMEM(acc_shape, F32)] if nk > 1 else []
    return pl.pallas_call(
        body, name=name, grid=grid, in_specs=[a_spec, b_spec], out_specs=o_spec,
        out_shape=jax.ShapeDtypeStruct(out_shape, out_dtype), scratch_shapes=scratch,
        compiler_params=_params(("parallel", "parallel", "arbitrary")),
    )(a, b)


def _tile(n, target):
    best = None
    for t in range(128, min(n, target) + 1, 128):
        if n % t == 0:
            best = t
    assert best is not None, (n, target)
    return best


def _token_rows(t, width):
    tt = t
    while tt > MM_TM and tt * width * jnp.dtype(MXU_DTYPE).itemsize * 2 > MM_IN_BYTES:
        tt //= 2
    return tt


def mm_fwd(name, a, w4, mode, out_dtype):
    t = a.shape[0]
    _, r, c = w4.shape
    tm = min(t, MM_TM)
    if mode == "col":
        tn = _tile(c, MM_TN)
        npj = c // tn
        grid = (t // tm, N_CHIP * npj, 1)
        a_spec = pl.BlockSpec((tm, r), lambda i, j, k: (i, 0))
        b_spec = pl.BlockSpec((None, r, tn), lambda i, j, k: (j // npj, 0, j % npj))
        o_spec = pl.BlockSpec((tm, tn), lambda i, j, k: (i, j))
        return _mm(name, a, w4, (t, N_CHIP * c), out_dtype, grid, a_spec, b_spec, o_spec, NN, (tm, tn))
    if w4.size * w4.dtype.itemsize <= MM_W1:
        grid = (t // tm, 1, 1)
        a_spec = pl.BlockSpec((tm, N_CHIP * r), lambda i, j, k: (i, 0))
        b_spec = pl.BlockSpec((N_CHIP, r, c), lambda i, j, k: (0, 0, 0))
        o_spec = pl.BlockSpec((tm, c), lambda i, j, k: (i, 0))
        return _mm(name, a, w4, (t, c), out_dtype, grid, a_spec, b_spec, o_spec, NN, (tm, c))
    grid = (t // tm, 1, N_CHIP)
    a_spec = pl.BlockSpec((tm, r), lambda i, j, k: (i, k))
    b_spec = pl.BlockSpec((None, r, c), lambda i, j, k: (k, 0, 0))
    o_spec = pl.BlockSpec((tm, c), lambda i, j, k: (i, 0))
    return _mm(name, a, w4, (t, c), out_dtype, grid, a_spec, b_spec, o_spec, NN, (tm, c))


def mm_dx(name, dy, w4, mode, out_dtype):
    t = dy.shape[-2]
    _, r, c = w4.shape
    tm = min(t, MM_TM)
    if mode == "col":
        tn, npj = c, 1
        hb = N_CHIP // 2 * npj
        grid = (t // tm, 1, N_CHIP * npj)
        if dy.ndim == 3:
            a_spec = pl.BlockSpec((None, tm, tn), lambda i, j, k: (k // hb, i, k % hb))
        else:
            a_spec = pl.BlockSpec((tm, tn), lambda i, j, k: (i, k))
        b_spec = pl.BlockSpec((None, r, tn), lambda i, j, k: (k // npj, 0, k % npj))
        o_spec = pl.BlockSpec((tm, r), lambda i, j, k: (i, 0))
        return _mm(name, dy, w4, (t, r), out_dtype, grid, a_spec, b_spec, o_spec, NT, (tm, r))
    if N_CHIP * r <= MM_K1:
        grid = (t // tm, 1, 1)
        a_spec = pl.BlockSpec((tm, c), lambda i, j, k: (i, 0))
        b_spec = pl.BlockSpec((N_CHIP, r, c), lambda i, j, k: (0, 0, 0))
        o_spec = pl.BlockSpec((tm, N_CHIP * r), lambda i, j, k: (i, 0))
        return _mm(name, dy, w4, (t, N_CHIP * r), out_dtype, grid, a_spec, b_spec, o_spec, NT, (tm, N_CHIP * r))
    grid = (t // tm, N_CHIP, 1)
    a_spec = pl.BlockSpec((tm, c), lambda i, j, k: (i, 0))
    b_spec = pl.BlockSpec((None, r, c), lambda i, j, k: (j, 0, 0))
    o_spec = pl.BlockSpec((tm, r), lambda i, j, k: (i, j))
    return _mm(name, dy, w4, (t, N_CHIP * r), out_dtype, grid, a_spec, b_spec, o_spec, NT, (tm, r))


def mm_dw(name, a, dy, mode, shape3):
    t = a.shape[0]
    _, r, c = shape3
    if mode == "col":
        tn = _tile(c, MM_TN)
        npj = c // tn
        tt = _token_rows(t, r + tn)
        grid = (1, N_CHIP * npj, t // tt)
        a_spec = pl.BlockSpec((tt, r), lambda i, j, k: (k, 0))
        hb = N_CHIP // 2 * npj
        if dy.ndim == 3:
            b_spec = pl.BlockSpec((None, tt, tn), lambda i, j, k: (j // hb, k, j % hb))
        else:
            b_spec = pl.BlockSpec((tt, tn), lambda i, j, k: (k, j))
        o_spec = pl.BlockSpec((None, r, tn), lambda i, j, k: (j // npj, 0, j % npj))
        return _mm(name, a, dy, shape3, MXU_DTYPE, grid, a_spec, b_spec, o_spec, TN, (r, tn))
    if N_CHIP * r <= MM_K1:
        tt = min(_token_rows(t, N_CHIP * r + c), max(t // 4, MM_TM))
        grid = (1, 1, t // tt)
        a_spec = pl.BlockSpec((tt, N_CHIP * r), lambda i, j, k: (k, 0))
        b_spec = pl.BlockSpec((tt, c), lambda i, j, k: (k, 0))
        o_spec = pl.BlockSpec((N_CHIP, r, c), lambda i, j, k: (0, 0, 0))
        return _mm(name, a, dy, shape3, MXU_DTYPE, grid, a_spec, b_spec, o_spec, TN, (N_CHIP * r, c))
    tt = _token_rows(t, r + c)
    grid = (N_CHIP, 1, t // tt)
    a_spec = pl.BlockSpec((tt, r), lambda i, j, k: (k, i))
    b_spec = pl.BlockSpec((tt, c), lambda i, j, k: (k, 0))
    o_spec = pl.BlockSpec((None, r, c), lambda i, j, k: (i, 0, 0))
    return _mm(name, a, dy, shape3, MXU_DTYPE, grid, a_spec, b_spec, o_spec, TN, (r, c))


def _rms_bwd(x, g, dh):
    r = lax.rsqrt(jnp.mean(x * x, axis=-1, keepdims=True) + NORM_EPS)
    xhat = x * r
    dyg = dh * g
    dx = r * (dyg - xhat * jnp.mean(dyg * xhat, axis=-1, keepdims=True))
    return dx, jnp.sum(dh * xhat, axis=0, keepdims=True)


def mm_dx_norms(name, dy, w4, x, g_pre, dres, prev, after):
    t = dy.shape[-2]
    _, r, c = w4.shape
    tm = min(t, MM_TM // 2)
    nt, nk = t // tm, N_CHIP
    hb = N_CHIP // 2
    chained = prev is not None

    def body(dy_ref, w_ref, x_ref, dres_ref, g_ref, *rest):
        rest = rest[1:] if after is not None else rest
        if chained:
            y_ref, gp_ref, dx_ref, dg_ref, dyp_ref, dgp_ref, acc = rest
        else:
            dx_ref, dg_ref, acc = rest
        i, k = pl.program_id(0), pl.program_id(1)
        part = lax.dot_general(dy_ref[...], w_ref[...], NT, preferred_element_type=F32)

        @pl.when(k == 0)
        def _():
            acc[...] = part

        @pl.when(k > 0)
        def _():
            acc[...] += part

        def add_to(ref, v):
            @pl.when(i == 0)
            def _():
                ref[...] = v

            @pl.when(i > 0)
            def _():
                ref[...] += v

        @pl.when(k == nk - 1)
        def _():
            dx, dg = _rms_bwd(x_ref[...], g_ref[...], acc[...])
            dxs = dres_ref[...] + dx
            dx_ref[...] = dxs
            add_to(dg_ref, dg)
            if chained:
                dyp, dgp = _rms_bwd(y_ref[...], gp_ref[...], dxs)
                dyp_ref[...] = (prev[2] * dyp).astype(dyp_ref.dtype)
                add_to(dgp_ref, prev[2] * dgp)

    if dy.ndim == 3:
        dy_spec = pl.BlockSpec((None, tm, c), lambda i, k: (k // hb, i, k % hb))
    else:
        dy_spec = pl.BlockSpec((tm, c), lambda i, k: (i, k))
    rows = pl.BlockSpec((tm, r), lambda i, k: (i, 0))
    gain = pl.BlockSpec((1, r), lambda i, k: (0, 0))
    in_specs = [dy_spec, pl.BlockSpec((None, r, c), lambda i, k: (k, 0, 0)), rows, rows, gain]
    args = [dy, w4, x, dres, g_pre]
    if after is not None:
        in_specs.append(pl.BlockSpec(memory_space=pl.ANY))
        args.append(after)
    out_specs = [rows, gain]
    out_shape = [jax.ShapeDtypeStruct((t, r), F32), jax.ShapeDtypeStruct((1, r), F32)]
    if chained:
        in_specs += [rows, gain]
        args += [prev[0], prev[1]]
        out_specs += [rows, gain]
        out_shape += [jax.ShapeDtypeStruct((t, r), MXU_DTYPE), jax.ShapeDtypeStruct((1, r), F32)]
    res = pl.pallas_call(
        body, name=name, grid=(nt, nk), in_specs=in_specs, out_specs=out_specs, out_shape=out_shape,
        scratch_shapes=[pltpu.VMEM((tm, r), F32)], compiler_params=_params(("arbitrary", "arbitrary")),
    )(*args)
    return tuple(res) if chained else (res[0], res[1], None, None)


def _rowwise(name, fn, rows, pars, outs, accs=(), tm=256, ncol=1):
    t = rows[0][0].shape[0]
    nrow, npar, nout = len(rows), len(pars), len(outs)

    def body(*refs):
        vals = [r[...] for r in refs[:nrow + npar]]
        res = fn(*vals)
        out_refs = refs[nrow + npar:nrow + npar + nout]
        acc_refs = refs[nrow + npar + nout:]
        for o, v in zip(out_refs, res[:nout]):
            o[...] = v.astype(o.dtype)
        i = pl.program_id(1)
        for a, v in zip(acc_refs, res[nout:]):
            @pl.when(i == 0)
            def _(a=a, v=v):
                a[...] = v.astype(F32)

            @pl.when(i > 0)
            def _(a=a, v=v):
                a[...] += v.astype(F32)

    in_specs = [pl.BlockSpec((tm, w), functools.partial(lambda j, i, b: (i, b + j), b=b)) for _, w, b in rows]
    for arr, w in pars:
        if w is None:
            in_specs.append(pl.BlockSpec(arr.shape, lambda j, i: (0, 0)))
        else:
            in_specs.append(pl.BlockSpec((1, w), lambda j, i: (0, j)))
    out_specs = [pl.BlockSpec((tm, w), lambda j, i: (i, j)) for _, w, _ in outs]
    out_specs += [pl.BlockSpec((1, w), lambda j, i: (0, j)) for _, w in accs]
    out_shape = [jax.ShapeDtypeStruct((t, tw), dt) for tw, _, dt in outs]
    out_shape += [jax.ShapeDtypeStruct((1, tw), F32) for tw, _ in accs]
    res = pl.pallas_call(
        body, name=name, grid=(ncol, t // tm), in_specs=in_specs, out_specs=out_specs, out_shape=out_shape,
        compiler_params=_params(("parallel", "arbitrary" if accs else "parallel")),
    )(*[r[0] for r in rows], *[p[0] for p in pars])
    return res


def _rms(x, g):
    xf = x.astype(F32)
    return xf * lax.rsqrt(jnp.mean(xf * xf, axis=-1, keepdims=True) + NORM_EPS) * g


def _silu(x):
    return x * jax.nn.sigmoid(x)


def rms_fwd(name, x, g):
    d = x.shape[1]
    return _rowwise(name, lambda x, g: (_rms(x, g),), [(x, d, 0)], [(g, None)], [(d, d, MXU_DTYPE)], tm=512)[0]


def rms_bwd(name, x, g, dh, dres):
    d = x.shape[1]

    def fn(x, dh, dres, g):
        _, vjp = jax.vjp(_rms, x, g)
        dx, dg = vjp(dh.astype(F32))
        return dres + dx, dg

    return _rowwise(name, fn, [(x, d, 0), (dh, d, 0), (dres, d, 0)], [(g, None)], [(d, d, F32)], [(d, d)], tm=256)


def mm_post(name, a, w4, x, g_post, scale, g_next):
    t = a.shape[0]
    _, r, c = w4.shape
    tm = min(t, MM_TM // 2)
    chained = g_next is not None

    def body(a_ref, w_ref, x_ref, gp_ref, *rest):
        gn_ref, y_ref, xn_ref, h_ref = rest if chained else (None,) + rest + (None,)
        y = lax.dot_general(a_ref[...], w_ref[...].reshape(N_CHIP * r, c), NN, preferred_element_type=F32)
        y_ref[...] = y
        xn = x_ref[...] + scale * _rms(y, gp_ref[...])
        xn_ref[...] = xn
        if chained:
            h_ref[...] = _rms(xn, gn_ref[...]).astype(h_ref.dtype)

    def rows(width):
        return pl.BlockSpec((tm, width), lambda i: (i, 0))

    gain = pl.BlockSpec((1, c), lambda i: (0, 0))
    in_specs = [rows(N_CHIP * r), pl.BlockSpec((N_CHIP, r, c), lambda i: (0, 0, 0)), rows(c), gain]
    args = [a, w4, x, g_post]
    out_specs, out_shape = [rows(c), rows(c)], [jax.ShapeDtypeStruct((t, c), F32)] * 2
    if chained:
        in_specs.append(gain)
        args.append(g_next)
        out_specs.append(rows(c))
        out_shape.append(jax.ShapeDtypeStruct((t, c), MXU_DTYPE))
    res = pl.pallas_call(
        body, name=name, grid=(t // tm,), in_specs=in_specs, out_specs=out_specs, out_shape=out_shape,
        compiler_params=_params(("parallel",)),
    )(*args)
    return res[0], res[1], (res[2] if chained else None)


def post_bwd(name, y, g, dx, scale):
    d = y.shape[1]

    def fn(y, dx, g):
        _, vjp = jax.vjp(lambda y, g: scale * _rms(y, g), y, g)
        return vjp(dx)

    return _rowwise(name, fn, [(y, d, 0), (dx, d, 0)], [(g, None)], [(d, d, MXU_DTYPE)], [(d, d)], tm=256)


def ffn_up(name, h, w4):
    t = h.shape[0]
    _, r, c = w4.shape
    tm = min(t, MM_TM)
    tn = _tile(c, MM_TM)
    npj = c // tn
    half = N_CHIP // 2

    def body(h_ref, wg_ref, wu_ref, gu_ref, a_ref):
        hv = h_ref[...]
        g = lax.dot_general(hv, wg_ref[...], NN, preferred_element_type=F32)
        u = lax.dot_general(hv, wu_ref[...], NN, preferred_element_type=F32)
        sg = jax.nn.sigmoid(g)
        silu = g * sg
        gu_ref[0] = (u * (sg + silu * (1.0 - sg))).astype(gu_ref.dtype)
        gu_ref[1] = silu.astype(gu_ref.dtype)
        a_ref[...] = (silu * u).astype(a_ref.dtype)

    f = half * c
    return pl.pallas_call(
        body, name=name, grid=(t // tm, half * npj),
        in_specs=[pl.BlockSpec((tm, r), lambda i, j: (i, 0)),
                  pl.BlockSpec((None, r, tn), lambda i, j: (j // npj, 0, j % npj)),
                  pl.BlockSpec((None, r, tn), lambda i, j: (half + j // npj, 0, j % npj))],
        out_specs=[pl.BlockSpec((2, tm, tn), lambda i, j: (0, i, j)), pl.BlockSpec((tm, tn), lambda i, j: (i, j))],
        out_shape=[jax.ShapeDtypeStruct((2, t, f), MXU_DTYPE), jax.ShapeDtypeStruct((t, f), MXU_DTYPE)],
        compiler_params=_params(("parallel", "parallel")),
    )(h, w4, w4)


def ffn_down_dx(name, dy, w4, gu):
    t = dy.shape[0]
    _, r, c = w4.shape
    tm = min(t, MM_TM)

    def body(dy_ref, w_ref, gu_ref, o_ref):
        dyv = dy_ref[...]
        for n0 in range(0, r, MM_SLICE):
            cols = pl.ds(n0, MM_SLICE)
            da = lax.dot_general(dyv, w_ref[cols, :], NT, preferred_element_type=F32)
            o_ref[0, :, cols] = (da * gu_ref[0, :, cols].astype(F32)).astype(o_ref.dtype)
            o_ref[1, :, cols] = (da * gu_ref[1, :, cols].astype(F32)).astype(o_ref.dtype)

    return pl.pallas_call(
        body, name=name, grid=(t // tm, N_CHIP),
        in_specs=[pl.BlockSpec((tm, c), lambda i, j: (i, 0)), pl.BlockSpec((None, r, c), lambda i, j: (j, 0, 0)),
                  pl.BlockSpec((2, tm, r), lambda i, j: (0, i, j))],
        out_specs=pl.BlockSpec((2, tm, r), lambda i, j: (0, i, j)),
        out_shape=jax.ShapeDtypeStruct((2, t, N_CHIP * r), MXU_DTYPE),
        compiler_params=_params(("parallel", "parallel")),
    )(dy, w4, gu)


def _head_gate(o, g):
    mu = jnp.mean(o, axis=-1, keepdims=True)
    var = jnp.mean(jnp.square(o - mu), axis=-1, keepdims=True)
    return _silu(g.astype(F32)) * ((o - mu) * lax.rsqrt(var + LN_EPS))


def head_gate_fwd(name, o, p, gate_blk):
    dv = RET_V_DIM
    return _rowwise(name, lambda o, g: (_head_gate(o, g),), [(o, dv, 0), (p, dv, gate_blk)], [],
                    [(RET_HEADS * dv, dv, MXU_DTYPE)], tm=min(o.shape[0], 2048), ncol=RET_HEADS)[0]


def head_gate_bwd(name, o, p, gate_blk, da):
    dv = RET_V_DIM

    def fn(o, g, da):
        _, vjp = jax.vjp(_head_gate, o, g.astype(F32))
        return vjp(da.astype(F32))

    w = RET_HEADS * dv
    return _rowwise(name, fn, [(o, dv, 0), (p, dv, gate_blk), (da, dv, 0)], [],
                    [(w, dv, MXU_DTYPE), (w, dv, MXU_DTYPE)], tm=min(o.shape[0], 2048), ncol=RET_HEADS)


def _ln_silu(u, g, b):
    mu = jnp.mean(u, axis=-1, keepdims=True)
    var = jnp.mean(jnp.square(u - mu), axis=-1, keepdims=True)
    return _silu((u - mu) * lax.rsqrt(var + LN_EPS) * g + b)


def ln_silu_fwd(name, u, g, b):
    d = u.shape[1]
    return _rowwise(name, lambda u, g, b: (_ln_silu(u, g, b),), [(u, d, 0)], [(g, None), (b, None)],
                    [(d, d, MXU_DTYPE)], tm=512)[0]


def ln_silu_bwd(name, u, g, b, dc):
    d = u.shape[1]

    def fn(u, dc, g, b):
        _, vjp = jax.vjp(_ln_silu, u, g, b)
        return vjp(dc.astype(F32))

    return _rowwise(name, fn, [(u, d, 0), (dc, d, 0)], [(g, None), (b, None)], [(d, d, F32)], [(d, d), (d, d)],
                    tm=512)


def _merge(g0, g1, g2, ya, yb, yc):
    s = jax.nn.sigmoid
    return s(g0.astype(F32)) * ya + s(g1.astype(F32)) * yb + s(g2.astype(F32)) * yc


def merge_fwd(name, p, blk, ya, yb, yc):
    d = ya.shape[1]
    rows = [(p, d, blk), (p, d, blk + 1), (p, d, blk + 2), (ya, d, 0), (yb, d, 0), (yc, d, 0)]
    return _rowwise(name, lambda *v: (_merge(*v),), rows, [], [(d, d, MXU_DTYPE)], tm=512)[0]


def merge_bwd(name, p, blk, ya, yb, yc, dmg):
    d = ya.shape[1]

    def fn(g0, g1, g2, ya, yb, yc, dmg):
        _, vjp = jax.vjp(_merge, g0.astype(F32), g1.astype(F32), g2.astype(F32), ya, yb, yc)
        return vjp(dmg.astype(F32))

    rows = [(p, d, blk), (p, d, blk + 1), (p, d, blk + 2), (ya, d, 0), (yb, d, 0), (yc, d, 0), (dmg, d, 0)]
    return _rowwise(name, fn, rows, [], [(d, d, MXU_DTYPE)] * 6, tm=256)


def concat_cols(name, pieces):
    t = pieces[0].shape[0]
    widths = [p.shape[1] for p in pieces]
    tm = 256

    def body(*refs):
        o_ref, at = refs[-1], 0
        for r, w in zip(refs[:-1], widths):
            o_ref[:, at:at + w] = r[...]
            at += w

    return pl.pallas_call(
        body, name=name, grid=(t // tm,),
        in_specs=[pl.BlockSpec((tm, w), lambda i: (i, 0)) for w in widths],
        out_specs=pl.BlockSpec((tm, sum(widths)), lambda i: (i, 0)),
        out_shape=jax.ShapeDtypeStruct((t, sum(widths)), pieces[0].dtype),
        compiler_params=_params(("parallel",)),
    )(*pieces)


def loss_head(name, y, target):
    t, d = y.shape
    tm = 512

    def body(y_ref, t_ref, dy_ref, loss_ref):
        err = y_ref[...] - t_ref[...]
        dy_ref[...] = err * (1.0 / d)
        part = jnp.sum(jnp.sum(err * err, axis=1, keepdims=True), axis=0, keepdims=True) * (0.5 / d)

        @pl.when(pl.program_id(0) == 0)
        def _():
            loss_ref[...] = part

        @pl.when(pl.program_id(0) > 0)
        def _():
            loss_ref[...] += part

    return pl.pallas_call(
        body, name=name, grid=(t // tm,),
        in_specs=[pl.BlockSpec((tm, d), lambda i: (i, 0))] * 2,
        out_specs=[pl.BlockSpec((tm, d), lambda i: (i, 0)), pl.BlockSpec((1, 1), lambda i: (0, 0))],
        out_shape=[jax.ShapeDtypeStruct((t, d), F32), jax.ShapeDtypeStruct((1, 1), F32)],
        compiler_params=_params(("arbitrary",)),
    )(y, target)


def _rot(x, cos2, sin2):
    return x * cos2 + pltpu.roll(x, RET_QK_DIM // 2, 1) * sin2


def _decay_mask(lg, n0, rows, cols):
    n = n0 + lax.broadcasted_iota(jnp.int32, (rows, cols), 0)
    m = lax.broadcasted_iota(jnp.int32, (rows, cols), 1)
    shift = CHUNK.bit_length() - 1
    dist = jnp.abs(n - m).astype(F32)
    return jnp.where((m >> shift) <= (n >> shift), jnp.exp(lg * dist), 0.0)


def _ret_specs(s):
    dk, dv, h = RET_QK_DIM, RET_V_DIM, RET_HEADS
    return [
        pl.BlockSpec((s, dk), lambda b, hh: (b, hh)),
        pl.BlockSpec((s, dk), lambda b, hh: (b, h + hh)),
        pl.BlockSpec((s, dv), lambda b, hh: (b, (2 * h * dk) // dv + hh)),
        pl.BlockSpec((s, dk), lambda b, hh: (b, 0)),
        pl.BlockSpec((s, dk), lambda b, hh: (b, 0)),
        pl.BlockSpec((None, 1, dk), lambda b, hh: (hh, 0, 0)),
    ]


def retention_fwd(name, p, cos2, sin2, log_g, nb, s):
    dk, dv, h = RET_QK_DIM, RET_V_DIM, RET_HEADS

    def body(q_ref, k_ref, v_ref, cos_ref, sin_ref, lg_ref, o_ref, kr_ref):
        lg = lg_ref[0:1, 0:1]
        kr = _rot(k_ref[...].astype(F32), cos_ref[...], sin_ref[...]) * (dk ** -0.5)
        kr_ref[...] = kr.astype(kr_ref.dtype)
        for qi in range(s // RET_TQ):
            n0, kmax = qi * RET_TQ, (qi + 1) * RET_TQ
            rows = pl.ds(n0, RET_TQ)
            qr = _rot(q_ref[rows, :].astype(F32), cos_ref[rows, :], sin_ref[rows, :]).astype(MXU_DTYPE)
            sc = lax.dot_general(qr, kr_ref[0:kmax, :], NT, preferred_element_type=F32)
            pm = (sc * _decay_mask(lg, n0, RET_TQ, kmax)).astype(MXU_DTYPE)
            o_ref[rows, :] = lax.dot_general(pm, v_ref[0:kmax, :], NN, preferred_element_type=F32)

    return pl.pallas_call(
        body, name=name, grid=(nb, h), in_specs=_ret_specs(s),
        out_specs=pl.BlockSpec((s, dv), lambda b, hh: (b, hh)),
        out_shape=jax.ShapeDtypeStruct((nb * s, h * dv), F32),
        scratch_shapes=[pltpu.VMEM((s, dk), MXU_DTYPE)],
        compiler_params=_params(("parallel", "parallel")),
    )(p, p, p, cos2, sin2, log_g)


def retention_bwd(name, p, cos2, sin2, log_g, do, nb, s):
    dk, dv, h = RET_QK_DIM, RET_V_DIM, RET_HEADS

    def body(q_ref, k_ref, v_ref, cos_ref, sin_ref, lg_ref, do_ref, dq_ref, dk_ref, dv_ref, kr_ref, dk_acc, dv_acc):
        lg = lg_ref[0:1, 0:1]
        kr = _rot(k_ref[...].astype(F32), cos_ref[...], sin_ref[...]) * (dk ** -0.5)
        kr_ref[...] = kr.astype(kr_ref.dtype)
        dk_acc[...] = jnp.zeros_like(dk_acc)
        dv_acc[...] = jnp.zeros_like(dv_acc)
        for qi in range(s // RET_TQ):
            n0, kmax = qi * RET_TQ, (qi + 1) * RET_TQ
            rows = pl.ds(n0, RET_TQ)
            cq, sq = cos_ref[rows, :], sin_ref[rows, :]
            qr = _rot(q_ref[rows, :].astype(F32), cq, sq).astype(MXU_DTYPE)
            dob = do_ref[rows, :]
            mask = _decay_mask(lg, n0, RET_TQ, kmax)
            sc = lax.dot_general(qr, kr_ref[0:kmax, :], NT, preferred_element_type=F32)
            pm = (sc * mask).astype(MXU_DTYPE)
            dv_acc[0:kmax, :] += lax.dot_general(pm, dob, TN, preferred_element_type=F32)
            dp = lax.dot_general(dob, v_ref[0:kmax, :], NT, preferred_element_type=F32)
            ds = (dp * mask).astype(MXU_DTYPE)
            dqr = lax.dot_general(ds, kr_ref[0:kmax, :], NN, preferred_element_type=F32)
            dq_ref[rows, :] = _rot(dqr, cq, -sq).astype(dq_ref.dtype)
            dk_acc[0:kmax, :] += lax.dot_general(ds, qr, TN, preferred_element_type=F32)
        dkr = dk_acc[...] * (dk ** -0.5)
        dk_ref[...] = _rot(dkr, cos_ref[...], -sin_ref[...]).astype(dk_ref.dtype)
        dv_ref[...] = dv_acc[...].astype(dv_ref.dtype)

    t = nb * s
    return pl.pallas_call(
        body, name=name, grid=(nb, h),
        in_specs=_ret_specs(s) + [pl.BlockSpec((s, dv), lambda b, hh: (b, hh))],
        out_specs=[pl.BlockSpec((s, dk), lambda b, hh: (b, hh)), pl.BlockSpec((s, dk), lambda b, hh: (b, hh)),
                   pl.BlockSpec((s, dv), lambda b, hh: (b, hh))],
        out_shape=[jax.ShapeDtypeStruct((t, h * dk), MXU_DTYPE), jax.ShapeDtypeStruct((t, h * dk), MXU_DTYPE),
                   jax.ShapeDtypeStruct((t, h * dv), MXU_DTYPE)],
        scratch_shapes=[pltpu.VMEM((s, dk), MXU_DTYPE), pltpu.VMEM((s, dk), F32), pltpu.VMEM((s, dv), F32)],
        compiler_params=_params(("parallel", "parallel")),
    )(p, p, p, cos2, sin2, log_g, do)


def _conv_grid(t, d, nb, ts):
    s = t // nb
    ns, nc = s // ts, d // CONV_TC
    return s, ns, nc


def _shifted(pad_ref, sh_ref, offsets):
    n = sh_ref.shape[1]
    for b in sorted({off % SUBLANES for off in offsets} - {0}):
        sh_ref[b - 1] = pad_ref[pl.ds(b, n), :]

    def read(off, r0):
        a, b = off - off % SUBLANES + r0, off % SUBLANES
        return pad_ref[pl.ds(a, SUBLANES), :] if b == 0 else sh_ref[b - 1, pl.ds(a, SUBLANES), :]

    return read


def _causal_taps(pad_ref, sh_ref, w_ref, k, emit):
    offs = [CONV_PAD - (k - 1) + j for j in range(k)]
    read = _shifted(pad_ref, sh_ref, offs)
    for r0 in range(0, pad_ref.shape[0] - CONV_PAD, CONV_ROWS):
        accs = [None] * len(CONV_TILES)
        for j in range(k):
            wj = w_ref[j]
            for q, dr in enumerate(CONV_TILES):
                term = wj * read(offs[j], r0 + dr)
                accs[q] = term if accs[q] is None else accs[q] + term
        emit(r0, jnp.concatenate(accs, axis=0))


def _tap_tiles(w):
    return jnp.broadcast_to(w[:, None, :], (w.shape[0], SUBLANES, w.shape[1]))


def _tap_spec(k):
    return pl.BlockSpec((k, SUBLANES, CONV_TC), lambda c, b, si: (0, 0, c))


def _carry_past(pad_ref, s_idx):
    ts = pad_ref.shape[0] - CONV_PAD

    @pl.when(s_idx == 0)
    def _():
        pad_ref[0:CONV_PAD, :] = jnp.zeros((CONV_PAD, pad_ref.shape[1]), F32)

    @pl.when(s_idx > 0)
    def _():
        pad_ref[0:CONV_PAD, :] = pad_ref[ts:ts + CONV_PAD, :]


def _carry_future(pad_ref, s_idx):
    ts = pad_ref.shape[0] - CONV_PAD

    @pl.when(s_idx == 0)
    def _():
        pad_ref[ts:ts + CONV_PAD, :] = jnp.zeros((CONV_PAD, pad_ref.shape[1]), F32)

    @pl.when(s_idx > 0)
    def _():
        pad_ref[ts:ts + CONV_PAD, :] = pad_ref[0:CONV_PAD, :]


def _conv_bwd_taps(pad_ref, sh_ref, w_ref, dw_acc, k, x_rows, emit, mix):
    read = _shifted(pad_ref, sh_ref, range(k))
    for r0 in range(0, pad_ref.shape[0] - CONV_PAD, CONV_ROWS):
        ops = x_rows(r0)
        x = mix(ops)
        accs = [None] * len(CONV_TILES)
        for j in range(k):
            wj, dwj = w_ref[j], None
            for q, dr in enumerate(CONV_TILES):
                sh = read(k - 1 - j, r0 + dr)
                term = wj * sh
                accs[q] = term if accs[q] is None else accs[q] + term
                prod = x[dr:dr + SUBLANES] * sh
                dwj = prod if dwj is None else dwj + prod
            dw_acc[j] += dwj
        emit(r0, ops, jnp.concatenate(accs, axis=0))


def _conv_bwd_edges(dw_acc, dw_ref, nb, ns, extra=()):
    first = jnp.logical_and(pl.program_id(1) == 0, pl.program_id(2) == 0)
    last = jnp.logical_and(pl.program_id(1) == nb - 1, pl.program_id(2) == ns - 1)

    @pl.when(first)
    def _():
        dw_acc[...] = jnp.zeros_like(dw_acc)
        for r in extra:
            r[...] = jnp.zeros_like(r)

    def finish():
        @pl.when(last)
        def _():
            dw_ref[...] = jnp.sum(dw_acc[...], axis=1)

    return finish


def short_conv_fwd(name, p, blk_b, w, nb):
    t = p.shape[0]
    d = w.shape[1]
    ts = SC_TS
    s, ns, nc = _conv_grid(t, d, nb, ts)
    cb = d // CONV_TC

    def body(b_ref, c_ref, x_ref, w_ref, y_ref, cz_ref, pad_ref, sh_ref):
        _carry_past(pad_ref, pl.program_id(2))
        pad_ref[CONV_PAD:CONV_PAD + ts, :] = c_ref[...].astype(F32) * x_ref[...].astype(F32)

        def emit(r0, cz):
            rows = pl.ds(r0, CONV_ROWS)
            cz_ref[rows, :] = cz
            y_ref[rows, :] = (b_ref[rows, :].astype(F32) * cz).astype(y_ref.dtype)

        _causal_taps(pad_ref, sh_ref, w_ref, SC_KERNEL, emit)

    def pspec(off):
        return pl.BlockSpec((ts, CONV_TC), lambda c, b, si: (b * ns + si, (blk_b + off) * cb + c))

    ospec = pl.BlockSpec((ts, CONV_TC), lambda c, b, si: (b * ns + si, c))
    return pl.pallas_call(
        body, name=name, grid=(nc, nb, ns),
        in_specs=[pspec(0), pspec(1), pspec(2), _tap_spec(SC_KERNEL)],
        out_specs=[ospec, ospec],
        out_shape=[jax.ShapeDtypeStruct((t, d), MXU_DTYPE), jax.ShapeDtypeStruct((t, d), F32)],
        scratch_shapes=_conv_scratch(ts),
        compiler_params=_params(("parallel", "arbitrary", "arbitrary")),
    )(p, p, p, _tap_tiles(w))


def short_conv_bwd(name, p, blk_b, w, cz, dy, nb):
    t = p.shape[0]
    d = w.shape[1]
    ts = SC_TS
    s, ns, nc = _conv_grid(t, d, nb, ts)
    cb = d // CONV_TC

    def body(b_ref, c_ref, x_ref, w_ref, cz_ref, dy_ref, db_ref, dc_ref, dx_ref, dw_ref, pad_ref, sh_ref, dw_acc):
        _carry_future(pad_ref, pl.program_id(2))
        dyv = dy_ref[...].astype(F32)
        db_ref[...] = (dyv * cz_ref[...]).astype(db_ref.dtype)
        pad_ref[0:ts, :] = dyv * b_ref[...].astype(F32)
        finish = _conv_bwd_edges(dw_acc, dw_ref, nb, ns)

        def x_rows(r0):
            rows = pl.ds(r0, CONV_ROWS)
            return c_ref[rows, :].astype(F32), x_ref[rows, :].astype(F32)

        def emit(r0, cx, dz):
            rows = pl.ds(r0, CONV_ROWS)
            dc_ref[rows, :] = (dz * cx[1]).astype(dc_ref.dtype)
            dx_ref[rows, :] = (dz * cx[0]).astype(dx_ref.dtype)

        _conv_bwd_taps(pad_ref, sh_ref, w_ref, dw_acc, SC_KERNEL, x_rows, emit, lambda cx: cx[0] * cx[1])
        finish()

    def row(b, si):
        return b * ns + (ns - 1 - si)

    def pspec(off):
        return pl.BlockSpec((ts, CONV_TC), lambda c, b, si: (row(b, si), (blk_b + off) * cb + c))

    ospec = pl.BlockSpec((ts, CONV_TC), lambda c, b, si: (row(b, si), c))
    wspec = pl.BlockSpec((SC_KERNEL, CONV_TC), lambda c, b, si: (0, c))
    return pl.pallas_call(
        body, name=name, grid=(nc, nb, ns),
        in_specs=[pspec(0), pspec(1), pspec(2), _tap_spec(SC_KERNEL), ospec, ospec],
        out_specs=[ospec, ospec, ospec, wspec],
        out_shape=[jax.ShapeDtypeStruct((t, d), MXU_DTYPE)] * 3 + [jax.ShapeDtypeStruct((SC_KERNEL, d), F32)],
        scratch_shapes=_conv_scratch(ts) + [pltpu.VMEM((SC_KERNEL, SUBLANES, CONV_TC), F32)],
        compiler_params=_params(("parallel", "arbitrary", "arbitrary")),
    )(p, p, p, _tap_tiles(w), cz, dy)


def conformer_conv_fwd(name, p, blk_a, w, bias, nb):
    t = p.shape[0]
    d = w.shape[1]
    ts = CONV_TS
    s, ns, nc = _conv_grid(t, d, nb, ts)
    cb = d // CONV_TC

    def body(a_ref, b_ref, w_ref, bias_ref, u_ref, pad_ref, sh_ref):
        _carry_past(pad_ref, pl.program_id(2))
        pad_ref[CONV_PAD:CONV_PAD + ts, :] = a_ref[...].astype(F32) * jax.nn.sigmoid(b_ref[...].astype(F32))

        def emit(r0, u):
            u_ref[pl.ds(r0, CONV_ROWS), :] = u + bias_ref[0:1, :]

        _causal_taps(pad_ref, sh_ref, w_ref, CF_KERNEL, emit)

    def pspec(off):
        return pl.BlockSpec((ts, CONV_TC), lambda c, b, si: (b * ns + si, (blk_a + off) * cb + c))

    return pl.pallas_call(
        body, name=name, grid=(nc, nb, ns),
        in_specs=[pspec(0), pspec(1), _tap_spec(CF_KERNEL), pl.BlockSpec((SUBLANES, CONV_TC), lambda c, b, si: (0, c))],
        out_specs=pl.BlockSpec((ts, CONV_TC), lambda c, b, si: (b * ns + si, c)),
        out_shape=jax.ShapeDtypeStruct((t, d), F32),
        scratch_shapes=_conv_scratch(ts),
        compiler_params=_params(("parallel", "arbitrary", "arbitrary")),
    )(p, p, _tap_tiles(w), jnp.broadcast_to(bias, (SUBLANES, d)))


def conformer_conv_bwd(name, p, blk_a, w, du, nb):
    t = p.shape[0]
    d = w.shape[1]
    ts = CONV_TS
    s, ns, nc = _conv_grid(t, d, nb, ts)
    cb = d // CONV_TC

    def body(a_ref, b_ref, w_ref, du_ref, da_ref, db_ref, dw_ref, dbias_ref, pad_ref, sh_ref, dw_acc):
        _carry_future(pad_ref, pl.program_id(2))
        duv = du_ref[...]
        pad_ref[0:ts, :] = duv
        finish = _conv_bwd_edges(dw_acc, dw_ref, nb, ns, extra=(dbias_ref,))
        dbias_ref[...] += jnp.sum(duv, axis=0, keepdims=True)

        def x_rows(r0):
            rows = pl.ds(r0, CONV_ROWS)
            return a_ref[rows, :].astype(F32), jax.nn.sigmoid(b_ref[rows, :].astype(F32))

        def emit(r0, asg, du0):
            rows = pl.ds(r0, CONV_ROWS)
            av, sg = asg
            da_ref[rows, :] = (du0 * sg).astype(da_ref.dtype)
            db_ref[rows, :] = (du0 * av * sg * (1.0 - sg)).astype(db_ref.dtype)

        _conv_bwd_taps(pad_ref, sh_ref, w_ref, dw_acc, CF_KERNEL, x_rows, emit, lambda asg: asg[0] * asg[1])
        finish()

    def row(b, si):
        return b * ns + (ns - 1 - si)

    def pspec(off):
        return pl.BlockSpec((ts, CONV_TC), lambda c, b, si: (row(b, si), (blk_a + off) * cb + c))

    ospec = pl.BlockSpec((ts, CONV_TC), lambda c, b, si: (row(b, si), c))
    wspec = pl.BlockSpec((CF_KERNEL, CONV_TC), lambda c, b, si: (0, c))
    bspec = pl.BlockSpec((1, CONV_TC), lambda c, b, si: (0, c))
    return pl.pallas_call(
        body, name=name, grid=(nc, nb, ns),
        in_specs=[pspec(0), pspec(1), _tap_spec(CF_KERNEL), ospec],
        out_specs=[ospec, ospec, wspec, bspec],
        out_shape=[jax.ShapeDtypeStruct((t, d), MXU_DTYPE)] * 2
        + [jax.ShapeDtypeStruct((CF_KERNEL, d), F32), jax.ShapeDtypeStruct((1, d), F32)],
        scratch_shapes=_conv_scratch(ts) + [pltpu.VMEM((CF_KERNEL, SUBLANES, CONV_TC), F32)],
        compiler_params=_params(("parallel", "arbitrary", "arbitrary")),
    )(p, p, _tap_tiles(w), du)


BLOCKS = ("ffn1", "mixer", "ffn2")
BLOCK_WEIGHTS = {"ffn1": ("ffn1_w_gu", "ffn1_w_down"), "mixer": ("w_in", "w_ret_o", "w_sc_o", "w_cf_o", "w_o"),
                 "ffn2": ("ffn2_w_gu", "ffn2_w_down")}
BIG = BLOCK_WEIGHTS["ffn1"] + BLOCK_WEIGHTS["mixer"] + BLOCK_WEIGHTS["ffn2"]
MODE = {"ffn1_w_gu": "col", "ffn1_w_down": "row", "w_in": "col", "w_ret_o": "row", "w_sc_o": "row",
        "w_cf_o": "row", "w_o": "row", "ffn2_w_gu": "col", "ffn2_w_down": "row"}
NORM_OF = {"ffn1": 0, "mixer": 2, "ffn2": 4}
BLK_GATE, BLK_SCB, BLK_CFA, BLK_MERGE = 2, 3, 6, 8


def _rope_tables(positions):
    half = RET_QK_DIM // 2
    inv_freq = ROPE_BASE ** (-jnp.arange(half, dtype=F32) / half)
    ang = positions.astype(F32)[..., None] * inv_freq
    cos, sin = jnp.cos(ang), jnp.sin(ang)
    nb, s = positions.shape
    cos2 = jnp.concatenate([cos, cos], axis=-1).reshape(nb * s, RET_QK_DIM)
    sin2 = jnp.concatenate([-sin, sin], axis=-1).reshape(nb * s, RET_QK_DIM)
    return cos2, sin2


def _log_gamma():
    lg = jnp.log(1.0 - 2.0 ** (-5.0 - jnp.arange(RET_HEADS, dtype=F32)))
    return jnp.broadcast_to(lg[:, None, None], (RET_HEADS, 1, RET_QK_DIM))


def _ffn_fwd(xs, h, w, tag, g_post, g_next):
    gu, a = ffn_up("ffn_up", h, w[tag + "_w_gu"])
    y, out, h_next = mm_post("ffn_down", a, w[tag + "_w_down"], xs, g_post, 0.5, g_next)
    return out, h_next, dict(x=xs, h=h, gu=gu, a=a, y=y, w=w)


def _ffn_bwd(dxs, dy, sv, tag, g_pre, push, prev):
    w = sv["w"]
    gu_w, down_w = w[tag + "_w_gu"], w[tag + "_w_down"]
    dgu = ffn_down_dx("ffn_down_dx", dy, down_w, sv["gu"])
    grads = {tag + "_w_down": mm_dw("ffn_down_dw", sv["a"], dy, "row", down_w.shape),
             tag + "_w_gu": mm_dw("ffn_gu_dw", sv["h"], dgu, "col", gu_w.shape)}
    return mm_dx_norms("ffn_gu_dx", dgu, gu_w, sv["x"], g_pre, dxs, prev, push(grads))


def _mixer_fwd(xs, h, w, sm, g_post, g_next, rope, nb, s, mid):
    cos2, sin2, log_g = rope
    d = xs.shape[1]
    gate_blk = (BLK_GATE * d) // RET_V_DIM
    p = mm_fwd("mx_in", h, w["w_in"], "col", MXU_DTYPE)
    if mid is not None:
        sm = dict(sm, cf_dw_b=sm["cf_dw_b"] + mid(p))
    o = retention_fwd("ret_fwd", p, cos2, sin2, log_g, nb, s)
    ya_in = head_gate_fwd("ret_gate", o, p, gate_blk)
    yb_in, cz = short_conv_fwd("sc_fwd", p, BLK_SCB, sm["sc_conv_w"], nb)
    u1 = conformer_conv_fwd("cf_fwd", p, BLK_CFA, sm["cf_dw_w"], sm["cf_dw_b"], nb)
    yc_in = ln_silu_fwd("cf_ln", u1, sm["cf_ln_g"], sm["cf_ln_b"])
    ya = mm_fwd("mx_proj", ya_in, w["w_ret_o"], "row", F32)
    yb = mm_fwd("mx_proj", yb_in, w["w_sc_o"], "row", F32)
    yc = mm_fwd("mx_proj", yc_in, w["w_cf_o"], "row", F32)
    mg = merge_fwd("mx_merge", p, BLK_MERGE, ya, yb, yc)
    m, out, h_next = mm_post("mx_out", mg, w["w_o"], xs, g_post, 1.0, g_next)
    return out, h_next, dict(x=xs, h=h, p=p, o=o, ya_in=ya_in, yb_in=yb_in, cz=cz, u1=u1, yc_in=yc_in, ya=ya, yb=yb, yc=yc,
                     mg=mg, m=m, w=w)


def _mixer_bwd(dxs, dm, sv, sm, g_pre, rope, nb, s, push, prev):
    cos2, sin2, log_g = rope
    w, p = sv["w"], sv["p"]
    d = dxs.shape[1]
    gate_blk = (BLK_GATE * d) // RET_V_DIM
    grads, gsm = {}, {}

    def proj_bwd(wname, a_in, dy, out_dtype):
        grads[wname] = mm_dw("mx_proj_dw", a_in, dy, "row", w[wname].shape)
        return mm_dx("mx_proj_dx", dy, w[wname], "row", out_dtype)

    dmg = proj_bwd("w_o", sv["mg"], dm, MXU_DTYPE)
    dg0, dg1, dg2, dya, dyb, dyc = merge_bwd("mx_merge_bwd", p, BLK_MERGE, sv["ya"], sv["yb"], sv["yc"], dmg)
    dya_in = proj_bwd("w_ret_o", sv["ya_in"], dya, MXU_DTYPE)
    dyb_in = proj_bwd("w_sc_o", sv["yb_in"], dyb, MXU_DTYPE)
    dyc_in = proj_bwd("w_cf_o", sv["yc_in"], dyc, MXU_DTYPE)
    do, dgret = head_gate_bwd("ret_gate_bwd", sv["o"], p, gate_blk, dya_in)
    dq, dk, dv = retention_bwd("ret_bwd", p, cos2, sin2, log_g, do, nb, s)
    dscb, dscc, dscx, gsm["sc_conv_w"] = short_conv_bwd("sc_bwd", p, BLK_SCB, sm["sc_conv_w"], sv["cz"], dyb_in, nb)
    du1, dlg, dlb = ln_silu_bwd("cf_ln_bwd", sv["u1"], sm["cf_ln_g"], sm["cf_ln_b"], dyc_in)
    dcfa, dcfb, gsm["cf_dw_w"], dbias = conformer_conv_bwd("cf_bwd", p, BLK_CFA, sm["cf_dw_w"], du1, nb)
    gsm.update(cf_ln_g=dlg[0], cf_ln_b=dlb[0], cf_dw_b=dbias[0])
    dp = concat_cols("mx_dp", [dq, dk, dv, dgret, dscb, dscc, dscx, dcfa, dcfb, dg0, dg1, dg2])
    grads["w_in"] = mm_dw("mx_in_dw", sv["h"], dp, "col", w["w_in"].shape)
    return mm_dx_norms("mx_in_dx", dp, w["w_in"], sv["x"], g_pre, dxs, prev, push(grads)) + (gsm,)


def local_step(x, positions, target, small, fetch, push):
    nb, s, d = x.shape
    t = nb * s
    depth = small["norm_g"].shape[0]
    rope = _rope_tables(positions) + (_log_gamma(),)
    xs = x.reshape(t, d)
    token = [None]

    def gain(l, i):
        g = small["norm_g"][l, i][None, :]
        if token[0] is not None:
            g, token[0] = g + token[0], None
        return g

    def mixer_small(l):
        return dict(sc_conv_w=small["sc_conv_w"][l], cf_dw_w=small["cf_dw_w"][l], cf_dw_b=small["cf_dw_b"][l][None, :],
                    cf_ln_g=small["cf_ln_g"][l][None, :], cf_ln_b=small["cf_ln_b"][l][None, :])

    saved = {}
    order = [(l, blk) for l in range(depth) for blk in BLOCKS]
    h = None
    for at, (l, blk) in enumerate(order):
        w, token[0], mid = fetch(l, blk, xs)
        i0 = NORM_OF[blk]
        if h is None:
            h = rms_fwd("first_rms", xs, gain(l, i0))
        g_post = gain(l, i0 + 1)
        g_next = gain(order[at + 1][0], NORM_OF[order[at + 1][1]]) if at + 1 < len(order) else None
        if blk == "mixer":
            xs, h, saved[l, blk] = _mixer_fwd(xs, h, w, mixer_small(l), g_post, g_next, rope, nb, s, mid)
        else:
            xs, h, saved[l, blk] = _ffn_fwd(xs, h, w, blk, g_post, g_next)

    dxs, loss = loss_head("loss", xs, target.reshape(t, d))

    dnorm = [[None] * 6 for _ in range(depth)]
    gsmall = {n: [None] * depth for n in ("sc_conv_w", "cf_dw_w", "cf_dw_b", "cf_ln_g", "cf_ln_b")}
    def branch(group):
        l, blk = group
        sv = saved[group]
        return (sv["m"], gain(l, NORM_OF[blk] + 1), 1.0) if blk == "mixer" else (sv["y"], gain(l, NORM_OF[blk] + 1), 0.5)

    l, blk = order[-1]
    y, g_post, scale = branch(order[-1])
    dy, dnorm[l][NORM_OF[blk] + 1] = post_bwd("last_post_bwd", y, g_post, dxs, scale)
    for at in reversed(range(len(order))):
        l, blk = order[at]
        i0 = NORM_OF[blk]
        prev = branch(order[at - 1]) if at > 0 else None
        put = functools.partial(push, l, blk)
        if blk == "mixer":
            dxs, dnorm[l][i0], dy, dg_prev, gsm = _mixer_bwd(
                dxs, dy, saved[l, blk], mixer_small(l), gain(l, i0), rope, nb, s, put, prev)
            for n, v in gsm.items():
                gsmall[n][l] = v
        else:
            dxs, dnorm[l][i0], dy, dg_prev = _ffn_bwd(dxs, dy, saved[l, blk], blk, gain(l, i0), put, prev)
        if at > 0:
            dnorm[order[at - 1][0]][NORM_OF[order[at - 1][1]] + 1] = dg_prev

    gs = {n: jnp.stack(v) for n, v in gsmall.items()}
    gs["norm_g"] = jnp.stack([jnp.concatenate(r, axis=0) for r in dnorm])
    return loss, dxs.reshape(nb, s, d), gs


ANY = pl.BlockSpec(memory_space=pl.ANY)
HBM = pl.BlockSpec(memory_space=pltpu.HBM)
SEM = pl.BlockSpec(memory_space=pltpu.SEMAPHORE)
VMEM_WHOLE = pl.BlockSpec(memory_space=pltpu.VMEM)
EFFECT = pltpu.SideEffectType.DATAFLOW_SIDE_EFFECTING
TOKEN = jax.ShapeDtypeStruct((8, 128), F32)


def _other_chips(x, y):
    return [(1 - x, y), (x, 1 - y), (1 - x, 1 - y)]


def _remote(src, dst, send_sem, recv_sem, to):
    return pltpu.make_async_remote_copy(src_ref=src, dst_ref=dst, send_sem=send_sem, recv_sem=recv_sem,
                                        device_id=to, device_id_type=MESH)


def _in_hbm(v):
    return pltpu.with_memory_space_constraint(v, pltpu.HBM)


def place_quarters(ws, layer, ids, after):
    m = len(ws)

    def body(ids_ref, *refs):
        for w_ref, o_ref in zip(refs[:m], refs[m + 1:]):
            o_ref[...] = w_ref[...].astype(o_ref.dtype)

    def spec(w, where):
        return pl.BlockSpec((None, w.shape[1] // STREAM_STEPS, w.shape[2]), where)

    return pl.pallas_call(
        body, name="place_quarters",
        grid_spec=pltpu.PrefetchScalarGridSpec(
            num_scalar_prefetch=1, grid=(STREAM_STEPS,),
            in_specs=[spec(w, lambda i, ids_ref: (layer, i, 0)) for w in ws] + [ANY],
            out_specs=[spec(w, lambda i, ids_ref: (ids_ref[0], i, 0)) for w in ws]),
        out_shape=[jax.ShapeDtypeStruct((N_CHIP,) + w.shape[1:], MXU_DTYPE) for w in ws],
        compiler_params=_params(("parallel",)),
    )(ids, *ws, after)


def _gather_copies(lands, send, recv):
    x, y, c = _axes()
    me = 2 * x + y
    mine, theirs = [], []
    for a, ld in enumerate(lands):
        rh = ld.shape[1] // 2
        rows = pl.ds(c * rh, rh)
        for k, (px, py) in enumerate(_other_chips(x, y)):
            to = (px, py, c)
            mine.append(_remote(ld.at[me, rows, :], ld.at[me, rows, :], send.at[3 * a + k], recv.at[3 * a + k], to))
            got = ld.at[2 * px + py, rows, :]
            theirs.append(_remote(got, got, send.at[3 * a + k], recv.at[3 * a + k], to))
    return mine, theirs


def gather_start(name, groups, after):
    flat = [s for g in groups for s in g]
    n, ng = len(flat), len(groups)
    sizes = [len(g) for g in groups]

    def body(*refs):
        lands = refs[:n]
        sems = refs[n + 1:n + 1 + 2 * ng]
        token = refs[-1]
        at = 0
        for g, m in enumerate(sizes):
            mine, _ = _gather_copies(lands[at:at + m], sems[2 * g], sems[2 * g + 1])
            for cp in mine:
                cp.start()
            at += m
        token[...] = jnp.zeros_like(token)

    sem_shapes = []
    for m in sizes:
        sem_shapes += [pltpu.SemaphoreType.DMA((3 * m,))] * 2
    res = pl.pallas_call(
        body, name=name, in_specs=[HBM] * n + [ANY],
        out_specs=[SEM] * (2 * ng) + [HBM] * n + [VMEM_WHOLE],
        out_shape=sem_shapes + [pltpu.HBM(s.shape, s.dtype) for s in flat] + [TOKEN],
        input_output_aliases={i: 2 * ng + i for i in range(n)},
        compiler_params=pltpu.CompilerParams(has_side_effects=EFFECT),
    )(*[_in_hbm(s) for s in flat], after)
    sems, thru, token = res[:2 * ng], res[2 * ng:2 * ng + n], res[-1]
    out, at = [], 0
    for g, m in enumerate(sizes):
        out.append((sems[2 * g], sems[2 * g + 1], thru[at:at + m]))
        at += m
    return out, token


def gather_wait(lands, send, recv, after):
    m = len(lands)

    def body(*refs):
        mine, theirs = _gather_copies(refs[:m], refs[m], refs[m + 1])
        for cp in mine:
            cp.wait_send()
        for cp in theirs:
            cp.wait_recv()

    return pl.pallas_call(
        body, name="gather_wait", in_specs=[HBM] * m + [SEM, SEM, ANY], out_specs=[HBM] * m,
        out_shape=[pltpu.HBM(l.shape, l.dtype) for l in lands],
        input_output_aliases={i: i for i in range(m)},
        compiler_params=pltpu.CompilerParams(has_side_effects=EFFECT),
    )(*lands, send, recv, after)


def copy_start(name, families, after=()):
    sizes = [len(f[0]) for f in families]
    n, k, nf = sum(sizes), len(after), len(families)

    def body(*refs):
        at = 0
        for f, (_, copies, _) in enumerate(families):
            for cp in copies(refs[at:at + sizes[f]], refs[n + k + 2 * f], refs[n + k + 2 * f + 1])[0]:
                cp.start()
            at += sizes[f]
        refs[-1][...] = jnp.zeros_like(refs[-1])

    flat = [b for f in families for b in f[0]]
    sems = [pltpu.SemaphoreType.DMA((f[2],)) for f in families for _ in range(2)]
    res = pl.pallas_call(
        body, name=name, in_specs=[HBM] * n + [ANY] * k, out_specs=[SEM] * (2 * nf) + [HBM] * n + [VMEM_WHOLE],
        out_shape=sems + [pltpu.HBM(b.shape, b.dtype) for b in flat] + [TOKEN],
        input_output_aliases={i: 2 * nf + i for i in range(n)},
        compiler_params=pltpu.CompilerParams(has_side_effects=EFFECT),
    )(*[_in_hbm(b) for b in flat], *after)
    out, at = [], 2 * nf
    for f in range(nf):
        out.append((res[2 * f], res[2 * f + 1], list(res[at:at + sizes[f]])))
        at += sizes[f]
    return out, res[-1]


def copy_wait(name, bufs, send, recv, copies, after=()):
    n = len(bufs)

    def body(*refs):
        mine, theirs = copies(refs[:n], refs[n], refs[n + 1])
        for cp in mine:
            cp.wait_send()
        for cp in theirs:
            cp.wait_recv()

    return list(pl.pallas_call(
        body, name=name, in_specs=[HBM] * n + [SEM, SEM] + [ANY] * len(after), out_specs=[HBM] * n,
        out_shape=[pltpu.HBM(b.shape, b.dtype) for b in bufs], input_output_aliases={i: i for i in range(n)},
        compiler_params=pltpu.CompilerParams(has_side_effects=EFFECT),
    )(*bufs, send, recv, *after))


def _fill_copies(lands, send, recv):
    x, y, c = _axes()
    sib = (x, y, 1 - c)
    mine, theirs = [], []
    for a, ld in enumerate(lands):
        rh = ld.shape[1] // 2
        for k, (px, py) in enumerate(_other_chips(x, y)):
            got = ld.at[2 * px + py, pl.ds(c * rh, rh), :]
            mine.append(_remote(got, got, send.at[3 * a + k], recv.at[3 * a + k], sib))
            blk = ld.at[2 * px + py, pl.ds((1 - c) * rh, rh), :]
            theirs.append(_remote(blk, blk, send.at[3 * a + k], recv.at[3 * a + k], sib))
    return mine, theirs


def _presum_copies(grads, lands, send, recv):
    x, y, c = _axes()
    cps = []
    for a, (g, ld) in enumerate(zip(grads, lands)):
        rh = g.shape[1] // 2
        cps.append(_remote(g.at[:, pl.ds((1 - c) * rh, rh), :], ld, send.at[a], recv.at[a], (x, y, 1 - c)))
    return cps


def presum_wait(grads, lands, send, recv, after):
    m = len(grads)

    def body(*refs):
        for cp in _presum_copies(refs[:m], refs[m:2 * m], refs[2 * m], refs[2 * m + 1]):
            cp.wait_send()
            cp.wait_recv()

    res = pl.pallas_call(
        body, name="presum_wait", in_specs=[HBM] * (2 * m) + [SEM, SEM] + [ANY] * len(after),
        out_specs=[HBM] * (2 * m),
        out_shape=[pltpu.HBM(g.shape, g.dtype) for g in grads] + [pltpu.HBM(l.shape, l.dtype) for l in lands],
        input_output_aliases={i: i for i in range(2 * m)},
        compiler_params=pltpu.CompilerParams(has_side_effects=EFFECT),
    )(*grads, *lands, send, recv, *after)
    return res[:m], res[m:]


def add_halves(gs, lands, ids):
    m = len(gs)

    def body(ids_ref, *refs):
        for a_ref, b_ref, o_ref in zip(refs[:m], refs[m:2 * m], refs[2 * m:]):
            o_ref[...] = (a_ref[...].astype(F32) + b_ref[...].astype(F32)).astype(o_ref.dtype)

    def spec(ld, where):
        return pl.BlockSpec((None,) + ld.shape[1:], where)

    return pl.pallas_call(
        body, name="add_halves",
        grid_spec=pltpu.PrefetchScalarGridSpec(
            num_scalar_prefetch=1, grid=(N_CHIP,),
            in_specs=[spec(ld, lambda i, ids_ref: (i, ids_ref[1], 0)) for ld in lands]
            + [spec(ld, lambda i, ids_ref: (i, 0, 0)) for ld in lands],
            out_specs=[spec(ld, lambda i, ids_ref: (i, 0, 0)) for ld in lands]),
        out_shape=[jax.ShapeDtypeStruct(ld.shape, ld.dtype) for ld in lands],
        compiler_params=_params(("parallel",)),
    )(ids, *gs, *lands)


def _scatter_copies(parts, lands, send, recv):
    x, y, c = _axes()
    cps = []
    for a, (pt, ld) in enumerate(zip(parts, lands)):
        for k, (px, py) in enumerate(_other_chips(x, y)):
            cps.append(_remote(pt.at[2 * px + py], ld.at[k], send.at[3 * a + k], recv.at[3 * a + k], (px, py, c)))
    return cps


def scatter_wait(parts, lands, send, recv, after):
    m = len(parts)

    def body(*refs):
        for cp in _scatter_copies(refs[:m], refs[m:2 * m], refs[2 * m], refs[2 * m + 1]):
            cp.wait_send()
            cp.wait_recv()

    res = pl.pallas_call(
        body, name="scatter_wait", in_specs=[HBM] * (2 * m) + [SEM, SEM] + [ANY] * len(after),
        out_specs=[HBM] * (2 * m),
        out_shape=[pltpu.HBM(p.shape, p.dtype) for p in parts] + [pltpu.HBM(l.shape, l.dtype) for l in lands],
        input_output_aliases={i: i for i in range(2 * m)},
        compiler_params=pltpu.CompilerParams(has_side_effects=EFFECT),
    )(*parts, *lands, send, recv, *after)
    return res[:m], res[m:]


def sum_partials(parts, lands, ids, layer, depth, intos):
    m = len(parts)
    nt = STREAM_STEPS

    def body(ids_ref, *refs):
        for p_ref, l_ref, o_ref in zip(refs[:m], refs[m:2 * m], refs[-m:]):
            acc = p_ref[...].astype(F32)
            for k in range(N_CHIP - 1):
                acc = acc + l_ref[k].astype(F32)
            o_ref[...] = acc

    def rows(p):
        return p.shape[1] // nt

    in_specs = [pl.BlockSpec((None, rows(p), p.shape[2]), lambda i, ids_ref: (ids_ref[0], i, 0)) for p in parts]
    in_specs += [pl.BlockSpec((N_CHIP - 1, rows(p), p.shape[2]), lambda i, ids_ref: (0, i, 0)) for p in parts]
    args = [ids, *parts, *lands]
    aliases = {}
    if intos is not None:
        in_specs += [ANY] * m
        args += list(intos)
        aliases = {1 + 2 * m + a: a for a in range(m)}
    return pl.pallas_call(
        body, name="sum_partials",
        grid_spec=pltpu.PrefetchScalarGridSpec(
            num_scalar_prefetch=1, grid=(nt,), in_specs=in_specs,
            out_specs=[pl.BlockSpec((None, rows(p), p.shape[2]), lambda i, ids_ref: (layer, ids_ref[1] * nt + i, 0))
                       for p in parts]),
        out_shape=[jax.ShapeDtypeStruct((depth, 2 * p.shape[1], p.shape[2]), F32) for p in parts],
        input_output_aliases=aliases, compiler_params=_params(("parallel",)),
    )(*args)


def _final_copies(layer):
    def copies(bufs, send, recv):
        x, y, c = _axes()
        sib = (x, y, 1 - c)
        mine, theirs = [], []
        for a, buf in enumerate(bufs):
            rh = buf.shape[1] // 2
            src = buf.at[layer, pl.ds(c * rh, rh), :]
            mine.append(_remote(src, src, send.at[a], recv.at[a], sib))
            dst = buf.at[layer, pl.ds((1 - c) * rh, rh), :]
            theirs.append(_remote(dst, dst, send.at[a], recv.at[a], sib))
        return mine, theirs

    return copies


def allgather_small(pk):
    def body(in_ref, out_ref, send, recv):
        x, y, c = _axes()
        me = 2 * x + y
        chips = _other_chips(x, y)
        out_ref[pl.ds(me, 1)] = in_ref[...][None]
        cps = []
        for k, (px, py) in enumerate(chips):
            cp = _remote(in_ref, out_ref.at[me], send.at[k], recv.at[k], (px, py, c))
            cp.start()
            cps.append(cp)
        for k, (px, py) in enumerate(chips):
            got = out_ref.at[2 * px + py]
            _remote(got, got, send.at[k], recv.at[k], (px, py, c)).wait_recv()
        for cp in cps:
            cp.wait_send()

    return pl.pallas_call(
        body, name="allgather_small", in_specs=[VMEM_WHOLE], out_specs=VMEM_WHOLE,
        out_shape=jax.ShapeDtypeStruct((N_CHIP,) + pk.shape, pk.dtype),
        scratch_shapes=[pltpu.SemaphoreType.DMA((3,))] * 2,
    )(pk)


N_DEV = 8


def _small_copies(bufs, send, recv):
    g, slots = bufs
    x, y, c = _axes()
    me = 4 * x + 2 * y + c
    mine, theirs = [], []
    for mask in range(1, N_DEV):
        px = 1 - x if mask & 4 else x
        py = 1 - y if mask & 2 else y
        pc = 1 - c if mask & 1 else c
        mine.append(_remote(g, slots.at[me], send.at[mask - 1], recv.at[mask - 1], (px, py, pc)))
        got = slots.at[4 * px + 2 * py + pc]
        theirs.append(_remote(got, got, send.at[mask - 1], recv.at[mask - 1], (px, py, pc)))
    return mine, theirs


def sum_slots(g, slots, me):
    def body(me_ref, g_ref, slots_ref, o_ref):
        acc = None
        for d in range(N_DEV):
            term = jnp.where(me_ref[0] == d, g_ref[...], slots_ref[d])
            acc = term if acc is None else acc + term
        o_ref[...] = acc

    return pl.pallas_call(
        body, name="sum_slots",
        grid_spec=pltpu.PrefetchScalarGridSpec(
            num_scalar_prefetch=1, grid=(1,),
            in_specs=[pl.BlockSpec(g.shape, lambda i, me_ref: (0, 0)),
                      pl.BlockSpec(slots.shape, lambda i, me_ref: (0, 0, 0))],
            out_specs=pl.BlockSpec(g.shape, lambda i, me_ref: (0, 0))),
        out_shape=jax.ShapeDtypeStruct(g.shape, g.dtype),
        compiler_params=_params(("arbitrary",)),
    )(me, g, slots)


def adamw(w, g, m, v, layer=None, intos=None):
    shape = w.shape
    cols = shape[-1]
    rows = int(np.prod(shape[:-1]))
    span = rows if layer is None else rows // shape[0]
    tr = span
    for cand in (256, 128):
        if span % cand == 0 and cand * cols * 4 <= 2 * 1024 * 1024:
            tr = cand
            break
    first = 0 if layer is None else layer * (span // tr)
    c1 = 1.0 - ADAM_B1 ** ADAM_STEP
    c2 = 1.0 - ADAM_B2 ** ADAM_STEP

    def body(w_ref, g_ref, m_ref, v_ref, *rest):
        d_ref, nm_ref, nv_ref, g_out = rest[-4:]
        gv = g_ref[...]
        g_out[...] = gv
        nm = ADAM_B1 * m_ref[...] + (1.0 - ADAM_B1) * gv
        nv = ADAM_B2 * v_ref[...] + (1.0 - ADAM_B2) * jnp.square(gv)
        d_ref[...] = -ADAM_LR * ((nm / c1) / (jnp.sqrt(nv / c2) + ADAM_EPS) + ADAM_WD * w_ref[...])
        nm_ref[...] = nm
        nv_ref[...] = nv

    spec = pl.BlockSpec((tr, cols), lambda i: (first + i, 0))
    args = [a.reshape(rows, cols) for a in (w, g, m, v)]
    in_specs, aliases = [spec] * 4, {}
    if intos is not None:
        args += [a.reshape(rows, cols) for a in intos]
        in_specs += [ANY] * 4
        aliases = {4 + k: k for k in range(4)}
    res = pl.pallas_call(
        body, name="adamw", grid=(span // tr,), in_specs=in_specs, out_specs=[spec] * 4,
        out_shape=[jax.ShapeDtypeStruct((rows, cols), F32)] * 4, input_output_aliases=aliases,
        compiler_params=_params(("parallel",)),
    )(*args)
    return [r.reshape(shape) for r in res]


WEIGHTS = ("norm_g", "ffn1_w_gu", "ffn1_w_down", "w_in", "w_ret_o", "sc_conv_w", "w_sc_o", "cf_dw_w", "cf_dw_b",
           "cf_ln_g", "cf_ln_b", "w_cf_o", "w_o", "ffn2_w_gu", "ffn2_w_down")
SHARDED_SMALL = ("norm_g", "sc_conv_w", "cf_dw_w")
REPLICATED_SMALL = ("cf_dw_b", "cf_ln_g", "cf_ln_b")

def _pack_rows(parts):
    padded, offs, at = [], [], 0
    for p in parts:
        r = -(-p.shape[0] // SUBLANES) * SUBLANES
        padded.append(jnp.pad(p, ((0, r - p.shape[0]), (0, 0))))
        offs.append(at)
        at += r
    return jnp.concatenate(padded, axis=0), offs


def kernel(x, positions, norm_g, ffn1_w_gu, ffn1_w_down, w_in, w_ret_o, sc_conv_w, w_sc_o, cf_dw_w, cf_dw_b, cf_ln_g, cf_ln_b, w_cf_o, w_o, ffn2_w_gu, ffn2_w_down, loss_target, m_norm_g, m_ffn1_w_gu, m_ffn1_w_down, m_w_in, m_w_ret_o, m_sc_conv_w, m_w_sc_o, m_cf_dw_w, m_cf_dw_b, m_cf_ln_g, m_cf_ln_b, m_w_cf_o, m_w_o, m_ffn2_w_gu, m_ffn2_w_down, v_norm_g, v_ffn1_w_gu, v_ffn1_w_down, v_w_in, v_w_ret_o, v_sc_conv_w, v_w_sc_o, v_cf_dw_w, v_cf_dw_b, v_cf_ln_g, v_cf_ln_b, v_w_cf_o, v_w_o, v_ffn2_w_gu, v_ffn2_w_down):
    wts = dict(zip(WEIGHTS, (norm_g, ffn1_w_gu, ffn1_w_down, w_in, w_ret_o, sc_conv_w, w_sc_o, cf_dw_w, cf_dw_b,
                             cf_ln_g, cf_ln_b, w_cf_o, w_o, ffn2_w_gu, ffn2_w_down)))
    mom = dict(zip(WEIGHTS, (m_norm_g, m_ffn1_w_gu, m_ffn1_w_down, m_w_in, m_w_ret_o, m_sc_conv_w, m_w_sc_o,
                             m_cf_dw_w, m_cf_dw_b, m_cf_ln_g, m_cf_ln_b, m_w_cf_o, m_w_o, m_ffn2_w_gu, m_ffn2_w_down)))
    var = dict(zip(WEIGHTS, (v_norm_g, v_ffn1_w_gu, v_ffn1_w_down, v_w_in, v_w_ret_o, v_sc_conv_w, v_w_sc_o,
                             v_cf_dw_w, v_cf_dw_b, v_cf_ln_g, v_cf_ln_b, v_w_cf_o, v_w_o, v_ffn2_w_gu, v_ffn2_w_down)))
    depth = norm_g.shape[0]
    dq = norm_g.shape[-1]
    d = N_CHIP * dq
    chip = 2 * lax.axis_index("x") + lax.axis_index("y")
    ids = jnp.stack([chip, lax.axis_index("c")]).astype(jnp.int32)

    pk, offs = _pack_rows([wts[n].reshape(-1, dq) for n in SHARDED_SMALL])
    gk4 = allgather_small(pk)
    gk = gk4.transpose(1, 0, 2).reshape(pk.shape[0], d)
    small = {n: wts[n] for n in REPLICATED_SMALL}
    for n, o in zip(SHARDED_SMALL, offs):
        rows = wts[n].shape[0] * wts[n].shape[1]
        small[n] = gk[o:o + rows].reshape(wts[n].shape[:2] + (d,))

    order = [(l, blk) for l in range(depth) for blk in BLOCKS]
    def placed(groups, after):
        return [place_quarters([wts[n] for n in BLOCK_WEIGHTS[blk]], l, ids, after) for l, blk in groups]

    first, token = gather_start("gather_start_first", placed(order[:1], gk4), gk4)
    rest, token = gather_start("gather_start_rest", placed(order[1:], token), token)
    started = dict(zip(order, first + rest))
    small["norm_g"] = small["norm_g"] + token[0:1, 0:1]

    filling = {}

    def fill(group, after):
        send, recv, lands = started[group]
        lands = gather_wait(lands, send, recv, after)
        started_fill, tok = copy_start("fill_start", [(lands, _fill_copies, 3 * len(lands))])
        filling[group] = started_fill[0]
        return tok[0:1, 0:1]

    def fetch(l, blk, after):
        at = order.index((l, blk))
        if (l, blk) not in filling:
            fill((l, blk), token if at == 0 else after)
        send, recv, lands = filling.pop((l, blk))
        lands = copy_wait("fill_wait", lands, send, recv, _fill_copies, (after,))
        tok, mid = None, None
        if at == 1:
            mid = functools.partial(fill, order[at + 1])
        elif 1 < at < len(order) - 1:
            tok = fill(order[at + 1], lands[0])
        return dict(zip(BLOCK_WEIGHTS[blk], lands)), tok, mid

    gsum = {n: None for n in BIG}
    presums, scatters, finals = [], [], []

    def scatter_ready(after):
        group, gl, lands, send, recv = presums.pop(0)
        gl, lands = presum_wait(gl, lands, send, recv, after)
        parts = list(add_halves(gl, lands, ids))
        m = len(parts)
        lands = [lax.empty((N_CHIP - 1,) + p.shape[1:], p.dtype) for p in parts]
        family = (parts + lands, lambda refs, sd, rv: (_scatter_copies(refs[:m], refs[m:], sd, rv),) * 2, 3 * m)
        return family, lambda sd, rv, bufs: scatters.append((group, bufs[:m], bufs[m:], sd, rv))

    def final_ready(after):
        (l, blk), parts, lands, send, recv = scatters.pop(0)
        parts, lands = scatter_wait(parts, lands, send, recv, after)
        names = BLOCK_WEIGHTS[blk]
        intos = None if gsum[names[0]] is None else [gsum[n] for n in names]
        sums = list(sum_partials(parts, lands, ids, l, depth, intos))

        def note(sd, rv, bufs):
            gsum.update(zip(names, bufs))
            finals.append((names, l, sd, rv))

        return (sums, _final_copies(l), len(sums)), note

    def start_all(name, ready, after=()):
        started, tok = copy_start(name, [family for family, _ in ready], after)
        for (_, note), (sd, rv, bufs) in zip(ready, started):
            note(sd, rv, bufs)
        return tok

    def scatter_next(after):
        return start_all("scatter_start", [scatter_ready(after)])

    def sum_next(after):
        return start_all("final_start", [final_ready(after)])

    def final_next(after):
        names, l, send, recv = finals.pop(0)
        gsum.update(zip(names, copy_wait("final_wait", [gsum[n] for n in names], send, recv, _final_copies(l), after)))

    def push(l, blk, grads):
        gl = [grads[n] for n in BLOCK_WEIGHTS[blk]]
        m = len(gl)
        lands = [lax.empty((g.shape[0], g.shape[1] // 2, g.shape[2]), g.dtype) for g in gl]
        ready = [((gl + lands, lambda refs, sd, rv: (_presum_copies(refs[:m], refs[m:], sd, rv),) * 2, m),
                  lambda sd, rv, bufs: presums.append(((l, blk), bufs[:m], bufs[m:], sd, rv)))]
        if scatters:
            ready.append(final_ready((gl[0],)))
        if presums:
            ready.append(scatter_ready((gl[0],)))
        return start_all("push_start", ready)[0:1, 0:1]

    loss, grad_x, gs = local_step(x, positions, loss_target, small, fetch, push)

    names = SHARDED_SMALL + REPLICATED_SMALL
    pg, offs = _pack_rows([gs[n].reshape(-1, d) for n in names])
    small_bufs = [pg, lax.empty((N_DEV,) + pg.shape, pg.dtype)]
    ((s_send, s_recv, s_bufs),), tok = copy_start("small_start", [(small_bufs, _small_copies, N_DEV - 1)], (grad_x,))
    tok = scatter_next((grad_x, tok))

    delta, new_m, new_v, grads = {}, {}, {}, {}

    def update(n, layer=None):
        g = gsum[n] if n in BIG else grads[n]
        prev = [delta[n], new_m[n], new_v[n], grads[n]] if layer is not None and n in delta else None
        delta[n], new_m[n], new_v[n], grads[n] = adamw(wts[n], g, mom[n], var[n], layer, prev)

    while finals and finals[0][1] > 0:
        done, l = finals[0][:2]
        final_next((tok,))
        for n in done:
            update(n, l)
    upper = tuple(delta[n] for n in BIG if n in delta)
    pg, slots = copy_wait("small_wait", s_bufs, s_send, s_recv, _small_copies, upper + (tok,))
    me = (2 * chip + lax.axis_index("c")).astype(jnp.int32).reshape(1)
    tot = sum_slots(pg, slots, me)
    for n, o in zip(names, offs):
        rows = int(np.prod(gs[n].shape[:-1]))
        full = tot[o:o + rows]
        if n in SHARDED_SMALL:
            full = lax.dynamic_slice_in_dim(full, chip * dq, dq, axis=1)
        grads[n] = full.reshape(wts[n].shape)

    for n in names:
        update(n)
    after = tuple(delta[n] for n in names)
    while scatters or finals:
        if scatters:
            after = (sum_next(after),)
        done, l = finals[0][:2]
        final_next(after)
        for n in done:
            update(n, l)
        after = tuple(delta[n] for n in done)

    loss_all = lax.psum(loss[0, 0], ("x", "y", "c"))
    return (loss_all, grad_x, *[grads[n] for n in WEIGHTS], *[delta[n] for n in WEIGHTS],
            *[new_m[n] for n in WEIGHTS], *[new_v[n] for n in WEIGHTS])
```

```python
import functools

import jax
import jax.numpy as jnp
import numpy as np
from jax import lax
from jax.experimental import pallas as pl
from jax.experimental.pallas import tpu as pltpu

F32 = jnp.float32
BF16 = jnp.bfloat16
MXU_DTYPE = BF16
VMEM_LIMIT_BYTES = 56 * 1024 * 1024
MESH = pl.DeviceIdType.MESH

N_CHIP = 4
CHUNK = 64
RET_HEADS = 4
RET_QK_DIM = 128
RET_V_DIM = 256
SC_KERNEL = 3
CF_KERNEL = 31
ROPE_BASE = 10000.0
NORM_EPS = 1e-6
LN_EPS = 1e-5
ADAM_LR = 0.001
ADAM_B1 = 0.9
ADAM_B2 = 0.999
ADAM_EPS = 1e-08
ADAM_WD = 0.01
ADAM_STEP = 10

SUBLANES = 8
CONV_PAD = 32
CONV_TS = 256
CONV_TC = 512
CONV_ROWS = 32
CONV_TILES = range(0, CONV_ROWS, SUBLANES)
SC_TS = 512


def _conv_scratch(ts):
    return [pltpu.VMEM((ts + CONV_PAD, CONV_TC), F32),
            pltpu.VMEM((SUBLANES - 1, ts + CONV_PAD - SUBLANES, CONV_TC), F32)]
RET_TQ = 512
MM_TM = 1024
MM_TN = 1536
MM_K1 = 1024
MM_W1 = 8 << 20
MM_SLICE = 1024
MM_IN_BYTES = 36 << 20
STREAM_STEPS = 2


def _params(sem):
    return pltpu.CompilerParams(dimension_semantics=sem, vmem_limit_bytes=VMEM_LIMIT_BYTES)


def _axes():
    return lax.axis_index("x"), lax.axis_index("y"), lax.axis_index("c")


NN = (((1,), (0,)), ((), ()))
NT = (((1,), (1,)), ((), ()))
TN = (((0,), (0,)), ((), ()))


def _mm(name, a, b, out_shape, out_dtype, grid, a_spec, b_spec, o_spec, dims, acc_shape):
    nk = grid[2]

    def body(a_ref, b_ref, o_ref, *scratch):
        bv = b_ref[...]
        if bv.ndim == 3:
            bv = bv.reshape(-1, bv.shape[-1])
        part = lax.dot_general(a_ref[...], bv, dims, preferred_element_type=F32)

        def put(v):
            o_ref[...] = v.reshape(o_ref.shape).astype(o_ref.dtype)

        if nk == 1:
            put(part)
        else:
            acc = scratch[0]
            k = pl.program_id(2)

            @pl.when(k == 0)
            def _():
                acc[...] = part

            @pl.when(k > 0)
            def _():
                acc[...] += part

            @pl.when(k == nk - 1)
            def _():
                put(acc[...])

    scratch = [pltpu.VMEM(acc_shape, F32)] if nk > 1 else []
    return pl.pallas_call(
        body, name=name, grid=grid, in_specs=[a_spec, b_spec], out_specs=o_spec,
        out_shape=jax.ShapeDtypeStruct(out_shape, out_dtype), scratch_shapes=scratch,
        compiler_params=_params(("parallel", "parallel", "arbitrary")),
    )(a, b)


def _tile(n, target):
    best = None
    for t in range(128, min(n, target) + 1, 128):
        if n % t == 0:
            best = t
    assert best is not None, (n, target)
    return best


def _token_rows(t, width):
    tt = t
    while tt > MM_TM and tt * width * jnp.dtype(MXU_DTYPE).itemsize * 2 > MM_IN_BYTES:
        tt //= 2
    return tt


def mm_fwd(name, a, w4, mode, out_dtype):
    t = a.shape[0]
    _, r, c = w4.shape
    tm = min(t, MM_TM)
    if mode == "col":
        tn = _tile(c, MM_TN)
        npj = c // tn
        grid = (t // tm, N_CHIP * npj, 1)
        a_spec = pl.BlockSpec((tm, r), lambda i, j, k: (i, 0))
        b_spec = pl.BlockSpec((None, r, tn), lambda i, j, k: (j // npj, 0, j % npj))
        o_spec = pl.BlockSpec((tm, tn), lambda i, j, k: (i, j))
        return _mm(name, a, w4, (t, N_CHIP * c), out_dtype, grid, a_spec, b_spec, o_spec, NN, (tm, tn))
    if w4.size * w4.dtype.itemsize <= MM_W1:
        grid = (t // tm, 1, 1)
        a_spec = pl.BlockSpec((tm, N_CHIP * r), lambda i, j, k: (i, 0))
        b_spec = pl.BlockSpec((N_CHIP, r, c), lambda i, j, k: (0, 0, 0))
        o_spec = pl.BlockSpec((tm, c), lambda i, j, k: (i, 0))
        return _mm(name, a, w4, (t, c), out_dtype, grid, a_spec, b_spec, o_spec, NN, (tm, c))
    grid = (t // tm, 1, N_CHIP)
    a_spec = pl.BlockSpec((tm, r), lambda i, j, k: (i, k))
    b_spec = pl.BlockSpec((None, r, c), lambda i, j, k: (k, 0, 0))
    o_spec = pl.BlockSpec((tm, c), lambda i, j, k: (i, 0))
    return _mm(name, a, w4, (t, c), out_dtype, grid, a_spec, b_spec, o_spec, NN, (tm, c))


def mm_dx(name, dy, w4, mode, out_dtype):
    t = dy.shape[-2]
    _, r, c = w4.shape
    tm = min(t, MM_TM)
    if mode == "col":
        tn, npj = c, 1
        hb = N_CHIP // 2 * npj
        grid = (t // tm, 1, N_CHIP * npj)
        if dy.ndim == 3:
            a_spec = pl.BlockSpec((None, tm, tn), lambda i, j, k: (k // hb, i, k % hb))
        else:
            a_spec = pl.BlockSpec((tm, tn), lambda i, j, k: (i, k))
        b_spec = pl.BlockSpec((None, r, tn), lambda i, j, k: (k // npj, 0, k % npj))
        o_spec = pl.BlockSpec((tm, r), lambda i, j, k: (i, 0))
        return _mm(name, dy, w4, (t, r), out_dtype, grid, a_spec, b_spec, o_spec, NT, (tm, r))
    if N_CHIP * r <= MM_K1:
        grid = (t // tm, 1, 1)
        a_spec = pl.BlockSpec((tm, c), lambda i, j, k: (i, 0))
        b_spec = pl.BlockSpec((N_CHIP, r, c), lambda i, j, k: (0, 0, 0))
        o_spec = pl.BlockSpec((tm, N_CHIP * r), lambda i, j, k: (i, 0))
        return _mm(name, dy, w4, (t, N_CHIP * r), out_dtype, grid, a_spec, b_spec, o_spec, NT, (tm, N_CHIP * r))
    grid = (t // tm, N_CHIP, 1)
    a_spec = pl.BlockSpec((tm, c), lambda i, j, k: (i, 0))
    b_spec = pl.BlockSpec((None, r, c), lambda i, j, k: (j, 0, 0))
    o_spec = pl.BlockSpec((tm, r), lambda i, j, k: (i, j))
    return _mm(name, dy, w4, (t, N_CHIP * r), out_dtype, grid, a_spec, b_spec, o_spec, NT, (tm, r))


def mm_dw(name, a, dy, mode, shape3):
    t = a.shape[0]
    _, r, c = shape3
    if mode == "col":
        tn = _tile(c, MM_TN)
        npj = c // tn
        tt = _token_rows(t, r + tn)
        grid = (1, N_CHIP * npj, t // tt)
        a_spec = pl.BlockSpec((tt, r), lambda i, j, k: (k, 0))
        hb = N_CHIP // 2 * npj
        if dy.ndim == 3:
            b_spec = pl.BlockSpec((None, tt, tn), lambda i, j, k: (j // hb, k, j % hb))
        else:
            b_spec = pl.BlockSpec((tt, tn), lambda i, j, k: (k, j))
        o_spec = pl.BlockSpec((None, r, tn), lambda i, j, k: (j // npj, 0, j % npj))
        return _mm(name, a, dy, shape3, MXU_DTYPE, grid, a_spec, b_spec, o_spec, TN, (r, tn))
    if N_CHIP * r <= MM_K1:
        tt = min(_token_rows(t, N_CHIP * r + c), max(t // 4, MM_TM))
        grid = (1, 1, t // tt)
        a_spec = pl.BlockSpec((tt, N_CHIP * r), lambda i, j, k: (k, 0))
        b_spec = pl.BlockSpec((tt, c), lambda i, j, k: (k, 0))
        o_spec = pl.BlockSpec((N_CHIP, r, c), lambda i, j, k: (0, 0, 0))
        return _mm(name, a, dy, shape3, MXU_DTYPE, grid, a_spec, b_spec, o_spec, TN, (N_CHIP * r, c))
    tt = _token_rows(t, r + c)
    grid = (N_CHIP, 1, t // tt)
    a_spec = pl.BlockSpec((tt, r), lambda i, j, k: (k, i))
    b_spec = pl.BlockSpec((tt, c), lambda i, j, k: (k, 0))
    o_spec = pl.BlockSpec((None, r, c), lambda i, j, k: (i, 0, 0))
    return _mm(name, a, dy, shape3, MXU_DTYPE, grid, a_spec, b_spec, o_spec, TN, (r, c))


def _rms_bwd(x, g, dh):
    r = lax.rsqrt(jnp.mean(x * x, axis=-1, keepdims=True) + NORM_EPS)
    xhat = x * r
    dyg = dh * g
    dx = r * (dyg - xhat * jnp.mean(dyg * xhat, axis=-1, keepdims=True))
    return dx, jnp.sum(dh * xhat, axis=0, keepdims=True)


def mm_dx_norms(name, dy, w4, x, g_pre, dres, prev, after):
    t = dy.shape[-2]
    _, r, c = w4.shape
    tm = min(t, MM_TM // 2)
    nt, nk = t // tm, N_CHIP
    hb = N_CHIP // 2
    chained = prev is not None

    def body(dy_ref, w_ref, x_ref, dres_ref, g_ref, *rest):
        rest = rest[1:] if after is not None else rest
        if chained:
            y_ref, gp_ref, dx_ref, dg_ref, dyp_ref, dgp_ref, acc = rest
        else:
            dx_ref, dg_ref, acc = rest
        i, k = pl.program_id(0), pl.program_id(1)
        part = lax.dot_general(dy_ref[...], w_ref[...], NT, preferred_element_type=F32)

        @pl.when(k == 0)
        def _():
            acc[...] = part

        @pl.when(k > 0)
        def _():
            acc[...] += part

        def add_to(ref, v):
            @pl.when(i == 0)
            def _():
                ref[...] = v

            @pl.when(i > 0)
            def _():
                ref[...] += v

        @pl.when(k == nk - 1)
        def _():
            dx, dg = _rms_bwd(x_ref[...], g_ref[...], acc[...])
            dxs = dres_ref[...] + dx
            dx_ref[...] = dxs
            add_to(dg_ref, dg)
            if chained:
                dyp, dgp = _rms_bwd(y_ref[...], gp_ref[...], dxs)
                dyp_ref[...] = (prev[2] * dyp).astype(dyp_ref.dtype)
                add_to(dgp_ref, prev[2] * dgp)

    if dy.ndim == 3:
        dy_spec = pl.BlockSpec((None, tm, c), lambda i, k: (k // hb, i, k % hb))
    else:
        dy_spec = pl.BlockSpec((tm, c), lambda i, k: (i, k))
    rows = pl.BlockSpec((tm, r), lambda i, k: (i, 0))
    gain = pl.BlockSpec((1, r), lambda i, k: (0, 0))
    in_specs = [dy_spec, pl.BlockSpec((None, r, c), lambda i, k: (k, 0, 0)), rows, rows, gain]
    args = [dy, w4, x, dres, g_pre]
    if after is not None:
        in_specs.append(pl.BlockSpec(memory_space=pl.ANY))
        args.append(after)
    out_specs = [rows, gain]
    out_shape = [jax.ShapeDtypeStruct((t, r), F32), jax.ShapeDtypeStruct((1, r), F32)]
    if chained:
        in_specs += [rows, gain]
        args += [prev[0], prev[1]]
        out_specs += [rows, gain]
        out_shape += [jax.ShapeDtypeStruct((t, r), MXU_DTYPE), jax.ShapeDtypeStruct((1, r), F32)]
    res = pl.pallas_call(
        body, name=name, grid=(nt, nk), in_specs=in_specs, out_specs=out_specs, out_shape=out_shape,
        scratch_shapes=[pltpu.VMEM((tm, r), F32)], compiler_params=_params(("arbitrary", "arbitrary")),
    )(*args)
    return tuple(res) if chained else (res[0], res[1], None, None)


def _rowwise(name, fn, rows, pars, outs, accs=(), tm=256, ncol=1):
    t = rows[0][0].shape[0]
    nrow, npar, nout = len(rows), len(pars), len(outs)

    def body(*refs):
        vals = [r[...] for r in refs[:nrow + npar]]
        res = fn(*vals)
        out_refs = refs[nrow + npar:nrow + npar + nout]
        acc_refs = refs[nrow + npar + nout:]
        for o, v in zip(out_refs, res[:nout]):
            o[...] = v.astype(o.dtype)
        i = pl.program_id(1)
        for a, v in zip(acc_refs, res[nout:]):
            @pl.when(i == 0)
            def _(a=a, v=v):
                a[...] = v.astype(F32)

            @pl.when(i > 0)
            def _(a=a, v=v):
                a[...] += v.astype(F32)

    in_specs = [pl.BlockSpec((tm, w), functools.partial(lambda j, i, b: (i, b + j), b=b)) for _, w, b in rows]
    for arr, w in pars:
        if w is None:
            in_specs.append(pl.BlockSpec(arr.shape, lambda j, i: (0, 0)))
        else:
            in_specs.append(pl.BlockSpec((1, w), lambda j, i: (0, j)))
    out_specs = [pl.BlockSpec((tm, w), lambda j, i: (i, j)) for _, w, _ in outs]
    out_specs += [pl.BlockSpec((1, w), lambda j, i: (0, j)) for _, w in accs]
    out_shape = [jax.ShapeDtypeStruct((t, tw), dt) for tw, _, dt in outs]
    out_shape += [jax.ShapeDtypeStruct((1, tw), F32) for tw, _ in accs]
    res = pl.pallas_call(
        body, name=name, grid=(ncol, t // tm), in_specs=in_specs, out_specs=out_specs, out_shape=out_shape,
        compiler_params=_params(("parallel", "arbitrary" if accs else "parallel")),
    )(*[r[0] for r in rows], *[p[0] for p in pars])
    return res


def _rms(x, g):
    xf = x.astype(F32)
    return xf * lax.rsqrt(jnp.mean(xf * xf, axis=-1, keepdims=True) + NORM_EPS) * g


def _silu(x):
    return x * jax.nn.sigmoid(x)


def rms_fwd(name, x, g):
    d = x.shape[1]
    return _rowwise(name, lambda x, g: (_rms(x, g),), [(x, d, 0)], [(g, None)], [(d, d, MXU_DTYPE)], tm=512)[0]


def rms_bwd(name, x, g, dh, dres):
    d = x.shape[1]

    def fn(x, dh, dres, g):
        _, vjp = jax.vjp(_rms, x, g)
        dx, dg = vjp(dh.astype(F32))
        return dres + dx, dg

    return _rowwise(name, fn, [(x, d, 0), (dh, d, 0), (dres, d, 0)], [(g, None)], [(d, d, F32)], [(d, d)], tm=256)


def mm_post(name, a, w4, x, g_post, scale, g_next):
    t = a.shape[0]
    _, r, c = w4.shape
    tm = min(t, MM_TM // 2)
    chained = g_next is not None

    def body(a_ref, w_ref, x_ref, gp_ref, *rest):
        gn_ref, y_ref, xn_ref, h_ref = rest if chained else (None,) + rest + (None,)
        y = lax.dot_general(a_ref[...], w_ref[...].reshape(N_CHIP * r, c), NN, preferred_element_type=F32)
        y_ref[...] = y
        xn = x_ref[...] + scale * _rms(y, gp_ref[...])
        xn_ref[...] = xn
        if chained:
            h_ref[...] = _rms(xn, gn_ref[...]).astype(h_ref.dtype)

    def rows(width):
        return pl.BlockSpec((tm, width), lambda i: (i, 0))

    gain = pl.BlockSpec((1, c), lambda i: (0, 0))
    in_specs = [rows(N_CHIP * r), pl.BlockSpec((N_CHIP, r, c), lambda i: (0, 0, 0)), rows(c), gain]
    args = [a, w4, x, g_post]
    out_specs, out_shape = [rows(c), rows(c)], [jax.ShapeDtypeStruct((t, c), F32)] * 2
    if chained:
        in_specs.append(gain)
        args.append(g_next)
        out_specs.append(rows(c))
        out_shape.append(jax.ShapeDtypeStruct((t, c), MXU_DTYPE))
    res = pl.pallas_call(
        body, name=name, grid=(t // tm,), in_specs=in_specs, out_specs=out_specs, out_shape=out_shape,
        compiler_params=_params(("parallel",)),
    )(*args)
    return res[0], res[1], (res[2] if chained else None)


def post_bwd(name, y, g, dx, scale):
    d = y.shape[1]

    def fn(y, dx, g):
        _, vjp = jax.vjp(lambda y, g: scale * _rms(y, g), y, g)
        return vjp(dx)

    return _rowwise(name, fn, [(y, d, 0), (dx, d, 0)], [(g, None)], [(d, d, MXU_DTYPE)], [(d, d)], tm=256)


def ffn_up(name, h, w4):
    t = h.shape[0]
    _, r, c = w4.shape
    tm = min(t, MM_TM)
    tn = _tile(c, MM_TM)
    npj = c // tn
    half = N_CHIP // 2

    def body(h_ref, wg_ref, wu_ref, gu_ref, a_ref):
        hv = h_ref[...]
        g = lax.dot_general(hv, wg_ref[...], NN, preferred_element_type=F32)
        u = lax.dot_general(hv, wu_ref[...], NN, preferred_element_type=F32)
        sg = jax.nn.sigmoid(g)
        silu = g * sg
        gu_ref[0] = (u * (sg + silu * (1.0 - sg))).astype(gu_ref.dtype)
        gu_ref[1] = silu.astype(gu_ref.dtype)
        a_ref[...] = (silu * u).astype(a_ref.dtype)

    f = half * c
    return pl.pallas_call(
        body, name=name, grid=(t // tm, half * npj),
        in_specs=[pl.BlockSpec((tm, r), lambda i, j: (i, 0)),
                  pl.BlockSpec((None, r, tn), lambda i, j: (j // npj, 0, j % npj)),
                  pl.BlockSpec((None, r, tn), lambda i, j: (half + j // npj, 0, j % npj))],
        out_specs=[pl.BlockSpec((2, tm, tn), lambda i, j: (0, i, j)), pl.BlockSpec((tm, tn), lambda i, j: (i, j))],
        out_shape=[jax.ShapeDtypeStruct((2, t, f), MXU_DTYPE), jax.ShapeDtypeStruct((t, f), MXU_DTYPE)],
        compiler_params=_params(("parallel", "parallel")),
    )(h, w4, w4)


def ffn_down_dx(name, dy, w4, gu):
    t = dy.shape[0]
    _, r, c = w4.shape
    tm = min(t, MM_TM)

    def body(dy_ref, w_ref, gu_ref, o_ref):
        dyv = dy_ref[...]
        for n0 in range(0, r, MM_SLICE):
            cols = pl.ds(n0, MM_SLICE)
            da = lax.dot_general(dyv, w_ref[cols, :], NT, preferred_element_type=F32)
            o_ref[0, :, cols] = (da * gu_ref[0, :, cols].astype(F32)).astype(o_ref.dtype)
            o_ref[1, :, cols] = (da * gu_ref[1, :, cols].astype(F32)).astype(o_ref.dtype)

    return pl.pallas_call(
        body, name=name, grid=(t // tm, N_CHIP),
        in_specs=[pl.BlockSpec((tm, c), lambda i, j: (i, 0)), pl.BlockSpec((None, r, c), lambda i, j: (j, 0, 0)),
                  pl.BlockSpec((2, tm, r), lambda i, j: (0, i, j))],
        out_specs=pl.BlockSpec((2, tm, r), lambda i, j: (0, i, j)),
        out_shape=jax.ShapeDtypeStruct((2, t, N_CHIP * r), MXU_DTYPE),
        compiler_params=_params(("parallel", "parallel")),
    )(dy, w4, gu)


def _head_gate(o, g):
    mu = jnp.mean(o, axis=-1, keepdims=True)
    var = jnp.mean(jnp.square(o - mu), axis=-1, keepdims=True)
    return _silu(g.astype(F32)) * ((o - mu) * lax.rsqrt(var + LN_EPS))


def head_gate_fwd(name, o, p, gate_blk):
    dv = RET_V_DIM
    return _rowwise(name, lambda o, g: (_head_gate(o, g),), [(o, dv, 0), (p, dv, gate_blk)], [],
                    [(RET_HEADS * dv, dv, MXU_DTYPE)], tm=min(o.shape[0], 2048), ncol=RET_HEADS)[0]


def head_gate_bwd(name, o, p, gate_blk, da):
    dv = RET_V_DIM

    def fn(o, g, da):
        _, vjp = jax.vjp(_head_gate, o, g.astype(F32))
        return vjp(da.astype(F32))

    w = RET_HEADS * dv
    return _rowwise(name, fn, [(o, dv, 0), (p, dv, gate_blk), (da, dv, 0)], [],
                    [(w, dv, MXU_DTYPE), (w, dv, MXU_DTYPE)], tm=min(o.shape[0], 2048), ncol=RET_HEADS)


def _ln_silu(u, g, b):
    mu = jnp.mean(u, axis=-1, keepdims=True)
    var = jnp.mean(jnp.square(u - mu), axis=-1, keepdims=True)
    return _silu((u - mu) * lax.rsqrt(var + LN_EPS) * g + b)


def ln_silu_fwd(name, u, g, b):
    d = u.shape[1]
    return _rowwise(name, lambda u, g, b: (_ln_silu(u, g, b),), [(u, d, 0)], [(g, None), (b, None)],
                    [(d, d, MXU_DTYPE)], tm=512)[0]


def ln_silu_bwd(name, u, g, b, dc):
    d = u.shape[1]

    def fn(u, dc, g, b):
        _, vjp = jax.vjp(_ln_silu, u, g, b)
        return vjp(dc.astype(F32))

    return _rowwise(name, fn, [(u, d, 0), (dc, d, 0)], [(g, None), (b, None)], [(d, d, F32)], [(d, d), (d, d)],
                    tm=512)


def _merge(g0, g1, g2, ya, yb, yc):
    s = jax.nn.sigmoid
    return s(g0.astype(F32)) * ya + s(g1.astype(F32)) * yb + s(g2.astype(F32)) * yc


def merge_fwd(name, p, blk, ya, yb, yc):
    d = ya.shape[1]
    rows = [(p, d, blk), (p, d, blk + 1), (p, d, blk + 2), (ya, d, 0), (yb, d, 0), (yc, d, 0)]
    return _rowwise(name, lambda *v: (_merge(*v),), rows, [], [(d, d, MXU_DTYPE)], tm=512)[0]


def merge_bwd(name, p, blk, ya, yb, yc, dmg):
    d = ya.shape[1]

    def fn(g0, g1, g2, ya, yb, yc, dmg):
        _, vjp = jax.vjp(_merge, g0.astype(F32), g1.astype(F32), g2.astype(F32), ya, yb, yc)
        return vjp(dmg.astype(F32))

    rows = [(p, d, blk), (p, d, blk + 1), (p, d, blk + 2), (ya, d, 0), (yb, d, 0), (yc, d, 0), (dmg, d, 0)]
    return _rowwise(name, fn, rows, [], [(d, d, MXU_DTYPE)] * 6, tm=256)


def concat_cols(name, pieces):
    t = pieces[0].shape[0]
    widths = [p.shape[1] for p in pieces]
    tm = 256

    def body(*refs):
        o_ref, at = refs[-1], 0
        for r, w in zip(refs[:-1], widths):
            o_ref[:, at:at + w] = r[...]
            at += w

    return pl.pallas_call(
        body, name=name, grid=(t // tm,),
        in_specs=[pl.BlockSpec((tm, w), lambda i: (i, 0)) for w in widths],
        out_specs=pl.BlockSpec((tm, sum(widths)), lambda i: (i, 0)),
        out_shape=jax.ShapeDtypeStruct((t, sum(widths)), pieces[0].dtype),
        compiler_params=_params(("parallel",)),
    )(*pieces)


def loss_head(name, y, target):
    t, d = y.shape
    tm = 512

    def body(y_ref, t_ref, dy_ref, loss_ref):
        err = y_ref[...] - t_ref[...]
        dy_ref[...] = err * (1.0 / d)
        part = jnp.sum(jnp.sum(err * err, axis=1, keepdims=True), axis=0, keepdims=True) * (0.5 / d)

        @pl.when(pl.program_id(0) == 0)
        def _():
            loss_ref[...] = part

        @pl.when(pl.program_id(0) > 0)
        def _():
            loss_ref[...] += part

    return pl.pallas_call(
        body, name=name, grid=(t // tm,),
        in_specs=[pl.BlockSpec((tm, d), lambda i: (i, 0))] * 2,
        out_specs=[pl.BlockSpec((tm, d), lambda i: (i, 0)), pl.BlockSpec((1, 1), lambda i: (0, 0))],
        out_shape=[jax.ShapeDtypeStruct((t, d), F32), jax.ShapeDtypeStruct((1, 1), F32)],
        compiler_params=_params(("arbitrary",)),
    )(y, target)


def _rot(x, cos2, sin2):
    return x * cos2 + pltpu.roll(x, RET_QK_DIM // 2, 1) * sin2


def _decay_mask(lg, n0, rows, cols):
    n = n0 + lax.broadcasted_iota(jnp.int32, (rows, cols), 0)
    m = lax.broadcasted_iota(jnp.int32, (rows, cols), 1)
    shift = CHUNK.bit_length() - 1
    dist = jnp.abs(n - m).astype(F32)
    return jnp.where((m >> shift) <= (n >> shift), jnp.exp(lg * dist), 0.0)


def _ret_specs(s):
    dk, dv, h = RET_QK_DIM, RET_V_DIM, RET_HEADS
    return [
        pl.BlockSpec((s, dk), lambda b, hh: (b, hh)),
        pl.BlockSpec((s, dk), lambda b, hh: (b, h + hh)),
        pl.BlockSpec((s, dv), lambda b, hh: (b, (2 * h * dk) // dv + hh)),
        pl.BlockSpec((s, dk), lambda b, hh: (b, 0)),
        pl.BlockSpec((s, dk), lambda b, hh: (b, 0)),
        pl.BlockSpec((None, 1, dk), lambda b, hh: (hh, 0, 0)),
    ]


def retention_fwd(name, p, cos2, sin2, log_g, nb, s):
    dk, dv, h = RET_QK_DIM, RET_V_DIM, RET_HEADS

    def body(q_ref, k_ref, v_ref, cos_ref, sin_ref, lg_ref, o_ref, kr_ref):
        lg = lg_ref[0:1, 0:1]
        kr = _rot(k_ref[...].astype(F32), cos_ref[...], sin_ref[...]) * (dk ** -0.5)
        kr_ref[...] = kr.astype(kr_ref.dtype)
        for qi in range(s // RET_TQ):
            n0, kmax = qi * RET_TQ, (qi + 1) * RET_TQ
            rows = pl.ds(n0, RET_TQ)
            qr = _rot(q_ref[rows, :].astype(F32), cos_ref[rows, :], sin_ref[rows, :]).astype(MXU_DTYPE)
            sc = lax.dot_general(qr, kr_ref[0:kmax, :], NT, preferred_element_type=F32)
            pm = (sc * _decay_mask(lg, n0, RET_TQ, kmax)).astype(MXU_DTYPE)
            o_ref[rows, :] = lax.dot_general(pm, v_ref[0:kmax, :], NN, preferred_element_type=F32)

    return pl.pallas_call(
        body, name=name, grid=(nb, h), in_specs=_ret_specs(s),
        out_specs=pl.BlockSpec((s, dv), lambda b, hh: (b, hh)),
        out_shape=jax.ShapeDtypeStruct((nb * s, h * dv), F32),
        scratch_shapes=[pltpu.VMEM((s, dk), MXU_DTYPE)],
        compiler_params=_params(("parallel", "parallel")),
    )(p, p, p, cos2, sin2, log_g)


def retention_bwd(name, p, cos2, sin2, log_g, do, nb, s):
    dk, dv, h = RET_QK_DIM, RET_V_DIM, RET_HEADS

    def body(q_ref, k_ref, v_ref, cos_ref, sin_ref, lg_ref, do_ref, dq_ref, dk_ref, dv_ref, kr_ref, dk_acc, dv_acc):
        lg = lg_ref[0:1, 0:1]
        kr = _rot(k_ref[...].astype(F32), cos_ref[...], sin_ref[...]) * (dk ** -0.5)
        kr_ref[...] = kr.astype(kr_ref.dtype)
        dk_acc[...] = jnp.zeros_like(dk_acc)
        dv_acc[...] = jnp.zeros_like(dv_acc)
        for qi in range(s // RET_TQ):
            n0, kmax = qi * RET_TQ, (qi + 1) * RET_TQ
            rows = pl.ds(n0, RET_TQ)
            cq, sq = cos_ref[rows, :], sin_ref[rows, :]
            qr = _rot(q_ref[rows, :].astype(F32), cq, sq).astype(MXU_DTYPE)
            dob = do_ref[rows, :]
            mask = _decay_mask(lg, n0, RET_TQ, kmax)
            sc = lax.dot_general(qr, kr_ref[0:kmax, :], NT, preferred_element_type=F32)
            pm = (sc * mask).astype(MXU_DTYPE)
            dv_acc[0:kmax, :] += lax.dot_general(pm, dob, TN, preferred_element_type=F32)
            dp = lax.dot_general(dob, v_ref[0:kmax, :], NT, preferred_element_type=F32)
            ds = (dp * mask).astype(MXU_DTYPE)
            dqr = lax.dot_general(ds, kr_ref[0:kmax, :], NN, preferred_element_type=F32)
            dq_ref[rows, :] = _rot(dqr, cq, -sq).astype(dq_ref.dtype)
            dk_acc[0:kmax, :] += lax.dot_general(ds, qr, TN, preferred_element_type=F32)
        dkr = dk_acc[...] * (dk ** -0.5)
        dk_ref[...] = _rot(dkr, cos_ref[...], -sin_ref[...]).astype(dk_ref.dtype)
        dv_ref[...] = dv_acc[...].astype(dv_ref.dtype)

    t = nb * s
    return pl.pallas_call(
        body, name=name, grid=(nb, h),
        in_specs=_ret_specs(s) + [pl.BlockSpec((s, dv), lambda b, hh: (b, hh))],
        out_specs=[pl.BlockSpec((s, dk), lambda b, hh: (b, hh)), pl.BlockSpec((s, dk), lambda b, hh: (b, hh)),
                   pl.BlockSpec((s, dv), lambda b, hh: (b, hh))],
        out_shape=[jax.ShapeDtypeStruct((t, h * dk), MXU_DTYPE), jax.ShapeDtypeStruct((t, h * dk), MXU_DTYPE),
                   jax.ShapeDtypeStruct((t, h * dv), MXU_DTYPE)],
        scratch_shapes=[pltpu.VMEM((s, dk), MXU_DTYPE), pltpu.VMEM((s, dk), F32), pltpu.VMEM((s, dv), F32)],
        compiler_params=_params(("parallel", "parallel")),
    )(p, p, p, cos2, sin2, log_g, do)


def _conv_grid(t, d, nb, ts):
    s = t // nb
    ns, nc = s // ts, d // CONV_TC
    return s, ns, nc


def _shifted(pad_ref, sh_ref, offsets):
    n = sh_ref.shape[1]
    for b in sorted({off % SUBLANES for off in offsets} - {0}):
        sh_ref[b - 1] = pad_ref[pl.ds(b, n), :]

    def read(off, r0):
        a, b = off - off % SUBLANES + r0, off % SUBLANES
        return pad_ref[pl.ds(a, SUBLANES), :] if b == 0 else sh_ref[b - 1, pl.ds(a, SUBLANES), :]

    return read


def _causal_taps(pad_ref, sh_ref, w_ref, k, emit):
    offs = [CONV_PAD - (k - 1) + j for j in range(k)]
    read = _shifted(pad_ref, sh_ref, offs)
    for r0 in range(0, pad_ref.shape[0] - CONV_PAD, CONV_ROWS):
        accs = [None] * len(CONV_TILES)
        for j in range(k):
            wj = w_ref[j]
            for q, dr in enumerate(CONV_TILES):
                term = wj * read(offs[j], r0 + dr)
                accs[q] = term if accs[q] is None else accs[q] + term
        emit(r0, jnp.concatenate(accs, axis=0))


def _tap_tiles(w):
    return jnp.broadcast_to(w[:, None, :], (w.shape[0], SUBLANES, w.shape[1]))


def _tap_spec(k):
    return pl.BlockSpec((k, SUBLANES, CONV_TC), lambda c, b, si: (0, 0, c))


def _carry_past(pad_ref, s_idx):
    ts = pad_ref.shape[0] - CONV_PAD

    @pl.when(s_idx == 0)
    def _():
        pad_ref[0:CONV_PAD, :] = jnp.zeros((CONV_PAD, pad_ref.shape[1]), F32)

    @pl.when(s_idx > 0)
    def _():
        pad_ref[0:CONV_PAD, :] = pad_ref[ts:ts + CONV_PAD, :]


def _carry_future(pad_ref, s_idx):
    ts = pad_ref.shape[0] - CONV_PAD

    @pl.when(s_idx == 0)
    def _():
        pad_ref[ts:ts + CONV_PAD, :] = jnp.zeros((CONV_PAD, pad_ref.shape[1]), F32)

    @pl.when(s_idx > 0)
    def _():
        pad_ref[ts:ts + CONV_PAD, :] = pad_ref[0:CONV_PAD, :]


def _conv_bwd_taps(pad_ref, sh_ref, w_ref, dw_acc, k, x_rows, emit, mix):
    read = _shifted(pad_ref, sh_ref, range(k))
    for r0 in range(0, pad_ref.shape[0] - CONV_PAD, CONV_ROWS):
        ops = x_rows(r0)
        x = mix(ops)
        accs = [None] * len(CONV_TILES)
        for j in range(k):
            wj, dwj = w_ref[j], None
            for q, dr in enumerate(CONV_TILES):
                sh = read(k - 1 - j, r0 + dr)
                term = wj * sh
                accs[q] = term if accs[q] is None else accs[q] + term
                prod = x[dr:dr + SUBLANES] * sh
                dwj = prod if dwj is None else dwj + prod
            dw_acc[j] += dwj
        emit(r0, ops, jnp.concatenate(accs, axis=0))


def _conv_bwd_edges(dw_acc, dw_ref, nb, ns, extra=()):
    first = jnp.logical_and(pl.program_id(1) == 0, pl.program_id(2) == 0)
    last = jnp.logical_and(pl.program_id(1) == nb - 1, pl.program_id(2) == ns - 1)

    @pl.when(first)
    def _():
        dw_acc[...] = jnp.zeros_like(dw_acc)
        for r in extra:
            r[...] = jnp.zeros_like(r)

    def finish():
        @pl.when(last)
        def _():
            dw_ref[...] = jnp.sum(dw_acc[...], axis=1)

    return finish


def short_conv_fwd(name, p, blk_b, w, nb):
    t = p.shape[0]
    d = w.shape[1]
    ts = SC_TS
    s, ns, nc = _conv_grid(t, d, nb, ts)
    cb = d // CONV_TC

    def body(b_ref, c_ref, x_ref, w_ref, y_ref, cz_ref, pad_ref, sh_ref):
        _carry_past(pad_ref, pl.program_id(2))
        pad_ref[CONV_PAD:CONV_PAD + ts, :] = c_ref[...].astype(F32) * x_ref[...].astype(F32)

        def emit(r0, cz):
            rows = pl.ds(r0, CONV_ROWS)
            cz_ref[rows, :] = cz
            y_ref[rows, :] = (b_ref[rows, :].astype(F32) * cz).astype(y_ref.dtype)

        _causal_taps(pad_ref, sh_ref, w_ref, SC_KERNEL, emit)

    def pspec(off):
        return pl.BlockSpec((ts, CONV_TC), lambda c, b, si: (b * ns + si, (blk_b + off) * cb + c))

    ospec = pl.BlockSpec((ts, CONV_TC), lambda c, b, si: (b * ns + si, c))
    return pl.pallas_call(
        body, name=name, grid=(nc, nb, ns),
        in_specs=[pspec(0), pspec(1), pspec(2), _tap_spec(SC_KERNEL)],
        out_specs=[ospec, ospec],
        out_shape=[jax.ShapeDtypeStruct((t, d), MXU_DTYPE), jax.ShapeDtypeStruct((t, d), F32)],
        scratch_shapes=_conv_scratch(ts),
        compiler_params=_params(("parallel", "arbitrary", "arbitrary")),
    )(p, p, p, _tap_tiles(w))


def short_conv_bwd(name, p, blk_b, w, cz, dy, nb):
    t = p.shape[0]
    d = w.shape[1]
    ts = SC_TS
    s, ns, nc = _conv_grid(t, d, nb, ts)
    cb = d // CONV_TC

    def body(b_ref, c_ref, x_ref, w_ref, cz_ref, dy_ref, db_ref, dc_ref, dx_ref, dw_ref, pad_ref, sh_ref, dw_acc):
        _carry_future(pad_ref, pl.program_id(2))
        dyv = dy_ref[...].astype(F32)
        db_ref[...] = (dyv * cz_ref[...]).astype(db_ref.dtype)
        pad_ref[0:ts, :] = dyv * b_ref[...].astype(F32)
        finish = _conv_bwd_edges(dw_acc, dw_ref, nb, ns)

        def x_rows(r0):
            rows = pl.ds(r0, CONV_ROWS)
            return c_ref[rows, :].astype(F32), x_ref[rows, :].astype(F32)

        def emit(r0, cx, dz):
            rows = pl.ds(r0, CONV_ROWS)
            dc_ref[rows, :] = (dz * cx[1]).astype(dc_ref.dtype)
            dx_ref[rows, :] = (dz * cx[0]).astype(dx_ref.dtype)

        _conv_bwd_taps(pad_ref, sh_ref, w_ref, dw_acc, SC_KERNEL, x_rows, emit, lambda cx: cx[0] * cx[1])
        finish()

    def row(b, si):
        return b * ns + (ns - 1 - si)

    def pspec(off):
        return pl.BlockSpec((ts, CONV_TC), lambda c, b, si: (row(b, si), (blk_b + off) * cb + c))

    ospec = pl.BlockSpec((ts, CONV_TC), lambda c, b, si: (row(b, si), c))
    wspec = pl.BlockSpec((SC_KERNEL, CONV_TC), lambda c, b, si: (0, c))
    return pl.pallas_call(
        body, name=name, grid=(nc, nb, ns),
        in_specs=[pspec(0), pspec(1), pspec(2), _tap_spec(SC_KERNEL), ospec, ospec],
        out_specs=[ospec, ospec, ospec, wspec],
        out_shape=[jax.ShapeDtypeStruct((t, d), MXU_DTYPE)] * 3 + [jax.ShapeDtypeStruct((SC_KERNEL, d), F32)],
        scratch_shapes=_conv_scratch(ts) + [pltpu.VMEM((SC_KERNEL, SUBLANES, CONV_TC), F32)],
        compiler_params=_params(("parallel", "arbitrary", "arbitrary")),
    )(p, p, p, _tap_tiles(w), cz, dy)


def conformer_conv_fwd(name, p, blk_a, w, bias, nb):
    t = p.shape[0]
    d = w.shape[1]
    ts = CONV_TS
    s, ns, nc = _conv_grid(t, d, nb, ts)
    cb = d // CONV_TC

    def body(a_ref, b_ref, w_ref, bias_ref, u_ref, pad_ref, sh_ref):
        _carry_past(pad_ref, pl.program_id(2))
        pad_ref[CONV_PAD:CONV_PAD + ts, :] = a_ref[...].astype(F32) * jax.nn.sigmoid(b_ref[...].astype(F32))

        def emit(r0, u):
            u_ref[pl.ds(r0, CONV_ROWS), :] = u + bias_ref[0:1, :]

        _causal_taps(pad_ref, sh_ref, w_ref, CF_KERNEL, emit)

    def pspec(off):
        return pl.BlockSpec((ts, CONV_TC), lambda c, b, si: (b * ns + si, (blk_a + off) * cb + c))

    return pl.pallas_call(
        body, name=name, grid=(nc, nb, ns),
        in_specs=[pspec(0), pspec(1), _tap_spec(CF_KERNEL), pl.BlockSpec((SUBLANES, CONV_TC), lambda c, b, si: (0, c))],
        out_specs=pl.BlockSpec((ts, CONV_TC), lambda c, b, si: (b * ns + si, c)),
        out_shape=jax.ShapeDtypeStruct((t, d), F32),
        scratch_shapes=_conv_scratch(ts),
        compiler_params=_params(("parallel", "arbitrary", "arbitrary")),
    )(p, p, _tap_tiles(w), jnp.broadcast_to(bias, (SUBLANES, d)))


def conformer_conv_bwd(name, p, blk_a, w, du, nb):
    t = p.shape[0]
    d = w.shape[1]
    ts = CONV_TS
    s, ns, nc = _conv_grid(t, d, nb, ts)
    cb = d // CONV_TC

    def body(a_ref, b_ref, w_ref, du_ref, da_ref, db_ref, dw_ref, dbias_ref, pad_ref, sh_ref, dw_acc):
        _carry_future(pad_ref, pl.program_id(2))
        duv = du_ref[...]
        pad_ref[0:ts, :] = duv
        finish = _conv_bwd_edges(dw_acc, dw_ref, nb, ns, extra=(dbias_ref,))
        dbias_ref[...] += jnp.sum(duv, axis=0, keepdims=True)

        def x_rows(r0):
            rows = pl.ds(r0, CONV_ROWS)
            return a_ref[rows, :].astype(F32), jax.nn.sigmoid(b_ref[rows, :].astype(F32))

        def emit(r0, asg, du0):
            rows = pl.ds(r0, CONV_ROWS)
            av, sg = asg
            da_ref[rows, :] = (du0 * sg).astype(da_ref.dtype)
            db_ref[rows, :] = (du0 * av * sg * (1.0 - sg)).astype(db_ref.dtype)

        _conv_bwd_taps(pad_ref, sh_ref, w_ref, dw_acc, CF_KERNEL, x_rows, emit, lambda asg: asg[0] * asg[1])
        finish()

    def row(b, si):
        return b * ns + (ns - 1 - si)

    def pspec(off):
        return pl.BlockSpec((ts, CONV_TC), lambda c, b, si: (row(b, si), (blk_a + off) * cb + c))

    ospec = pl.BlockSpec((ts, CONV_TC), lambda c, b, si: (row(b, si), c))
    wspec = pl.BlockSpec((CF_KERNEL, CONV_TC), lambda c, b, si: (0, c))
    bspec = pl.BlockSpec((1, CONV_TC), lambda c, b, si: (0, c))
    return pl.pallas_call(
        body, name=name, grid=(nc, nb, ns),
        in_specs=[pspec(0), pspec(1), _tap_spec(CF_KERNEL), ospec],
        out_specs=[ospec, ospec, wspec, bspec],
        out_shape=[jax.ShapeDtypeStruct((t, d), MXU_DTYPE)] * 2
        + [jax.ShapeDtypeStruct((CF_KERNEL, d), F32), jax.ShapeDtypeStruct((1, d), F32)],
        scratch_shapes=_conv_scratch(ts) + [pltpu.VMEM((CF_KERNEL, SUBLANES, CONV_TC), F32)],
        compiler_params=_params(("parallel", "arbitrary", "arbitrary")),
    )(p, p, _tap_tiles(w), du)


BLOCKS = ("ffn1", "mixer", "ffn2")
BLOCK_WEIGHTS = {"ffn1": ("ffn1_w_gu", "ffn1_w_down"), "mixer": ("w_in", "w_ret_o", "w_sc_o", "w_cf_o", "w_o"),
                 "ffn2": ("ffn2_w_gu", "ffn2_w_down")}
BIG = BLOCK_WEIGHTS["ffn1"] + BLOCK_WEIGHTS["mixer"] + BLOCK_WEIGHTS["ffn2"]
MODE = {"ffn1_w_gu": "col", "ffn1_w_down": "row", "w_in": "col", "w_ret_o": "row", "w_sc_o": "row",
        "w_cf_o": "row", "w_o": "row", "ffn2_w_gu": "col", "ffn2_w_down": "row"}
NORM_OF = {"ffn1": 0, "mixer": 2, "ffn2": 4}
BLK_GATE, BLK_SCB, BLK_CFA, BLK_MERGE = 2, 3, 6, 8


def _rope_tables(positions):
    half = RET_QK_DIM // 2
    inv_freq = ROPE_BASE ** (-jnp.arange(half, dtype=F32) / half)
    ang = positions.astype(F32)[..., None] * inv_freq
    cos, sin = jnp.cos(ang), jnp.sin(ang)
    nb, s = positions.shape
    cos2 = jnp.concatenate([cos, cos], axis=-1).reshape(nb * s, RET_QK_DIM)
    sin2 = jnp.concatenate([-sin, sin], axis=-1).reshape(nb * s, RET_QK_DIM)
    return cos2, sin2


def _log_gamma():
    lg = jnp.log(1.0 - 2.0 ** (-5.0 - jnp.arange(RET_HEADS, dtype=F32)))
    return jnp.broadcast_to(lg[:, None, None], (RET_HEADS, 1, RET_QK_DIM))


def _ffn_fwd(xs, h, w, tag, g_post, g_next):
    gu, a = ffn_up("ffn_up", h, w[tag + "_w_gu"])
    y, out, h_next = mm_post("ffn_down", a, w[tag + "_w_down"], xs, g_post, 0.5, g_next)
    return out, h_next, dict(x=xs, h=h, gu=gu, a=a, y=y, w=w)


def _ffn_bwd(dxs, dy, sv, tag, g_pre, push, prev):
    w = sv["w"]
    gu_w, down_w = w[tag + "_w_gu"], w[tag + "_w_down"]
    dgu = ffn_down_dx("ffn_down_dx", dy, down_w, sv["gu"])
    grads = {tag + "_w_down": mm_dw("ffn_down_dw", sv["a"], dy, "row", down_w.shape),
             tag + "_w_gu": mm_dw("ffn_gu_dw", sv["h"], dgu, "col", gu_w.shape)}
    return mm_dx_norms("ffn_gu_dx", dgu, gu_w, sv["x"], g_pre, dxs, prev, push(grads))


def _mixer_fwd(xs, h, w, sm, g_post, g_next, rope, nb, s, mid):
    cos2, sin2, log_g = rope
    d = xs.shape[1]
    gate_blk = (BLK_GATE * d) // RET_V_DIM
    p = mm_fwd("mx_in", h, w["w_in"], "col", MXU_DTYPE)
    if mid is not None:
        sm = dict(sm, cf_dw_b=sm["cf_dw_b"] + mid(p))
    o = retention_fwd("ret_fwd", p, cos2, sin2, log_g, nb, s)
    ya_in = head_gate_fwd("ret_gate", o, p, gate_blk)
    yb_in, cz = short_conv_fwd("sc_fwd", p, BLK_SCB, sm["sc_conv_w"], nb)
    u1 = conformer_conv_fwd("cf_fwd", p, BLK_CFA, sm["cf_dw_w"], sm["cf_dw_b"], nb)
    yc_in = ln_silu_fwd("cf_ln", u1, sm["cf_ln_g"], sm["cf_ln_b"])
    ya = mm_fwd("mx_proj", ya_in, w["w_ret_o"], "row", F32)
    yb = mm_fwd("mx_proj", yb_in, w["w_sc_o"], "row", F32)
    yc = mm_fwd("mx_proj", yc_in, w["w_cf_o"], "row", F32)
    mg = merge_fwd("mx_merge", p, BLK_MERGE, ya, yb, yc)
    m, out, h_next = mm_post("mx_out", mg, w["w_o"], xs, g_post, 1.0, g_next)
    return out, h_next, dict(x=xs, h=h, p=p, o=o, ya_in=ya_in, yb_in=yb_in, cz=cz, u1=u1, yc_in=yc_in, ya=ya, yb=yb, yc=yc,
                     mg=mg, m=m, w=w)


def _mixer_bwd(dxs, dm, sv, sm, g_pre, rope, nb, s, push, prev):
    cos2, sin2, log_g = rope
    w, p = sv["w"], sv["p"]
    d = dxs.shape[1]
    gate_blk = (BLK_GATE * d) // RET_V_DIM
    grads, gsm = {}, {}

    def proj_bwd(wname, a_in, dy, out_dtype):
        grads[wname] = mm_dw("mx_proj_dw", a_in, dy, "row", w[wname].shape)
        return mm_dx("mx_proj_dx", dy, w[wname], "row", out_dtype)

    dmg = proj_bwd("w_o", sv["mg"], dm, MXU_DTYPE)
    dg0, dg1, dg2, dya, dyb, dyc = merge_bwd("mx_merge_bwd", p, BLK_MERGE, sv["ya"], sv["yb"], sv["yc"], dmg)
    dya_in = proj_bwd("w_ret_o", sv["ya_in"], dya, MXU_DTYPE)
    dyb_in = proj_bwd("w_sc_o", sv["yb_in"], dyb, MXU_DTYPE)
    dyc_in = proj_bwd("w_cf_o", sv["yc_in"], dyc, MXU_DTYPE)
    do, dgret = head_gate_bwd("ret_gate_bwd", sv["o"], p, gate_blk, dya_in)
    dq, dk, dv = retention_bwd("ret_bwd", p, cos2, sin2, log_g, do, nb, s)
    dscb, dscc, dscx, gsm["sc_conv_w"] = short_conv_bwd("sc_bwd", p, BLK_SCB, sm["sc_conv_w"], sv["cz"], dyb_in, nb)
    du1, dlg, dlb = ln_silu_bwd("cf_ln_bwd", sv["u1"], sm["cf_ln_g"], sm["cf_ln_b"], dyc_in)
    dcfa, dcfb, gsm["cf_dw_w"], dbias = conformer_conv_bwd("cf_bwd", p, BLK_CFA, sm["cf_dw_w"], du1, nb)
    gsm.update(cf_ln_g=dlg[0], cf_ln_b=dlb[0], cf_dw_b=dbias[0])
    dp = concat_cols("mx_dp", [dq, dk, dv, dgret, dscb, dscc, dscx, dcfa, dcfb, dg0, dg1, dg2])
    grads["w_in"] = mm_dw("mx_in_dw", sv["h"], dp, "col", w["w_in"].shape)
    return mm_dx_norms("mx_in_dx", dp, w["w_in"], sv["x"], g_pre, dxs, prev, push(grads)) + (gsm,)


def local_step(x, positions, target, small, fetch, push):
    nb, s, d = x.shape
    t = nb * s
    depth = small["norm_g"].shape[0]
    rope = _rope_tables(positions) + (_log_gamma(),)
    xs = x.reshape(t, d)
    token = [None]

    def gain(l, i):
        g = small["norm_g"][l, i][None, :]
        if token[0] is not None:
            g, token[0] = g + token[0], None
        return g

    def mixer_small(l):
        return dict(sc_conv_w=small["sc_conv_w"][l], cf_dw_w=small["cf_dw_w"][l], cf_dw_b=small["cf_dw_b"][l][None, :],
                    cf_ln_g=small["cf_ln_g"][l][None, :], cf_ln_b=small["cf_ln_b"][l][None, :])

    saved = {}
    order = [(l, blk) for l in range(depth) for blk in BLOCKS]
    h = None
    for at, (l, blk) in enumerate(order):
        w, token[0], mid = fetch(l, blk, xs)
        i0 = NORM_OF[blk]
        if h is None:
            h = rms_fwd("first_rms", xs, gain(l, i0))
        g_post = gain(l, i0 + 1)
        g_next = gain(order[at + 1][0], NORM_OF[order[at + 1][1]]) if at + 1 < len(order) else None
        if blk == "mixer":
            xs, h, saved[l, blk] = _mixer_fwd(xs, h, w, mixer_small(l), g_post, g_next, rope, nb, s, mid)
        else:
            xs, h, saved[l, blk] = _ffn_fwd(xs, h, w, blk, g_post, g_next)

    dxs, loss = loss_head("loss", xs, target.reshape(t, d))

    dnorm = [[None] * 6 for _ in range(depth)]
    gsmall = {n: [None] * depth for n in ("sc_conv_w", "cf_dw_w", "cf_dw_b", "cf_ln_g", "cf_ln_b")}
    def branch(group):
        l, blk = group
        sv = saved[group]
        return (sv["m"], gain(l, NORM_OF[blk] + 1), 1.0) if blk == "mixer" else (sv["y"], gain(l, NORM_OF[blk] + 1), 0.5)

    l, blk = order[-1]
    y, g_post, scale = branch(order[-1])
    dy, dnorm[l][NORM_OF[blk] + 1] = post_bwd("last_post_bwd", y, g_post, dxs, scale)
    for at in reversed(range(len(order))):
        l, blk = order[at]
        i0 = NORM_OF[blk]
        prev = branch(order[at - 1]) if at > 0 else None
        put = functools.partial(push, l, blk)
        if blk == "mixer":
            dxs, dnorm[l][i0], dy, dg_prev, gsm = _mixer_bwd(
                dxs, dy, saved[l, blk], mixer_small(l), gain(l, i0), rope, nb, s, put, prev)
            for n, v in gsm.items():
                gsmall[n][l] = v
        else:
            dxs, dnorm[l][i0], dy, dg_prev = _ffn_bwd(dxs, dy, saved[l, blk], blk, gain(l, i0), put, prev)
        if at > 0:
            dnorm[order[at - 1][0]][NORM_OF[order[at - 1][1]] + 1] = dg_prev

    gs = {n: jnp.stack(v) for n, v in gsmall.items()}
    gs["norm_g"] = jnp.stack([jnp.concatenate(r, axis=0) for r in dnorm])
    return loss, dxs.reshape(nb, s, d), gs


ANY = pl.BlockSpec(memory_space=pl.ANY)
HBM = pl.BlockSpec(memory_space=pltpu.HBM)
SEM = pl.BlockSpec(memory_space=pltpu.SEMAPHORE)
VMEM_WHOLE = pl.BlockSpec(memory_space=pltpu.VMEM)
EFFECT = pltpu.SideEffectType.DATAFLOW_SIDE_EFFECTING
TOKEN = jax.ShapeDtypeStruct((8, 128), F32)


def _other_chips(x, y):
    return [(1 - x, y), (x, 1 - y), (1 - x, 1 - y)]


def _remote(src, dst, send_sem, recv_sem, to):
    return pltpu.make_async_remote_copy(src_ref=src, dst_ref=dst, send_sem=send_sem, recv_sem=recv_sem,
                                        device_id=to, device_id_type=MESH)


def _in_hbm(v):
    return pltpu.with_memory_space_constraint(v, pltpu.HBM)


def place_quarters(ws, layer, ids, after):
    m = len(ws)

    def body(ids_ref, *refs):
        for w_ref, o_ref in zip(refs[:m], refs[m + 1:]):
            o_ref[...] = w_ref[...].astype(o_ref.dtype)

    def spec(w, where):
        return pl.BlockSpec((None, w.shape[1] // STREAM_STEPS, w.shape[2]), where)

    return pl.pallas_call(
        body, name="place_quarters",
        grid_spec=pltpu.PrefetchScalarGridSpec(
            num_scalar_prefetch=1, grid=(STREAM_STEPS,),
            in_specs=[spec(w, lambda i, ids_ref: (layer, i, 0)) for w in ws] + [ANY],
            out_specs=[spec(w, lambda i, ids_ref: (ids_ref[0], i, 0)) for w in ws]),
        out_shape=[jax.ShapeDtypeStruct((N_CHIP,) + w.shape[1:], MXU_DTYPE) for w in ws],
        compiler_params=_params(("parallel",)),
    )(ids, *ws, after)


def _gather_copies(lands, send, recv):
    x, y, c = _axes()
    me = 2 * x + y
    mine, theirs = [], []
    for a, ld in enumerate(lands):
        rh = ld.shape[1] // 2
        rows = pl.ds(c * rh, rh)
        for k, (px, py) in enumerate(_other_chips(x, y)):
            to = (px, py, c)
            mine.append(_remote(ld.at[me, rows, :], ld.at[me, rows, :], send.at[3 * a + k], recv.at[3 * a + k], to))
            got = ld.at[2 * px + py, rows, :]
            theirs.append(_remote(got, got, send.at[3 * a + k], recv.at[3 * a + k], to))
    return mine, theirs


def gather_start(name, groups, after):
    flat = [s for g in groups for s in g]
    n, ng = len(flat), len(groups)
    sizes = [len(g) for g in groups]

    def body(*refs):
        lands = refs[:n]
        sems = refs[n + 1:n + 1 + 2 * ng]
        token = refs[-1]
        at = 0
        for g, m in enumerate(sizes):
            mine, _ = _gather_copies(lands[at:at + m], sems[2 * g], sems[2 * g + 1])
            for cp in mine:
                cp.start()
            at += m
        token[...] = jnp.zeros_like(token)

    sem_shapes = []
    for m in sizes:
        sem_shapes += [pltpu.SemaphoreType.DMA((3 * m,))] * 2
    res = pl.pallas_call(
        body, name=name, in_specs=[HBM] * n + [ANY],
        out_specs=[SEM] * (2 * ng) + [HBM] * n + [VMEM_WHOLE],
        out_shape=sem_shapes + [pltpu.HBM(s.shape, s.dtype) for s in flat] + [TOKEN],
        input_output_aliases={i: 2 * ng + i for i in range(n)},
        compiler_params=pltpu.CompilerParams(has_side_effects=EFFECT),
    )(*[_in_hbm(s) for s in flat], after)
    sems, thru, token = res[:2 * ng], res[2 * ng:2 * ng + n], res[-1]
    out, at = [], 0
    for g, m in enumerate(sizes):
        out.append((sems[2 * g], sems[2 * g + 1], thru[at:at + m]))
        at += m
    return out, token


def gather_wait(lands, send, recv, after):
    m = len(lands)

    def body(*refs):
        mine, theirs = _gather_copies(refs[:m], refs[m], refs[m + 1])
        for cp in mine:
            cp.wait_send()
        for cp in theirs:
            cp.wait_recv()

    return pl.pallas_call(
        body, name="gather_wait", in_specs=[HBM] * m + [SEM, SEM, ANY], out_specs=[HBM] * m,
        out_shape=[pltpu.HBM(l.shape, l.dtype) for l in lands],
        input_output_aliases={i: i for i in range(m)},
        compiler_params=pltpu.CompilerParams(has_side_effects=EFFECT),
    )(*lands, send, recv, after)


def copy_start(name, families, after=()):
    sizes = [len(f[0]) for f in families]
    n, k, nf = sum(sizes), len(after), len(families)

    def body(*refs):
        at = 0
        for f, (_, copies, _) in enumerate(families):
            for cp in copies(refs[at:at + sizes[f]], refs[n + k + 2 * f], refs[n + k + 2 * f + 1])[0]:
                cp.start()
            at += sizes[f]
        refs[-1][...] = jnp.zeros_like(refs[-1])

    flat = [b for f in families for b in f[0]]
    sems = [pltpu.SemaphoreType.DMA((f[2],)) for f in families for _ in range(2)]
    res = pl.pallas_call(
        body, name=name, in_specs=[HBM] * n + [ANY] * k, out_specs=[SEM] * (2 * nf) + [HBM] * n + [VMEM_WHOLE],
        out_shape=sems + [pltpu.HBM(b.shape, b.dtype) for b in flat] + [TOKEN],
        input_output_aliases={i: 2 * nf + i for i in range(n)},
        compiler_params=pltpu.CompilerParams(has_side_effects=EFFECT),
    )(*[_in_hbm(b) for b in flat], *after)
    out, at = [], 2 * nf
    for f in range(nf):
        out.append((res[2 * f], res[2 * f + 1], list(res[at:at + sizes[f]])))
        at += sizes[f]
    return out, res[-1]


def copy_wait(name, bufs, send, recv, copies, after=()):
    n = len(bufs)

    def body(*refs):
        mine, theirs = copies(refs[:n], refs[n], refs[n + 1])
        for cp in mine:
            cp.wait_send()
        for cp in theirs:
            cp.wait_recv()

    return list(pl.pallas_call(
        body, name=name, in_specs=[HBM] * n + [SEM, SEM] + [ANY] * len(after), out_specs=[HBM] * n,
        out_shape=[pltpu.HBM(b.shape, b.dtype) for b in bufs], input_output_aliases={i: i for i in range(n)},
        compiler_params=pltpu.CompilerParams(has_side_effects=EFFECT),
    )(*bufs, send, recv, *after))


def _fill_copies(lands, send, recv):
    x, y, c = _axes()
    sib = (x, y, 1 - c)
    mine, theirs = [], []
    for a, ld in enumerate(lands):
        rh = ld.shape[1] // 2
        for k, (px, py) in enumerate(_other_chips(x, y)):
            got = ld.at[2 * px + py, pl.ds(c * rh, rh), :]
            mine.append(_remote(got, got, send.at[3 * a + k], recv.at[3 * a + k], sib))
            blk = ld.at[2 * px + py, pl.ds((1 - c) * rh, rh), :]
            theirs.append(_remote(blk, blk, send.at[3 * a + k], recv.at[3 * a + k], sib))
    return mine, theirs


def _presum_copies(grads, lands, send, recv):
    x, y, c = _axes()
    cps = []
    for a, (g, ld) in enumerate(zip(grads, lands)):
        rh = g.shape[1] // 2
        cps.append(_remote(g.at[:, pl.ds((1 - c) * rh, rh), :], ld, send.at[a], recv.at[a], (x, y, 1 - c)))
    return cps


def presum_wait(grads, lands, send, recv, after):
    m = len(grads)

    def body(*refs):
        for cp in _presum_copies(refs[:m], refs[m:2 * m], refs[2 * m], refs[2 * m + 1]):
            cp.wait_send()
            cp.wait_recv()

    res = pl.pallas_call(
        body, name="presum_wait", in_specs=[HBM] * (2 * m) + [SEM, SEM] + [ANY] * len(after),
        out_specs=[HBM] * (2 * m),
        out_shape=[pltpu.HBM(g.shape, g.dtype) for g in grads] + [pltpu.HBM(l.shape, l.dtype) for l in lands],
        input_output_aliases={i: i for i in range(2 * m)},
        compiler_params=pltpu.CompilerParams(has_side_effects=EFFECT),
    )(*grads, *lands, send, recv, *after)
    return res[:m], res[m:]


def add_halves(gs, lands, ids):
    m = len(gs)

    def body(ids_ref, *refs):
        for a_ref, b_ref, o_ref in zip(refs[:m], refs[m:2 * m], refs[2 * m:]):
            o_ref[...] = (a_ref[...].astype(F32) + b_ref[...].astype(F32)).astype(o_ref.dtype)

    def spec(ld, where):
        return pl.BlockSpec((None,) + ld.shape[1:], where)

    return pl.pallas_call(
        body, name="add_halves",
        grid_spec=pltpu.PrefetchScalarGridSpec(
            num_scalar_prefetch=1, grid=(N_CHIP,),
            in_specs=[spec(ld, lambda i, ids_ref: (i, ids_ref[1], 0)) for ld in lands]
            + [spec(ld, lambda i, ids_ref: (i, 0, 0)) for ld in lands],
            out_specs=[spec(ld, lambda i, ids_ref: (i, 0, 0)) for ld in lands]),
        out_shape=[jax.ShapeDtypeStruct(ld.shape, ld.dtype) for ld in lands],
        compiler_params=_params(("parallel",)),
    )(ids, *gs, *lands)


def _scatter_copies(parts, lands, send, recv):
    x, y, c = _axes()
    cps = []
    for a, (pt, ld) in enumerate(zip(parts, lands)):
        for k, (px, py) in enumerate(_other_chips(x, y)):
            cps.append(_remote(pt.at[2 * px + py], ld.at[k], send.at[3 * a + k], recv.at[3 * a + k], (px, py, c)))
    return cps


def scatter_wait(parts, lands, send, recv, after):
    m = len(parts)

    def body(*refs):
        for cp in _scatter_copies(refs[:m], refs[m:2 * m], refs[2 * m], refs[2 * m + 1]):
            cp.wait_send()
            cp.wait_recv()

    res = pl.pallas_call(
        body, name="scatter_wait", in_specs=[HBM] * (2 * m) + [SEM, SEM] + [ANY] * len(after),
        out_specs=[HBM] * (2 * m),
        out_shape=[pltpu.HBM(p.shape, p.dtype) for p in parts] + [pltpu.HBM(l.shape, l.dtype) for l in lands],
        input_output_aliases={i: i for i in range(2 * m)},
        compiler_params=pltpu.CompilerParams(has_side_effects=EFFECT),
    )(*parts, *lands, send, recv, *after)
    return res[:m], res[m:]


def sum_partials(parts, lands, ids, layer, depth, intos):
    m = len(parts)
    nt = STREAM_STEPS

    def body(ids_ref, *refs):
        for p_ref, l_ref, o_ref in zip(refs[:m], refs[m:2 * m], refs[-m:]):
            acc = p_ref[...].astype(F32)
            for k in range(N_CHIP - 1):
                acc = acc + l_ref[k].astype(F32)
            o_ref[...] = acc

    def rows(p):
        return p.shape[1] // nt

    in_specs = [pl.BlockSpec((None, rows(p), p.shape[2]), lambda i, ids_ref: (ids_ref[0], i, 0)) for p in parts]
    in_specs += [pl.BlockSpec((N_CHIP - 1, rows(p), p.shape[2]), lambda i, ids_ref: (0, i, 0)) for p in parts]
    args = [ids, *parts, *lands]
    aliases = {}
    if intos is not None:
        in_specs += [ANY] * m
        args += list(intos)
        aliases = {1 + 2 * m + a: a for a in range(m)}
    return pl.pallas_call(
        body, name="sum_partials",
        grid_spec=pltpu.PrefetchScalarGridSpec(
            num_scalar_prefetch=1, grid=(nt,), in_specs=in_specs,
            out_specs=[pl.BlockSpec((None, rows(p), p.shape[2]), lambda i, ids_ref: (layer, ids_ref[1] * nt + i, 0))
                       for p in parts]),
        out_shape=[jax.ShapeDtypeStruct((depth, 2 * p.shape[1], p.shape[2]), F32) for p in parts],
        input_output_aliases=aliases, compiler_params=_params(("parallel",)),
    )(*args)


def _final_copies(layer):
    def copies(bufs, send, recv):
        x, y, c = _axes()
        sib = (x, y, 1 - c)
        mine, theirs = [], []
        for a, buf in enumerate(bufs):
            rh = buf.shape[1] // 2
            src = buf.at[layer, pl.ds(c * rh, rh), :]
            mine.append(_remote(src, src, send.at[a], recv.at[a], sib))
            dst = buf.at[layer, pl.ds((1 - c) * rh, rh), :]
            theirs.append(_remote(dst, dst, send.at[a], recv.at[a], sib))
        return mine, theirs

    return copies


def allgather_small(pk):
    def body(in_ref, out_ref, send, recv):
        x, y, c = _axes()
        me = 2 * x + y
        chips = _other_chips(x, y)
        out_ref[pl.ds(me, 1)] = in_ref[...][None]
        cps = []
        for k, (px, py) in enumerate(chips):
            cp = _remote(in_ref, out_ref.at[me], send.at[k], recv.at[k], (px, py, c))
            cp.start()
            cps.append(cp)
        for k, (px, py) in enumerate(chips):
            got = out_ref.at[2 * px + py]
            _remote(got, got, send.at[k], recv.at[k], (px, py, c)).wait_recv()
        for cp in cps:
            cp.wait_send()

    return pl.pallas_call(
        body, name="allgather_small", in_specs=[VMEM_WHOLE], out_specs=VMEM_WHOLE,
        out_shape=jax.ShapeDtypeStruct((N_CHIP,) + pk.shape, pk.dtype),
        scratch_shapes=[pltpu.SemaphoreType.DMA((3,))] * 2,
    )(pk)


N_DEV = 8


def _small_copies(bufs, send, recv):
    g, slots = bufs
    x, y, c = _axes()
    me = 4 * x + 2 * y + c
    mine, theirs = [], []
    for mask in range(1, N_DEV):
        px = 1 - x if mask & 4 else x
        py = 1 - y if mask & 2 else y
        pc = 1 - c if mask & 1 else c
        mine.append(_remote(g, slots.at[me], send.at[mask - 1], recv.at[mask - 1], (px, py, pc)))
        got = slots.at[4 * px + 2 * py + pc]
        theirs.append(_remote(got, got, send.at[mask - 1], recv.at[mask - 1], (px, py, pc)))
    return mine, theirs


def sum_slots(g, slots, me):
    def body(me_ref, g_ref, slots_ref, o_ref):
        acc = None
        for d in range(N_DEV):
            term = jnp.where(me_ref[0] == d, g_ref[...], slots_ref[d])
            acc = term if acc is None else acc + term
        o_ref[...] = acc

    return pl.pallas_call(
        body, name="sum_slots",
        grid_spec=pltpu.PrefetchScalarGridSpec(
            num_scalar_prefetch=1, grid=(1,),
            in_specs=[pl.BlockSpec(g.shape, lambda i, me_ref: (0, 0)),
                      pl.BlockSpec(slots.shape, lambda i, me_ref: (0, 0, 0))],
            out_specs=pl.BlockSpec(g.shape, lambda i, me_ref: (0, 0))),
        out_shape=jax.ShapeDtypeStruct(g.shape, g.dtype),
        compiler_params=_params(("arbitrary",)),
    )(me, g, slots)


def adamw(w, g, m, v, layer=None, intos=None):
    shape = w.shape
    cols = shape[-1]
    rows = int(np.prod(shape[:-1]))
    span = rows if layer is None else rows // shape[0]
    tr = span
    for cand in (256, 128):
        if span % cand == 0 and cand * cols * 4 <= 2 * 1024 * 1024:
            tr = cand
            break
    first = 0 if layer is None else layer * (span // tr)
    c1 = 1.0 - ADAM_B1 ** ADAM_STEP
    c2 = 1.0 - ADAM_B2 ** ADAM_STEP

    def body(w_ref, g_ref, m_ref, v_ref, *rest):
        d_ref, nm_ref, nv_ref, g_out = rest[-4:]
        gv = g_ref[...]
        g_out[...] = gv
        nm = ADAM_B1 * m_ref[...] + (1.0 - ADAM_B1) * gv
        nv = ADAM_B2 * v_ref[...] + (1.0 - ADAM_B2) * jnp.square(gv)
        d_ref[...] = -ADAM_LR * ((nm / c1) / (jnp.sqrt(nv / c2) + ADAM_EPS) + ADAM_WD * w_ref[...])
        nm_ref[...] = nm
        nv_ref[...] = nv

    spec = pl.BlockSpec((tr, cols), lambda i: (first + i, 0))
    args = [a.reshape(rows, cols) for a in (w, g, m, v)]
    in_specs, aliases = [spec] * 4, {}
    if intos is not None:
        args += [a.reshape(rows, cols) for a in intos]
        in_specs += [ANY] * 4
        aliases = {4 + k: k for k in range(4)}
    res = pl.pallas_call(
        body, name="adamw", grid=(span // tr,), in_specs=in_specs, out_specs=[spec] * 4,
        out_shape=[jax.ShapeDtypeStruct((rows, cols), F32)] * 4, input_output_aliases=aliases,
        compiler_params=_params(("parallel",)),
    )(*args)
    return [r.reshape(shape) for r in res]


WEIGHTS = ("norm_g", "ffn1_w_gu", "ffn1_w_down", "w_in", "w_ret_o", "sc_conv_w", "w_sc_o", "cf_dw_w", "cf_dw_b",
           "cf_ln_g", "cf_ln_b", "w_cf_o", "w_o", "ffn2_w_gu", "ffn2_w_down")
SHARDED_SMALL = ("norm_g", "sc_conv_w", "cf_dw_w")
REPLICATED_SMALL = ("cf_dw_b", "cf_ln_g", "cf_ln_b")

def _pack_rows(parts):
    padded, offs, at = [], [], 0
    for p in parts:
        r = -(-p.shape[0] // SUBLANES) * SUBLANES
        padded.append(jnp.pad(p, ((0, r - p.shape[0]), (0, 0))))
        offs.append(at)
        at += r
    return jnp.concatenate(padded, axis=0), offs


def kernel(x, positions, norm_g, ffn1_w_gu, ffn1_w_down, w_in, w_ret_o, sc_conv_w, w_sc_o, cf_dw_w, cf_dw_b, cf_ln_g, cf_ln_b, w_cf_o, w_o, ffn2_w_gu, ffn2_w_down, loss_target, m_norm_g, m_ffn1_w_gu, m_ffn1_w_down, m_w_in, m_w_ret_o, m_sc_conv_w, m_w_sc_o, m_cf_dw_w, m_cf_dw_b, m_cf_ln_g, m_cf_ln_b, m_w_cf_o, m_w_o, m_ffn2_w_gu, m_ffn2_w_down, v_norm_g, v_ffn1_w_gu, v_ffn1_w_down, v_w_in, v_w_ret_o, v_sc_conv_w, v_w_sc_o, v_cf_dw_w, v_cf_dw_b, v_cf_ln_g, v_cf_ln_b, v_w_cf_o, v_w_o, v_ffn2_w_gu, v_ffn2_w_down):
    wts = dict(zip(WEIGHTS, (norm_g, ffn1_w_gu, ffn1_w_down, w_in, w_ret_o, sc_conv_w, w_sc_o, cf_dw_w, cf_dw_b,
                             cf_ln_g, cf_ln_b, w_cf_o, w_o, ffn2_w_gu, ffn2_w_down)))
    mom = dict(zip(WEIGHTS, (m_norm_g, m_ffn1_w_gu, m_ffn1_w_down, m_w_in, m_w_ret_o, m_sc_conv_w, m_w_sc_o,
                             m_cf_dw_w, m_cf_dw_b, m_cf_ln_g, m_cf_ln_b, m_w_cf_o, m_w_o, m_ffn2_w_gu, m_ffn2_w_down)))
    var = dict(zip(WEIGHTS, (v_norm_g, v_ffn1_w_gu, v_ffn1_w_down, v_w_in, v_w_ret_o, v_sc_conv_w, v_w_sc_o,
                             v_cf_dw_w, v_cf_dw_b, v_cf_ln_g, v_cf_ln_b, v_w_cf_o, v_w_o, v_ffn2_w_gu, v_ffn2_w_down)))
    depth = norm_g.shape[0]
    dq = norm_g.shape[-1]
    d = N_CHIP * dq
    chip = 2 * lax.axis_index("x") + lax.axis_index("y")
    ids = jnp.stack([chip, lax.axis_index("c")]).astype(jnp.int32)

    pk, offs = _pack_rows([wts[n].reshape(-1, dq) for n in SHARDED_SMALL])
    gk4 = allgather_small(pk)
    gk = gk4.transpose(1, 0, 2).reshape(pk.shape[0], d)
    small = {n: wts[n] for n in REPLICATED_SMALL}
    for n, o in zip(SHARDED_SMALL, offs):
        rows = wts[n].shape[0] * wts[n].shape[1]
        small[n] = gk[o:o + rows].reshape(wts[n].shape[:2] + (d,))

    order = [(l, blk) for l in range(depth) for blk in BLOCKS]
    def placed(groups, after):
        return [place_quarters([wts[n] for n in BLOCK_WEIGHTS[blk]], l, ids, after) for l, blk in groups]

    first, token = gather_start("gather_start_first", placed(order[:1], gk4), gk4)
    rest, token = gather_start("gather_start_rest", placed(order[1:], token), token)
    started = dict(zip(order, first + rest))
    small["norm_g"] = small["norm_g"] + token[0:1, 0:1]

    filling = {}

    def fill(group, after):
        send, recv, lands = started[group]
        lands = gather_wait(lands, send, recv, after)
        started_fill, tok = copy_start("fill_start", [(lands, _fill_copies, 3 * len(lands))])
        filling[group] = started_fill[0]
        return tok[0:1, 0:1]

    def fetch(l, blk, after):
        at = order.index((l, blk))
        if (l, blk) not in filling:
            fill((l, blk), token if at == 0 else after)
        send, recv, lands = filling.pop((l, blk))
        lands = copy_wait("fill_wait", lands, send, recv, _fill_copies, (after,))
        tok, mid = None, None
        if at == 1:
            mid = functools.partial(fill, order[at + 1])
        elif 1 < at < len(order) - 1:
            tok = fill(order[at + 1], lands[0])
        return dict(zip(BLOCK_WEIGHTS[blk], lands)), tok, mid

    gsum = {n: None for n in BIG}
    presums, scatters, finals = [], [], []

    def scatter_ready(after):
        group, gl, lands, send, recv = presums.pop(0)
        gl, lands = presum_wait(gl, lands, send, recv, after)
        parts = list(add_halves(gl, lands, ids))
        m = len(parts)
        lands = [lax.empty((N_CHIP - 1,) + p.shape[1:], p.dtype) for p in parts]
        family = (parts + lands, lambda refs, sd, rv: (_scatter_copies(refs[:m], refs[m:], sd, rv),) * 2, 3 * m)
        return family, lambda sd, rv, bufs: scatters.append((group, bufs[:m], bufs[m:], sd, rv))

    def final_ready(after):
        (l, blk), parts, lands, send, recv = scatters.pop(0)
        parts, lands = scatter_wait(parts, lands, send, recv, after)
        names = BLOCK_WEIGHTS[blk]
        intos = None if gsum[names[0]] is None else [gsum[n] for n in names]
        sums = list(sum_partials(parts, lands, ids, l, depth, intos))

        def note(sd, rv, bufs):
            gsum.update(zip(names, bufs))
            finals.append((names, l, sd, rv))

        return (sums, _final_copies(l), len(sums)), note

    def start_all(name, ready, after=()):
        started, tok = copy_start(name, [family for family, _ in ready], after)
        for (_, note), (sd, rv, bufs) in zip(ready, started):
            note(sd, rv, bufs)
        return tok

    def scatter_next(after):
        return start_all("scatter_start", [scatter_ready(after)])

    def sum_next(after):
        return start_all("final_start", [final_ready(after)])

    def final_next(after):
        names, l, send, recv = finals.pop(0)
        gsum.update(zip(names, copy_wait("final_wait", [gsum[n] for n in names], send, recv, _final_copies(l), after)))

    def push(l, blk, grads):
        gl = [grads[n] for n in BLOCK_WEIGHTS[blk]]
        m = len(gl)
        lands = [lax.empty((g.shape[0], g.shape[1] // 2, g.shape[2]), g.dtype) for g in gl]
        ready = [((gl + lands, lambda refs, sd, rv: (_presum_copies(refs[:m], refs[m:], sd, rv),) * 2, m),
                  lambda sd, rv, bufs: presums.append(((l, blk), bufs[:m], bufs[m:], sd, rv)))]
        if scatters:
            ready.append(final_ready((gl[0],)))
        if presums:
            ready.append(scatter_ready((gl[0],)))
        return start_all("push_start", ready)[0:1, 0:1]

    loss, grad_x, gs = local_step(x, positions, loss_target, small, fetch, push)

    names = SHARDED_SMALL + REPLICATED_SMALL
    pg, offs = _pack_rows([gs[n].reshape(-1, d) for n in names])
    small_bufs = [pg, lax.empty((N_DEV,) + pg.shape, pg.dtype)]
    ((s_send, s_recv, s_bufs),), tok = copy_start("small_start", [(small_bufs, _small_copies, N_DEV - 1)], (grad_x,))
    tok = scatter_next((grad_x, tok))

    delta, new_m, new_v, grads = {}, {}, {}, {}

    def update(n, layer=None):
        g = gsum[n] if n in BIG else grads[n]
        prev = [delta[n], new_m[n], new_v[n], grads[n]] if layer is not None and n in delta else None
        delta[n], new_m[n], new_v[n], grads[n] = adamw(wts[n], g, mom[n], var[n], layer, prev)

    while finals and finals[0][1] > 0:
        done, l = finals[0][:2]
        final_next((tok,))
        for n in done:
            update(n, l)
    upper = tuple(delta[n] for n in BIG if n in delta)
    pg, slots = copy_wait("small_wait", s_bufs, s_send, s_recv, _small_copies, upper + (tok,))
    me = (2 * chip + lax.axis_index("c")).astype(jnp.int32).reshape(1)
    tot = sum_slots(pg, slots, me)
    for n, o in zip(names, offs):
        rows = int(np.prod(gs[n].shape[:-1]))
        full = tot[o:o + rows]
        if n in SHARDED_SMALL:
            full = lax.dynamic_slice_in_dim(full, chip * dq, dq, axis=1)
        grads[n] = full.reshape(wts[n].shape)

    for n in names:
        update(n)
    after = tuple(delta[n] for n in names)
    while scatters or finals:
        if scatters:
            after = (sum_next(after),)
        done, l = finals[0][:2]
        final_next(after)
        for n in done:
            update(n, l)
        after = tuple(delta[n] for n in done)

    loss_all = lax.psum(loss[0, 0], ("x", "y", "c"))
    return (loss_all, grad_x, *[grads[n] for n in WEIGHTS], *[delta[n] for n in WEIGHTS],
            *[new_m[n] for n in WEIGHTS], *[new_v[n] for n in WEIGHTS])
```

```python
import functools

import jax
import jax.numpy as jnp
import numpy as np
from jax import lax
from jax.experimental import pallas as pl
from jax.experimental.pallas import tpu as pltpu

F32 = jnp.float32
BF16 = jnp.bfloat16
MXU_DTYPE = BF16
VMEM_LIMIT_BYTES = 56 * 1024 * 1024
MESH = pl.DeviceIdType.MESH

N_CHIP = 4
CHUNK = 64
RET_HEADS = 4
RET_QK_DIM = 128
RET_V_DIM = 256
SC_KERNEL = 3
CF_KERNEL = 31
ROPE_BASE = 10000.0
NORM_EPS = 1e-6
LN_EPS = 1e-5
ADAM_LR = 0.001
ADAM_B1 = 0.9
ADAM_B2 = 0.999
ADAM_EPS = 1e-08
ADAM_WD = 0.01
ADAM_STEP = 10

SUBLANES = 8
CONV_PAD = 32
CONV_TS = 256
CONV_TC = 512
CONV_ROWS = 32
CONV_TILES = range(0, CONV_ROWS, SUBLANES)
SC_TS = 512


def _conv_scratch(ts):
    return [pltpu.VMEM((ts + CONV_PAD, CONV_TC), F32),
            pltpu.VMEM((SUBLANES - 1, ts + CONV_PAD - SUBLANES, CONV_TC), F32)]
RET_TQ = 512
MM_TM = 1024
MM_TN = 1536
MM_K1 = 1024
MM_W1 = 8 << 20
MM_SLICE = 256
MM_IN_BYTES = 36 << 20
STREAM_STEPS = 2


def _params(sem):
    return pltpu.CompilerParams(dimension_semantics=sem, vmem_limit_bytes=VMEM_LIMIT_BYTES)


def _axes():
    return lax.axis_index("x"), lax.axis_index("y"), lax.axis_index("c")


NN = (((1,), (0,)), ((), ()))
NT = (((1,), (1,)), ((), ()))
TN = (((0,), (0,)), ((), ()))


def _mm(name, a, b, out_shape, out_dtype, grid, a_spec, b_spec, o_spec, dims, acc_shape):
    nk = grid[2]

    def body(a_ref, b_ref, o_ref, *scratch):
        bv = b_ref[...]
        if bv.ndim == 3:
            bv = bv.reshape(-1, bv.shape[-1])
        part = lax.dot_general(a_ref[...], bv, dims, preferred_element_type=F32)

        def put(v):
            o_ref[...] = v.reshape(o_ref.shape).astype(o_ref.dtype)

        if nk == 1:
            put(part)
        else:
            acc = scratch[0]
            k = pl.program_id(2)

            @pl.when(k == 0)
            def _():
                acc[...] = part

            @pl.when(k > 0)
            def _():
                acc[...] += part

            @pl.when(k == nk - 1)
            def _():
                put(acc[...])

    scratch = [pltpu.VMEM(acc_shape, F32)] if nk > 1 else []
    return pl.pallas_call(
        body, name=name, grid=grid, in_specs=[a_spec, b_spec], out_specs=o_spec,
        out_shape=jax.ShapeDtypeStruct(out_shape, out_dtype), scratch_shapes=scratch,
        compiler_params=_params(("parallel", "parallel", "arbitrary")),
    )(a, b)


def _tile(n, target):
    best = None
    for t in range(128, min(n, target) + 1, 128):
        if n % t == 0:
            best = t
    assert best is not None, (n, target)
    return best


def _token_rows(t, width):
    tt = t
    while tt > MM_TM and tt * width * jnp.dtype(MXU_DTYPE).itemsize * 2 > MM_IN_BYTES:
        tt //= 2
    return tt


def mm_fwd(name, a, w4, mode, out_dtype):
    t = a.shape[0]
    _, r, c = w4.shape
    tm = min(t, MM_TM)
    if mode == "col":
        tn = _tile(c, MM_TN)
        npj = c // tn
        grid = (t // tm, N_CHIP * npj, 1)
        a_spec = pl.BlockSpec((tm, r), lambda i, j, k: (i, 0))
        b_spec = pl.BlockSpec((None, r, tn), lambda i, j, k: (j // npj, 0, j % npj))
        o_spec = pl.BlockSpec((tm, tn), lambda i, j, k: (i, j))
        return _mm(name, a, w4, (t, N_CHIP * c), out_dtype, grid, a_spec, b_spec, o_spec, NN, (tm, tn))
    if w4.size * w4.dtype.itemsize <= MM_W1:
        grid = (t // tm, 1, 1)
        a_spec = pl.BlockSpec((tm, N_CHIP * r), lambda i, j, k: (i, 0))
        b_spec = pl.BlockSpec((N_CHIP, r, c), lambda i, j, k: (0, 0, 0))
        o_spec = pl.BlockSpec((tm, c), lambda i, j, k: (i, 0))
        return _mm(name, a, w4, (t, c), out_dtype, grid, a_spec, b_spec, o_spec, NN, (tm, c))
    grid = (t // tm, 1, N_CHIP)
    a_spec = pl.BlockSpec((tm, r), lambda i, j, k: (i, k))
    b_spec = pl.BlockSpec((None, r, c), lambda i, j, k: (k, 0, 0))
    o_spec = pl.BlockSpec((tm, c), lambda i, j, k: (i, 0))
    return _mm(name, a, w4, (t, c), out_dtype, grid, a_spec, b_spec, o_spec, NN, (tm, c))


def mm_dx(name, dy, w4, mode, out_dtype):
    t = dy.shape[-2]
    _, r, c = w4.shape
    tm = min(t, MM_TM)
    if mode == "col":
        tn, npj = c, 1
        hb = N_CHIP // 2 * npj
        grid = (t // tm, 1, N_CHIP * npj)
        if dy.ndim == 3:
            a_spec = pl.BlockSpec((None, tm, tn), lambda i, j, k: (k // hb, i, k % hb))
        else:
            a_spec = pl.BlockSpec((tm, tn), lambda i, j, k: (i, k))
        b_spec = pl.BlockSpec((None, r, tn), lambda i, j, k: (k // npj, 0, k % npj))
        o_spec = pl.BlockSpec((tm, r), lambda i, j, k: (i, 0))
        return _mm(name, dy, w4, (t, r), out_dtype, grid, a_spec, b_spec, o_spec, NT, (tm, r))
    if N_CHIP * r <= MM_K1:
        grid = (t // tm, 1, 1)
        a_spec = pl.BlockSpec((tm, c), lambda i, j, k: (i, 0))
        b_spec = pl.BlockSpec((N_CHIP, r, c), lambda i, j, k: (0, 0, 0))
        o_spec = pl.BlockSpec((tm, N_CHIP * r), lambda i, j, k: (i, 0))
        return _mm(name, dy, w4, (t, N_CHIP * r), out_dtype, grid, a_spec, b_spec, o_spec, NT, (tm, N_CHIP * r))
    grid = (t // tm, N_CHIP, 1)
    a_spec = pl.BlockSpec((tm, c), lambda i, j, k: (i, 0))
    b_spec = pl.BlockSpec((None, r, c), lambda i, j, k: (j, 0, 0))
    o_spec = pl.BlockSpec((tm, r), lambda i, j, k: (i, j))
    return _mm(name, dy, w4, (t, N_CHIP * r), out_dtype, grid, a_spec, b_spec, o_spec, NT, (tm, r))


def mm_dw(name, a, dy, mode, shape3):
    t = a.shape[0]
    _, r, c = shape3
    if mode == "col":
        tn = _tile(c, MM_TN)
        npj = c // tn
        tt = _token_rows(t, r + tn)
        grid = (1, N_CHIP * npj, t // tt)
        a_spec = pl.BlockSpec((tt, r), lambda i, j, k: (k, 0))
        hb = N_CHIP // 2 * npj
        if dy.ndim == 3:
            b_spec = pl.BlockSpec((None, tt, tn), lambda i, j, k: (j // hb, k, j % hb))
        else:
            b_spec = pl.BlockSpec((tt, tn), lambda i, j, k: (k, j))
        o_spec = pl.BlockSpec((None, r, tn), lambda i, j, k: (j // npj, 0, j % npj))
        return _mm(name, a, dy, shape3, MXU_DTYPE, grid, a_spec, b_spec, o_spec, TN, (r, tn))
    if N_CHIP * r <= MM_K1:
        tt = min(_token_rows(t, N_CHIP * r + c), max(t // 4, MM_TM))
        grid = (1, 1, t // tt)
        a_spec = pl.BlockSpec((tt, N_CHIP * r), lambda i, j, k: (k, 0))
        b_spec = pl.BlockSpec((tt, c), lambda i, j, k: (k, 0))
        o_spec = pl.BlockSpec((N_CHIP, r, c), lambda i, j, k: (0, 0, 0))
        return _mm(name, a, dy, shape3, MXU_DTYPE, grid, a_spec, b_spec, o_spec, TN, (N_CHIP * r, c))
    tt = _token_rows(t, r + c)
    grid = (N_CHIP, 1, t // tt)
    a_spec = pl.BlockSpec((tt, r), lambda i, j, k: (k, i))
    b_spec = pl.BlockSpec((tt, c), lambda i, j, k: (k, 0))
    o_spec = pl.BlockSpec((None, r, c), lambda i, j, k: (i, 0, 0))
    return _mm(name, a, dy, shape3, MXU_DTYPE, grid, a_spec, b_spec, o_spec, TN, (r, c))


def _rms_bwd(x, g, dh):
    r = lax.rsqrt(jnp.mean(x * x, axis=-1, keepdims=True) + NORM_EPS)
    xhat = x * r
    dyg = dh * g
    dx = r * (dyg - xhat * jnp.mean(dyg * xhat, axis=-1, keepdims=True))
    return dx, jnp.sum(dh * xhat, axis=0, keepdims=True)


def mm_dx_norms(name, dy, w4, x, g_pre, dres, prev, after):
    t = dy.shape[-2]
    _, r, c = w4.shape
    tm = min(t, MM_TM // 2)
    nt, nk = t // tm, N_CHIP
    hb = N_CHIP // 2
    chained = prev is not None

    def body(dy_ref, w_ref, x_ref, dres_ref, g_ref, *rest):
        rest = rest[1:] if after is not None else rest
        if chained:
            y_ref, gp_ref, dx_ref, dg_ref, dyp_ref, dgp_ref, acc = rest
        else:
            dx_ref, dg_ref, acc = rest
        i, k = pl.program_id(0), pl.program_id(1)
        part = lax.dot_general(dy_ref[...], w_ref[...], NT, preferred_element_type=F32)

        @pl.when(k == 0)
        def _():
            acc[...] = part

        @pl.when(k > 0)
        def _():
            acc[...] += part

        def add_to(ref, v):
            @pl.when(i == 0)
            def _():
                ref[...] = v

            @pl.when(i > 0)
            def _():
                ref[...] += v

        @pl.when(k == nk - 1)
        def _():
            dx, dg = _rms_bwd(x_ref[...], g_ref[...], acc[...])
            dxs = dres_ref[...] + dx
            dx_ref[...] = dxs
            add_to(dg_ref, dg)
            if chained:
                dyp, dgp = _rms_bwd(y_ref[...], gp_ref[...], dxs)
                dyp_ref[...] = (prev[2] * dyp).astype(dyp_ref.dtype)
                add_to(dgp_ref, prev[2] * dgp)

    if dy.ndim == 3:
        dy_spec = pl.BlockSpec((None, tm, c), lambda i, k: (k // hb, i, k % hb))
    else:
        dy_spec = pl.BlockSpec((tm, c), lambda i, k: (i, k))
    rows = pl.BlockSpec((tm, r), lambda i, k: (i, 0))
    gain = pl.BlockSpec((1, r), lambda i, k: (0, 0))
    in_specs = [dy_spec, pl.BlockSpec((None, r, c), lambda i, k: (k, 0, 0)), rows, rows, gain]
    args = [dy, w4, x, dres, g_pre]
    if after is not None:
        in_specs.append(pl.BlockSpec(memory_space=pl.ANY))
        args.append(after)
    out_specs = [rows, gain]
    out_shape = [jax.ShapeDtypeStruct((t, r), F32), jax.ShapeDtypeStruct((1, r), F32)]
    if chained:
        in_specs += [rows, gain]
        args += [prev[0], prev[1]]
        out_specs += [rows, gain]
        out_shape += [jax.ShapeDtypeStruct((t, r), MXU_DTYPE), jax.ShapeDtypeStruct((1, r), F32)]
    res = pl.pallas_call(
        body, name=name, grid=(nt, nk), in_specs=in_specs, out_specs=out_specs, out_shape=out_shape,
        scratch_shapes=[pltpu.VMEM((tm, r), F32)], compiler_params=_params(("arbitrary", "arbitrary")),
    )(*args)
    return tuple(res) if chained else (res[0], res[1], None, None)


def _rowwise(name, fn, rows, pars, outs, accs=(), tm=256, ncol=1):
    t = rows[0][0].shape[0]
    nrow, npar, nout = len(rows), len(pars), len(outs)

    def body(*refs):
        vals = [r[...] for r in refs[:nrow + npar]]
        res = fn(*vals)
        out_refs = refs[nrow + npar:nrow + npar + nout]
        acc_refs = refs[nrow + npar + nout:]
        for o, v in zip(out_refs, res[:nout]):
            o[...] = v.astype(o.dtype)
        i = pl.program_id(1)
        for a, v in zip(acc_refs, res[nout:]):
            @pl.when(i == 0)
            def _(a=a, v=v):
                a[...] = v.astype(F32)

            @pl.when(i > 0)
            def _(a=a, v=v):
                a[...] += v.astype(F32)

    in_specs = [pl.BlockSpec((tm, w), functools.partial(lambda j, i, b: (i, b + j), b=b)) for _, w, b in rows]
    for arr, w in pars:
        if w is None:
            in_specs.append(pl.BlockSpec(arr.shape, lambda j, i: (0, 0)))
        else:
            in_specs.append(pl.BlockSpec((1, w), lambda j, i: (0, j)))
    out_specs = [pl.BlockSpec((tm, w), lambda j, i: (i, j)) for _, w, _ in outs]
    out_specs += [pl.BlockSpec((1, w), lambda j, i: (0, j)) for _, w in accs]
    out_shape = [jax.ShapeDtypeStruct((t, tw), dt) for tw, _, dt in outs]
    out_shape += [jax.ShapeDtypeStruct((1, tw), F32) for tw, _ in accs]
    res = pl.pallas_call(
        body, name=name, grid=(ncol, t // tm), in_specs=in_specs, out_specs=out_specs, out_shape=out_shape,
        compiler_params=_params(("parallel", "arbitrary" if accs else "parallel")),
    )(*[r[0] for r in rows], *[p[0] for p in pars])
    return res


def _rms(x, g):
    xf = x.astype(F32)
    return xf * lax.rsqrt(jnp.mean(xf * xf, axis=-1, keepdims=True) + NORM_EPS) * g


def _silu(x):
    return x * jax.nn.sigmoid(x)


def rms_fwd(name, x, g):
    d = x.shape[1]
    return _rowwise(name, lambda x, g: (_rms(x, g),), [(x, d, 0)], [(g, None)], [(d, d, MXU_DTYPE)], tm=512)[0]


def rms_bwd(name, x, g, dh, dres):
    d = x.shape[1]

    def fn(x, dh, dres, g):
        _, vjp = jax.vjp(_rms, x, g)
        dx, dg = vjp(dh.astype(F32))
        return dres + dx, dg

    return _rowwise(name, fn, [(x, d, 0), (dh, d, 0), (dres, d, 0)], [(g, None)], [(d, d, F32)], [(d, d)], tm=256)


def mm_post(name, a, w4, x, g_post, scale, g_next):
    t = a.shape[0]
    _, r, c = w4.shape
    tm = min(t, MM_TM // 2)
    chained = g_next is not None

    def body(a_ref, w_ref, x_ref, gp_ref, *rest):
        gn_ref, y_ref, xn_ref, h_ref = rest if chained else (None,) + rest + (None,)
        y = lax.dot_general(a_ref[...], w_ref[...].reshape(N_CHIP * r, c), NN, preferred_element_type=F32)
        y_ref[...] = y
        xn = x_ref[...] + scale * _rms(y, gp_ref[...])
        xn_ref[...] = xn
        if chained:
            h_ref[...] = _rms(xn, gn_ref[...]).astype(h_ref.dtype)

    def rows(width):
        return pl.BlockSpec((tm, width), lambda i: (i, 0))

    gain = pl.BlockSpec((1, c), lambda i: (0, 0))
    in_specs = [rows(N_CHIP * r), pl.BlockSpec((N_CHIP, r, c), lambda i: (0, 0, 0)), rows(c), gain]
    args = [a, w4, x, g_post]
    out_specs, out_shape = [rows(c), rows(c)], [jax.ShapeDtypeStruct((t, c), F32)] * 2
    if chained:
        in_specs.append(gain)
        args.append(g_next)
        out_specs.append(rows(c))
        out_shape.append(jax.ShapeDtypeStruct((t, c), MXU_DTYPE))
    res = pl.pallas_call(
        body, name=name, grid=(t // tm,), in_specs=in_specs, out_specs=out_specs, out_shape=out_shape,
        compiler_params=_params(("parallel",)),
    )(*args)
    return res[0], res[1], (res[2] if chained else None)


def post_bwd(name, y, g, dx, scale):
    d = y.shape[1]

    def fn(y, dx, g):
        _, vjp = jax.vjp(lambda y, g: scale * _rms(y, g), y, g)
        return vjp(dx)

    return _rowwise(name, fn, [(y, d, 0), (dx, d, 0)], [(g, None)], [(d, d, MXU_DTYPE)], [(d, d)], tm=256)


def ffn_up(name, h, w4):
    t = h.shape[0]
    _, r, c = w4.shape
    tm = min(t, MM_TM)
    tn = _tile(c, MM_TM)
    npj = c // tn
    half = N_CHIP // 2

    def body(h_ref, wg_ref, wu_ref, gu_ref, a_ref):
        hv = h_ref[...]
        g = lax.dot_general(hv, wg_ref[...], NN, preferred_element_type=F32)
        u = lax.dot_general(hv, wu_ref[...], NN, preferred_element_type=F32)
        sg = jax.nn.sigmoid(g)
        silu = g * sg
        gu_ref[0] = (u * (sg + silu * (1.0 - sg))).astype(gu_ref.dtype)
        gu_ref[1] = silu.astype(gu_ref.dtype)
        a_ref[...] = (silu * u).astype(a_ref.dtype)

    f = half * c
    return pl.pallas_call(
        body, name=name, grid=(t // tm, half * npj),
        in_specs=[pl.BlockSpec((tm, r), lambda i, j: (i, 0)),
                  pl.BlockSpec((None, r, tn), lambda i, j: (j // npj, 0, j % npj)),
                  pl.BlockSpec((None, r, tn), lambda i, j: (half + j // npj, 0, j % npj))],
        out_specs=[pl.BlockSpec((2, tm, tn), lambda i, j: (0, i, j)), pl.BlockSpec((tm, tn), lambda i, j: (i, j))],
        out_shape=[jax.ShapeDtypeStruct((2, t, f), MXU_DTYPE), jax.ShapeDtypeStruct((t, f), MXU_DTYPE)],
        compiler_params=_params(("parallel", "parallel")),
    )(h, w4, w4)


def ffn_down_dx(name, dy, w4, gu):
    t = dy.shape[0]
    _, r, c = w4.shape
    tm = min(t, MM_TM)

    def body(dy_ref, w_ref, gu_ref, o_ref):
        dyv = dy_ref[...]
        for n0 in range(0, r, MM_SLICE):
            cols = pl.ds(n0, MM_SLICE)
            da = lax.dot_general(dyv, w_ref[cols, :], NT, preferred_element_type=F32)
            o_ref[0, :, cols] = (da * gu_ref[0, :, cols].astype(F32)).astype(o_ref.dtype)
            o_ref[1, :, cols] = (da * gu_ref[1, :, cols].astype(F32)).astype(o_ref.dtype)

    return pl.pallas_call(
        body, name=name, grid=(t // tm, N_CHIP),
        in_specs=[pl.BlockSpec((tm, c), lambda i, j: (i, 0)), pl.BlockSpec((None, r, c), lambda i, j: (j, 0, 0)),
                  pl.BlockSpec((2, tm, r), lambda i, j: (0, i, j))],
        out_specs=pl.BlockSpec((2, tm, r), lambda i, j: (0, i, j)),
        out_shape=jax.ShapeDtypeStruct((2, t, N_CHIP * r), MXU_DTYPE),
        compiler_params=_params(("parallel", "parallel")),
    )(dy, w4, gu)


def _head_gate(o, g):
    mu = jnp.mean(o, axis=-1, keepdims=True)
    var = jnp.mean(jnp.square(o - mu), axis=-1, keepdims=True)
    return _silu(g.astype(F32)) * ((o - mu) * lax.rsqrt(var + LN_EPS))


def head_gate_fwd(name, o, p, gate_blk):
    dv = RET_V_DIM
    return _rowwise(name, lambda o, g: (_head_gate(o, g),), [(o, dv, 0), (p, dv, gate_blk)], [],
                    [(RET_HEADS * dv, dv, MXU_DTYPE)], tm=min(o.shape[0], 2048), ncol=RET_HEADS)[0]


def head_gate_bwd(name, o, p, gate_blk, da):
    dv = RET_V_DIM

    def fn(o, g, da):
        _, vjp = jax.vjp(_head_gate, o, g.astype(F32))
        return vjp(da.astype(F32))

    w = RET_HEADS * dv
    return _rowwise(name, fn, [(o, dv, 0), (p, dv, gate_blk), (da, dv, 0)], [],
                    [(w, dv, MXU_DTYPE), (w, dv, MXU_DTYPE)], tm=min(o.shape[0], 2048), ncol=RET_HEADS)


def _ln_silu(u, g, b):
    mu = jnp.mean(u, axis=-1, keepdims=True)
    var = jnp.mean(jnp.square(u - mu), axis=-1, keepdims=True)
    return _silu((u - mu) * lax.rsqrt(var + LN_EPS) * g + b)


def ln_silu_fwd(name, u, g, b):
    d = u.shape[1]
    return _rowwise(name, lambda u, g, b: (_ln_silu(u, g, b),), [(u, d, 0)], [(g, None), (b, None)],
                    [(d, d, MXU_DTYPE)], tm=512)[0]


def ln_silu_bwd(name, u, g, b, dc):
    d = u.shape[1]

    def fn(u, dc, g, b):
        _, vjp = jax.vjp(_ln_silu, u, g, b)
        return vjp(dc.astype(F32))

    return _rowwise(name, fn, [(u, d, 0), (dc, d, 0)], [(g, None), (b, None)], [(d, d, F32)], [(d, d), (d, d)],
                    tm=512)


def _merge(g0, g1, g2, ya, yb, yc):
    s = jax.nn.sigmoid
    return s(g0.astype(F32)) * ya + s(g1.astype(F32)) * yb + s(g2.astype(F32)) * yc


def merge_fwd(name, p, blk, ya, yb, yc):
    d = ya.shape[1]
    rows = [(p, d, blk), (p, d, blk + 1), (p, d, blk + 2), (ya, d, 0), (yb, d, 0), (yc, d, 0)]
    return _rowwise(name, lambda *v: (_merge(*v),), rows, [], [(d, d, MXU_DTYPE)], tm=512)[0]


def merge_bwd(name, p, blk, ya, yb, yc, dmg):
    d = ya.shape[1]

    def fn(g0, g1, g2, ya, yb, yc, dmg):
        _, vjp = jax.vjp(_merge, g0.astype(F32), g1.astype(F32), g2.astype(F32), ya, yb, yc)
        return vjp(dmg.astype(F32))

    rows = [(p, d, blk), (p, d, blk + 1), (p, d, blk + 2), (ya, d, 0), (yb, d, 0), (yc, d, 0), (dmg, d, 0)]
    return _rowwise(name, fn, rows, [], [(d, d, MXU_DTYPE)] * 6, tm=256)


def concat_cols(name, pieces):
    t = pieces[0].shape[0]
    widths = [p.shape[1] for p in pieces]
    tm = 256

    def body(*refs):
        o_ref, at = refs[-1], 0
        for r, w in zip(refs[:-1], widths):
            o_ref[:, at:at + w] = r[...]
            at += w

    return pl.pallas_call(
        body, name=name, grid=(t // tm,),
        in_specs=[pl.BlockSpec((tm, w), lambda i: (i, 0)) for w in widths],
        out_specs=pl.BlockSpec((tm, sum(widths)), lambda i: (i, 0)),
        out_shape=jax.ShapeDtypeStruct((t, sum(widths)), pieces[0].dtype),
        compiler_params=_params(("parallel",)),
    )(*pieces)


def loss_head(name, y, target):
    t, d = y.shape
    tm = 512

    def body(y_ref, t_ref, dy_ref, loss_ref):
        err = y_ref[...] - t_ref[...]
        dy_ref[...] = err * (1.0 / d)
        part = jnp.sum(jnp.sum(err * err, axis=1, keepdims=True), axis=0, keepdims=True) * (0.5 / d)

        @pl.when(pl.program_id(0) == 0)
        def _():
            loss_ref[...] = part

        @pl.when(pl.program_id(0) > 0)
        def _():
            loss_ref[...] += part

    return pl.pallas_call(
        body, name=name, grid=(t // tm,),
        in_specs=[pl.BlockSpec((tm, d), lambda i: (i, 0))] * 2,
        out_specs=[pl.BlockSpec((tm, d), lambda i: (i, 0)), pl.BlockSpec((1, 1), lambda i: (0, 0))],
        out_shape=[jax.ShapeDtypeStruct((t, d), F32), jax.ShapeDtypeStruct((1, 1), F32)],
        compiler_params=_params(("arbitrary",)),
    )(y, target)


def _rot(x, cos2, sin2):
    return x * cos2 + pltpu.roll(x, RET_QK_DIM // 2, 1) * sin2


def _decay_mask(lg, n0, rows, cols):
    n = n0 + lax.broadcasted_iota(jnp.int32, (rows, cols), 0)
    m = lax.broadcasted_iota(jnp.int32, (rows, cols), 1)
    shift = CHUNK.bit_length() - 1
    dist = jnp.abs(n - m).astype(F32)
    return jnp.where((m >> shift) <= (n >> shift), jnp.exp(lg * dist), 0.0)


def _ret_specs(s):
    dk, dv, h = RET_QK_DIM, RET_V_DIM, RET_HEADS
    return [
        pl.BlockSpec((s, dk), lambda b, hh: (b, hh)),
        pl.BlockSpec((s, dk), lambda b, hh: (b, h + hh)),
        pl.BlockSpec((s, dv), lambda b, hh: (b, (2 * h * dk) // dv + hh)),
        pl.BlockSpec((s, dk), lambda b, hh: (b, 0)),
        pl.BlockSpec((s, dk), lambda b, hh: (b, 0)),
        pl.BlockSpec((None, 1, dk), lambda b, hh: (hh, 0, 0)),
    ]


def retention_fwd(name, p, cos2, sin2, log_g, nb, s, gate_blk):
    dk, dv, h = RET_QK_DIM, RET_V_DIM, RET_HEADS

    def body(q_ref, k_ref, v_ref, cos_ref, sin_ref, lg_ref, g_ref, o_ref, a_ref, kr_ref):
        lg = lg_ref[0:1, 0:1]
        kr = _rot(k_ref[...].astype(F32), cos_ref[...], sin_ref[...]) * (dk ** -0.5)
        kr_ref[...] = kr.astype(kr_ref.dtype)
        for qi in range(s // RET_TQ):
            n0, kmax = qi * RET_TQ, (qi + 1) * RET_TQ
            rows = pl.ds(n0, RET_TQ)
            qr = _rot(q_ref[rows, :].astype(F32), cos_ref[rows, :], sin_ref[rows, :]).astype(MXU_DTYPE)
            sc = lax.dot_general(qr, kr_ref[0:kmax, :], NT, preferred_element_type=F32)
            pm = (sc * _decay_mask(lg, n0, RET_TQ, kmax)).astype(MXU_DTYPE)
            o = lax.dot_general(pm, v_ref[0:kmax, :], NN, preferred_element_type=F32)
            o_ref[rows, :] = o
            a_ref[rows, :] = _head_gate(o, g_ref[rows, :]).astype(a_ref.dtype)

    ospec = pl.BlockSpec((s, dv), lambda b, hh: (b, hh))
    return pl.pallas_call(
        body, name=name, grid=(nb, h),
        in_specs=_ret_specs(s) + [pl.BlockSpec((s, dv), lambda b, hh: (b, gate_blk + hh))],
        out_specs=[ospec, ospec],
        out_shape=[jax.ShapeDtypeStruct((nb * s, h * dv), F32), jax.ShapeDtypeStruct((nb * s, h * dv), MXU_DTYPE)],
        scratch_shapes=[pltpu.VMEM((s, dk), MXU_DTYPE)],
        compiler_params=_params(("parallel", "parallel")),
    )(p, p, p, cos2, sin2, log_g, p)


def retention_bwd(name, p, cos2, sin2, log_g, do, nb, s):
    dk, dv, h = RET_QK_DIM, RET_V_DIM, RET_HEADS

    def body(q_ref, k_ref, v_ref, cos_ref, sin_ref, lg_ref, do_ref, dq_ref, dk_ref, dv_ref, kr_ref, dk_acc, dv_acc):
        lg = lg_ref[0:1, 0:1]
        kr = _rot(k_ref[...].astype(F32), cos_ref[...], sin_ref[...]) * (dk ** -0.5)
        kr_ref[...] = kr.astype(kr_ref.dtype)
        dk_acc[...] = jnp.zeros_like(dk_acc)
        dv_acc[...] = jnp.zeros_like(dv_acc)
        for qi in range(s // RET_TQ):
            n0, kmax = qi * RET_TQ, (qi + 1) * RET_TQ
            rows = pl.ds(n0, RET_TQ)
            cq, sq = cos_ref[rows, :], sin_ref[rows, :]
            qr = _rot(q_ref[rows, :].astype(F32), cq, sq).astype(MXU_DTYPE)
            dob = do_ref[rows, :]
            mask = _decay_mask(lg, n0, RET_TQ, kmax)
            sc = lax.dot_general(qr, kr_ref[0:kmax, :], NT, preferred_element_type=F32)
            pm = (sc * mask).astype(MXU_DTYPE)
            dv_acc[0:kmax, :] += lax.dot_general(pm, dob, TN, preferred_element_type=F32)
            dp = lax.dot_general(dob, v_ref[0:kmax, :], NT, preferred_element_type=F32)
            ds = (dp * mask).astype(MXU_DTYPE)
            dqr = lax.dot_general(ds, kr_ref[0:kmax, :], NN, preferred_element_type=F32)
            dq_ref[rows, :] = _rot(dqr, cq, -sq).astype(dq_ref.dtype)
            dk_acc[0:kmax, :] += lax.dot_general(ds, qr, TN, preferred_element_type=F32)
        dkr = dk_acc[...] * (dk ** -0.5)
        dk_ref[...] = _rot(dkr, cos_ref[...], -sin_ref[...]).astype(dk_ref.dtype)
        dv_ref[...] = dv_acc[...].astype(dv_ref.dtype)

    t = nb * s
    return pl.pallas_call(
        body, name=name, grid=(nb, h),
        in_specs=_ret_specs(s) + [pl.BlockSpec((s, dv), lambda b, hh: (b, hh))],
        out_specs=[pl.BlockSpec((s, dk), lambda b, hh: (b, hh)), pl.BlockSpec((s, dk), lambda b, hh: (b, hh)),
                   pl.BlockSpec((s, dv), lambda b, hh: (b, hh))],
        out_shape=[jax.ShapeDtypeStruct((t, h * dk), MXU_DTYPE), jax.ShapeDtypeStruct((t, h * dk), MXU_DTYPE),
                   jax.ShapeDtypeStruct((t, h * dv), MXU_DTYPE)],
        scratch_shapes=[pltpu.VMEM((s, dk), MXU_DTYPE), pltpu.VMEM((s, dk), F32), pltpu.VMEM((s, dv), F32)],
        compiler_params=_params(("parallel", "parallel")),
    )(p, p, p, cos2, sin2, log_g, do)


def _conv_grid(t, d, nb, ts):
    s = t // nb
    ns, nc = s // ts, d // CONV_TC
    return s, ns, nc


def _shifted(pad_ref, sh_ref, offsets):
    n = sh_ref.shape[1]
    for b in sorted({off % SUBLANES for off in offsets} - {0}):
        sh_ref[b - 1] = pad_ref[pl.ds(b, n), :]

    def read(off, r0):
        a, b = off - off % SUBLANES + r0, off % SUBLANES
        return pad_ref[pl.ds(a, SUBLANES), :] if b == 0 else sh_ref[b - 1, pl.ds(a, SUBLANES), :]

    return read


def _causal_taps(pad_ref, sh_ref, w_ref, k, emit):
    offs = [CONV_PAD - (k - 1) + j for j in range(k)]
    read = _shifted(pad_ref, sh_ref, offs)
    for r0 in range(0, pad_ref.shape[0] - CONV_PAD, CONV_ROWS):
        accs = [None] * len(CONV_TILES)
        for j in range(k):
            wj = w_ref[j]
            for q, dr in enumerate(CONV_TILES):
                term = wj * read(offs[j], r0 + dr)
                accs[q] = term if accs[q] is None else accs[q] + term
        emit(r0, jnp.concatenate(accs, axis=0))


def _tap_tiles(w):
    return jnp.broadcast_to(w[:, None, :], (w.shape[0], SUBLANES, w.shape[1]))


def _tap_spec(k):
    return pl.BlockSpec((k, SUBLANES, CONV_TC), lambda c, b, si: (0, 0, c))


def _carry_past(pad_ref, s_idx):
    ts = pad_ref.shape[0] - CONV_PAD

    @pl.when(s_idx == 0)
    def _():
        pad_ref[0:CONV_PAD, :] = jnp.zeros((CONV_PAD, pad_ref.shape[1]), F32)

    @pl.when(s_idx > 0)
    def _():
        pad_ref[0:CONV_PAD, :] = pad_ref[ts:ts + CONV_PAD, :]


def _carry_future(pad_ref, s_idx):
    ts = pad_ref.shape[0] - CONV_PAD

    @pl.when(s_idx == 0)
    def _():
        pad_ref[ts:ts + CONV_PAD, :] = jnp.zeros((CONV_PAD, pad_ref.shape[1]), F32)

    @pl.when(s_idx > 0)
    def _():
        pad_ref[ts:ts + CONV_PAD, :] = pad_ref[0:CONV_PAD, :]


def _conv_bwd_taps(pad_ref, sh_ref, w_ref, dw_acc, k, x_rows, emit, mix):
    read = _shifted(pad_ref, sh_ref, range(k))
    for r0 in range(0, pad_ref.shape[0] - CONV_PAD, CONV_ROWS):
        ops = x_rows(r0)
        x = mix(ops)
        accs = [None] * len(CONV_TILES)
        for j in range(k):
            wj, dwj = w_ref[j], None
            for q, dr in enumerate(CONV_TILES):
                sh = read(k - 1 - j, r0 + dr)
                term = wj * sh
                accs[q] = term if accs[q] is None else accs[q] + term
                prod = x[dr:dr + SUBLANES] * sh
                dwj = prod if dwj is None else dwj + prod
            dw_acc[j] += dwj
        emit(r0, ops, jnp.concatenate(accs, axis=0))


def _conv_bwd_edges(dw_acc, dw_ref, nb, ns, extra=()):
    first = jnp.logical_and(pl.program_id(1) == 0, pl.program_id(2) == 0)
    last = jnp.logical_and(pl.program_id(1) == nb - 1, pl.program_id(2) == ns - 1)

    @pl.when(first)
    def _():
        dw_acc[...] = jnp.zeros_like(dw_acc)
        for r in extra:
            r[...] = jnp.zeros_like(r)

    def finish():
        @pl.when(last)
        def _():
            dw_ref[...] = jnp.sum(dw_acc[...], axis=1)

    return finish


def short_conv_fwd(name, p, blk_b, w, nb):
    t = p.shape[0]
    d = w.shape[1]
    ts = SC_TS
    s, ns, nc = _conv_grid(t, d, nb, ts)
    cb = d // CONV_TC

    def body(b_ref, c_ref, x_ref, w_ref, y_ref, cz_ref, pad_ref, sh_ref):
        _carry_past(pad_ref, pl.program_id(2))
        pad_ref[CONV_PAD:CONV_PAD + ts, :] = c_ref[...].astype(F32) * x_ref[...].astype(F32)

        def emit(r0, cz):
            rows = pl.ds(r0, CONV_ROWS)
            cz_ref[rows, :] = cz
            y_ref[rows, :] = (b_ref[rows, :].astype(F32) * cz).astype(y_ref.dtype)

        _causal_taps(pad_ref, sh_ref, w_ref, SC_KERNEL, emit)

    def pspec(off):
        return pl.BlockSpec((ts, CONV_TC), lambda c, b, si: (b * ns + si, (blk_b + off) * cb + c))

    ospec = pl.BlockSpec((ts, CONV_TC), lambda c, b, si: (b * ns + si, c))
    return pl.pallas_call(
        body, name=name, grid=(nc, nb, ns),
        in_specs=[pspec(0), pspec(1), pspec(2), _tap_spec(SC_KERNEL)],
        out_specs=[ospec, ospec],
        out_shape=[jax.ShapeDtypeStruct((t, d), MXU_DTYPE), jax.ShapeDtypeStruct((t, d), F32)],
        scratch_shapes=_conv_scratch(ts),
        compiler_params=_params(("parallel", "arbitrary", "arbitrary")),
    )(p, p, p, _tap_tiles(w))


def short_conv_bwd(name, p, blk_b, w, cz, dy, nb):
    t = p.shape[0]
    d = w.shape[1]
    ts = SC_TS
    s, ns, nc = _conv_grid(t, d, nb, ts)
    cb = d // CONV_TC

    def body(b_ref, c_ref, x_ref, w_ref, cz_ref, dy_ref, db_ref, dc_ref, dx_ref, dw_ref, pad_ref, sh_ref, dw_acc):
        _carry_future(pad_ref, pl.program_id(2))
        dyv = dy_ref[...].astype(F32)
        db_ref[...] = (dyv * cz_ref[...]).astype(db_ref.dtype)
        pad_ref[0:ts, :] = dyv * b_ref[...].astype(F32)
        finish = _conv_bwd_edges(dw_acc, dw_ref, nb, ns)

        def x_rows(r0):
            rows = pl.ds(r0, CONV_ROWS)
            return c_ref[rows, :].astype(F32), x_ref[rows, :].astype(F32)

        def emit(r0, cx, dz):
            rows = pl.ds(r0, CONV_ROWS)
            dc_ref[rows, :] = (dz * cx[1]).astype(dc_ref.dtype)
            dx_ref[rows, :] = (dz * cx[0]).astype(dx_ref.dtype)

        _conv_bwd_taps(pad_ref, sh_ref, w_ref, dw_acc, SC_KERNEL, x_rows, emit, lambda cx: cx[0] * cx[1])
        finish()

    def row(b, si):
        return b * ns + (ns - 1 - si)

    def pspec(off):
        return pl.BlockSpec((ts, CONV_TC), lambda c, b, si: (row(b, si), (blk_b + off) * cb + c))

    ospec = pl.BlockSpec((ts, CONV_TC), lambda c, b, si: (row(b, si), c))
    wspec = pl.BlockSpec((SC_KERNEL, CONV_TC), lambda c, b, si: (0, c))
    return pl.pallas_call(
        body, name=name, grid=(nc, nb, ns),
        in_specs=[pspec(0), pspec(1), pspec(2), _tap_spec(SC_KERNEL), ospec, ospec],
        out_specs=[ospec, ospec, ospec, wspec],
        out_shape=[jax.ShapeDtypeStruct((t, d), MXU_DTYPE)] * 3 + [jax.ShapeDtypeStruct((SC_KERNEL, d), F32)],
        scratch_shapes=_conv_scratch(ts) + [pltpu.VMEM((SC_KERNEL, SUBLANES, CONV_TC), F32)],
        compiler_params=_params(("parallel", "arbitrary", "arbitrary")),
    )(p, p, p, _tap_tiles(w), cz, dy)


def conformer_conv_fwd(name, p, blk_a, w, bias, nb):
    t = p.shape[0]
    d = w.shape[1]
    ts = CONV_TS
    s, ns, nc = _conv_grid(t, d, nb, ts)
    cb = d // CONV_TC

    def body(a_ref, b_ref, w_ref, bias_ref, u_ref, pad_ref, sh_ref):
        _carry_past(pad_ref, pl.program_id(2))
        pad_ref[CONV_PAD:CONV_PAD + ts, :] = a_ref[...].astype(F32) * jax.nn.sigmoid(b_ref[...].astype(F32))

        def emit(r0, u):
            u_ref[pl.ds(r0, CONV_ROWS), :] = u + bias_ref[0:1, :]

        _causal_taps(pad_ref, sh_ref, w_ref, CF_KERNEL, emit)

    def pspec(off):
        return pl.BlockSpec((ts, CONV_TC), lambda c, b, si: (b * ns + si, (blk_a + off) * cb + c))

    return pl.pallas_call(
        body, name=name, grid=(nc, nb, ns),
        in_specs=[pspec(0), pspec(1), _tap_spec(CF_KERNEL), pl.BlockSpec((SUBLANES, CONV_TC), lambda c, b, si: (0, c))],
        out_specs=pl.BlockSpec((ts, CONV_TC), lambda c, b, si: (b * ns + si, c)),
        out_shape=jax.ShapeDtypeStruct((t, d), F32),
        scratch_shapes=_conv_scratch(ts),
        compiler_params=_params(("parallel", "arbitrary", "arbitrary")),
    )(p, p, _tap_tiles(w), jnp.broadcast_to(bias, (SUBLANES, d)))


def conformer_conv_bwd(name, p, blk_a, w, du, nb):
    t = p.shape[0]
    d = w.shape[1]
    ts = CONV_TS
    s, ns, nc = _conv_grid(t, d, nb, ts)
    cb = d // CONV_TC

    def body(a_ref, b_ref, w_ref, du_ref, da_ref, db_ref, dw_ref, dbias_ref, pad_ref, sh_ref, dw_acc):
        _carry_future(pad_ref, pl.program_id(2))
        duv = du_ref[...]
        pad_ref[0:ts, :] = duv
        finish = _conv_bwd_edges(dw_acc, dw_ref, nb, ns, extra=(dbias_ref,))
        dbias_ref[...] += jnp.sum(duv, axis=0, keepdims=True)

        def x_rows(r0):
            rows = pl.ds(r0, CONV_ROWS)
            return a_ref[rows, :].astype(F32), jax.nn.sigmoid(b_ref[rows, :].astype(F32))

        def emit(r0, asg, du0):
            rows = pl.ds(r0, CONV_ROWS)
            av, sg = asg
            da_ref[rows, :] = (du0 * sg).astype(da_ref.dtype)
            db_ref[rows, :] = (du0 * av * sg * (1.0 - sg)).astype(db_ref.dtype)

        _conv_bwd_taps(pad_ref, sh_ref, w_ref, dw_acc, CF_KERNEL, x_rows, emit, lambda asg: asg[0] * asg[1])
        finish()

    def row(b, si):
        return b * ns + (ns - 1 - si)

    def pspec(off):
        return pl.BlockSpec((ts, CONV_TC), lambda c, b, si: (row(b, si), (blk_a + off) * cb + c))

    ospec = pl.BlockSpec((ts, CONV_TC), lambda c, b, si: (row(b, si), c))
    wspec = pl.BlockSpec((CF_KERNEL, CONV_TC), lambda c, b, si: (0, c))
    bspec = pl.BlockSpec((1, CONV_TC), lambda c, b, si: (0, c))
    return pl.pallas_call(
        body, name=name, grid=(nc, nb, ns),
        in_specs=[pspec(0), pspec(1), _tap_spec(CF_KERNEL), ospec],
        out_specs=[ospec, ospec, wspec, bspec],
        out_shape=[jax.ShapeDtypeStruct((t, d), MXU_DTYPE)] * 2
        + [jax.ShapeDtypeStruct((CF_KERNEL, d), F32), jax.ShapeDtypeStruct((1, d), F32)],
        scratch_shapes=_conv_scratch(ts) + [pltpu.VMEM((CF_KERNEL, SUBLANES, CONV_TC), F32)],
        compiler_params=_params(("parallel", "arbitrary", "arbitrary")),
    )(p, p, _tap_tiles(w), du)


BLOCKS = ("ffn1", "mixer", "ffn2")
BLOCK_WEIGHTS = {"ffn1": ("ffn1_w_gu", "ffn1_w_down"), "mixer": ("w_in", "w_ret_o", "w_sc_o", "w_cf_o", "w_o"),
                 "ffn2": ("ffn2_w_gu", "ffn2_w_down")}
BIG = BLOCK_WEIGHTS["ffn1"] + BLOCK_WEIGHTS["mixer"] + BLOCK_WEIGHTS["ffn2"]
MODE = {"ffn1_w_gu": "col", "ffn1_w_down": "row", "w_in": "col", "w_ret_o": "row", "w_sc_o": "row",
        "w_cf_o": "row", "w_o": "row", "ffn2_w_gu": "col", "ffn2_w_down": "row"}
NORM_OF = {"ffn1": 0, "mixer": 2, "ffn2": 4}
BLK_GATE, BLK_SCB, BLK_CFA, BLK_MERGE = 2, 3, 6, 8


def _rope_tables(positions):
    half = RET_QK_DIM // 2
    inv_freq = ROPE_BASE ** (-jnp.arange(half, dtype=F32) / half)
    ang = positions.astype(F32)[..., None] * inv_freq
    cos, sin = jnp.cos(ang), jnp.sin(ang)
    nb, s = positions.shape
    cos2 = jnp.concatenate([cos, cos], axis=-1).reshape(nb * s, RET_QK_DIM)
    sin2 = jnp.concatenate([-sin, sin], axis=-1).reshape(nb * s, RET_QK_DIM)
    return cos2, sin2


def _log_gamma():
    lg = jnp.log(1.0 - 2.0 ** (-5.0 - jnp.arange(RET_HEADS, dtype=F32)))
    return jnp.broadcast_to(lg[:, None, None], (RET_HEADS, 1, RET_QK_DIM))


def _ffn_fwd(xs, h, w, tag, g_post, g_next):
    gu, a = ffn_up("ffn_up", h, w[tag + "_w_gu"])
    y, out, h_next = mm_post("ffn_down", a, w[tag + "_w_down"], xs, g_post, 0.5, g_next)
    return out, h_next, dict(x=xs, h=h, gu=gu, a=a, y=y, w=w)


def _ffn_bwd(dxs, dy, sv, tag, g_pre, push, prev):
    w = sv["w"]
    gu_w, down_w = w[tag + "_w_gu"], w[tag + "_w_down"]
    dgu = ffn_down_dx("ffn_down_dx", dy, down_w, sv["gu"])
    grads = {tag + "_w_down": mm_dw("ffn_down_dw", sv["a"], dy, "row", down_w.shape),
             tag + "_w_gu": mm_dw("ffn_gu_dw", sv["h"], dgu, "col", gu_w.shape)}
    return mm_dx_norms("ffn_gu_dx", dgu, gu_w, sv["x"], g_pre, dxs, prev, push(grads))


def _mixer_fwd(xs, h, w, sm, g_post, g_next, rope, nb, s, mid):
    cos2, sin2, log_g = rope
    d = xs.shape[1]
    gate_blk = (BLK_GATE * d) // RET_V_DIM
    p = mm_fwd("mx_in", h, w["w_in"], "col", MXU_DTYPE)
    if mid is not None:
        sm = dict(sm, cf_dw_b=sm["cf_dw_b"] + mid(p))
    o, ya_in = retention_fwd("ret_fwd", p, cos2, sin2, log_g, nb, s, gate_blk)
    yb_in, cz = short_conv_fwd("sc_fwd", p, BLK_SCB, sm["sc_conv_w"], nb)
    u1 = conformer_conv_fwd("cf_fwd", p, BLK_CFA, sm["cf_dw_w"], sm["cf_dw_b"], nb)
    yc_in = ln_silu_fwd("cf_ln", u1, sm["cf_ln_g"], sm["cf_ln_b"])
    ya = mm_fwd("mx_proj", ya_in, w["w_ret_o"], "row", F32)
    yb = mm_fwd("mx_proj", yb_in, w["w_sc_o"], "row", F32)
    yc = mm_fwd("mx_proj", yc_in, w["w_cf_o"], "row", F32)
    mg = merge_fwd("mx_merge", p, BLK_MERGE, ya, yb, yc)
    m, out, h_next = mm_post("mx_out", mg, w["w_o"], xs, g_post, 1.0, g_next)
    return out, h_next, dict(x=xs, h=h, p=p, o=o, ya_in=ya_in, yb_in=yb_in, cz=cz, u1=u1, yc_in=yc_in, ya=ya, yb=yb, yc=yc,
                     mg=mg, m=m, w=w)


def _mixer_bwd(dxs, dm, sv, sm, g_pre, rope, nb, s, push, prev):
    cos2, sin2, log_g = rope
    w, p = sv["w"], sv["p"]
    d = dxs.shape[1]
    gate_blk = (BLK_GATE * d) // RET_V_DIM
    grads, gsm = {}, {}

    def proj_bwd(wname, a_in, dy, out_dtype):
        grads[wname] = mm_dw("mx_proj_dw", a_in, dy, "row", w[wname].shape)
        return mm_dx("mx_proj_dx", dy, w[wname], "row", out_dtype)

    dmg = proj_bwd("w_o", sv["mg"], dm, MXU_DTYPE)
    dg0, dg1, dg2, dya, dyb, dyc = merge_bwd("mx_merge_bwd", p, BLK_MERGE, sv["ya"], sv["yb"], sv["yc"], dmg)
    dya_in = proj_bwd("w_ret_o", sv["ya_in"], dya, MXU_DTYPE)
    dyb_in = proj_bwd("w_sc_o", sv["yb_in"], dyb, MXU_DTYPE)
    dyc_in = proj_bwd("w_cf_o", sv["yc_in"], dyc, MXU_DTYPE)
    do, dgret = head_gate_bwd("ret_gate_bwd", sv["o"], p, gate_blk, dya_in)
    dq, dk, dv = retention_bwd("ret_bwd", p, cos2, sin2, log_g, do, nb, s)
    dscb, dscc, dscx, gsm["sc_conv_w"] = short_conv_bwd("sc_bwd", p, BLK_SCB, sm["sc_conv_w"], sv["cz"], dyb_in, nb)
    du1, dlg, dlb = ln_silu_bwd("cf_ln_bwd", sv["u1"], sm["cf_ln_g"], sm["cf_ln_b"], dyc_in)
    dcfa, dcfb, gsm["cf_dw_w"], dbias = conformer_conv_bwd("cf_bwd", p, BLK_CFA, sm["cf_dw_w"], du1, nb)
    gsm.update(cf_ln_g=dlg[0], cf_ln_b=dlb[0], cf_dw_b=dbias[0])
    dp = concat_cols("mx_dp", [dq, dk, dv, dgret, dscb, dscc, dscx, dcfa, dcfb, dg0, dg1, dg2])
    grads["w_in"] = mm_dw("mx_in_dw", sv["h"], dp, "col", w["w_in"].shape)
    return mm_dx_norms("mx_in_dx", dp, w["w_in"], sv["x"], g_pre, dxs, prev, push(grads)) + (gsm,)


def local_step(x, positions, target, small, fetch, push):
    nb, s, d = x.shape
    t = nb * s
    depth = small["norm_g"].shape[0]
    rope = _rope_tables(positions) + (_log_gamma(),)
    xs = x.reshape(t, d)
    token = [None]

    def gain(l, i):
        g = small["norm_g"][l, i][None, :]
        if token[0] is not None:
            g, token[0] = g + token[0], None
        return g

    def mixer_small(l):
        return dict(sc_conv_w=small["sc_conv_w"][l], cf_dw_w=small["cf_dw_w"][l], cf_dw_b=small["cf_dw_b"][l][None, :],
                    cf_ln_g=small["cf_ln_g"][l][None, :], cf_ln_b=small["cf_ln_b"][l][None, :])

    saved = {}
    order = [(l, blk) for l in range(depth) for blk in BLOCKS]
    h = None
    for at, (l, blk) in enumerate(order):
        w, token[0], mid = fetch(l, blk, xs)
        i0 = NORM_OF[blk]
        if h is None:
            h = rms_fwd("first_rms", xs, gain(l, i0))
        g_post = gain(l, i0 + 1)
        g_next = gain(order[at + 1][0], NORM_OF[order[at + 1][1]]) if at + 1 < len(order) else None
        if blk == "mixer":
            xs, h, saved[l, blk] = _mixer_fwd(xs, h, w, mixer_small(l), g_post, g_next, rope, nb, s, mid)
        else:
            xs, h, saved[l, blk] = _ffn_fwd(xs, h, w, blk, g_post, g_next)

    dxs, loss = loss_head("loss", xs, target.reshape(t, d))

    dnorm = [[None] * 6 for _ in range(depth)]
    gsmall = {n: [None] * depth for n in ("sc_conv_w", "cf_dw_w", "cf_dw_b", "cf_ln_g", "cf_ln_b")}
    def branch(group):
        l, blk = group
        sv = saved[group]
        return (sv["m"], gain(l, NORM_OF[blk] + 1), 1.0) if blk == "mixer" else (sv["y"], gain(l, NORM_OF[blk] + 1), 0.5)

    l, blk = order[-1]
    y, g_post, scale = branch(order[-1])
    dy, dnorm[l][NORM_OF[blk] + 1] = post_bwd("last_post_bwd", y, g_post, dxs, scale)
    for at in reversed(range(len(order))):
        l, blk = order[at]
        i0 = NORM_OF[blk]
        prev = branch(order[at - 1]) if at > 0 else None
        put = functools.partial(push, l, blk)
        if blk == "mixer":
            dxs, dnorm[l][i0], dy, dg_prev, gsm = _mixer_bwd(
                dxs, dy, saved[l, blk], mixer_small(l), gain(l, i0), rope, nb, s, put, prev)
            for n, v in gsm.items():
                gsmall[n][l] = v
        else:
            dxs, dnorm[l][i0], dy, dg_prev = _ffn_bwd(dxs, dy, saved[l, blk], blk, gain(l, i0), put, prev)
        if at > 0:
            dnorm[order[at - 1][0]][NORM_OF[order[at - 1][1]] + 1] = dg_prev

    gs = {n: jnp.stack(v) for n, v in gsmall.items()}
    gs["norm_g"] = jnp.stack([jnp.concatenate(r, axis=0) for r in dnorm])
    return loss, dxs.reshape(nb, s, d), gs


ANY = pl.BlockSpec(memory_space=pl.ANY)
HBM = pl.BlockSpec(memory_space=pltpu.HBM)
SEM = pl.BlockSpec(memory_space=pltpu.SEMAPHORE)
VMEM_WHOLE = pl.BlockSpec(memory_space=pltpu.VMEM)
EFFECT = pltpu.SideEffectType.DATAFLOW_SIDE_EFFECTING
TOKEN = jax.ShapeDtypeStruct((8, 128), F32)


def _other_chips(x, y):
    return [(1 - x, y), (x, 1 - y), (1 - x, 1 - y)]


def _remote(src, dst, send_sem, recv_sem, to):
    return pltpu.make_async_remote_copy(src_ref=src, dst_ref=dst, send_sem=send_sem, recv_sem=recv_sem,
                                        device_id=to, device_id_type=MESH)


def _in_hbm(v):
    return pltpu.with_memory_space_constraint(v, pltpu.HBM)


def place_quarters(ws, layer, ids, after):
    m = len(ws)

    def body(ids_ref, *refs):
        for w_ref, o_ref in zip(refs[:m], refs[m + 1:]):
            o_ref[...] = w_ref[...].astype(o_ref.dtype)

    def spec(w, where):
        return pl.BlockSpec((None, w.shape[1] // STREAM_STEPS, w.shape[2]), where)

    return pl.pallas_call(
        body, name="place_quarters",
        grid_spec=pltpu.PrefetchScalarGridSpec(
            num_scalar_prefetch=1, grid=(STREAM_STEPS,),
            in_specs=[spec(w, lambda i, ids_ref: (layer, i, 0)) for w in ws] + [ANY],
            out_specs=[spec(w, lambda i, ids_ref: (ids_ref[0], i, 0)) for w in ws]),
        out_shape=[jax.ShapeDtypeStruct((N_CHIP,) + w.shape[1:], MXU_DTYPE) for w in ws],
        compiler_params=_params(("parallel",)),
    )(ids, *ws, after)


def _gather_copies(lands, send, recv):
    x, y, c = _axes()
    me = 2 * x + y
    mine, theirs = [], []
    for a, ld in enumerate(lands):
        rh = ld.shape[1] // 2
        rows = pl.ds(c * rh, rh)
        for k, (px, py) in enumerate(_other_chips(x, y)):
            to = (px, py, c)
            mine.append(_remote(ld.at[me, rows, :], ld.at[me, rows, :], send.at[3 * a + k], recv.at[3 * a + k], to))
            got = ld.at[2 * px + py, rows, :]
            theirs.append(_remote(got, got, send.at[3 * a + k], recv.at[3 * a + k], to))
    return mine, theirs


def gather_start(name, groups, after):
    flat = [s for g in groups for s in g]
    n, ng = len(flat), len(groups)
    sizes = [len(g) for g in groups]

    def body(*refs):
        lands = refs[:n]
        sems = refs[n + 1:n + 1 + 2 * ng]
        token = refs[-1]
        at = 0
        for g, m in enumerate(sizes):
            mine, _ = _gather_copies(lands[at:at + m], sems[2 * g], sems[2 * g + 1])
            for cp in mine:
                cp.start()
            at += m
        token[...] = jnp.zeros_like(token)

    sem_shapes = []
    for m in sizes:
        sem_shapes += [pltpu.SemaphoreType.DMA((3 * m,))] * 2
    res = pl.pallas_call(
        body, name=name, in_specs=[HBM] * n + [ANY],
        out_specs=[SEM] * (2 * ng) + [HBM] * n + [VMEM_WHOLE],
        out_shape=sem_shapes + [pltpu.HBM(s.shape, s.dtype) for s in flat] + [TOKEN],
        input_output_aliases={i: 2 * ng + i for i in range(n)},
        compiler_params=pltpu.CompilerParams(has_side_effects=EFFECT),
    )(*[_in_hbm(s) for s in flat], after)
    sems, thru, token = res[:2 * ng], res[2 * ng:2 * ng + n], res[-1]
    out, at = [], 0
    for g, m in enumerate(sizes):
        out.append((sems[2 * g], sems[2 * g + 1], thru[at:at + m]))
        at += m
    return out, token


def gather_wait(lands, send, recv, after):
    m = len(lands)

    def body(*refs):
        mine, theirs = _gather_copies(refs[:m], refs[m], refs[m + 1])
        for cp in mine:
            cp.wait_send()
        for cp in theirs:
            cp.wait_recv()

    return pl.pallas_call(
        body, name="gather_wait", in_specs=[HBM] * m + [SEM, SEM, ANY], out_specs=[HBM] * m,
        out_shape=[pltpu.HBM(l.shape, l.dtype) for l in lands],
        input_output_aliases={i: i for i in range(m)},
        compiler_params=pltpu.CompilerParams(has_side_effects=EFFECT),
    )(*lands, send, recv, after)


def copy_start(name, families, after=()):
    sizes = [len(f[0]) for f in families]
    n, k, nf = sum(sizes), len(after), len(families)

    def body(*refs):
        at = 0
        for f, (_, copies, _) in enumerate(families):
            for cp in copies(refs[at:at + sizes[f]], refs[n + k + 2 * f], refs[n + k + 2 * f + 1])[0]:
                cp.start()
            at += sizes[f]
        refs[-1][...] = jnp.zeros_like(refs[-1])

    flat = [b for f in families for b in f[0]]
    sems = [pltpu.SemaphoreType.DMA((f[2],)) for f in families for _ in range(2)]
    res = pl.pallas_call(
        body, name=name, in_specs=[HBM] * n + [ANY] * k, out_specs=[SEM] * (2 * nf) + [HBM] * n + [VMEM_WHOLE],
        out_shape=sems + [pltpu.HBM(b.shape, b.dtype) for b in flat] + [TOKEN],
        input_output_aliases={i: 2 * nf + i for i in range(n)},
        compiler_params=pltpu.CompilerParams(has_side_effects=EFFECT),
    )(*[_in_hbm(b) for b in flat], *after)
    out, at = [], 2 * nf
    for f in range(nf):
        out.append((res[2 * f], res[2 * f + 1], list(res[at:at + sizes[f]])))
        at += sizes[f]
    return out, res[-1]


def copy_wait(name, bufs, send, recv, copies, after=()):
    n = len(bufs)

    def body(*refs):
        mine, theirs = copies(refs[:n], refs[n], refs[n + 1])
        for cp in mine:
            cp.wait_send()
        for cp in theirs:
            cp.wait_recv()

    return list(pl.pallas_call(
        body, name=name, in_specs=[HBM] * n + [SEM, SEM] + [ANY] * len(after), out_specs=[HBM] * n,
        out_shape=[pltpu.HBM(b.shape, b.dtype) for b in bufs], input_output_aliases={i: i for i in range(n)},
        compiler_params=pltpu.CompilerParams(has_side_effects=EFFECT),
    )(*bufs, send, recv, *after))


def _fill_copies(lands, send, recv):
    x, y, c = _axes()
    sib = (x, y, 1 - c)
    mine, theirs = [], []
    for a, ld in enumerate(lands):
        rh = ld.shape[1] // 2
        for k, (px, py) in enumerate(_other_chips(x, y)):
            got = ld.at[2 * px + py, pl.ds(c * rh, rh), :]
            mine.append(_remote(got, got, send.at[3 * a + k], recv.at[3 * a + k], sib))
            blk = ld.at[2 * px + py, pl.ds((1 - c) * rh, rh), :]
            theirs.append(_remote(blk, blk, send.at[3 * a + k], recv.at[3 * a + k], sib))
    return mine, theirs


def _presum_copies(grads, lands, send, recv):
    x, y, c = _axes()
    cps = []
    for a, (g, ld) in enumerate(zip(grads, lands)):
        rh = g.shape[1] // 2
        cps.append(_remote(g.at[:, pl.ds((1 - c) * rh, rh), :], ld, send.at[a], recv.at[a], (x, y, 1 - c)))
    return cps


def presum_wait(grads, lands, send, recv, after):
    m = len(grads)

    def body(*refs):
        for cp in _presum_copies(refs[:m], refs[m:2 * m], refs[2 * m], refs[2 * m + 1]):
            cp.wait_send()
            cp.wait_recv()

    res = pl.pallas_call(
        body, name="presum_wait", in_specs=[HBM] * (2 * m) + [SEM, SEM] + [ANY] * len(after),
        out_specs=[HBM] * (2 * m),
        out_shape=[pltpu.HBM(g.shape, g.dtype) for g in grads] + [pltpu.HBM(l.shape, l.dtype) for l in lands],
        input_output_aliases={i: i for i in range(2 * m)},
        compiler_params=pltpu.CompilerParams(has_side_effects=EFFECT),
    )(*grads, *lands, send, recv, *after)
    return res[:m], res[m:]


def add_halves(gs, lands, ids):
    m = len(gs)

    def body(ids_ref, *refs):
        for a_ref, b_ref, o_ref in zip(refs[:m], refs[m:2 * m], refs[2 * m:]):
            o_ref[...] = (a_ref[...].astype(F32) + b_ref[...].astype(F32)).astype(o_ref.dtype)

    def spec(ld, where):
        return pl.BlockSpec((None,) + ld.shape[1:], where)

    return pl.pallas_call(
        body, name="add_halves",
        grid_spec=pltpu.PrefetchScalarGridSpec(
            num_scalar_prefetch=1, grid=(N_CHIP,),
            in_specs=[spec(ld, lambda i, ids_ref: (i, ids_ref[1], 0)) for ld in lands]
            + [spec(ld, lambda i, ids_ref: (i, 0, 0)) for ld in lands],
            out_specs=[spec(ld, lambda i, ids_ref: (i, 0, 0)) for ld in lands]),
        out_shape=[jax.ShapeDtypeStruct(ld.shape, ld.dtype) for ld in lands],
        compiler_params=_params(("parallel",)),
    )(ids, *gs, *lands)


def _scatter_copies(parts, lands, send, recv):
    x, y, c = _axes()
    cps = []
    for a, (pt, ld) in enumerate(zip(parts, lands)):
        for k, (px, py) in enumerate(_other_chips(x, y)):
            cps.append(_remote(pt.at[2 * px + py], ld.at[k], send.at[3 * a + k], recv.at[3 * a + k], (px, py, c)))
    return cps


def scatter_wait(parts, lands, send, recv, after):
    m = len(parts)

    def body(*refs):
        for cp in _scatter_copies(refs[:m], refs[m:2 * m], refs[2 * m], refs[2 * m + 1]):
            cp.wait_send()
            cp.wait_recv()

    res = pl.pallas_call(
        body, name="scatter_wait", in_specs=[HBM] * (2 * m) + [SEM, SEM] + [ANY] * len(after),
        out_specs=[HBM] * (2 * m),
        out_shape=[pltpu.HBM(p.shape, p.dtype) for p in parts] + [pltpu.HBM(l.shape, l.dtype) for l in lands],
        input_output_aliases={i: i for i in range(2 * m)},
        compiler_params=pltpu.CompilerParams(has_side_effects=EFFECT),
    )(*parts, *lands, send, recv, *after)
    return res[:m], res[m:]


def sum_partials(parts, lands, ids, layer, depth, intos):
    m = len(parts)
    nt = STREAM_STEPS

    def body(ids_ref, *refs):
        for p_ref, l_ref, o_ref in zip(refs[:m], refs[m:2 * m], refs[-m:]):
            acc = p_ref[...].astype(F32)
            for k in range(N_CHIP - 1):
                acc = acc + l_ref[k].astype(F32)
            o_ref[...] = acc

    def rows(p):
        return p.shape[1] // nt

    in_specs = [pl.BlockSpec((None, rows(p), p.shape[2]), lambda i, ids_ref: (ids_ref[0], i, 0)) for p in parts]
    in_specs += [pl.BlockSpec((N_CHIP - 1, rows(p), p.shape[2]), lambda i, ids_ref: (0, i, 0)) for p in parts]
    args = [ids, *parts, *lands]
    aliases = {}
    if intos is not None:
        in_specs += [ANY] * m
        args += list(intos)
        aliases = {1 + 2 * m + a: a for a in range(m)}
    return pl.pallas_call(
        body, name="sum_partials",
        grid_spec=pltpu.PrefetchScalarGridSpec(
            num_scalar_prefetch=1, grid=(nt,), in_specs=in_specs,
            out_specs=[pl.BlockSpec((None, rows(p), p.shape[2]), lambda i, ids_ref: (layer, ids_ref[1] * nt + i, 0))
                       for p in parts]),
        out_shape=[jax.ShapeDtypeStruct((depth, 2 * p.shape[1], p.shape[2]), F32) for p in parts],
        input_output_aliases=aliases, compiler_params=_params(("parallel",)),
    )(*args)


def _final_copies(layer):
    def copies(bufs, send, recv):
        x, y, c = _axes()
        sib = (x, y, 1 - c)
        mine, theirs = [], []
        for a, buf in enumerate(bufs):
            rh = buf.shape[1] // 2
            src = buf.at[layer, pl.ds(c * rh, rh), :]
            mine.append(_remote(src, src, send.at[a], recv.at[a], sib))
            dst = buf.at[layer, pl.ds((1 - c) * rh, rh), :]
            theirs.append(_remote(dst, dst, send.at[a], recv.at[a], sib))
        return mine, theirs

    return copies


def allgather_small(pk):
    def body(in_ref, out_ref, send, recv):
        x, y, c = _axes()
        me = 2 * x + y
        chips = _other_chips(x, y)
        out_ref[pl.ds(me, 1)] = in_ref[...][None]
        cps = []
        for k, (px, py) in enumerate(chips):
            cp = _remote(in_ref, out_ref.at[me], send.at[k], recv.at[k], (px, py, c))
            cp.start()
            cps.append(cp)
        for k, (px, py) in enumerate(chips):
            got = out_ref.at[2 * px + py]
            _remote(got, got, send.at[k], recv.at[k], (px, py, c)).wait_recv()
        for cp in cps:
            cp.wait_send()

    return pl.pallas_call(
        body, name="allgather_small", in_specs=[VMEM_WHOLE], out_specs=VMEM_WHOLE,
        out_shape=jax.ShapeDtypeStruct((N_CHIP,) + pk.shape, pk.dtype),
        scratch_shapes=[pltpu.SemaphoreType.DMA((3,))] * 2,
    )(pk)


N_DEV = 8


def _small_copies(bufs, send, recv):
    g, slots = bufs
    x, y, c = _axes()
    me = 4 * x + 2 * y + c
    mine, theirs = [], []
    for mask in range(1, N_DEV):
        px = 1 - x if mask & 4 else x
        py = 1 - y if mask & 2 else y
        pc = 1 - c if mask & 1 else c
        mine.append(_remote(g, slots.at[me], send.at[mask - 1], recv.at[mask - 1], (px, py, pc)))
        got = slots.at[4 * px + 2 * py + pc]
        theirs.append(_remote(got, got, send.at[mask - 1], recv.at[mask - 1], (px, py, pc)))
    return mine, theirs


def sum_slots(g, slots, me):
    def body(me_ref, g_ref, slots_ref, o_ref):
        acc = None
        for d in range(N_DEV):
            term = jnp.where(me_ref[0] == d, g_ref[...], slots_ref[d])
            acc = term if acc is None else acc + term
        o_ref[...] = acc

    return pl.pallas_call(
        body, name="sum_slots",
        grid_spec=pltpu.PrefetchScalarGridSpec(
            num_scalar_prefetch=1, grid=(1,),
            in_specs=[pl.BlockSpec(g.shape, lambda i, me_ref: (0, 0)),
                      pl.BlockSpec(slots.shape, lambda i, me_ref: (0, 0, 0))],
            out_specs=pl.BlockSpec(g.shape, lambda i, me_ref: (0, 0))),
        out_shape=jax.ShapeDtypeStruct(g.shape, g.dtype),
        compiler_params=_params(("arbitrary",)),
    )(me, g, slots)


def adamw(w, g, m, v, layer=None, intos=None):
    shape = w.shape
    cols = shape[-1]
    rows = int(np.prod(shape[:-1]))
    span = rows if layer is None else rows // shape[0]
    tr = span
    for cand in (256, 128):
        if span % cand == 0 and cand * cols * 4 <= 2 * 1024 * 1024:
            tr = cand
            break
    first = 0 if layer is None else layer * (span // tr)
    c1 = 1.0 - ADAM_B1 ** ADAM_STEP
    c2 = 1.0 - ADAM_B2 ** ADAM_STEP

    def body(w_ref, g_ref, m_ref, v_ref, *rest):
        d_ref, nm_ref, nv_ref, g_out = rest[-4:]
        gv = g_ref[...]
        g_out[...] = gv
        nm = ADAM_B1 * m_ref[...] + (1.0 - ADAM_B1) * gv
        nv = ADAM_B2 * v_ref[...] + (1.0 - ADAM_B2) * jnp.square(gv)
        d_ref[...] = -ADAM_LR * ((nm / c1) / (jnp.sqrt(nv / c2) + ADAM_EPS) + ADAM_WD * w_ref[...])
        nm_ref[...] = nm
        nv_ref[...] = nv

    spec = pl.BlockSpec((tr, cols), lambda i: (first + i, 0))
    args = [a.reshape(rows, cols) for a in (w, g, m, v)]
    in_specs, aliases = [spec] * 4, {}
    if intos is not None:
        args += [a.reshape(rows, cols) for a in intos]
        in_specs += [ANY] * 4
        aliases = {4 + k: k for k in range(4)}
    res = pl.pallas_call(
        body, name="adamw", grid=(span // tr,), in_specs=in_specs, out_specs=[spec] * 4,
        out_shape=[jax.ShapeDtypeStruct((rows, cols), F32)] * 4, input_output_aliases=aliases,
        compiler_params=_params(("parallel",)),
    )(*args)
    return [r.reshape(shape) for r in res]


WEIGHTS = ("norm_g", "ffn1_w_gu", "ffn1_w_down", "w_in", "w_ret_o", "sc_conv_w", "w_sc_o", "cf_dw_w", "cf_dw_b",
           "cf_ln_g", "cf_ln_b", "w_cf_o", "w_o", "ffn2_w_gu", "ffn2_w_down")
SHARDED_SMALL = ("norm_g", "sc_conv_w", "cf_dw_w")
REPLICATED_SMALL = ("cf_dw_b", "cf_ln_g", "cf_ln_b")

def _pack_rows(parts):
    padded, offs, at = [], [], 0
    for p in parts:
        r = -(-p.shape[0] // SUBLANES) * SUBLANES
        padded.append(jnp.pad(p, ((0, r - p.shape[0]), (0, 0))))
        offs.append(at)
        at += r
    return jnp.concatenate(padded, axis=0), offs


def kernel(x, positions, norm_g, ffn1_w_gu, ffn1_w_down, w_in, w_ret_o, sc_conv_w, w_sc_o, cf_dw_w, cf_dw_b, cf_ln_g, cf_ln_b, w_cf_o, w_o, ffn2_w_gu, ffn2_w_down, loss_target, m_norm_g, m_ffn1_w_gu, m_ffn1_w_down, m_w_in, m_w_ret_o, m_sc_conv_w, m_w_sc_o, m_cf_dw_w, m_cf_dw_b, m_cf_ln_g, m_cf_ln_b, m_w_cf_o, m_w_o, m_ffn2_w_gu, m_ffn2_w_down, v_norm_g, v_ffn1_w_gu, v_ffn1_w_down, v_w_in, v_w_ret_o, v_sc_conv_w, v_w_sc_o, v_cf_dw_w, v_cf_dw_b, v_cf_ln_g, v_cf_ln_b, v_w_cf_o, v_w_o, v_ffn2_w_gu, v_ffn2_w_down):
    wts = dict(zip(WEIGHTS, (norm_g, ffn1_w_gu, ffn1_w_down, w_in, w_ret_o, sc_conv_w, w_sc_o, cf_dw_w, cf_dw_b,
                             cf_ln_g, cf_ln_b, w_cf_o, w_o, ffn2_w_gu, ffn2_w_down)))
    mom = dict(zip(WEIGHTS, (m_norm_g, m_ffn1_w_gu, m_ffn1_w_down, m_w_in, m_w_ret_o, m_sc_conv_w, m_w_sc_o,
                             m_cf_dw_w, m_cf_dw_b, m_cf_ln_g, m_cf_ln_b, m_w_cf_o, m_w_o, m_ffn2_w_gu, m_ffn2_w_down)))
    var = dict(zip(WEIGHTS, (v_norm_g, v_ffn1_w_gu, v_ffn1_w_down, v_w_in, v_w_ret_o, v_sc_conv_w, v_w_sc_o,
                             v_cf_dw_w, v_cf_dw_b, v_cf_ln_g, v_cf_ln_b, v_w_cf_o, v_w_o, v_ffn2_w_gu, v_ffn2_w_down)))
    depth = norm_g.shape[0]
    dq = norm_g.shape[-1]
    d = N_CHIP * dq
    chip = 2 * lax.axis_index("x") + lax.axis_index("y")
    ids = jnp.stack([chip, lax.axis_index("c")]).astype(jnp.int32)

    pk, offs = _pack_rows([wts[n].reshape(-1, dq) for n in SHARDED_SMALL])
    gk4 = allgather_small(pk)
    gk = gk4.transpose(1, 0, 2).reshape(pk.shape[0], d)
    small = {n: wts[n] for n in REPLICATED_SMALL}
    for n, o in zip(SHARDED_SMALL, offs):
        rows = wts[n].shape[0] * wts[n].shape[1]
        small[n] = gk[o:o + rows].reshape(wts[n].shape[:2] + (d,))

    order = [(l, blk) for l in range(depth) for blk in BLOCKS]
    def placed(groups, after):
        return [place_quarters([wts[n] for n in BLOCK_WEIGHTS[blk]], l, ids, after) for l, blk in groups]

    first, token = gather_start("gather_start_first", placed(order[:1], gk4), gk4)
    rest, token = gather_start("gather_start_rest", placed(order[1:], token), token)
    started = dict(zip(order, first + rest))
    small["norm_g"] = small["norm_g"] + token[0:1, 0:1]

    filling = {}

    def fill(group, after):
        send, recv, lands = started[group]
        lands = gather_wait(lands, send, recv, after)
        started_fill, tok = copy_start("fill_start", [(lands, _fill_copies, 3 * len(lands))])
        filling[group] = started_fill[0]
        return tok[0:1, 0:1]

    def fetch(l, blk, after):
        at = order.index((l, blk))
        if (l, blk) not in filling:
            fill((l, blk), token if at == 0 else after)
        send, recv, lands = filling.pop((l, blk))
        lands = copy_wait("fill_wait", lands, send, recv, _fill_copies, (after,))
        tok, mid = None, None
        if at == 1:
            mid = functools.partial(fill, order[at + 1])
        elif 1 < at < len(order) - 1:
            tok = fill(order[at + 1], lands[0])
        return dict(zip(BLOCK_WEIGHTS[blk], lands)), tok, mid

    gsum = {n: None for n in BIG}
    presums, scatters, finals = [], [], []

    def scatter_ready(after):
        group, gl, lands, send, recv = presums.pop(0)
        gl, lands = presum_wait(gl, lands, send, recv, after)
        parts = list(add_halves(gl, lands, ids))
        m = len(parts)
        lands = [lax.empty((N_CHIP - 1,) + p.shape[1:], p.dtype) for p in parts]
        family = (parts + lands, lambda refs, sd, rv: (_scatter_copies(refs[:m], refs[m:], sd, rv),) * 2, 3 * m)
        return family, lambda sd, rv, bufs: scatters.append((group, bufs[:m], bufs[m:], sd, rv))

    def final_ready(after):
        (l, blk), parts, lands, send, recv = scatters.pop(0)
        parts, lands = scatter_wait(parts, lands, send, recv, after)
        names = BLOCK_WEIGHTS[blk]
        intos = None if gsum[names[0]] is None else [gsum[n] for n in names]
        sums = list(sum_partials(parts, lands, ids, l, depth, intos))

        def note(sd, rv, bufs):
            gsum.update(zip(names, bufs))
            finals.append((names, l, sd, rv))

        return (sums, _final_copies(l), len(sums)), note

    def start_all(name, ready, after=()):
        started, tok = copy_start(name, [family for family, _ in ready], after)
        for (_, note), (sd, rv, bufs) in zip(ready, started):
            note(sd, rv, bufs)
        return tok

    def scatter_next(after):
        return start_all("scatter_start", [scatter_ready(after)])

    def sum_next(after):
        return start_all("final_start", [final_ready(after)])

    def final_next(after):
        names, l, send, recv = finals.pop(0)
        gsum.update(zip(names, copy_wait("final_wait", [gsum[n] for n in names], send, recv, _final_copies(l), after)))

    def push(l, blk, grads):
        gl = [grads[n] for n in BLOCK_WEIGHTS[blk]]
        m = len(gl)
        lands = [lax.empty((g.shape[0], g.shape[1] // 2, g.shape[2]), g.dtype) for g in gl]
        ready = [((gl + lands, lambda refs, sd, rv: (_presum_copies(refs[:m], refs[m:], sd, rv),) * 2, m),
                  lambda sd, rv, bufs: presums.append(((l, blk), bufs[:m], bufs[m:], sd, rv)))]
        if scatters:
            ready.append(final_ready((gl[0],)))
        if presums:
            ready.append(scatter_ready((gl[0],)))
        return start_all("push_start", ready)[0:1, 0:1]

    loss, grad_x, gs = local_step(x, positions, loss_target, small, fetch, push)

    names = SHARDED_SMALL + REPLICATED_SMALL
    pg, offs = _pack_rows([gs[n].reshape(-1, d) for n in names])
    small_bufs = [pg, lax.empty((N_DEV,) + pg.shape, pg.dtype)]
    ((s_send, s_recv, s_bufs),), tok = copy_start("small_start", [(small_bufs, _small_copies, N_DEV - 1)], (grad_x,))
    tok = scatter_next((grad_x, tok))

    delta, new_m, new_v, grads = {}, {}, {}, {}

    def update(n, layer=None):
        g = gsum[n] if n in BIG else grads[n]
        prev = [delta[n], new_m[n], new_v[n], grads[n]] if layer is not None and n in delta else None
        delta[n], new_m[n], new_v[n], grads[n] = adamw(wts[n], g, mom[n], var[n], layer, prev)

    while finals and finals[0][1] > 0:
        done, l = finals[0][:2]
        final_next((tok,))
        for n in done:
            update(n, l)
    upper = tuple(delta[n] for n in BIG if n in delta)
    pg, slots = copy_wait("small_wait", s_bufs, s_send, s_recv, _small_copies, upper + (tok,))
    me = (2 * chip + lax.axis_index("c")).astype(jnp.int32).reshape(1)
    tot = sum_slots(pg, slots, me)
    for n, o in zip(names, offs):
        rows = int(np.prod(gs[n].shape[:-1]))
        full = tot[o:o + rows]
        if n in SHARDED_SMALL:
            full = lax.dynamic_slice_in_dim(full, chip * dq, dq, axis=1)
        grads[n] = full.reshape(wts[n].shape)

    for n in names:
        update(n)
    after = tuple(delta[n] for n in names)
    while scatters or finals:
        if scatters:
            after = (sum_next(after),)
        done, l = finals[0][:2]
        final_next(after)
        for n in done:
            update(n, l)
        after = tuple(delta[n] for n in done)

    loss_all = lax.psum(loss[0, 0], ("x", "y", "c"))
    return (loss_all, grad_x, *[grads[n] for n in WEIGHTS], *[delta[n] for n in WEIGHTS],
            *[new_m[n] for n in WEIGHTS], *[new_v[n] for n in WEIGHTS])
```
